```python
import numpy as np
import jax
import jax.numpy as jnp
from jax import lax

D_MODEL = 1024
BATCH = 8
SEQ = 4096
DEPTH = 2

HEAD_DIM = 64
MIX_WIDTH = 512
NSA_HEADS = 8
NSA_KV_HEADS = 2
NSA_GROUP = NSA_HEADS // NSA_KV_HEADS
NSA_KV_WIDTH = NSA_KV_HEADS * HEAD_DIM
CMP_BLOCK = 32
CMP_STRIDE = 16
CMP_HIDDEN = 256
SEL_BLOCK = 64
SEL_TOPK = 16
WINDOW = 512
FORCE_BONUS = 1e4
SB_HEADS = 8
ML_HEADS = 4
ML_HEAD_DIM = 128
ML_CHUNK = 64
CONV_WIDTH = 4
Q_BLOCK = 128
N_BRANCH = 3
D_FF_DENSE = 2752
N_EXPERTS = 8
TOP_K = 2
D_FF_EXPERT = 3584
MOE_BLOCK = 256
N_DENSE = (DEPTH + 1) // 2
N_MOE = DEPTH // 2
EPS = 1e-6
NEG = -1e30
IN_SPLITS = (MIX_WIDTH,) + (NSA_KV_WIDTH,) * 6 + (NSA_HEADS * N_BRANCH,) + (MIX_WIDTH,) * 6 + (ML_HEADS, ML_HEADS, MIX_WIDTH, N_BRANCH * D_MODEL)
IN_WIDTH = sum(IN_SPLITS)

kernel_name = 'hybrid_nsa_stickbreak_mlstm_moe_block'


def rms_norm(x, g):
    x32 = x.astype(jnp.float32)
    y = x32 * lax.rsqrt(jnp.mean(x32 * x32, axis=-1, keepdims=True) + EPS)
    return (y * g.astype(jnp.float32)).astype(x.dtype)


def masked_softmax(s, valid):
    p = jax.nn.softmax(jnp.where(valid, s, NEG), axis=-1)
    return jnp.where(valid, p, 0.0)


def alibi_slopes(n_heads):
    return jnp.asarray(2.0 ** (-8.0 * np.arange(1, n_heads + 1) / n_heads), dtype=jnp.float32)


def selection_overlap(n_cmp, n_sel):
    c0 = np.arange(n_cmp)[:, None] * CMP_STRIDE
    s0 = np.arange(n_sel)[None, :] * SEL_BLOCK
    return jnp.asarray((c0 < s0 + SEL_BLOCK) & (c0 + CMP_BLOCK > s0), dtype=jnp.float32)


def swiglu(h, w_gate, w_up, w_down):
    return (jax.nn.silu(h @ w_gate) * (h @ w_up)) @ w_down


def nsa_attention(q, k_cmp, v_cmp, k_slc, v_slc, k_win, v_win, gate_logits,
                  q_norm, k_norm, cmp_pos, cmp_w1, cmp_b1, cmp_w2, cmp_b2):
    B, S, _ = q.shape
    G, R, dh = NSA_KV_HEADS, NSA_GROUP, HEAD_DIM
    n_cmp = (S - CMP_BLOCK) // CMP_STRIDE + 1
    n_sel = S // SEL_BLOCK
    top = min(SEL_TOPK, n_sel)
    nq = S // Q_BLOCK
    scale = dh ** -0.5

    def kv_heads(a):
        return a.reshape(B, S, G, dh).transpose(0, 2, 1, 3)

    blk_idx = np.arange(n_cmp)[:, None] * CMP_STRIDE + np.arange(CMP_BLOCK)[None, :]

    def compress(a, j):
        blocks = kv_heads(a)[:, :, blk_idx] + cmp_pos[j]
        blocks = blocks.reshape(B, G, n_cmp, CMP_BLOCK * dh)
        hid = jax.nn.gelu(blocks @ cmp_w1[j] + cmp_b1[j])
        return hid @ cmp_w2[j] + cmp_b2[j]

    qh = rms_norm(q.reshape(B, S, G, R, dh), q_norm).transpose(0, 2, 3, 1, 4)
    kc = rms_norm(compress(k_cmp, 0), k_norm[0])
    vc = compress(v_cmp, 1)
    ks = rms_norm(kv_heads(k_slc), k_norm[1]).reshape(B, G, n_sel, SEL_BLOCK, dh)
    vs = kv_heads(v_slc).reshape(B, G, n_sel, SEL_BLOCK, dh)
    pad = ((0, 0), (0, 0), (WINDOW, 0), (0, 0))
    kw = jnp.pad(rms_norm(kv_heads(k_win), k_norm[2]), pad)
    vw = jnp.pad(kv_heads(v_win), pad)
    cmp_end = jnp.asarray(blk_idx[:, -1], dtype=jnp.int32)
    overlap = selection_overlap(n_cmp, n_sel)
    slopes = alibi_slopes(NSA_HEADS).reshape(1, G, R, 1, 1)
    gates = jax.nn.sigmoid(gate_logits.astype(jnp.float32)).reshape(B, S, G, R, N_BRANCH)
    q_blocks = qh.reshape(B, G, R, nq, Q_BLOCK, dh).transpose(3, 0, 1, 2, 4, 5)
    g_blocks = gates.transpose(0, 2, 3, 1, 4).reshape(B, G, R, nq, Q_BLOCK, N_BRANCH).transpose(3, 0, 1, 2, 4, 5)
    q_starts = jnp.arange(nq, dtype=jnp.int32) * Q_BLOCK
    sel_ids = jnp.arange(n_sel, dtype=jnp.int32)
    blk_offs = jnp.arange(SEL_BLOCK, dtype=jnp.int32)
    win_offs = jnp.arange(WINDOW + Q_BLOCK, dtype=jnp.int32)
    gather_blocks = jax.vmap(jax.vmap(lambda a, ix: a[ix]))

    def block(args):
        qb, gb, q0 = args
        t = q0 + jnp.arange(Q_BLOCK, dtype=jnp.int32)
        dist = (t[:, None] - cmp_end[None, :]).astype(jnp.float32)
        s = jnp.einsum('bgrqd,bgnd->bgrqn', qb, kc).astype(jnp.float32) * scale - slopes * dist
        p_cmp = masked_softmax(s, dist >= 0)
        o_cmp = jnp.einsum('bgrqn,bgnd->bgrqd', p_cmp.astype(vc.dtype), vc)
        imp = jnp.einsum('bgrqn,ns->bgqs', p_cmp, overlap)
        cur = (t // SEL_BLOCK)[:, None]
        forced = (sel_ids[None, :] == 0) | (sel_ids[None, :] == cur) | (sel_ids[None, :] == cur - 1)
        causal_blk = sel_ids[None, :] * SEL_BLOCK <= t[:, None]
        imp = jnp.where(causal_blk, imp + jnp.where(forced, FORCE_BONUS, 0.0), -1.0)
        _, sel = lax.top_k(imp, top)
        k_sel = gather_blocks(ks, sel)
        v_sel = gather_blocks(vs, sel)
        dist = (t[:, None, None] - (sel[..., None] * SEL_BLOCK + blk_offs)).astype(jnp.float32)[:, :, None]
        s = jnp.einsum('bgrqd,bgqkld->bgrqkl', qb, k_sel).astype(jnp.float32) * scale - slopes[..., None] * dist
        p_sel = masked_softmax(s.reshape(B, G, R, Q_BLOCK, top * SEL_BLOCK),
                               (dist >= 0).reshape(B, G, 1, Q_BLOCK, top * SEL_BLOCK))
        o_sel = jnp.einsum('bgrqkl,bgqkld->bgrqd', p_sel.reshape(s.shape).astype(v_sel.dtype), v_sel)
        k_w = lax.dynamic_slice_in_dim(kw, q0, WINDOW + Q_BLOCK, axis=2)
        v_w = lax.dynamic_slice_in_dim(vw, q0, WINDOW + Q_BLOCK, axis=2)
        pos = q0 - WINDOW + win_offs
        dist = (t[:, None] - pos[None, :]).astype(jnp.float32)
        valid = (dist >= 0) & (dist < WINDOW) & (pos[None, :] >= 0)
        s = jnp.einsum('bgrqd,bgkd->bgrqk', qb, k_w).astype(jnp.float32) * scale - slopes * dist
        p_win = masked_softmax(s, valid)
        o_win = jnp.einsum('bgrqk,bgkd->bgrqd', p_win.astype(v_w.dtype), v_w)
        return gb[..., 0:1] * o_cmp + gb[..., 1:2] * o_sel + gb[..., 2:3] * o_win

    out = lax.map(block, (q_blocks, g_blocks, q_starts))
    return out.transpose(1, 0, 4, 2, 3, 5).reshape(B, S, NSA_HEADS * dh).astype(q.dtype)


def stick_breaking_attention(q, k, v):
    B, S, _ = q.shape
    H, dh = SB_HEADS, HEAD_DIM
    nq = S // Q_BLOCK
    scale = dh ** -0.5

    def heads(a):
        return a.reshape(B, S, H, dh).transpose(0, 2, 1, 3)

    qh, kh, vh = heads(q), heads(k), heads(v)
    q_blocks = qh.reshape(B, H, nq, Q_BLOCK, dh).transpose(2, 0, 1, 3, 4)
    q_starts = jnp.arange(nq, dtype=jnp.int32) * Q_BLOCK
    key_pos = jnp.arange(S, dtype=jnp.int32)

    def block(args):
        qb, q0 = args
        t = q0 + jnp.arange(Q_BLOCK, dtype=jnp.int32)
        z = jnp.einsum('bhqd,bhsd->bhqs', qb, kh).astype(jnp.float32) * scale
        strict = key_pos[None, :] < t[:, None]
        log_keep = jnp.where(strict, jax.nn.log_sigmoid(-z), 0.0)
        after = lax.cumsum(log_keep, axis=3, reverse=True) - log_keep
        a = jnp.where(strict, jnp.exp(jax.nn.log_sigmoid(z) + after), 0.0)
        return jnp.einsum('bhqs,bhsd->bhqd', a.astype(vh.dtype), vh)

    out = lax.map(block, (q_blocks, q_starts))
    return out.transpose(1, 0, 3, 2, 4).reshape(B, S, H * dh)


def mlstm(q_pre, k_pre, v, i_pre, f_pre, o_pre, conv_w, conv_b, gate_b):
    B, S, _ = q_pre.shape
    H, dh = ML_HEADS, ML_HEAD_DIM
    L = ML_CHUNK
    nc = S // L
    qk = jnp.concatenate([q_pre, k_pre], axis=-1)
    qk = lax.conv_general_dilated(qk, conv_w[:, None, :], window_strides=(1,),
                                  padding=((CONV_WIDTH - 1, 0),),
                                  dimension_numbers=('NWC', 'WIO', 'NWC'),
                                  feature_group_count=qk.shape[-1]) + conv_b
    q, k = jnp.split(jax.nn.silu(qk), 2, axis=-1)

    def chunks(a):
        return a.astype(jnp.float32).reshape(B, nc, L, H, dh).transpose(1, 0, 3, 2, 4)

    def gate_chunks(a):
        return a.astype(jnp.float32).reshape(B, nc, L, H).transpose(1, 0, 3, 2)

    qc = chunks(q) * dh ** -0.5
    kc = chunks(k)
    vc = chunks(v)
    ic = gate_chunks(i_pre + gate_b[0])
    fc = jax.nn.log_sigmoid(gate_chunks(f_pre + gate_b[1]))
    causal = jnp.asarray(np.tril(np.ones((L, L), dtype=bool)))

    def step(carry, xs):
        C, n, m = carry
        qq, kk, vv, ig, lf = xs
        b = jnp.cumsum(lf, axis=-1)
        dmat = jnp.where(causal, b[..., :, None] - b[..., None, :] + ig[..., None, :], NEG)
        m_inter = b + m[..., None]
        m_t = jnp.maximum(m_inter, jnp.max(dmat, axis=-1))
        w = jnp.exp(dmat - m_t[..., None]) * jnp.einsum('bhtd,bhsd->bhts', qq, kk)
        inter = jnp.exp(m_inter - m_t)
        num = inter[..., None] * jnp.einsum('bhvk,bhtk->bhtv', C, qq) + jnp.einsum('bhts,bhsv->bhtv', w, vv)
        den = inter * jnp.einsum('bhk,bhtk->bht', n, qq) + jnp.sum(w, axis=-1)
        h = num / jnp.maximum(jnp.abs(den), jnp.exp(-m_t))[..., None]
        b_last = b[..., -1]
        decay = b_last[..., None] - b + ig
        m_new = jnp.maximum(b_last + m, jnp.max(decay, axis=-1))
        ws = jnp.exp(decay - m_new[..., None])
        carry_scale = jnp.exp(b_last + m - m_new)
        C = carry_scale[..., None, None] * C + jnp.einsum('bhs,bhsv,bhsk->bhvk', ws, vv, kk)
        n = carry_scale[..., None] * n + jnp.einsum('bhs,bhsk->bhk', ws, kk)
        return (C, n, m_new), h

    init = (jnp.zeros((B, H, dh, dh), jnp.float32), jnp.zeros((B, H, dh), jnp.float32),
            jnp.zeros((B, H), jnp.float32))
    _, h = lax.scan(step, init, (qc, kc, vc, ic, fc))
    h = h.transpose(1, 0, 3, 2, 4).reshape(B, S, H * dh)
    return (jax.nn.sigmoid(o_pre.astype(jnp.float32)) * h).astype(q_pre.dtype)


def token_mixer(h, w_in, nsa_q_norm, nsa_k_norm, cmp_pos, cmp_w1, cmp_b1, cmp_w2, cmp_b2,
                ml_conv_w, ml_conv_b, ml_gate_b, w_branch, w_out):
    B, S, D = h.shape
    parts = jnp.split(h @ w_in, np.cumsum(IN_SPLITS)[:-1].tolist(), axis=-1)
    (nsa_q, nsa_kc, nsa_vc, nsa_ks, nsa_vs, nsa_kw, nsa_vw, nsa_gate,
     sb_q, sb_k, sb_v, ml_q, ml_k, ml_v, ml_i, ml_f, ml_o, merge_logits) = parts
    o_nsa = nsa_attention(nsa_q, nsa_kc, nsa_vc, nsa_ks, nsa_vs, nsa_kw, nsa_vw, nsa_gate,
                          nsa_q_norm, nsa_k_norm, cmp_pos, cmp_w1, cmp_b1, cmp_w2, cmp_b2)
    o_sb = stick_breaking_attention(sb_q, sb_k, sb_v)
    o_ml = mlstm(ml_q, ml_k, ml_v, ml_i, ml_f, ml_o, ml_conv_w, ml_conv_b, ml_gate_b)
    gates = jax.nn.sigmoid(merge_logits).reshape(B, S, N_BRANCH, D)
    merged = (gates[:, :, 0] * (o_nsa @ w_branch[0])
              + gates[:, :, 1] * (o_sb @ w_branch[1])
              + gates[:, :, 2] * (o_ml @ w_branch[2]))
    return merged @ w_out


def moe_swiglu(h, w_router, w_gate, w_up, w_down):
    B, S, D = h.shape
    T = B * S
    A = T * TOP_K
    xt = h.reshape(T, D)
    probs = jax.nn.softmax((xt @ w_router).astype(jnp.float32), axis=-1)
    top_p, top_e = lax.top_k(probs, TOP_K)
    top_p = top_p / jnp.sum(top_p, axis=-1, keepdims=True)
    e_flat = top_e.reshape(A)
    tok_flat = jnp.repeat(jnp.arange(T, dtype=jnp.int32), TOP_K)
    w_flat = top_p.reshape(A)
    order = jnp.argsort(e_flat)
    e_sorted = e_flat[order]
    counts = jnp.bincount(e_flat, length=N_EXPERTS).astype(jnp.int32)
    starts = jnp.cumsum(counts) - counts
    padded = (counts + MOE_BLOCK - 1) // MOE_BLOCK * MOE_BLOCK
    pad_ends = jnp.cumsum(padded)
    pad_starts = pad_ends - padded
    dest = pad_starts[e_sorted] + jnp.arange(A, dtype=jnp.int32) - starts[e_sorted]
    P = (-(-A // MOE_BLOCK) + N_EXPERTS) * MOE_BLOCK
    n_blk = P // MOE_BLOCK
    tok_pad = jnp.zeros((P,), jnp.int32).at[dest].set(tok_flat[order])
    w_pad = jnp.zeros((P,), jnp.float32).at[dest].set(w_flat[order])
    blk_expert = jnp.minimum(
        jnp.searchsorted(pad_ends, jnp.arange(n_blk, dtype=jnp.int32) * MOE_BLOCK, side='right'),
        N_EXPERTS - 1)
    x_pad = xt[tok_pad].reshape(n_blk, MOE_BLOCK, D)

    def expert_block(args):
        xb, e = args
        return swiglu(xb, w_gate[e], w_up[e], w_down[e])

    y = lax.map(expert_block, (x_pad, blk_expert)).reshape(P, D)
    out = jnp.zeros((T, D), h.dtype).at[tok_pad].add(y * w_pad[:, None].astype(y.dtype))
    return out.reshape(B, S, D)


def setup_inputs(seed: int = 0) -> dict:
    key = jax.random.key(seed)
    ks = jax.random.split(key, 32)
    f32 = jnp.float32

    def normal(k, shape, std):
        return jax.random.normal(k, shape, f32) * std

    def gain(k, shape):
        return 1.0 + normal(k, shape, 0.02)

    D, dh = D_MODEL, HEAD_DIM
    ml_gate_b = jnp.stack([
        normal(ks[16], (DEPTH, ML_HEADS), 0.1),
        jnp.linspace(3.0, 6.0, ML_HEADS, dtype=f32)[None, :] + normal(ks[17], (DEPTH, ML_HEADS), 0.1),
    ], axis=1)
    return {
        'x': normal(ks[0], (BATCH, SEQ, D), 1.0),
        'c': normal(ks[1], (BATCH, D), 1.0),
        'ada_w': normal(ks[2], (DEPTH, D, 6 * D), D ** -0.5),
        'ada_b': normal(ks[3], (DEPTH, 6 * D), 0.02),
        'norm_mix': gain(ks[4], (DEPTH, D)),
        'norm_ffn': gain(ks[5], (DEPTH, D)),
        'w_in': normal(ks[6], (DEPTH, D, IN_WIDTH), D ** -0.5),
        'nsa_q_norm': gain(ks[7], (DEPTH, dh)),
        'nsa_k_norm': gain(ks[8], (DEPTH, N_BRANCH, dh)),
        'cmp_pos': normal(ks[9], (DEPTH, 2, CMP_BLOCK, dh), 0.1),
        'cmp_w1': normal(ks[10], (DEPTH, 2, CMP_BLOCK * dh, CMP_HIDDEN), (CMP_BLOCK * dh) ** -0.5),
        'cmp_b1': normal(ks[11], (DEPTH, 2, CMP_HIDDEN), 0.02),
        'cmp_w2': normal(ks[12], (DEPTH, 2, CMP_HIDDEN, dh), CMP_HIDDEN ** -0.5),
        'cmp_b2': normal(ks[13], (DEPTH, 2, dh), 0.02),
        'ml_conv_w': normal(ks[14], (DEPTH, CONV_WIDTH, 2 * MIX_WIDTH), CONV_WIDTH ** -0.5),
        'ml_conv_b': normal(ks[15], (DEPTH, 2 * MIX_WIDTH), 0.02),
        'ml_gate_b': ml_gate_b,
        'w_branch': normal(ks[18], (DEPTH, N_BRANCH, MIX_WIDTH, D), MIX_WIDTH ** -0.5),
        'w_out': normal(ks[19], (DEPTH, D, D), D ** -0.5),
        'ffn_wg': normal(ks[20], (N_DENSE, D, D_FF_DENSE), D ** -0.5),
        'ffn_wu': normal(ks[21], (N_DENSE, D, D_FF_DENSE), D ** -0.5),
        'ffn_wd': normal(ks[22], (N_DENSE, D_FF_DENSE, D), D_FF_DENSE ** -0.5),
        'moe_router': normal(ks[23], (N_MOE, D, N_EXPERTS), D ** -0.5),
        'moe_wg': normal(ks[24], (N_MOE, N_EXPERTS, D, D_FF_EXPERT), D ** -0.5),
        'moe_wu': normal(ks[25], (N_MOE, N_EXPERTS, D, D_FF_EXPERT), D ** -0.5),
        'moe_wd': normal(ks[26], (N_MOE, N_EXPERTS, D_FF_EXPERT, D), D_FF_EXPERT ** -0.5),
    }


def reference(x, c, ada_w, ada_b, norm_mix, norm_ffn, w_in, nsa_q_norm, nsa_k_norm,
              cmp_pos, cmp_w1, cmp_b1, cmp_w2, cmp_b2, ml_conv_w, ml_conv_b, ml_gate_b,
              w_branch, w_out, ffn_wg, ffn_wu, ffn_wd, moe_router, moe_wg, moe_wu, moe_wd):
    cond = jax.nn.silu(c)
    for layer in range(DEPTH):
        mod = cond @ ada_w[layer] + ada_b[layer]
        shift_m, scale_m, gate_m, shift_f, scale_f, gate_f = [m[:, None, :] for m in jnp.split(mod, 6, axis=-1)]
        h = rms_norm(x, norm_mix[layer]) * (1.0 + scale_m) + shift_m
        x = x + gate_m * token_mixer(h, w_in[layer], nsa_q_norm[layer], nsa_k_norm[layer],
                                     cmp_pos[layer], cmp_w1[layer], cmp_b1[layer], cmp_w2[layer],
                                     cmp_b2[layer], ml_conv_w[layer], ml_conv_b[layer],
                                     ml_gate_b[layer], w_branch[layer], w_out[layer])
        h = rms_norm(x, norm_ffn[layer]) * (1.0 + scale_f) + shift_f
        j = layer // 2
        if layer % 2 == 0:
            f = swiglu(h, ffn_wg[j], ffn_wu[j], ffn_wd[j])
        else:
            f = moe_swiglu(h, moe_router[j], moe_wg[j], moe_wu[j], moe_wd[j])
        x = x + gate_f * f
    return x
```

```python
import functools

import numpy as np
import jax
import jax.numpy as jnp
from jax import lax
from jax.experimental import pallas as pl
from jax.experimental.pallas import tpu as pltpu

F32 = jnp.float32
BF16 = jnp.bfloat16
I32 = jnp.int32
HIGHEST = lax.Precision.HIGHEST

EPS = 1e-6
NEG = -1e30
HEAD_DIM = 64
MIX_WIDTH = 512
NSA_HEADS = 8
NSA_KV_HEADS = 2
NSA_GROUP = NSA_HEADS // NSA_KV_HEADS
CMP_BLOCK = 32
CMP_STRIDE = 16
SEL_BLOCK = 64
SEL_TOPK = 16
WINDOW = 512
FORCE_BONUS = 1e4
ML_HEADS = 4
ML_HEAD_DIM = 128
ML_CHUNK = 64
CONV_WIDTH = 4
N_BRANCH = 3
N_EXPERTS = 8
TOP_K = 2
LANES = 128

C_MERGE = 0
C_NSA_Q = 3072
C_ML_Q = 3584
C_ML_K = 4096
C_ML_V = 4608
C_ML_O = 5120
C_SB_Q = 5632
C_SB_K = 6144
C_SB_V = 6656
C_NSA_KV = 7168
N_MAIN = 7936
S_NSA_GATE = 0
S_ML_I = 24
S_ML_F = 28
N_SMALL = 128

VMEM_LIMIT = 56 * 1024 * 1024


def _params(*sem):
    return pltpu.CompilerParams(dimension_semantics=sem, vmem_limit_bytes=VMEM_LIMIT)


def _iota(shape, dim):
    return lax.broadcasted_iota(I32, shape, dim)


def _split_dot(a32, b_bf16):
    hi = a32.astype(BF16)
    lo = (a32 - hi.astype(F32)).astype(BF16)
    return (jnp.dot(hi, b_bf16, preferred_element_type=F32)
            + jnp.dot(lo, b_bf16, preferred_element_type=F32))


def _split_dot_left(a_bf16, b32):
    hi = b32.astype(BF16)
    lo = (b32 - hi.astype(F32)).astype(BF16)
    return (jnp.dot(a_bf16, hi, preferred_element_type=F32)
            + jnp.dot(a_bf16, lo, preferred_element_type=F32))


def _dot_nt(a, b):
    return lax.dot_general(a, b, (((1,), (1,)), ((), ())), preferred_element_type=F32)


def _log_sigmoid(z):
    return jnp.minimum(z, 0.0) - jnp.log1p(jnp.exp(-jnp.abs(z)))


def _adaln_kernel(c_ref, w_ref, b_ref, o_ref):
    c = c_ref[...]
    cond = c * jax.nn.sigmoid(c)
    o_ref[0] = jnp.dot(cond, w_ref[0], precision=HIGHEST, preferred_element_type=F32) + b_ref[0]


def adaln(c, ada_w, ada_b):
    depth, d, n = ada_w.shape
    b = c.shape[0]
    tn = 1536
    return pl.pallas_call(
        _adaln_kernel,
        out_shape=jax.ShapeDtypeStruct((depth, b, n), F32),
        grid=(depth, n // tn),
        in_specs=[pl.BlockSpec((b, d), lambda l, j: (0, 0)),
                  pl.BlockSpec((1, d, tn), lambda l, j: (l, 0, j)),
                  pl.BlockSpec((1, 1, tn), lambda l, j: (l, 0, j))],
        out_specs=pl.BlockSpec((1, b, tn), lambda l, j: (l, 0, j)),
        compiler_params=_params("parallel", "parallel"),
        name="adaln",
    )(c, ada_w, ada_b.reshape(depth, 1, n))


def _norm_mod(x, g, mod, shift_row, scale_row):
    ms = jnp.mean(x * x, axis=-1, keepdims=True)
    y = x * lax.rsqrt(ms + EPS) * g
    return y * (1.0 + mod[scale_row:scale_row + 1, :]) + mod[shift_row:shift_row + 1, :]


def _normmod_kernel(x_ref, g_ref, mod_ref, o_ref, *, shift_row, scale_row):
    o_ref[0] = _norm_mod(x_ref[0], g_ref[...], mod_ref[0], shift_row, scale_row).astype(o_ref.dtype)


def normmod(x, g, mod, shift_row, scale_row, ts=512):
    b, s, d = x.shape
    return pl.pallas_call(
        functools.partial(_normmod_kernel, shift_row=shift_row, scale_row=scale_row),
        out_shape=jax.ShapeDtypeStruct((b, s, d), BF16),
        grid=(b, s // ts),
        in_specs=[pl.BlockSpec((1, ts, d), lambda i, j: (i, j, 0)),
                  pl.BlockSpec((1, d), lambda i, j: (0, 0)),
                  pl.BlockSpec((1, 6, d), lambda i, j: (i, 0, 0))],
        out_specs=pl.BlockSpec((1, ts, d), lambda i, j: (i, j, 0)),
        compiler_params=_params("parallel", "parallel"),
        name="normmod",
    )(x, g.reshape(1, d), mod)


def _mm_kernel(a_ref, w_ref, o_ref):
    o_ref[...] = jnp.dot(a_ref[...], w_ref[...], preferred_element_type=F32).astype(o_ref.dtype)


def matmul(a, w, out_dtype, tm, tn):
    m, k = a.shape
    n = w.shape[1]
    return pl.pallas_call(
        _mm_kernel,
        out_shape=jax.ShapeDtypeStruct((m, n), out_dtype),
        grid=(m // tm, n // tn),
        in_specs=[pl.BlockSpec((tm, k), lambda i, j: (i, 0)),
                  pl.BlockSpec((k, tn), lambda i, j: (0, j))],
        out_specs=pl.BlockSpec((tm, tn), lambda i, j: (i, j)),
        compiler_params=_params("parallel", "parallel"),
        name="matmul",
    )(a, w)


SB_EXP_FLOOR = -104.0


def _sb_kernel(q_ref, k_ref, v_ref, o_ref, *, tq, tk):
    qi = pl.program_id(2)
    q0 = qi * tq
    lo_half = _iota((1, LANES), 1) < HEAD_DIM
    qs = q_ref[0] * jnp.asarray(HEAD_DIM ** -0.5, BF16)
    zero = jnp.zeros_like(qs)
    q_a = jnp.where(lo_half, qs, zero)
    q_b = jnp.where(lo_half, zero, qs)
    t_idx = q0 + _iota((tq, tk), 0)
    s_loc = _iota((tq, tk), 1)
    upper = (_iota((tk, tk), 0) > _iota((tk, tk), 1)).astype(BF16)

    def head(qh, kblk, vblk, strict, carry, acc):
        z = _dot_nt(qh, kblk)
        ls = _log_sigmoid(z)
        lk = jnp.where(strict, ls - z, 0.0)
        after = _split_dot(lk, upper) + carry
        a = jnp.where(strict, jnp.exp(ls + after), 0.0)
        acc = acc + jnp.dot(a.astype(BF16), vblk, preferred_element_type=F32)
        carry = carry + jnp.sum(lk, axis=1, keepdims=True)
        return carry, acc

    def cond(st):
        j, ca, cb, _, _ = st
        return (j >= 0) & (jnp.max(jnp.maximum(ca, cb)) > SB_EXP_FLOOR)

    def body(st):
        j, ca, cb, acc_a, acc_b = st
        k0 = pl.multiple_of(j * tk, tk)
        kblk = k_ref[0, pl.ds(k0, tk), :]
        vblk = v_ref[0, pl.ds(k0, tk), :]
        strict = (k0 + s_loc) < t_idx
        ca, acc_a = head(q_a, kblk, vblk, strict, ca, acc_a)
        cb, acc_b = head(q_b, kblk, vblk, strict, cb, acc_b)
        return j - 1, ca, cb, acc_a, acc_b

    c0 = jnp.zeros((tq, 1), F32)
    a0 = jnp.zeros((tq, LANES), F32)
    _, _, _, acc_a, acc_b = lax.while_loop(cond, body, ((q0 + tq) // tk - 1, c0, c0, a0, a0))
    o_ref[0] = jnp.where(lo_half, acc_a, acc_b).astype(o_ref.dtype)


def sb_attention(y3, tq=256, tk=128):
    b, s, _ = y3.shape
    n_pairs = MIX_WIDTH // LANES
    qb, kb, vb = C_SB_Q // LANES, C_SB_K // LANES, C_SB_V // LANES
    return pl.pallas_call(
        functools.partial(_sb_kernel, tq=tq, tk=tk),
        out_shape=jax.ShapeDtypeStruct((b, s, MIX_WIDTH), BF16),
        grid=(b, n_pairs, s // tq),
        in_specs=[pl.BlockSpec((1, tq, LANES), lambda i, p, j: (i, j, qb + p)),
                  pl.BlockSpec((1, s, LANES), lambda i, p, j: (i, 0, kb + p)),
                  pl.BlockSpec((1, s, LANES), lambda i, p, j: (i, 0, vb + p))],
        out_specs=pl.BlockSpec((1, tq, LANES), lambda i, p, j: (i, j, p)),
        compiler_params=_params("parallel", "parallel", "parallel"),
        name="sb_attention",
    )(y3, y3, y3)


def _mlstm_kernel(q_ref, k_ref, v_ref, og_ref, sm_ref, gr_ref, cw_ref, cb_ref, gb_ref, out_ref,
                  ct_ref, n_ref, m_ref, xbuf_ref, qk_ref, *, ts):
    L, dh, H, W = ML_CHUNK, ML_HEAD_DIM, ML_HEADS, MIX_WIDTH
    halo = 8
    sblk = pl.program_id(1)

    @pl.when(sblk == 0)
    def _():
        ct_ref[...] = jnp.zeros_like(ct_ref)
        n_ref[...] = jnp.zeros_like(n_ref)
        m_ref[...] = jnp.zeros_like(m_ref)
        xbuf_ref[0:halo, :] = jnp.zeros((halo, 2 * W), F32)

    @pl.when(sblk > 0)
    def _():
        xbuf_ref[0:halo, :] = xbuf_ref[ts:ts + halo, :]

    xbuf_ref[halo:halo + ts, 0:W] = q_ref[0].astype(F32)
    xbuf_ref[halo:halo + ts, W:2 * W] = k_ref[0].astype(F32)
    conv = cb_ref[...] + jnp.zeros((ts, 2 * W), F32)
    for j in range(CONV_WIDTH):
        off = halo - (CONV_WIDTH - 1) + j
        conv = conv + cw_ref[j:j + 1, :] * xbuf_ref[off:off + ts, :]
    act = conv * jax.nn.sigmoid(conv)
    qk_ref[:, 0:W] = (act[:, 0:W] * (dh ** -0.5)).astype(BF16)
    qk_ref[:, W:2 * W] = act[:, W:2 * W].astype(BF16)

    it0, it1 = _iota((L, L), 0), _iota((L, L), 1)
    causal = it0 >= it1
    tri_lo = causal.astype(BF16)
    tri_up = (it0 <= it1).astype(BF16)

    def chunk(c, carry):
        r0 = pl.multiple_of(c * L, L)
        sm = sm_ref[0, pl.ds(r0, L), :]
        gr = gr_ref[0, c]
        for h in range(H):
            ig_col = sm[:, S_ML_I + h:S_ML_I + h + 1] + gb_ref[0, h]
            lf_col = _log_sigmoid(sm[:, S_ML_F + h:S_ML_F + h + 1] + gb_ref[1, h])
            ig_row = gr[h:h + 1, :] + gb_ref[0, h]
            lf_row = _log_sigmoid(gr[H + h:H + h + 1, :] + gb_ref[1, h])
            b_t = _split_dot_left(tri_lo, jnp.broadcast_to(lf_col, (L, L)))
            b_s = _split_dot(jnp.broadcast_to(lf_row, (L, L)), tri_up)
            dmat = jnp.where(causal, b_t - b_s + ig_row, NEG)
            b_col = b_t[:, 0:1]
            m_prev = m_ref[h][:, 0:1]
            m_inter = b_col + m_prev
            m_t = jnp.maximum(m_inter, jnp.max(dmat, axis=1, keepdims=True))
            cols = slice(h * dh, (h + 1) * dh)
            qq = qk_ref[pl.ds(r0, L), cols]
            kk = qk_ref[pl.ds(r0, L), W + h * dh:W + (h + 1) * dh]
            vv = v_ref[0, pl.ds(r0, L), cols]
            w = jnp.exp(dmat - m_t) * _dot_nt(qq, kk)
            inter = jnp.exp(m_inter - m_t)
            ct = ct_ref[h]
            nvec = n_ref[h]
            num = (inter * jnp.dot(qq, ct.astype(BF16), preferred_element_type=F32)
                   + jnp.dot(w.astype(BF16), vv, preferred_element_type=F32))
            den = (inter * jnp.sum(qq.astype(F32) * nvec, axis=1, keepdims=True)
                   + jnp.sum(w, axis=1, keepdims=True))
            hval = num / jnp.maximum(jnp.abs(den), jnp.exp(-m_t))
            b_last = b_t[L - 1:L, 0:1]
            decay = b_last - b_col + ig_col
            m_new = jnp.maximum(b_last + m_prev, jnp.max(decay, axis=0, keepdims=True))
            ws = jnp.exp(decay - m_new)
            cscale = jnp.exp(b_last + m_prev - m_new)
            wv = (ws * vv.astype(F32)).astype(BF16)
            kt = kk.astype(F32).T.astype(BF16)
            ct_ref[h] = cscale * ct + jnp.dot(kt, wv, preferred_element_type=F32)
            n_ref[h] = cscale * nvec + jnp.sum(ws * kk.astype(F32), axis=0, keepdims=True)
            m_ref[h] = jnp.broadcast_to(m_new, (1, LANES))
            gate = jax.nn.sigmoid(og_ref[0, pl.ds(r0, L), cols].astype(F32))
            out_ref[0, pl.ds(r0, L), cols] = (gate * hval).astype(out_ref.dtype)
        return carry

    lax.fori_loop(0, ts // L, chunk, 0)


def mlstm(y3, small3, conv_w, conv_b, gate_b, ts=512):
    b, s, _ = y3.shape
    W, H, L = MIX_WIDTH, ML_HEADS, ML_CHUNK
    gr = small3[:, :, S_ML_I:S_ML_I + 2 * H].reshape(b, s // L, L, 2 * H).transpose(0, 1, 3, 2)
    cq, ck, cv, co = C_ML_Q // W, C_ML_K // W, C_ML_V // W, C_ML_O // W
    return pl.pallas_call(
        functools.partial(_mlstm_kernel, ts=ts),
        out_shape=jax.ShapeDtypeStruct((b, s, W), BF16),
        grid=(b, s // ts),
        in_specs=[pl.BlockSpec((1, ts, W), lambda i, j: (i, j, cq)),
                  pl.BlockSpec((1, ts, W), lambda i, j: (i, j, ck)),
                  pl.BlockSpec((1, ts, W), lambda i, j: (i, j, cv)),
                  pl.BlockSpec((1, ts, W), lambda i, j: (i, j, co)),
                  pl.BlockSpec((1, ts, N_SMALL), lambda i, j: (i, j, 0)),
                  pl.BlockSpec((1, ts // L, 2 * H, L), lambda i, j: (i, j, 0, 0)),
                  pl.BlockSpec((CONV_WIDTH, 2 * W), lambda i, j: (0, 0)),
                  pl.BlockSpec((1, 2 * W), lambda i, j: (0, 0)),
                  pl.BlockSpec(memory_space=pltpu.SMEM)],
        out_specs=pl.BlockSpec((1, ts, W), lambda i, j: (i, j, 0)),
        scratch_shapes=[pltpu.VMEM((H, ML_HEAD_DIM, ML_HEAD_DIM), F32),
                        pltpu.VMEM((H, 1, ML_HEAD_DIM), F32),
                        pltpu.VMEM((H, 1, LANES), F32),
                        pltpu.VMEM((ts + 8, 2 * W), F32),
                        pltpu.VMEM((ts, 2 * W), BF16)],
        compiler_params=_params("parallel", "arbitrary"),
        name="mlstm",
    )(y3, y3, y3, y3, small3, gr, conv_w, conv_b.reshape(1, 2 * W), gate_b)


def _headnorm_kernel(x_ref, g_ref, o_ref):
    x = x_ref[0].astype(F32)
    same_head = (_iota((LANES, LANES), 0) // HEAD_DIM == _iota((LANES, LANES), 1) // HEAD_DIM)
    ss = _split_dot(x * x, same_head.astype(BF16))
    o_ref[0] = (x * lax.rsqrt(ss * (1.0 / HEAD_DIM) + EPS) * g_ref[0]).astype(o_ref.dtype)


def nsa_headnorm(y3, q_norm, k_norm, ts=1024):
    b, s, _ = y3.shape
    qb = C_NSA_Q // LANES
    ksb = C_NSA_KV // LANES + 2
    kwb = C_NSA_KV // LANES + 4
    gains = jnp.stack([jnp.tile(q_norm, 2)] * 4 + [jnp.tile(k_norm[1], 2), jnp.tile(k_norm[2], 2)])

    def col(j):
        return jnp.where(j < 4, qb + j, jnp.where(j == 4, ksb, kwb))

    return pl.pallas_call(
        _headnorm_kernel,
        out_shape=jax.ShapeDtypeStruct((b, s, 6 * LANES), BF16),
        grid=(b, s // ts, 6),
        in_specs=[pl.BlockSpec((1, ts, LANES), lambda i, t, j: (i, t, col(j))),
                  pl.BlockSpec((1, 1, LANES), lambda i, t, j: (j, 0, 0))],
        out_specs=pl.BlockSpec((1, ts, LANES), lambda i, t, j: (i, t, j)),
        compiler_params=_params("parallel", "parallel", "parallel"),
        name="nsa_headnorm",
    )(y3, gains.reshape(6, 1, LANES))


def _gelu_tanh(x):
    return 0.5 * x * (1.0 + jnp.tanh(0.7978845608028654 * (x + 0.044715 * (x * x * x))))


def _compress_kernel(ra_ref, rb_ref, pos_ref, w1_ref, b1_ref, w2_ref, b2_ref, kn_ref, kc_ref, vc_ref):
    half = (CMP_BLOCK // 2) * HEAD_DIM
    for j, o_ref in enumerate((kc_ref, vc_ref)):
        xa = (ra_ref[j, 0, 0].astype(F32) + pos_ref[j, :, 0:half]).astype(BF16)
        xb = (rb_ref[j, 0, 0].astype(F32) + pos_ref[j, :, half:2 * half]).astype(BF16)
        hid = (jnp.dot(xa, w1_ref[j, 0:half, :], preferred_element_type=F32)
               + jnp.dot(xb, w1_ref[j, half:2 * half, :], preferred_element_type=F32) + b1_ref[j])
        out = jnp.dot(_gelu_tanh(hid).astype(BF16), w2_ref[j], preferred_element_type=F32) + b2_ref[j]
        if j == 0:
            out = out * lax.rsqrt(jnp.mean(out * out, axis=-1, keepdims=True) + EPS) * kn_ref[...]
        o_ref[0, 0] = out


def nsa_compress(y3, cmp_pos, cmp_w1, cmp_b1, cmp_w2, cmp_b2, k_norm0):
    b, s, _ = y3.shape
    G, dh = NSA_KV_HEADS, HEAD_DIM
    nr = s // CMP_STRIDE
    wide = CMP_STRIDE * dh
    kv = y3[:, :, C_NSA_KV:C_NSA_KV + 2 * G * dh].reshape(b, s, 2, G, dh)
    ra = kv.transpose(2, 0, 3, 1, 4).reshape(2, b, G, nr, wide)
    rb = jnp.concatenate([ra[:, :, :, 1:], jnp.zeros((2, b, G, 1, wide), ra.dtype)], axis=3)
    hidden = cmp_w1.shape[-1]
    out = jax.ShapeDtypeStruct((b, G, nr, dh), F32)
    blk = pl.BlockSpec((2, 1, 1, nr, wide), lambda i, g: (0, i, g, 0, 0))
    oblk = pl.BlockSpec((1, 1, nr, dh), lambda i, g: (i, g, 0, 0))

    def full(shape):
        return pl.BlockSpec(shape, lambda i, g: (0,) * len(shape))

    return pl.pallas_call(
        _compress_kernel,
        out_shape=(out, out),
        grid=(b, G),
        in_specs=[blk, blk, full((2, 1, 2 * wide)), full((2, 2 * wide, hidden)), full((2, 1, hidden)),
                  full((2, hidden, dh)), full((2, 1, dh)), full((1, dh))],
        out_specs=(oblk, oblk),
        compiler_params=_params("parallel", "parallel"),
        name="nsa_compress",
    )(ra, rb, cmp_pos.reshape(2, 1, 2 * wide), cmp_w1.astype(BF16), cmp_b1.reshape(2, 1, hidden),
      cmp_w2.astype(BF16), cmp_b2.reshape(2, 1, dh), k_norm0.reshape(1, dh))


def _nsa_kernel(q_ref, gl_ref, kc_ref, vc_ref, ks_ref, vs_ref, kw_ref, vw_ref, o_ref, *, n_sel, top):
    QB, R, dh = 128, NSA_GROUP, HEAD_DIM
    rows = R * QB
    g = pl.program_id(1)
    qi = pl.program_id(2)
    q0 = qi * QB
    q = q_ref[0, 0, 0] * jnp.asarray(dh ** -0.5, BF16)
    row = _iota((rows, 1), 0)
    t_f = (q0 + (row & (QB - 1))).astype(F32)
    head = g * R + (row >> 7)
    slope = jnp.exp2(-(head + 1).astype(F32))

    n_cmp = kc_ref.shape[2]
    kc = kc_ref[0, 0]
    kc_hi = kc.astype(BF16)
    kc_lo = (kc - kc_hi.astype(F32)).astype(BF16)
    cmp_end = (_iota((1, n_cmp), 1) * CMP_STRIDE + (CMP_BLOCK - 1)).astype(F32)
    dist = t_f - cmp_end
    valid = dist >= 0.0
    s = jnp.where(valid, _dot_nt(q, kc_hi) + _dot_nt(q, kc_lo) - slope * dist, NEG)
    e = jnp.exp(s - jnp.max(s, axis=1, keepdims=True))
    p = jnp.where(valid, e / jnp.sum(e, axis=1, keepdims=True), 0.0)
    o_cmp = jnp.dot(p.astype(BF16), vc_ref[0, 0].astype(BF16), preferred_element_type=F32)

    p_grp = p[0:QB] + p[QB:2 * QB] + p[2 * QB:3 * QB] + p[3 * QB:4 * QB]
    c0 = _iota((n_cmp, n_sel), 0) * CMP_STRIDE
    s0 = _iota((n_cmp, n_sel), 1) * SEL_BLOCK
    overlap = ((c0 < s0 + SEL_BLOCK) & (c0 + CMP_BLOCK > s0)).astype(BF16)
    imp = _split_dot(p_grp, overlap)
    tq = q0 + _iota((QB, 1), 0)
    j_idx = _iota((1, n_sel), 1)
    cur = tq >> 6
    forced = (j_idx == 0) | (j_idx == cur) | (j_idx == cur - 1)
    imp = jnp.where(j_idx * SEL_BLOCK <= tq, imp + jnp.where(forced, FORCE_BONUS, 0.0), -1.0)
    sel = jnp.zeros((QB, n_sel), F32)
    for _ in range(top):
        mx = jnp.max(imp, axis=1, keepdims=True)
        first = jnp.min(jnp.where(imp == mx, j_idx, n_sel), axis=1, keepdims=True)
        pick = j_idx == first
        sel = jnp.where(pick, 1.0, sel)
        imp = jnp.where(pick, -3e38, imp)
    sel_bf = sel.astype(BF16)

    def flash(k_ref, v_ref, lo, hi, mask_fn):
        def body(kb, st):
            m_run, l_run, acc = st
            k0 = pl.multiple_of(kb * QB, QB)
            kblk = k_ref[0, 0, pl.ds(k0, QB), :]
            vblk = v_ref[0, 0, pl.ds(k0, QB), :]
            dist = t_f - (k0 + _iota((1, QB), 1)).astype(F32)
            ok = mask_fn(kb, dist)
            sc = jnp.where(ok, _dot_nt(q, kblk) - slope * dist, NEG)
            m_new = jnp.maximum(m_run, jnp.max(sc, axis=1, keepdims=True))
            alpha = jnp.exp(m_run - m_new)
            pr = jnp.where(ok, jnp.exp(sc - m_new), 0.0)
            l_run = alpha * l_run + jnp.sum(pr, axis=1, keepdims=True)
            acc = alpha * acc + jnp.dot(pr.astype(BF16), vblk, preferred_element_type=F32)
            return m_new, l_run, acc

        init = (jnp.full((rows, 1), NEG, F32), jnp.zeros((rows, 1), F32), jnp.zeros((rows, dh), F32))
        _, l_run, acc = lax.fori_loop(lo, hi, body, init)
        return acc / l_run

    def sel_mask(kb, dist):
        expand = (_iota((n_sel, QB), 0) == 2 * kb + (_iota((n_sel, QB), 1) >> 6)).astype(BF16)
        m = jnp.dot(sel_bf, expand, preferred_element_type=F32)
        m = jnp.concatenate([m] * R, axis=0)
        return (m > 0.5) & (dist >= 0.0)

    def win_mask(kb, dist):
        return (dist >= 0.0) & (dist < float(WINDOW))

    o_sel = flash(ks_ref, vs_ref, 0, qi + 1, sel_mask)
    o_win = flash(kw_ref, vw_ref, jnp.maximum(qi - WINDOW // QB, 0), qi + 1, win_mask)

    gate = jax.nn.sigmoid(gl_ref[0, 0, 0])
    o_ref[0, 0, 0] = (gate[:, 0:1] * o_cmp + gate[:, 1:2] * o_sel + gate[:, 2:3] * o_win).astype(o_ref.dtype)


def nsa_attention(y3, small3, normed, kc, vc):
    b, s, _ = y3.shape
    G, R, dh, QB = NSA_KV_HEADS, NSA_GROUP, HEAD_DIM, 128
    nq = s // QB
    n_sel = s // SEL_BLOCK
    top = min(SEL_TOPK, n_sel)

    def stack_heads(a, width):
        return (a.reshape(b, nq, QB, G, R, width).transpose(0, 3, 1, 4, 2, 5)
                .reshape(b, G, nq, R * QB, width))

    def kv_heads(a):
        return a.reshape(b, s, G, dh).transpose(0, 2, 1, 3)

    q = stack_heads(normed[:, :, 0:MIX_WIDTH], dh)
    gl = stack_heads(small3[:, :, S_NSA_GATE:S_NSA_GATE + NSA_HEADS * N_BRANCH], N_BRANCH)
    ks = kv_heads(normed[:, :, 4 * LANES:5 * LANES])
    kw = kv_heads(normed[:, :, 5 * LANES:6 * LANES])
    vs = kv_heads(y3[:, :, C_NSA_KV + 3 * LANES:C_NSA_KV + 4 * LANES])
    vw = kv_heads(y3[:, :, C_NSA_KV + 5 * LANES:C_NSA_KV + 6 * LANES])
    n_cmp = kc.shape[2]
    qspec = pl.BlockSpec((1, 1, 1, R * QB, dh), lambda i, g, j: (i, g, j, 0, 0))
    gspec = pl.BlockSpec((1, 1, 1, R * QB, N_BRANCH), lambda i, g, j: (i, g, j, 0, 0))
    cspec = pl.BlockSpec((1, 1, n_cmp, dh), lambda i, g, j: (i, g, 0, 0))
    kvspec = pl.BlockSpec((1, 1, s, dh), lambda i, g, j: (i, g, 0, 0))
    out = pl.pallas_call(
        functools.partial(_nsa_kernel, n_sel=n_sel, top=top),
        out_shape=jax.ShapeDtypeStruct((b, G, nq, R * QB, dh), BF16),
        grid=(b, G, nq),
        in_specs=[qspec, gspec, cspec, cspec, kvspec, kvspec, kvspec, kvspec],
        out_specs=qspec,
        compiler_params=_params("parallel", "parallel", "parallel"),
        name="nsa_attention",
    )(q, gl, kc, vc, ks, vs, kw, vw)
    return (out.reshape(b, G, nq, R, QB, dh).transpose(0, 2, 4, 1, 3, 5).reshape(b, s, MIX_WIDTH))


def _merge_kernel(on_ref, os_ref, om_ref, g0_ref, g1_ref, g2_ref, wb_ref, wo_ref, x_ref, mod_ref, o_ref):
    merged = None
    for i, (o_r, g_r) in enumerate(((on_ref, g0_ref), (os_ref, g1_ref), (om_ref, g2_ref))):
        br = jnp.dot(o_r[0], wb_ref[i], preferred_element_type=F32)
        term = jax.nn.sigmoid(g_r[0].astype(F32)) * br
        merged = term if merged is None else merged + term
    out = jnp.dot(merged.astype(BF16), wo_ref[...], preferred_element_type=F32)
    o_ref[0] = x_ref[0] + mod_ref[0, 2:3, :] * out


def merge_project(o_nsa, o_sb, o_ml, y3, w_branch, w_out, x, mod, tm=512):
    b, s, d = x.shape
    W = MIX_WIDTH
    ospec = pl.BlockSpec((1, tm, W), lambda i, j: (i, j, 0))
    xspec = pl.BlockSpec((1, tm, d), lambda i, j: (i, j, 0))
    gspecs = [pl.BlockSpec((1, tm, d), functools.partial(lambda i, j, c: (i, j, c), c=C_MERGE // d + c))
              for c in range(N_BRANCH)]
    return pl.pallas_call(
        _merge_kernel,
        out_shape=jax.ShapeDtypeStruct((b, s, d), F32),
        grid=(b, s // tm),
        in_specs=[ospec, ospec, ospec] + gspecs + [
            pl.BlockSpec((N_BRANCH, W, d), lambda i, j: (0, 0, 0)),
            pl.BlockSpec((d, d), lambda i, j: (0, 0)),
            xspec,
            pl.BlockSpec((1, 6, d), lambda i, j: (i, 0, 0))],
        out_specs=xspec,
        compiler_params=_params("parallel", "parallel"),
        name="merge_project",
    )(o_nsa, o_sb, o_ml, y3, y3, y3, w_branch.astype(BF16), w_out.astype(BF16), x, mod)


def _ffn_kernel(x_ref, g_ref, mod_ref, wg_ref, wu_ref, wd_ref, o_ref, h_ref, acc_ref):
    f = pl.program_id(2)

    @pl.when(f == 0)
    def _():
        h_ref[...] = _norm_mod(x_ref[0], g_ref[...], mod_ref[0], 3, 4).astype(BF16)
        acc_ref[...] = jnp.zeros_like(acc_ref)

    h = h_ref[...]
    a = jnp.dot(h, wg_ref[...], preferred_element_type=F32)
    u = jnp.dot(h, wu_ref[...], preferred_element_type=F32)
    act = (a * jax.nn.sigmoid(a) * u).astype(BF16)
    acc_ref[...] += jnp.dot(act, wd_ref[...], preferred_element_type=F32)

    @pl.when(f == pl.num_programs(2) - 1)
    def _():
        o_ref[0] = x_ref[0] + mod_ref[0, 5:6, :] * acc_ref[...]


def dense_ffn(x, g, mod, wg, wu, wd, tm=512, n_ftiles=2):
    b, s, d = x.shape
    ff = wg.shape[1]
    tf = -(-ff // (n_ftiles * LANES)) * LANES
    pad = n_ftiles * tf - ff
    wg = jnp.pad(wg, ((0, 0), (0, pad))).astype(BF16)
    wu = jnp.pad(wu, ((0, 0), (0, pad))).astype(BF16)
    wd = jnp.pad(wd, ((0, pad), (0, 0))).astype(BF16)
    xspec = pl.BlockSpec((1, tm, d), lambda i, j, f: (i, j, 0))
    return pl.pallas_call(
        _ffn_kernel,
        out_shape=jax.ShapeDtypeStruct((b, s, d), F32),
        grid=(b, s // tm, n_ftiles),
        in_specs=[xspec,
                  pl.BlockSpec((1, d), lambda i, j, f: (0, 0)),
                  pl.BlockSpec((1, 6, d), lambda i, j, f: (i, 0, 0)),
                  pl.BlockSpec((d, tf), lambda i, j, f: (0, f)),
                  pl.BlockSpec((d, tf), lambda i, j, f: (0, f)),
                  pl.BlockSpec((tf, d), lambda i, j, f: (f, 0))],
        out_specs=xspec,
        scratch_shapes=[pltpu.VMEM((tm, d), BF16), pltpu.VMEM((tm, d), F32)],
        compiler_params=_params("parallel", "parallel", "arbitrary"),
        name="dense_ffn",
    )(x, g.reshape(1, d), mod, wg, wu, wd)


def _router_kernel(x_ref, g_ref, mod_ref, wr_ref, h_ref, e_ref, p_ref):
    h = _norm_mod(x_ref[0], g_ref[...], mod_ref[0], 3, 4)
    h_ref[...] = h
    lane = _iota((1, LANES), 1)
    real = lane < N_EXPERTS
    logits = jnp.where(real, jnp.dot(h, wr_ref[...], precision=HIGHEST, preferred_element_type=F32), NEG)
    e = jnp.exp(logits - jnp.max(logits, axis=1, keepdims=True))
    p = jnp.where(real, e / jnp.sum(e, axis=1, keepdims=True), -1.0)
    p1 = jnp.max(p, axis=1, keepdims=True)
    i1 = jnp.min(jnp.where(p == p1, lane, LANES), axis=1, keepdims=True)
    rest = jnp.where(lane == i1, -1.0, p)
    p2 = jnp.max(rest, axis=1, keepdims=True)
    i2 = jnp.min(jnp.where(rest == p2, lane, LANES), axis=1, keepdims=True)
    tot = p1 + p2
    e_ref[...] = jnp.where(lane == 0, i1, jnp.where(lane == 1, i2, 0))[:, 0:N_EXPERTS]
    p_ref[...] = jnp.where(lane == 0, p1 / tot, jnp.where(lane == 1, p2 / tot, 0.0))[:, 0:N_EXPERTS]


def moe_router(x, g, mod, w_router, tm=512):
    b, s, d = x.shape
    t = b * s
    spb = s // tm
    wr = jnp.pad(w_router, ((0, 0), (0, LANES - N_EXPERTS)))
    return pl.pallas_call(
        _router_kernel,
        out_shape=(jax.ShapeDtypeStruct((t, d), F32),
                   jax.ShapeDtypeStruct((t, N_EXPERTS), I32),
                   jax.ShapeDtypeStruct((t, N_EXPERTS), F32)),
        grid=(b, spb),
        in_specs=[pl.BlockSpec((1, tm, d), lambda i, j: (i, j, 0)),
                  pl.BlockSpec((1, d), lambda i, j: (0, 0)),
                  pl.BlockSpec((1, 6, d), lambda i, j: (i, 0, 0)),
                  pl.BlockSpec((d, LANES), lambda i, j: (0, 0))],
        out_specs=(pl.BlockSpec((tm, d), lambda i, j: (i * spb + j, 0)),
                   pl.BlockSpec((tm, N_EXPERTS), lambda i, j: (i * spb + j, 0)),
                   pl.BlockSpec((tm, N_EXPERTS), lambda i, j: (i * spb + j, 0))),
        compiler_params=_params("parallel", "parallel"),
        name="moe_router",
    )(x, g.reshape(1, d), mod, wr)


def _dispatch_kernel(dest_ref, h_hbm, zero_hbm, xpad_hbm, sem, *, chunk):
    del zero_hbm
    base = pl.program_id(0) * chunk

    def row_copy(a):
        return pltpu.make_async_copy(h_hbm.at[a // TOP_K], xpad_hbm.at[dest_ref[a]], sem)

    def issue(r, c):
        row_copy(base + r).start()
        return c

    def drain(r, c):
        row_copy(base + r).wait()
        return c

    lax.fori_loop(0, chunk, issue, 0)
    lax.fori_loop(0, chunk, drain, 0)


def moe_dispatch(h, dest, n_rows, chunk=512):
    t, d = h.shape
    a = dest.shape[0]
    return pl.pallas_call(
        functools.partial(_dispatch_kernel, chunk=chunk),
        out_shape=jax.ShapeDtypeStruct((n_rows, d), h.dtype),
        grid_spec=pltpu.PrefetchScalarGridSpec(
            num_scalar_prefetch=1,
            grid=(a // chunk,),
            in_specs=[pl.BlockSpec(memory_space=pl.ANY), pl.BlockSpec(memory_space=pl.ANY)],
            out_specs=pl.BlockSpec(memory_space=pl.ANY),
            scratch_shapes=[pltpu.SemaphoreType.DMA(())]),
        input_output_aliases={2: 0},
        compiler_params=pltpu.CompilerParams(dimension_semantics=("arbitrary",), has_side_effects=True),
        name="moe_dispatch",
    )(dest, h, jnp.zeros((n_rows, d), h.dtype))


def _expert_kernel(be_ref, nu_ref, x_ref, wg_ref, wu_ref, wd_ref, o_ref, xb_ref, acc_ref):
    i = pl.program_id(0)
    f = pl.program_id(1)
    used = i < nu_ref[0]

    @pl.when(f == 0)
    def _():
        xb_ref[...] = x_ref[...].astype(BF16)
        acc_ref[...] = jnp.zeros_like(acc_ref)

    @pl.when(used)
    def _():
        xb = xb_ref[...]
        a = jnp.dot(xb, wg_ref[0], preferred_element_type=F32)
        u = jnp.dot(xb, wu_ref[0], preferred_element_type=F32)
        act = (a * jax.nn.sigmoid(a) * u).astype(BF16)
        acc_ref[...] += jnp.dot(act, wd_ref[0], preferred_element_type=F32)

    @pl.when(f == pl.num_programs(1) - 1)
    def _():
        o_ref[...] = acc_ref[...]


def moe_experts(x_pad, blk_expert, n_used, wg, wu, wd, tb, tf=896):
    p, d = x_pad.shape
    ff = wg.shape[2]
    return pl.pallas_call(
        _expert_kernel,
        out_shape=jax.ShapeDtypeStruct((p, d), F32),
        grid_spec=pltpu.PrefetchScalarGridSpec(
            num_scalar_prefetch=2,
            grid=(p // tb, ff // tf),
            in_specs=[pl.BlockSpec((tb, d), lambda i, f, be, nu: (i, 0)),
                      pl.BlockSpec((1, d, tf), lambda i, f, be, nu: (be[i], 0, f)),
                      pl.BlockSpec((1, d, tf), lambda i, f, be, nu: (be[i], 0, f)),
                      pl.BlockSpec((1, tf, d), lambda i, f, be, nu: (be[i], f, 0))],
            out_specs=pl.BlockSpec((tb, d), lambda i, f, be, nu: (i, 0)),
            scratch_shapes=[pltpu.VMEM((tb, d), BF16), pltpu.VMEM((tb, d), F32)]),
        compiler_params=_params("parallel", "arbitrary"),
        name="moe_experts",
    )(blk_expert, n_used, x_pad, wg, wu, wd)


def _combine_kernel(dest_ref, y_hbm, x_ref, p_ref, mod_ref, o_ref, buf0, buf1, sem, *, td, spb):
    tok0 = (pl.program_id(0) * spb + pl.program_id(1)) * td

    def row_copies(r):
        a = (tok0 + r) * TOP_K
        return (pltpu.make_async_copy(y_hbm.at[dest_ref[a]], buf0.at[r], sem),
                pltpu.make_async_copy(y_hbm.at[dest_ref[a + 1]], buf1.at[r], sem))

    def issue(r, c):
        c0, c1 = row_copies(r)
        c0.start()
        c1.start()
        return c

    def drain(r, c):
        c0, c1 = row_copies(r)
        c0.wait()
        c1.wait()
        return c

    lax.fori_loop(0, td, issue, 0)
    lax.fori_loop(0, td, drain, 0)
    w = p_ref[...]
    f = w[:, 0:1] * buf0[...] + w[:, 1:2] * buf1[...]
    o_ref[0] = x_ref[0] + mod_ref[0, 5:6, :] * f


def moe_combine(y, dest, x, top_p, mod, td=256):
    b, s, d = x.shape
    spb = s // td
    return pl.pallas_call(
        functools.partial(_combine_kernel, td=td, spb=spb),
        out_shape=jax.ShapeDtypeStruct((b, s, d), F32),
        grid_spec=pltpu.PrefetchScalarGridSpec(
            num_scalar_prefetch=1,
            grid=(b, spb),
            in_specs=[pl.BlockSpec(memory_space=pl.ANY),
                      pl.BlockSpec((1, td, d), lambda i, j, dr: (i, j, 0)),
                      pl.BlockSpec((td, N_EXPERTS), lambda i, j, dr: (i * spb + j, 0)),
                      pl.BlockSpec((1, 6, d), lambda i, j, dr: (i, 0, 0))],
            out_specs=pl.BlockSpec((1, td, d), lambda i, j, dr: (i, j, 0)),
            scratch_shapes=[pltpu.VMEM((td, d), F32), pltpu.VMEM((td, d), F32),
                            pltpu.SemaphoreType.DMA(())]),
        compiler_params=_params("arbitrary", "arbitrary"),
        name="moe_combine",
    )(dest, y, x, top_p, mod)


def moe_ffn(x, g, mod, w_router, wg, wu, wd, tb=512):
    b, s, d = x.shape
    t = b * s
    a = t * TOP_K
    h, top_e, top_p = moe_router(x, g, mod, w_router)
    e_flat = top_e[:, 0:TOP_K].reshape(a)
    onehot = (e_flat[:, None] == jnp.arange(N_EXPERTS, dtype=I32)[None, :]).astype(I32)
    csum = jnp.cumsum(onehot, axis=0)
    rank = jnp.sum(onehot * csum, axis=1) - 1
    counts = csum[-1]
    padded = (counts + tb - 1) // tb * tb
    pad_ends = jnp.cumsum(padded)
    pad_starts = pad_ends - padded
    dest = (jnp.sum(onehot * pad_starts[None, :], axis=1) + rank).astype(I32)
    n_rows = (a // tb + N_EXPERTS) * tb
    n_blk = n_rows // tb
    blk_expert = jnp.minimum(
        jnp.searchsorted(pad_ends, jnp.arange(n_blk, dtype=I32) * tb, side="right"), N_EXPERTS - 1).astype(I32)
    n_used = (pad_ends[-1:] // tb).astype(I32)
    x_pad = moe_dispatch(h, dest, n_rows)
    y = moe_experts(x_pad, blk_expert, n_used, wg.astype(BF16), wu.astype(BF16), wd.astype(BF16), tb)
    return moe_combine(y, dest, x, top_p, mod)


def _pack_w_in(w_in):
    kv = 2 * NSA_KV_HEADS * HEAD_DIM * 3
    o = 0
    nsa_q = w_in[:, o:o + MIX_WIDTH]; o += MIX_WIDTH
    nsa_kv = w_in[:, o:o + kv]; o += kv
    nsa_gate = w_in[:, o:o + NSA_HEADS * N_BRANCH]; o += NSA_HEADS * N_BRANCH
    sb = w_in[:, o:o + 3 * MIX_WIDTH]; o += 3 * MIX_WIDTH
    ml_qkv = w_in[:, o:o + 3 * MIX_WIDTH]; o += 3 * MIX_WIDTH
    ml_if = w_in[:, o:o + 2 * ML_HEADS]; o += 2 * ML_HEADS
    ml_o = w_in[:, o:o + MIX_WIDTH]; o += MIX_WIDTH
    merge = w_in[:, o:]
    main = jnp.concatenate([merge, nsa_q, ml_qkv, ml_o, sb, nsa_kv], axis=1).astype(BF16)
    small = jnp.concatenate([nsa_gate, ml_if], axis=1)
    small = jnp.pad(small, ((0, 0), (0, N_SMALL - small.shape[1]))).astype(BF16)
    return main, small


def token_mixer_layer(x, mod, norm_g, w_in, nsa_q_norm, nsa_k_norm, cmp_pos, cmp_w1, cmp_b1, cmp_w2,
                      cmp_b2, ml_conv_w, ml_conv_b, ml_gate_b, w_branch, w_out):
    b, s, d = x.shape
    t = b * s
    h = normmod(x, norm_g, mod, 0, 1).reshape(t, d)
    w_main, w_small = _pack_w_in(w_in)
    y3 = matmul(h, w_main, BF16, 512, N_MAIN // 2).reshape(b, s, N_MAIN)
    small3 = matmul(h, w_small, F32, 1024, N_SMALL).reshape(b, s, N_SMALL)
    o_sb = sb_attention(y3)
    o_ml = mlstm(y3, small3, ml_conv_w, ml_conv_b, ml_gate_b)
    normed = nsa_headnorm(y3, nsa_q_norm, nsa_k_norm)
    kc, vc = nsa_compress(y3, cmp_pos, cmp_w1, cmp_b1, cmp_w2, cmp_b2, nsa_k_norm[0])
    o_nsa = nsa_attention(y3, small3, normed, kc, vc)
    return merge_project(o_nsa, o_sb, o_ml, y3, w_branch, w_out, x, mod)


def kernel(x, c, ada_w, ada_b, norm_mix, norm_ffn, w_in, nsa_q_norm, nsa_k_norm, cmp_pos, cmp_w1, cmp_b1,
           cmp_w2, cmp_b2, ml_conv_w, ml_conv_b, ml_gate_b, w_branch, w_out, ffn_wg, ffn_wu, ffn_wd,
           moe_router, moe_wg, moe_wu, moe_wd):
    depth = ada_w.shape[0]
    b, s, d = x.shape
    mods = adaln(c, ada_w, ada_b).reshape(depth, b, 6, d)
    for layer in range(depth):
        mod = mods[layer]
        x = token_mixer_layer(x, mod, norm_mix[layer], w_in[layer], nsa_q_norm[layer], nsa_k_norm[layer],
                              cmp_pos[layer], cmp_w1[layer], cmp_b1[layer], cmp_w2[layer], cmp_b2[layer],
                              ml_conv_w[layer], ml_conv_b[layer], ml_gate_b[layer], w_branch[layer],
                              w_out[layer])
        j = layer // 2
        if layer % 2 == 0:
            x = dense_ffn(x, norm_ffn[layer], mod, ffn_wg[j], ffn_wu[j], ffn_wd[j])
        else:
            x = moe_ffn(x, norm_ffn[layer], mod, moe_router[j], moe_wg[j], moe_wu[j], moe_wd[j])
    return x
```

```python
import functools

import numpy as np
import jax
import jax.numpy as jnp
from jax import lax
from jax.experimental import pallas as pl
from jax.experimental.pallas import tpu as pltpu

F32 = jnp.float32
BF16 = jnp.bfloat16
I32 = jnp.int32
HIGHEST = lax.Precision.HIGHEST

EPS = 1e-6
NEG = -1e30
HEAD_DIM = 64
MIX_WIDTH = 512
NSA_HEADS = 8
NSA_KV_HEADS = 2
NSA_GROUP = NSA_HEADS // NSA_KV_HEADS
CMP_BLOCK = 32
CMP_STRIDE = 16
SEL_BLOCK = 64
SEL_TOPK = 16
WINDOW = 512
FORCE_BONUS = 1e4
ML_HEADS = 4
ML_HEAD_DIM = 128
ML_CHUNK = 64
CONV_WIDTH = 4
N_BRANCH = 3
N_EXPERTS = 8
TOP_K = 2
LANES = 128

C_MERGE = 0
C_NSA_Q = 3072
C_ML_Q = 3584
C_ML_K = 4096
C_ML_V = 4608
C_ML_O = 5120
C_SB_Q = 5632
C_SB_K = 6144
C_SB_V = 6656
C_NSA_KV = 7168
N_MAIN = 7936
S_NSA_GATE = 0
S_ML_I = 24
S_ML_F = 28
N_SMALL = 128

VMEM_LIMIT = 56 * 1024 * 1024


def _params(*sem):
    return pltpu.CompilerParams(dimension_semantics=sem, vmem_limit_bytes=VMEM_LIMIT)


def _iota(shape, dim):
    return lax.broadcasted_iota(I32, shape, dim)


def _split_dot(a32, b_bf16):
    hi = a32.astype(BF16)
    lo = (a32 - hi.astype(F32)).astype(BF16)
    return (jnp.dot(hi, b_bf16, preferred_element_type=F32)
            + jnp.dot(lo, b_bf16, preferred_element_type=F32))


def _split_dot_left(a_bf16, b32):
    hi = b32.astype(BF16)
    lo = (b32 - hi.astype(F32)).astype(BF16)
    return (jnp.dot(a_bf16, hi, preferred_element_type=F32)
            + jnp.dot(a_bf16, lo, preferred_element_type=F32))


def _dot_nt(a, b):
    return lax.dot_general(a, b, (((1,), (1,)), ((), ())), preferred_element_type=F32)


def _log_sigmoid(z):
    return jnp.minimum(z, 0.0) - jnp.log1p(jnp.exp(-jnp.abs(z)))


def _adaln_kernel(c_ref, w_ref, b_ref, o_ref):
    c = c_ref[...]
    cond = c * jax.nn.sigmoid(c)
    o_ref[0] = jnp.dot(cond, w_ref[0], precision=HIGHEST, preferred_element_type=F32) + b_ref[0]


def adaln(c, ada_w, ada_b):
    depth, d, n = ada_w.shape
    b = c.shape[0]
    tn = 1536
    return pl.pallas_call(
        _adaln_kernel,
        out_shape=jax.ShapeDtypeStruct((depth, b, n), F32),
        grid=(depth, n // tn),
        in_specs=[pl.BlockSpec((b, d), lambda l, j: (0, 0)),
                  pl.BlockSpec((1, d, tn), lambda l, j: (l, 0, j)),
                  pl.BlockSpec((1, 1, tn), lambda l, j: (l, 0, j))],
        out_specs=pl.BlockSpec((1, b, tn), lambda l, j: (l, 0, j)),
        compiler_params=_params("parallel", "parallel"),
        name="adaln",
    )(c, ada_w, ada_b.reshape(depth, 1, n))


def _norm_mod(x, g, mod, shift_row, scale_row):
    ms = jnp.mean(x * x, axis=-1, keepdims=True)
    y = x * lax.rsqrt(ms + EPS) * g
    return y * (1.0 + mod[scale_row:scale_row + 1, :]) + mod[shift_row:shift_row + 1, :]


def _normmod_kernel(x_ref, g_ref, mod_ref, o_ref, *, shift_row, scale_row):
    o_ref[0] = _norm_mod(x_ref[0], g_ref[...], mod_ref[0], shift_row, scale_row).astype(o_ref.dtype)


def normmod(x, g, mod, shift_row, scale_row, ts=512):
    b, s, d = x.shape
    return pl.pallas_call(
        functools.partial(_normmod_kernel, shift_row=shift_row, scale_row=scale_row),
        out_shape=jax.ShapeDtypeStruct((b, s, d), BF16),
        grid=(b, s // ts),
        in_specs=[pl.BlockSpec((1, ts, d), lambda i, j: (i, j, 0)),
                  pl.BlockSpec((1, d), lambda i, j: (0, 0)),
                  pl.BlockSpec((1, 6, d), lambda i, j: (i, 0, 0))],
        out_specs=pl.BlockSpec((1, ts, d), lambda i, j: (i, j, 0)),
        compiler_params=_params("parallel", "parallel"),
        name="normmod",
    )(x, g.reshape(1, d), mod)


def _mm_kernel(a_ref, w_ref, o_ref):
    o_ref[...] = jnp.dot(a_ref[...], w_ref[...], preferred_element_type=F32).astype(o_ref.dtype)


def matmul(a, w, out_dtype, tm, tn):
    m, k = a.shape
    n = w.shape[1]
    return pl.pallas_call(
        _mm_kernel,
        out_shape=jax.ShapeDtypeStruct((m, n), out_dtype),
        grid=(m // tm, n // tn),
        in_specs=[pl.BlockSpec((tm, k), lambda i, j: (i, 0)),
                  pl.BlockSpec((k, tn), lambda i, j: (0, j))],
        out_specs=pl.BlockSpec((tm, tn), lambda i, j: (i, j)),
        compiler_params=_params("parallel", "parallel"),
        name="matmul",
    )(a, w)


SB_EXP_FLOOR = -104.0


def _sb_kernel(q_ref, k_ref, v_ref, o_ref, *, tq, tk):
    qi = pl.program_id(2)
    q0 = qi * tq
    lo_half = _iota((1, LANES), 1) < HEAD_DIM
    qs = q_ref[0] * jnp.asarray(HEAD_DIM ** -0.5, BF16)
    zero = jnp.zeros_like(qs)
    q_a = jnp.where(lo_half, qs, zero)
    q_b = jnp.where(lo_half, zero, qs)
    t_idx = q0 + _iota((tq, tk), 0)
    s_loc = _iota((tq, tk), 1)
    upper = (_iota((tk, tk), 0) > _iota((tk, tk), 1)).astype(BF16)

    def head(qh, kblk, vblk, strict, carry, acc):
        z = _dot_nt(qh, kblk)
        ls = _log_sigmoid(z)
        lk = jnp.where(strict, ls - z, 0.0)
        after = _split_dot(lk, upper) + carry
        a = jnp.where(strict, jnp.exp(ls + after), 0.0)
        acc = acc + jnp.dot(a.astype(BF16), vblk, preferred_element_type=F32)
        carry = carry + jnp.sum(lk, axis=1, keepdims=True)
        return carry, acc

    def cond(st):
        j, ca, cb, _, _ = st
        return (j >= 0) & (jnp.max(jnp.maximum(ca, cb)) > SB_EXP_FLOOR)

    def body(st):
        j, ca, cb, acc_a, acc_b = st
        k0 = pl.multiple_of(j * tk, tk)
        kblk = k_ref[0, pl.ds(k0, tk), :]
        vblk = v_ref[0, pl.ds(k0, tk), :]
        strict = (k0 + s_loc) < t_idx
        ca, acc_a = head(q_a, kblk, vblk, strict, ca, acc_a)
        cb, acc_b = head(q_b, kblk, vblk, strict, cb, acc_b)
        return j - 1, ca, cb, acc_a, acc_b

    c0 = jnp.zeros((tq, 1), F32)
    a0 = jnp.zeros((tq, LANES), F32)
    _, _, _, acc_a, acc_b = lax.while_loop(cond, body, ((q0 + tq) // tk - 1, c0, c0, a0, a0))
    o_ref[0] = jnp.where(lo_half, acc_a, acc_b).astype(o_ref.dtype)


def sb_attention(y3, tq=256, tk=128):
    b, s, _ = y3.shape
    n_pairs = MIX_WIDTH // LANES
    qb, kb, vb = C_SB_Q // LANES, C_SB_K // LANES, C_SB_V // LANES
    return pl.pallas_call(
        functools.partial(_sb_kernel, tq=tq, tk=tk),
        out_shape=jax.ShapeDtypeStruct((b, s, MIX_WIDTH), BF16),
        grid=(b, n_pairs, s // tq),
        in_specs=[pl.BlockSpec((1, tq, LANES), lambda i, p, j: (i, j, qb + p)),
                  pl.BlockSpec((1, s, LANES), lambda i, p, j: (i, 0, kb + p)),
                  pl.BlockSpec((1, s, LANES), lambda i, p, j: (i, 0, vb + p))],
        out_specs=pl.BlockSpec((1, tq, LANES), lambda i, p, j: (i, j, p)),
        compiler_params=_params("parallel", "parallel", "parallel"),
        name="sb_attention",
    )(y3, y3, y3)


def _mlstm_kernel(q_ref, k_ref, v_ref, og_ref, sm_ref, gr_ref, cw_ref, cb_ref, gb_ref, out_ref,
                  ct_ref, n_ref, m_ref, xbuf_ref, qk_ref, *, ts):
    L, dh, H, W = ML_CHUNK, ML_HEAD_DIM, ML_HEADS, MIX_WIDTH
    halo = 8
    sblk = pl.program_id(1)

    @pl.when(sblk == 0)
    def _():
        ct_ref[...] = jnp.zeros_like(ct_ref)
        n_ref[...] = jnp.zeros_like(n_ref)
        m_ref[...] = jnp.zeros_like(m_ref)
        xbuf_ref[0:halo, :] = jnp.zeros((halo, 2 * W), F32)

    @pl.when(sblk > 0)
    def _():
        xbuf_ref[0:halo, :] = xbuf_ref[ts:ts + halo, :]

    xbuf_ref[halo:halo + ts, 0:W] = q_ref[0].astype(F32)
    xbuf_ref[halo:halo + ts, W:2 * W] = k_ref[0].astype(F32)
    conv = cb_ref[...] + jnp.zeros((ts, 2 * W), F32)
    for j in range(CONV_WIDTH):
        off = halo - (CONV_WIDTH - 1) + j
        conv = conv + cw_ref[j:j + 1, :] * xbuf_ref[off:off + ts, :]
    act = conv * jax.nn.sigmoid(conv)
    qk_ref[:, 0:W] = (act[:, 0:W] * (dh ** -0.5)).astype(BF16)
    qk_ref[:, W:2 * W] = act[:, W:2 * W].astype(BF16)

    it0, it1 = _iota((L, L), 0), _iota((L, L), 1)
    causal = it0 >= it1
    tri_lo = causal.astype(BF16)
    tri_up = (it0 <= it1).astype(BF16)

    def chunk(c, carry):
        r0 = pl.multiple_of(c * L, L)
        sm = sm_ref[0, pl.ds(r0, L), :]
        gr = gr_ref[0, c]
        for h in range(H):
            ig_col = sm[:, S_ML_I + h:S_ML_I + h + 1] + gb_ref[0, h]
            lf_col = _log_sigmoid(sm[:, S_ML_F + h:S_ML_F + h + 1] + gb_ref[1, h])
            ig_row = gr[h:h + 1, :] + gb_ref[0, h]
            lf_row = _log_sigmoid(gr[H + h:H + h + 1, :] + gb_ref[1, h])
            b_t = _split_dot_left(tri_lo, jnp.broadcast_to(lf_col, (L, L)))
            b_s = _split_dot(jnp.broadcast_to(lf_row, (L, L)), tri_up)
            dmat = jnp.where(causal, b_t - b_s + ig_row, NEG)
            b_col = b_t[:, 0:1]
            m_prev = m_ref[h][:, 0:1]
            m_inter = b_col + m_prev
            m_t = jnp.maximum(m_inter, jnp.max(dmat, axis=1, keepdims=True))
            cols = slice(h * dh, (h + 1) * dh)
            qq = qk_ref[pl.ds(r0, L), cols]
            kk = qk_ref[pl.ds(r0, L), W + h * dh:W + (h + 1) * dh]
            vv = v_ref[0, pl.ds(r0, L), cols]
            w = jnp.exp(dmat - m_t) * _dot_nt(qq, kk)
            inter = jnp.exp(m_inter - m_t)
            ct = ct_ref[h]
            nvec = n_ref[h]
            num = (inter * jnp.dot(qq, ct.astype(BF16), preferred_element_type=F32)
                   + jnp.dot(w.astype(BF16), vv, preferred_element_type=F32))
            den = (inter * jnp.sum(qq.astype(F32) * nvec, axis=1, keepdims=True)
                   + jnp.sum(w, axis=1, keepdims=True))
            hval = num / jnp.maximum(jnp.abs(den), jnp.exp(-m_t))
            b_last = b_t[L - 1:L, 0:1]
            decay = b_last - b_col + ig_col
            m_new = jnp.maximum(b_last + m_prev, jnp.max(decay, axis=0, keepdims=True))
            ws = jnp.exp(decay - m_new)
            cscale = jnp.exp(b_last + m_prev - m_new)
            wv = (ws * vv.astype(F32)).astype(BF16)
            kt = kk.astype(F32).T.astype(BF16)
            ct_ref[h] = cscale * ct + jnp.dot(kt, wv, preferred_element_type=F32)
            n_ref[h] = cscale * nvec + jnp.sum(ws * kk.astype(F32), axis=0, keepdims=True)
            m_ref[h] = jnp.broadcast_to(m_new, (1, LANES))
            gate = jax.nn.sigmoid(og_ref[0, pl.ds(r0, L), cols].astype(F32))
            out_ref[0, pl.ds(r0, L), cols] = (gate * hval).astype(out_ref.dtype)
        return carry

    lax.fori_loop(0, ts // L, chunk, 0)


def mlstm(y3, small3, conv_w, conv_b, gate_b, ts=512):
    b, s, _ = y3.shape
    W, H, L = MIX_WIDTH, ML_HEADS, ML_CHUNK
    gr = small3[:, :, S_ML_I:S_ML_I + 2 * H].reshape(b, s // L, L, 2 * H).transpose(0, 1, 3, 2)
    cq, ck, cv, co = C_ML_Q // W, C_ML_K // W, C_ML_V // W, C_ML_O // W
    return pl.pallas_call(
        functools.partial(_mlstm_kernel, ts=ts),
        out_shape=jax.ShapeDtypeStruct((b, s, W), BF16),
        grid=(b, s // ts),
        in_specs=[pl.BlockSpec((1, ts, W), lambda i, j: (i, j, cq)),
                  pl.BlockSpec((1, ts, W), lambda i, j: (i, j, ck)),
                  pl.BlockSpec((1, ts, W), lambda i, j: (i, j, cv)),
                  pl.BlockSpec((1, ts, W), lambda i, j: (i, j, co)),
                  pl.BlockSpec((1, ts, N_SMALL), lambda i, j: (i, j, 0)),
                  pl.BlockSpec((1, ts // L, 2 * H, L), lambda i, j: (i, j, 0, 0)),
                  pl.BlockSpec((CONV_WIDTH, 2 * W), lambda i, j: (0, 0)),
                  pl.BlockSpec((1, 2 * W), lambda i, j: (0, 0)),
                  pl.BlockSpec(memory_space=pltpu.SMEM)],
        out_specs=pl.BlockSpec((1, ts, W), lambda i, j: (i, j, 0)),
        scratch_shapes=[pltpu.VMEM((H, ML_HEAD_DIM, ML_HEAD_DIM), F32),
                        pltpu.VMEM((H, 1, ML_HEAD_DIM), F32),
                        pltpu.VMEM((H, 1, LANES), F32),
                        pltpu.VMEM((ts + 8, 2 * W), F32),
                        pltpu.VMEM((ts, 2 * W), BF16)],
        compiler_params=_params("parallel", "arbitrary"),
        name="mlstm",
    )(y3, y3, y3, y3, small3, gr, conv_w, conv_b.reshape(1, 2 * W), gate_b)


def _headnorm_kernel(x_ref, g_ref, o_ref):
    x = x_ref[0].astype(F32)
    same_head = (_iota((LANES, LANES), 0) // HEAD_DIM == _iota((LANES, LANES), 1) // HEAD_DIM)
    ss = _split_dot(x * x, same_head.astype(BF16))
    o_ref[0] = (x * lax.rsqrt(ss * (1.0 / HEAD_DIM) + EPS) * g_ref[0]).astype(o_ref.dtype)


def nsa_headnorm(y3, q_norm, k_norm, ts=1024):
    b, s, _ = y3.shape
    qb = C_NSA_Q // LANES
    ksb = C_NSA_KV // LANES + 2
    kwb = C_NSA_KV // LANES + 4
    gains = jnp.stack([jnp.tile(q_norm, 2)] * 4 + [jnp.tile(k_norm[1], 2), jnp.tile(k_norm[2], 2)])

    def col(j):
        return jnp.where(j < 4, qb + j, jnp.where(j == 4, ksb, kwb))

    return pl.pallas_call(
        _headnorm_kernel,
        out_shape=jax.ShapeDtypeStruct((b, s, 6 * LANES), BF16),
        grid=(b, s // ts, 6),
        in_specs=[pl.BlockSpec((1, ts, LANES), lambda i, t, j: (i, t, col(j))),
                  pl.BlockSpec((1, 1, LANES), lambda i, t, j: (j, 0, 0))],
        out_specs=pl.BlockSpec((1, ts, LANES), lambda i, t, j: (i, t, j)),
        compiler_params=_params("parallel", "parallel", "parallel"),
        name="nsa_headnorm",
    )(y3, gains.reshape(6, 1, LANES))


def _gelu_tanh(x):
    return 0.5 * x * (1.0 + jnp.tanh(0.7978845608028654 * (x + 0.044715 * (x * x * x))))


def _compress_kernel(ra_ref, rb_ref, pos_ref, w1_ref, b1_ref, w2_ref, b2_ref, kn_ref, kc_ref, vc_ref):
    half = (CMP_BLOCK // 2) * HEAD_DIM
    for j, o_ref in enumerate((kc_ref, vc_ref)):
        xa = (ra_ref[j, 0, 0].astype(F32) + pos_ref[j, :, 0:half]).astype(BF16)
        xb = (rb_ref[j, 0, 0].astype(F32) + pos_ref[j, :, half:2 * half]).astype(BF16)
        hid = (jnp.dot(xa, w1_ref[j, 0:half, :], preferred_element_type=F32)
               + jnp.dot(xb, w1_ref[j, half:2 * half, :], preferred_element_type=F32) + b1_ref[j])
        out = jnp.dot(_gelu_tanh(hid).astype(BF16), w2_ref[j], preferred_element_type=F32) + b2_ref[j]
        if j == 0:
            out = out * lax.rsqrt(jnp.mean(out * out, axis=-1, keepdims=True) + EPS) * kn_ref[...]
        o_ref[0, 0] = out


def nsa_compress(y3, cmp_pos, cmp_w1, cmp_b1, cmp_w2, cmp_b2, k_norm0):
    b, s, _ = y3.shape
    G, dh = NSA_KV_HEADS, HEAD_DIM
    nr = s // CMP_STRIDE
    wide = CMP_STRIDE * dh
    kv = y3[:, :, C_NSA_KV:C_NSA_KV + 2 * G * dh].reshape(b, s, 2, G, dh)
    ra = kv.transpose(2, 0, 3, 1, 4).reshape(2, b, G, nr, wide)
    rb = jnp.concatenate([ra[:, :, :, 1:], jnp.zeros((2, b, G, 1, wide), ra.dtype)], axis=3)
    hidden = cmp_w1.shape[-1]
    out = jax.ShapeDtypeStruct((b, G, nr, dh), F32)
    blk = pl.BlockSpec((2, 1, 1, nr, wide), lambda i, g: (0, i, g, 0, 0))
    oblk = pl.BlockSpec((1, 1, nr, dh), lambda i, g: (i, g, 0, 0))

    def full(shape):
        return pl.BlockSpec(shape, lambda i, g: (0,) * len(shape))

    return pl.pallas_call(
        _compress_kernel,
        out_shape=(out, out),
        grid=(b, G),
        in_specs=[blk, blk, full((2, 1, 2 * wide)), full((2, 2 * wide, hidden)), full((2, 1, hidden)),
                  full((2, hidden, dh)), full((2, 1, dh)), full((1, dh))],
        out_specs=(oblk, oblk),
        compiler_params=_params("parallel", "parallel"),
        name="nsa_compress",
    )(ra, rb, cmp_pos.reshape(2, 1, 2 * wide), cmp_w1.astype(BF16), cmp_b1.reshape(2, 1, hidden),
      cmp_w2.astype(BF16), cmp_b2.reshape(2, 1, dh), k_norm0.reshape(1, dh))


def _nsa_kernel_rowmajor(q_ref, gl_ref, kc_ref, vc_ref, ks_ref, vs_ref, kw_ref, vw_ref, o_ref, *, n_sel, top):
    QB, R, dh = 128, NSA_GROUP, HEAD_DIM
    rows = R * QB
    g = pl.program_id(1)
    qi = pl.program_id(2)
    q0 = qi * QB
    q = q_ref[0, 0, 0] * jnp.asarray(dh ** -0.5, BF16)
    row = _iota((rows, 1), 0)
    t_f = (q0 + (row & (QB - 1))).astype(F32)
    head = g * R + (row >> 7)
    slope = jnp.exp2(-(head + 1).astype(F32))

    n_cmp = kc_ref.shape[2]
    kc = kc_ref[0, 0]
    kc_hi = kc.astype(BF16)
    kc_lo = (kc - kc_hi.astype(F32)).astype(BF16)
    cmp_end = (_iota((1, n_cmp), 1) * CMP_STRIDE + (CMP_BLOCK - 1)).astype(F32)
    dist = t_f - cmp_end
    valid = dist >= 0.0
    s = jnp.where(valid, _dot_nt(q, kc_hi) + _dot_nt(q, kc_lo) - slope * dist, NEG)
    e = jnp.exp(s - jnp.max(s, axis=1, keepdims=True))
    p = jnp.where(valid, e / jnp.sum(e, axis=1, keepdims=True), 0.0)
    o_cmp = jnp.dot(p.astype(BF16), vc_ref[0, 0].astype(BF16), preferred_element_type=F32)

    p_grp = p[0:QB] + p[QB:2 * QB] + p[2 * QB:3 * QB] + p[3 * QB:4 * QB]
    c0 = _iota((n_cmp, n_sel), 0) * CMP_STRIDE
    s0 = _iota((n_cmp, n_sel), 1) * SEL_BLOCK
    overlap = ((c0 < s0 + SEL_BLOCK) & (c0 + CMP_BLOCK > s0)).astype(BF16)
    imp = _split_dot(p_grp, overlap)
    tq = q0 + _iota((QB, 1), 0)
    j_idx = _iota((1, n_sel), 1)
    cur = tq >> 6
    forced = (j_idx == 0) | (j_idx == cur) | (j_idx == cur - 1)
    imp = jnp.where(j_idx * SEL_BLOCK <= tq, imp + jnp.where(forced, FORCE_BONUS, 0.0), -1.0)
    sel = jnp.zeros((QB, n_sel), F32)
    for _ in range(top):
        mx = jnp.max(imp, axis=1, keepdims=True)
        first = jnp.min(jnp.where(imp == mx, j_idx, n_sel), axis=1, keepdims=True)
        pick = j_idx == first
        sel = jnp.where(pick, 1.0, sel)
        imp = jnp.where(pick, -3e38, imp)
    sel_bf = sel.astype(BF16)

    def flash(k_ref, v_ref, lo, hi, mask_fn):
        def body(kb, st):
            m_run, l_run, acc = st
            k0 = pl.multiple_of(kb * QB, QB)
            kblk = k_ref[0, 0, pl.ds(k0, QB), :]
            vblk = v_ref[0, 0, pl.ds(k0, QB), :]
            dist = t_f - (k0 + _iota((1, QB), 1)).astype(F32)
            ok = mask_fn(kb, dist)
            sc = jnp.where(ok, _dot_nt(q, kblk) - slope * dist, NEG)
            m_new = jnp.maximum(m_run, jnp.max(sc, axis=1, keepdims=True))
            alpha = jnp.exp(m_run - m_new)
            pr = jnp.where(ok, jnp.exp(sc - m_new), 0.0)
            l_run = alpha * l_run + jnp.sum(pr, axis=1, keepdims=True)
            acc = alpha * acc + jnp.dot(pr.astype(BF16), vblk, preferred_element_type=F32)
            return m_new, l_run, acc

        init = (jnp.full((rows, 1), NEG, F32), jnp.zeros((rows, 1), F32), jnp.zeros((rows, dh), F32))
        _, l_run, acc = lax.fori_loop(lo, hi, body, init)
        return acc / l_run

    def sel_mask(kb, dist):
        expand = (_iota((n_sel, QB), 0) == 2 * kb + (_iota((n_sel, QB), 1) >> 6)).astype(BF16)
        m = jnp.dot(sel_bf, expand, preferred_element_type=F32)
        m = jnp.concatenate([m] * R, axis=0)
        return (m > 0.5) & (dist >= 0.0)

    def win_mask(kb, dist):
        return (dist >= 0.0) & (dist < float(WINDOW))

    o_sel = flash(ks_ref, vs_ref, 0, qi + 1, sel_mask)
    o_win = flash(kw_ref, vw_ref, jnp.maximum(qi - WINDOW // QB, 0), qi + 1, win_mask)

    gate = jax.nn.sigmoid(gl_ref[0, 0, 0])
    o_ref[0, 0, 0] = (gate[:, 0:1] * o_cmp + gate[:, 1:2] * o_sel + gate[:, 2:3] * o_win).astype(o_ref.dtype)


def nsa_attention_rowmajor(y3, small3, normed, kc, vc):
    b, s, _ = y3.shape
    G, R, dh, QB = NSA_KV_HEADS, NSA_GROUP, HEAD_DIM, 128
    nq = s // QB
    n_sel = s // SEL_BLOCK
    top = min(SEL_TOPK, n_sel)

    def stack_heads(a, width):
        return (a.reshape(b, nq, QB, G, R, width).transpose(0, 3, 1, 4, 2, 5)
                .reshape(b, G, nq, R * QB, width))

    def kv_heads(a):
        return a.reshape(b, s, G, dh).transpose(0, 2, 1, 3)

    q = stack_heads(normed[:, :, 0:MIX_WIDTH], dh)
    gl = stack_heads(small3[:, :, S_NSA_GATE:S_NSA_GATE + NSA_HEADS * N_BRANCH], N_BRANCH)
    ks = kv_heads(normed[:, :, 4 * LANES:5 * LANES])
    kw = kv_heads(normed[:, :, 5 * LANES:6 * LANES])
    vs = kv_heads(y3[:, :, C_NSA_KV + 3 * LANES:C_NSA_KV + 4 * LANES])
    vw = kv_heads(y3[:, :, C_NSA_KV + 5 * LANES:C_NSA_KV + 6 * LANES])
    n_cmp = kc.shape[2]
    qspec = pl.BlockSpec((1, 1, 1, R * QB, dh), lambda i, g, j: (i, g, j, 0, 0))
    gspec = pl.BlockSpec((1, 1, 1, R * QB, N_BRANCH), lambda i, g, j: (i, g, j, 0, 0))
    cspec = pl.BlockSpec((1, 1, n_cmp, dh), lambda i, g, j: (i, g, 0, 0))
    kvspec = pl.BlockSpec((1, 1, s, dh), lambda i, g, j: (i, g, 0, 0))
    out = pl.pallas_call(
        functools.partial(_nsa_kernel_rowmajor, n_sel=n_sel, top=top),
        out_shape=jax.ShapeDtypeStruct((b, G, nq, R * QB, dh), BF16),
        grid=(b, G, nq),
        in_specs=[qspec, gspec, cspec, cspec, kvspec, kvspec, kvspec, kvspec],
        out_specs=qspec,
        compiler_params=_params("parallel", "parallel", "parallel"),
        name="nsa_attention",
    )(q, gl, kc, vc, ks, vs, kw, vw)
    return (out.reshape(b, G, nq, R, QB, dh).transpose(0, 2, 4, 1, 3, 5).reshape(b, s, MIX_WIDTH))


NSA_QB = 128
NSA_KS = 512
NSA_CK = 256
NSA_VR = 80
A_FEAT, A_PEN, A_BIAS = 0, 64, 128


def _nsa_t_kernel(qT_ref, glT_ref, kca_ref, vcT_ref, ksn_ref, kwn_ref, vsT_ref, vwT_ref, kconst_ref, o_ref,
                  qt_ref, ksa_ref, kwa_ref, vsa_ref, vwa_ref, *, n_sel, top):
    QB, R, dh, CK, VR, KS = NSA_QB, NSA_GROUP, HEAD_DIM, NSA_CK, NSA_VR, NSA_KS
    HQ = R * QB
    g = pl.program_id(1)
    qi = pl.program_id(2)
    q0 = qi * QB
    nkb = vsa_ref.shape[0]

    @pl.when(qi == 0)
    def _():
        ksa_ref[...] = kconst_ref[...]
        kwa_ref[...] = kconst_ref[...]
        kwa_ref[:, A_PEN:A_PEN + 64] = jnp.zeros((kwa_ref.shape[0], 64), BF16)
        for gg in range(NSA_KV_HEADS):
            @pl.when(g == gg)
            def _():
                ksa_ref[:, A_FEAT:A_FEAT + dh] = ksn_ref[0, :, gg * dh:(gg + 1) * dh]
                kwa_ref[:, A_FEAT:A_FEAT + dh] = kwn_ref[0, :, gg * dh:(gg + 1) * dh]
        ones_rows = jnp.where(_iota((nkb, VR - dh, QB), 1) == 0, 1.0, 0.0).astype(BF16)
        vsa_ref[:, 0:dh, :] = vsT_ref[0, 0]
        vwa_ref[:, 0:dh, :] = vwT_ref[0, 0]
        vsa_ref[:, dh:VR, :] = ones_rows
        vwa_ref[:, dh:VR, :] = ones_rows
        qt_ref[A_BIAS + 16:CK, :] = jnp.zeros((CK - A_BIAS - 16, HQ), BF16)

    qT = qT_ref[0, 0, 0] * jnp.asarray(dh ** -0.5, BF16)
    qt_ref[A_FEAT:A_FEAT + dh, :] = qT
    qt_ref[A_PEN:A_PEN + dh, :] = qT
    lane = _iota((16, HQ), 1)
    rowi = _iota((16, HQ), 0)
    t_q = q0 + (lane & (QB - 1))
    slope = jnp.exp2(-(g * R + (lane >> 7) + 1).astype(F32))
    t_hi = ((t_q >> 6) << 6).astype(F32)
    t_lo = (t_q & 63).astype(F32)
    bias_rows = jnp.where(rowi < 2, slope,
                          jnp.where(rowi == 2, -slope * t_hi, jnp.where(rowi == 3, -slope * t_lo, 0.0)))
    qt_ref[A_BIAS:A_BIAS + 16, :] = bias_rows.astype(BF16)

    n_cmp = kca_ref.shape[2]
    sc = jnp.dot(kca_ref[0, 0], qt_ref[...], preferred_element_type=F32)
    cmp_end = _iota((n_cmp, HQ), 0) * CMP_STRIDE + (CMP_BLOCK - 1)
    valid = cmp_end <= q0 + (_iota((n_cmp, HQ), 1) & (QB - 1))
    sc = jnp.where(valid, sc, NEG)
    e = jnp.exp(sc - jnp.max(sc, axis=0, keepdims=True))
    p = jnp.where(valid, e / jnp.sum(e, axis=0, keepdims=True), 0.0)
    o_cmp = jnp.dot(vcT_ref[0, 0], p.astype(BF16), preferred_element_type=F32)

    p_grp = p[:, 0:QB] + p[:, QB:2 * QB] + p[:, 2 * QB:3 * QB] + p[:, 3 * QB:4 * QB]
    c0 = _iota((n_sel, n_cmp), 1) * CMP_STRIDE
    s0 = _iota((n_sel, n_cmp), 0) * SEL_BLOCK
    overlap_t = ((c0 < s0 + SEL_BLOCK) & (c0 + CMP_BLOCK > s0)).astype(BF16)
    imp = _split_dot_left(overlap_t, p_grp)
    j_idx = _iota((n_sel, QB), 0)
    tq = q0 + _iota((n_sel, QB), 1)
    cur = tq >> 6
    forced = (j_idx == 0) | (j_idx == cur) | (j_idx == cur - 1)
    imp = jnp.where(j_idx * SEL_BLOCK <= tq, imp + jnp.where(forced, FORCE_BONUS, 0.0), -1.0)
    sel = jnp.zeros((n_sel, QB), F32)
    for _ in range(top):
        mx = jnp.max(imp, axis=0, keepdims=True)
        first = jnp.min(jnp.where(imp == mx, j_idx, n_sel), axis=0, keepdims=True)
        pick = j_idx == first
        sel = jnp.where(pick, 1.0, sel)
        imp = jnp.where(pick, -3e38, imp)
    pen = jnp.where(sel > 0.5, 0.0, NEG)
    if n_sel < 64:
        pen = jnp.concatenate([pen, jnp.zeros((64 - n_sel, QB), F32)], axis=0)
    qt_ref[A_PEN:A_PEN + 64, :] = jnp.concatenate([pen] * R, axis=1).astype(BF16)

    def attend(kaug_ref, vaug_ref, k0, nk, st, mode):
        k0 = pl.multiple_of(k0, QB)
        s = jnp.dot(kaug_ref[pl.ds(k0, nk), :], qt_ref[...], preferred_element_type=F32)
        if mode != "full":
            dist = (q0 + (_iota((nk, HQ), 1) & (QB - 1))) - (k0 + _iota((nk, HQ), 0))
            ok = dist >= 0
            if mode == "window":
                ok = ok & (dist < WINDOW)
            s = jnp.where(ok, s, NEG)
        m_run, acc = st
        m_new = jnp.maximum(m_run, jnp.max(s, axis=0, keepdims=True))
        pr = jnp.exp(s - m_new).astype(BF16)
        acc = jnp.exp(m_run - m_new) * acc
        kb0 = k0 // QB
        for i in range(nk // QB):
            acc = acc + jnp.dot(vaug_ref[kb0 + i], pr[i * QB:(i + 1) * QB], preferred_element_type=F32)
        return m_new, acc

    def finish(st):
        return st[1][0:dh] / st[1][dh:dh + 1]

    init = (jnp.full((1, HQ), NEG, F32), jnp.zeros((VR, HQ), F32))
    n_full = qi // (KS // QB)
    st = lax.fori_loop(0, n_full, lambda j, s_: attend(ksa_ref, vsa_ref, j * KS, KS, s_, "full"), init)
    o_sel = finish(attend(ksa_ref, vsa_ref, n_full * KS, KS, st, "causal"))
    o_win = finish(attend(kwa_ref, vwa_ref, jnp.maximum(q0 - WINDOW, 0), WINDOW + QB, init, "window"))

    gate = jax.nn.sigmoid(glT_ref[0, 0, 0])
    o_ref[0, 0, 0] = (gate[0:1] * o_cmp + gate[1:2] * o_sel + gate[2:3] * o_win).astype(o_ref.dtype)


def nsa_attention(y3, small3, normed, kc, vc):
    b, s, _ = y3.shape
    G, R, dh, QB, CK, VR = NSA_KV_HEADS, NSA_GROUP, HEAD_DIM, NSA_QB, NSA_CK, NSA_VR
    HQ = R * QB
    nq = s // QB
    n_sel = s // SEL_BLOCK
    assert n_sel <= 64, "selection one-hot columns hold at most 64 blocks"
    top = min(SEL_TOPK, n_sel)
    n_cmp = kc.shape[2]

    def heads_on_lanes(a, width):
        return (a.reshape(b, nq, QB, G, R, width).transpose(0, 3, 1, 5, 4, 2).reshape(b, G, nq, width, HQ))

    def value_blocks(a):
        return a.reshape(b, nq, QB, G, dh).transpose(0, 3, 1, 4, 2)

    def pos_cols(pos):
        return np.stack([pos // 64 * 64, pos % 64, np.ones_like(pos), np.ones_like(pos)], axis=1)

    q_t = heads_on_lanes(normed[:, :, 0:MIX_WIDTH], dh)
    gl_t = heads_on_lanes(small3[:, :, S_NSA_GATE:S_NSA_GATE + NSA_HEADS * N_BRANCH], N_BRANCH)
    vs_t = value_blocks(y3[:, :, C_NSA_KV + 3 * LANES:C_NSA_KV + 4 * LANES])
    vw_t = value_blocks(y3[:, :, C_NSA_KV + 5 * LANES:C_NSA_KV + 6 * LANES])
    vc_t = vc.transpose(0, 1, 3, 2).astype(BF16)

    pos = np.arange(s)
    kconst = np.zeros((s, CK), np.float32)
    kconst[pos, A_PEN + pos // SEL_BLOCK] = 1.0
    kconst[:, A_BIAS:A_BIAS + 4] = pos_cols(pos)
    kconst = jnp.asarray(kconst, BF16)

    kc_hi = kc.astype(BF16)
    kc_lo = (kc - kc_hi.astype(F32)).astype(BF16)
    cend = np.arange(n_cmp) * CMP_STRIDE + (CMP_BLOCK - 1)
    cbias = np.zeros((n_cmp, CK - 2 * dh), np.float32)
    cbias[:, 0:4] = pos_cols(cend)
    kc_aug = jnp.concatenate([kc_hi, kc_lo, jnp.broadcast_to(jnp.asarray(cbias, BF16), (b, G, n_cmp, CK - 2 * dh))],
                             axis=-1)

    qspec = pl.BlockSpec((1, 1, 1, dh, HQ), lambda i, g, j: (i, g, j, 0, 0))
    vspec = pl.BlockSpec((1, 1, nq, dh, QB), lambda i, g, j: (i, g, 0, 0, 0))
    out = pl.pallas_call(
        functools.partial(_nsa_t_kernel, n_sel=n_sel, top=top),
        out_shape=jax.ShapeDtypeStruct((b, G, nq, dh, HQ), BF16),
        grid=(b, G, nq),
        in_specs=[qspec,
                  pl.BlockSpec((1, 1, 1, N_BRANCH, HQ), lambda i, g, j: (i, g, j, 0, 0)),
                  pl.BlockSpec((1, 1, n_cmp, CK), lambda i, g, j: (i, g, 0, 0)),
                  pl.BlockSpec((1, 1, dh, n_cmp), lambda i, g, j: (i, g, 0, 0)),
                  pl.BlockSpec((1, s, LANES), lambda i, g, j: (i, 0, 4)),
                  pl.BlockSpec((1, s, LANES), lambda i, g, j: (i, 0, 5)),
                  vspec, vspec,
                  pl.BlockSpec((s, CK), lambda i, g, j: (0, 0))],
        out_specs=qspec,
        scratch_shapes=[pltpu.VMEM((CK, HQ), BF16),
                        pltpu.VMEM((s, CK), BF16), pltpu.VMEM((s, CK), BF16),
                        pltpu.VMEM((nq, VR, QB), BF16), pltpu.VMEM((nq, VR, QB), BF16)],
        compiler_params=_params("parallel", "parallel", "arbitrary"),
        name="nsa_attention",
    )(q_t, gl_t, kc_aug, vc_t, normed, normed, vs_t, vw_t, kconst)
    return out.reshape(b, G, nq, dh, R, QB).transpose(0, 2, 5, 1, 4, 3).reshape(b, s, MIX_WIDTH)


def _merge_kernel(on_ref, os_ref, om_ref, g0_ref, g1_ref, g2_ref, wb_ref, wo_ref, x_ref, mod_ref, o_ref):
    merged = None
    for i, (o_r, g_r) in enumerate(((on_ref, g0_ref), (os_ref, g1_ref), (om_ref, g2_ref))):
        br = jnp.dot(o_r[0], wb_ref[i], preferred_element_type=F32)
        term = jax.nn.sigmoid(g_r[0].astype(F32)) * br
        merged = term if merged is None else merged + term
    out = jnp.dot(merged.astype(BF16), wo_ref[...], preferred_element_type=F32)
    o_ref[0] = x_ref[0] + mod_ref[0, 2:3, :] * out


def merge_project(o_nsa, o_sb, o_ml, y3, w_branch, w_out, x, mod, tm=512):
    b, s, d = x.shape
    W = MIX_WIDTH
    ospec = pl.BlockSpec((1, tm, W), lambda i, j: (i, j, 0))
    xspec = pl.BlockSpec((1, tm, d), lambda i, j: (i, j, 0))
    gspecs = [pl.BlockSpec((1, tm, d), functools.partial(lambda i, j, c: (i, j, c), c=C_MERGE // d + c))
              for c in range(N_BRANCH)]
    return pl.pallas_call(
        _merge_kernel,
        out_shape=jax.ShapeDtypeStruct((b, s, d), F32),
        grid=(b, s // tm),
        in_specs=[ospec, ospec, ospec] + gspecs + [
            pl.BlockSpec((N_BRANCH, W, d), lambda i, j: (0, 0, 0)),
            pl.BlockSpec((d, d), lambda i, j: (0, 0)),
            xspec,
            pl.BlockSpec((1, 6, d), lambda i, j: (i, 0, 0))],
        out_specs=xspec,
        compiler_params=_params("parallel", "parallel"),
        name="merge_project",
    )(o_nsa, o_sb, o_ml, y3, y3, y3, w_branch.astype(BF16), w_out.astype(BF16), x, mod)


def _ffn_kernel(x_ref, g_ref, mod_ref, wg_ref, wu_ref, wd_ref, o_ref, h_ref, acc_ref):
    f = pl.program_id(2)

    @pl.when(f == 0)
    def _():
        h_ref[...] = _norm_mod(x_ref[0], g_ref[...], mod_ref[0], 3, 4).astype(BF16)
        acc_ref[...] = jnp.zeros_like(acc_ref)

    h = h_ref[...]
    a = jnp.dot(h, wg_ref[...], preferred_element_type=F32)
    u = jnp.dot(h, wu_ref[...], preferred_element_type=F32)
    act = (a * jax.nn.sigmoid(a) * u).astype(BF16)
    acc_ref[...] += jnp.dot(act, wd_ref[...], preferred_element_type=F32)

    @pl.when(f == pl.num_programs(2) - 1)
    def _():
        o_ref[0] = x_ref[0] + mod_ref[0, 5:6, :] * acc_ref[...]


def dense_ffn(x, g, mod, wg, wu, wd, tm=512, n_ftiles=2):
    b, s, d = x.shape
    ff = wg.shape[1]
    tf = -(-ff // (n_ftiles * LANES)) * LANES
    pad = n_ftiles * tf - ff
    wg = jnp.pad(wg, ((0, 0), (0, pad))).astype(BF16)
    wu = jnp.pad(wu, ((0, 0), (0, pad))).astype(BF16)
    wd = jnp.pad(wd, ((0, pad), (0, 0))).astype(BF16)
    xspec = pl.BlockSpec((1, tm, d), lambda i, j, f: (i, j, 0))
    return pl.pallas_call(
        _ffn_kernel,
        out_shape=jax.ShapeDtypeStruct((b, s, d), F32),
        grid=(b, s // tm, n_ftiles),
        in_specs=[xspec,
                  pl.BlockSpec((1, d), lambda i, j, f: (0, 0)),
                  pl.BlockSpec((1, 6, d), lambda i, j, f: (i, 0, 0)),
                  pl.BlockSpec((d, tf), lambda i, j, f: (0, f)),
                  pl.BlockSpec((d, tf), lambda i, j, f: (0, f)),
                  pl.BlockSpec((tf, d), lambda i, j, f: (f, 0))],
        out_specs=xspec,
        scratch_shapes=[pltpu.VMEM((tm, d), BF16), pltpu.VMEM((tm, d), F32)],
        compiler_params=_params("parallel", "parallel", "arbitrary"),
        name="dense_ffn",
    )(x, g.reshape(1, d), mod, wg, wu, wd)


def _router_kernel(x_ref, g_ref, mod_ref, wr_ref, h_ref, e_ref, p_ref):
    h = _norm_mod(x_ref[0], g_ref[...], mod_ref[0], 3, 4)
    h_ref[...] = h
    lane = _iota((1, LANES), 1)
    real = lane < N_EXPERTS
    logits = jnp.where(real, jnp.dot(h, wr_ref[...], precision=HIGHEST, preferred_element_type=F32), NEG)
    e = jnp.exp(logits - jnp.max(logits, axis=1, keepdims=True))
    p = jnp.where(real, e / jnp.sum(e, axis=1, keepdims=True), -1.0)
    p1 = jnp.max(p, axis=1, keepdims=True)
    i1 = jnp.min(jnp.where(p == p1, lane, LANES), axis=1, keepdims=True)
    rest = jnp.where(lane == i1, -1.0, p)
    p2 = jnp.max(rest, axis=1, keepdims=True)
    i2 = jnp.min(jnp.where(rest == p2, lane, LANES), axis=1, keepdims=True)
    tot = p1 + p2
    e_ref[...] = jnp.where(lane == 0, i1, jnp.where(lane == 1, i2, 0))[:, 0:N_EXPERTS]
    p_ref[...] = jnp.where(lane == 0, p1 / tot, jnp.where(lane == 1, p2 / tot, 0.0))[:, 0:N_EXPERTS]


def moe_router(x, g, mod, w_router, tm=512):
    b, s, d = x.shape
    t = b * s
    spb = s // tm
    wr = jnp.pad(w_router, ((0, 0), (0, LANES - N_EXPERTS)))
    return pl.pallas_call(
        _router_kernel,
        out_shape=(jax.ShapeDtypeStruct((t, d), F32),
                   jax.ShapeDtypeStruct((t, N_EXPERTS), I32),
                   jax.ShapeDtypeStruct((t, N_EXPERTS), F32)),
        grid=(b, spb),
        in_specs=[pl.BlockSpec((1, tm, d), lambda i, j: (i, j, 0)),
                  pl.BlockSpec((1, d), lambda i, j: (0, 0)),
                  pl.BlockSpec((1, 6, d), lambda i, j: (i, 0, 0)),
                  pl.BlockSpec((d, LANES), lambda i, j: (0, 0))],
        out_specs=(pl.BlockSpec((tm, d), lambda i, j: (i * spb + j, 0)),
                   pl.BlockSpec((tm, N_EXPERTS), lambda i, j: (i * spb + j, 0)),
                   pl.BlockSpec((tm, N_EXPERTS), lambda i, j: (i * spb + j, 0))),
        compiler_params=_params("parallel", "parallel"),
        name="moe_router",
    )(x, g.reshape(1, d), mod, wr)


def _dispatch_kernel(dest_ref, h_ref, zero_hbm, xpad_hbm, sem, *, td):
    del zero_hbm
    base = pl.program_id(0) * td * TOP_K

    def row_copy(a):
        return pltpu.make_async_copy(h_ref.at[a // TOP_K], xpad_hbm.at[dest_ref[base + a]], sem)

    def issue(a, c):
        row_copy(a).start()
        return c

    def drain(a, c):
        row_copy(a).wait()
        return c

    lax.fori_loop(0, td * TOP_K, issue, 0)
    lax.fori_loop(0, td * TOP_K, drain, 0)


def moe_dispatch(h, dest, n_rows, td=256):
    t, d = h.shape
    return pl.pallas_call(
        functools.partial(_dispatch_kernel, td=td),
        out_shape=jax.ShapeDtypeStruct((n_rows, d), h.dtype),
        grid_spec=pltpu.PrefetchScalarGridSpec(
            num_scalar_prefetch=1,
            grid=(t // td,),
            in_specs=[pl.BlockSpec((td, d), lambda i, dr: (i, 0)), pl.BlockSpec(memory_space=pl.ANY)],
            out_specs=pl.BlockSpec(memory_space=pl.ANY),
            scratch_shapes=[pltpu.SemaphoreType.DMA(())]),
        input_output_aliases={2: 0},
        compiler_params=pltpu.CompilerParams(dimension_semantics=("arbitrary",), has_side_effects=True),
        name="moe_dispatch",
    )(dest, h, jnp.zeros((n_rows, d), h.dtype))


def _expert_kernel(be_ref, nu_ref, x_ref, wg_ref, wu_ref, wd_ref, o_ref, xb_ref, acc_ref):
    i = pl.program_id(0)
    f = pl.program_id(1)
    used = i < nu_ref[0]

    @pl.when(f == 0)
    def _():
        xb_ref[...] = x_ref[...].astype(BF16)
        acc_ref[...] = jnp.zeros_like(acc_ref)

    @pl.when(used)
    def _():
        xb = xb_ref[...]
        a = jnp.dot(xb, wg_ref[0], preferred_element_type=F32)
        u = jnp.dot(xb, wu_ref[0], preferred_element_type=F32)
        act = (a * jax.nn.sigmoid(a) * u).astype(BF16)
        acc_ref[...] += jnp.dot(act, wd_ref[0], preferred_element_type=F32)

    @pl.when(f == pl.num_programs(1) - 1)
    def _():
        o_ref[...] = acc_ref[...]


def moe_experts(x_pad, blk_expert, n_used, wg, wu, wd, tb, tf=896):
    p, d = x_pad.shape
    ff = wg.shape[2]
    return pl.pallas_call(
        _expert_kernel,
        out_shape=jax.ShapeDtypeStruct((p, d), F32),
        grid_spec=pltpu.PrefetchScalarGridSpec(
            num_scalar_prefetch=2,
            grid=(p // tb, ff // tf),
            in_specs=[pl.BlockSpec((tb, d), lambda i, f, be, nu: (i, 0)),
                      pl.BlockSpec((1, d, tf), lambda i, f, be, nu: (be[i], 0, f)),
                      pl.BlockSpec((1, d, tf), lambda i, f, be, nu: (be[i], 0, f)),
                      pl.BlockSpec((1, tf, d), lambda i, f, be, nu: (be[i], f, 0))],
            out_specs=pl.BlockSpec((tb, d), lambda i, f, be, nu: (i, 0)),
            scratch_shapes=[pltpu.VMEM((tb, d), BF16), pltpu.VMEM((tb, d), F32)]),
        compiler_params=_params("parallel", "arbitrary"),
        name="moe_experts",
    )(blk_expert, n_used, x_pad, wg, wu, wd)


def _combine_kernel(dest_ref, y_hbm, x_ref, p_ref, mod_ref, o_ref, buf0, buf1, sem, *, td, spb):
    tok0 = (pl.program_id(0) * spb + pl.program_id(1)) * td

    def row_copies(r):
        a = (tok0 + r) * TOP_K
        return (pltpu.make_async_copy(y_hbm.at[dest_ref[a]], buf0.at[r], sem),
                pltpu.make_async_copy(y_hbm.at[dest_ref[a + 1]], buf1.at[r], sem))

    def issue(r, c):
        c0, c1 = row_copies(r)
        c0.start()
        c1.start()
        return c

    def drain(r, c):
        c0, c1 = row_copies(r)
        c0.wait()
        c1.wait()
        return c

    lax.fori_loop(0, td, issue, 0)
    lax.fori_loop(0, td, drain, 0)
    w = p_ref[...]
    f = w[:, 0:1] * buf0[...] + w[:, 1:2] * buf1[...]
    o_ref[0] = x_ref[0] + mod_ref[0, 5:6, :] * f


def moe_combine(y, dest, x, top_p, mod, td=256):
    b, s, d = x.shape
    spb = s // td
    return pl.pallas_call(
        functools.partial(_combine_kernel, td=td, spb=spb),
        out_shape=jax.ShapeDtypeStruct((b, s, d), F32),
        grid_spec=pltpu.PrefetchScalarGridSpec(
            num_scalar_prefetch=1,
            grid=(b, spb),
            in_specs=[pl.BlockSpec(memory_space=pl.ANY),
                      pl.BlockSpec((1, td, d), lambda i, j, dr: (i, j, 0)),
                      pl.BlockSpec((td, N_EXPERTS), lambda i, j, dr: (i * spb + j, 0)),
                      pl.BlockSpec((1, 6, d), lambda i, j, dr: (i, 0, 0))],
            out_specs=pl.BlockSpec((1, td, d), lambda i, j, dr: (i, j, 0)),
            scratch_shapes=[pltpu.VMEM((td, d), F32), pltpu.VMEM((td, d), F32),
                            pltpu.SemaphoreType.DMA(())]),
        compiler_params=_params("arbitrary", "arbitrary"),
        name="moe_combine",
    )(dest, y, x, top_p, mod)


def moe_ffn(x, g, mod, w_router, wg, wu, wd, tb=512):
    b, s, d = x.shape
    t = b * s
    a = t * TOP_K
    h, top_e, top_p = moe_router(x, g, mod, w_router)
    e_flat = top_e[:, 0:TOP_K].reshape(a)
    onehot = (e_flat[:, None] == jnp.arange(N_EXPERTS, dtype=I32)[None, :]).astype(I32)
    csum = jnp.cumsum(onehot, axis=0)
    rank = jnp.sum(onehot * csum, axis=1) - 1
    counts = csum[-1]
    padded = (counts + tb - 1) // tb * tb
    pad_ends = jnp.cumsum(padded)
    pad_starts = pad_ends - padded
    dest = (jnp.sum(onehot * pad_starts[None, :], axis=1) + rank).astype(I32)
    n_rows = (a // tb + N_EXPERTS) * tb
    n_blk = n_rows // tb
    blk_expert = jnp.minimum(
        jnp.searchsorted(pad_ends, jnp.arange(n_blk, dtype=I32) * tb, side="right"), N_EXPERTS - 1).astype(I32)
    n_used = (pad_ends[-1:] // tb).astype(I32)
    x_pad = moe_dispatch(h, dest, n_rows)
    y = moe_experts(x_pad, blk_expert, n_used, wg.astype(BF16), wu.astype(BF16), wd.astype(BF16), tb)
    return moe_combine(y, dest, x, top_p, mod)


def _pack_w_in(w_in):
    kv = 2 * NSA_KV_HEADS * HEAD_DIM * 3
    o = 0
    nsa_q = w_in[:, o:o + MIX_WIDTH]; o += MIX_WIDTH
    nsa_kv = w_in[:, o:o + kv]; o += kv
    nsa_gate = w_in[:, o:o + NSA_HEADS * N_BRANCH]; o += NSA_HEADS * N_BRANCH
    sb = w_in[:, o:o + 3 * MIX_WIDTH]; o += 3 * MIX_WIDTH
    ml_qkv = w_in[:, o:o + 3 * MIX_WIDTH]; o += 3 * MIX_WIDTH
    ml_if = w_in[:, o:o + 2 * ML_HEADS]; o += 2 * ML_HEADS
    ml_o = w_in[:, o:o + MIX_WIDTH]; o += MIX_WIDTH
    merge = w_in[:, o:]
    main = jnp.concatenate([merge, nsa_q, ml_qkv, ml_o, sb, nsa_kv], axis=1).astype(BF16)
    small = jnp.concatenate([nsa_gate, ml_if], axis=1)
    small = jnp.pad(small, ((0, 0), (0, N_SMALL - small.shape[1]))).astype(BF16)
    return main, small


def token_mixer_layer(x, mod, norm_g, w_in, nsa_q_norm, nsa_k_norm, cmp_pos, cmp_w1, cmp_b1, cmp_w2,
                      cmp_b2, ml_conv_w, ml_conv_b, ml_gate_b, w_branch, w_out):
    b, s, d = x.shape
    t = b * s
    h = normmod(x, norm_g, mod, 0, 1).reshape(t, d)
    w_main, w_small = _pack_w_in(w_in)
    y3 = matmul(h, w_main, BF16, 512, N_MAIN // 2).reshape(b, s, N_MAIN)
    small3 = matmul(h, w_small, F32, 1024, N_SMALL).reshape(b, s, N_SMALL)
    o_sb = sb_attention(y3)
    o_ml = mlstm(y3, small3, ml_conv_w, ml_conv_b, ml_gate_b)
    normed = nsa_headnorm(y3, nsa_q_norm, nsa_k_norm)
    kc, vc = nsa_compress(y3, cmp_pos, cmp_w1, cmp_b1, cmp_w2, cmp_b2, nsa_k_norm[0])
    o_nsa = nsa_attention(y3, small3, normed, kc, vc)
    return merge_project(o_nsa, o_sb, o_ml, y3, w_branch, w_out, x, mod)


def kernel(x, c, ada_w, ada_b, norm_mix, norm_ffn, w_in, nsa_q_norm, nsa_k_norm, cmp_pos, cmp_w1, cmp_b1,
           cmp_w2, cmp_b2, ml_conv_w, ml_conv_b, ml_gate_b, w_branch, w_out, ffn_wg, ffn_wu, ffn_wd,
           moe_router, moe_wg, moe_wu, moe_wd):
    depth = ada_w.shape[0]
    b, s, d = x.shape
    mods = adaln(c, ada_w, ada_b).reshape(depth, b, 6, d)
    for layer in range(depth):
        mod = mods[layer]
        x = token_mixer_layer(x, mod, norm_mix[layer], w_in[layer], nsa_q_norm[layer], nsa_k_norm[layer],
                              cmp_pos[layer], cmp_w1[layer], cmp_b1[layer], cmp_w2[layer], cmp_b2[layer],
                              ml_conv_w[layer], ml_conv_b[layer], ml_gate_b[layer], w_branch[layer],
                              w_out[layer])
        j = layer // 2
        if layer % 2 == 0:
            x = dense_ffn(x, norm_ffn[layer], mod, ffn_wg[j], ffn_wu[j], ffn_wd[j])
        else:
            x = moe_ffn(x, norm_ffn[layer], mod, moe_router[j], moe_wg[j], moe_wu[j], moe_wd[j])
    return x
```

```python
import functools

import numpy as np
import jax
import jax.numpy as jnp
from jax import lax
from jax.experimental import pallas as pl
from jax.experimental.pallas import tpu as pltpu

F32 = jnp.float32
BF16 = jnp.bfloat16
I32 = jnp.int32
HIGHEST = lax.Precision.HIGHEST

EPS = 1e-6
NEG = -1e30
HEAD_DIM = 64
MIX_WIDTH = 512
NSA_HEADS = 8
NSA_KV_HEADS = 2
NSA_GROUP = NSA_HEADS // NSA_KV_HEADS
CMP_BLOCK = 32
CMP_STRIDE = 16
SEL_BLOCK = 64
SEL_TOPK = 16
WINDOW = 512
FORCE_BONUS = 1e4
ML_HEADS = 4
ML_HEAD_DIM = 128
ML_CHUNK = 64
CONV_WIDTH = 4
N_BRANCH = 3
N_EXPERTS = 8
TOP_K = 2
LANES = 128

C_MERGE = 0
C_NSA_Q = 3072
C_ML_Q = 3584
C_ML_K = 4096
C_ML_V = 4608
C_ML_O = 5120
C_SB_Q = 5632
C_SB_K = 6144
C_SB_V = 6656
C_NSA_KV = 7168
N_MAIN = 7936
S_NSA_GATE = 0
S_ML_I = 24
S_ML_F = 28
N_SMALL = 128

VMEM_LIMIT = 56 * 1024 * 1024


def _params(*sem):
    return pltpu.CompilerParams(dimension_semantics=sem, vmem_limit_bytes=VMEM_LIMIT)


def _iota(shape, dim):
    return lax.broadcasted_iota(I32, shape, dim)


def _split_dot(a32, b_bf16):
    hi = a32.astype(BF16)
    lo = (a32 - hi.astype(F32)).astype(BF16)
    return (jnp.dot(hi, b_bf16, preferred_element_type=F32)
            + jnp.dot(lo, b_bf16, preferred_element_type=F32))


def _split_dot_left(a_bf16, b32):
    hi = b32.astype(BF16)
    lo = (b32 - hi.astype(F32)).astype(BF16)
    return (jnp.dot(a_bf16, hi, preferred_element_type=F32)
            + jnp.dot(a_bf16, lo, preferred_element_type=F32))


def _dot_nt(a, b):
    return lax.dot_general(a, b, (((1,), (1,)), ((), ())), preferred_element_type=F32)


def _log_sigmoid(z):
    return jnp.minimum(z, 0.0) - jnp.log1p(jnp.exp(-jnp.abs(z)))


def _cast_kernel(x_ref, o_ref):
    o_ref[...] = x_ref[...].astype(o_ref.dtype)


def to_bf16(w, max_rows=512):
    cols = w.shape[-1]
    w2 = w.reshape(-1, cols)
    rows = w2.shape[0]
    tr = max(t for t in range(8, max_rows + 1, 8) if rows % t == 0)
    out = pl.pallas_call(
        _cast_kernel,
        out_shape=jax.ShapeDtypeStruct((rows, cols), BF16),
        grid=(rows // tr,),
        in_specs=[pl.BlockSpec((tr, cols), lambda i: (i, 0))],
        out_specs=pl.BlockSpec((tr, cols), lambda i: (i, 0)),
        compiler_params=_params("parallel"),
        name="to_bf16",
    )(w2)
    return out.reshape(w.shape)


def _adaln_kernel(c_ref, w_ref, b_ref, o_ref):
    c = c_ref[...]
    cond = c * jax.nn.sigmoid(c)
    o_ref[0] = jnp.dot(cond, w_ref[0], precision=HIGHEST, preferred_element_type=F32) + b_ref[0]


def adaln(c, ada_w, ada_b):
    depth, d, n = ada_w.shape
    b = c.shape[0]
    tn = 1536
    return pl.pallas_call(
        _adaln_kernel,
        out_shape=jax.ShapeDtypeStruct((depth, b, n), F32),
        grid=(depth, n // tn),
        in_specs=[pl.BlockSpec((b, d), lambda l, j: (0, 0)),
                  pl.BlockSpec((1, d, tn), lambda l, j: (l, 0, j)),
                  pl.BlockSpec((1, 1, tn), lambda l, j: (l, 0, j))],
        out_specs=pl.BlockSpec((1, b, tn), lambda l, j: (l, 0, j)),
        compiler_params=_params("parallel", "parallel"),
        name="adaln",
    )(c, ada_w, ada_b.reshape(depth, 1, n))


def _norm_mod(x, g, mod, shift_row, scale_row):
    ms = jnp.mean(x * x, axis=-1, keepdims=True)
    y = x * lax.rsqrt(ms + EPS) * g
    return y * (1.0 + mod[scale_row:scale_row + 1, :]) + mod[shift_row:shift_row + 1, :]


def _normmod_kernel(x_ref, g_ref, mod_ref, o_ref, *, shift_row, scale_row):
    o_ref[0] = _norm_mod(x_ref[0], g_ref[...], mod_ref[0], shift_row, scale_row).astype(o_ref.dtype)


def normmod(x, g, mod, shift_row, scale_row, ts=512):
    b, s, d = x.shape
    return pl.pallas_call(
        functools.partial(_normmod_kernel, shift_row=shift_row, scale_row=scale_row),
        out_shape=jax.ShapeDtypeStruct((b, s, d), BF16),
        grid=(b, s // ts),
        in_specs=[pl.BlockSpec((1, ts, d), lambda i, j: (i, j, 0)),
                  pl.BlockSpec((1, d), lambda i, j: (0, 0)),
                  pl.BlockSpec((1, 6, d), lambda i, j: (i, 0, 0))],
        out_specs=pl.BlockSpec((1, ts, d), lambda i, j: (i, j, 0)),
        compiler_params=_params("parallel", "parallel"),
        name="normmod",
    )(x, g.reshape(1, d), mod)


def _mm_kernel(a_ref, w_ref, o_ref):
    o_ref[...] = jnp.dot(a_ref[...], w_ref[...], preferred_element_type=F32).astype(o_ref.dtype)


def matmul(a, w, out_dtype, tm, tn):
    m, k = a.shape
    n = w.shape[1]
    return pl.pallas_call(
        _mm_kernel,
        out_shape=jax.ShapeDtypeStruct((m, n), out_dtype),
        grid=(m // tm, n // tn),
        in_specs=[pl.BlockSpec((tm, k), lambda i, j: (i, 0)),
                  pl.BlockSpec((k, tn), lambda i, j: (0, j))],
        out_specs=pl.BlockSpec((tm, tn), lambda i, j: (i, j)),
        compiler_params=_params("parallel", "parallel"),
        name="matmul",
    )(a, w)


SB_EXP_FLOOR = -104.0


def _sb_kernel(q_ref, k_ref, v_ref, o_ref, vt_ref, *, tq):
    TK, dh = LANES, HEAD_DIM
    n_sub = tq // TK
    W = 2 * tq
    qi = pl.program_id(2)
    q0 = qi * tq

    @pl.when(qi == 0)
    def _():
        for c in range(v_ref.shape[1] // tq):
            vt_ref[c] = v_ref[0, c * tq:(c + 1) * tq, :].astype(F32).T.astype(BF16)

    q_t = (q_ref[0].astype(F32) * (dh ** -0.5)).T
    chan = _iota((2 * dh, tq), 0)
    q_cat = jnp.concatenate([jnp.where(chan < dh, q_t, 0.0), jnp.where(chan < dh, 0.0, q_t)],
                            axis=1).astype(BF16)
    later = (_iota((TK, TK), 0) < _iota((TK, TK), 1)).astype(BF16)
    suffix = jnp.concatenate([jnp.concatenate([later, later], axis=1), jnp.ones((8, 2 * TK), BF16)], axis=0)

    def step(j, masked, st):
        carry, acc = st
        k0 = pl.multiple_of(j * tq, tq)
        z = jnp.dot(k_ref[0, pl.ds(k0, tq), :], q_cat, preferred_element_type=F32)
        lk = -(jnp.maximum(z, 0.0) + jnp.log(1.0 + jnp.exp(-jnp.abs(z))))
        ls = lk + z
        if masked:
            strict = (k0 + _iota((tq, W), 0)) < (q0 + (_iota((tq, W), 1) & (tq - 1)))
            lk = jnp.where(strict, lk, 0.0)
        hi = lk.astype(BF16)
        lo = (lk - hi.astype(F32)).astype(BF16)
        after = [None] * n_sub
        for sub in range(n_sub - 1, -1, -1):
            rows = slice(sub * TK, (sub + 1) * TK)
            res = jnp.dot(suffix, jnp.concatenate([hi[rows], lo[rows]], axis=0),
                          preferred_element_type=F32)
            after[sub] = res[0:TK] + carry
            carry = carry + res[TK:TK + 1]
        a = jnp.exp(ls + jnp.concatenate(after, axis=0))
        if masked:
            a = jnp.where(strict, a, 0.0)
        acc = acc + jnp.dot(vt_ref[j], a.astype(BF16), preferred_element_type=F32)
        return carry, acc

    def cond(st):
        return (st[0] >= 0) & (jnp.max(st[1]) > SB_EXP_FLOOR)

    def body(st):
        return (st[0] - 1,) + step(st[0], False, st[1:])

    st = step(qi, True, (jnp.zeros((1, W), F32), jnp.zeros((2 * dh, W), F32)))
    _, _, acc = lax.while_loop(cond, body, (qi - 1,) + st)
    o_ref[0] = jnp.concatenate([acc[0:dh, 0:tq], acc[dh:2 * dh, tq:W]], axis=0).T.astype(o_ref.dtype)


def sb_attention(y3, tq=256):
    b, s, _ = y3.shape
    n_pairs = MIX_WIDTH // LANES
    qb, kb, vb = C_SB_Q // LANES, C_SB_K // LANES, C_SB_V // LANES
    return pl.pallas_call(
        functools.partial(_sb_kernel, tq=tq),
        out_shape=jax.ShapeDtypeStruct((b, s, MIX_WIDTH), BF16),
        grid=(b, n_pairs, s // tq),
        in_specs=[pl.BlockSpec((1, tq, LANES), lambda i, p, j: (i, j, qb + p)),
                  pl.BlockSpec((1, s, LANES), lambda i, p, j: (i, 0, kb + p)),
                  pl.BlockSpec((1, s, LANES), lambda i, p, j: (i, 0, vb + p))],
        out_specs=pl.BlockSpec((1, tq, LANES), lambda i, p, j: (i, j, p)),
        scratch_shapes=[pltpu.VMEM((s // tq, LANES, tq), BF16)],
        compiler_params=_params("parallel", "parallel", "arbitrary"),
        name="sb_attention",
    )(y3, y3, y3)


def _mlstm_kernel(q_ref, k_ref, v_ref, og_ref, sm_ref, gr_ref, cw_ref, cb_ref, gb_ref, out_ref,
                  ct_ref, n_ref, m_ref, xbuf_ref, qk_ref, *, ts):
    L, dh, H, W = ML_CHUNK, ML_HEAD_DIM, ML_HEADS, MIX_WIDTH
    halo = 8
    sblk = pl.program_id(1)

    @pl.when(sblk == 0)
    def _():
        ct_ref[...] = jnp.zeros_like(ct_ref)
        n_ref[...] = jnp.zeros_like(n_ref)
        m_ref[...] = jnp.zeros_like(m_ref)
        xbuf_ref[0:halo, :] = jnp.zeros((halo, 2 * W), F32)

    @pl.when(sblk > 0)
    def _():
        xbuf_ref[0:halo, :] = xbuf_ref[ts:ts + halo, :]

    xbuf_ref[halo:halo + ts, 0:W] = q_ref[0].astype(F32)
    xbuf_ref[halo:halo + ts, W:2 * W] = k_ref[0].astype(F32)
    conv = cb_ref[...] + jnp.zeros((ts, 2 * W), F32)
    for j in range(CONV_WIDTH):
        off = halo - (CONV_WIDTH - 1) + j
        conv = conv + cw_ref[j:j + 1, :] * xbuf_ref[off:off + ts, :]
    act = conv * jax.nn.sigmoid(conv)
    qk_ref[:, 0:W] = (act[:, 0:W] * (dh ** -0.5)).astype(BF16)
    qk_ref[:, W:2 * W] = act[:, W:2 * W].astype(BF16)

    it0, it1 = _iota((L, L), 0), _iota((L, L), 1)
    causal = it0 >= it1
    tri_lo = causal.astype(BF16)
    tri_up = (it0 <= it1).astype(BF16)

    def chunk(c, carry):
        r0 = pl.multiple_of(c * L, L)
        sm = sm_ref[0, pl.ds(r0, L), :]
        gr = gr_ref[0, c]
        HR = range(H)
        rows = pl.ds(r0, L)
        cols = [slice(h * dh, (h + 1) * dh) for h in HR]
        ig_col = [sm[:, S_ML_I + h:S_ML_I + h + 1] + gb_ref[0, h] for h in HR]
        lf_col = [_log_sigmoid(sm[:, S_ML_F + h:S_ML_F + h + 1] + gb_ref[1, h]) for h in HR]
        ig_row = [gr[h:h + 1, :] + gb_ref[0, h] for h in HR]
        lf_row = [_log_sigmoid(gr[H + h:H + h + 1, :] + gb_ref[1, h]) for h in HR]
        b_t = [_split_dot_left(tri_lo, jnp.broadcast_to(lf_col[h], (L, L))) for h in HR]
        b_s = [_split_dot(jnp.broadcast_to(lf_row[h], (L, L)), tri_up) for h in HR]
        qq = [qk_ref[rows, cols[h]] for h in HR]
        kk = [qk_ref[rows, W + h * dh:W + (h + 1) * dh] for h in HR]
        vv = [v_ref[0, rows, cols[h]] for h in HR]
        ct = [ct_ref[h] for h in HR]
        nvec = [n_ref[h] for h in HR]
        m_prev = [m_ref[h][:, 0:1] for h in HR]
        qk = [_dot_nt(qq[h], kk[h]) for h in HR]
        q_c = [jnp.dot(qq[h], ct[h].astype(BF16), preferred_element_type=F32) for h in HR]
        kt = [kk[h].astype(F32).T.astype(BF16) for h in HR]
        dmat = [jnp.where(causal, b_t[h] - b_s[h] + ig_row[h], NEG) for h in HR]
        b_col = [b_t[h][:, 0:1] for h in HR]
        m_inter = [b_col[h] + m_prev[h] for h in HR]
        m_t = [jnp.maximum(m_inter[h], jnp.max(dmat[h], axis=1, keepdims=True)) for h in HR]
        w = [jnp.exp(dmat[h] - m_t[h]) * qk[h] for h in HR]
        inter = [jnp.exp(m_inter[h] - m_t[h]) for h in HR]
        w_v = [jnp.dot(w[h].astype(BF16), vv[h], preferred_element_type=F32) for h in HR]
        b_last = [b_t[h][L - 1:L, 0:1] for h in HR]
        decay = [b_last[h] - b_col[h] + ig_col[h] for h in HR]
        m_new = [jnp.maximum(b_last[h] + m_prev[h], jnp.max(decay[h], axis=0, keepdims=True)) for h in HR]
        ws = [jnp.exp(decay[h] - m_new[h]) for h in HR]
        cscale = [jnp.exp(b_last[h] + m_prev[h] - m_new[h]) for h in HR]
        wv = [(ws[h] * vv[h].astype(F32)).astype(BF16) for h in HR]
        k_wv = [jnp.dot(kt[h], wv[h], preferred_element_type=F32) for h in HR]
        for h in HR:
            num = inter[h] * q_c[h] + w_v[h]
            den = (inter[h] * jnp.sum(qq[h].astype(F32) * nvec[h], axis=1, keepdims=True)
                   + jnp.sum(w[h], axis=1, keepdims=True))
            hval = num / jnp.maximum(jnp.abs(den), jnp.exp(-m_t[h]))
            ct_ref[h] = cscale[h] * ct[h] + k_wv[h]
            n_ref[h] = cscale[h] * nvec[h] + jnp.sum(ws[h] * kk[h].astype(F32), axis=0, keepdims=True)
            m_ref[h] = jnp.broadcast_to(m_new[h], (1, LANES))
            gate = jax.nn.sigmoid(og_ref[0, rows, cols[h]].astype(F32))
            out_ref[0, rows, cols[h]] = (gate * hval).astype(out_ref.dtype)
        return carry

    lax.fori_loop(0, ts // L, chunk, 0)


def mlstm(y3, small3, conv_w, conv_b, gate_b, ts=512):
    b, s, _ = y3.shape
    W, H, L = MIX_WIDTH, ML_HEADS, ML_CHUNK
    gr = small3[:, :, S_ML_I:S_ML_I + 2 * H].reshape(b, s // L, L, 2 * H).transpose(0, 1, 3, 2)
    cq, ck, cv, co = C_ML_Q // W, C_ML_K // W, C_ML_V // W, C_ML_O // W
    return pl.pallas_call(
        functools.partial(_mlstm_kernel, ts=ts),
        out_shape=jax.ShapeDtypeStruct((b, s, W), BF16),
        grid=(b, s // ts),
        in_specs=[pl.BlockSpec((1, ts, W), lambda i, j: (i, j, cq)),
                  pl.BlockSpec((1, ts, W), lambda i, j: (i, j, ck)),
                  pl.BlockSpec((1, ts, W), lambda i, j: (i, j, cv)),
                  pl.BlockSpec((1, ts, W), lambda i, j: (i, j, co)),
                  pl.BlockSpec((1, ts, N_SMALL), lambda i, j: (i, j, 0)),
                  pl.BlockSpec((1, ts // L, 2 * H, L), lambda i, j: (i, j, 0, 0)),
                  pl.BlockSpec((CONV_WIDTH, 2 * W), lambda i, j: (0, 0)),
                  pl.BlockSpec((1, 2 * W), lambda i, j: (0, 0)),
                  pl.BlockSpec(memory_space=pltpu.SMEM)],
        out_specs=pl.BlockSpec((1, ts, W), lambda i, j: (i, j, 0)),
        scratch_shapes=[pltpu.VMEM((H, ML_HEAD_DIM, ML_HEAD_DIM), F32),
                        pltpu.VMEM((H, 1, ML_HEAD_DIM), F32),
                        pltpu.VMEM((H, 1, LANES), F32),
                        pltpu.VMEM((ts + 8, 2 * W), F32),
                        pltpu.VMEM((ts, 2 * W), BF16)],
        compiler_params=_params("parallel", "arbitrary"),
        name="mlstm",
    )(y3, y3, y3, y3, small3, gr, conv_w, conv_b.reshape(1, 2 * W), gate_b)


def _headnorm_kernel(x_ref, g_ref, o_ref):
    x = x_ref[0].astype(F32)
    same_head = (_iota((LANES, LANES), 0) // HEAD_DIM == _iota((LANES, LANES), 1) // HEAD_DIM)
    ss = _split_dot(x * x, same_head.astype(BF16))
    o_ref[0] = (x * lax.rsqrt(ss * (1.0 / HEAD_DIM) + EPS) * g_ref[0]).astype(o_ref.dtype)


def nsa_headnorm(y3, q_norm, k_norm, ts=1024):
    b, s, _ = y3.shape
    qb = C_NSA_Q // LANES
    ksb = C_NSA_KV // LANES + 2
    kwb = C_NSA_KV // LANES + 4
    gains = jnp.stack([jnp.tile(q_norm, 2)] * 4 + [jnp.tile(k_norm[1], 2), jnp.tile(k_norm[2], 2)])

    def col(j):
        return jnp.where(j < 4, qb + j, jnp.where(j == 4, ksb, kwb))

    return pl.pallas_call(
        _headnorm_kernel,
        out_shape=jax.ShapeDtypeStruct((b, s, 6 * LANES), BF16),
        grid=(b, s // ts, 6),
        in_specs=[pl.BlockSpec((1, ts, LANES), lambda i, t, j: (i, t, col(j))),
                  pl.BlockSpec((1, 1, LANES), lambda i, t, j: (j, 0, 0))],
        out_specs=pl.BlockSpec((1, ts, LANES), lambda i, t, j: (i, t, j)),
        compiler_params=_params("parallel", "parallel", "parallel"),
        name="nsa_headnorm",
    )(y3, gains.reshape(6, 1, LANES))


def _gelu_tanh(x):
    return 0.5 * x * (1.0 + jnp.tanh(0.7978845608028654 * (x + 0.044715 * (x * x * x))))


def _compress_kernel(ra_ref, rb_ref, pos_ref, w1_ref, b1_ref, w2_ref, b2_ref, kn_ref, kc_ref, vc_ref):
    half = (CMP_BLOCK // 2) * HEAD_DIM
    for j, o_ref in enumerate((kc_ref, vc_ref)):
        xa = (ra_ref[j, 0, 0].astype(F32) + pos_ref[j, :, 0:half]).astype(BF16)
        xb = (rb_ref[j, 0, 0].astype(F32) + pos_ref[j, :, half:2 * half]).astype(BF16)
        hid = (jnp.dot(xa, w1_ref[j, 0:half, :], preferred_element_type=F32)
               + jnp.dot(xb, w1_ref[j, half:2 * half, :], preferred_element_type=F32) + b1_ref[j])
        out = jnp.dot(_gelu_tanh(hid).astype(BF16), w2_ref[j], preferred_element_type=F32) + b2_ref[j]
        if j == 0:
            out = out * lax.rsqrt(jnp.mean(out * out, axis=-1, keepdims=True) + EPS) * kn_ref[...]
        o_ref[0, 0] = out


def nsa_compress(y3, cmp_pos, cmp_w1, cmp_b1, cmp_w2, cmp_b2, k_norm0):
    b, s, _ = y3.shape
    G, dh = NSA_KV_HEADS, HEAD_DIM
    nr = s // CMP_STRIDE
    wide = CMP_STRIDE * dh
    kv = y3[:, :, C_NSA_KV:C_NSA_KV + 2 * G * dh].reshape(b, s, 2, G, dh)
    ra = kv.transpose(2, 0, 3, 1, 4).reshape(2, b, G, nr, wide)
    rb = jnp.concatenate([ra[:, :, :, 1:], jnp.zeros((2, b, G, 1, wide), ra.dtype)], axis=3)
    hidden = cmp_w1.shape[-1]
    out = jax.ShapeDtypeStruct((b, G, nr, dh), F32)
    blk = pl.BlockSpec((2, 1, 1, nr, wide), lambda i, g: (0, i, g, 0, 0))
    oblk = pl.BlockSpec((1, 1, nr, dh), lambda i, g: (i, g, 0, 0))

    def full(shape):
        return pl.BlockSpec(shape, lambda i, g: (0,) * len(shape))

    return pl.pallas_call(
        _compress_kernel,
        out_shape=(out, out),
        grid=(b, G),
        in_specs=[blk, blk, full((2, 1, 2 * wide)), full((2, 2 * wide, hidden)), full((2, 1, hidden)),
                  full((2, hidden, dh)), full((2, 1, dh)), full((1, dh))],
        out_specs=(oblk, oblk),
        compiler_params=_params("parallel", "parallel"),
        name="nsa_compress",
    )(ra, rb, cmp_pos.reshape(2, 1, 2 * wide), cmp_w1.astype(BF16), cmp_b1.reshape(2, 1, hidden),
      cmp_w2.astype(BF16), cmp_b2.reshape(2, 1, dh), k_norm0.reshape(1, dh))


def _nsa_kernel_rowmajor(q_ref, gl_ref, kc_ref, vc_ref, ks_ref, vs_ref, kw_ref, vw_ref, o_ref, *, n_sel, top):
    QB, R, dh = 128, NSA_GROUP, HEAD_DIM
    rows = R * QB
    g = pl.program_id(1)
    qi = pl.program_id(2)
    q0 = qi * QB
    q = q_ref[0, 0, 0] * jnp.asarray(dh ** -0.5, BF16)
    row = _iota((rows, 1), 0)
    t_f = (q0 + (row & (QB - 1))).astype(F32)
    head = g * R + (row >> 7)
    slope = jnp.exp2(-(head + 1).astype(F32))

    n_cmp = kc_ref.shape[2]
    kc = kc_ref[0, 0]
    kc_hi = kc.astype(BF16)
    kc_lo = (kc - kc_hi.astype(F32)).astype(BF16)
    cmp_end = (_iota((1, n_cmp), 1) * CMP_STRIDE + (CMP_BLOCK - 1)).astype(F32)
    dist = t_f - cmp_end
    valid = dist >= 0.0
    s = jnp.where(valid, _dot_nt(q, kc_hi) + _dot_nt(q, kc_lo) - slope * dist, NEG)
    e = jnp.exp(s - jnp.max(s, axis=1, keepdims=True))
    p = jnp.where(valid, e / jnp.sum(e, axis=1, keepdims=True), 0.0)
    o_cmp = jnp.dot(p.astype(BF16), vc_ref[0, 0].astype(BF16), preferred_element_type=F32)

    p_grp = p[0:QB] + p[QB:2 * QB] + p[2 * QB:3 * QB] + p[3 * QB:4 * QB]
    c0 = _iota((n_cmp, n_sel), 0) * CMP_STRIDE
    s0 = _iota((n_cmp, n_sel), 1) * SEL_BLOCK
    overlap = ((c0 < s0 + SEL_BLOCK) & (c0 + CMP_BLOCK > s0)).astype(BF16)
    imp = _split_dot(p_grp, overlap)
    tq = q0 + _iota((QB, 1), 0)
    j_idx = _iota((1, n_sel), 1)
    cur = tq >> 6
    forced = (j_idx == 0) | (j_idx == cur) | (j_idx == cur - 1)
    imp = jnp.where(j_idx * SEL_BLOCK <= tq, imp + jnp.where(forced, FORCE_BONUS, 0.0), -1.0)
    sel = jnp.zeros((QB, n_sel), F32)
    for _ in range(top):
        mx = jnp.max(imp, axis=1, keepdims=True)
        first = jnp.min(jnp.where(imp == mx, j_idx, n_sel), axis=1, keepdims=True)
        pick = j_idx == first
        sel = jnp.where(pick, 1.0, sel)
        imp = jnp.where(pick, -3e38, imp)
    sel_bf = sel.astype(BF16)

    def flash(k_ref, v_ref, lo, hi, mask_fn):
        def body(kb, st):
            m_run, l_run, acc = st
            k0 = pl.multiple_of(kb * QB, QB)
            kblk = k_ref[0, 0, pl.ds(k0, QB), :]
            vblk = v_ref[0, 0, pl.ds(k0, QB), :]
            dist = t_f - (k0 + _iota((1, QB), 1)).astype(F32)
            ok = mask_fn(kb, dist)
            sc = jnp.where(ok, _dot_nt(q, kblk) - slope * dist, NEG)
            m_new = jnp.maximum(m_run, jnp.max(sc, axis=1, keepdims=True))
            alpha = jnp.exp(m_run - m_new)
            pr = jnp.where(ok, jnp.exp(sc - m_new), 0.0)
            l_run = alpha * l_run + jnp.sum(pr, axis=1, keepdims=True)
            acc = alpha * acc + jnp.dot(pr.astype(BF16), vblk, preferred_element_type=F32)
            return m_new, l_run, acc

        init = (jnp.full((rows, 1), NEG, F32), jnp.zeros((rows, 1), F32), jnp.zeros((rows, dh), F32))
        _, l_run, acc = lax.fori_loop(lo, hi, body, init)
        return acc / l_run

    def sel_mask(kb, dist):
        expand = (_iota((n_sel, QB), 0) == 2 * kb + (_iota((n_sel, QB), 1) >> 6)).astype(BF16)
        m = jnp.dot(sel_bf, expand, preferred_element_type=F32)
        m = jnp.concatenate([m] * R, axis=0)
        return (m > 0.5) & (dist >= 0.0)

    def win_mask(kb, dist):
        return (dist >= 0.0) & (dist < float(WINDOW))

    o_sel = flash(ks_ref, vs_ref, 0, qi + 1, sel_mask)
    o_win = flash(kw_ref, vw_ref, jnp.maximum(qi - WINDOW // QB, 0), qi + 1, win_mask)

    gate = jax.nn.sigmoid(gl_ref[0, 0, 0])
    o_ref[0, 0, 0] = (gate[:, 0:1] * o_cmp + gate[:, 1:2] * o_sel + gate[:, 2:3] * o_win).astype(o_ref.dtype)


def nsa_attention_rowmajor(y3, small3, normed, kc, vc):
    b, s, _ = y3.shape
    G, R, dh, QB = NSA_KV_HEADS, NSA_GROUP, HEAD_DIM, 128
    nq = s // QB
    n_sel = s // SEL_BLOCK
    top = min(SEL_TOPK, n_sel)

    def stack_heads(a, width):
        return (a.reshape(b, nq, QB, G, R, width).transpose(0, 3, 1, 4, 2, 5)
                .reshape(b, G, nq, R * QB, width))

    def kv_heads(a):
        return a.reshape(b, s, G, dh).transpose(0, 2, 1, 3)

    q = stack_heads(normed[:, :, 0:MIX_WIDTH], dh)
    gl = stack_heads(small3[:, :, S_NSA_GATE:S_NSA_GATE + NSA_HEADS * N_BRANCH], N_BRANCH)
    ks = kv_heads(normed[:, :, 4 * LANES:5 * LANES])
    kw = kv_heads(normed[:, :, 5 * LANES:6 * LANES])
    vs = kv_heads(y3[:, :, C_NSA_KV + 3 * LANES:C_NSA_KV + 4 * LANES])
    vw = kv_heads(y3[:, :, C_NSA_KV + 5 * LANES:C_NSA_KV + 6 * LANES])
    n_cmp = kc.shape[2]
    qspec = pl.BlockSpec((1, 1, 1, R * QB, dh), lambda i, g, j: (i, g, j, 0, 0))
    gspec = pl.BlockSpec((1, 1, 1, R * QB, N_BRANCH), lambda i, g, j: (i, g, j, 0, 0))
    cspec = pl.BlockSpec((1, 1, n_cmp, dh), lambda i, g, j: (i, g, 0, 0))
    kvspec = pl.BlockSpec((1, 1, s, dh), lambda i, g, j: (i, g, 0, 0))
    out = pl.pallas_call(
        functools.partial(_nsa_kernel_rowmajor, n_sel=n_sel, top=top),
        out_shape=jax.ShapeDtypeStruct((b, G, nq, R * QB, dh), BF16),
        grid=(b, G, nq),
        in_specs=[qspec, gspec, cspec, cspec, kvspec, kvspec, kvspec, kvspec],
        out_specs=qspec,
        compiler_params=_params("parallel", "parallel", "parallel"),
        name="nsa_attention",
    )(q, gl, kc, vc, ks, vs, kw, vw)
    return (out.reshape(b, G, nq, R, QB, dh).transpose(0, 2, 4, 1, 3, 5).reshape(b, s, MIX_WIDTH))


NSA_QB = 128
NSA_KS = 512
NSA_CK = 256
NSA_VR = 80
A_FEAT, A_PEN, A_BIAS = 0, 64, 128


def _nsa_t_kernel(qn_ref, gl_ref, kca_ref, vcT_ref, ksn_ref, kwn_ref, vs_ref, vw_ref, kconst_ref, o_ref,
                  qt_ref, ksa_ref, kwa_ref, vsa_ref, vwa_ref, *, n_sel, top):
    QB, R, dh, CK, VR, KS = NSA_QB, NSA_GROUP, HEAD_DIM, NSA_CK, NSA_VR, NSA_KS
    HQ = R * QB
    g = pl.program_id(1)
    qi = pl.program_id(2)
    q0 = qi * QB
    nkb = vsa_ref.shape[0]

    @pl.when(qi == 0)
    def _():
        ksa_ref[...] = kconst_ref[...]
        kwa_ref[...] = kconst_ref[...]
        kwa_ref[:, A_PEN:A_PEN + 64] = jnp.zeros((kwa_ref.shape[0], 64), BF16)
        for gg in range(NSA_KV_HEADS):
            @pl.when(g == gg)
            def _():
                heads = slice(gg * dh, (gg + 1) * dh)
                ksa_ref[:, A_FEAT:A_FEAT + dh] = ksn_ref[0, :, heads]
                kwa_ref[:, A_FEAT:A_FEAT + dh] = kwn_ref[0, :, heads]
                for c in range(nkb):
                    keys = slice(c * QB, (c + 1) * QB)
                    vsa_ref[c, 0:dh, :] = vs_ref[0, keys, :].astype(F32).T[heads].astype(BF16)
                    vwa_ref[c, 0:dh, :] = vw_ref[0, keys, :].astype(F32).T[heads].astype(BF16)
        ones_rows = jnp.where(_iota((nkb, VR - dh, QB), 1) == 0, 1.0, 0.0).astype(BF16)
        vsa_ref[:, dh:VR, :] = ones_rows
        vwa_ref[:, dh:VR, :] = ones_rows
        qt_ref[A_BIAS + 16:CK, :] = jnp.zeros((CK - A_BIAS - 16, HQ), BF16)

    q_rows = (qn_ref[0].astype(F32) * (dh ** -0.5)).T
    qT = jnp.concatenate([q_rows[r * dh:(r + 1) * dh] for r in range(R)], axis=1).astype(BF16)
    qt_ref[A_FEAT:A_FEAT + dh, :] = qT
    qt_ref[A_PEN:A_PEN + dh, :] = qT
    lane = _iota((16, HQ), 1)
    rowi = _iota((16, HQ), 0)
    t_q = q0 + (lane & (QB - 1))
    slope = jnp.exp2(-(g * R + (lane >> 7) + 1).astype(F32))
    t_hi = ((t_q >> 6) << 6).astype(F32)
    t_lo = (t_q & 63).astype(F32)
    bias_rows = jnp.where(rowi < 2, slope,
                          jnp.where(rowi == 2, -slope * t_hi, jnp.where(rowi == 3, -slope * t_lo, 0.0)))
    qt_ref[A_BIAS:A_BIAS + 16, :] = bias_rows.astype(BF16)

    n_cmp = kca_ref.shape[2]
    sc = jnp.dot(kca_ref[0, 0], qt_ref[...], preferred_element_type=F32)
    cmp_end = _iota((n_cmp, HQ), 0) * CMP_STRIDE + (CMP_BLOCK - 1)
    valid = cmp_end <= q0 + (_iota((n_cmp, HQ), 1) & (QB - 1))
    sc = jnp.where(valid, sc, NEG)
    e = jnp.exp(sc - jnp.max(sc, axis=0, keepdims=True))
    p = jnp.where(valid, e / jnp.sum(e, axis=0, keepdims=True), 0.0)
    o_cmp = jnp.dot(vcT_ref[0, 0], p.astype(BF16), preferred_element_type=F32)

    p_grp = p[:, 0:QB] + p[:, QB:2 * QB] + p[:, 2 * QB:3 * QB] + p[:, 3 * QB:4 * QB]
    c0 = _iota((n_sel, n_cmp), 1) * CMP_STRIDE
    s0 = _iota((n_sel, n_cmp), 0) * SEL_BLOCK
    overlap_t = ((c0 < s0 + SEL_BLOCK) & (c0 + CMP_BLOCK > s0)).astype(BF16)
    imp = _split_dot_left(overlap_t, p_grp)
    j_idx = _iota((n_sel, QB), 0)
    tq = q0 + _iota((n_sel, QB), 1)
    cur = tq >> 6
    forced = (j_idx == 0) | (j_idx == cur) | (j_idx == cur - 1)
    imp = jnp.where(j_idx * SEL_BLOCK <= tq, imp + jnp.where(forced, FORCE_BONUS, 0.0), -1.0)
    sel = jnp.zeros((n_sel, QB), F32)
    for _ in range(top):
        mx = jnp.max(imp, axis=0, keepdims=True)
        first = jnp.min(jnp.where(imp == mx, j_idx, n_sel), axis=0, keepdims=True)
        pick = j_idx == first
        sel = jnp.where(pick, 1.0, sel)
        imp = jnp.where(pick, -3e38, imp)
    pen = jnp.where(sel > 0.5, 0.0, NEG)
    if n_sel < 64:
        pen = jnp.concatenate([pen, jnp.zeros((64 - n_sel, QB), F32)], axis=0)
    qt_ref[A_PEN:A_PEN + 64, :] = jnp.concatenate([pen] * R, axis=1).astype(BF16)

    def attend(kaug_ref, vaug_ref, k0, nk, st, mode):
        k0 = pl.multiple_of(k0, QB)
        s = jnp.dot(kaug_ref[pl.ds(k0, nk), :], qt_ref[...], preferred_element_type=F32)
        if mode != "full":
            dist = (q0 + (_iota((nk, HQ), 1) & (QB - 1))) - (k0 + _iota((nk, HQ), 0))
            ok = dist >= 0
            if mode == "window":
                ok = ok & (dist < WINDOW)
            s = jnp.where(ok, s, NEG)
        m_run, acc = st
        m_new = jnp.maximum(m_run, jnp.max(s, axis=0, keepdims=True))
        pr = jnp.exp(s - m_new).astype(BF16)
        acc = jnp.exp(m_run - m_new) * acc
        kb0 = k0 // QB
        for i in range(nk // QB):
            acc = acc + jnp.dot(vaug_ref[kb0 + i], pr[i * QB:(i + 1) * QB], preferred_element_type=F32)
        return m_new, acc

    def finish(st):
        return st[1][0:dh] / st[1][dh:dh + 1]

    init = (jnp.full((1, HQ), NEG, F32), jnp.zeros((VR, HQ), F32))
    n_full = qi // (KS // QB)
    st = lax.fori_loop(0, n_full, lambda j, s_: attend(ksa_ref, vsa_ref, j * KS, KS, s_, "full"), init)
    o_sel = finish(attend(ksa_ref, vsa_ref, n_full * KS, KS, st, "causal"))
    o_win = finish(attend(kwa_ref, vwa_ref, jnp.maximum(q0 - WINDOW, 0), WINDOW + QB, init, "window"))

    gl_t = gl_ref[0].T
    gate = []
    for br in range(N_BRANCH):
        per_kv = [jnp.concatenate([gl_t[(gg * R + r) * N_BRANCH + br:(gg * R + r) * N_BRANCH + br + 1]
                                   for r in range(R)], axis=1) for gg in range(NSA_KV_HEADS)]
        gate.append(jax.nn.sigmoid(jnp.where(g == 0, per_kv[0], per_kv[1])))
    o_t = gate[0] * o_cmp + gate[1] * o_sel + gate[2] * o_win
    o_ref[0] = jnp.concatenate([o_t[:, r * QB:(r + 1) * QB] for r in range(R)], axis=0).T.astype(o_ref.dtype)


def nsa_attention(y3, small3, normed, kc, vc):
    b, s, _ = y3.shape
    G, R, dh, QB, CK, VR = NSA_KV_HEADS, NSA_GROUP, HEAD_DIM, NSA_QB, NSA_CK, NSA_VR
    assert G == 2, "the kernel picks a kv head's gate rows with a two-way select"
    HQ = R * QB
    nq = s // QB
    n_sel = s // SEL_BLOCK
    assert n_sel <= 64, "selection one-hot columns hold at most 64 blocks"
    top = min(SEL_TOPK, n_sel)
    n_cmp = kc.shape[2]

    def pos_cols(pos):
        return np.stack([pos // 64 * 64, pos % 64, np.ones_like(pos), np.ones_like(pos)], axis=1)

    vc_t = vc.transpose(0, 1, 3, 2).astype(BF16)

    pos = np.arange(s)
    kconst = np.zeros((s, CK), np.float32)
    kconst[pos, A_PEN + pos // SEL_BLOCK] = 1.0
    kconst[:, A_BIAS:A_BIAS + 4] = pos_cols(pos)
    kconst = jnp.asarray(kconst, BF16)

    kc_hi = kc.astype(BF16)
    kc_lo = (kc - kc_hi.astype(F32)).astype(BF16)
    cend = np.arange(n_cmp) * CMP_STRIDE + (CMP_BLOCK - 1)
    cbias = np.zeros((n_cmp, CK - 2 * dh), np.float32)
    cbias[:, 0:4] = pos_cols(cend)
    kc_aug = jnp.concatenate([kc_hi, kc_lo, jnp.broadcast_to(jnp.asarray(cbias, BF16), (b, G, n_cmp, CK - 2 * dh))],
                             axis=-1)

    qspec = pl.BlockSpec((1, QB, R * dh), lambda i, g, j: (i, j, g))
    vsb = C_NSA_KV // LANES + 3
    vwb = C_NSA_KV // LANES + 5
    return pl.pallas_call(
        functools.partial(_nsa_t_kernel, n_sel=n_sel, top=top),
        out_shape=jax.ShapeDtypeStruct((b, s, MIX_WIDTH), BF16),
        grid=(b, G, nq),
        in_specs=[qspec,
                  pl.BlockSpec((1, QB, N_SMALL), lambda i, g, j: (i, j, 0)),
                  pl.BlockSpec((1, 1, n_cmp, CK), lambda i, g, j: (i, g, 0, 0)),
                  pl.BlockSpec((1, 1, dh, n_cmp), lambda i, g, j: (i, g, 0, 0)),
                  pl.BlockSpec((1, s, LANES), lambda i, g, j: (i, 0, 4)),
                  pl.BlockSpec((1, s, LANES), lambda i, g, j: (i, 0, 5)),
                  pl.BlockSpec((1, s, LANES), lambda i, g, j: (i, 0, vsb)),
                  pl.BlockSpec((1, s, LANES), lambda i, g, j: (i, 0, vwb)),
                  pl.BlockSpec((s, CK), lambda i, g, j: (0, 0))],
        out_specs=qspec,
        scratch_shapes=[pltpu.VMEM((CK, HQ), BF16),
                        pltpu.VMEM((s, CK), BF16), pltpu.VMEM((s, CK), BF16),
                        pltpu.VMEM((nq, VR, QB), BF16), pltpu.VMEM((nq, VR, QB), BF16)],
        compiler_params=_params("parallel", "parallel", "arbitrary"),
        name="nsa_attention",
    )(normed, small3, kc_aug, vc_t, normed, normed, y3, y3, kconst)


def _merge_kernel(on_ref, os_ref, om_ref, g0_ref, g1_ref, g2_ref, wb_ref, wo_ref, x_ref, mod_ref, o_ref):
    merged = None
    for i, (o_r, g_r) in enumerate(((on_ref, g0_ref), (os_ref, g1_ref), (om_ref, g2_ref))):
        br = jnp.dot(o_r[0], wb_ref[i], preferred_element_type=F32)
        term = jax.nn.sigmoid(g_r[0].astype(F32)) * br
        merged = term if merged is None else merged + term
    out = jnp.dot(merged.astype(BF16), wo_ref[...], preferred_element_type=F32)
    o_ref[0] = x_ref[0] + mod_ref[0, 2:3, :] * out


def merge_project(o_nsa, o_sb, o_ml, y3, w_branch, w_out, x, mod, tm=512):
    b, s, d = x.shape
    W = MIX_WIDTH
    ospec = pl.BlockSpec((1, tm, W), lambda i, j: (i, j, 0))
    xspec = pl.BlockSpec((1, tm, d), lambda i, j: (i, j, 0))
    gspecs = [pl.BlockSpec((1, tm, d), functools.partial(lambda i, j, c: (i, j, c), c=C_MERGE // d + c))
              for c in range(N_BRANCH)]
    return pl.pallas_call(
        _merge_kernel,
        out_shape=jax.ShapeDtypeStruct((b, s, d), F32),
        grid=(b, s // tm),
        in_specs=[ospec, ospec, ospec] + gspecs + [
            pl.BlockSpec((N_BRANCH, W, d), lambda i, j: (0, 0, 0)),
            pl.BlockSpec((d, d), lambda i, j: (0, 0)),
            xspec,
            pl.BlockSpec((1, 6, d), lambda i, j: (i, 0, 0))],
        out_specs=xspec,
        compiler_params=_params("parallel", "parallel"),
        name="merge_project",
    )(o_nsa, o_sb, o_ml, y3, y3, y3, w_branch.astype(BF16), w_out.astype(BF16), x, mod)


def _ffn_kernel(x_ref, g_ref, mod_ref, wg_ref, wu_ref, wd_ref, o_ref, h_ref, acc_ref):
    f = pl.program_id(2)

    @pl.when(f == 0)
    def _():
        h_ref[...] = _norm_mod(x_ref[0], g_ref[...], mod_ref[0], 3, 4).astype(BF16)
        acc_ref[...] = jnp.zeros_like(acc_ref)

    h = h_ref[...]
    a = jnp.dot(h, wg_ref[...], preferred_element_type=F32)
    u = jnp.dot(h, wu_ref[...], preferred_element_type=F32)
    act = (a * jax.nn.sigmoid(a) * u).astype(BF16)
    acc_ref[...] += jnp.dot(act, wd_ref[...], preferred_element_type=F32)

    @pl.when(f == pl.num_programs(2) - 1)
    def _():
        o_ref[0] = x_ref[0] + mod_ref[0, 5:6, :] * acc_ref[...]


def dense_ffn(x, g, mod, wg, wu, wd, tm=512, n_ftiles=2):
    b, s, d = x.shape
    ff = wg.shape[1]
    tf = -(-ff // (n_ftiles * LANES)) * LANES
    pad = n_ftiles * tf - ff
    wg = jnp.pad(to_bf16(wg), ((0, 0), (0, pad)))
    wu = jnp.pad(to_bf16(wu), ((0, 0), (0, pad)))
    wd = jnp.pad(to_bf16(wd), ((0, pad), (0, 0)))
    xspec = pl.BlockSpec((1, tm, d), lambda i, j, f: (i, j, 0))
    return pl.pallas_call(
        _ffn_kernel,
        out_shape=jax.ShapeDtypeStruct((b, s, d), F32),
        grid=(b, s // tm, n_ftiles),
        in_specs=[xspec,
                  pl.BlockSpec((1, d), lambda i, j, f: (0, 0)),
                  pl.BlockSpec((1, 6, d), lambda i, j, f: (i, 0, 0)),
                  pl.BlockSpec((d, tf), lambda i, j, f: (0, f)),
                  pl.BlockSpec((d, tf), lambda i, j, f: (0, f)),
                  pl.BlockSpec((tf, d), lambda i, j, f: (f, 0))],
        out_specs=xspec,
        scratch_shapes=[pltpu.VMEM((tm, d), BF16), pltpu.VMEM((tm, d), F32)],
        compiler_params=_params("parallel", "parallel", "arbitrary"),
        name="dense_ffn",
    )(x, g.reshape(1, d), mod, wg, wu, wd)


def _router_kernel(x_ref, g_ref, mod_ref, wr_ref, h_ref, e_ref, p_ref):
    h = _norm_mod(x_ref[0], g_ref[...], mod_ref[0], 3, 4)
    h_ref[...] = h
    lane = _iota((1, LANES), 1)
    real = lane < N_EXPERTS
    logits = jnp.where(real, jnp.dot(h, wr_ref[...], precision=HIGHEST, preferred_element_type=F32), NEG)
    e = jnp.exp(logits - jnp.max(logits, axis=1, keepdims=True))
    p = jnp.where(real, e / jnp.sum(e, axis=1, keepdims=True), -1.0)
    p1 = jnp.max(p, axis=1, keepdims=True)
    i1 = jnp.min(jnp.where(p == p1, lane, LANES), axis=1, keepdims=True)
    rest = jnp.where(lane == i1, -1.0, p)
    p2 = jnp.max(rest, axis=1, keepdims=True)
    i2 = jnp.min(jnp.where(rest == p2, lane, LANES), axis=1, keepdims=True)
    tot = p1 + p2
    e_ref[...] = jnp.where(lane == 0, i1, jnp.where(lane == 1, i2, 0))[:, 0:N_EXPERTS]
    p_ref[...] = jnp.where(lane == 0, p1 / tot, jnp.where(lane == 1, p2 / tot, 0.0))[:, 0:N_EXPERTS]


def moe_router(x, g, mod, w_router, tm=512):
    b, s, d = x.shape
    t = b * s
    spb = s // tm
    wr = jnp.pad(w_router, ((0, 0), (0, LANES - N_EXPERTS)))
    return pl.pallas_call(
        _router_kernel,
        out_shape=(jax.ShapeDtypeStruct((t, d), F32),
                   jax.ShapeDtypeStruct((t, N_EXPERTS), I32),
                   jax.ShapeDtypeStruct((t, N_EXPERTS), F32)),
        grid=(b, spb),
        in_specs=[pl.BlockSpec((1, tm, d), lambda i, j: (i, j, 0)),
                  pl.BlockSpec((1, d), lambda i, j: (0, 0)),
                  pl.BlockSpec((1, 6, d), lambda i, j: (i, 0, 0)),
                  pl.BlockSpec((d, LANES), lambda i, j: (0, 0))],
        out_specs=(pl.BlockSpec((tm, d), lambda i, j: (i * spb + j, 0)),
                   pl.BlockSpec((tm, N_EXPERTS), lambda i, j: (i * spb + j, 0)),
                   pl.BlockSpec((tm, N_EXPERTS), lambda i, j: (i * spb + j, 0))),
        compiler_params=_params("parallel", "parallel"),
        name="moe_router",
    )(x, g.reshape(1, d), mod, wr)


def _dispatch_kernel(dest_ref, h_ref, zero_hbm, xpad_hbm, sem, *, td):
    del zero_hbm
    base = pl.program_id(0) * td * TOP_K

    def row_copy(a):
        return pltpu.make_async_copy(h_ref.at[a // TOP_K], xpad_hbm.at[dest_ref[base + a]], sem)

    def issue(a, c):
        row_copy(a).start()
        return c

    def drain(a, c):
        row_copy(a).wait()
        return c

    lax.fori_loop(0, td * TOP_K, issue, 0)
    lax.fori_loop(0, td * TOP_K, drain, 0)


def moe_dispatch(h, dest, n_rows, td=256):
    t, d = h.shape
    return pl.pallas_call(
        functools.partial(_dispatch_kernel, td=td),
        out_shape=jax.ShapeDtypeStruct((n_rows, d), h.dtype),
        grid_spec=pltpu.PrefetchScalarGridSpec(
            num_scalar_prefetch=1,
            grid=(t // td,),
            in_specs=[pl.BlockSpec((td, d), lambda i, dr: (i, 0)), pl.BlockSpec(memory_space=pl.ANY)],
            out_specs=pl.BlockSpec(memory_space=pl.ANY),
            scratch_shapes=[pltpu.SemaphoreType.DMA(())]),
        input_output_aliases={2: 0},
        compiler_params=pltpu.CompilerParams(dimension_semantics=("arbitrary",), has_side_effects=True),
        name="moe_dispatch",
    )(dest, h, jnp.zeros((n_rows, d), h.dtype))


def _expert_kernel(be_ref, nu_ref, x_ref, wg_ref, wu_ref, wd_ref, o_ref, xb_ref, acc_ref):
    i = pl.program_id(0)
    f = pl.program_id(1)
    used = i < nu_ref[0]

    @pl.when(f == 0)
    def _():
        xb_ref[...] = x_ref[...].astype(BF16)
        acc_ref[...] = jnp.zeros_like(acc_ref)

    @pl.when(used)
    def _():
        xb = xb_ref[...]
        a = jnp.dot(xb, wg_ref[0], preferred_element_type=F32)
        u = jnp.dot(xb, wu_ref[0], preferred_element_type=F32)
        act = (a * jax.nn.sigmoid(a) * u).astype(BF16)
        acc_ref[...] += jnp.dot(act, wd_ref[0], preferred_element_type=F32)

    @pl.when(f == pl.num_programs(1) - 1)
    def _():
        o_ref[...] = acc_ref[...]


def moe_experts(x_pad, blk_expert, n_used, wg, wu, wd, tb, tf=896):
    p, d = x_pad.shape
    ff = wg.shape[2]
    return pl.pallas_call(
        _expert_kernel,
        out_shape=jax.ShapeDtypeStruct((p, d), F32),
        grid_spec=pltpu.PrefetchScalarGridSpec(
            num_scalar_prefetch=2,
            grid=(p // tb, ff // tf),
            in_specs=[pl.BlockSpec((tb, d), lambda i, f, be, nu: (i, 0)),
                      pl.BlockSpec((1, d, tf), lambda i, f, be, nu: (be[i], 0, f)),
                      pl.BlockSpec((1, d, tf), lambda i, f, be, nu: (be[i], 0, f)),
                      pl.BlockSpec((1, tf, d), lambda i, f, be, nu: (be[i], f, 0))],
            out_specs=pl.BlockSpec((tb, d), lambda i, f, be, nu: (i, 0)),
            scratch_shapes=[pltpu.VMEM((tb, d), BF16), pltpu.VMEM((tb, d), F32)]),
        compiler_params=_params("parallel", "arbitrary"),
        name="moe_experts",
    )(blk_expert, n_used, x_pad, wg, wu, wd)


def _combine_kernel(dest_ref, y_hbm, x_ref, p_ref, mod_ref, o_ref, buf0, buf1, sem, *, td, spb):
    tok0 = (pl.program_id(0) * spb + pl.program_id(1)) * td

    def row_copies(r):
        a = (tok0 + r) * TOP_K
        return (pltpu.make_async_copy(y_hbm.at[dest_ref[a]], buf0.at[r], sem),
                pltpu.make_async_copy(y_hbm.at[dest_ref[a + 1]], buf1.at[r], sem))

    def issue(r, c):
        c0, c1 = row_copies(r)
        c0.start()
        c1.start()
        return c

    def drain(r, c):
        c0, c1 = row_copies(r)
        c0.wait()
        c1.wait()
        return c

    lax.fori_loop(0, td, issue, 0)
    lax.fori_loop(0, td, drain, 0)
    w = p_ref[...]
    f = w[:, 0:1] * buf0[...] + w[:, 1:2] * buf1[...]
    o_ref[0] = x_ref[0] + mod_ref[0, 5:6, :] * f


def moe_combine(y, dest, x, top_p, mod, td=256):
    b, s, d = x.shape
    spb = s // td
    return pl.pallas_call(
        functools.partial(_combine_kernel, td=td, spb=spb),
        out_shape=jax.ShapeDtypeStruct((b, s, d), F32),
        grid_spec=pltpu.PrefetchScalarGridSpec(
            num_scalar_prefetch=1,
            grid=(b, spb),
            in_specs=[pl.BlockSpec(memory_space=pl.ANY),
                      pl.BlockSpec((1, td, d), lambda i, j, dr: (i, j, 0)),
                      pl.BlockSpec((td, N_EXPERTS), lambda i, j, dr: (i * spb + j, 0)),
                      pl.BlockSpec((1, 6, d), lambda i, j, dr: (i, 0, 0))],
            out_specs=pl.BlockSpec((1, td, d), lambda i, j, dr: (i, j, 0)),
            scratch_shapes=[pltpu.VMEM((td, d), F32), pltpu.VMEM((td, d), F32),
                            pltpu.SemaphoreType.DMA(())]),
        compiler_params=_params("arbitrary", "arbitrary"),
        name="moe_combine",
    )(dest, y, x, top_p, mod)


def moe_ffn(x, g, mod, w_router, wg, wu, wd, tb=512):
    b, s, d = x.shape
    t = b * s
    a = t * TOP_K
    h, top_e, top_p = moe_router(x, g, mod, w_router)
    e_flat = top_e[:, 0:TOP_K].reshape(a)
    onehot = (e_flat[:, None] == jnp.arange(N_EXPERTS, dtype=I32)[None, :]).astype(I32)
    csum = jnp.cumsum(onehot, axis=0)
    rank = jnp.sum(onehot * csum, axis=1) - 1
    counts = csum[-1]
    padded = (counts + tb - 1) // tb * tb
    pad_ends = jnp.cumsum(padded)
    pad_starts = pad_ends - padded
    dest = (jnp.sum(onehot * pad_starts[None, :], axis=1) + rank).astype(I32)
    n_rows = (a // tb + N_EXPERTS) * tb
    n_blk = n_rows // tb
    blk_expert = jnp.minimum(
        jnp.searchsorted(pad_ends, jnp.arange(n_blk, dtype=I32) * tb, side="right"), N_EXPERTS - 1).astype(I32)
    n_used = (pad_ends[-1:] // tb).astype(I32)
    x_pad = moe_dispatch(h, dest, n_rows)
    y = moe_experts(x_pad, blk_expert, n_used, to_bf16(wg), to_bf16(wu), to_bf16(wd), tb)
    return moe_combine(y, dest, x, top_p, mod)


def _pack_w_in(w_in):
    kv = 2 * NSA_KV_HEADS * HEAD_DIM * 3
    w_in = to_bf16(w_in)
    o = 0
    nsa_q = w_in[:, o:o + MIX_WIDTH]; o += MIX_WIDTH
    nsa_kv = w_in[:, o:o + kv]; o += kv
    nsa_gate = w_in[:, o:o + NSA_HEADS * N_BRANCH]; o += NSA_HEADS * N_BRANCH
    sb = w_in[:, o:o + 3 * MIX_WIDTH]; o += 3 * MIX_WIDTH
    ml_qkv = w_in[:, o:o + 3 * MIX_WIDTH]; o += 3 * MIX_WIDTH
    ml_if = w_in[:, o:o + 2 * ML_HEADS]; o += 2 * ML_HEADS
    ml_o = w_in[:, o:o + MIX_WIDTH]; o += MIX_WIDTH
    merge = w_in[:, o:]
    main = jnp.concatenate([merge, nsa_q, ml_qkv, ml_o, sb, nsa_kv], axis=1)
    small = jnp.concatenate([nsa_gate, ml_if], axis=1)
    small = jnp.pad(small, ((0, 0), (0, N_SMALL - small.shape[1])))
    return main, small


def token_mixer_layer(x, mod, norm_g, w_in, nsa_q_norm, nsa_k_norm, cmp_pos, cmp_w1, cmp_b1, cmp_w2,
                      cmp_b2, ml_conv_w, ml_conv_b, ml_gate_b, w_branch, w_out):
    b, s, d = x.shape
    t = b * s
    h = normmod(x, norm_g, mod, 0, 1).reshape(t, d)
    w_main, w_small = _pack_w_in(w_in)
    y3 = matmul(h, w_main, BF16, 512, N_MAIN // 2).reshape(b, s, N_MAIN)
    small3 = matmul(h, w_small, F32, 1024, N_SMALL).reshape(b, s, N_SMALL)
    o_sb = sb_attention(y3)
    o_ml = mlstm(y3, small3, ml_conv_w, ml_conv_b, ml_gate_b)
    normed = nsa_headnorm(y3, nsa_q_norm, nsa_k_norm)
    kc, vc = nsa_compress(y3, cmp_pos, cmp_w1, cmp_b1, cmp_w2, cmp_b2, nsa_k_norm[0])
    o_nsa = nsa_attention(y3, small3, normed, kc, vc)
    return merge_project(o_nsa, o_sb, o_ml, y3, w_branch, w_out, x, mod)


def kernel(x, c, ada_w, ada_b, norm_mix, norm_ffn, w_in, nsa_q_norm, nsa_k_norm, cmp_pos, cmp_w1, cmp_b1,
           cmp_w2, cmp_b2, ml_conv_w, ml_conv_b, ml_gate_b, w_branch, w_out, ffn_wg, ffn_wu, ffn_wd,
           moe_router, moe_wg, moe_wu, moe_wd):
    depth = ada_w.shape[0]
    b, s, d = x.shape
    mods = adaln(c, ada_w, ada_b).reshape(depth, b, 6, d)
    for layer in range(depth):
        mod = mods[layer]
        x = token_mixer_layer(x, mod, norm_mix[layer], w_in[layer], nsa_q_norm[layer], nsa_k_norm[layer],
                              cmp_pos[layer], cmp_w1[layer], cmp_b1[layer], cmp_w2[layer], cmp_b2[layer],
                              ml_conv_w[layer], ml_conv_b[layer], ml_gate_b[layer], w_branch[layer],
                              w_out[layer])
        j = layer // 2
        if layer % 2 == 0:
            x = dense_ffn(x, norm_ffn[layer], mod, ffn_wg[j], ffn_wu[j], ffn_wd[j])
        else:
            x = moe_ffn(x, norm_ffn[layer], mod, moe_router[j], moe_wg[j], moe_wu[j], moe_wd[j])
    return x
```

```python
import functools

import numpy as np
import jax
import jax.numpy as jnp
from jax import lax
from jax.experimental import pallas as pl
from jax.experimental.pallas import tpu as pltpu

F32 = jnp.float32
BF16 = jnp.bfloat16
I32 = jnp.int32
HIGHEST = lax.Precision.HIGHEST

EPS = 1e-6
NEG = -1e30
HEAD_DIM = 64
MIX_WIDTH = 512
NSA_HEADS = 8
NSA_KV_HEADS = 2
NSA_GROUP = NSA_HEADS // NSA_KV_HEADS
CMP_BLOCK = 32
CMP_STRIDE = 16
SEL_BLOCK = 64
SEL_TOPK = 16
WINDOW = 512
FORCE_BONUS = 1e4
ML_HEADS = 4
ML_HEAD_DIM = 128
ML_CHUNK = 64
CONV_WIDTH = 4
N_BRANCH = 3
N_EXPERTS = 8
TOP_K = 2
LANES = 128

C_MERGE = 0
C_NSA_Q = 3072
C_ML_Q = 3584
C_ML_K = 4096
C_ML_V = 4608
C_ML_O = 5120
C_SB_Q = 5632
C_SB_K = 6144
C_SB_V = 6656
C_NSA_KV = 7168
N_MAIN = 7936
S_NSA_GATE = 0
S_ML_I = 24
S_ML_F = 28
N_SMALL = 128

VMEM_LIMIT = 56 * 1024 * 1024


def _params(*sem):
    return pltpu.CompilerParams(dimension_semantics=sem, vmem_limit_bytes=VMEM_LIMIT)


def _iota(shape, dim):
    return lax.broadcasted_iota(I32, shape, dim)


def _split_dot(a32, b_bf16):
    hi = a32.astype(BF16)
    lo = (a32 - hi.astype(F32)).astype(BF16)
    return (jnp.dot(hi, b_bf16, preferred_element_type=F32)
            + jnp.dot(lo, b_bf16, preferred_element_type=F32))


def _split_dot_left(a_bf16, b32):
    hi = b32.astype(BF16)
    lo = (b32 - hi.astype(F32)).astype(BF16)
    return (jnp.dot(a_bf16, hi, preferred_element_type=F32)
            + jnp.dot(a_bf16, lo, preferred_element_type=F32))


def _dot_nt(a, b):
    return lax.dot_general(a, b, (((1,), (1,)), ((), ())), preferred_element_type=F32)


def _log_sigmoid(z):
    return jnp.minimum(z, 0.0) - jnp.log1p(jnp.exp(-jnp.abs(z)))


def _cast_kernel(x_ref, o_ref):
    o_ref[...] = x_ref[...].astype(o_ref.dtype)


def to_bf16(w, max_rows=512):
    cols = w.shape[-1]
    w2 = w.reshape(-1, cols)
    rows = w2.shape[0]
    tr = max(t for t in range(8, max_rows + 1, 8) if rows % t == 0)
    out = pl.pallas_call(
        _cast_kernel,
        out_shape=jax.ShapeDtypeStruct((rows, cols), BF16),
        grid=(rows // tr,),
        in_specs=[pl.BlockSpec((tr, cols), lambda i: (i, 0))],
        out_specs=pl.BlockSpec((tr, cols), lambda i: (i, 0)),
        compiler_params=_params("parallel"),
        name="to_bf16",
    )(w2)
    return out.reshape(w.shape)


def _adaln_kernel(c_ref, w_ref, b_ref, o_ref):
    c = c_ref[...]
    cond = c * jax.nn.sigmoid(c)
    o_ref[0] = jnp.dot(cond, w_ref[0], precision=HIGHEST, preferred_element_type=F32) + b_ref[0]


def adaln(c, ada_w, ada_b):
    depth, d, n = ada_w.shape
    b = c.shape[0]
    tn = 1536
    return pl.pallas_call(
        _adaln_kernel,
        out_shape=jax.ShapeDtypeStruct((depth, b, n), F32),
        grid=(depth, n // tn),
        in_specs=[pl.BlockSpec((b, d), lambda l, j: (0, 0)),
                  pl.BlockSpec((1, d, tn), lambda l, j: (l, 0, j)),
                  pl.BlockSpec((1, 1, tn), lambda l, j: (l, 0, j))],
        out_specs=pl.BlockSpec((1, b, tn), lambda l, j: (l, 0, j)),
        compiler_params=_params("parallel", "parallel"),
        name="adaln",
    )(c, ada_w, ada_b.reshape(depth, 1, n))


def _norm_mod(x, g, mod, shift_row, scale_row):
    ms = jnp.mean(x * x, axis=-1, keepdims=True)
    y = x * lax.rsqrt(ms + EPS) * g
    return y * (1.0 + mod[scale_row:scale_row + 1, :]) + mod[shift_row:shift_row + 1, :]


def _normmod_kernel(x_ref, g_ref, mod_ref, o_ref, *, shift_row, scale_row):
    o_ref[0] = _norm_mod(x_ref[0], g_ref[...], mod_ref[0], shift_row, scale_row).astype(o_ref.dtype)


def normmod(x, g, mod, shift_row, scale_row, ts=512):
    b, s, d = x.shape
    return pl.pallas_call(
        functools.partial(_normmod_kernel, shift_row=shift_row, scale_row=scale_row),
        out_shape=jax.ShapeDtypeStruct((b, s, d), BF16),
        grid=(b, s // ts),
        in_specs=[pl.BlockSpec((1, ts, d), lambda i, j: (i, j, 0)),
                  pl.BlockSpec((1, d), lambda i, j: (0, 0)),
                  pl.BlockSpec((1, 6, d), lambda i, j: (i, 0, 0))],
        out_specs=pl.BlockSpec((1, ts, d), lambda i, j: (i, j, 0)),
        compiler_params=_params("parallel", "parallel"),
        name="normmod",
    )(x, g.reshape(1, d), mod)


def _mm_kernel(a_ref, w_ref, o_ref):
    o_ref[...] = jnp.dot(a_ref[...], w_ref[...], preferred_element_type=F32).astype(o_ref.dtype)


def matmul(a, w, out_dtype, tm, tn):
    m, k = a.shape
    n = w.shape[1]
    return pl.pallas_call(
        _mm_kernel,
        out_shape=jax.ShapeDtypeStruct((m, n), out_dtype),
        grid=(m // tm, n // tn),
        in_specs=[pl.BlockSpec((tm, k), lambda i, j: (i, 0)),
                  pl.BlockSpec((k, tn), lambda i, j: (0, j))],
        out_specs=pl.BlockSpec((tm, tn), lambda i, j: (i, j)),
        compiler_params=_params("parallel", "parallel"),
        name="matmul",
    )(a, w)


SB_EXP_FLOOR = -104.0


def _sb_kernel(q_ref, k_ref, v_ref, o_ref, vt_ref, *, tq):
    TK, dh = LANES, HEAD_DIM
    n_sub = tq // TK
    W = 2 * tq
    qi = pl.program_id(2)
    q0 = qi * tq

    @pl.when(qi == 0)
    def _():
        for c in range(v_ref.shape[1] // tq):
            vt_ref[c] = v_ref[0, c * tq:(c + 1) * tq, :].astype(F32).T.astype(BF16)

    q_t = (q_ref[0].astype(F32) * (dh ** -0.5)).T
    chan = _iota((2 * dh, tq), 0)
    q_cat = jnp.concatenate([jnp.where(chan < dh, q_t, 0.0), jnp.where(chan < dh, 0.0, q_t)],
                            axis=1).astype(BF16)
    later = (_iota((TK, TK), 0) < _iota((TK, TK), 1)).astype(BF16)
    suffix = jnp.concatenate([jnp.concatenate([later, later], axis=1), jnp.ones((8, 2 * TK), BF16)], axis=0)

    def step(j, masked, st):
        carry, acc = st
        k0 = pl.multiple_of(j * tq, tq)
        z = jnp.dot(k_ref[0, pl.ds(k0, tq), :], q_cat, preferred_element_type=F32)
        lk = -(jnp.maximum(z, 0.0) + jnp.log(1.0 + jnp.exp(-jnp.abs(z))))
        ls = lk + z
        if masked:
            strict = (k0 + _iota((tq, W), 0)) < (q0 + (_iota((tq, W), 1) & (tq - 1)))
            lk = jnp.where(strict, lk, 0.0)
        hi = lk.astype(BF16)
        lo = (lk - hi.astype(F32)).astype(BF16)
        after = [None] * n_sub
        for sub in range(n_sub - 1, -1, -1):
            rows = slice(sub * TK, (sub + 1) * TK)
            res = jnp.dot(suffix, jnp.concatenate([hi[rows], lo[rows]], axis=0),
                          preferred_element_type=F32)
            after[sub] = res[0:TK] + carry
            carry = carry + res[TK:TK + 1]
        a = jnp.exp(ls + jnp.concatenate(after, axis=0))
        if masked:
            a = jnp.where(strict, a, 0.0)
        acc = acc + jnp.dot(vt_ref[j], a.astype(BF16), preferred_element_type=F32)
        return carry, acc

    def cond(st):
        return (st[0] >= 0) & (jnp.max(st[1]) > SB_EXP_FLOOR)

    def body(st):
        return (st[0] - 1,) + step(st[0], False, st[1:])

    st = step(qi, True, (jnp.zeros((1, W), F32), jnp.zeros((2 * dh, W), F32)))
    _, _, acc = lax.while_loop(cond, body, (qi - 1,) + st)
    o_ref[0] = jnp.concatenate([acc[0:dh, 0:tq], acc[dh:2 * dh, tq:W]], axis=0).T.astype(o_ref.dtype)


def sb_attention(y3, tq=256):
    b, s, _ = y3.shape
    n_pairs = MIX_WIDTH // LANES
    qb, kb, vb = C_SB_Q // LANES, C_SB_K // LANES, C_SB_V // LANES
    return pl.pallas_call(
        functools.partial(_sb_kernel, tq=tq),
        out_shape=jax.ShapeDtypeStruct((b, s, MIX_WIDTH), BF16),
        grid=(b, n_pairs, s // tq),
        in_specs=[pl.BlockSpec((1, tq, LANES), lambda i, p, j: (i, j, qb + p)),
                  pl.BlockSpec((1, s, LANES), lambda i, p, j: (i, 0, kb + p)),
                  pl.BlockSpec((1, s, LANES), lambda i, p, j: (i, 0, vb + p))],
        out_specs=pl.BlockSpec((1, tq, LANES), lambda i, p, j: (i, j, p)),
        scratch_shapes=[pltpu.VMEM((s // tq, LANES, tq), BF16)],
        compiler_params=_params("parallel", "parallel", "arbitrary"),
        name="sb_attention",
    )(y3, y3, y3)


def _mlstm_kernel(q_ref, k_ref, v_ref, og_ref, sm_ref, gr_ref, cw_ref, cb_ref, gb_ref, out_ref,
                  ct_ref, n_ref, m_ref, xbuf_ref, qk_ref, *, ts):
    L, dh, H, W = ML_CHUNK, ML_HEAD_DIM, ML_HEADS, MIX_WIDTH
    halo = 8
    sblk = pl.program_id(1)

    @pl.when(sblk == 0)
    def _():
        ct_ref[...] = jnp.zeros_like(ct_ref)
        n_ref[...] = jnp.zeros_like(n_ref)
        m_ref[...] = jnp.zeros_like(m_ref)
        xbuf_ref[0:halo, :] = jnp.zeros((halo, 2 * W), F32)

    @pl.when(sblk > 0)
    def _():
        xbuf_ref[0:halo, :] = xbuf_ref[ts:ts + halo, :]

    xbuf_ref[halo:halo + ts, 0:W] = q_ref[0].astype(F32)
    xbuf_ref[halo:halo + ts, W:2 * W] = k_ref[0].astype(F32)
    conv = cb_ref[...] + jnp.zeros((ts, 2 * W), F32)
    for j in range(CONV_WIDTH):
        off = halo - (CONV_WIDTH - 1) + j
        conv = conv + cw_ref[j:j + 1, :] * xbuf_ref[off:off + ts, :]
    act = conv * jax.nn.sigmoid(conv)
    qk_ref[:, 0:W] = (act[:, 0:W] * (dh ** -0.5)).astype(BF16)
    qk_ref[:, W:2 * W] = act[:, W:2 * W].astype(BF16)

    it0, it1 = _iota((L, L), 0), _iota((L, L), 1)
    causal = it0 >= it1
    tri_lo = causal.astype(BF16)
    tri_up = (it0 <= it1).astype(BF16)

    def chunk(c, carry):
        r0 = pl.multiple_of(c * L, L)
        sm = sm_ref[0, pl.ds(r0, L), :]
        gr = gr_ref[0, c]
        HR = range(H)
        rows = pl.ds(r0, L)
        cols = [slice(h * dh, (h + 1) * dh) for h in HR]
        ig_col = [sm[:, S_ML_I + h:S_ML_I + h + 1] + gb_ref[0, h] for h in HR]
        lf_col = [_log_sigmoid(sm[:, S_ML_F + h:S_ML_F + h + 1] + gb_ref[1, h]) for h in HR]
        ig_row = [gr[h:h + 1, :] + gb_ref[0, h] for h in HR]
        lf_row = [_log_sigmoid(gr[H + h:H + h + 1, :] + gb_ref[1, h]) for h in HR]
        b_t = [_split_dot_left(tri_lo, jnp.broadcast_to(lf_col[h], (L, L))) for h in HR]
        b_s = [_split_dot(jnp.broadcast_to(lf_row[h], (L, L)), tri_up) for h in HR]
        qq = [qk_ref[rows, cols[h]] for h in HR]
        kk = [qk_ref[rows, W + h * dh:W + (h + 1) * dh] for h in HR]
        vv = [v_ref[0, rows, cols[h]] for h in HR]
        ct = [ct_ref[h] for h in HR]
        nvec = [n_ref[h] for h in HR]
        m_prev = [m_ref[h][:, 0:1] for h in HR]
        qk = [_dot_nt(qq[h], kk[h]) for h in HR]
        q_c = [jnp.dot(qq[h], ct[h].astype(BF16), preferred_element_type=F32) for h in HR]
        kt = [kk[h].astype(F32).T.astype(BF16) for h in HR]
        dmat = [jnp.where(causal, b_t[h] - b_s[h] + ig_row[h], NEG) for h in HR]
        b_col = [b_t[h][:, 0:1] for h in HR]
        m_inter = [b_col[h] + m_prev[h] for h in HR]
        m_t = [jnp.maximum(m_inter[h], jnp.max(dmat[h], axis=1, keepdims=True)) for h in HR]
        w = [jnp.exp(dmat[h] - m_t[h]) * qk[h] for h in HR]
        inter = [jnp.exp(m_inter[h] - m_t[h]) for h in HR]
        w_v = [jnp.dot(w[h].astype(BF16), vv[h], preferred_element_type=F32) for h in HR]
        b_last = [b_t[h][L - 1:L, 0:1] for h in HR]
        decay = [b_last[h] - b_col[h] + ig_col[h] for h in HR]
        m_new = [jnp.maximum(b_last[h] + m_prev[h], jnp.max(decay[h], axis=0, keepdims=True)) for h in HR]
        ws = [jnp.exp(decay[h] - m_new[h]) for h in HR]
        cscale = [jnp.exp(b_last[h] + m_prev[h] - m_new[h]) for h in HR]
        wv = [(ws[h] * vv[h].astype(F32)).astype(BF16) for h in HR]
        k_wv = [jnp.dot(kt[h], wv[h], preferred_element_type=F32) for h in HR]
        for h in HR:
            num = inter[h] * q_c[h] + w_v[h]
            den = (inter[h] * jnp.sum(qq[h].astype(F32) * nvec[h], axis=1, keepdims=True)
                   + jnp.sum(w[h], axis=1, keepdims=True))
            hval = num / jnp.maximum(jnp.abs(den), jnp.exp(-m_t[h]))
            ct_ref[h] = cscale[h] * ct[h] + k_wv[h]
            n_ref[h] = cscale[h] * nvec[h] + jnp.sum(ws[h] * kk[h].astype(F32), axis=0, keepdims=True)
            m_ref[h] = jnp.broadcast_to(m_new[h], (1, LANES))
            gate = jax.nn.sigmoid(og_ref[0, rows, cols[h]].astype(F32))
            out_ref[0, rows, cols[h]] = (gate * hval).astype(out_ref.dtype)
        return carry

    lax.fori_loop(0, ts // L, chunk, 0)


def mlstm(y3, small3, conv_w, conv_b, gate_b, ts=512):
    b, s, _ = y3.shape
    W, H, L = MIX_WIDTH, ML_HEADS, ML_CHUNK
    gr = small3[:, :, S_ML_I:S_ML_I + 2 * H].reshape(b, s // L, L, 2 * H).transpose(0, 1, 3, 2)
    cq, ck, cv, co = C_ML_Q // W, C_ML_K // W, C_ML_V // W, C_ML_O // W
    return pl.pallas_call(
        functools.partial(_mlstm_kernel, ts=ts),
        out_shape=jax.ShapeDtypeStruct((b, s, W), BF16),
        grid=(b, s // ts),
        in_specs=[pl.BlockSpec((1, ts, W), lambda i, j: (i, j, cq)),
                  pl.BlockSpec((1, ts, W), lambda i, j: (i, j, ck)),
                  pl.BlockSpec((1, ts, W), lambda i, j: (i, j, cv)),
                  pl.BlockSpec((1, ts, W), lambda i, j: (i, j, co)),
                  pl.BlockSpec((1, ts, N_SMALL), lambda i, j: (i, j, 0)),
                  pl.BlockSpec((1, ts // L, 2 * H, L), lambda i, j: (i, j, 0, 0)),
                  pl.BlockSpec((CONV_WIDTH, 2 * W), lambda i, j: (0, 0)),
                  pl.BlockSpec((1, 2 * W), lambda i, j: (0, 0)),
                  pl.BlockSpec(memory_space=pltpu.SMEM)],
        out_specs=pl.BlockSpec((1, ts, W), lambda i, j: (i, j, 0)),
        scratch_shapes=[pltpu.VMEM((H, ML_HEAD_DIM, ML_HEAD_DIM), F32),
                        pltpu.VMEM((H, 1, ML_HEAD_DIM), F32),
                        pltpu.VMEM((H, 1, LANES), F32),
                        pltpu.VMEM((ts + 8, 2 * W), F32),
                        pltpu.VMEM((ts, 2 * W), BF16)],
        compiler_params=_params("parallel", "arbitrary"),
        name="mlstm",
    )(y3, y3, y3, y3, small3, gr, conv_w, conv_b.reshape(1, 2 * W), gate_b)


def _headnorm_kernel(x_ref, g_ref, o_ref):
    x = x_ref[0].astype(F32)
    same_head = (_iota((LANES, LANES), 0) // HEAD_DIM == _iota((LANES, LANES), 1) // HEAD_DIM)
    ss = _split_dot(x * x, same_head.astype(BF16))
    o_ref[0] = (x * lax.rsqrt(ss * (1.0 / HEAD_DIM) + EPS) * g_ref[0]).astype(o_ref.dtype)


def nsa_headnorm(y3, q_norm, k_norm, ts=1024):
    b, s, _ = y3.shape
    qb = C_NSA_Q // LANES
    ksb = C_NSA_KV // LANES + 2
    kwb = C_NSA_KV // LANES + 4
    gains = jnp.stack([jnp.tile(q_norm, 2)] * 4 + [jnp.tile(k_norm[1], 2), jnp.tile(k_norm[2], 2)])

    def col(j):
        return jnp.where(j < 4, qb + j, jnp.where(j == 4, ksb, kwb))

    return pl.pallas_call(
        _headnorm_kernel,
        out_shape=jax.ShapeDtypeStruct((b, s, 6 * LANES), BF16),
        grid=(b, s // ts, 6),
        in_specs=[pl.BlockSpec((1, ts, LANES), lambda i, t, j: (i, t, col(j))),
                  pl.BlockSpec((1, 1, LANES), lambda i, t, j: (j, 0, 0))],
        out_specs=pl.BlockSpec((1, ts, LANES), lambda i, t, j: (i, t, j)),
        compiler_params=_params("parallel", "parallel", "parallel"),
        name="nsa_headnorm",
    )(y3, gains.reshape(6, 1, LANES))


def _gelu_tanh(x):
    return 0.5 * x * (1.0 + jnp.tanh(0.7978845608028654 * (x + 0.044715 * (x * x * x))))


def _compress_kernel(ra_ref, rb_ref, pos_ref, w1_ref, b1_ref, w2_ref, b2_ref, kn_ref, kc_ref, vc_ref):
    half = (CMP_BLOCK // 2) * HEAD_DIM
    for j, o_ref in enumerate((kc_ref, vc_ref)):
        xa = (ra_ref[j, 0, 0].astype(F32) + pos_ref[j, :, 0:half]).astype(BF16)
        xb = (rb_ref[j, 0, 0].astype(F32) + pos_ref[j, :, half:2 * half]).astype(BF16)
        hid = (jnp.dot(xa, w1_ref[j, 0:half, :], preferred_element_type=F32)
               + jnp.dot(xb, w1_ref[j, half:2 * half, :], preferred_element_type=F32) + b1_ref[j])
        out = jnp.dot(_gelu_tanh(hid).astype(BF16), w2_ref[j], preferred_element_type=F32) + b2_ref[j]
        if j == 0:
            out = out * lax.rsqrt(jnp.mean(out * out, axis=-1, keepdims=True) + EPS) * kn_ref[...]
        o_ref[0, 0] = out


def nsa_compress(y3, cmp_pos, cmp_w1, cmp_b1, cmp_w2, cmp_b2, k_norm0):
    b, s, _ = y3.shape
    G, dh = NSA_KV_HEADS, HEAD_DIM
    nr = s // CMP_STRIDE
    wide = CMP_STRIDE * dh
    kv = y3[:, :, C_NSA_KV:C_NSA_KV + 2 * G * dh].reshape(b, s, 2, G, dh)
    ra = kv.transpose(2, 0, 3, 1, 4).reshape(2, b, G, nr, wide)
    rb = jnp.concatenate([ra[:, :, :, 1:], jnp.zeros((2, b, G, 1, wide), ra.dtype)], axis=3)
    hidden = cmp_w1.shape[-1]
    out = jax.ShapeDtypeStruct((b, G, nr, dh), F32)
    blk = pl.BlockSpec((2, 1, 1, nr, wide), lambda i, g: (0, i, g, 0, 0))
    oblk = pl.BlockSpec((1, 1, nr, dh), lambda i, g: (i, g, 0, 0))

    def full(shape):
        return pl.BlockSpec(shape, lambda i, g: (0,) * len(shape))

    return pl.pallas_call(
        _compress_kernel,
        out_shape=(out, out),
        grid=(b, G),
        in_specs=[blk, blk, full((2, 1, 2 * wide)), full((2, 2 * wide, hidden)), full((2, 1, hidden)),
                  full((2, hidden, dh)), full((2, 1, dh)), full((1, dh))],
        out_specs=(oblk, oblk),
        compiler_params=_params("parallel", "parallel"),
        name="nsa_compress",
    )(ra, rb, cmp_pos.reshape(2, 1, 2 * wide), cmp_w1.astype(BF16), cmp_b1.reshape(2, 1, hidden),
      cmp_w2.astype(BF16), cmp_b2.reshape(2, 1, dh), k_norm0.reshape(1, dh))


def _nsa_kernel_rowmajor(q_ref, gl_ref, kc_ref, vc_ref, ks_ref, vs_ref, kw_ref, vw_ref, o_ref, *, n_sel, top):
    QB, R, dh = 128, NSA_GROUP, HEAD_DIM
    rows = R * QB
    g = pl.program_id(1)
    qi = pl.program_id(2)
    q0 = qi * QB
    q = q_ref[0, 0, 0] * jnp.asarray(dh ** -0.5, BF16)
    row = _iota((rows, 1), 0)
    t_f = (q0 + (row & (QB - 1))).astype(F32)
    head = g * R + (row >> 7)
    slope = jnp.exp2(-(head + 1).astype(F32))

    n_cmp = kc_ref.shape[2]
    kc = kc_ref[0, 0]
    kc_hi = kc.astype(BF16)
    kc_lo = (kc - kc_hi.astype(F32)).astype(BF16)
    cmp_end = (_iota((1, n_cmp), 1) * CMP_STRIDE + (CMP_BLOCK - 1)).astype(F32)
    dist = t_f - cmp_end
    valid = dist >= 0.0
    s = jnp.where(valid, _dot_nt(q, kc_hi) + _dot_nt(q, kc_lo) - slope * dist, NEG)
    e = jnp.exp(s - jnp.max(s, axis=1, keepdims=True))
    p = jnp.where(valid, e / jnp.sum(e, axis=1, keepdims=True), 0.0)
    o_cmp = jnp.dot(p.astype(BF16), vc_ref[0, 0].astype(BF16), preferred_element_type=F32)

    p_grp = p[0:QB] + p[QB:2 * QB] + p[2 * QB:3 * QB] + p[3 * QB:4 * QB]
    c0 = _iota((n_cmp, n_sel), 0) * CMP_STRIDE
    s0 = _iota((n_cmp, n_sel), 1) * SEL_BLOCK
    overlap = ((c0 < s0 + SEL_BLOCK) & (c0 + CMP_BLOCK > s0)).astype(BF16)
    imp = _split_dot(p_grp, overlap)
    tq = q0 + _iota((QB, 1), 0)
    j_idx = _iota((1, n_sel), 1)
    cur = tq >> 6
    forced = (j_idx == 0) | (j_idx == cur) | (j_idx == cur - 1)
    imp = jnp.where(j_idx * SEL_BLOCK <= tq, imp + jnp.where(forced, FORCE_BONUS, 0.0), -1.0)
    sel = jnp.zeros((QB, n_sel), F32)
    for _ in range(top):
        mx = jnp.max(imp, axis=1, keepdims=True)
        first = jnp.min(jnp.where(imp == mx, j_idx, n_sel), axis=1, keepdims=True)
        pick = j_idx == first
        sel = jnp.where(pick, 1.0, sel)
        imp = jnp.where(pick, -3e38, imp)
    sel_bf = sel.astype(BF16)

    def flash(k_ref, v_ref, lo, hi, mask_fn):
        def body(kb, st):
            m_run, l_run, acc = st
            k0 = pl.multiple_of(kb * QB, QB)
            kblk = k_ref[0, 0, pl.ds(k0, QB), :]
            vblk = v_ref[0, 0, pl.ds(k0, QB), :]
            dist = t_f - (k0 + _iota((1, QB), 1)).astype(F32)
            ok = mask_fn(kb, dist)
            sc = jnp.where(ok, _dot_nt(q, kblk) - slope * dist, NEG)
            m_new = jnp.maximum(m_run, jnp.max(sc, axis=1, keepdims=True))
            alpha = jnp.exp(m_run - m_new)
            pr = jnp.where(ok, jnp.exp(sc - m_new), 0.0)
            l_run = alpha * l_run + jnp.sum(pr, axis=1, keepdims=True)
            acc = alpha * acc + jnp.dot(pr.astype(BF16), vblk, preferred_element_type=F32)
            return m_new, l_run, acc

        init = (jnp.full((rows, 1), NEG, F32), jnp.zeros((rows, 1), F32), jnp.zeros((rows, dh), F32))
        _, l_run, acc = lax.fori_loop(lo, hi, body, init)
        return acc / l_run

    def sel_mask(kb, dist):
        expand = (_iota((n_sel, QB), 0) == 2 * kb + (_iota((n_sel, QB), 1) >> 6)).astype(BF16)
        m = jnp.dot(sel_bf, expand, preferred_element_type=F32)
        m = jnp.concatenate([m] * R, axis=0)
        return (m > 0.5) & (dist >= 0.0)

    def win_mask(kb, dist):
        return (dist >= 0.0) & (dist < float(WINDOW))

    o_sel = flash(ks_ref, vs_ref, 0, qi + 1, sel_mask)
    o_win = flash(kw_ref, vw_ref, jnp.maximum(qi - WINDOW // QB, 0), qi + 1, win_mask)

    gate = jax.nn.sigmoid(gl_ref[0, 0, 0])
    o_ref[0, 0, 0] = (gate[:, 0:1] * o_cmp + gate[:, 1:2] * o_sel + gate[:, 2:3] * o_win).astype(o_ref.dtype)


def nsa_attention_rowmajor(y3, small3, normed, kc, vc):
    b, s, _ = y3.shape
    G, R, dh, QB = NSA_KV_HEADS, NSA_GROUP, HEAD_DIM, 128
    nq = s // QB
    n_sel = s // SEL_BLOCK
    top = min(SEL_TOPK, n_sel)

    def stack_heads(a, width):
        return (a.reshape(b, nq, QB, G, R, width).transpose(0, 3, 1, 4, 2, 5)
                .reshape(b, G, nq, R * QB, width))

    def kv_heads(a):
        return a.reshape(b, s, G, dh).transpose(0, 2, 1, 3)

    q = stack_heads(normed[:, :, 0:MIX_WIDTH], dh)
    gl = stack_heads(small3[:, :, S_NSA_GATE:S_NSA_GATE + NSA_HEADS * N_BRANCH], N_BRANCH)
    ks = kv_heads(normed[:, :, 4 * LANES:5 * LANES])
    kw = kv_heads(normed[:, :, 5 * LANES:6 * LANES])
    vs = kv_heads(y3[:, :, C_NSA_KV + 3 * LANES:C_NSA_KV + 4 * LANES])
    vw = kv_heads(y3[:, :, C_NSA_KV + 5 * LANES:C_NSA_KV + 6 * LANES])
    n_cmp = kc.shape[2]
    qspec = pl.BlockSpec((1, 1, 1, R * QB, dh), lambda i, g, j: (i, g, j, 0, 0))
    gspec = pl.BlockSpec((1, 1, 1, R * QB, N_BRANCH), lambda i, g, j: (i, g, j, 0, 0))
    cspec = pl.BlockSpec((1, 1, n_cmp, dh), lambda i, g, j: (i, g, 0, 0))
    kvspec = pl.BlockSpec((1, 1, s, dh), lambda i, g, j: (i, g, 0, 0))
    out = pl.pallas_call(
        functools.partial(_nsa_kernel_rowmajor, n_sel=n_sel, top=top),
        out_shape=jax.ShapeDtypeStruct((b, G, nq, R * QB, dh), BF16),
        grid=(b, G, nq),
        in_specs=[qspec, gspec, cspec, cspec, kvspec, kvspec, kvspec, kvspec],
        out_specs=qspec,
        compiler_params=_params("parallel", "parallel", "parallel"),
        name="nsa_attention",
    )(q, gl, kc, vc, ks, vs, kw, vw)
    return (out.reshape(b, G, nq, R, QB, dh).transpose(0, 2, 4, 1, 3, 5).reshape(b, s, MIX_WIDTH))


NSA_QB = 128
NSA_KS = 512
NSA_CK = 256
NSA_VR = 80
A_FEAT, A_PEN, A_BIAS = 0, 64, 128
A_DUMMY = A_BIAS + 4


def _nsa_kernel(qn_ref, gl_ref, kca_ref, vcT_ref, ksn_ref, kwn_ref, vs_ref, vw_ref, kconst_ref, o_ref,
                qt_ref, ksa_ref, kwa_ref, vsa_ref, vwa_ref, s_ref, *, n_sel, top):
    QB, R, dh, CK, VR, KS = NSA_QB, NSA_GROUP, HEAD_DIM, NSA_CK, NSA_VR, NSA_KS
    HQ = R * QB
    NPAD = WINDOW // QB
    SUB = KS // QB
    g = pl.program_id(1)
    qi = pl.program_id(2)
    q0 = qi * QB
    nkb = vsa_ref.shape[0]

    @pl.when(qi == 0)
    def _():
        ksa_ref[...] = kconst_ref[...]
        kwa_ref[0:WINDOW, :] = jnp.where(_iota((WINDOW, CK), 1) == A_DUMMY, 1.0, 0.0).astype(BF16)
        kwa_ref[WINDOW:, :] = kconst_ref[...]
        kwa_ref[WINDOW:, A_PEN:A_PEN + 64] = jnp.zeros((kwa_ref.shape[0] - WINDOW, 64), BF16)
        for gg in range(NSA_KV_HEADS):
            @pl.when(g == gg)
            def _():
                heads = slice(gg * dh, (gg + 1) * dh)
                ksa_ref[:, A_FEAT:A_FEAT + dh] = ksn_ref[0, :, heads]
                kwa_ref[WINDOW:, A_FEAT:A_FEAT + dh] = kwn_ref[0, :, heads]
                for c in range(nkb):
                    keys = slice(c * QB, (c + 1) * QB)
                    vsa_ref[c, 0:dh, :] = vs_ref[0, keys, :].astype(F32).T[heads].astype(BF16)
                    vwa_ref[NPAD + c, 0:dh, :] = vw_ref[0, keys, :].astype(F32).T[heads].astype(BF16)
        vwa_ref[0:NPAD, 0:dh, :] = jnp.zeros((NPAD, dh, QB), BF16)
        ones_rows = jnp.where(_iota((nkb + NPAD, VR - dh, QB), 1) == 0, 1.0, 0.0).astype(BF16)
        vsa_ref[:, dh:VR, :] = ones_rows[0:nkb]
        vwa_ref[:, dh:VR, :] = ones_rows
        qt_ref[A_BIAS + 16:CK, :] = jnp.zeros((CK - A_BIAS - 16, HQ), BF16)

    q_rows = (qn_ref[0].astype(F32) * (dh ** -0.5)).T
    qT = jnp.concatenate([q_rows[r * dh:(r + 1) * dh] for r in range(R)], axis=1).astype(BF16)
    qt_ref[A_FEAT:A_FEAT + dh, :] = qT
    qt_ref[A_PEN:A_PEN + dh, :] = qT
    lane = _iota((16, HQ), 1)
    rowi = _iota((16, HQ), 0)
    t_q = q0 + (lane & (QB - 1))
    slope = jnp.exp2(-(g * R + (lane >> 7) + 1).astype(F32))
    t_hi = ((t_q >> 6) << 6).astype(F32)
    t_lo = (t_q & 63).astype(F32)
    bias_rows = jnp.where(rowi < 2, slope,
                          jnp.where(rowi == 2, -slope * t_hi,
                                    jnp.where(rowi == 3, -slope * t_lo,
                                              jnp.where(rowi == A_DUMMY - A_BIAS, NEG, 0.0))))
    qt_ref[A_BIAS:A_BIAS + 16, :] = bias_rows.astype(BF16)
    k_loc = _iota((QB, HQ), 0)
    q_loc = _iota((QB, HQ), 1) & (QB - 1)

    def pv(v_ref_, kb0, pr):
        out = None
        for i in range(pr.shape[0] // QB):
            term = jnp.dot(v_ref_[kb0 + i], pr[i * QB:(i + 1) * QB], preferred_element_type=F32)
            out = term if out is None else out + term
        return out

    n_cmp = kca_ref.shape[2]
    sc = jnp.dot(kca_ref[0, 0], qt_ref[...], preferred_element_type=F32)
    cmp_end = _iota((n_cmp, HQ), 0) * CMP_STRIDE + (CMP_BLOCK - 1)
    valid = cmp_end <= q0 + (_iota((n_cmp, HQ), 1) & (QB - 1))
    sc = jnp.where(valid, sc, NEG)
    e = jnp.exp(sc - jnp.max(sc, axis=0, keepdims=True))
    p = jnp.where(valid, e * (1.0 / jnp.sum(e, axis=0, keepdims=True)), 0.0)
    o_cmp = jnp.dot(vcT_ref[0, 0], p.astype(BF16), preferred_element_type=F32)

    sw = jnp.dot(kwa_ref[pl.ds(pl.multiple_of(q0, QB), WINDOW + QB), :], qt_ref[...],
                 preferred_element_type=F32)
    sw = jnp.concatenate([jnp.where(k_loc > q_loc, sw[0:QB], NEG), sw[QB:WINDOW],
                          jnp.where(k_loc <= q_loc, sw[WINDOW:WINDOW + QB], NEG)], axis=0)
    pw = jnp.exp(sw - jnp.max(sw, axis=0, keepdims=True)).astype(BF16)
    acc_w = pv(vwa_ref, qi, pw)
    o_win = acc_w[0:dh] / acc_w[dh:dh + 1]

    p_grp = p[:, 0:QB] + p[:, QB:2 * QB] + p[:, 2 * QB:3 * QB] + p[:, 3 * QB:4 * QB]
    c0 = _iota((n_sel, n_cmp), 1) * CMP_STRIDE
    s0 = _iota((n_sel, n_cmp), 0) * SEL_BLOCK
    overlap_t = ((c0 < s0 + SEL_BLOCK) & (c0 + CMP_BLOCK > s0)).astype(BF16)
    imp = _split_dot_left(overlap_t, p_grp)
    j_idx = _iota((n_sel, QB), 0)
    tq = q0 + _iota((n_sel, QB), 1)
    cur = tq >> 6
    forced = (j_idx == 0) | (j_idx == cur) | (j_idx == cur - 1)
    causal_blk = j_idx * SEL_BLOCK <= tq
    imp = jnp.where(causal_blk, imp + jnp.where(forced, FORCE_BONUS, 0.0), -1.0)
    sel = jnp.zeros((n_sel, QB), F32)
    for _ in range(top):
        mx = jnp.max(imp, axis=0, keepdims=True)
        first = jnp.min(jnp.where(imp == mx, j_idx, n_sel), axis=0, keepdims=True)
        pick = j_idx == first
        sel = jnp.where(pick, 1.0, sel)
        imp = jnp.where(pick, -3e38, imp)
    pen = jnp.where((sel > 0.5) & causal_blk, 0.0, NEG)
    if n_sel < 64:
        pen = jnp.concatenate([pen, jnp.zeros((64 - n_sel, QB), F32)], axis=0)
    qt_ref[A_PEN:A_PEN + 64, :] = jnp.concatenate([pen] * R, axis=1).astype(BF16)

    def score(j):
        return jnp.dot(ksa_ref[pl.ds(pl.multiple_of(j * KS, KS), KS), :], qt_ref[...],
                       preferred_element_type=F32)

    def absorb(s, j, st):
        m_run, acc = st
        m_new = jnp.maximum(m_run, jnp.max(s, axis=0, keepdims=True))
        pr = jnp.exp(s - m_new).astype(BF16)
        return m_new, jnp.exp(m_run - m_new) * acc + pv(vsa_ref, j * SUB, pr)

    def body(j, carry):
        s_cur, st = carry
        s_next = score(j + 1)
        return s_next, absorb(s_cur, j, st)

    n_full = qi // SUB
    init = (jnp.full((1, HQ), NEG, F32), jnp.zeros((VR, HQ), F32))
    s_last, st = lax.fori_loop(0, n_full, body, (score(0), init))
    s_ref[...] = s_last
    diag = pl.ds(pl.multiple_of(q0 - n_full * KS, QB), QB)
    s_ref[diag, :] = jnp.where(k_loc <= q_loc, s_ref[diag, :], NEG)
    _, acc_s = absorb(s_ref[...], n_full, st)
    o_sel = acc_s[0:dh] / acc_s[dh:dh + 1]

    gl_t = gl_ref[0].T
    gate = []
    for br in range(N_BRANCH):
        per_kv = [jnp.concatenate([gl_t[(gg * R + r) * N_BRANCH + br:(gg * R + r) * N_BRANCH + br + 1]
                                   for r in range(R)], axis=1) for gg in range(NSA_KV_HEADS)]
        gate.append(jax.nn.sigmoid(jnp.where(g == 0, per_kv[0], per_kv[1])))
    o_t = gate[0] * o_cmp + gate[1] * o_sel + gate[2] * o_win
    o_ref[0] = jnp.concatenate([o_t[:, r * QB:(r + 1) * QB] for r in range(R)], axis=0).T.astype(o_ref.dtype)


def _nsa_t_kernel(qn_ref, gl_ref, kca_ref, vcT_ref, ksn_ref, kwn_ref, vs_ref, vw_ref, kconst_ref, o_ref,
                  qt_ref, ksa_ref, kwa_ref, vsa_ref, vwa_ref, *, n_sel, top):
    QB, R, dh, CK, VR, KS = NSA_QB, NSA_GROUP, HEAD_DIM, NSA_CK, NSA_VR, NSA_KS
    HQ = R * QB
    g = pl.program_id(1)
    qi = pl.program_id(2)
    q0 = qi * QB
    nkb = vsa_ref.shape[0]

    @pl.when(qi == 0)
    def _():
        ksa_ref[...] = kconst_ref[...]
        kwa_ref[...] = kconst_ref[...]
        kwa_ref[:, A_PEN:A_PEN + 64] = jnp.zeros((kwa_ref.shape[0], 64), BF16)
        for gg in range(NSA_KV_HEADS):
            @pl.when(g == gg)
            def _():
                heads = slice(gg * dh, (gg + 1) * dh)
                ksa_ref[:, A_FEAT:A_FEAT + dh] = ksn_ref[0, :, heads]
                kwa_ref[:, A_FEAT:A_FEAT + dh] = kwn_ref[0, :, heads]
                for c in range(nkb):
                    keys = slice(c * QB, (c + 1) * QB)
                    vsa_ref[c, 0:dh, :] = vs_ref[0, keys, :].astype(F32).T[heads].astype(BF16)
                    vwa_ref[c, 0:dh, :] = vw_ref[0, keys, :].astype(F32).T[heads].astype(BF16)
        ones_rows = jnp.where(_iota((nkb, VR - dh, QB), 1) == 0, 1.0, 0.0).astype(BF16)
        vsa_ref[:, dh:VR, :] = ones_rows
        vwa_ref[:, dh:VR, :] = ones_rows
        qt_ref[A_BIAS + 16:CK, :] = jnp.zeros((CK - A_BIAS - 16, HQ), BF16)

    q_rows = (qn_ref[0].astype(F32) * (dh ** -0.5)).T
    qT = jnp.concatenate([q_rows[r * dh:(r + 1) * dh] for r in range(R)], axis=1).astype(BF16)
    qt_ref[A_FEAT:A_FEAT + dh, :] = qT
    qt_ref[A_PEN:A_PEN + dh, :] = qT
    lane = _iota((16, HQ), 1)
    rowi = _iota((16, HQ), 0)
    t_q = q0 + (lane & (QB - 1))
    slope = jnp.exp2(-(g * R + (lane >> 7) + 1).astype(F32))
    t_hi = ((t_q >> 6) << 6).astype(F32)
    t_lo = (t_q & 63).astype(F32)
    bias_rows = jnp.where(rowi < 2, slope,
                          jnp.where(rowi == 2, -slope * t_hi, jnp.where(rowi == 3, -slope * t_lo, 0.0)))
    qt_ref[A_BIAS:A_BIAS + 16, :] = bias_rows.astype(BF16)

    n_cmp = kca_ref.shape[2]
    sc = jnp.dot(kca_ref[0, 0], qt_ref[...], preferred_element_type=F32)
    cmp_end = _iota((n_cmp, HQ), 0) * CMP_STRIDE + (CMP_BLOCK - 1)
    valid = cmp_end <= q0 + (_iota((n_cmp, HQ), 1) & (QB - 1))
    sc = jnp.where(valid, sc, NEG)
    e = jnp.exp(sc - jnp.max(sc, axis=0, keepdims=True))
    p = jnp.where(valid, e / jnp.sum(e, axis=0, keepdims=True), 0.0)
    o_cmp = jnp.dot(vcT_ref[0, 0], p.astype(BF16), preferred_element_type=F32)

    p_grp = p[:, 0:QB] + p[:, QB:2 * QB] + p[:, 2 * QB:3 * QB] + p[:, 3 * QB:4 * QB]
    c0 = _iota((n_sel, n_cmp), 1) * CMP_STRIDE
    s0 = _iota((n_sel, n_cmp), 0) * SEL_BLOCK
    overlap_t = ((c0 < s0 + SEL_BLOCK) & (c0 + CMP_BLOCK > s0)).astype(BF16)
    imp = _split_dot_left(overlap_t, p_grp)
    j_idx = _iota((n_sel, QB), 0)
    tq = q0 + _iota((n_sel, QB), 1)
    cur = tq >> 6
    forced = (j_idx == 0) | (j_idx == cur) | (j_idx == cur - 1)
    imp = jnp.where(j_idx * SEL_BLOCK <= tq, imp + jnp.where(forced, FORCE_BONUS, 0.0), -1.0)
    sel = jnp.zeros((n_sel, QB), F32)
    for _ in range(top):
        mx = jnp.max(imp, axis=0, keepdims=True)
        first = jnp.min(jnp.where(imp == mx, j_idx, n_sel), axis=0, keepdims=True)
        pick = j_idx == first
        sel = jnp.where(pick, 1.0, sel)
        imp = jnp.where(pick, -3e38, imp)
    pen = jnp.where(sel > 0.5, 0.0, NEG)
    if n_sel < 64:
        pen = jnp.concatenate([pen, jnp.zeros((64 - n_sel, QB), F32)], axis=0)
    qt_ref[A_PEN:A_PEN + 64, :] = jnp.concatenate([pen] * R, axis=1).astype(BF16)

    def attend(kaug_ref, vaug_ref, k0, nk, st, mode):
        k0 = pl.multiple_of(k0, QB)
        s = jnp.dot(kaug_ref[pl.ds(k0, nk), :], qt_ref[...], preferred_element_type=F32)
        if mode != "full":
            dist = (q0 + (_iota((nk, HQ), 1) & (QB - 1))) - (k0 + _iota((nk, HQ), 0))
            ok = dist >= 0
            if mode == "window":
                ok = ok & (dist < WINDOW)
            s = jnp.where(ok, s, NEG)
        m_run, acc = st
        m_new = jnp.maximum(m_run, jnp.max(s, axis=0, keepdims=True))
        pr = jnp.exp(s - m_new).astype(BF16)
        acc = jnp.exp(m_run - m_new) * acc
        kb0 = k0 // QB
        for i in range(nk // QB):
            acc = acc + jnp.dot(vaug_ref[kb0 + i], pr[i * QB:(i + 1) * QB], preferred_element_type=F32)
        return m_new, acc

    def finish(st):
        return st[1][0:dh] / st[1][dh:dh + 1]

    init = (jnp.full((1, HQ), NEG, F32), jnp.zeros((VR, HQ), F32))
    n_full = qi // (KS // QB)
    st = lax.fori_loop(0, n_full, lambda j, s_: attend(ksa_ref, vsa_ref, j * KS, KS, s_, "full"), init)
    o_sel = finish(attend(ksa_ref, vsa_ref, n_full * KS, KS, st, "causal"))
    o_win = finish(attend(kwa_ref, vwa_ref, jnp.maximum(q0 - WINDOW, 0), WINDOW + QB, init, "window"))

    gl_t = gl_ref[0].T
    gate = []
    for br in range(N_BRANCH):
        per_kv = [jnp.concatenate([gl_t[(gg * R + r) * N_BRANCH + br:(gg * R + r) * N_BRANCH + br + 1]
                                   for r in range(R)], axis=1) for gg in range(NSA_KV_HEADS)]
        gate.append(jax.nn.sigmoid(jnp.where(g == 0, per_kv[0], per_kv[1])))
    o_t = gate[0] * o_cmp + gate[1] * o_sel + gate[2] * o_win
    o_ref[0] = jnp.concatenate([o_t[:, r * QB:(r + 1) * QB] for r in range(R)], axis=0).T.astype(o_ref.dtype)


def nsa_attention(y3, small3, normed, kc, vc):
    b, s, _ = y3.shape
    G, R, dh, QB, CK, VR = NSA_KV_HEADS, NSA_GROUP, HEAD_DIM, NSA_QB, NSA_CK, NSA_VR
    assert G == 2, "the kernel picks a kv head's gate rows with a two-way select"
    HQ = R * QB
    nq = s // QB
    n_sel = s // SEL_BLOCK
    assert n_sel <= 64, "selection one-hot columns hold at most 64 blocks"
    top = min(SEL_TOPK, n_sel)
    n_cmp = kc.shape[2]

    def pos_cols(pos):
        return np.stack([pos // 64 * 64, pos % 64, np.ones_like(pos), np.ones_like(pos)], axis=1)

    vc_t = vc.transpose(0, 1, 3, 2).astype(BF16)

    pos = np.arange(s)
    kconst = np.zeros((s, CK), np.float32)
    kconst[pos, A_PEN + pos // SEL_BLOCK] = 1.0
    kconst[:, A_BIAS:A_BIAS + 4] = pos_cols(pos)
    kconst = jnp.asarray(kconst, BF16)

    kc_hi = kc.astype(BF16)
    kc_lo = (kc - kc_hi.astype(F32)).astype(BF16)
    cend = np.arange(n_cmp) * CMP_STRIDE + (CMP_BLOCK - 1)
    cbias = np.zeros((n_cmp, CK - 2 * dh), np.float32)
    cbias[:, 0:4] = pos_cols(cend)
    kc_aug = jnp.concatenate([kc_hi, kc_lo, jnp.broadcast_to(jnp.asarray(cbias, BF16), (b, G, n_cmp, CK - 2 * dh))],
                             axis=-1)

    qspec = pl.BlockSpec((1, QB, R * dh), lambda i, g, j: (i, j, g))
    vsb = C_NSA_KV // LANES + 3
    vwb = C_NSA_KV // LANES + 5
    return pl.pallas_call(
        functools.partial(_nsa_kernel, n_sel=n_sel, top=top),
        out_shape=jax.ShapeDtypeStruct((b, s, MIX_WIDTH), BF16),
        grid=(b, G, nq),
        in_specs=[qspec,
                  pl.BlockSpec((1, QB, N_SMALL), lambda i, g, j: (i, j, 0)),
                  pl.BlockSpec((1, 1, n_cmp, CK), lambda i, g, j: (i, g, 0, 0)),
                  pl.BlockSpec((1, 1, dh, n_cmp), lambda i, g, j: (i, g, 0, 0)),
                  pl.BlockSpec((1, s, LANES), lambda i, g, j: (i, 0, 4)),
                  pl.BlockSpec((1, s, LANES), lambda i, g, j: (i, 0, 5)),
                  pl.BlockSpec((1, s, LANES), lambda i, g, j: (i, 0, vsb)),
                  pl.BlockSpec((1, s, LANES), lambda i, g, j: (i, 0, vwb)),
                  pl.BlockSpec((s, CK), lambda i, g, j: (0, 0))],
        out_specs=qspec,
        scratch_shapes=[pltpu.VMEM((CK, HQ), BF16),
                        pltpu.VMEM((s, CK), BF16), pltpu.VMEM((s + WINDOW, CK), BF16),
                        pltpu.VMEM((nq, VR, QB), BF16), pltpu.VMEM((nq + WINDOW // QB, VR, QB), BF16),
                        pltpu.VMEM((NSA_KS, HQ), F32)],
        compiler_params=_params("parallel", "parallel", "arbitrary"),
        name="nsa_attention",
    )(normed, small3, kc_aug, vc_t, normed, normed, y3, y3, kconst)


def _merge_kernel(on_ref, os_ref, om_ref, g0_ref, g1_ref, g2_ref, wb_ref, wo_ref, x_ref, mod_ref, o_ref):
    merged = None
    for i, (o_r, g_r) in enumerate(((on_ref, g0_ref), (os_ref, g1_ref), (om_ref, g2_ref))):
        br = jnp.dot(o_r[0], wb_ref[i], preferred_element_type=F32)
        term = jax.nn.sigmoid(g_r[0].astype(F32)) * br
        merged = term if merged is None else merged + term
    out = jnp.dot(merged.astype(BF16), wo_ref[...], preferred_element_type=F32)
    o_ref[0] = x_ref[0] + mod_ref[0, 2:3, :] * out


def merge_project(o_nsa, o_sb, o_ml, y3, w_branch, w_out, x, mod, tm=512):
    b, s, d = x.shape
    W = MIX_WIDTH
    ospec = pl.BlockSpec((1, tm, W), lambda i, j: (i, j, 0))
    xspec = pl.BlockSpec((1, tm, d), lambda i, j: (i, j, 0))
    gspecs = [pl.BlockSpec((1, tm, d), functools.partial(lambda i, j, c: (i, j, c), c=C_MERGE // d + c))
              for c in range(N_BRANCH)]
    return pl.pallas_call(
        _merge_kernel,
        out_shape=jax.ShapeDtypeStruct((b, s, d), F32),
        grid=(b, s // tm),
        in_specs=[ospec, ospec, ospec] + gspecs + [
            pl.BlockSpec((N_BRANCH, W, d), lambda i, j: (0, 0, 0)),
            pl.BlockSpec((d, d), lambda i, j: (0, 0)),
            xspec,
            pl.BlockSpec((1, 6, d), lambda i, j: (i, 0, 0))],
        out_specs=xspec,
        compiler_params=_params("parallel", "parallel"),
        name="merge_project",
    )(o_nsa, o_sb, o_ml, y3, y3, y3, w_branch.astype(BF16), w_out.astype(BF16), x, mod)


def _ffn_kernel(x_ref, g_ref, mod_ref, wg_ref, wu_ref, wd_ref, o_ref, h_ref, acc_ref):
    f = pl.program_id(2)

    @pl.when(f == 0)
    def _():
        h_ref[...] = _norm_mod(x_ref[0], g_ref[...], mod_ref[0], 3, 4).astype(BF16)
        acc_ref[...] = jnp.zeros_like(acc_ref)

    h = h_ref[...]
    a = jnp.dot(h, wg_ref[...], preferred_element_type=F32)
    u = jnp.dot(h, wu_ref[...], preferred_element_type=F32)
    act = (a * jax.nn.sigmoid(a) * u).astype(BF16)
    acc_ref[...] += jnp.dot(act, wd_ref[...], preferred_element_type=F32)

    @pl.when(f == pl.num_programs(2) - 1)
    def _():
        o_ref[0] = x_ref[0] + mod_ref[0, 5:6, :] * acc_ref[...]


def dense_ffn(x, g, mod, wg, wu, wd, tm=512, n_ftiles=2):
    b, s, d = x.shape
    ff = wg.shape[1]
    tf = -(-ff // (n_ftiles * LANES)) * LANES
    pad = n_ftiles * tf - ff
    wg = jnp.pad(to_bf16(wg), ((0, 0), (0, pad)))
    wu = jnp.pad(to_bf16(wu), ((0, 0), (0, pad)))
    wd = jnp.pad(to_bf16(wd), ((0, pad), (0, 0)))
    xspec = pl.BlockSpec((1, tm, d), lambda i, j, f: (i, j, 0))
    return pl.pallas_call(
        _ffn_kernel,
        out_shape=jax.ShapeDtypeStruct((b, s, d), F32),
        grid=(b, s // tm, n_ftiles),
        in_specs=[xspec,
                  pl.BlockSpec((1, d), lambda i, j, f: (0, 0)),
                  pl.BlockSpec((1, 6, d), lambda i, j, f: (i, 0, 0)),
                  pl.BlockSpec((d, tf), lambda i, j, f: (0, f)),
                  pl.BlockSpec((d, tf), lambda i, j, f: (0, f)),
                  pl.BlockSpec((tf, d), lambda i, j, f: (f, 0))],
        out_specs=xspec,
        scratch_shapes=[pltpu.VMEM((tm, d), BF16), pltpu.VMEM((tm, d), F32)],
        compiler_params=_params("parallel", "parallel", "arbitrary"),
        name="dense_ffn",
    )(x, g.reshape(1, d), mod, wg, wu, wd)


def _router_kernel(x_ref, g_ref, mod_ref, wr_ref, h_ref, e_ref, p_ref):
    h = _norm_mod(x_ref[0], g_ref[...], mod_ref[0], 3, 4)
    h_ref[...] = h
    lane = _iota((1, LANES), 1)
    real = lane < N_EXPERTS
    logits = jnp.where(real, jnp.dot(h, wr_ref[...], precision=HIGHEST, preferred_element_type=F32), NEG)
    e = jnp.exp(logits - jnp.max(logits, axis=1, keepdims=True))
    p = jnp.where(real, e / jnp.sum(e, axis=1, keepdims=True), -1.0)
    p1 = jnp.max(p, axis=1, keepdims=True)
    i1 = jnp.min(jnp.where(p == p1, lane, LANES), axis=1, keepdims=True)
    rest = jnp.where(lane == i1, -1.0, p)
    p2 = jnp.max(rest, axis=1, keepdims=True)
    i2 = jnp.min(jnp.where(rest == p2, lane, LANES), axis=1, keepdims=True)
    tot = p1 + p2
    e_ref[...] = jnp.where(lane == 0, i1, jnp.where(lane == 1, i2, 0))[:, 0:N_EXPERTS]
    p_ref[...] = jnp.where(lane == 0, p1 / tot, jnp.where(lane == 1, p2 / tot, 0.0))[:, 0:N_EXPERTS]


def moe_router(x, g, mod, w_router, tm=512):
    b, s, d = x.shape
    t = b * s
    spb = s // tm
    wr = jnp.pad(w_router, ((0, 0), (0, LANES - N_EXPERTS)))
    return pl.pallas_call(
        _router_kernel,
        out_shape=(jax.ShapeDtypeStruct((t, d), F32),
                   jax.ShapeDtypeStruct((t, N_EXPERTS), I32),
                   jax.ShapeDtypeStruct((t, N_EXPERTS), F32)),
        grid=(b, spb),
        in_specs=[pl.BlockSpec((1, tm, d), lambda i, j: (i, j, 0)),
                  pl.BlockSpec((1, d), lambda i, j: (0, 0)),
                  pl.BlockSpec((1, 6, d), lambda i, j: (i, 0, 0)),
                  pl.BlockSpec((d, LANES), lambda i, j: (0, 0))],
        out_specs=(pl.BlockSpec((tm, d), lambda i, j: (i * spb + j, 0)),
                   pl.BlockSpec((tm, N_EXPERTS), lambda i, j: (i * spb + j, 0)),
                   pl.BlockSpec((tm, N_EXPERTS), lambda i, j: (i * spb + j, 0))),
        compiler_params=_params("parallel", "parallel"),
        name="moe_router",
    )(x, g.reshape(1, d), mod, wr)


def _dispatch_kernel(dest_ref, h_ref, zero_hbm, xpad_hbm, sem, *, td):
    del zero_hbm
    base = pl.program_id(0) * td * TOP_K

    def row_copy(a):
        return pltpu.make_async_copy(h_ref.at[a // TOP_K], xpad_hbm.at[dest_ref[base + a]], sem)

    def issue(a, c):
        row_copy(a).start()
        return c

    def drain(a, c):
        row_copy(a).wait()
        return c

    lax.fori_loop(0, td * TOP_K, issue, 0)
    lax.fori_loop(0, td * TOP_K, drain, 0)


def moe_dispatch(h, dest, n_rows, td=256):
    t, d = h.shape
    return pl.pallas_call(
        functools.partial(_dispatch_kernel, td=td),
        out_shape=jax.ShapeDtypeStruct((n_rows, d), h.dtype),
        grid_spec=pltpu.PrefetchScalarGridSpec(
            num_scalar_prefetch=1,
            grid=(t // td,),
            in_specs=[pl.BlockSpec((td, d), lambda i, dr: (i, 0)), pl.BlockSpec(memory_space=pl.ANY)],
            out_specs=pl.BlockSpec(memory_space=pl.ANY),
            scratch_shapes=[pltpu.SemaphoreType.DMA(())]),
        input_output_aliases={2: 0},
        compiler_params=pltpu.CompilerParams(dimension_semantics=("arbitrary",), has_side_effects=True),
        name="moe_dispatch",
    )(dest, h, jnp.zeros((n_rows, d), h.dtype))


def _expert_kernel(be_ref, nu_ref, x_ref, wg_ref, wu_ref, wd_ref, o_ref, xb_ref, acc_ref):
    i = pl.program_id(0)
    f = pl.program_id(1)
    used = i < nu_ref[0]

    @pl.when(f == 0)
    def _():
        xb_ref[...] = x_ref[...].astype(BF16)
        acc_ref[...] = jnp.zeros_like(acc_ref)

    @pl.when(used)
    def _():
        xb = xb_ref[...]
        a = jnp.dot(xb, wg_ref[0], preferred_element_type=F32)
        u = jnp.dot(xb, wu_ref[0], preferred_element_type=F32)
        act = (a * jax.nn.sigmoid(a) * u).astype(BF16)
        acc_ref[...] += jnp.dot(act, wd_ref[0], preferred_element_type=F32)

    @pl.when(f == pl.num_programs(1) - 1)
    def _():
        o_ref[...] = acc_ref[...]


def moe_experts(x_pad, blk_expert, n_used, wg, wu, wd, tb, tf=896):
    p, d = x_pad.shape
    ff = wg.shape[2]
    return pl.pallas_call(
        _expert_kernel,
        out_shape=jax.ShapeDtypeStruct((p, d), F32),
        grid_spec=pltpu.PrefetchScalarGridSpec(
            num_scalar_prefetch=2,
            grid=(p // tb, ff // tf),
            in_specs=[pl.BlockSpec((tb, d), lambda i, f, be, nu: (i, 0)),
                      pl.BlockSpec((1, d, tf), lambda i, f, be, nu: (be[i], 0, f)),
                      pl.BlockSpec((1, d, tf), lambda i, f, be, nu: (be[i], 0, f)),
                      pl.BlockSpec((1, tf, d), lambda i, f, be, nu: (be[i], f, 0))],
            out_specs=pl.BlockSpec((tb, d), lambda i, f, be, nu: (i, 0)),
            scratch_shapes=[pltpu.VMEM((tb, d), BF16), pltpu.VMEM((tb, d), F32)]),
        compiler_params=_params("parallel", "arbitrary"),
        name="moe_experts",
    )(blk_expert, n_used, x_pad, wg, wu, wd)


def _combine_kernel(dest_ref, y_hbm, x_ref, p_ref, mod_ref, o_ref, buf0, buf1, sem, *, td, spb):
    tok0 = (pl.program_id(0) * spb + pl.program_id(1)) * td

    def row_copies(r):
        a = (tok0 + r) * TOP_K
        return (pltpu.make_async_copy(y_hbm.at[dest_ref[a]], buf0.at[r], sem),
                pltpu.make_async_copy(y_hbm.at[dest_ref[a + 1]], buf1.at[r], sem))

    def issue(r, c):
        c0, c1 = row_copies(r)
        c0.start()
        c1.start()
        return c

    def drain(r, c):
        c0, c1 = row_copies(r)
        c0.wait()
        c1.wait()
        return c

    lax.fori_loop(0, td, issue, 0)
    lax.fori_loop(0, td, drain, 0)
    w = p_ref[...]
    f = w[:, 0:1] * buf0[...] + w[:, 1:2] * buf1[...]
    o_ref[0] = x_ref[0] + mod_ref[0, 5:6, :] * f


def moe_combine(y, dest, x, top_p, mod, td=256):
    b, s, d = x.shape
    spb = s // td
    return pl.pallas_call(
        functools.partial(_combine_kernel, td=td, spb=spb),
        out_shape=jax.ShapeDtypeStruct((b, s, d), F32),
        grid_spec=pltpu.PrefetchScalarGridSpec(
            num_scalar_prefetch=1,
            grid=(b, spb),
            in_specs=[pl.BlockSpec(memory_space=pl.ANY),
                      pl.BlockSpec((1, td, d), lambda i, j, dr: (i, j, 0)),
                      pl.BlockSpec((td, N_EXPERTS), lambda i, j, dr: (i * spb + j, 0)),
                      pl.BlockSpec((1, 6, d), lambda i, j, dr: (i, 0, 0))],
            out_specs=pl.BlockSpec((1, td, d), lambda i, j, dr: (i, j, 0)),
            scratch_shapes=[pltpu.VMEM((td, d), F32), pltpu.VMEM((td, d), F32),
                            pltpu.SemaphoreType.DMA(())]),
        compiler_params=_params("arbitrary", "arbitrary"),
        name="moe_combine",
    )(dest, y, x, top_p, mod)


def moe_ffn(x, g, mod, w_router, wg, wu, wd, tb=512):
    b, s, d = x.shape
    t = b * s
    a = t * TOP_K
    h, top_e, top_p = moe_router(x, g, mod, w_router)
    e_flat = top_e[:, 0:TOP_K].reshape(a)
    onehot = (e_flat[:, None] == jnp.arange(N_EXPERTS, dtype=I32)[None, :]).astype(I32)
    csum = jnp.cumsum(onehot, axis=0)
    rank = jnp.sum(onehot * csum, axis=1) - 1
    counts = csum[-1]
    padded = (counts + tb - 1) // tb * tb
    pad_ends = jnp.cumsum(padded)
    pad_starts = pad_ends - padded
    dest = (jnp.sum(onehot * pad_starts[None, :], axis=1) + rank).astype(I32)
    n_rows = (a // tb + N_EXPERTS + 1) * tb
    n_blk = n_rows // tb
    blk_expert = jnp.minimum(
        jnp.searchsorted(pad_ends, jnp.arange(n_blk, dtype=I32) * tb, side="right"), N_EXPERTS - 1).astype(I32)
    n_used = (pad_ends[-1:] // tb).astype(I32)
    slot_assign = moe_invert(dest, n_rows, tb)
    y2 = moe_experts_fused(h, slot_assign, blk_expert, n_used, to_bf16(wg), to_bf16(wu), to_bf16(wd), tb)
    return moe_mix(y2, x, top_p, mod)


def _invert_kernel(dest_ref, sa_ref, *, n_assign, tb, n_chunks):
    phase = pl.program_id(0)
    chunk = pl.program_id(1)

    @pl.when(phase == 0)
    def _():
        per = sa_ref.shape[0] // n_chunks

        def fill(j, c):
            p = chunk * per + j
            sa_ref[p] = n_assign + (p & (2 * tb - 1))
            return c

        lax.fori_loop(0, per, fill, 0, unroll=8)

    @pl.when(phase == 1)
    def _():
        per = n_assign // n_chunks

        def put(j, c):
            a = chunk * per + j
            sa_ref[dest_ref[a]] = a
            return c

        lax.fori_loop(0, per, put, 0, unroll=8)


def moe_invert(dest, n_rows, tb, n_chunks=16):
    n_assign = dest.shape[0]
    assert tb & (tb - 1) == 0 and n_rows % n_chunks == 0 and n_assign % n_chunks == 0
    return pl.pallas_call(
        functools.partial(_invert_kernel, n_assign=n_assign, tb=tb, n_chunks=n_chunks),
        out_shape=jax.ShapeDtypeStruct((n_rows,), I32),
        grid=(2, n_chunks),
        in_specs=[pl.BlockSpec(memory_space=pltpu.SMEM)],
        out_specs=pl.BlockSpec(memory_space=pltpu.SMEM),
        compiler_params=pltpu.CompilerParams(dimension_semantics=("arbitrary", "arbitrary")),
        name="moe_invert",
    )(dest)


def _expert_fused_kernel(be_ref, nu_ref, sa_ref, h_hbm, wg_ref, wu_ref, wd_ref, y_hbm,
                         xin_ref, yout_ref, xb_ref, acc_ref, sem_in, sem_out, *, tb, n_tok, n_f):
    i = pl.program_id(0)
    f = pl.program_id(1)
    n_used = nu_ref[0]
    n_assign = n_tok * TOP_K
    rows_f = tb // n_f
    active = i <= n_used

    def gather_row(blk, r):
        a = sa_ref[blk * tb + r]
        tok = jnp.where(a < n_assign, a >> 1, 0)
        return pltpu.make_async_copy(h_hbm.at[tok], xin_ref.at[blk % 2, r], sem_in.at[blk % 2])

    def scatter_row(blk, r):
        a = jnp.where(blk >= 0, sa_ref[jnp.maximum(blk, 0) * tb + r], n_assign + tb + r)
        row = jnp.where(a < n_assign, (a & 1) * n_tok + (a >> 1), a)
        return pltpu.make_async_copy(yout_ref.at[(blk + 2) % 2, r], y_hbm.at[row], sem_out.at[(blk + 2) % 2])

    def for_rows(fn):
        def body(r, c):
            fn(r)
            return c
        lax.fori_loop(0, tb, body, 0, unroll=8)

    @pl.when(f == 0)
    def _():
        @pl.when(i == 0)
        def _():
            yout_ref[1] = jnp.zeros((tb, yout_ref.shape[2]), F32)
            for_rows(lambda r: gather_row(0, r).start())

            def clear_row(r):
                return pltpu.make_async_copy(yout_ref.at[1, r], y_hbm.at[n_assign + r], sem_out.at[0])

            for_rows(lambda r: clear_row(r).start())
            for_rows(lambda r: clear_row(r).wait())

        @pl.when((i == 0) | (i - 1 <= n_used))
        def _():
            for_rows(lambda r: gather_row(i, r).wait())

        @pl.when(active)
        def _():
            xb_ref[...] = xin_ref[i % 2].astype(BF16)
            acc_ref[...] = jnp.zeros_like(acc_ref)

    @pl.when(active)
    def _():
        for r in range(rows_f):
            gather_row(i + 1, f * rows_f + r).start()
            scatter_row(i - 1, f * rows_f + r).start()
        xb = xb_ref[...]
        a = jnp.dot(xb, wg_ref[0], preferred_element_type=F32)
        u = jnp.dot(xb, wu_ref[0], preferred_element_type=F32)
        act = (a * jax.nn.sigmoid(a) * u).astype(BF16)
        acc_ref[...] += jnp.dot(act, wd_ref[0], preferred_element_type=F32)

    @pl.when(f == n_f - 1)
    def _():
        @pl.when((i >= 1) & (i - 1 <= n_used))
        def _():
            for_rows(lambda r: scatter_row(i - 2, r).wait())

        @pl.when(active)
        def _():
            yout_ref[i % 2] = acc_ref[...]


def moe_experts_fused(h, slot_assign, blk_expert, n_used, wg, wu, wd, tb, tf=896):
    n_tok, d = h.shape
    p = slot_assign.shape[0]
    ff = wg.shape[2]
    return pl.pallas_call(
        functools.partial(_expert_fused_kernel, tb=tb, n_tok=n_tok, n_f=ff // tf),
        out_shape=jax.ShapeDtypeStruct((n_tok * TOP_K + 2 * tb, d), F32),
        grid_spec=pltpu.PrefetchScalarGridSpec(
            num_scalar_prefetch=3,
            grid=(p // tb, ff // tf),
            in_specs=[pl.BlockSpec(memory_space=pl.ANY),
                      pl.BlockSpec((1, d, tf), lambda i, f, be, nu, sa: (be[i], 0, f)),
                      pl.BlockSpec((1, d, tf), lambda i, f, be, nu, sa: (be[i], 0, f)),
                      pl.BlockSpec((1, tf, d), lambda i, f, be, nu, sa: (be[i], f, 0))],
            out_specs=pl.BlockSpec(memory_space=pl.ANY),
            scratch_shapes=[pltpu.VMEM((2, tb, d), F32), pltpu.VMEM((2, tb, d), F32),
                            pltpu.VMEM((tb, d), BF16), pltpu.VMEM((tb, d), F32),
                            pltpu.SemaphoreType.DMA((2,)), pltpu.SemaphoreType.DMA((2,))]),
        compiler_params=pltpu.CompilerParams(dimension_semantics=("arbitrary", "arbitrary"),
                                             vmem_limit_bytes=VMEM_LIMIT, has_side_effects=True),
        name="moe_experts",
    )(blk_expert, n_used, slot_assign, h, wg, wu, wd)


def _mix_kernel(y0_ref, y1_ref, x_ref, p_ref, mod_ref, o_ref):
    w = p_ref[...]
    f = w[:, 0:1] * y0_ref[...] + w[:, 1:2] * y1_ref[...]
    o_ref[0] = x_ref[0] + mod_ref[0, 5:6, :] * f


def moe_mix(y2, x, top_p, mod, td=512):
    b, s, d = x.shape
    spb = s // td
    nt = b * spb
    return pl.pallas_call(
        _mix_kernel,
        out_shape=jax.ShapeDtypeStruct((b, s, d), F32),
        grid=(b, spb),
        in_specs=[pl.BlockSpec((td, d), lambda i, j: (i * spb + j, 0)),
                  pl.BlockSpec((td, d), lambda i, j: (nt + i * spb + j, 0)),
                  pl.BlockSpec((1, td, d), lambda i, j: (i, j, 0)),
                  pl.BlockSpec((td, N_EXPERTS), lambda i, j: (i * spb + j, 0)),
                  pl.BlockSpec((1, 6, d), lambda i, j: (i, 0, 0))],
        out_specs=pl.BlockSpec((1, td, d), lambda i, j: (i, j, 0)),
        compiler_params=_params("parallel", "parallel"),
        name="moe_mix",
    )(y2, y2, x, top_p, mod)


def _pack_w_in(w_in):
    kv = 2 * NSA_KV_HEADS * HEAD_DIM * 3
    w_in = to_bf16(w_in)
    o = 0
    nsa_q = w_in[:, o:o + MIX_WIDTH]; o += MIX_WIDTH
    nsa_kv = w_in[:, o:o + kv]; o += kv
    nsa_gate = w_in[:, o:o + NSA_HEADS * N_BRANCH]; o += NSA_HEADS * N_BRANCH
    sb = w_in[:, o:o + 3 * MIX_WIDTH]; o += 3 * MIX_WIDTH
    ml_qkv = w_in[:, o:o + 3 * MIX_WIDTH]; o += 3 * MIX_WIDTH
    ml_if = w_in[:, o:o + 2 * ML_HEADS]; o += 2 * ML_HEADS
    ml_o = w_in[:, o:o + MIX_WIDTH]; o += MIX_WIDTH
    merge = w_in[:, o:]
    main = jnp.concatenate([merge, nsa_q, ml_qkv, ml_o, sb, nsa_kv], axis=1)
    small = jnp.concatenate([nsa_gate, ml_if], axis=1)
    small = jnp.pad(small, ((0, 0), (0, N_SMALL - small.shape[1])))
    return main, small


def token_mixer_layer(x, mod, norm_g, w_in, nsa_q_norm, nsa_k_norm, cmp_pos, cmp_w1, cmp_b1, cmp_w2,
                      cmp_b2, ml_conv_w, ml_conv_b, ml_gate_b, w_branch, w_out):
    b, s, d = x.shape
    t = b * s
    h = normmod(x, norm_g, mod, 0, 1).reshape(t, d)
    w_main, w_small = _pack_w_in(w_in)
    y3 = matmul(h, w_main, BF16, 512, N_MAIN // 2).reshape(b, s, N_MAIN)
    small3 = matmul(h, w_small, F32, 1024, N_SMALL).reshape(b, s, N_SMALL)
    o_sb = sb_attention(y3)
    o_ml = mlstm(y3, small3, ml_conv_w, ml_conv_b, ml_gate_b)
    normed = nsa_headnorm(y3, nsa_q_norm, nsa_k_norm)
    kc, vc = nsa_compress(y3, cmp_pos, cmp_w1, cmp_b1, cmp_w2, cmp_b2, nsa_k_norm[0])
    o_nsa = nsa_attention(y3, small3, normed, kc, vc)
    return merge_project(o_nsa, o_sb, o_ml, y3, w_branch, w_out, x, mod)


def kernel(x, c, ada_w, ada_b, norm_mix, norm_ffn, w_in, nsa_q_norm, nsa_k_norm, cmp_pos, cmp_w1, cmp_b1,
           cmp_w2, cmp_b2, ml_conv_w, ml_conv_b, ml_gate_b, w_branch, w_out, ffn_wg, ffn_wu, ffn_wd,
           moe_router, moe_wg, moe_wu, moe_wd):
    depth = ada_w.shape[0]
    b, s, d = x.shape
    mods = adaln(c, ada_w, ada_b).reshape(depth, b, 6, d)
    for layer in range(depth):
        mod = mods[layer]
        x = token_mixer_layer(x, mod, norm_mix[layer], w_in[layer], nsa_q_norm[layer], nsa_k_norm[layer],
                              cmp_pos[layer], cmp_w1[layer], cmp_b1[layer], cmp_w2[layer], cmp_b2[layer],
                              ml_conv_w[layer], ml_conv_b[layer], ml_gate_b[layer], w_branch[layer],
                              w_out[layer])
        j = layer // 2
        if layer % 2 == 0:
            x = dense_ffn(x, norm_ffn[layer], mod, ffn_wg[j], ffn_wu[j], ffn_wd[j])
        else:
            x = moe_ffn(x, norm_ffn[layer], mod, moe_router[j], moe_wg[j], moe_wu[j], moe_wd[j])
    return x
```

```python
import functools

import numpy as np
import jax
import jax.numpy as jnp
from jax import lax
from jax.experimental import pallas as pl
from jax.experimental.pallas import tpu as pltpu

F32 = jnp.float32
BF16 = jnp.bfloat16
I32 = jnp.int32
HIGHEST = lax.Precision.HIGHEST

EPS = 1e-6
NEG = -1e30
HEAD_DIM = 64
MIX_WIDTH = 512
NSA_HEADS = 8
NSA_KV_HEADS = 2
NSA_GROUP = NSA_HEADS // NSA_KV_HEADS
CMP_BLOCK = 32
CMP_STRIDE = 16
SEL_BLOCK = 64
SEL_TOPK = 16
WINDOW = 512
FORCE_BONUS = 1e4
ML_HEADS = 4
ML_HEAD_DIM = 128
ML_CHUNK = 64
CONV_WIDTH = 4
N_BRANCH = 3
N_EXPERTS = 8
TOP_K = 2
LANES = 128

C_MERGE = 0
C_NSA_Q = 3072
C_ML_Q = 3584
C_ML_K = 4096
C_ML_V = 4608
C_ML_O = 5120
C_SB_Q = 5632
C_SB_K = 6144
C_SB_V = 6656
C_NSA_KV = 7168
N_MAIN = 7936
S_NSA_GATE = 0
S_ML_I = 24
S_ML_F = 28
N_SMALL = 128

VMEM_LIMIT = 56 * 1024 * 1024


def _params(*sem):
    return pltpu.CompilerParams(dimension_semantics=sem, vmem_limit_bytes=VMEM_LIMIT)


def _iota(shape, dim):
    return lax.broadcasted_iota(I32, shape, dim)


def _split_dot(a32, b_bf16):
    hi = a32.astype(BF16)
    lo = (a32 - hi.astype(F32)).astype(BF16)
    return (jnp.dot(hi, b_bf16, preferred_element_type=F32)
            + jnp.dot(lo, b_bf16, preferred_element_type=F32))


def _split_dot_left(a_bf16, b32):
    hi = b32.astype(BF16)
    lo = (b32 - hi.astype(F32)).astype(BF16)
    return (jnp.dot(a_bf16, hi, preferred_element_type=F32)
            + jnp.dot(a_bf16, lo, preferred_element_type=F32))


def _dot_nt(a, b):
    return lax.dot_general(a, b, (((1,), (1,)), ((), ())), preferred_element_type=F32)


def _log_sigmoid(z):
    return jnp.minimum(z, 0.0) - jnp.log1p(jnp.exp(-jnp.abs(z)))


def _cast_kernel(x_ref, o_ref):
    o_ref[...] = x_ref[...].astype(o_ref.dtype)


def to_bf16(w, max_rows=512):
    cols = w.shape[-1]
    w2 = w.reshape(-1, cols)
    rows = w2.shape[0]
    tr = max(t for t in range(8, max_rows + 1, 8) if rows % t == 0)
    out = pl.pallas_call(
        _cast_kernel,
        out_shape=jax.ShapeDtypeStruct((rows, cols), BF16),
        grid=(rows // tr,),
        in_specs=[pl.BlockSpec((tr, cols), lambda i: (i, 0))],
        out_specs=pl.BlockSpec((tr, cols), lambda i: (i, 0)),
        compiler_params=_params("parallel"),
        name="to_bf16",
    )(w2)
    return out.reshape(w.shape)


def _adaln_kernel(c_ref, w_ref, b_ref, o_ref):
    c = c_ref[...]
    cond = c * jax.nn.sigmoid(c)
    o_ref[0] = jnp.dot(cond, w_ref[0], precision=HIGHEST, preferred_element_type=F32) + b_ref[0]


def adaln(c, ada_w, ada_b):
    depth, d, n = ada_w.shape
    b = c.shape[0]
    tn = 1536
    return pl.pallas_call(
        _adaln_kernel,
        out_shape=jax.ShapeDtypeStruct((depth, b, n), F32),
        grid=(depth, n // tn),
        in_specs=[pl.BlockSpec((b, d), lambda l, j: (0, 0)),
                  pl.BlockSpec((1, d, tn), lambda l, j: (l, 0, j)),
                  pl.BlockSpec((1, 1, tn), lambda l, j: (l, 0, j))],
        out_specs=pl.BlockSpec((1, b, tn), lambda l, j: (l, 0, j)),
        compiler_params=_params("parallel", "parallel"),
        name="adaln",
    )(c, ada_w, ada_b.reshape(depth, 1, n))


def _norm_mod(x, g, mod, shift_row, scale_row):
    ms = jnp.mean(x * x, axis=-1, keepdims=True)
    y = x * lax.rsqrt(ms + EPS) * g
    return y * (1.0 + mod[scale_row:scale_row + 1, :]) + mod[shift_row:shift_row + 1, :]


def _normmod_kernel(x_ref, g_ref, mod_ref, o_ref, *, shift_row, scale_row):
    o_ref[0] = _norm_mod(x_ref[0], g_ref[...], mod_ref[0], shift_row, scale_row).astype(o_ref.dtype)


def normmod(x, g, mod, shift_row, scale_row, ts=512):
    b, s, d = x.shape
    return pl.pallas_call(
        functools.partial(_normmod_kernel, shift_row=shift_row, scale_row=scale_row),
        out_shape=jax.ShapeDtypeStruct((b, s, d), BF16),
        grid=(b, s // ts),
        in_specs=[pl.BlockSpec((1, ts, d), lambda i, j: (i, j, 0)),
                  pl.BlockSpec((1, d), lambda i, j: (0, 0)),
                  pl.BlockSpec((1, 6, d), lambda i, j: (i, 0, 0))],
        out_specs=pl.BlockSpec((1, ts, d), lambda i, j: (i, j, 0)),
        compiler_params=_params("parallel", "parallel"),
        name="normmod",
    )(x, g.reshape(1, d), mod)


def _in_proj_kernel(x_ref, g_ref, mod_ref, wm_ref, ws_ref, y_ref, sm_ref, h_ref):
    @pl.when(pl.program_id(2) == 0)
    def _():
        h_ref[...] = _norm_mod(x_ref[0], g_ref[...], mod_ref[0], 0, 1).astype(BF16)
        sm_ref[0] = jnp.dot(h_ref[...], ws_ref[...], preferred_element_type=F32)

    y_ref[0] = jnp.dot(h_ref[...], wm_ref[...], preferred_element_type=F32).astype(y_ref.dtype)


def in_projection(x, g, mod, w_main, w_small, tm=512, n_tiles=2):
    b, s, d = x.shape
    tn = N_MAIN // n_tiles
    return pl.pallas_call(
        _in_proj_kernel,
        out_shape=(jax.ShapeDtypeStruct((b, s, N_MAIN), BF16), jax.ShapeDtypeStruct((b, s, N_SMALL), F32)),
        grid=(b, s // tm, n_tiles),
        in_specs=[pl.BlockSpec((1, tm, d), lambda i, j, n: (i, j, 0)),
                  pl.BlockSpec((1, d), lambda i, j, n: (0, 0)),
                  pl.BlockSpec((1, 6, d), lambda i, j, n: (i, 0, 0)),
                  pl.BlockSpec((d, tn), lambda i, j, n: (0, n)),
                  pl.BlockSpec((d, N_SMALL), lambda i, j, n: (0, 0))],
        out_specs=(pl.BlockSpec((1, tm, tn), lambda i, j, n: (i, j, n)),
                   pl.BlockSpec((1, tm, N_SMALL), lambda i, j, n: (i, j, 0))),
        scratch_shapes=[pltpu.VMEM((tm, d), BF16)],
        compiler_params=_params("parallel", "parallel", "arbitrary"),
        name="in_projection",
    )(x, g.reshape(1, d), mod, w_main, w_small)


def _mm_kernel(a_ref, w_ref, o_ref):
    o_ref[...] = jnp.dot(a_ref[...], w_ref[...], preferred_element_type=F32).astype(o_ref.dtype)


def matmul(a, w, out_dtype, tm, tn):
    m, k = a.shape
    n = w.shape[1]
    return pl.pallas_call(
        _mm_kernel,
        out_shape=jax.ShapeDtypeStruct((m, n), out_dtype),
        grid=(m // tm, n // tn),
        in_specs=[pl.BlockSpec((tm, k), lambda i, j: (i, 0)),
                  pl.BlockSpec((k, tn), lambda i, j: (0, j))],
        out_specs=pl.BlockSpec((tm, tn), lambda i, j: (i, j)),
        compiler_params=_params("parallel", "parallel"),
        name="matmul",
    )(a, w)


SB_EXP_FLOOR = -104.0


def _sb_kernel(q_ref, k_ref, v_ref, o_ref, vt_ref, *, tq):
    TK, dh = LANES, HEAD_DIM
    n_sub = tq // TK
    W = 2 * tq
    qi = pl.program_id(2)
    q0 = qi * tq

    @pl.when(qi == 0)
    def _():
        for c in range(v_ref.shape[1] // tq):
            vt_ref[c] = v_ref[0, c * tq:(c + 1) * tq, :].astype(F32).T.astype(BF16)

    q_t = (q_ref[0].astype(F32) * (dh ** -0.5)).T
    chan = _iota((2 * dh, tq), 0)
    q_cat = jnp.concatenate([jnp.where(chan < dh, q_t, 0.0), jnp.where(chan < dh, 0.0, q_t)],
                            axis=1).astype(BF16)
    later = (_iota((TK, TK), 0) < _iota((TK, TK), 1)).astype(BF16)
    suffix = jnp.concatenate([jnp.concatenate([later, later], axis=1), jnp.ones((8, 2 * TK), BF16)], axis=0)

    def step(j, masked, st):
        carry, acc = st
        k0 = pl.multiple_of(j * tq, tq)
        z = jnp.dot(k_ref[0, pl.ds(k0, tq), :], q_cat, preferred_element_type=F32)
        lk = -(jnp.maximum(z, 0.0) + jnp.log(1.0 + jnp.exp(-jnp.abs(z))))
        ls = lk + z
        if masked:
            strict = (k0 + _iota((tq, W), 0)) < (q0 + (_iota((tq, W), 1) & (tq - 1)))
            lk = jnp.where(strict, lk, 0.0)
        hi = lk.astype(BF16)
        lo = (lk - hi.astype(F32)).astype(BF16)
        after = [None] * n_sub
        for sub in range(n_sub - 1, -1, -1):
            rows = slice(sub * TK, (sub + 1) * TK)
            res = jnp.dot(suffix, jnp.concatenate([hi[rows], lo[rows]], axis=0),
                          preferred_element_type=F32)
            after[sub] = res[0:TK] + carry
            carry = carry + res[TK:TK + 1]
        a = jnp.exp(ls + jnp.concatenate(after, axis=0))
        if masked:
            a = jnp.where(strict, a, 0.0)
        acc = acc + jnp.dot(vt_ref[j], a.astype(BF16), preferred_element_type=F32)
        return carry, acc

    def cond(st):
        return (st[0] >= 0) & (jnp.max(st[1]) > SB_EXP_FLOOR)

    def body(st):
        return (st[0] - 1,) + step(st[0], False, st[1:])

    st = step(qi, True, (jnp.zeros((1, W), F32), jnp.zeros((2 * dh, W), F32)))
    _, _, acc = lax.while_loop(cond, body, (qi - 1,) + st)
    o_ref[0] = jnp.concatenate([acc[0:dh, 0:tq], acc[dh:2 * dh, tq:W]], axis=0).T.astype(o_ref.dtype)


def sb_attention(y3, tq=256):
    b, s, _ = y3.shape
    n_pairs = MIX_WIDTH // LANES
    qb, kb, vb = C_SB_Q // LANES, C_SB_K // LANES, C_SB_V // LANES
    return pl.pallas_call(
        functools.partial(_sb_kernel, tq=tq),
        out_shape=jax.ShapeDtypeStruct((b, s, MIX_WIDTH), BF16),
        grid=(b, n_pairs, s // tq),
        in_specs=[pl.BlockSpec((1, tq, LANES), lambda i, p, j: (i, j, qb + p)),
                  pl.BlockSpec((1, s, LANES), lambda i, p, j: (i, 0, kb + p)),
                  pl.BlockSpec((1, s, LANES), lambda i, p, j: (i, 0, vb + p))],
        out_specs=pl.BlockSpec((1, tq, LANES), lambda i, p, j: (i, j, p)),
        scratch_shapes=[pltpu.VMEM((s // tq, LANES, tq), BF16)],
        compiler_params=_params("parallel", "parallel", "arbitrary"),
        name="sb_attention",
    )(y3, y3, y3)


def _mlstm_kernel(q_ref, k_ref, v_ref, og_ref, sm_ref, gr_ref, cw_ref, cb_ref, gb_ref, out_ref,
                  ct_ref, n_ref, m_ref, xbuf_ref, qk_ref, *, ts):
    L, dh, H, W = ML_CHUNK, ML_HEAD_DIM, ML_HEADS, MIX_WIDTH
    halo = 8
    sblk = pl.program_id(1)

    @pl.when(sblk == 0)
    def _():
        ct_ref[...] = jnp.zeros_like(ct_ref)
        n_ref[...] = jnp.zeros_like(n_ref)
        m_ref[...] = jnp.zeros_like(m_ref)
        xbuf_ref[0:halo, :] = jnp.zeros((halo, 2 * W), F32)

    @pl.when(sblk > 0)
    def _():
        xbuf_ref[0:halo, :] = xbuf_ref[ts:ts + halo, :]

    xbuf_ref[halo:halo + ts, 0:W] = q_ref[0].astype(F32)
    xbuf_ref[halo:halo + ts, W:2 * W] = k_ref[0].astype(F32)
    conv = cb_ref[...] + jnp.zeros((ts, 2 * W), F32)
    for j in range(CONV_WIDTH):
        off = halo - (CONV_WIDTH - 1) + j
        conv = conv + cw_ref[j:j + 1, :] * xbuf_ref[off:off + ts, :]
    act = conv * jax.nn.sigmoid(conv)
    qk_ref[:, 0:W] = (act[:, 0:W] * (dh ** -0.5)).astype(BF16)
    qk_ref[:, W:2 * W] = act[:, W:2 * W].astype(BF16)

    it0, it1 = _iota((L, L), 0), _iota((L, L), 1)
    causal = it0 >= it1
    tri_lo = causal.astype(BF16)
    tri_up = (it0 <= it1).astype(BF16)

    def chunk(c, carry):
        r0 = pl.multiple_of(c * L, L)
        sm = sm_ref[0, pl.ds(r0, L), :]
        gr = gr_ref[0, c]
        HR = range(H)
        rows = pl.ds(r0, L)
        cols = [slice(h * dh, (h + 1) * dh) for h in HR]
        ig_col = [sm[:, S_ML_I + h:S_ML_I + h + 1] + gb_ref[0, h] for h in HR]
        lf_col = [_log_sigmoid(sm[:, S_ML_F + h:S_ML_F + h + 1] + gb_ref[1, h]) for h in HR]
        ig_row = [gr[h:h + 1, :] + gb_ref[0, h] for h in HR]
        lf_row = [_log_sigmoid(gr[H + h:H + h + 1, :] + gb_ref[1, h]) for h in HR]
        b_t = [_split_dot_left(tri_lo, jnp.broadcast_to(lf_col[h], (L, L))) for h in HR]
        b_s = [_split_dot(jnp.broadcast_to(lf_row[h], (L, L)), tri_up) for h in HR]
        qq = [qk_ref[rows, cols[h]] for h in HR]
        kk = [qk_ref[rows, W + h * dh:W + (h + 1) * dh] for h in HR]
        vv = [v_ref[0, rows, cols[h]] for h in HR]
        ct = [ct_ref[h] for h in HR]
        nvec = [n_ref[h] for h in HR]
        m_prev = [m_ref[h][:, 0:1] for h in HR]
        qk = [_dot_nt(qq[h], kk[h]) for h in HR]
        q_c = [jnp.dot(qq[h], ct[h].astype(BF16), preferred_element_type=F32) for h in HR]
        kt = [kk[h].astype(F32).T.astype(BF16) for h in HR]
        dmat = [jnp.where(causal, b_t[h] - b_s[h] + ig_row[h], NEG) for h in HR]
        b_col = [b_t[h][:, 0:1] for h in HR]
        m_inter = [b_col[h] + m_prev[h] for h in HR]
        m_t = [jnp.maximum(m_inter[h], jnp.max(dmat[h], axis=1, keepdims=True)) for h in HR]
        w = [jnp.exp(dmat[h] - m_t[h]) * qk[h] for h in HR]
        inter = [jnp.exp(m_inter[h] - m_t[h]) for h in HR]
        w_v = [jnp.dot(w[h].astype(BF16), vv[h], preferred_element_type=F32) for h in HR]
        b_last = [b_t[h][L - 1:L, 0:1] for h in HR]
        decay = [b_last[h] - b_col[h] + ig_col[h] for h in HR]
        m_new = [jnp.maximum(b_last[h] + m_prev[h], jnp.max(decay[h], axis=0, keepdims=True)) for h in HR]
        ws = [jnp.exp(decay[h] - m_new[h]) for h in HR]
        cscale = [jnp.exp(b_last[h] + m_prev[h] - m_new[h]) for h in HR]
        wv = [(ws[h] * vv[h].astype(F32)).astype(BF16) for h in HR]
        k_wv = [jnp.dot(kt[h], wv[h], preferred_element_type=F32) for h in HR]
        for h in HR:
            num = inter[h] * q_c[h] + w_v[h]
            den = (inter[h] * jnp.sum(qq[h].astype(F32) * nvec[h], axis=1, keepdims=True)
                   + jnp.sum(w[h], axis=1, keepdims=True))
            hval = num / jnp.maximum(jnp.abs(den), jnp.exp(-m_t[h]))
            ct_ref[h] = cscale[h] * ct[h] + k_wv[h]
            n_ref[h] = cscale[h] * nvec[h] + jnp.sum(ws[h] * kk[h].astype(F32), axis=0, keepdims=True)
            m_ref[h] = jnp.broadcast_to(m_new[h], (1, LANES))
            gate = jax.nn.sigmoid(og_ref[0, rows, cols[h]].astype(F32))
            out_ref[0, rows, cols[h]] = (gate * hval).astype(out_ref.dtype)
        return carry

    lax.fori_loop(0, ts // L, chunk, 0)


def mlstm(y3, small3, conv_w, conv_b, gate_b, ts=512):
    b, s, _ = y3.shape
    W, H, L = MIX_WIDTH, ML_HEADS, ML_CHUNK
    gr = small3[:, :, S_ML_I:S_ML_I + 2 * H].reshape(b, s // L, L, 2 * H).transpose(0, 1, 3, 2)
    cq, ck, cv, co = C_ML_Q // W, C_ML_K // W, C_ML_V // W, C_ML_O // W
    return pl.pallas_call(
        functools.partial(_mlstm_kernel, ts=ts),
        out_shape=jax.ShapeDtypeStruct((b, s, W), BF16),
        grid=(b, s // ts),
        in_specs=[pl.BlockSpec((1, ts, W), lambda i, j: (i, j, cq)),
                  pl.BlockSpec((1, ts, W), lambda i, j: (i, j, ck)),
                  pl.BlockSpec((1, ts, W), lambda i, j: (i, j, cv)),
                  pl.BlockSpec((1, ts, W), lambda i, j: (i, j, co)),
                  pl.BlockSpec((1, ts, N_SMALL), lambda i, j: (i, j, 0)),
                  pl.BlockSpec((1, ts // L, 2 * H, L), lambda i, j: (i, j, 0, 0)),
                  pl.BlockSpec((CONV_WIDTH, 2 * W), lambda i, j: (0, 0)),
                  pl.BlockSpec((1, 2 * W), lambda i, j: (0, 0)),
                  pl.BlockSpec(memory_space=pltpu.SMEM)],
        out_specs=pl.BlockSpec((1, ts, W), lambda i, j: (i, j, 0)),
        scratch_shapes=[pltpu.VMEM((H, ML_HEAD_DIM, ML_HEAD_DIM), F32),
                        pltpu.VMEM((H, 1, ML_HEAD_DIM), F32),
                        pltpu.VMEM((H, 1, LANES), F32),
                        pltpu.VMEM((ts + 8, 2 * W), F32),
                        pltpu.VMEM((ts, 2 * W), BF16)],
        compiler_params=_params("parallel", "arbitrary"),
        name="mlstm",
    )(y3, y3, y3, y3, small3, gr, conv_w, conv_b.reshape(1, 2 * W), gate_b)


def _headnorm_kernel(x_ref, g_ref, o_ref):
    x = x_ref[0].astype(F32)
    same_head = (_iota((LANES, LANES), 0) // HEAD_DIM == _iota((LANES, LANES), 1) // HEAD_DIM)
    ss = _split_dot(x * x, same_head.astype(BF16))
    o_ref[0] = (x * lax.rsqrt(ss * (1.0 / HEAD_DIM) + EPS) * g_ref[0]).astype(o_ref.dtype)


def nsa_headnorm(y3, q_norm, k_norm, ts=1024):
    b, s, _ = y3.shape
    qb = C_NSA_Q // LANES
    ksb = C_NSA_KV // LANES + 2
    kwb = C_NSA_KV // LANES + 4
    gains = jnp.stack([jnp.tile(q_norm, 2)] * 4 + [jnp.tile(k_norm[1], 2), jnp.tile(k_norm[2], 2)])

    def col(j):
        return jnp.where(j < 4, qb + j, jnp.where(j == 4, ksb, kwb))

    return pl.pallas_call(
        _headnorm_kernel,
        out_shape=jax.ShapeDtypeStruct((b, s, 6 * LANES), BF16),
        grid=(b, s // ts, 6),
        in_specs=[pl.BlockSpec((1, ts, LANES), lambda i, t, j: (i, t, col(j))),
                  pl.BlockSpec((1, 1, LANES), lambda i, t, j: (j, 0, 0))],
        out_specs=pl.BlockSpec((1, ts, LANES), lambda i, t, j: (i, t, j)),
        compiler_params=_params("parallel", "parallel", "parallel"),
        name="nsa_headnorm",
    )(y3, gains.reshape(6, 1, LANES))


def _gelu_tanh(x):
    return 0.5 * x * (1.0 + jnp.tanh(0.7978845608028654 * (x + 0.044715 * (x * x * x))))


def _compress_kernel(ra_ref, rb_ref, pos_ref, w1_ref, b1_ref, w2_ref, b2_ref, kn_ref, kc_ref, vc_ref):
    half = (CMP_BLOCK // 2) * HEAD_DIM
    for j, o_ref in enumerate((kc_ref, vc_ref)):
        xa = (ra_ref[j, 0, 0].astype(F32) + pos_ref[j, :, 0:half]).astype(BF16)
        xb = (rb_ref[j, 0, 0].astype(F32) + pos_ref[j, :, half:2 * half]).astype(BF16)
        hid = (jnp.dot(xa, w1_ref[j, 0:half, :], preferred_element_type=F32)
               + jnp.dot(xb, w1_ref[j, half:2 * half, :], preferred_element_type=F32) + b1_ref[j])
        out = jnp.dot(_gelu_tanh(hid).astype(BF16), w2_ref[j], preferred_element_type=F32) + b2_ref[j]
        if j == 0:
            out = out * lax.rsqrt(jnp.mean(out * out, axis=-1, keepdims=True) + EPS) * kn_ref[...]
        o_ref[0, 0] = out


def nsa_compress(y3, cmp_pos, cmp_w1, cmp_b1, cmp_w2, cmp_b2, k_norm0):
    b, s, _ = y3.shape
    G, dh = NSA_KV_HEADS, HEAD_DIM
    nr = s // CMP_STRIDE
    wide = CMP_STRIDE * dh
    kv = y3[:, :, C_NSA_KV:C_NSA_KV + 2 * G * dh].reshape(b, s, 2, G, dh)
    ra = kv.transpose(2, 0, 3, 1, 4).reshape(2, b, G, nr, wide)
    rb = jnp.concatenate([ra[:, :, :, 1:], jnp.zeros((2, b, G, 1, wide), ra.dtype)], axis=3)
    hidden = cmp_w1.shape[-1]
    out = jax.ShapeDtypeStruct((b, G, nr, dh), F32)
    blk = pl.BlockSpec((2, 1, 1, nr, wide), lambda i, g: (0, i, g, 0, 0))
    oblk = pl.BlockSpec((1, 1, nr, dh), lambda i, g: (i, g, 0, 0))

    def full(shape):
        return pl.BlockSpec(shape, lambda i, g: (0,) * len(shape))

    return pl.pallas_call(
        _compress_kernel,
        out_shape=(out, out),
        grid=(b, G),
        in_specs=[blk, blk, full((2, 1, 2 * wide)), full((2, 2 * wide, hidden)), full((2, 1, hidden)),
                  full((2, hidden, dh)), full((2, 1, dh)), full((1, dh))],
        out_specs=(oblk, oblk),
        compiler_params=_params("parallel", "parallel"),
        name="nsa_compress",
    )(ra, rb, cmp_pos.reshape(2, 1, 2 * wide), cmp_w1.astype(BF16), cmp_b1.reshape(2, 1, hidden),
      cmp_w2.astype(BF16), cmp_b2.reshape(2, 1, dh), k_norm0.reshape(1, dh))


def _nsa_kernel_rowmajor(q_ref, gl_ref, kc_ref, vc_ref, ks_ref, vs_ref, kw_ref, vw_ref, o_ref, *, n_sel, top):
    QB, R, dh = 128, NSA_GROUP, HEAD_DIM
    rows = R * QB
    g = pl.program_id(1)
    qi = pl.program_id(2)
    q0 = qi * QB
    q = q_ref[0, 0, 0] * jnp.asarray(dh ** -0.5, BF16)
    row = _iota((rows, 1), 0)
    t_f = (q0 + (row & (QB - 1))).astype(F32)
    head = g * R + (row >> 7)
    slope = jnp.exp2(-(head + 1).astype(F32))

    n_cmp = kc_ref.shape[2]
    kc = kc_ref[0, 0]
    kc_hi = kc.astype(BF16)
    kc_lo = (kc - kc_hi.astype(F32)).astype(BF16)
    cmp_end = (_iota((1, n_cmp), 1) * CMP_STRIDE + (CMP_BLOCK - 1)).astype(F32)
    dist = t_f - cmp_end
    valid = dist >= 0.0
    s = jnp.where(valid, _dot_nt(q, kc_hi) + _dot_nt(q, kc_lo) - slope * dist, NEG)
    e = jnp.exp(s - jnp.max(s, axis=1, keepdims=True))
    p = jnp.where(valid, e / jnp.sum(e, axis=1, keepdims=True), 0.0)
    o_cmp = jnp.dot(p.astype(BF16), vc_ref[0, 0].astype(BF16), preferred_element_type=F32)

    p_grp = p[0:QB] + p[QB:2 * QB] + p[2 * QB:3 * QB] + p[3 * QB:4 * QB]
    c0 = _iota((n_cmp, n_sel), 0) * CMP_STRIDE
    s0 = _iota((n_cmp, n_sel), 1) * SEL_BLOCK
    overlap = ((c0 < s0 + SEL_BLOCK) & (c0 + CMP_BLOCK > s0)).astype(BF16)
    imp = _split_dot(p_grp, overlap)
    tq = q0 + _iota((QB, 1), 0)
    j_idx = _iota((1, n_sel), 1)
    cur = tq >> 6
    forced = (j_idx == 0) | (j_idx == cur) | (j_idx == cur - 1)
    imp = jnp.where(j_idx * SEL_BLOCK <= tq, imp + jnp.where(forced, FORCE_BONUS, 0.0), -1.0)
    sel = jnp.zeros((QB, n_sel), F32)
    for _ in range(top):
        mx = jnp.max(imp, axis=1, keepdims=True)
        first = jnp.min(jnp.where(imp == mx, j_idx, n_sel), axis=1, keepdims=True)
        pick = j_idx == first
        sel = jnp.where(pick, 1.0, sel)
        imp = jnp.where(pick, -3e38, imp)
    sel_bf = sel.astype(BF16)

    def flash(k_ref, v_ref, lo, hi, mask_fn):
        def body(kb, st):
            m_run, l_run, acc = st
            k0 = pl.multiple_of(kb * QB, QB)
            kblk = k_ref[0, 0, pl.ds(k0, QB), :]
            vblk = v_ref[0, 0, pl.ds(k0, QB), :]
            dist = t_f - (k0 + _iota((1, QB), 1)).astype(F32)
            ok = mask_fn(kb, dist)
            sc = jnp.where(ok, _dot_nt(q, kblk) - slope * dist, NEG)
            m_new = jnp.maximum(m_run, jnp.max(sc, axis=1, keepdims=True))
            alpha = jnp.exp(m_run - m_new)
            pr = jnp.where(ok, jnp.exp(sc - m_new), 0.0)
            l_run = alpha * l_run + jnp.sum(pr, axis=1, keepdims=True)
            acc = alpha * acc + jnp.dot(pr.astype(BF16), vblk, preferred_element_type=F32)
            return m_new, l_run, acc

        init = (jnp.full((rows, 1), NEG, F32), jnp.zeros((rows, 1), F32), jnp.zeros((rows, dh), F32))
        _, l_run, acc = lax.fori_loop(lo, hi, body, init)
        return acc / l_run

    def sel_mask(kb, dist):
        expand = (_iota((n_sel, QB), 0) == 2 * kb + (_iota((n_sel, QB), 1) >> 6)).astype(BF16)
        m = jnp.dot(sel_bf, expand, preferred_element_type=F32)
        m = jnp.concatenate([m] * R, axis=0)
        return (m > 0.5) & (dist >= 0.0)

    def win_mask(kb, dist):
        return (dist >= 0.0) & (dist < float(WINDOW))

    o_sel = flash(ks_ref, vs_ref, 0, qi + 1, sel_mask)
    o_win = flash(kw_ref, vw_ref, jnp.maximum(qi - WINDOW // QB, 0), qi + 1, win_mask)

    gate = jax.nn.sigmoid(gl_ref[0, 0, 0])
    o_ref[0, 0, 0] = (gate[:, 0:1] * o_cmp + gate[:, 1:2] * o_sel + gate[:, 2:3] * o_win).astype(o_ref.dtype)


def nsa_attention_rowmajor(y3, small3, normed, kc, vc):
    b, s, _ = y3.shape
    G, R, dh, QB = NSA_KV_HEADS, NSA_GROUP, HEAD_DIM, 128
    nq = s // QB
    n_sel = s // SEL_BLOCK
    top = min(SEL_TOPK, n_sel)

    def stack_heads(a, width):
        return (a.reshape(b, nq, QB, G, R, width).transpose(0, 3, 1, 4, 2, 5)
                .reshape(b, G, nq, R * QB, width))

    def kv_heads(a):
        return a.reshape(b, s, G, dh).transpose(0, 2, 1, 3)

    q = stack_heads(normed[:, :, 0:MIX_WIDTH], dh)
    gl = stack_heads(small3[:, :, S_NSA_GATE:S_NSA_GATE + NSA_HEADS * N_BRANCH], N_BRANCH)
    ks = kv_heads(normed[:, :, 4 * LANES:5 * LANES])
    kw = kv_heads(normed[:, :, 5 * LANES:6 * LANES])
    vs = kv_heads(y3[:, :, C_NSA_KV + 3 * LANES:C_NSA_KV + 4 * LANES])
    vw = kv_heads(y3[:, :, C_NSA_KV + 5 * LANES:C_NSA_KV + 6 * LANES])
    n_cmp = kc.shape[2]
    qspec = pl.BlockSpec((1, 1, 1, R * QB, dh), lambda i, g, j: (i, g, j, 0, 0))
    gspec = pl.BlockSpec((1, 1, 1, R * QB, N_BRANCH), lambda i, g, j: (i, g, j, 0, 0))
    cspec = pl.BlockSpec((1, 1, n_cmp, dh), lambda i, g, j: (i, g, 0, 0))
    kvspec = pl.BlockSpec((1, 1, s, dh), lambda i, g, j: (i, g, 0, 0))
    out = pl.pallas_call(
        functools.partial(_nsa_kernel_rowmajor, n_sel=n_sel, top=top),
        out_shape=jax.ShapeDtypeStruct((b, G, nq, R * QB, dh), BF16),
        grid=(b, G, nq),
        in_specs=[qspec, gspec, cspec, cspec, kvspec, kvspec, kvspec, kvspec],
        out_specs=qspec,
        compiler_params=_params("parallel", "parallel", "parallel"),
        name="nsa_attention",
    )(q, gl, kc, vc, ks, vs, kw, vw)
    return (out.reshape(b, G, nq, R, QB, dh).transpose(0, 2, 4, 1, 3, 5).reshape(b, s, MIX_WIDTH))


NSA_QB = 128
NSA_KS = 512
NSA_CK = 256
NSA_VR = 80
A_FEAT, A_PEN, A_BIAS = 0, 64, 128
A_DUMMY = A_BIAS + 4


def _head_rms(x, gain):
    w = x.shape[1]
    same_head = (_iota((w, w), 0) // HEAD_DIM == _iota((w, w), 1) // HEAD_DIM).astype(BF16)
    ss = _split_dot(x * x, same_head)
    return x * lax.rsqrt(ss * (1.0 / HEAD_DIM) + EPS) * gain


def _nsa_kernel(q_ref, gl_ref, kca_ref, vcT_ref, ks_ref, kw_ref, vs_ref, vw_ref, kconst_ref, gq_ref, gk_ref,
                o_ref, qt_ref, ksa_ref, kwa_ref, vsa_ref, vwa_ref, s_ref, *, n_sel, top):
    QB, R, dh, CK, VR, KS = NSA_QB, NSA_GROUP, HEAD_DIM, NSA_CK, NSA_VR, NSA_KS
    HQ = R * QB
    NPAD = WINDOW // QB
    SUB = KS // QB
    g = pl.program_id(1)
    qi = pl.program_id(2)
    q0 = qi * QB
    nkb = vsa_ref.shape[0]

    @pl.when(qi == 0)
    def _():
        ksa_ref[...] = kconst_ref[...]
        kwa_ref[0:WINDOW, :] = jnp.where(_iota((WINDOW, CK), 1) == A_DUMMY, 1.0, 0.0).astype(BF16)
        kwa_ref[WINDOW:, :] = kconst_ref[...]
        kwa_ref[WINDOW:, A_PEN:A_PEN + 64] = jnp.zeros((kwa_ref.shape[0] - WINDOW, 64), BF16)
        for gg in range(NSA_KV_HEADS):
            @pl.when(g == gg)
            def _():
                heads = slice(gg * dh, (gg + 1) * dh)
                for c in range(ks_ref.shape[1] // KS):
                    keys = slice(c * KS, (c + 1) * KS)
                    ks_n = _head_rms(ks_ref[0, keys, :].astype(F32), gk_ref[0])
                    kw_n = _head_rms(kw_ref[0, keys, :].astype(F32), gk_ref[1])
                    ksa_ref[keys, A_FEAT:A_FEAT + dh] = ks_n[:, heads].astype(BF16)
                    kwa_ref[WINDOW + c * KS:WINDOW + (c + 1) * KS, A_FEAT:A_FEAT + dh] = kw_n[:, heads].astype(BF16)
                for c in range(nkb):
                    keys = slice(c * QB, (c + 1) * QB)
                    vsa_ref[c, 0:dh, :] = vs_ref[0, keys, :].astype(F32).T[heads].astype(BF16)
                    vwa_ref[NPAD + c, 0:dh, :] = vw_ref[0, keys, :].astype(F32).T[heads].astype(BF16)
        vwa_ref[0:NPAD, 0:dh, :] = jnp.zeros((NPAD, dh, QB), BF16)
        ones_rows = jnp.where(_iota((nkb + NPAD, VR - dh, QB), 1) == 0, 1.0, 0.0).astype(BF16)
        vsa_ref[:, dh:VR, :] = ones_rows[0:nkb]
        vwa_ref[:, dh:VR, :] = ones_rows
        qt_ref[A_BIAS + 16:CK, :] = jnp.zeros((CK - A_BIAS - 16, HQ), BF16)

    q_n = _head_rms(q_ref[0].astype(F32), gq_ref[...]).astype(BF16)
    q_rows = (q_n.astype(F32) * (dh ** -0.5)).T
    qT = jnp.concatenate([q_rows[r * dh:(r + 1) * dh] for r in range(R)], axis=1).astype(BF16)
    qt_ref[A_FEAT:A_FEAT + dh, :] = qT
    qt_ref[A_PEN:A_PEN + dh, :] = qT
    lane = _iota((16, HQ), 1)
    rowi = _iota((16, HQ), 0)
    t_q = q0 + (lane & (QB - 1))
    slope = jnp.exp2(-(g * R + (lane >> 7) + 1).astype(F32))
    t_hi = ((t_q >> 6) << 6).astype(F32)
    t_lo = (t_q & 63).astype(F32)
    bias_rows = jnp.where(rowi < 2, slope,
                          jnp.where(rowi == 2, -slope * t_hi,
                                    jnp.where(rowi == 3, -slope * t_lo,
                                              jnp.where(rowi == A_DUMMY - A_BIAS, NEG, 0.0))))
    qt_ref[A_BIAS:A_BIAS + 16, :] = bias_rows.astype(BF16)
    k_loc = _iota((QB, HQ), 0)
    q_loc = _iota((QB, HQ), 1) & (QB - 1)

    def pv(v_ref_, kb0, pr):
        out = None
        for i in range(pr.shape[0] // QB):
            term = jnp.dot(v_ref_[kb0 + i], pr[i * QB:(i + 1) * QB], preferred_element_type=F32)
            out = term if out is None else out + term
        return out

    n_cmp = kca_ref.shape[2]
    sc = jnp.dot(kca_ref[0, 0], qt_ref[...], preferred_element_type=F32)
    cmp_end = _iota((n_cmp, HQ), 0) * CMP_STRIDE + (CMP_BLOCK - 1)
    valid = cmp_end <= q0 + (_iota((n_cmp, HQ), 1) & (QB - 1))
    sc = jnp.where(valid, sc, NEG)
    e = jnp.exp(sc - jnp.max(sc, axis=0, keepdims=True))
    p = jnp.where(valid, e * (1.0 / jnp.sum(e, axis=0, keepdims=True)), 0.0)
    o_cmp = jnp.dot(vcT_ref[0, 0], p.astype(BF16), preferred_element_type=F32)

    sw = jnp.dot(kwa_ref[pl.ds(pl.multiple_of(q0, QB), WINDOW + QB), :], qt_ref[...],
                 preferred_element_type=F32)
    sw = jnp.concatenate([jnp.where(k_loc > q_loc, sw[0:QB], NEG), sw[QB:WINDOW],
                          jnp.where(k_loc <= q_loc, sw[WINDOW:WINDOW + QB], NEG)], axis=0)
    pw = jnp.exp(sw - jnp.max(sw, axis=0, keepdims=True)).astype(BF16)
    acc_w = pv(vwa_ref, qi, pw)
    o_win = acc_w[0:dh] / acc_w[dh:dh + 1]

    p_grp = p[:, 0:QB] + p[:, QB:2 * QB] + p[:, 2 * QB:3 * QB] + p[:, 3 * QB:4 * QB]
    c0 = _iota((n_sel, n_cmp), 1) * CMP_STRIDE
    s0 = _iota((n_sel, n_cmp), 0) * SEL_BLOCK
    overlap_t = ((c0 < s0 + SEL_BLOCK) & (c0 + CMP_BLOCK > s0)).astype(BF16)
    imp = _split_dot_left(overlap_t, p_grp)
    j_idx = _iota((n_sel, QB), 0)
    tq = q0 + _iota((n_sel, QB), 1)
    cur = tq >> 6
    forced = (j_idx == 0) | (j_idx == cur) | (j_idx == cur - 1)
    causal_blk = j_idx * SEL_BLOCK <= tq
    imp = jnp.where(causal_blk, imp + jnp.where(forced, FORCE_BONUS, 0.0), -1.0)
    sel = jnp.zeros((n_sel, QB), F32)
    for _ in range(top):
        mx = jnp.max(imp, axis=0, keepdims=True)
        first = jnp.min(jnp.where(imp == mx, j_idx, n_sel), axis=0, keepdims=True)
        pick = j_idx == first
        sel = jnp.where(pick, 1.0, sel)
        imp = jnp.where(pick, -3e38, imp)
    pen = jnp.where((sel > 0.5) & causal_blk, 0.0, NEG)
    if n_sel < 64:
        pen = jnp.concatenate([pen, jnp.zeros((64 - n_sel, QB), F32)], axis=0)
    qt_ref[A_PEN:A_PEN + 64, :] = jnp.concatenate([pen] * R, axis=1).astype(BF16)

    def score(j):
        return jnp.dot(ksa_ref[pl.ds(pl.multiple_of(j * KS, KS), KS), :], qt_ref[...],
                       preferred_element_type=F32)

    def absorb(s, j, st):
        m_run, acc = st
        m_new = jnp.maximum(m_run, jnp.max(s, axis=0, keepdims=True))
        pr = jnp.exp(s - m_new).astype(BF16)
        return m_new, jnp.exp(m_run - m_new) * acc + pv(vsa_ref, j * SUB, pr)

    def body(j, carry):
        s_cur, st = carry
        s_next = score(j + 1)
        return s_next, absorb(s_cur, j, st)

    n_full = qi // SUB
    init = (jnp.full((1, HQ), NEG, F32), jnp.zeros((VR, HQ), F32))
    s_last, st = lax.fori_loop(0, n_full, body, (score(0), init))
    s_ref[...] = s_last
    diag = pl.ds(pl.multiple_of(q0 - n_full * KS, QB), QB)
    s_ref[diag, :] = jnp.where(k_loc <= q_loc, s_ref[diag, :], NEG)
    _, acc_s = absorb(s_ref[...], n_full, st)
    o_sel = acc_s[0:dh] / acc_s[dh:dh + 1]

    gl_t = gl_ref[0].T
    gate = []
    for br in range(N_BRANCH):
        per_kv = [jnp.concatenate([gl_t[(gg * R + r) * N_BRANCH + br:(gg * R + r) * N_BRANCH + br + 1]
                                   for r in range(R)], axis=1) for gg in range(NSA_KV_HEADS)]
        gate.append(jax.nn.sigmoid(jnp.where(g == 0, per_kv[0], per_kv[1])))
    o_t = gate[0] * o_cmp + gate[1] * o_sel + gate[2] * o_win
    o_ref[0] = jnp.concatenate([o_t[:, r * QB:(r + 1) * QB] for r in range(R)], axis=0).T.astype(o_ref.dtype)


def _nsa_t_kernel(qn_ref, gl_ref, kca_ref, vcT_ref, ksn_ref, kwn_ref, vs_ref, vw_ref, kconst_ref, o_ref,
                  qt_ref, ksa_ref, kwa_ref, vsa_ref, vwa_ref, *, n_sel, top):
    QB, R, dh, CK, VR, KS = NSA_QB, NSA_GROUP, HEAD_DIM, NSA_CK, NSA_VR, NSA_KS
    HQ = R * QB
    g = pl.program_id(1)
    qi = pl.program_id(2)
    q0 = qi * QB
    nkb = vsa_ref.shape[0]

    @pl.when(qi == 0)
    def _():
        ksa_ref[...] = kconst_ref[...]
        kwa_ref[...] = kconst_ref[...]
        kwa_ref[:, A_PEN:A_PEN + 64] = jnp.zeros((kwa_ref.shape[0], 64), BF16)
        for gg in range(NSA_KV_HEADS):
            @pl.when(g == gg)
            def _():
                heads = slice(gg * dh, (gg + 1) * dh)
                ksa_ref[:, A_FEAT:A_FEAT + dh] = ksn_ref[0, :, heads]
                kwa_ref[:, A_FEAT:A_FEAT + dh] = kwn_ref[0, :, heads]
                for c in range(nkb):
                    keys = slice(c * QB, (c + 1) * QB)
                    vsa_ref[c, 0:dh, :] = vs_ref[0, keys, :].astype(F32).T[heads].astype(BF16)
                    vwa_ref[c, 0:dh, :] = vw_ref[0, keys, :].astype(F32).T[heads].astype(BF16)
        ones_rows = jnp.where(_iota((nkb, VR - dh, QB), 1) == 0, 1.0, 0.0).astype(BF16)
        vsa_ref[:, dh:VR, :] = ones_rows
        vwa_ref[:, dh:VR, :] = ones_rows
        qt_ref[A_BIAS + 16:CK, :] = jnp.zeros((CK - A_BIAS - 16, HQ), BF16)

    q_rows = (qn_ref[0].astype(F32) * (dh ** -0.5)).T
    qT = jnp.concatenate([q_rows[r * dh:(r + 1) * dh] for r in range(R)], axis=1).astype(BF16)
    qt_ref[A_FEAT:A_FEAT + dh, :] = qT
    qt_ref[A_PEN:A_PEN + dh, :] = qT
    lane = _iota((16, HQ), 1)
    rowi = _iota((16, HQ), 0)
    t_q = q0 + (lane & (QB - 1))
    slope = jnp.exp2(-(g * R + (lane >> 7) + 1).astype(F32))
    t_hi = ((t_q >> 6) << 6).astype(F32)
    t_lo = (t_q & 63).astype(F32)
    bias_rows = jnp.where(rowi < 2, slope,
                          jnp.where(rowi == 2, -slope * t_hi, jnp.where(rowi == 3, -slope * t_lo, 0.0)))
    qt_ref[A_BIAS:A_BIAS + 16, :] = bias_rows.astype(BF16)

    n_cmp = kca_ref.shape[2]
    sc = jnp.dot(kca_ref[0, 0], qt_ref[...], preferred_element_type=F32)
    cmp_end = _iota((n_cmp, HQ), 0) * CMP_STRIDE + (CMP_BLOCK - 1)
    valid = cmp_end <= q0 + (_iota((n_cmp, HQ), 1) & (QB - 1))
    sc = jnp.where(valid, sc, NEG)
    e = jnp.exp(sc - jnp.max(sc, axis=0, keepdims=True))
    p = jnp.where(valid, e / jnp.sum(e, axis=0, keepdims=True), 0.0)
    o_cmp = jnp.dot(vcT_ref[0, 0], p.astype(BF16), preferred_element_type=F32)

    p_grp = p[:, 0:QB] + p[:, QB:2 * QB] + p[:, 2 * QB:3 * QB] + p[:, 3 * QB:4 * QB]
    c0 = _iota((n_sel, n_cmp), 1) * CMP_STRIDE
    s0 = _iota((n_sel, n_cmp), 0) * SEL_BLOCK
    overlap_t = ((c0 < s0 + SEL_BLOCK) & (c0 + CMP_BLOCK > s0)).astype(BF16)
    imp = _split_dot_left(overlap_t, p_grp)
    j_idx = _iota((n_sel, QB), 0)
    tq = q0 + _iota((n_sel, QB), 1)
    cur = tq >> 6
    forced = (j_idx == 0) | (j_idx == cur) | (j_idx == cur - 1)
    imp = jnp.where(j_idx * SEL_BLOCK <= tq, imp + jnp.where(forced, FORCE_BONUS, 0.0), -1.0)
    sel = jnp.zeros((n_sel, QB), F32)
    for _ in range(top):
        mx = jnp.max(imp, axis=0, keepdims=True)
        first = jnp.min(jnp.where(imp == mx, j_idx, n_sel), axis=0, keepdims=True)
        pick = j_idx == first
        sel = jnp.where(pick, 1.0, sel)
        imp = jnp.where(pick, -3e38, imp)
    pen = jnp.where(sel > 0.5, 0.0, NEG)
    if n_sel < 64:
        pen = jnp.concatenate([pen, jnp.zeros((64 - n_sel, QB), F32)], axis=0)
    qt_ref[A_PEN:A_PEN + 64, :] = jnp.concatenate([pen] * R, axis=1).astype(BF16)

    def attend(kaug_ref, vaug_ref, k0, nk, st, mode):
        k0 = pl.multiple_of(k0, QB)
        s = jnp.dot(kaug_ref[pl.ds(k0, nk), :], qt_ref[...], preferred_element_type=F32)
        if mode != "full":
            dist = (q0 + (_iota((nk, HQ), 1) & (QB - 1))) - (k0 + _iota((nk, HQ), 0))
            ok = dist >= 0
            if mode == "window":
                ok = ok & (dist < WINDOW)
            s = jnp.where(ok, s, NEG)
        m_run, acc = st
        m_new = jnp.maximum(m_run, jnp.max(s, axis=0, keepdims=True))
        pr = jnp.exp(s - m_new).astype(BF16)
        acc = jnp.exp(m_run - m_new) * acc
        kb0 = k0 // QB
        for i in range(nk // QB):
            acc = acc + jnp.dot(vaug_ref[kb0 + i], pr[i * QB:(i + 1) * QB], preferred_element_type=F32)
        return m_new, acc

    def finish(st):
        return st[1][0:dh] / st[1][dh:dh + 1]

    init = (jnp.full((1, HQ), NEG, F32), jnp.zeros((VR, HQ), F32))
    n_full = qi // (KS // QB)
    st = lax.fori_loop(0, n_full, lambda j, s_: attend(ksa_ref, vsa_ref, j * KS, KS, s_, "full"), init)
    o_sel = finish(attend(ksa_ref, vsa_ref, n_full * KS, KS, st, "causal"))
    o_win = finish(attend(kwa_ref, vwa_ref, jnp.maximum(q0 - WINDOW, 0), WINDOW + QB, init, "window"))

    gl_t = gl_ref[0].T
    gate = []
    for br in range(N_BRANCH):
        per_kv = [jnp.concatenate([gl_t[(gg * R + r) * N_BRANCH + br:(gg * R + r) * N_BRANCH + br + 1]
                                   for r in range(R)], axis=1) for gg in range(NSA_KV_HEADS)]
        gate.append(jax.nn.sigmoid(jnp.where(g == 0, per_kv[0], per_kv[1])))
    o_t = gate[0] * o_cmp + gate[1] * o_sel + gate[2] * o_win
    o_ref[0] = jnp.concatenate([o_t[:, r * QB:(r + 1) * QB] for r in range(R)], axis=0).T.astype(o_ref.dtype)


def nsa_attention(y3, small3, q_norm, k_norm, kc, vc):
    b, s, _ = y3.shape
    G, R, dh, QB, CK, VR = NSA_KV_HEADS, NSA_GROUP, HEAD_DIM, NSA_QB, NSA_CK, NSA_VR
    assert G == 2, "the kernel picks a kv head's gate rows with a two-way select"
    HQ = R * QB
    nq = s // QB
    n_sel = s // SEL_BLOCK
    assert n_sel <= 64, "selection one-hot columns hold at most 64 blocks"
    top = min(SEL_TOPK, n_sel)
    n_cmp = kc.shape[2]

    def pos_cols(pos):
        return np.stack([pos // 64 * 64, pos % 64, np.ones_like(pos), np.ones_like(pos)], axis=1)

    vc_t = vc.transpose(0, 1, 3, 2).astype(BF16)

    pos = np.arange(s)
    kconst = np.zeros((s, CK), np.float32)
    kconst[pos, A_PEN + pos // SEL_BLOCK] = 1.0
    kconst[:, A_BIAS:A_BIAS + 4] = pos_cols(pos)
    kconst = jnp.asarray(kconst, BF16)

    kc_hi = kc.astype(BF16)
    kc_lo = (kc - kc_hi.astype(F32)).astype(BF16)
    cend = np.arange(n_cmp) * CMP_STRIDE + (CMP_BLOCK - 1)
    cbias = np.zeros((n_cmp, CK - 2 * dh), np.float32)
    cbias[:, 0:4] = pos_cols(cend)
    kc_aug = jnp.concatenate([kc_hi, kc_lo, jnp.broadcast_to(jnp.asarray(cbias, BF16), (b, G, n_cmp, CK - 2 * dh))],
                             axis=-1)

    qw = R * dh
    qcol = C_NSA_Q // qw
    kvb = C_NSA_KV // LANES
    gq = jnp.tile(q_norm, R).reshape(1, qw)
    gk = jnp.stack([jnp.tile(k_norm[1], G), jnp.tile(k_norm[2], G)]).reshape(2, 1, LANES)

    def kv_spec(blk):
        return pl.BlockSpec((1, s, LANES), lambda i, g, j: (i, 0, kvb + blk))

    return pl.pallas_call(
        functools.partial(_nsa_kernel, n_sel=n_sel, top=top),
        out_shape=jax.ShapeDtypeStruct((b, s, MIX_WIDTH), BF16),
        grid=(b, G, nq),
        in_specs=[pl.BlockSpec((1, QB, qw), lambda i, g, j: (i, j, qcol + g)),
                  pl.BlockSpec((1, QB, N_SMALL), lambda i, g, j: (i, j, 0)),
                  pl.BlockSpec((1, 1, n_cmp, CK), lambda i, g, j: (i, g, 0, 0)),
                  pl.BlockSpec((1, 1, dh, n_cmp), lambda i, g, j: (i, g, 0, 0)),
                  kv_spec(2), kv_spec(4), kv_spec(3), kv_spec(5),
                  pl.BlockSpec((s, CK), lambda i, g, j: (0, 0)),
                  pl.BlockSpec((1, qw), lambda i, g, j: (0, 0)),
                  pl.BlockSpec((2, 1, LANES), lambda i, g, j: (0, 0, 0))],
        out_specs=pl.BlockSpec((1, QB, qw), lambda i, g, j: (i, j, g)),
        scratch_shapes=[pltpu.VMEM((CK, HQ), BF16),
                        pltpu.VMEM((s, CK), BF16), pltpu.VMEM((s + WINDOW, CK), BF16),
                        pltpu.VMEM((nq, VR, QB), BF16), pltpu.VMEM((nq + WINDOW // QB, VR, QB), BF16),
                        pltpu.VMEM((NSA_KS, HQ), F32)],
        compiler_params=_params("parallel", "parallel", "arbitrary"),
        name="nsa_attention",
    )(y3, small3, kc_aug, vc_t, y3, y3, y3, y3, kconst, gq, gk)


def _merge_kernel(on_ref, os_ref, om_ref, g0_ref, g1_ref, g2_ref, wb_ref, wo_ref, x_ref, mod_ref, o_ref):
    merged = None
    for i, (o_r, g_r) in enumerate(((on_ref, g0_ref), (os_ref, g1_ref), (om_ref, g2_ref))):
        br = jnp.dot(o_r[0], wb_ref[i], preferred_element_type=F32)
        term = jax.nn.sigmoid(g_r[0].astype(F32)) * br
        merged = term if merged is None else merged + term
    out = jnp.dot(merged.astype(BF16), wo_ref[...], preferred_element_type=F32)
    o_ref[0] = x_ref[0] + mod_ref[0, 2:3, :] * out


def merge_project(o_nsa, o_sb, o_ml, y3, w_branch, w_out, x, mod, tm=512):
    b, s, d = x.shape
    W = MIX_WIDTH
    ospec = pl.BlockSpec((1, tm, W), lambda i, j: (i, j, 0))
    xspec = pl.BlockSpec((1, tm, d), lambda i, j: (i, j, 0))
    gspecs = [pl.BlockSpec((1, tm, d), functools.partial(lambda i, j, c: (i, j, c), c=C_MERGE // d + c))
              for c in range(N_BRANCH)]
    return pl.pallas_call(
        _merge_kernel,
        out_shape=jax.ShapeDtypeStruct((b, s, d), F32),
        grid=(b, s // tm),
        in_specs=[ospec, ospec, ospec] + gspecs + [
            pl.BlockSpec((N_BRANCH, W, d), lambda i, j: (0, 0, 0)),
            pl.BlockSpec((d, d), lambda i, j: (0, 0)),
            xspec,
            pl.BlockSpec((1, 6, d), lambda i, j: (i, 0, 0))],
        out_specs=xspec,
        compiler_params=_params("parallel", "parallel"),
        name="merge_project",
    )(o_nsa, o_sb, o_ml, y3, y3, y3, w_branch.astype(BF16), w_out.astype(BF16), x, mod)


def _ffn_kernel(x_ref, g_ref, mod_ref, wg_ref, wu_ref, wd_ref, o_ref, h_ref, acc_ref):
    f = pl.program_id(2)

    @pl.when(f == 0)
    def _():
        h_ref[...] = _norm_mod(x_ref[0], g_ref[...], mod_ref[0], 3, 4).astype(BF16)
        acc_ref[...] = jnp.zeros_like(acc_ref)

    h = h_ref[...]
    a = jnp.dot(h, wg_ref[...], preferred_element_type=F32)
    u = jnp.dot(h, wu_ref[...], preferred_element_type=F32)
    act = (a * jax.nn.sigmoid(a) * u).astype(BF16)
    acc_ref[...] += jnp.dot(act, wd_ref[...], preferred_element_type=F32)

    @pl.when(f == pl.num_programs(2) - 1)
    def _():
        o_ref[0] = x_ref[0] + mod_ref[0, 5:6, :] * acc_ref[...]


def dense_ffn(x, g, mod, wg, wu, wd, tm=512, n_ftiles=2):
    b, s, d = x.shape
    ff = wg.shape[1]
    tf = -(-ff // (n_ftiles * LANES)) * LANES
    pad = n_ftiles * tf - ff
    wg = jnp.pad(to_bf16(wg), ((0, 0), (0, pad)))
    wu = jnp.pad(to_bf16(wu), ((0, 0), (0, pad)))
    wd = jnp.pad(to_bf16(wd), ((0, pad), (0, 0)))
    xspec = pl.BlockSpec((1, tm, d), lambda i, j, f: (i, j, 0))
    return pl.pallas_call(
        _ffn_kernel,
        out_shape=jax.ShapeDtypeStruct((b, s, d), F32),
        grid=(b, s // tm, n_ftiles),
        in_specs=[xspec,
                  pl.BlockSpec((1, d), lambda i, j, f: (0, 0)),
                  pl.BlockSpec((1, 6, d), lambda i, j, f: (i, 0, 0)),
                  pl.BlockSpec((d, tf), lambda i, j, f: (0, f)),
                  pl.BlockSpec((d, tf), lambda i, j, f: (0, f)),
                  pl.BlockSpec((tf, d), lambda i, j, f: (f, 0))],
        out_specs=xspec,
        scratch_shapes=[pltpu.VMEM((tm, d), BF16), pltpu.VMEM((tm, d), F32)],
        compiler_params=_params("parallel", "parallel", "arbitrary"),
        name="dense_ffn",
    )(x, g.reshape(1, d), mod, wg, wu, wd)


def _router_kernel(x_ref, g_ref, mod_ref, wr_ref, h_ref, e_ref, p_ref):
    h = _norm_mod(x_ref[0], g_ref[...], mod_ref[0], 3, 4)
    h_ref[...] = h
    lane = _iota((1, LANES), 1)
    real = lane < N_EXPERTS
    logits = jnp.where(real, jnp.dot(h, wr_ref[...], precision=HIGHEST, preferred_element_type=F32), NEG)
    e = jnp.exp(logits - jnp.max(logits, axis=1, keepdims=True))
    p = jnp.where(real, e / jnp.sum(e, axis=1, keepdims=True), -1.0)
    p1 = jnp.max(p, axis=1, keepdims=True)
    i1 = jnp.min(jnp.where(p == p1, lane, LANES), axis=1, keepdims=True)
    rest = jnp.where(lane == i1, -1.0, p)
    p2 = jnp.max(rest, axis=1, keepdims=True)
    i2 = jnp.min(jnp.where(rest == p2, lane, LANES), axis=1, keepdims=True)
    tot = p1 + p2
    e_ref[...] = jnp.where(lane == 0, i1, jnp.where(lane == 1, i2, 0))[:, 0:N_EXPERTS]
    p_ref[...] = jnp.where(lane == 0, p1 / tot, jnp.where(lane == 1, p2 / tot, 0.0))[:, 0:N_EXPERTS]


def moe_router(x, g, mod, w_router, tm=512):
    b, s, d = x.shape
    t = b * s
    spb = s // tm
    wr = jnp.pad(w_router, ((0, 0), (0, LANES - N_EXPERTS)))
    return pl.pallas_call(
        _router_kernel,
        out_shape=(jax.ShapeDtypeStruct((t, d), F32),
                   jax.ShapeDtypeStruct((t, N_EXPERTS), I32),
                   jax.ShapeDtypeStruct((t, N_EXPERTS), F32)),
        grid=(b, spb),
        in_specs=[pl.BlockSpec((1, tm, d), lambda i, j: (i, j, 0)),
                  pl.BlockSpec((1, d), lambda i, j: (0, 0)),
                  pl.BlockSpec((1, 6, d), lambda i, j: (i, 0, 0)),
                  pl.BlockSpec((d, LANES), lambda i, j: (0, 0))],
        out_specs=(pl.BlockSpec((tm, d), lambda i, j: (i * spb + j, 0)),
                   pl.BlockSpec((tm, N_EXPERTS), lambda i, j: (i * spb + j, 0)),
                   pl.BlockSpec((tm, N_EXPERTS), lambda i, j: (i * spb + j, 0))),
        compiler_params=_params("parallel", "parallel"),
        name="moe_router",
    )(x, g.reshape(1, d), mod, wr)


def _dispatch_kernel(dest_ref, h_ref, zero_hbm, xpad_hbm, sem, *, td):
    del zero_hbm
    base = pl.program_id(0) * td * TOP_K

    def row_copy(a):
        return pltpu.make_async_copy(h_ref.at[a // TOP_K], xpad_hbm.at[dest_ref[base + a]], sem)

    def issue(a, c):
        row_copy(a).start()
        return c

    def drain(a, c):
        row_copy(a).wait()
        return c

    lax.fori_loop(0, td * TOP_K, issue, 0)
    lax.fori_loop(0, td * TOP_K, drain, 0)


def moe_dispatch(h, dest, n_rows, td=256):
    t, d = h.shape
    return pl.pallas_call(
        functools.partial(_dispatch_kernel, td=td),
        out_shape=jax.ShapeDtypeStruct((n_rows, d), h.dtype),
        grid_spec=pltpu.PrefetchScalarGridSpec(
            num_scalar_prefetch=1,
            grid=(t // td,),
            in_specs=[pl.BlockSpec((td, d), lambda i, dr: (i, 0)), pl.BlockSpec(memory_space=pl.ANY)],
            out_specs=pl.BlockSpec(memory_space=pl.ANY),
            scratch_shapes=[pltpu.SemaphoreType.DMA(())]),
        input_output_aliases={2: 0},
        compiler_params=pltpu.CompilerParams(dimension_semantics=("arbitrary",), has_side_effects=True),
        name="moe_dispatch",
    )(dest, h, jnp.zeros((n_rows, d), h.dtype))


def _expert_kernel(be_ref, nu_ref, x_ref, wg_ref, wu_ref, wd_ref, o_ref, xb_ref, acc_ref):
    i = pl.program_id(0)
    f = pl.program_id(1)
    used = i < nu_ref[0]

    @pl.when(f == 0)
    def _():
        xb_ref[...] = x_ref[...].astype(BF16)
        acc_ref[...] = jnp.zeros_like(acc_ref)

    @pl.when(used)
    def _():
        xb = xb_ref[...]
        a = jnp.dot(xb, wg_ref[0], preferred_element_type=F32)
        u = jnp.dot(xb, wu_ref[0], preferred_element_type=F32)
        act = (a * jax.nn.sigmoid(a) * u).astype(BF16)
        acc_ref[...] += jnp.dot(act, wd_ref[0], preferred_element_type=F32)

    @pl.when(f == pl.num_programs(1) - 1)
    def _():
        o_ref[...] = acc_ref[...]


def moe_experts(x_pad, blk_expert, n_used, wg, wu, wd, tb, tf=896):
    p, d = x_pad.shape
    ff = wg.shape[2]
    return pl.pallas_call(
        _expert_kernel,
        out_shape=jax.ShapeDtypeStruct((p, d), F32),
        grid_spec=pltpu.PrefetchScalarGridSpec(
            num_scalar_prefetch=2,
            grid=(p // tb, ff // tf),
            in_specs=[pl.BlockSpec((tb, d), lambda i, f, be, nu: (i, 0)),
                      pl.BlockSpec((1, d, tf), lambda i, f, be, nu: (be[i], 0, f)),
                      pl.BlockSpec((1, d, tf), lambda i, f, be, nu: (be[i], 0, f)),
                      pl.BlockSpec((1, tf, d), lambda i, f, be, nu: (be[i], f, 0))],
            out_specs=pl.BlockSpec((tb, d), lambda i, f, be, nu: (i, 0)),
            scratch_shapes=[pltpu.VMEM((tb, d), BF16), pltpu.VMEM((tb, d), F32)]),
        compiler_params=_params("parallel", "arbitrary"),
        name="moe_experts",
    )(blk_expert, n_used, x_pad, wg, wu, wd)


def _combine_kernel(dest_ref, y_hbm, x_ref, p_ref, mod_ref, o_ref, buf0, buf1, sem, *, td, spb):
    tok0 = (pl.program_id(0) * spb + pl.program_id(1)) * td

    def row_copies(r):
        a = (tok0 + r) * TOP_K
        return (pltpu.make_async_copy(y_hbm.at[dest_ref[a]], buf0.at[r], sem),
                pltpu.make_async_copy(y_hbm.at[dest_ref[a + 1]], buf1.at[r], sem))

    def issue(r, c):
        c0, c1 = row_copies(r)
        c0.start()
        c1.start()
        return c

    def drain(r, c):
        c0, c1 = row_copies(r)
        c0.wait()
        c1.wait()
        return c

    lax.fori_loop(0, td, issue, 0)
    lax.fori_loop(0, td, drain, 0)
    w = p_ref[...]
    f = w[:, 0:1] * buf0[...] + w[:, 1:2] * buf1[...]
    o_ref[0] = x_ref[0] + mod_ref[0, 5:6, :] * f


def moe_combine(y, dest, x, top_p, mod, td=256):
    b, s, d = x.shape
    spb = s // td
    return pl.pallas_call(
        functools.partial(_combine_kernel, td=td, spb=spb),
        out_shape=jax.ShapeDtypeStruct((b, s, d), F32),
        grid_spec=pltpu.PrefetchScalarGridSpec(
            num_scalar_prefetch=1,
            grid=(b, spb),
            in_specs=[pl.BlockSpec(memory_space=pl.ANY),
                      pl.BlockSpec((1, td, d), lambda i, j, dr: (i, j, 0)),
                      pl.BlockSpec((td, N_EXPERTS), lambda i, j, dr: (i * spb + j, 0)),
                      pl.BlockSpec((1, 6, d), lambda i, j, dr: (i, 0, 0))],
            out_specs=pl.BlockSpec((1, td, d), lambda i, j, dr: (i, j, 0)),
            scratch_shapes=[pltpu.VMEM((td, d), F32), pltpu.VMEM((td, d), F32),
                            pltpu.SemaphoreType.DMA(())]),
        compiler_params=_params("arbitrary", "arbitrary"),
        name="moe_combine",
    )(dest, y, x, top_p, mod)


def moe_ffn(x, g, mod, w_router, wg, wu, wd, tb=512):
    b, s, d = x.shape
    t = b * s
    a = t * TOP_K
    h, top_e, top_p = moe_router(x, g, mod, w_router)
    e_flat = top_e[:, 0:TOP_K].reshape(a)
    onehot = (e_flat[:, None] == jnp.arange(N_EXPERTS, dtype=I32)[None, :]).astype(I32)
    csum = jnp.cumsum(onehot, axis=0)
    rank = jnp.sum(onehot * csum, axis=1) - 1
    counts = csum[-1]
    padded = (counts + tb - 1) // tb * tb
    pad_ends = jnp.cumsum(padded)
    pad_starts = pad_ends - padded
    dest = (jnp.sum(onehot * pad_starts[None, :], axis=1) + rank).astype(I32)
    n_rows = (a // tb + N_EXPERTS + 1) * tb
    n_blk = n_rows // tb
    blk_expert = jnp.minimum(
        jnp.searchsorted(pad_ends, jnp.arange(n_blk, dtype=I32) * tb, side="right"), N_EXPERTS - 1).astype(I32)
    n_used = (pad_ends[-1:] // tb).astype(I32)
    slot_assign = moe_invert(dest, n_rows, tb)
    y2 = moe_experts_fused(h, slot_assign, blk_expert, n_used, to_bf16(wg), to_bf16(wu), to_bf16(wd), tb)
    return moe_mix(y2, x, top_p, mod)


def _invert_kernel(dest_ref, sa_ref, *, n_assign, tb, n_chunks):
    phase = pl.program_id(0)
    chunk = pl.program_id(1)

    @pl.when(phase == 0)
    def _():
        per = sa_ref.shape[0] // n_chunks

        def fill(j, c):
            p = chunk * per + j
            sa_ref[p] = n_assign + (p & (2 * tb - 1))
            return c

        lax.fori_loop(0, per, fill, 0, unroll=8)

    @pl.when(phase == 1)
    def _():
        per = n_assign // n_chunks

        def put(j, c):
            a = chunk * per + j
            sa_ref[dest_ref[a]] = a
            return c

        lax.fori_loop(0, per, put, 0, unroll=8)


def moe_invert(dest, n_rows, tb, n_chunks=16):
    n_assign = dest.shape[0]
    assert tb & (tb - 1) == 0 and n_rows % n_chunks == 0 and n_assign % n_chunks == 0
    return pl.pallas_call(
        functools.partial(_invert_kernel, n_assign=n_assign, tb=tb, n_chunks=n_chunks),
        out_shape=jax.ShapeDtypeStruct((n_rows,), I32),
        grid=(2, n_chunks),
        in_specs=[pl.BlockSpec(memory_space=pltpu.SMEM)],
        out_specs=pl.BlockSpec(memory_space=pltpu.SMEM),
        compiler_params=pltpu.CompilerParams(dimension_semantics=("arbitrary", "arbitrary")),
        name="moe_invert",
    )(dest)


def _expert_fused_kernel(be_ref, nu_ref, sa_ref, h_hbm, wg_ref, wu_ref, wd_ref, y_hbm,
                         xin_ref, yout_ref, xb_ref, acc_ref, sem_in, sem_out, *, tb, n_tok, n_f):
    i = pl.program_id(0)
    f = pl.program_id(1)
    n_used = nu_ref[0]
    n_assign = n_tok * TOP_K
    rows_f = tb // n_f
    active = i <= n_used

    def gather_row(blk, r):
        a = sa_ref[blk * tb + r]
        tok = jnp.where(a < n_assign, a >> 1, 0)
        return pltpu.make_async_copy(h_hbm.at[tok], xin_ref.at[blk % 2, r], sem_in.at[blk % 2])

    def scatter_row(blk, r):
        a = jnp.where(blk >= 0, sa_ref[jnp.maximum(blk, 0) * tb + r], n_assign + tb + r)
        row = jnp.where(a < n_assign, (a & 1) * n_tok + (a >> 1), a)
        return pltpu.make_async_copy(yout_ref.at[(blk + 2) % 2, r], y_hbm.at[row], sem_out.at[(blk + 2) % 2])

    def for_rows(fn):
        def body(r, c):
            fn(r)
            return c
        lax.fori_loop(0, tb, body, 0, unroll=8)

    @pl.when(f == 0)
    def _():
        @pl.when(i == 0)
        def _():
            yout_ref[1] = jnp.zeros((tb, yout_ref.shape[2]), F32)
            for_rows(lambda r: gather_row(0, r).start())

            def clear_row(r):
                return pltpu.make_async_copy(yout_ref.at[1, r], y_hbm.at[n_assign + r], sem_out.at[0])

            for_rows(lambda r: clear_row(r).start())
            for_rows(lambda r: clear_row(r).wait())

        @pl.when((i == 0) | (i - 1 <= n_used))
        def _():
            for_rows(lambda r: gather_row(i, r).wait())

        @pl.when(active)
        def _():
            xb_ref[...] = xin_ref[i % 2].astype(BF16)
            acc_ref[...] = jnp.zeros_like(acc_ref)

    @pl.when(active)
    def _():
        for r in range(rows_f):
            gather_row(i + 1, f * rows_f + r).start()
            scatter_row(i - 1, f * rows_f + r).start()
        xb = xb_ref[...]
        a = jnp.dot(xb, wg_ref[0], preferred_element_type=F32)
        u = jnp.dot(xb, wu_ref[0], preferred_element_type=F32)
        act = (a * jax.nn.sigmoid(a) * u).astype(BF16)
        acc_ref[...] += jnp.dot(act, wd_ref[0], preferred_element_type=F32)

    @pl.when(f == n_f - 1)
    def _():
        @pl.when((i >= 1) & (i - 1 <= n_used))
        def _():
            for_rows(lambda r: scatter_row(i - 2, r).wait())

        @pl.when(active)
        def _():
            yout_ref[i % 2] = acc_ref[...]


def moe_experts_fused(h, slot_assign, blk_expert, n_used, wg, wu, wd, tb, tf=896):
    n_tok, d = h.shape
    p = slot_assign.shape[0]
    ff = wg.shape[2]
    return pl.pallas_call(
        functools.partial(_expert_fused_kernel, tb=tb, n_tok=n_tok, n_f=ff // tf),
        out_shape=jax.ShapeDtypeStruct((n_tok * TOP_K + 2 * tb, d), F32),
        grid_spec=pltpu.PrefetchScalarGridSpec(
            num_scalar_prefetch=3,
            grid=(p // tb, ff // tf),
            in_specs=[pl.BlockSpec(memory_space=pl.ANY),
                      pl.BlockSpec((1, d, tf), lambda i, f, be, nu, sa: (be[i], 0, f)),
                      pl.BlockSpec((1, d, tf), lambda i, f, be, nu, sa: (be[i], 0, f)),
                      pl.BlockSpec((1, tf, d), lambda i, f, be, nu, sa: (be[i], f, 0))],
            out_specs=pl.BlockSpec(memory_space=pl.ANY),
            scratch_shapes=[pltpu.VMEM((2, tb, d), F32), pltpu.VMEM((2, tb, d), F32),
                            pltpu.VMEM((tb, d), BF16), pltpu.VMEM((tb, d), F32),
                            pltpu.SemaphoreType.DMA((2,)), pltpu.SemaphoreType.DMA((2,))]),
        compiler_params=pltpu.CompilerParams(dimension_semantics=("arbitrary", "arbitrary"),
                                             vmem_limit_bytes=VMEM_LIMIT, has_side_effects=True),
        name="moe_experts",
    )(blk_expert, n_used, slot_assign, h, wg, wu, wd)


def _mix_kernel(y0_ref, y1_ref, x_ref, p_ref, mod_ref, o_ref):
    w = p_ref[...]
    f = w[:, 0:1] * y0_ref[...] + w[:, 1:2] * y1_ref[...]
    o_ref[0] = x_ref[0] + mod_ref[0, 5:6, :] * f


def moe_mix(y2, x, top_p, mod, td=512):
    b, s, d = x.shape
    spb = s // td
    nt = b * spb
    return pl.pallas_call(
        _mix_kernel,
        out_shape=jax.ShapeDtypeStruct((b, s, d), F32),
        grid=(b, spb),
        in_specs=[pl.BlockSpec((td, d), lambda i, j: (i * spb + j, 0)),
                  pl.BlockSpec((td, d), lambda i, j: (nt + i * spb + j, 0)),
                  pl.BlockSpec((1, td, d), lambda i, j: (i, j, 0)),
                  pl.BlockSpec((td, N_EXPERTS), lambda i, j: (i * spb + j, 0)),
                  pl.BlockSpec((1, 6, d), lambda i, j: (i, 0, 0))],
        out_specs=pl.BlockSpec((1, td, d), lambda i, j: (i, j, 0)),
        compiler_params=_params("parallel", "parallel"),
        name="moe_mix",
    )(y2, y2, x, top_p, mod)


def _pack_w_in(w_in):
    kv = 2 * NSA_KV_HEADS * HEAD_DIM * 3
    w_in = to_bf16(w_in)
    o = 0
    nsa_q = w_in[:, o:o + MIX_WIDTH]; o += MIX_WIDTH
    nsa_kv = w_in[:, o:o + kv]; o += kv
    nsa_gate = w_in[:, o:o + NSA_HEADS * N_BRANCH]; o += NSA_HEADS * N_BRANCH
    sb = w_in[:, o:o + 3 * MIX_WIDTH]; o += 3 * MIX_WIDTH
    ml_qkv = w_in[:, o:o + 3 * MIX_WIDTH]; o += 3 * MIX_WIDTH
    ml_if = w_in[:, o:o + 2 * ML_HEADS]; o += 2 * ML_HEADS
    ml_o = w_in[:, o:o + MIX_WIDTH]; o += MIX_WIDTH
    merge = w_in[:, o:]
    main = jnp.concatenate([merge, nsa_q, ml_qkv, ml_o, sb, nsa_kv], axis=1)
    small = jnp.concatenate([nsa_gate, ml_if], axis=1)
    small = jnp.pad(small, ((0, 0), (0, N_SMALL - small.shape[1])))
    return main, small


def token_mixer_layer(x, mod, norm_g, w_in, nsa_q_norm, nsa_k_norm, cmp_pos, cmp_w1, cmp_b1, cmp_w2,
                      cmp_b2, ml_conv_w, ml_conv_b, ml_gate_b, w_branch, w_out):
    w_main, w_small = _pack_w_in(w_in)
    y3, small3 = in_projection(x, norm_g, mod, w_main, w_small)
    o_sb = sb_attention(y3)
    o_ml = mlstm(y3, small3, ml_conv_w, ml_conv_b, ml_gate_b)
    kc, vc = nsa_compress(y3, cmp_pos, cmp_w1, cmp_b1, cmp_w2, cmp_b2, nsa_k_norm[0])
    o_nsa = nsa_attention(y3, small3, nsa_q_norm, nsa_k_norm, kc, vc)
    return merge_project(o_nsa, o_sb, o_ml, y3, w_branch, w_out, x, mod)


def kernel(x, c, ada_w, ada_b, norm_mix, norm_ffn, w_in, nsa_q_norm, nsa_k_norm, cmp_pos, cmp_w1, cmp_b1,
           cmp_w2, cmp_b2, ml_conv_w, ml_conv_b, ml_gate_b, w_branch, w_out, ffn_wg, ffn_wu, ffn_wd,
           moe_router, moe_wg, moe_wu, moe_wd):
    depth = ada_w.shape[0]
    b, s, d = x.shape
    mods = adaln(c, ada_w, ada_b).reshape(depth, b, 6, d)
    for layer in range(depth):
        mod = mods[layer]
        x = token_mixer_layer(x, mod, norm_mix[layer], w_in[layer], nsa_q_norm[layer], nsa_k_norm[layer],
                              cmp_pos[layer], cmp_w1[layer], cmp_b1[layer], cmp_w2[layer], cmp_b2[layer],
                              ml_conv_w[layer], ml_conv_b[layer], ml_gate_b[layer], w_branch[layer],
                              w_out[layer])
        j = layer // 2
        if layer % 2 == 0:
            x = dense_ffn(x, norm_ffn[layer], mod, ffn_wg[j], ffn_wu[j], ffn_wd[j])
        else:
            x = moe_ffn(x, norm_ffn[layer], mod, moe_router[j], moe_wg[j], moe_wu[j], moe_wd[j])
    return x
```

```python
import functools

import numpy as np
import jax
import jax.numpy as jnp
from jax import lax
from jax.experimental import pallas as pl
from jax.experimental.pallas import tpu as pltpu

F32 = jnp.float32
BF16 = jnp.bfloat16
I32 = jnp.int32
HIGHEST = lax.Precision.HIGHEST

EPS = 1e-6
NEG = -1e30
HEAD_DIM = 64
MIX_WIDTH = 512
NSA_HEADS = 8
NSA_KV_HEADS = 2
NSA_GROUP = NSA_HEADS // NSA_KV_HEADS
CMP_BLOCK = 32
CMP_STRIDE = 16
SEL_BLOCK = 64
SEL_TOPK = 16
WINDOW = 512
FORCE_BONUS = 1e4
ML_HEADS = 4
ML_HEAD_DIM = 128
ML_CHUNK = 64
CONV_WIDTH = 4
N_BRANCH = 3
N_EXPERTS = 8
TOP_K = 2
LANES = 128

C_MERGE = 0
C_NSA_Q = 3072
C_ML_Q = 3584
C_ML_K = 4096
C_ML_V = 4608
C_ML_O = 5120
C_SB_Q = 5632
C_SB_K = 6144
C_SB_V = 6656
C_NSA_KV = 7168
N_MAIN = 7936
S_NSA_GATE = 0
S_ML_I = 24
S_ML_F = 28
N_SMALL = 128

VMEM_LIMIT = 56 * 1024 * 1024


def _params(*sem):
    return pltpu.CompilerParams(dimension_semantics=sem, vmem_limit_bytes=VMEM_LIMIT)


def _iota(shape, dim):
    return lax.broadcasted_iota(I32, shape, dim)


def _split_dot(a32, b_bf16):
    hi = a32.astype(BF16)
    lo = (a32 - hi.astype(F32)).astype(BF16)
    return (jnp.dot(hi, b_bf16, preferred_element_type=F32)
            + jnp.dot(lo, b_bf16, preferred_element_type=F32))


def _split_dot_left(a_bf16, b32):
    hi = b32.astype(BF16)
    lo = (b32 - hi.astype(F32)).astype(BF16)
    return (jnp.dot(a_bf16, hi, preferred_element_type=F32)
            + jnp.dot(a_bf16, lo, preferred_element_type=F32))


def _dot_nt(a, b):
    return lax.dot_general(a, b, (((1,), (1,)), ((), ())), preferred_element_type=F32)


def _log_sigmoid(z):
    return jnp.minimum(z, 0.0) - jnp.log1p(jnp.exp(-jnp.abs(z)))


def _cast_kernel(x_ref, o_ref):
    o_ref[...] = x_ref[...].astype(o_ref.dtype)


def to_bf16(w, max_rows=512):
    cols = w.shape[-1]
    w2 = w.reshape(-1, cols)
    rows = w2.shape[0]
    tr = max(t for t in range(8, max_rows + 1, 8) if rows % t == 0)
    out = pl.pallas_call(
        _cast_kernel,
        out_shape=jax.ShapeDtypeStruct((rows, cols), BF16),
        grid=(rows // tr,),
        in_specs=[pl.BlockSpec((tr, cols), lambda i: (i, 0))],
        out_specs=pl.BlockSpec((tr, cols), lambda i: (i, 0)),
        compiler_params=_params("parallel"),
        name="to_bf16",
    )(w2)
    return out.reshape(w.shape)


def _adaln_kernel(c_ref, w_ref, b_ref, o_ref):
    c = c_ref[...]
    cond = c * jax.nn.sigmoid(c)
    o_ref[0] = jnp.dot(cond, w_ref[0], precision=HIGHEST, preferred_element_type=F32) + b_ref[0]


def adaln(c, ada_w, ada_b):
    depth, d, n = ada_w.shape
    b = c.shape[0]
    tn = 1536
    return pl.pallas_call(
        _adaln_kernel,
        out_shape=jax.ShapeDtypeStruct((depth, b, n), F32),
        grid=(depth, n // tn),
        in_specs=[pl.BlockSpec((b, d), lambda l, j: (0, 0)),
                  pl.BlockSpec((1, d, tn), lambda l, j: (l, 0, j)),
                  pl.BlockSpec((1, 1, tn), lambda l, j: (l, 0, j))],
        out_specs=pl.BlockSpec((1, b, tn), lambda l, j: (l, 0, j)),
        compiler_params=_params("parallel", "parallel"),
        name="adaln",
    )(c, ada_w, ada_b.reshape(depth, 1, n))


def _norm_mod(x, g, mod, shift_row, scale_row):
    ms = jnp.mean(x * x, axis=-1, keepdims=True)
    y = x * lax.rsqrt(ms + EPS) * g
    return y * (1.0 + mod[scale_row:scale_row + 1, :]) + mod[shift_row:shift_row + 1, :]


def _normmod_kernel(x_ref, g_ref, mod_ref, o_ref, *, shift_row, scale_row):
    o_ref[0] = _norm_mod(x_ref[0], g_ref[...], mod_ref[0], shift_row, scale_row).astype(o_ref.dtype)


def normmod(x, g, mod, shift_row, scale_row, ts=512):
    b, s, d = x.shape
    return pl.pallas_call(
        functools.partial(_normmod_kernel, shift_row=shift_row, scale_row=scale_row),
        out_shape=jax.ShapeDtypeStruct((b, s, d), BF16),
        grid=(b, s // ts),
        in_specs=[pl.BlockSpec((1, ts, d), lambda i, j: (i, j, 0)),
                  pl.BlockSpec((1, d), lambda i, j: (0, 0)),
                  pl.BlockSpec((1, 6, d), lambda i, j: (i, 0, 0))],
        out_specs=pl.BlockSpec((1, ts, d), lambda i, j: (i, j, 0)),
        compiler_params=_params("parallel", "parallel"),
        name="normmod",
    )(x, g.reshape(1, d), mod)


def _in_proj_kernel(x_ref, g_ref, mod_ref, wm_ref, ws_ref, y_ref, sm_ref, h_ref):
    @pl.when(pl.program_id(2) == 0)
    def _():
        h_ref[...] = _norm_mod(x_ref[0], g_ref[...], mod_ref[0], 0, 1).astype(BF16)
        sm_ref[0] = jnp.dot(h_ref[...], ws_ref[...], preferred_element_type=F32)

    y_ref[0] = jnp.dot(h_ref[...], wm_ref[...], preferred_element_type=F32).astype(y_ref.dtype)


def in_projection(x, g, mod, w_main, w_small, tm=512, n_tiles=2):
    b, s, d = x.shape
    tn = N_MAIN // n_tiles
    return pl.pallas_call(
        _in_proj_kernel,
        out_shape=(jax.ShapeDtypeStruct((b, s, N_MAIN), BF16), jax.ShapeDtypeStruct((b, s, N_SMALL), F32)),
        grid=(b, s // tm, n_tiles),
        in_specs=[pl.BlockSpec((1, tm, d), lambda i, j, n: (i, j, 0)),
                  pl.BlockSpec((1, d), lambda i, j, n: (0, 0)),
                  pl.BlockSpec((1, 6, d), lambda i, j, n: (i, 0, 0)),
                  pl.BlockSpec((d, tn), lambda i, j, n: (0, n)),
                  pl.BlockSpec((d, N_SMALL), lambda i, j, n: (0, 0))],
        out_specs=(pl.BlockSpec((1, tm, tn), lambda i, j, n: (i, j, n)),
                   pl.BlockSpec((1, tm, N_SMALL), lambda i, j, n: (i, j, 0))),
        scratch_shapes=[pltpu.VMEM((tm, d), BF16)],
        compiler_params=_params("parallel", "parallel", "arbitrary"),
        name="in_projection",
    )(x, g.reshape(1, d), mod, w_main, w_small)


def _mm_kernel(a_ref, w_ref, o_ref):
    o_ref[...] = jnp.dot(a_ref[...], w_ref[...], preferred_element_type=F32).astype(o_ref.dtype)


def matmul(a, w, out_dtype, tm, tn):
    m, k = a.shape
    n = w.shape[1]
    return pl.pallas_call(
        _mm_kernel,
        out_shape=jax.ShapeDtypeStruct((m, n), out_dtype),
        grid=(m // tm, n // tn),
        in_specs=[pl.BlockSpec((tm, k), lambda i, j: (i, 0)),
                  pl.BlockSpec((k, tn), lambda i, j: (0, j))],
        out_specs=pl.BlockSpec((tm, tn), lambda i, j: (i, j)),
        compiler_params=_params("parallel", "parallel"),
        name="matmul",
    )(a, w)


SB_EXP_FLOOR = -104.0
SB_EAGER = 2


def _sb_kernel(q_ref, k_ref, v_ref, o_ref, vt_ref, *, tq):
    TK, dh = LANES, HEAD_DIM
    n_sub = tq // TK
    W = 2 * tq
    qi = pl.program_id(2)
    q0 = qi * tq

    @pl.when(qi == 0)
    def _():
        for c in range(v_ref.shape[1] // tq):
            vt_ref[c] = v_ref[0, c * tq:(c + 1) * tq, :].astype(F32).T.astype(BF16)

    q_t = (q_ref[0].astype(F32) * (dh ** -0.5)).T
    chan = _iota((2 * dh, tq), 0)
    q_cat = jnp.concatenate([jnp.where(chan < dh, q_t, 0.0), jnp.where(chan < dh, 0.0, q_t)],
                            axis=1).astype(BF16)
    later = (_iota((TK, TK), 0) < _iota((TK, TK), 1)).astype(BF16)
    suffix = jnp.concatenate([jnp.concatenate([later, later], axis=1), jnp.ones((8, 2 * TK), BF16)], axis=0)

    def steps(blocks, st):
        carry, acc = st
        lks, lss, his, los = [], [], [], []
        for j, keep in blocks:
            k0 = pl.multiple_of(j * tq, tq)
            z = jnp.dot(k_ref[0, pl.ds(k0, tq), :], q_cat, preferred_element_type=F32)
            lk = -(jnp.maximum(z, 0.0) + jnp.log(1.0 + jnp.exp(-jnp.abs(z))))
            lss.append(lk + z)
            if keep is not None:
                lk = jnp.where(keep, lk, 0.0)
            hi = lk.astype(BF16)
            his.append(hi)
            los.append((lk - hi.astype(F32)).astype(BF16))
        afters = []
        for b in range(len(blocks)):
            after = [None] * n_sub
            for sub in range(n_sub - 1, -1, -1):
                rows = slice(sub * TK, (sub + 1) * TK)
                res = jnp.dot(suffix, jnp.concatenate([his[b][rows], los[b][rows]], axis=0),
                              preferred_element_type=F32)
                after[sub] = res[0:TK] + carry
                carry = carry + res[TK:TK + 1]
            afters.append(jnp.concatenate(after, axis=0))
        for b, (j, keep) in enumerate(blocks):
            a = jnp.exp(lss[b] + afters[b])
            if keep is not None:
                a = jnp.where(keep, a, 0.0)
            acc = acc + jnp.dot(vt_ref[j], a.astype(BF16), preferred_element_type=F32)
        return carry, acc

    def cond(st):
        return (st[0] >= 0) & (jnp.max(st[1]) > SB_EXP_FLOOR)

    def body(st):
        return (st[0] - 1,) + steps([(st[0], None)], st[1:])

    strict = _iota((tq, W), 0) < (_iota((tq, W), 1) & (tq - 1))
    eager = [(jnp.maximum(qi - d, 0), qi >= d) for d in range(1, SB_EAGER + 1)]
    st = steps([(qi, strict)] + eager, (jnp.zeros((1, W), F32), jnp.zeros((2 * dh, W), F32)))
    _, _, acc = lax.while_loop(cond, body, (qi - 1 - SB_EAGER,) + st)
    o_ref[0] = jnp.concatenate([acc[0:dh, 0:tq], acc[dh:2 * dh, tq:W]], axis=0).T.astype(o_ref.dtype)


def sb_attention(y3, tq=256):
    b, s, _ = y3.shape
    n_pairs = MIX_WIDTH // LANES
    qb, kb, vb = C_SB_Q // LANES, C_SB_K // LANES, C_SB_V // LANES
    return pl.pallas_call(
        functools.partial(_sb_kernel, tq=tq),
        out_shape=jax.ShapeDtypeStruct((b, s, MIX_WIDTH), BF16),
        grid=(b, n_pairs, s // tq),
        in_specs=[pl.BlockSpec((1, tq, LANES), lambda i, p, j: (i, j, qb + p)),
                  pl.BlockSpec((1, s, LANES), lambda i, p, j: (i, 0, kb + p)),
                  pl.BlockSpec((1, s, LANES), lambda i, p, j: (i, 0, vb + p))],
        out_specs=pl.BlockSpec((1, tq, LANES), lambda i, p, j: (i, j, p)),
        scratch_shapes=[pltpu.VMEM((s // tq, LANES, tq), BF16)],
        compiler_params=_params("parallel", "parallel", "arbitrary"),
        name="sb_attention",
    )(y3, y3, y3)


def _mlstm_kernel(q_ref, k_ref, v_ref, og_ref, sm_ref, gr_ref, cw_ref, cb_ref, gb_ref, out_ref,
                  ct_ref, n_ref, m_ref, xbuf_ref, qk_ref, *, ts):
    L, dh, H, W = ML_CHUNK, ML_HEAD_DIM, ML_HEADS, MIX_WIDTH
    halo = 8
    sblk = pl.program_id(1)

    @pl.when(sblk == 0)
    def _():
        ct_ref[...] = jnp.zeros_like(ct_ref)
        n_ref[...] = jnp.zeros_like(n_ref)
        m_ref[...] = jnp.zeros_like(m_ref)
        xbuf_ref[0:halo, :] = jnp.zeros((halo, 2 * W), F32)

    @pl.when(sblk > 0)
    def _():
        xbuf_ref[0:halo, :] = xbuf_ref[ts:ts + halo, :]

    xbuf_ref[halo:halo + ts, 0:W] = q_ref[0].astype(F32)
    xbuf_ref[halo:halo + ts, W:2 * W] = k_ref[0].astype(F32)
    conv = cb_ref[...] + jnp.zeros((ts, 2 * W), F32)
    for j in range(CONV_WIDTH):
        off = halo - (CONV_WIDTH - 1) + j
        conv = conv + cw_ref[j:j + 1, :] * xbuf_ref[off:off + ts, :]
    act = conv * jax.nn.sigmoid(conv)
    qk_ref[:, 0:W] = (act[:, 0:W] * (dh ** -0.5)).astype(BF16)
    qk_ref[:, W:2 * W] = act[:, W:2 * W].astype(BF16)

    it0, it1 = _iota((L, L), 0), _iota((L, L), 1)
    causal = it0 >= it1
    tri_lo = causal.astype(BF16)
    tri_up = (it0 <= it1).astype(BF16)

    def chunk(c, carry):
        r0 = pl.multiple_of(c * L, L)
        sm = sm_ref[0, pl.ds(r0, L), :]
        gr = gr_ref[0, c]
        HR = range(H)
        rows = pl.ds(r0, L)
        cols = [slice(h * dh, (h + 1) * dh) for h in HR]
        ig_col = [sm[:, S_ML_I + h:S_ML_I + h + 1] + gb_ref[0, h] for h in HR]
        lf_col = [_log_sigmoid(sm[:, S_ML_F + h:S_ML_F + h + 1] + gb_ref[1, h]) for h in HR]
        ig_row = [gr[h:h + 1, :] + gb_ref[0, h] for h in HR]
        lf_row = [_log_sigmoid(gr[H + h:H + h + 1, :] + gb_ref[1, h]) for h in HR]
        b_t = [_split_dot_left(tri_lo, jnp.broadcast_to(lf_col[h], (L, L))) for h in HR]
        b_s = [_split_dot(jnp.broadcast_to(lf_row[h], (L, L)), tri_up) for h in HR]
        qq = [qk_ref[rows, cols[h]] for h in HR]
        kk = [qk_ref[rows, W + h * dh:W + (h + 1) * dh] for h in HR]
        vv = [v_ref[0, rows, cols[h]] for h in HR]
        ct = [ct_ref[h] for h in HR]
        nvec = [n_ref[h] for h in HR]
        m_prev = [m_ref[h][:, 0:1] for h in HR]
        qk = [_dot_nt(qq[h], kk[h]) for h in HR]
        q_c = [jnp.dot(qq[h], ct[h].astype(BF16), preferred_element_type=F32) for h in HR]
        kt = [kk[h].astype(F32).T.astype(BF16) for h in HR]
        dmat = [jnp.where(causal, b_t[h] - b_s[h] + ig_row[h], NEG) for h in HR]
        b_col = [b_t[h][:, 0:1] for h in HR]
        m_inter = [b_col[h] + m_prev[h] for h in HR]
        m_t = [jnp.maximum(m_inter[h], jnp.max(dmat[h], axis=1, keepdims=True)) for h in HR]
        w = [jnp.exp(dmat[h] - m_t[h]) * qk[h] for h in HR]
        inter = [jnp.exp(m_inter[h] - m_t[h]) for h in HR]
        w_v = [jnp.dot(w[h].astype(BF16), vv[h], preferred_element_type=F32) for h in HR]
        b_last = [b_t[h][L - 1:L, 0:1] for h in HR]
        decay = [b_last[h] - b_col[h] + ig_col[h] for h in HR]
        m_new = [jnp.maximum(b_last[h] + m_prev[h], jnp.max(decay[h], axis=0, keepdims=True)) for h in HR]
        ws = [jnp.exp(decay[h] - m_new[h]) for h in HR]
        cscale = [jnp.exp(b_last[h] + m_prev[h] - m_new[h]) for h in HR]
        wv = [(ws[h] * vv[h].astype(F32)).astype(BF16) for h in HR]
        k_wv = [jnp.dot(kt[h], wv[h], preferred_element_type=F32) for h in HR]
        for h in HR:
            num = inter[h] * q_c[h] + w_v[h]
            den = (inter[h] * jnp.sum(qq[h].astype(F32) * nvec[h], axis=1, keepdims=True)
                   + jnp.sum(w[h], axis=1, keepdims=True))
            hval = num / jnp.maximum(jnp.abs(den), jnp.exp(-m_t[h]))
            ct_ref[h] = cscale[h] * ct[h] + k_wv[h]
            n_ref[h] = cscale[h] * nvec[h] + jnp.sum(ws[h] * kk[h].astype(F32), axis=0, keepdims=True)
            m_ref[h] = jnp.broadcast_to(m_new[h], (1, LANES))
            gate = jax.nn.sigmoid(og_ref[0, rows, cols[h]].astype(F32))
            out_ref[0, rows, cols[h]] = (gate * hval).astype(out_ref.dtype)
        return carry

    lax.fori_loop(0, ts // L, chunk, 0)


def mlstm(y3, small3, conv_w, conv_b, gate_b, ts=512):
    b, s, _ = y3.shape
    W, H, L = MIX_WIDTH, ML_HEADS, ML_CHUNK
    gr = small3[:, :, S_ML_I:S_ML_I + 2 * H].reshape(b, s // L, L, 2 * H).transpose(0, 1, 3, 2)
    cq, ck, cv, co = C_ML_Q // W, C_ML_K // W, C_ML_V // W, C_ML_O // W
    return pl.pallas_call(
        functools.partial(_mlstm_kernel, ts=ts),
        out_shape=jax.ShapeDtypeStruct((b, s, W), BF16),
        grid=(b, s // ts),
        in_specs=[pl.BlockSpec((1, ts, W), lambda i, j: (i, j, cq)),
                  pl.BlockSpec((1, ts, W), lambda i, j: (i, j, ck)),
                  pl.BlockSpec((1, ts, W), lambda i, j: (i, j, cv)),
                  pl.BlockSpec((1, ts, W), lambda i, j: (i, j, co)),
                  pl.BlockSpec((1, ts, N_SMALL), lambda i, j: (i, j, 0)),
                  pl.BlockSpec((1, ts // L, 2 * H, L), lambda i, j: (i, j, 0, 0)),
                  pl.BlockSpec((CONV_WIDTH, 2 * W), lambda i, j: (0, 0)),
                  pl.BlockSpec((1, 2 * W), lambda i, j: (0, 0)),
                  pl.BlockSpec(memory_space=pltpu.SMEM)],
        out_specs=pl.BlockSpec((1, ts, W), lambda i, j: (i, j, 0)),
        scratch_shapes=[pltpu.VMEM((H, ML_HEAD_DIM, ML_HEAD_DIM), F32),
                        pltpu.VMEM((H, 1, ML_HEAD_DIM), F32),
                        pltpu.VMEM((H, 1, LANES), F32),
                        pltpu.VMEM((ts + 8, 2 * W), F32),
                        pltpu.VMEM((ts, 2 * W), BF16)],
        compiler_params=_params("parallel", "arbitrary"),
        name="mlstm",
    )(y3, y3, y3, y3, small3, gr, conv_w, conv_b.reshape(1, 2 * W), gate_b)


def _headnorm_kernel(x_ref, g_ref, o_ref):
    x = x_ref[0].astype(F32)
    same_head = (_iota((LANES, LANES), 0) // HEAD_DIM == _iota((LANES, LANES), 1) // HEAD_DIM)
    ss = _split_dot(x * x, same_head.astype(BF16))
    o_ref[0] = (x * lax.rsqrt(ss * (1.0 / HEAD_DIM) + EPS) * g_ref[0]).astype(o_ref.dtype)


def nsa_headnorm(y3, q_norm, k_norm, ts=1024):
    b, s, _ = y3.shape
    qb = C_NSA_Q // LANES
    ksb = C_NSA_KV // LANES + 2
    kwb = C_NSA_KV // LANES + 4
    gains = jnp.stack([jnp.tile(q_norm, 2)] * 4 + [jnp.tile(k_norm[1], 2), jnp.tile(k_norm[2], 2)])

    def col(j):
        return jnp.where(j < 4, qb + j, jnp.where(j == 4, ksb, kwb))

    return pl.pallas_call(
        _headnorm_kernel,
        out_shape=jax.ShapeDtypeStruct((b, s, 6 * LANES), BF16),
        grid=(b, s // ts, 6),
        in_specs=[pl.BlockSpec((1, ts, LANES), lambda i, t, j: (i, t, col(j))),
                  pl.BlockSpec((1, 1, LANES), lambda i, t, j: (j, 0, 0))],
        out_specs=pl.BlockSpec((1, ts, LANES), lambda i, t, j: (i, t, j)),
        compiler_params=_params("parallel", "parallel", "parallel"),
        name="nsa_headnorm",
    )(y3, gains.reshape(6, 1, LANES))


def _gelu_tanh(x):
    return 0.5 * x * (1.0 + jnp.tanh(0.7978845608028654 * (x + 0.044715 * (x * x * x))))


def _compress_kernel(ra_ref, rb_ref, pos_ref, w1_ref, b1_ref, w2_ref, b2_ref, kn_ref, kc_ref, vc_ref):
    half = (CMP_BLOCK // 2) * HEAD_DIM
    for j, o_ref in enumerate((kc_ref, vc_ref)):
        xa = (ra_ref[j, 0, 0].astype(F32) + pos_ref[j, :, 0:half]).astype(BF16)
        xb = (rb_ref[j, 0, 0].astype(F32) + pos_ref[j, :, half:2 * half]).astype(BF16)
        hid = (jnp.dot(xa, w1_ref[j, 0:half, :], preferred_element_type=F32)
               + jnp.dot(xb, w1_ref[j, half:2 * half, :], preferred_element_type=F32) + b1_ref[j])
        out = jnp.dot(_gelu_tanh(hid).astype(BF16), w2_ref[j], preferred_element_type=F32) + b2_ref[j]
        if j == 0:
            out = out * lax.rsqrt(jnp.mean(out * out, axis=-1, keepdims=True) + EPS) * kn_ref[...]
        o_ref[0, 0] = out


def nsa_compress(y3, cmp_pos, cmp_w1, cmp_b1, cmp_w2, cmp_b2, k_norm0):
    b, s, _ = y3.shape
    G, dh = NSA_KV_HEADS, HEAD_DIM
    nr = s // CMP_STRIDE
    wide = CMP_STRIDE * dh
    kv = y3[:, :, C_NSA_KV:C_NSA_KV + 2 * G * dh].reshape(b, s, 2, G, dh)
    ra = kv.transpose(2, 0, 3, 1, 4).reshape(2, b, G, nr, wide)
    rb = jnp.concatenate([ra[:, :, :, 1:], jnp.zeros((2, b, G, 1, wide), ra.dtype)], axis=3)
    hidden = cmp_w1.shape[-1]
    out = jax.ShapeDtypeStruct((b, G, nr, dh), F32)
    blk = pl.BlockSpec((2, 1, 1, nr, wide), lambda i, g: (0, i, g, 0, 0))
    oblk = pl.BlockSpec((1, 1, nr, dh), lambda i, g: (i, g, 0, 0))

    def full(shape):
        return pl.BlockSpec(shape, lambda i, g: (0,) * len(shape))

    return pl.pallas_call(
        _compress_kernel,
        out_shape=(out, out),
        grid=(b, G),
        in_specs=[blk, blk, full((2, 1, 2 * wide)), full((2, 2 * wide, hidden)), full((2, 1, hidden)),
                  full((2, hidden, dh)), full((2, 1, dh)), full((1, dh))],
        out_specs=(oblk, oblk),
        compiler_params=_params("parallel", "parallel"),
        name="nsa_compress",
    )(ra, rb, cmp_pos.reshape(2, 1, 2 * wide), cmp_w1.astype(BF16), cmp_b1.reshape(2, 1, hidden),
      cmp_w2.astype(BF16), cmp_b2.reshape(2, 1, dh), k_norm0.reshape(1, dh))


def _nsa_kernel_rowmajor(q_ref, gl_ref, kc_ref, vc_ref, ks_ref, vs_ref, kw_ref, vw_ref, o_ref, *, n_sel, top):
    QB, R, dh = 128, NSA_GROUP, HEAD_DIM
    rows = R * QB
    g = pl.program_id(1)
    qi = pl.program_id(2)
    q0 = qi * QB
    q = q_ref[0, 0, 0] * jnp.asarray(dh ** -0.5, BF16)
    row = _iota((rows, 1), 0)
    t_f = (q0 + (row & (QB - 1))).astype(F32)
    head = g * R + (row >> 7)
    slope = jnp.exp2(-(head + 1).astype(F32))

    n_cmp = kc_ref.shape[2]
    kc = kc_ref[0, 0]
    kc_hi = kc.astype(BF16)
    kc_lo = (kc - kc_hi.astype(F32)).astype(BF16)
    cmp_end = (_iota((1, n_cmp), 1) * CMP_STRIDE + (CMP_BLOCK - 1)).astype(F32)
    dist = t_f - cmp_end
    valid = dist >= 0.0
    s = jnp.where(valid, _dot_nt(q, kc_hi) + _dot_nt(q, kc_lo) - slope * dist, NEG)
    e = jnp.exp(s - jnp.max(s, axis=1, keepdims=True))
    p = jnp.where(valid, e / jnp.sum(e, axis=1, keepdims=True), 0.0)
    o_cmp = jnp.dot(p.astype(BF16), vc_ref[0, 0].astype(BF16), preferred_element_type=F32)

    p_grp = p[0:QB] + p[QB:2 * QB] + p[2 * QB:3 * QB] + p[3 * QB:4 * QB]
    c0 = _iota((n_cmp, n_sel), 0) * CMP_STRIDE
    s0 = _iota((n_cmp, n_sel), 1) * SEL_BLOCK
    overlap = ((c0 < s0 + SEL_BLOCK) & (c0 + CMP_BLOCK > s0)).astype(BF16)
    imp = _split_dot(p_grp, overlap)
    tq = q0 + _iota((QB, 1), 0)
    j_idx = _iota((1, n_sel), 1)
    cur = tq >> 6
    forced = (j_idx == 0) | (j_idx == cur) | (j_idx == cur - 1)
    imp = jnp.where(j_idx * SEL_BLOCK <= tq, imp + jnp.where(forced, FORCE_BONUS, 0.0), -1.0)
    sel = jnp.zeros((QB, n_sel), F32)
    for _ in range(top):
        mx = jnp.max(imp, axis=1, keepdims=True)
        first = jnp.min(jnp.where(imp == mx, j_idx, n_sel), axis=1, keepdims=True)
        pick = j_idx == first
        sel = jnp.where(pick, 1.0, sel)
        imp = jnp.where(pick, -3e38, imp)
    sel_bf = sel.astype(BF16)

    def flash(k_ref, v_ref, lo, hi, mask_fn):
        def body(kb, st):
            m_run, l_run, acc = st
            k0 = pl.multiple_of(kb * QB, QB)
            kblk = k_ref[0, 0, pl.ds(k0, QB), :]
            vblk = v_ref[0, 0, pl.ds(k0, QB), :]
            dist = t_f - (k0 + _iota((1, QB), 1)).astype(F32)
            ok = mask_fn(kb, dist)
            sc = jnp.where(ok, _dot_nt(q, kblk) - slope * dist, NEG)
            m_new = jnp.maximum(m_run, jnp.max(sc, axis=1, keepdims=True))
            alpha = jnp.exp(m_run - m_new)
            pr = jnp.where(ok, jnp.exp(sc - m_new), 0.0)
            l_run = alpha * l_run + jnp.sum(pr, axis=1, keepdims=True)
            acc = alpha * acc + jnp.dot(pr.astype(BF16), vblk, preferred_element_type=F32)
            return m_new, l_run, acc

        init = (jnp.full((rows, 1), NEG, F32), jnp.zeros((rows, 1), F32), jnp.zeros((rows, dh), F32))
        _, l_run, acc = lax.fori_loop(lo, hi, body, init)
        return acc / l_run

    def sel_mask(kb, dist):
        expand = (_iota((n_sel, QB), 0) == 2 * kb + (_iota((n_sel, QB), 1) >> 6)).astype(BF16)
        m = jnp.dot(sel_bf, expand, preferred_element_type=F32)
        m = jnp.concatenate([m] * R, axis=0)
        return (m > 0.5) & (dist >= 0.0)

    def win_mask(kb, dist):
        return (dist >= 0.0) & (dist < float(WINDOW))

    o_sel = flash(ks_ref, vs_ref, 0, qi + 1, sel_mask)
    o_win = flash(kw_ref, vw_ref, jnp.maximum(qi - WINDOW // QB, 0), qi + 1, win_mask)

    gate = jax.nn.sigmoid(gl_ref[0, 0, 0])
    o_ref[0, 0, 0] = (gate[:, 0:1] * o_cmp + gate[:, 1:2] * o_sel + gate[:, 2:3] * o_win).astype(o_ref.dtype)


def nsa_attention_rowmajor(y3, small3, normed, kc, vc):
    b, s, _ = y3.shape
    G, R, dh, QB = NSA_KV_HEADS, NSA_GROUP, HEAD_DIM, 128
    nq = s // QB
    n_sel = s // SEL_BLOCK
    top = min(SEL_TOPK, n_sel)

    def stack_heads(a, width):
        return (a.reshape(b, nq, QB, G, R, width).transpose(0, 3, 1, 4, 2, 5)
                .reshape(b, G, nq, R * QB, width))

    def kv_heads(a):
        return a.reshape(b, s, G, dh).transpose(0, 2, 1, 3)

    q = stack_heads(normed[:, :, 0:MIX_WIDTH], dh)
    gl = stack_heads(small3[:, :, S_NSA_GATE:S_NSA_GATE + NSA_HEADS * N_BRANCH], N_BRANCH)
    ks = kv_heads(normed[:, :, 4 * LANES:5 * LANES])
    kw = kv_heads(normed[:, :, 5 * LANES:6 * LANES])
    vs = kv_heads(y3[:, :, C_NSA_KV + 3 * LANES:C_NSA_KV + 4 * LANES])
    vw = kv_heads(y3[:, :, C_NSA_KV + 5 * LANES:C_NSA_KV + 6 * LANES])
    n_cmp = kc.shape[2]
    qspec = pl.BlockSpec((1, 1, 1, R * QB, dh), lambda i, g, j: (i, g, j, 0, 0))
    gspec = pl.BlockSpec((1, 1, 1, R * QB, N_BRANCH), lambda i, g, j: (i, g, j, 0, 0))
    cspec = pl.BlockSpec((1, 1, n_cmp, dh), lambda i, g, j: (i, g, 0, 0))
    kvspec = pl.BlockSpec((1, 1, s, dh), lambda i, g, j: (i, g, 0, 0))
    out = pl.pallas_call(
        functools.partial(_nsa_kernel_rowmajor, n_sel=n_sel, top=top),
        out_shape=jax.ShapeDtypeStruct((b, G, nq, R * QB, dh), BF16),
        grid=(b, G, nq),
        in_specs=[qspec, gspec, cspec, cspec, kvspec, kvspec, kvspec, kvspec],
        out_specs=qspec,
        compiler_params=_params("parallel", "parallel", "parallel"),
        name="nsa_attention",
    )(q, gl, kc, vc, ks, vs, kw, vw)
    return (out.reshape(b, G, nq, R, QB, dh).transpose(0, 2, 4, 1, 3, 5).reshape(b, s, MIX_WIDTH))


NSA_QB = 128
NSA_KS = 512
NSA_CK = 256
NSA_VR = 80
A_FEAT, A_PEN, A_BIAS = 0, 64, 128
A_DUMMY = A_BIAS + 4


def _head_rms(x, gain):
    w = x.shape[1]
    same_head = (_iota((w, w), 0) // HEAD_DIM == _iota((w, w), 1) // HEAD_DIM).astype(BF16)
    ss = _split_dot(x * x, same_head)
    return x * lax.rsqrt(ss * (1.0 / HEAD_DIM) + EPS) * gain


def _nsa_kernel(q_ref, gl_ref, kca_ref, vcT_ref, ks_ref, kw_ref, vs_ref, vw_ref, kconst_ref, gq_ref, gk_ref,
                o_ref, qt_ref, ksa_ref, kwa_ref, vsa_ref, vwa_ref, s_ref, *, n_sel, top):
    QB, R, dh, CK, VR, KS = NSA_QB, NSA_GROUP, HEAD_DIM, NSA_CK, NSA_VR, NSA_KS
    HQ = R * QB
    NPAD = WINDOW // QB
    SUB = KS // QB
    g = pl.program_id(1)
    qi = pl.program_id(2)
    q0 = qi * QB
    nkb = vsa_ref.shape[0]

    @pl.when(qi == 0)
    def _():
        ksa_ref[...] = kconst_ref[...]
        kwa_ref[0:WINDOW, :] = jnp.where(_iota((WINDOW, CK), 1) == A_DUMMY, 1.0, 0.0).astype(BF16)
        kwa_ref[WINDOW:, :] = kconst_ref[...]
        kwa_ref[WINDOW:, A_PEN:A_PEN + 64] = jnp.zeros((kwa_ref.shape[0] - WINDOW, 64), BF16)
        for gg in range(NSA_KV_HEADS):
            @pl.when(g == gg)
            def _():
                heads = slice(gg * dh, (gg + 1) * dh)
                for c in range(ks_ref.shape[1] // KS):
                    keys = slice(c * KS, (c + 1) * KS)
                    ks_n = _head_rms(ks_ref[0, keys, :].astype(F32), gk_ref[0])
                    kw_n = _head_rms(kw_ref[0, keys, :].astype(F32), gk_ref[1])
                    ksa_ref[keys, A_FEAT:A_FEAT + dh] = ks_n[:, heads].astype(BF16)
                    kwa_ref[WINDOW + c * KS:WINDOW + (c + 1) * KS, A_FEAT:A_FEAT + dh] = kw_n[:, heads].astype(BF16)
                for c in range(nkb):
                    keys = slice(c * QB, (c + 1) * QB)
                    vsa_ref[c, 0:dh, :] = vs_ref[0, keys, :].astype(F32).T[heads].astype(BF16)
                    vwa_ref[NPAD + c, 0:dh, :] = vw_ref[0, keys, :].astype(F32).T[heads].astype(BF16)
        vwa_ref[0:NPAD, 0:dh, :] = jnp.zeros((NPAD, dh, QB), BF16)
        ones_rows = jnp.where(_iota((nkb + NPAD, VR - dh, QB), 1) == 0, 1.0, 0.0).astype(BF16)
        vsa_ref[:, dh:VR, :] = ones_rows[0:nkb]
        vwa_ref[:, dh:VR, :] = ones_rows
        qt_ref[A_BIAS + 16:CK, :] = jnp.zeros((CK - A_BIAS - 16, HQ), BF16)

    q_n = _head_rms(q_ref[0].astype(F32), gq_ref[...]).astype(BF16)
    q_rows = (q_n.astype(F32) * (dh ** -0.5)).T
    qT = jnp.concatenate([q_rows[r * dh:(r + 1) * dh] for r in range(R)], axis=1).astype(BF16)
    qt_ref[A_FEAT:A_FEAT + dh, :] = qT
    qt_ref[A_PEN:A_PEN + dh, :] = qT
    lane = _iota((16, HQ), 1)
    rowi = _iota((16, HQ), 0)
    t_q = q0 + (lane & (QB - 1))
    slope = jnp.exp2(-(g * R + (lane >> 7) + 1).astype(F32))
    t_hi = ((t_q >> 6) << 6).astype(F32)
    t_lo = (t_q & 63).astype(F32)
    bias_rows = jnp.where(rowi < 2, slope,
                          jnp.where(rowi == 2, -slope * t_hi,
                                    jnp.where(rowi == 3, -slope * t_lo,
                                              jnp.where(rowi == A_DUMMY - A_BIAS, NEG, 0.0))))
    qt_ref[A_BIAS:A_BIAS + 16, :] = bias_rows.astype(BF16)
    k_loc = _iota((QB, HQ), 0)
    q_loc = _iota((QB, HQ), 1) & (QB - 1)

    def pv(v_ref_, kb0, pr):
        out = None
        for i in range(pr.shape[0] // QB):
            term = jnp.dot(v_ref_[kb0 + i], pr[i * QB:(i + 1) * QB], preferred_element_type=F32)
            out = term if out is None else out + term
        return out

    n_cmp = kca_ref.shape[2]
    sc = jnp.dot(kca_ref[0, 0], qt_ref[...], preferred_element_type=F32)
    cmp_end = _iota((n_cmp, HQ), 0) * CMP_STRIDE + (CMP_BLOCK - 1)
    valid = cmp_end <= q0 + (_iota((n_cmp, HQ), 1) & (QB - 1))
    sc = jnp.where(valid, sc, NEG)
    e = jnp.exp(sc - jnp.max(sc, axis=0, keepdims=True))
    p = jnp.where(valid, e * (1.0 / jnp.sum(e, axis=0, keepdims=True)), 0.0)
    o_cmp = jnp.dot(vcT_ref[0, 0], p.astype(BF16), preferred_element_type=F32)

    sw = jnp.dot(kwa_ref[pl.ds(pl.multiple_of(q0, QB), WINDOW + QB), :], qt_ref[...],
                 preferred_element_type=F32)
    sw = jnp.concatenate([jnp.where(k_loc > q_loc, sw[0:QB], NEG), sw[QB:WINDOW],
                          jnp.where(k_loc <= q_loc, sw[WINDOW:WINDOW + QB], NEG)], axis=0)
    pw = jnp.exp(sw - jnp.max(sw, axis=0, keepdims=True)).astype(BF16)
    acc_w = pv(vwa_ref, qi, pw)
    o_win = acc_w[0:dh] / acc_w[dh:dh + 1]

    p_grp = p[:, 0:QB] + p[:, QB:2 * QB] + p[:, 2 * QB:3 * QB] + p[:, 3 * QB:4 * QB]
    c0 = _iota((n_sel, n_cmp), 1) * CMP_STRIDE
    s0 = _iota((n_sel, n_cmp), 0) * SEL_BLOCK
    overlap_t = ((c0 < s0 + SEL_BLOCK) & (c0 + CMP_BLOCK > s0)).astype(BF16)
    imp = _split_dot_left(overlap_t, p_grp)
    j_idx = _iota((n_sel, QB), 0)
    tq = q0 + _iota((n_sel, QB), 1)
    cur = tq >> 6
    forced = (j_idx == 0) | (j_idx == cur) | (j_idx == cur - 1)
    causal_blk = j_idx * SEL_BLOCK <= tq
    imp = jnp.where(causal_blk, imp + jnp.where(forced, FORCE_BONUS, 0.0), -1.0)
    sel = jnp.zeros((n_sel, QB), F32)
    for _ in range(top):
        mx = jnp.max(imp, axis=0, keepdims=True)
        first = jnp.min(jnp.where(imp == mx, j_idx, n_sel), axis=0, keepdims=True)
        pick = j_idx == first
        sel = jnp.where(pick, 1.0, sel)
        imp = jnp.where(pick, -3e38, imp)
    pen = jnp.where((sel > 0.5) & causal_blk, 0.0, NEG)
    if n_sel < 64:
        pen = jnp.concatenate([pen, jnp.zeros((64 - n_sel, QB), F32)], axis=0)
    qt_ref[A_PEN:A_PEN + 64, :] = jnp.concatenate([pen] * R, axis=1).astype(BF16)

    def score(j):
        return jnp.dot(ksa_ref[pl.ds(pl.multiple_of(j * KS, KS), KS), :], qt_ref[...],
                       preferred_element_type=F32)

    def absorb(s, j, st):
        m_run, acc = st
        m_new = jnp.maximum(m_run, jnp.max(s, axis=0, keepdims=True))
        pr = jnp.exp(s - m_new).astype(BF16)
        return m_new, jnp.exp(m_run - m_new) * acc + pv(vsa_ref, j * SUB, pr)

    def body(j, carry):
        s_cur, st = carry
        s_next = score(j + 1)
        return s_next, absorb(s_cur, j, st)

    n_full = qi // SUB
    init = (jnp.full((1, HQ), NEG, F32), jnp.zeros((VR, HQ), F32))
    s_last, st = lax.fori_loop(0, n_full, body, (score(0), init))
    s_ref[...] = s_last
    diag = pl.ds(pl.multiple_of(q0 - n_full * KS, QB), QB)
    s_ref[diag, :] = jnp.where(k_loc <= q_loc, s_ref[diag, :], NEG)
    _, acc_s = absorb(s_ref[...], n_full, st)
    o_sel = acc_s[0:dh] / acc_s[dh:dh + 1]

    gl_t = gl_ref[0].T
    gate = []
    for br in range(N_BRANCH):
        per_kv = [jnp.concatenate([gl_t[(gg * R + r) * N_BRANCH + br:(gg * R + r) * N_BRANCH + br + 1]
                                   for r in range(R)], axis=1) for gg in range(NSA_KV_HEADS)]
        gate.append(jax.nn.sigmoid(jnp.where(g == 0, per_kv[0], per_kv[1])))
    o_t = gate[0] * o_cmp + gate[1] * o_sel + gate[2] * o_win
    o_ref[0] = jnp.concatenate([o_t[:, r * QB:(r + 1) * QB] for r in range(R)], axis=0).T.astype(o_ref.dtype)


def _nsa_pair_kernel(q_ref, gl_ref, kca_ref, vcT_ref, ks_ref, kw_ref, vs_ref, vw_ref, kconst_ref, gq_ref, gk_ref,
                     o_ref, qt_ref, ksa_ref, kwa_ref, vsa_ref, vwa_ref, s_ref, *, n_sel, top):
    QB, R, dh, CK, VR, KS = NSA_QB, NSA_GROUP, HEAD_DIM, NSA_CK, NSA_VR, NSA_KS
    G = NSA_KV_HEADS
    GR = range(G)
    HQ = R * QB
    NPAD = WINDOW // QB
    SUB = KS // QB
    qi = pl.program_id(1)
    q0 = qi * QB
    nkb = vsa_ref.shape[1]

    @pl.when(qi == 0)
    def _():
        pad_keys = jnp.where(_iota((WINDOW, CK), 1) == A_DUMMY, 1.0, 0.0).astype(BF16)
        ones_rows = jnp.where(_iota((nkb + NPAD, VR - dh, QB), 1) == 0, 1.0, 0.0).astype(BF16)
        for g in GR:
            heads = slice(g * dh, (g + 1) * dh)
            ksa_ref[g] = kconst_ref[...]
            kwa_ref[g, 0:WINDOW, :] = pad_keys
            kwa_ref[g, WINDOW:, :] = kconst_ref[...]
            kwa_ref[g, WINDOW:, A_PEN:A_PEN + 64] = jnp.zeros((kwa_ref.shape[1] - WINDOW, 64), BF16)
            vwa_ref[g, 0:NPAD, 0:dh, :] = jnp.zeros((NPAD, dh, QB), BF16)
            vsa_ref[g, :, dh:VR, :] = ones_rows[0:nkb]
            vwa_ref[g, :, dh:VR, :] = ones_rows
        for c in range(ks_ref.shape[1] // KS):
            keys = slice(c * KS, (c + 1) * KS)
            ks_n = _head_rms(ks_ref[0, keys, :].astype(F32), gk_ref[0]).astype(BF16)
            kw_n = _head_rms(kw_ref[0, keys, :].astype(F32), gk_ref[1]).astype(BF16)
            for g in GR:
                heads = slice(g * dh, (g + 1) * dh)
                ksa_ref[g, keys, A_FEAT:A_FEAT + dh] = ks_n[:, heads]
                kwa_ref[g, WINDOW + c * KS:WINDOW + (c + 1) * KS, A_FEAT:A_FEAT + dh] = kw_n[:, heads]
        for c in range(nkb):
            keys = slice(c * QB, (c + 1) * QB)
            vs_t = vs_ref[0, keys, :].astype(F32).T.astype(BF16)
            vw_t = vw_ref[0, keys, :].astype(F32).T.astype(BF16)
            for g in GR:
                heads = slice(g * dh, (g + 1) * dh)
                vsa_ref[g, c, 0:dh, :] = vs_t[heads]
                vwa_ref[g, NPAD + c, 0:dh, :] = vw_t[heads]
        qt_ref[:, A_BIAS + 16:CK, :] = jnp.zeros((G, CK - A_BIAS - 16, HQ), BF16)

    q_n = _head_rms(q_ref[0].astype(F32), gq_ref[...]).astype(BF16)
    q_rows = (q_n.astype(F32) * (dh ** -0.5)).T
    lane = _iota((16, HQ), 1)
    rowi = _iota((16, HQ), 0)
    t_q = q0 + (lane & (QB - 1))
    t_hi = ((t_q >> 6) << 6).astype(F32)
    t_lo = (t_q & 63).astype(F32)
    for g in GR:
        qT = jnp.concatenate([q_rows[(g * R + r) * dh:(g * R + r + 1) * dh] for r in range(R)],
                             axis=1).astype(BF16)
        qt_ref[g, A_FEAT:A_FEAT + dh, :] = qT
        qt_ref[g, A_PEN:A_PEN + dh, :] = qT
        slope = jnp.exp2(-(g * R + (lane >> 7) + 1).astype(F32))
        bias_rows = jnp.where(rowi < 2, slope,
                              jnp.where(rowi == 2, -slope * t_hi,
                                        jnp.where(rowi == 3, -slope * t_lo,
                                                  jnp.where(rowi == A_DUMMY - A_BIAS, NEG, 0.0))))
        qt_ref[g, A_BIAS:A_BIAS + 16, :] = bias_rows.astype(BF16)
    k_loc = _iota((QB, HQ), 0)
    q_loc = _iota((QB, HQ), 1) & (QB - 1)

    def pv(v_ref_, g, kb0, pr):
        out = None
        for i in range(pr.shape[0] // QB):
            term = jnp.dot(v_ref_[g, kb0 + i], pr[i * QB:(i + 1) * QB], preferred_element_type=F32)
            out = term if out is None else out + term
        return out

    n_cmp = kca_ref.shape[2]
    cmp_end = _iota((n_cmp, HQ), 0) * CMP_STRIDE + (CMP_BLOCK - 1)
    valid = cmp_end <= q0 + (_iota((n_cmp, HQ), 1) & (QB - 1))
    sc = [jnp.where(valid, jnp.dot(kca_ref[0, g], qt_ref[g], preferred_element_type=F32), NEG) for g in GR]
    e = [jnp.exp(sc[g] - jnp.max(sc[g], axis=0, keepdims=True)) for g in GR]
    p = [jnp.where(valid, e[g] * (1.0 / jnp.sum(e[g], axis=0, keepdims=True)), 0.0) for g in GR]
    o_cmp = [jnp.dot(vcT_ref[0, g], p[g].astype(BF16), preferred_element_type=F32) for g in GR]

    win_rows = pl.ds(pl.multiple_of(q0, QB), WINDOW + QB)
    sw = [jnp.dot(kwa_ref[g, win_rows, :], qt_ref[g], preferred_element_type=F32) for g in GR]
    sw = [jnp.concatenate([jnp.where(k_loc > q_loc, sw[g][0:QB], NEG), sw[g][QB:WINDOW],
                           jnp.where(k_loc <= q_loc, sw[g][WINDOW:WINDOW + QB], NEG)], axis=0) for g in GR]
    pw = [jnp.exp(sw[g] - jnp.max(sw[g], axis=0, keepdims=True)).astype(BF16) for g in GR]
    acc_w = [pv(vwa_ref, g, qi, pw[g]) for g in GR]
    o_win = [acc_w[g][0:dh] / acc_w[g][dh:dh + 1] for g in GR]

    c0 = _iota((n_sel, n_cmp), 1) * CMP_STRIDE
    s0 = _iota((n_sel, n_cmp), 0) * SEL_BLOCK
    overlap_t = ((c0 < s0 + SEL_BLOCK) & (c0 + CMP_BLOCK > s0)).astype(BF16)
    j_idx = _iota((n_sel, QB), 0)
    tq = q0 + _iota((n_sel, QB), 1)
    cur = tq >> 6
    forced = (j_idx == 0) | (j_idx == cur) | (j_idx == cur - 1)
    causal_blk = j_idx * SEL_BLOCK <= tq
    p_grp = [p[g][:, 0:QB] + p[g][:, QB:2 * QB] + p[g][:, 2 * QB:3 * QB] + p[g][:, 3 * QB:4 * QB] for g in GR]
    imp = [jnp.where(causal_blk, _split_dot_left(overlap_t, p_grp[g]) + jnp.where(forced, FORCE_BONUS, 0.0), -1.0)
           for g in GR]
    sel = [jnp.zeros((n_sel, QB), F32) for g in GR]
    for _ in range(top):
        for g in GR:
            mx = jnp.max(imp[g], axis=0, keepdims=True)
            first = jnp.min(jnp.where(imp[g] == mx, j_idx, n_sel), axis=0, keepdims=True)
            pick = j_idx == first
            sel[g] = jnp.where(pick, 1.0, sel[g])
            imp[g] = jnp.where(pick, -3e38, imp[g])
    for g in GR:
        pen = jnp.where((sel[g] > 0.5) & causal_blk, 0.0, NEG)
        if n_sel < 64:
            pen = jnp.concatenate([pen, jnp.zeros((64 - n_sel, QB), F32)], axis=0)
        qt_ref[g, A_PEN:A_PEN + 64, :] = jnp.concatenate([pen] * R, axis=1).astype(BF16)

    def score(j):
        rows = pl.ds(pl.multiple_of(j * KS, KS), KS)
        return [jnp.dot(ksa_ref[g, rows, :], qt_ref[g], preferred_element_type=F32) for g in GR]

    def absorb(s, j, st):
        m_new = [jnp.maximum(st[g][0], jnp.max(s[g], axis=0, keepdims=True)) for g in GR]
        pr = [jnp.exp(s[g] - m_new[g]).astype(BF16) for g in GR]
        return [(m_new[g], jnp.exp(st[g][0] - m_new[g]) * st[g][1] + pv(vsa_ref, g, j * SUB, pr[g])) for g in GR]

    n_full = qi // SUB
    init = [(jnp.full((1, HQ), NEG, F32), jnp.zeros((VR, HQ), F32)) for g in GR]
    st = lax.fori_loop(0, n_full, lambda j, st_: absorb(score(j), j, st_), init)
    s_last = score(n_full)
    diag = pl.ds(pl.multiple_of(q0 - n_full * KS, QB), QB)
    for g in GR:
        s_ref[g] = s_last[g]
        s_ref[g, diag, :] = jnp.where(k_loc <= q_loc, s_ref[g, diag, :], NEG)
    st = absorb([s_ref[g] for g in GR], n_full, st)
    o_sel = [st[g][1][0:dh] / st[g][1][dh:dh + 1] for g in GR]

    gl_t = gl_ref[0].T
    rows_out = []
    for g in GR:
        gate = [jax.nn.sigmoid(jnp.concatenate(
            [gl_t[(g * R + r) * N_BRANCH + br:(g * R + r) * N_BRANCH + br + 1] for r in range(R)], axis=1))
            for br in range(N_BRANCH)]
        o_t = gate[0] * o_cmp[g] + gate[1] * o_sel[g] + gate[2] * o_win[g]
        rows_out += [o_t[:, r * QB:(r + 1) * QB] for r in range(R)]
    o_ref[0] = jnp.concatenate(rows_out, axis=0).T.astype(o_ref.dtype)


def _nsa_t_kernel(qn_ref, gl_ref, kca_ref, vcT_ref, ksn_ref, kwn_ref, vs_ref, vw_ref, kconst_ref, o_ref,
                  qt_ref, ksa_ref, kwa_ref, vsa_ref, vwa_ref, *, n_sel, top):
    QB, R, dh, CK, VR, KS = NSA_QB, NSA_GROUP, HEAD_DIM, NSA_CK, NSA_VR, NSA_KS
    HQ = R * QB
    g = pl.program_id(1)
    qi = pl.program_id(2)
    q0 = qi * QB
    nkb = vsa_ref.shape[0]

    @pl.when(qi == 0)
    def _():
        ksa_ref[...] = kconst_ref[...]
        kwa_ref[...] = kconst_ref[...]
        kwa_ref[:, A_PEN:A_PEN + 64] = jnp.zeros((kwa_ref.shape[0], 64), BF16)
        for gg in range(NSA_KV_HEADS):
            @pl.when(g == gg)
            def _():
                heads = slice(gg * dh, (gg + 1) * dh)
                ksa_ref[:, A_FEAT:A_FEAT + dh] = ksn_ref[0, :, heads]
                kwa_ref[:, A_FEAT:A_FEAT + dh] = kwn_ref[0, :, heads]
                for c in range(nkb):
                    keys = slice(c * QB, (c + 1) * QB)
                    vsa_ref[c, 0:dh, :] = vs_ref[0, keys, :].astype(F32).T[heads].astype(BF16)
                    vwa_ref[c, 0:dh, :] = vw_ref[0, keys, :].astype(F32).T[heads].astype(BF16)
        ones_rows = jnp.where(_iota((nkb, VR - dh, QB), 1) == 0, 1.0, 0.0).astype(BF16)
        vsa_ref[:, dh:VR, :] = ones_rows
        vwa_ref[:, dh:VR, :] = ones_rows
        qt_ref[A_BIAS + 16:CK, :] = jnp.zeros((CK - A_BIAS - 16, HQ), BF16)

    q_rows = (qn_ref[0].astype(F32) * (dh ** -0.5)).T
    qT = jnp.concatenate([q_rows[r * dh:(r + 1) * dh] for r in range(R)], axis=1).astype(BF16)
    qt_ref[A_FEAT:A_FEAT + dh, :] = qT
    qt_ref[A_PEN:A_PEN + dh, :] = qT
    lane = _iota((16, HQ), 1)
    rowi = _iota((16, HQ), 0)
    t_q = q0 + (lane & (QB - 1))
    slope = jnp.exp2(-(g * R + (lane >> 7) + 1).astype(F32))
    t_hi = ((t_q >> 6) << 6).astype(F32)
    t_lo = (t_q & 63).astype(F32)
    bias_rows = jnp.where(rowi < 2, slope,
                          jnp.where(rowi == 2, -slope * t_hi, jnp.where(rowi == 3, -slope * t_lo, 0.0)))
    qt_ref[A_BIAS:A_BIAS + 16, :] = bias_rows.astype(BF16)

    n_cmp = kca_ref.shape[2]
    sc = jnp.dot(kca_ref[0, 0], qt_ref[...], preferred_element_type=F32)
    cmp_end = _iota((n_cmp, HQ), 0) * CMP_STRIDE + (CMP_BLOCK - 1)
    valid = cmp_end <= q0 + (_iota((n_cmp, HQ), 1) & (QB - 1))
    sc = jnp.where(valid, sc, NEG)
    e = jnp.exp(sc - jnp.max(sc, axis=0, keepdims=True))
    p = jnp.where(valid, e / jnp.sum(e, axis=0, keepdims=True), 0.0)
    o_cmp = jnp.dot(vcT_ref[0, 0], p.astype(BF16), preferred_element_type=F32)

    p_grp = p[:, 0:QB] + p[:, QB:2 * QB] + p[:, 2 * QB:3 * QB] + p[:, 3 * QB:4 * QB]
    c0 = _iota((n_sel, n_cmp), 1) * CMP_STRIDE
    s0 = _iota((n_sel, n_cmp), 0) * SEL_BLOCK
    overlap_t = ((c0 < s0 + SEL_BLOCK) & (c0 + CMP_BLOCK > s0)).astype(BF16)
    imp = _split_dot_left(overlap_t, p_grp)
    j_idx = _iota((n_sel, QB), 0)
    tq = q0 + _iota((n_sel, QB), 1)
    cur = tq >> 6
    forced = (j_idx == 0) | (j_idx == cur) | (j_idx == cur - 1)
    imp = jnp.where(j_idx * SEL_BLOCK <= tq, imp + jnp.where(forced, FORCE_BONUS, 0.0), -1.0)
    sel = jnp.zeros((n_sel, QB), F32)
    for _ in range(top):
        mx = jnp.max(imp, axis=0, keepdims=True)
        first = jnp.min(jnp.where(imp == mx, j_idx, n_sel), axis=0, keepdims=True)
        pick = j_idx == first
        sel = jnp.where(pick, 1.0, sel)
        imp = jnp.where(pick, -3e38, imp)
    pen = jnp.where(sel > 0.5, 0.0, NEG)
    if n_sel < 64:
        pen = jnp.concatenate([pen, jnp.zeros((64 - n_sel, QB), F32)], axis=0)
    qt_ref[A_PEN:A_PEN + 64, :] = jnp.concatenate([pen] * R, axis=1).astype(BF16)

    def attend(kaug_ref, vaug_ref, k0, nk, st, mode):
        k0 = pl.multiple_of(k0, QB)
        s = jnp.dot(kaug_ref[pl.ds(k0, nk), :], qt_ref[...], preferred_element_type=F32)
        if mode != "full":
            dist = (q0 + (_iota((nk, HQ), 1) & (QB - 1))) - (k0 + _iota((nk, HQ), 0))
            ok = dist >= 0
            if mode == "window":
                ok = ok & (dist < WINDOW)
            s = jnp.where(ok, s, NEG)
        m_run, acc = st
        m_new = jnp.maximum(m_run, jnp.max(s, axis=0, keepdims=True))
        pr = jnp.exp(s - m_new).astype(BF16)
        acc = jnp.exp(m_run - m_new) * acc
        kb0 = k0 // QB
        for i in range(nk // QB):
            acc = acc + jnp.dot(vaug_ref[kb0 + i], pr[i * QB:(i + 1) * QB], preferred_element_type=F32)
        return m_new, acc

    def finish(st):
        return st[1][0:dh] / st[1][dh:dh + 1]

    init = (jnp.full((1, HQ), NEG, F32), jnp.zeros((VR, HQ), F32))
    n_full = qi // (KS // QB)
    st = lax.fori_loop(0, n_full, lambda j, s_: attend(ksa_ref, vsa_ref, j * KS, KS, s_, "full"), init)
    o_sel = finish(attend(ksa_ref, vsa_ref, n_full * KS, KS, st, "causal"))
    o_win = finish(attend(kwa_ref, vwa_ref, jnp.maximum(q0 - WINDOW, 0), WINDOW + QB, init, "window"))

    gl_t = gl_ref[0].T
    gate = []
    for br in range(N_BRANCH):
        per_kv = [jnp.concatenate([gl_t[(gg * R + r) * N_BRANCH + br:(gg * R + r) * N_BRANCH + br + 1]
                                   for r in range(R)], axis=1) for gg in range(NSA_KV_HEADS)]
        gate.append(jax.nn.sigmoid(jnp.where(g == 0, per_kv[0], per_kv[1])))
    o_t = gate[0] * o_cmp + gate[1] * o_sel + gate[2] * o_win
    o_ref[0] = jnp.concatenate([o_t[:, r * QB:(r + 1) * QB] for r in range(R)], axis=0).T.astype(o_ref.dtype)


def nsa_attention(y3, small3, q_norm, k_norm, kc, vc):
    b, s, _ = y3.shape
    G, R, dh, QB, CK, VR = NSA_KV_HEADS, NSA_GROUP, HEAD_DIM, NSA_QB, NSA_CK, NSA_VR
    assert G == 2, "the kernel picks a kv head's gate rows with a two-way select"
    HQ = R * QB
    nq = s // QB
    n_sel = s // SEL_BLOCK
    assert n_sel <= 64, "selection one-hot columns hold at most 64 blocks"
    top = min(SEL_TOPK, n_sel)
    n_cmp = kc.shape[2]

    def pos_cols(pos):
        return np.stack([pos // 64 * 64, pos % 64, np.ones_like(pos), np.ones_like(pos)], axis=1)

    vc_t = vc.transpose(0, 1, 3, 2).astype(BF16)

    pos = np.arange(s)
    kconst = np.zeros((s, CK), np.float32)
    kconst[pos, A_PEN + pos // SEL_BLOCK] = 1.0
    kconst[:, A_BIAS:A_BIAS + 4] = pos_cols(pos)
    kconst = jnp.asarray(kconst, BF16)

    kc_hi = kc.astype(BF16)
    kc_lo = (kc - kc_hi.astype(F32)).astype(BF16)
    cend = np.arange(n_cmp) * CMP_STRIDE + (CMP_BLOCK - 1)
    cbias = np.zeros((n_cmp, CK - 2 * dh), np.float32)
    cbias[:, 0:4] = pos_cols(cend)
    kc_aug = jnp.concatenate([kc_hi, kc_lo, jnp.broadcast_to(jnp.asarray(cbias, BF16), (b, G, n_cmp, CK - 2 * dh))],
                             axis=-1)

    qw = G * R * dh
    kvb = C_NSA_KV // LANES
    gq = jnp.tile(q_norm, G * R).reshape(1, qw)
    gk = jnp.stack([jnp.tile(k_norm[1], G), jnp.tile(k_norm[2], G)]).reshape(2, 1, LANES)

    def kv_spec(blk):
        return pl.BlockSpec((1, s, LANES), lambda i, j: (i, 0, kvb + blk))

    return pl.pallas_call(
        functools.partial(_nsa_pair_kernel, n_sel=n_sel, top=top),
        out_shape=jax.ShapeDtypeStruct((b, s, MIX_WIDTH), BF16),
        grid=(b, nq),
        in_specs=[pl.BlockSpec((1, QB, qw), lambda i, j: (i, j, C_NSA_Q // qw)),
                  pl.BlockSpec((1, QB, N_SMALL), lambda i, j: (i, j, 0)),
                  pl.BlockSpec((1, G, n_cmp, CK), lambda i, j: (i, 0, 0, 0)),
                  pl.BlockSpec((1, G, dh, n_cmp), lambda i, j: (i, 0, 0, 0)),
                  kv_spec(2), kv_spec(4), kv_spec(3), kv_spec(5),
                  pl.BlockSpec((s, CK), lambda i, j: (0, 0)),
                  pl.BlockSpec((1, qw), lambda i, j: (0, 0)),
                  pl.BlockSpec((2, 1, LANES), lambda i, j: (0, 0, 0))],
        out_specs=pl.BlockSpec((1, QB, qw), lambda i, j: (i, j, 0)),
        scratch_shapes=[pltpu.VMEM((G, CK, HQ), BF16),
                        pltpu.VMEM((G, s, CK), BF16), pltpu.VMEM((G, s + WINDOW, CK), BF16),
                        pltpu.VMEM((G, nq, VR, QB), BF16), pltpu.VMEM((G, nq + WINDOW // QB, VR, QB), BF16),
                        pltpu.VMEM((G, NSA_KS, HQ), F32)],
        compiler_params=_params("parallel", "arbitrary"),
        name="nsa_attention",
    )(y3, small3, kc_aug, vc_t, y3, y3, y3, y3, kconst, gq, gk)


def _merge_kernel(on_ref, os_ref, om_ref, g0_ref, g1_ref, g2_ref, wb_ref, wo_ref, x_ref, mod_ref, o_ref):
    merged = None
    for i, (o_r, g_r) in enumerate(((on_ref, g0_ref), (os_ref, g1_ref), (om_ref, g2_ref))):
        br = jnp.dot(o_r[0], wb_ref[i], preferred_element_type=F32)
        term = jax.nn.sigmoid(g_r[0].astype(F32)) * br
        merged = term if merged is None else merged + term
    out = jnp.dot(merged.astype(BF16), wo_ref[...], preferred_element_type=F32)
    o_ref[0] = x_ref[0] + mod_ref[0, 2:3, :] * out


def merge_project(o_nsa, o_sb, o_ml, y3, w_branch, w_out, x, mod, tm=512):
    b, s, d = x.shape
    W = MIX_WIDTH
    ospec = pl.BlockSpec((1, tm, W), lambda i, j: (i, j, 0))
    xspec = pl.BlockSpec((1, tm, d), lambda i, j: (i, j, 0))
    gspecs = [pl.BlockSpec((1, tm, d), functools.partial(lambda i, j, c: (i, j, c), c=C_MERGE // d + c))
              for c in range(N_BRANCH)]
    return pl.pallas_call(
        _merge_kernel,
        out_shape=jax.ShapeDtypeStruct((b, s, d), F32),
        grid=(b, s // tm),
        in_specs=[ospec, ospec, ospec] + gspecs + [
            pl.BlockSpec((N_BRANCH, W, d), lambda i, j: (0, 0, 0)),
            pl.BlockSpec((d, d), lambda i, j: (0, 0)),
            xspec,
            pl.BlockSpec((1, 6, d), lambda i, j: (i, 0, 0))],
        out_specs=xspec,
        compiler_params=_params("parallel", "parallel"),
        name="merge_project",
    )(o_nsa, o_sb, o_ml, y3, y3, y3, w_branch.astype(BF16), w_out.astype(BF16), x, mod)


def _ffn_kernel(x_ref, g_ref, mod_ref, wg_ref, wu_ref, wd_ref, o_ref, h_ref, acc_ref):
    f = pl.program_id(2)

    @pl.when(f == 0)
    def _():
        h_ref[...] = _norm_mod(x_ref[0], g_ref[...], mod_ref[0], 3, 4).astype(BF16)
        acc_ref[...] = jnp.zeros_like(acc_ref)

    h = h_ref[...]
    a = jnp.dot(h, wg_ref[...], preferred_element_type=F32)
    u = jnp.dot(h, wu_ref[...], preferred_element_type=F32)
    act = (a * jax.nn.sigmoid(a) * u).astype(BF16)
    acc_ref[...] += jnp.dot(act, wd_ref[...], preferred_element_type=F32)

    @pl.when(f == pl.num_programs(2) - 1)
    def _():
        o_ref[0] = x_ref[0] + mod_ref[0, 5:6, :] * acc_ref[...]


def dense_ffn(x, g, mod, wg, wu, wd, tm=512, n_ftiles=2):
    b, s, d = x.shape
    ff = wg.shape[1]
    tf = -(-ff // (n_ftiles * LANES)) * LANES
    pad = n_ftiles * tf - ff
    wg = jnp.pad(to_bf16(wg), ((0, 0), (0, pad)))
    wu = jnp.pad(to_bf16(wu), ((0, 0), (0, pad)))
    wd = jnp.pad(to_bf16(wd), ((0, pad), (0, 0)))
    xspec = pl.BlockSpec((1, tm, d), lambda i, j, f: (i, j, 0))
    return pl.pallas_call(
        _ffn_kernel,
        out_shape=jax.ShapeDtypeStruct((b, s, d), F32),
        grid=(b, s // tm, n_ftiles),
        in_specs=[xspec,
                  pl.BlockSpec((1, d), lambda i, j, f: (0, 0)),
                  pl.BlockSpec((1, 6, d), lambda i, j, f: (i, 0, 0)),
                  pl.BlockSpec((d, tf), lambda i, j, f: (0, f)),
                  pl.BlockSpec((d, tf), lambda i, j, f: (0, f)),
                  pl.BlockSpec((tf, d), lambda i, j, f: (f, 0))],
        out_specs=xspec,
        scratch_shapes=[pltpu.VMEM((tm, d), BF16), pltpu.VMEM((tm, d), F32)],
        compiler_params=_params("parallel", "parallel", "arbitrary"),
        name="dense_ffn",
    )(x, g.reshape(1, d), mod, wg, wu, wd)


def _router_kernel(x_ref, g_ref, mod_ref, wr_ref, h_ref, e_ref, p_ref):
    h = _norm_mod(x_ref[0], g_ref[...], mod_ref[0], 3, 4)
    h_ref[...] = h
    lane = _iota((1, LANES), 1)
    real = lane < N_EXPERTS
    logits = jnp.where(real, jnp.dot(h, wr_ref[...], precision=HIGHEST, preferred_element_type=F32), NEG)
    e = jnp.exp(logits - jnp.max(logits, axis=1, keepdims=True))
    p = jnp.where(real, e / jnp.sum(e, axis=1, keepdims=True), -1.0)
    p1 = jnp.max(p, axis=1, keepdims=True)
    i1 = jnp.min(jnp.where(p == p1, lane, LANES), axis=1, keepdims=True)
    rest = jnp.where(lane == i1, -1.0, p)
    p2 = jnp.max(rest, axis=1, keepdims=True)
    i2 = jnp.min(jnp.where(rest == p2, lane, LANES), axis=1, keepdims=True)
    tot = p1 + p2
    e_ref[...] = jnp.where(lane == 0, i1, jnp.where(lane == 1, i2, 0))[:, 0:N_EXPERTS]
    p_ref[...] = jnp.where(lane == 0, p1 / tot, jnp.where(lane == 1, p2 / tot, 0.0))[:, 0:N_EXPERTS]


def moe_router(x, g, mod, w_router, tm=512):
    b, s, d = x.shape
    t = b * s
    spb = s // tm
    wr = jnp.pad(w_router, ((0, 0), (0, LANES - N_EXPERTS)))
    return pl.pallas_call(
        _router_kernel,
        out_shape=(jax.ShapeDtypeStruct((t, d), F32),
                   jax.ShapeDtypeStruct((t, N_EXPERTS), I32),
                   jax.ShapeDtypeStruct((t, N_EXPERTS), F32)),
        grid=(b, spb),
        in_specs=[pl.BlockSpec((1, tm, d), lambda i, j: (i, j, 0)),
                  pl.BlockSpec((1, d), lambda i, j: (0, 0)),
                  pl.BlockSpec((1, 6, d), lambda i, j: (i, 0, 0)),
                  pl.BlockSpec((d, LANES), lambda i, j: (0, 0))],
        out_specs=(pl.BlockSpec((tm, d), lambda i, j: (i * spb + j, 0)),
                   pl.BlockSpec((tm, N_EXPERTS), lambda i, j: (i * spb + j, 0)),
                   pl.BlockSpec((tm, N_EXPERTS), lambda i, j: (i * spb + j, 0))),
        compiler_params=_params("parallel", "parallel"),
        name="moe_router",
    )(x, g.reshape(1, d), mod, wr)


def _dispatch_kernel(dest_ref, h_ref, zero_hbm, xpad_hbm, sem, *, td):
    del zero_hbm
    base = pl.program_id(0) * td * TOP_K

    def row_copy(a):
        return pltpu.make_async_copy(h_ref.at[a // TOP_K], xpad_hbm.at[dest_ref[base + a]], sem)

    def issue(a, c):
        row_copy(a).start()
        return c

    def drain(a, c):
        row_copy(a).wait()
        return c

    lax.fori_loop(0, td * TOP_K, issue, 0)
    lax.fori_loop(0, td * TOP_K, drain, 0)


def moe_dispatch(h, dest, n_rows, td=256):
    t, d = h.shape
    return pl.pallas_call(
        functools.partial(_dispatch_kernel, td=td),
        out_shape=jax.ShapeDtypeStruct((n_rows, d), h.dtype),
        grid_spec=pltpu.PrefetchScalarGridSpec(
            num_scalar_prefetch=1,
            grid=(t // td,),
            in_specs=[pl.BlockSpec((td, d), lambda i, dr: (i, 0)), pl.BlockSpec(memory_space=pl.ANY)],
            out_specs=pl.BlockSpec(memory_space=pl.ANY),
            scratch_shapes=[pltpu.SemaphoreType.DMA(())]),
        input_output_aliases={2: 0},
        compiler_params=pltpu.CompilerParams(dimension_semantics=("arbitrary",), has_side_effects=True),
        name="moe_dispatch",
    )(dest, h, jnp.zeros((n_rows, d), h.dtype))


def _expert_kernel(be_ref, nu_ref, x_ref, wg_ref, wu_ref, wd_ref, o_ref, xb_ref, acc_ref):
    i = pl.program_id(0)
    f = pl.program_id(1)
    used = i < nu_ref[0]

    @pl.when(f == 0)
    def _():
        xb_ref[...] = x_ref[...].astype(BF16)
        acc_ref[...] = jnp.zeros_like(acc_ref)

    @pl.when(used)
    def _():
        xb = xb_ref[...]
        a = jnp.dot(xb, wg_ref[0], preferred_element_type=F32)
        u = jnp.dot(xb, wu_ref[0], preferred_element_type=F32)
        act = (a * jax.nn.sigmoid(a) * u).astype(BF16)
        acc_ref[...] += jnp.dot(act, wd_ref[0], preferred_element_type=F32)

    @pl.when(f == pl.num_programs(1) - 1)
    def _():
        o_ref[...] = acc_ref[...]


def moe_experts(x_pad, blk_expert, n_used, wg, wu, wd, tb, tf=896):
    p, d = x_pad.shape
    ff = wg.shape[2]
    return pl.pallas_call(
        _expert_kernel,
        out_shape=jax.ShapeDtypeStruct((p, d), F32),
        grid_spec=pltpu.PrefetchScalarGridSpec(
            num_scalar_prefetch=2,
            grid=(p // tb, ff // tf),
            in_specs=[pl.BlockSpec((tb, d), lambda i, f, be, nu: (i, 0)),
                      pl.BlockSpec((1, d, tf), lambda i, f, be, nu: (be[i], 0, f)),
                      pl.BlockSpec((1, d, tf), lambda i, f, be, nu: (be[i], 0, f)),
                      pl.BlockSpec((1, tf, d), lambda i, f, be, nu: (be[i], f, 0))],
            out_specs=pl.BlockSpec((tb, d), lambda i, f, be, nu: (i, 0)),
            scratch_shapes=[pltpu.VMEM((tb, d), BF16), pltpu.VMEM((tb, d), F32)]),
        compiler_params=_params("parallel", "arbitrary"),
        name="moe_experts",
    )(blk_expert, n_used, x_pad, wg, wu, wd)


def _combine_kernel(dest_ref, y_hbm, x_ref, p_ref, mod_ref, o_ref, buf0, buf1, sem, *, td, spb):
    tok0 = (pl.program_id(0) * spb + pl.program_id(1)) * td

    def row_copies(r):
        a = (tok0 + r) * TOP_K
        return (pltpu.make_async_copy(y_hbm.at[dest_ref[a]], buf0.at[r], sem),
                pltpu.make_async_copy(y_hbm.at[dest_ref[a + 1]], buf1.at[r], sem))

    def issue(r, c):
        c0, c1 = row_copies(r)
        c0.start()
        c1.start()
        return c

    def drain(r, c):
        c0, c1 = row_copies(r)
        c0.wait()
        c1.wait()
        return c

    lax.fori_loop(0, td, issue, 0)
    lax.fori_loop(0, td, drain, 0)
    w = p_ref[...]
    f = w[:, 0:1] * buf0[...] + w[:, 1:2] * buf1[...]
    o_ref[0] = x_ref[0] + mod_ref[0, 5:6, :] * f


def moe_combine(y, dest, x, top_p, mod, td=256):
    b, s, d = x.shape
    spb = s // td
    return pl.pallas_call(
        functools.partial(_combine_kernel, td=td, spb=spb),
        out_shape=jax.ShapeDtypeStruct((b, s, d), F32),
        grid_spec=pltpu.PrefetchScalarGridSpec(
            num_scalar_prefetch=1,
            grid=(b, spb),
            in_specs=[pl.BlockSpec(memory_space=pl.ANY),
                      pl.BlockSpec((1, td, d), lambda i, j, dr: (i, j, 0)),
                      pl.BlockSpec((td, N_EXPERTS), lambda i, j, dr: (i * spb + j, 0)),
                      pl.BlockSpec((1, 6, d), lambda i, j, dr: (i, 0, 0))],
            out_specs=pl.BlockSpec((1, td, d), lambda i, j, dr: (i, j, 0)),
            scratch_shapes=[pltpu.VMEM((td, d), F32), pltpu.VMEM((td, d), F32),
                            pltpu.SemaphoreType.DMA(())]),
        compiler_params=_params("arbitrary", "arbitrary"),
        name="moe_combine",
    )(dest, y, x, top_p, mod)


def moe_ffn(x, g, mod, w_router, wg, wu, wd, tb=512):
    b, s, d = x.shape
    t = b * s
    a = t * TOP_K
    h, top_e, top_p = moe_router(x, g, mod, w_router)
    e_flat = top_e[:, 0:TOP_K].reshape(a)
    onehot = (e_flat[:, None] == jnp.arange(N_EXPERTS, dtype=I32)[None, :]).astype(I32)
    csum = jnp.cumsum(onehot, axis=0)
    rank = jnp.sum(onehot * csum, axis=1) - 1
    counts = csum[-1]
    padded = (counts + tb - 1) // tb * tb
    pad_ends = jnp.cumsum(padded)
    pad_starts = pad_ends - padded
    dest = (jnp.sum(onehot * pad_starts[None, :], axis=1) + rank).astype(I32)
    n_rows = (a // tb + N_EXPERTS + 1) * tb
    n_blk = n_rows // tb
    blk_expert = jnp.minimum(
        jnp.searchsorted(pad_ends, jnp.arange(n_blk, dtype=I32) * tb, side="right"), N_EXPERTS - 1).astype(I32)
    n_used = (pad_ends[-1:] // tb).astype(I32)
    slot_assign = moe_invert(dest, n_rows, tb)
    y2 = moe_experts_fused(h, slot_assign, blk_expert, n_used, to_bf16(wg), to_bf16(wu), to_bf16(wd), tb)
    return moe_mix(y2, x, top_p, mod)


def _invert_kernel(dest_ref, sa_ref, *, n_assign, tb, n_chunks):
    phase = pl.program_id(0)
    chunk = pl.program_id(1)

    @pl.when(phase == 0)
    def _():
        per = sa_ref.shape[0] // n_chunks

        def fill(j, c):
            p = chunk * per + j
            sa_ref[p] = n_assign + (p & (2 * tb - 1))
            return c

        lax.fori_loop(0, per, fill, 0, unroll=8)

    @pl.when(phase == 1)
    def _():
        per = n_assign // n_chunks

        def put(j, c):
            a = chunk * per + j
            sa_ref[dest_ref[a]] = a
            return c

        lax.fori_loop(0, per, put, 0, unroll=8)


def moe_invert(dest, n_rows, tb, n_chunks=16):
    n_assign = dest.shape[0]
    assert tb & (tb - 1) == 0 and n_rows % n_chunks == 0 and n_assign % n_chunks == 0
    return pl.pallas_call(
        functools.partial(_invert_kernel, n_assign=n_assign, tb=tb, n_chunks=n_chunks),
        out_shape=jax.ShapeDtypeStruct((n_rows,), I32),
        grid=(2, n_chunks),
        in_specs=[pl.BlockSpec(memory_space=pltpu.SMEM)],
        out_specs=pl.BlockSpec(memory_space=pltpu.SMEM),
        compiler_params=pltpu.CompilerParams(dimension_semantics=("arbitrary", "arbitrary")),
        name="moe_invert",
    )(dest)


def _expert_fused_kernel(be_ref, nu_ref, sa_ref, h_hbm, wg_ref, wu_ref, wd_ref, y_hbm,
                         xin_ref, yout_ref, xb_ref, acc_ref, sem_in, sem_out, *, tb, n_tok, n_f):
    i = pl.program_id(0)
    f = pl.program_id(1)
    n_used = nu_ref[0]
    n_assign = n_tok * TOP_K
    rows_f = tb // n_f
    active = i <= n_used

    def gather_row(blk, r):
        a = sa_ref[blk * tb + r]
        tok = jnp.where(a < n_assign, a >> 1, 0)
        return pltpu.make_async_copy(h_hbm.at[tok], xin_ref.at[blk % 2, r], sem_in.at[blk % 2])

    def scatter_row(blk, r):
        a = jnp.where(blk >= 0, sa_ref[jnp.maximum(blk, 0) * tb + r], n_assign + tb + r)
        row = jnp.where(a < n_assign, (a & 1) * n_tok + (a >> 1), a)
        return pltpu.make_async_copy(yout_ref.at[(blk + 2) % 2, r], y_hbm.at[row], sem_out.at[(blk + 2) % 2])

    def for_rows(fn):
        def body(r, c):
            fn(r)
            return c
        lax.fori_loop(0, tb, body, 0, unroll=8)

    @pl.when(f == 0)
    def _():
        @pl.when(i == 0)
        def _():
            yout_ref[1] = jnp.zeros((tb, yout_ref.shape[2]), F32)
            for_rows(lambda r: gather_row(0, r).start())

            def clear_row(r):
                return pltpu.make_async_copy(yout_ref.at[1, r], y_hbm.at[n_assign + r], sem_out.at[0])

            for_rows(lambda r: clear_row(r).start())
            for_rows(lambda r: clear_row(r).wait())

        @pl.when((i == 0) | (i - 1 <= n_used))
        def _():
            for_rows(lambda r: gather_row(i, r).wait())

        @pl.when(active)
        def _():
            xb_ref[...] = xin_ref[i % 2].astype(BF16)
            acc_ref[...] = jnp.zeros_like(acc_ref)

    @pl.when(active)
    def _():
        for r in range(rows_f):
            gather_row(i + 1, f * rows_f + r).start()
            scatter_row(i - 1, f * rows_f + r).start()
        xb = xb_ref[...]
        a = jnp.dot(xb, wg_ref[0], preferred_element_type=F32)
        u = jnp.dot(xb, wu_ref[0], preferred_element_type=F32)
        act = (a * jax.nn.sigmoid(a) * u).astype(BF16)
        acc_ref[...] += jnp.dot(act, wd_ref[0], preferred_element_type=F32)

    @pl.when(f == n_f - 1)
    def _():
        @pl.when((i >= 1) & (i - 1 <= n_used))
        def _():
            for_rows(lambda r: scatter_row(i - 2, r).wait())

        @pl.when(active)
        def _():
            yout_ref[i % 2] = acc_ref[...]


def moe_experts_fused(h, slot_assign, blk_expert, n_used, wg, wu, wd, tb, tf=896):
    n_tok, d = h.shape
    p = slot_assign.shape[0]
    ff = wg.shape[2]
    return pl.pallas_call(
        functools.partial(_expert_fused_kernel, tb=tb, n_tok=n_tok, n_f=ff // tf),
        out_shape=jax.ShapeDtypeStruct((n_tok * TOP_K + 2 * tb, d), F32),
        grid_spec=pltpu.PrefetchScalarGridSpec(
            num_scalar_prefetch=3,
            grid=(p // tb, ff // tf),
            in_specs=[pl.BlockSpec(memory_space=pl.ANY),
                      pl.BlockSpec((1, d, tf), lambda i, f, be, nu, sa: (be[i], 0, f)),
                      pl.BlockSpec((1, d, tf), lambda i, f, be, nu, sa: (be[i], 0, f)),
                      pl.BlockSpec((1, tf, d), lambda i, f, be, nu, sa: (be[i], f, 0))],
            out_specs=pl.BlockSpec(memory_space=pl.ANY),
            scratch_shapes=[pltpu.VMEM((2, tb, d), F32), pltpu.VMEM((2, tb, d), F32),
                            pltpu.VMEM((tb, d), BF16), pltpu.VMEM((tb, d), F32),
                            pltpu.SemaphoreType.DMA((2,)), pltpu.SemaphoreType.DMA((2,))]),
        compiler_params=pltpu.CompilerParams(dimension_semantics=("arbitrary", "arbitrary"),
                                             vmem_limit_bytes=VMEM_LIMIT, has_side_effects=True),
        name="moe_experts",
    )(blk_expert, n_used, slot_assign, h, wg, wu, wd)


def _mix_kernel(y0_ref, y1_ref, x_ref, p_ref, mod_ref, o_ref):
    w = p_ref[...]
    f = w[:, 0:1] * y0_ref[...] + w[:, 1:2] * y1_ref[...]
    o_ref[0] = x_ref[0] + mod_ref[0, 5:6, :] * f


def moe_mix(y2, x, top_p, mod, td=512):
    b, s, d = x.shape
    spb = s // td
    nt = b * spb
    return pl.pallas_call(
        _mix_kernel,
        out_shape=jax.ShapeDtypeStruct((b, s, d), F32),
        grid=(b, spb),
        in_specs=[pl.BlockSpec((td, d), lambda i, j: (i * spb + j, 0)),
                  pl.BlockSpec((td, d), lambda i, j: (nt + i * spb + j, 0)),
                  pl.BlockSpec((1, td, d), lambda i, j: (i, j, 0)),
                  pl.BlockSpec((td, N_EXPERTS), lambda i, j: (i * spb + j, 0)),
                  pl.BlockSpec((1, 6, d), lambda i, j: (i, 0, 0))],
        out_specs=pl.BlockSpec((1, td, d), lambda i, j: (i, j, 0)),
        compiler_params=_params("parallel", "parallel"),
        name="moe_mix",
    )(y2, y2, x, top_p, mod)


def _pack_w_in(w_in):
    kv = 2 * NSA_KV_HEADS * HEAD_DIM * 3
    w_in = to_bf16(w_in)
    o = 0
    nsa_q = w_in[:, o:o + MIX_WIDTH]; o += MIX_WIDTH
    nsa_kv = w_in[:, o:o + kv]; o += kv
    nsa_gate = w_in[:, o:o + NSA_HEADS * N_BRANCH]; o += NSA_HEADS * N_BRANCH
    sb = w_in[:, o:o + 3 * MIX_WIDTH]; o += 3 * MIX_WIDTH
    ml_qkv = w_in[:, o:o + 3 * MIX_WIDTH]; o += 3 * MIX_WIDTH
    ml_if = w_in[:, o:o + 2 * ML_HEADS]; o += 2 * ML_HEADS
    ml_o = w_in[:, o:o + MIX_WIDTH]; o += MIX_WIDTH
    merge = w_in[:, o:]
    main = jnp.concatenate([merge, nsa_q, ml_qkv, ml_o, sb, nsa_kv], axis=1)
    small = jnp.concatenate([nsa_gate, ml_if], axis=1)
    small = jnp.pad(small, ((0, 0), (0, N_SMALL - small.shape[1])))
    return main, small


def token_mixer_layer(x, mod, norm_g, w_in, nsa_q_norm, nsa_k_norm, cmp_pos, cmp_w1, cmp_b1, cmp_w2,
                      cmp_b2, ml_conv_w, ml_conv_b, ml_gate_b, w_branch, w_out):
    w_main, w_small = _pack_w_in(w_in)
    y3, small3 = in_projection(x, norm_g, mod, w_main, w_small)
    o_sb = sb_attention(y3)
    o_ml = mlstm(y3, small3, ml_conv_w, ml_conv_b, ml_gate_b)
    kc, vc = nsa_compress(y3, cmp_pos, cmp_w1, cmp_b1, cmp_w2, cmp_b2, nsa_k_norm[0])
    o_nsa = nsa_attention(y3, small3, nsa_q_norm, nsa_k_norm, kc, vc)
    return merge_project(o_nsa, o_sb, o_ml, y3, w_branch, w_out, x, mod)


def kernel(x, c, ada_w, ada_b, norm_mix, norm_ffn, w_in, nsa_q_norm, nsa_k_norm, cmp_pos, cmp_w1, cmp_b1,
           cmp_w2, cmp_b2, ml_conv_w, ml_conv_b, ml_gate_b, w_branch, w_out, ffn_wg, ffn_wu, ffn_wd,
           moe_router, moe_wg, moe_wu, moe_wd):
    depth = ada_w.shape[0]
    b, s, d = x.shape
    mods = adaln(c, ada_w, ada_b).reshape(depth, b, 6, d)
    for layer in range(depth):
        mod = mods[layer]
        x = token_mixer_layer(x, mod, norm_mix[layer], w_in[layer], nsa_q_norm[layer], nsa_k_norm[layer],
                              cmp_pos[layer], cmp_w1[layer], cmp_b1[layer], cmp_w2[layer], cmp_b2[layer],
                              ml_conv_w[layer], ml_conv_b[layer], ml_gate_b[layer], w_branch[layer],
                              w_out[layer])
        j = layer // 2
        if layer % 2 == 0:
            x = dense_ffn(x, norm_ffn[layer], mod, ffn_wg[j], ffn_wu[j], ffn_wd[j])
        else:
            x = moe_ffn(x, norm_ffn[layer], mod, moe_router[j], moe_wg[j], moe_wu[j], moe_wd[j])
    return x
```

```python
import functools

import numpy as np
import jax
import jax.numpy as jnp
from jax import lax
from jax.experimental import pallas as pl
from jax.experimental.pallas import tpu as pltpu

F32 = jnp.float32
BF16 = jnp.bfloat16
I32 = jnp.int32
HIGHEST = lax.Precision.HIGHEST

EPS = 1e-6
NEG = -1e30
HEAD_DIM = 64
MIX_WIDTH = 512
NSA_HEADS = 8
NSA_KV_HEADS = 2
NSA_GROUP = NSA_HEADS // NSA_KV_HEADS
CMP_BLOCK = 32
CMP_STRIDE = 16
SEL_BLOCK = 64
SEL_TOPK = 16
WINDOW = 512
FORCE_BONUS = 1e4
ML_HEADS = 4
ML_HEAD_DIM = 128
ML_CHUNK = 64
CONV_WIDTH = 4
N_BRANCH = 3
N_EXPERTS = 8
TOP_K = 2
LANES = 128

C_MERGE = 0
C_NSA_Q = 3072
C_ML_Q = 3584
C_ML_K = 4096
C_ML_V = 4608
C_ML_O = 5120
C_SB_Q = 5632
C_SB_K = 6144
C_SB_V = 6656
C_NSA_KV = 7168
N_MAIN = 7936
S_NSA_GATE = 0
S_ML_I = 24
S_ML_F = 28
N_SMALL = 128

VMEM_LIMIT = 56 * 1024 * 1024


def _params(*sem):
    return pltpu.CompilerParams(dimension_semantics=sem, vmem_limit_bytes=VMEM_LIMIT)


def _iota(shape, dim):
    return lax.broadcasted_iota(I32, shape, dim)


def _split_dot(a32, b_bf16):
    hi = a32.astype(BF16)
    lo = (a32 - hi.astype(F32)).astype(BF16)
    return (jnp.dot(hi, b_bf16, preferred_element_type=F32)
            + jnp.dot(lo, b_bf16, preferred_element_type=F32))


def _split_dot_left(a_bf16, b32):
    hi = b32.astype(BF16)
    lo = (b32 - hi.astype(F32)).astype(BF16)
    return (jnp.dot(a_bf16, hi, preferred_element_type=F32)
            + jnp.dot(a_bf16, lo, preferred_element_type=F32))


def _dot_nt(a, b):
    return lax.dot_general(a, b, (((1,), (1,)), ((), ())), preferred_element_type=F32)


def _log_sigmoid(z):
    return jnp.minimum(z, 0.0) - jnp.log1p(jnp.exp(-jnp.abs(z)))


def _cast_kernel(x_ref, o_ref):
    o_ref[...] = x_ref[...].astype(o_ref.dtype)


def to_bf16(w, max_rows=512):
    cols = w.shape[-1]
    w2 = w.reshape(-1, cols)
    rows = w2.shape[0]
    tr = max(t for t in range(8, max_rows + 1, 8) if rows % t == 0)
    out = pl.pallas_call(
        _cast_kernel,
        out_shape=jax.ShapeDtypeStruct((rows, cols), BF16),
        grid=(rows // tr,),
        in_specs=[pl.BlockSpec((tr, cols), lambda i: (i, 0))],
        out_specs=pl.BlockSpec((tr, cols), lambda i: (i, 0)),
        compiler_params=_params("parallel"),
        name="to_bf16",
    )(w2)
    return out.reshape(w.shape)


def _adaln_kernel(c_ref, w_ref, b_ref, o_ref):
    c = c_ref[...]
    cond = c * jax.nn.sigmoid(c)
    o_ref[0] = jnp.dot(cond, w_ref[0], precision=HIGHEST, preferred_element_type=F32) + b_ref[0]


def adaln(c, ada_w, ada_b):
    depth, d, n = ada_w.shape
    b = c.shape[0]
    tn = 1536
    return pl.pallas_call(
        _adaln_kernel,
        out_shape=jax.ShapeDtypeStruct((depth, b, n), F32),
        grid=(depth, n // tn),
        in_specs=[pl.BlockSpec((b, d), lambda l, j: (0, 0)),
                  pl.BlockSpec((1, d, tn), lambda l, j: (l, 0, j)),
                  pl.BlockSpec((1, 1, tn), lambda l, j: (l, 0, j))],
        out_specs=pl.BlockSpec((1, b, tn), lambda l, j: (l, 0, j)),
        compiler_params=_params("parallel", "parallel"),
        name="adaln",
    )(c, ada_w, ada_b.reshape(depth, 1, n))


def _norm_mod(x, g, mod, shift_row, scale_row):
    ms = jnp.mean(x * x, axis=-1, keepdims=True)
    y = x * lax.rsqrt(ms + EPS) * g
    return y * (1.0 + mod[scale_row:scale_row + 1, :]) + mod[shift_row:shift_row + 1, :]


def _normmod_kernel(x_ref, g_ref, mod_ref, o_ref, *, shift_row, scale_row):
    o_ref[0] = _norm_mod(x_ref[0], g_ref[...], mod_ref[0], shift_row, scale_row).astype(o_ref.dtype)


def normmod(x, g, mod, shift_row, scale_row, ts=512):
    b, s, d = x.shape
    return pl.pallas_call(
        functools.partial(_normmod_kernel, shift_row=shift_row, scale_row=scale_row),
        out_shape=jax.ShapeDtypeStruct((b, s, d), BF16),
        grid=(b, s // ts),
        in_specs=[pl.BlockSpec((1, ts, d), lambda i, j: (i, j, 0)),
                  pl.BlockSpec((1, d), lambda i, j: (0, 0)),
                  pl.BlockSpec((1, 6, d), lambda i, j: (i, 0, 0))],
        out_specs=pl.BlockSpec((1, ts, d), lambda i, j: (i, j, 0)),
        compiler_params=_params("parallel", "parallel"),
        name="normmod",
    )(x, g.reshape(1, d), mod)


def _in_proj_kernel(x_ref, g_ref, mod_ref, wm_ref, ws_ref, y_ref, sm_ref, h_ref):
    @pl.when(pl.program_id(2) == 0)
    def _():
        h_ref[...] = _norm_mod(x_ref[0], g_ref[...], mod_ref[0], 0, 1).astype(BF16)
        sm_ref[0] = jnp.dot(h_ref[...], ws_ref[...], preferred_element_type=F32)

    y_ref[0] = jnp.dot(h_ref[...], wm_ref[...], preferred_element_type=F32).astype(y_ref.dtype)


def in_projection(x, g, mod, w_main, w_small, tm=512, n_tiles=2):
    b, s, d = x.shape
    tn = N_MAIN // n_tiles
    return pl.pallas_call(
        _in_proj_kernel,
        out_shape=(jax.ShapeDtypeStruct((b, s, N_MAIN), BF16), jax.ShapeDtypeStruct((b, s, N_SMALL), F32)),
        grid=(b, s // tm, n_tiles),
        in_specs=[pl.BlockSpec((1, tm, d), lambda i, j, n: (i, j, 0)),
                  pl.BlockSpec((1, d), lambda i, j, n: (0, 0)),
                  pl.BlockSpec((1, 6, d), lambda i, j, n: (i, 0, 0)),
                  pl.BlockSpec((d, tn), lambda i, j, n: (0, n)),
                  pl.BlockSpec((d, N_SMALL), lambda i, j, n: (0, 0))],
        out_specs=(pl.BlockSpec((1, tm, tn), lambda i, j, n: (i, j, n)),
                   pl.BlockSpec((1, tm, N_SMALL), lambda i, j, n: (i, j, 0))),
        scratch_shapes=[pltpu.VMEM((tm, d), BF16)],
        compiler_params=_params("parallel", "parallel", "arbitrary"),
        name="in_projection",
    )(x, g.reshape(1, d), mod, w_main, w_small)


def _mm_kernel(a_ref, w_ref, o_ref):
    o_ref[...] = jnp.dot(a_ref[...], w_ref[...], preferred_element_type=F32).astype(o_ref.dtype)


def matmul(a, w, out_dtype, tm, tn):
    m, k = a.shape
    n = w.shape[1]
    return pl.pallas_call(
        _mm_kernel,
        out_shape=jax.ShapeDtypeStruct((m, n), out_dtype),
        grid=(m // tm, n // tn),
        in_specs=[pl.BlockSpec((tm, k), lambda i, j: (i, 0)),
                  pl.BlockSpec((k, tn), lambda i, j: (0, j))],
        out_specs=pl.BlockSpec((tm, tn), lambda i, j: (i, j)),
        compiler_params=_params("parallel", "parallel"),
        name="matmul",
    )(a, w)


SB_EXP_FLOOR = -104.0
SB_EAGER = 2


def _sb_kernel(q_ref, k_ref, v_ref, o_ref, vt_ref, *, tq):
    TK, dh = LANES, HEAD_DIM
    n_sub = tq // TK
    W = 2 * tq
    qi = pl.program_id(2)
    q0 = qi * tq

    @pl.when(qi == 0)
    def _():
        for c in range(v_ref.shape[1] // tq):
            vt_ref[c] = v_ref[0, c * tq:(c + 1) * tq, :].astype(F32).T.astype(BF16)

    q_t = (q_ref[0].astype(F32) * (dh ** -0.5)).T
    chan = _iota((2 * dh, tq), 0)
    q_cat = jnp.concatenate([jnp.where(chan < dh, q_t, 0.0), jnp.where(chan < dh, 0.0, q_t)],
                            axis=1).astype(BF16)
    later = (_iota((TK, TK), 0) < _iota((TK, TK), 1)).astype(BF16)
    suffix = jnp.concatenate([jnp.concatenate([later, later], axis=1), jnp.ones((8, 2 * TK), BF16)], axis=0)

    def steps(blocks, st):
        carry, acc = st
        lks, lss, his, los = [], [], [], []
        for j, keep in blocks:
            k0 = pl.multiple_of(j * tq, tq)
            z = jnp.dot(k_ref[0, pl.ds(k0, tq), :], q_cat, preferred_element_type=F32)
            lk = -(jnp.maximum(z, 0.0) + jnp.log(1.0 + jnp.exp(-jnp.abs(z))))
            lss.append(lk + z)
            if keep is not None:
                lk = jnp.where(keep, lk, 0.0)
            hi = lk.astype(BF16)
            his.append(hi)
            los.append((lk - hi.astype(F32)).astype(BF16))
        afters = []
        for b in range(len(blocks)):
            after = [None] * n_sub
            for sub in range(n_sub - 1, -1, -1):
                rows = slice(sub * TK, (sub + 1) * TK)
                res = jnp.dot(suffix, jnp.concatenate([his[b][rows], los[b][rows]], axis=0),
                              preferred_element_type=F32)
                after[sub] = res[0:TK] + carry
                carry = carry + res[TK:TK + 1]
            afters.append(jnp.concatenate(after, axis=0))
        for b, (j, keep) in enumerate(blocks):
            a = jnp.exp(lss[b] + afters[b])
            if keep is not None:
                a = jnp.where(keep, a, 0.0)
            acc = acc + jnp.dot(vt_ref[j], a.astype(BF16), preferred_element_type=F32)
        return carry, acc

    def cond(st):
        return (st[0] >= 0) & (jnp.max(st[1]) > SB_EXP_FLOOR)

    def body(st):
        return (st[0] - 1,) + steps([(st[0], None)], st[1:])

    strict = _iota((tq, W), 0) < (_iota((tq, W), 1) & (tq - 1))
    eager = [(jnp.maximum(qi - d, 0), qi >= d) for d in range(1, SB_EAGER + 1)]
    st = steps([(qi, strict)] + eager, (jnp.zeros((1, W), F32), jnp.zeros((2 * dh, W), F32)))
    _, _, acc = lax.while_loop(cond, body, (qi - 1 - SB_EAGER,) + st)
    o_ref[0] = jnp.concatenate([acc[0:dh, 0:tq], acc[dh:2 * dh, tq:W]], axis=0).T.astype(o_ref.dtype)


def _sb_multi_kernel(q_ref, k_ref, v_ref, o_ref, vt_ref, *, tq, n_str):
    TK, dh = LANES, HEAD_DIM
    n_sub = tq // TK
    W = 2 * tq
    SR = range(n_str)
    qi = pl.program_id(2)

    @pl.when(qi == 0)
    def _():
        for c in range(v_ref.shape[1] // tq):
            v_t = v_ref[0, c * tq:(c + 1) * tq, :].astype(F32).T.astype(BF16)
            for p in SR:
                vt_ref[p, c] = v_t[p * LANES:(p + 1) * LANES]

    q_t = (q_ref[0].astype(F32) * (dh ** -0.5)).T
    chan = _iota((2 * dh, tq), 0)
    q_cat = []
    for p in SR:
        q_p = q_t[p * LANES:(p + 1) * LANES]
        q_cat.append(jnp.concatenate([jnp.where(chan < dh, q_p, 0.0), jnp.where(chan < dh, 0.0, q_p)],
                                     axis=1).astype(BF16))
    later = (_iota((TK, TK), 0) < _iota((TK, TK), 1)).astype(BF16)
    suffix = jnp.concatenate([jnp.concatenate([later, later], axis=1), jnp.ones((8, 2 * TK), BF16)], axis=0)

    def steps(blocks, st):
        work = [(p, b) for p in SR for b in range(len(blocks))]
        lss, his, los = {}, {}, {}
        for p, b in work:
            j, keep = blocks[b]
            k0 = pl.multiple_of(j * tq, tq)
            z = jnp.dot(k_ref[0, pl.ds(k0, tq), p * LANES:(p + 1) * LANES], q_cat[p],
                        preferred_element_type=F32)
            lk = -(jnp.maximum(z, 0.0) + jnp.log(1.0 + jnp.exp(-jnp.abs(z))))
            lss[p, b] = lk + z
            if keep is not None:
                lk = jnp.where(keep, lk, 0.0)
            his[p, b] = lk.astype(BF16)
            los[p, b] = (lk - his[p, b].astype(F32)).astype(BF16)
        carry = [st[p][0] for p in SR]
        afters = {}
        for p, b in work:
            after = [None] * n_sub
            for sub in range(n_sub - 1, -1, -1):
                rows = slice(sub * TK, (sub + 1) * TK)
                res = jnp.dot(suffix, jnp.concatenate([his[p, b][rows], los[p, b][rows]], axis=0),
                              preferred_element_type=F32)
                after[sub] = res[0:TK] + carry[p]
                carry[p] = carry[p] + res[TK:TK + 1]
            afters[p, b] = jnp.concatenate(after, axis=0)
        acc = [st[p][1] for p in SR]
        for p, b in work:
            j, keep = blocks[b]
            a = jnp.exp(lss[p, b] + afters[p, b])
            if keep is not None:
                a = jnp.where(keep, a, 0.0)
            acc[p] = acc[p] + jnp.dot(vt_ref[p, j], a.astype(BF16), preferred_element_type=F32)
        return [(carry[p], acc[p]) for p in SR]

    def cond(c):
        live = jnp.max(c[1][0][0])
        for p in range(1, n_str):
            live = jnp.maximum(live, jnp.max(c[1][p][0]))
        return (c[0] >= 0) & (live > SB_EXP_FLOOR)

    def body(c):
        return c[0] - 1, steps([(c[0], None)], c[1])

    strict = _iota((tq, W), 0) < (_iota((tq, W), 1) & (tq - 1))
    eager = [(jnp.maximum(qi - d, 0), qi >= d) for d in range(1, SB_EAGER + 1)]
    st = steps([(qi, strict)] + eager, [(jnp.zeros((1, W), F32), jnp.zeros((2 * dh, W), F32)) for p in SR])
    _, st = lax.while_loop(cond, body, (qi - 1 - SB_EAGER, st))
    out_rows = []
    for p in SR:
        out_rows += [st[p][1][0:dh, 0:tq], st[p][1][dh:2 * dh, tq:W]]
    o_ref[0] = jnp.concatenate(out_rows, axis=0).T.astype(o_ref.dtype)


def sb_attention(y3, tq=256, n_str=2):
    b, s, _ = y3.shape
    wide = n_str * LANES
    qb, kb, vb = C_SB_Q // wide, C_SB_K // wide, C_SB_V // wide
    return pl.pallas_call(
        functools.partial(_sb_multi_kernel, tq=tq, n_str=n_str),
        out_shape=jax.ShapeDtypeStruct((b, s, MIX_WIDTH), BF16),
        grid=(b, MIX_WIDTH // wide, s // tq),
        in_specs=[pl.BlockSpec((1, tq, wide), lambda i, p, j: (i, j, qb + p)),
                  pl.BlockSpec((1, s, wide), lambda i, p, j: (i, 0, kb + p)),
                  pl.BlockSpec((1, s, wide), lambda i, p, j: (i, 0, vb + p))],
        out_specs=pl.BlockSpec((1, tq, wide), lambda i, p, j: (i, j, p)),
        scratch_shapes=[pltpu.VMEM((n_str, s // tq, LANES, tq), BF16)],
        compiler_params=_params("parallel", "parallel", "arbitrary"),
        name="sb_attention",
    )(y3, y3, y3)


def _mlstm_kernel(q_ref, k_ref, v_ref, og_ref, sm_ref, gr_ref, cw_ref, cb_ref, gb_ref, out_ref,
                  ct_ref, n_ref, m_ref, xbuf_ref, qk_ref, *, ts):
    L, dh, H, W = ML_CHUNK, ML_HEAD_DIM, ML_HEADS, MIX_WIDTH
    halo = 8
    sblk = pl.program_id(1)

    @pl.when(sblk == 0)
    def _():
        ct_ref[...] = jnp.zeros_like(ct_ref)
        n_ref[...] = jnp.zeros_like(n_ref)
        m_ref[...] = jnp.zeros_like(m_ref)
        xbuf_ref[0:halo, :] = jnp.zeros((halo, 2 * W), F32)

    @pl.when(sblk > 0)
    def _():
        xbuf_ref[0:halo, :] = xbuf_ref[ts:ts + halo, :]

    xbuf_ref[halo:halo + ts, 0:W] = q_ref[0].astype(F32)
    xbuf_ref[halo:halo + ts, W:2 * W] = k_ref[0].astype(F32)
    conv = cb_ref[...] + jnp.zeros((ts, 2 * W), F32)
    for j in range(CONV_WIDTH):
        off = halo - (CONV_WIDTH - 1) + j
        conv = conv + cw_ref[j:j + 1, :] * xbuf_ref[off:off + ts, :]
    act = conv * jax.nn.sigmoid(conv)
    qk_ref[:, 0:W] = (act[:, 0:W] * (dh ** -0.5)).astype(BF16)
    qk_ref[:, W:2 * W] = act[:, W:2 * W].astype(BF16)

    it0, it1 = _iota((L, L), 0), _iota((L, L), 1)
    causal = it0 >= it1
    tri_lo = causal.astype(BF16)
    tri_up = (it0 <= it1).astype(BF16)

    def chunk(c, carry):
        r0 = pl.multiple_of(c * L, L)
        sm = sm_ref[0, pl.ds(r0, L), :]
        gr = gr_ref[0, c]
        HR = range(H)
        rows = pl.ds(r0, L)
        cols = [slice(h * dh, (h + 1) * dh) for h in HR]
        ig_col = [sm[:, S_ML_I + h:S_ML_I + h + 1] + gb_ref[0, h] for h in HR]
        lf_col = [_log_sigmoid(sm[:, S_ML_F + h:S_ML_F + h + 1] + gb_ref[1, h]) for h in HR]
        ig_row = [gr[h:h + 1, :] + gb_ref[0, h] for h in HR]
        lf_row = [_log_sigmoid(gr[H + h:H + h + 1, :] + gb_ref[1, h]) for h in HR]
        b_t = [_split_dot_left(tri_lo, jnp.broadcast_to(lf_col[h], (L, L))) for h in HR]
        b_s = [_split_dot(jnp.broadcast_to(lf_row[h], (L, L)), tri_up) for h in HR]
        qq = [qk_ref[rows, cols[h]] for h in HR]
        kk = [qk_ref[rows, W + h * dh:W + (h + 1) * dh] for h in HR]
        vv = [v_ref[0, rows, cols[h]] for h in HR]
        ct = [ct_ref[h] for h in HR]
        nvec = [n_ref[h] for h in HR]
        m_prev = [m_ref[h][:, 0:1] for h in HR]
        qk = [_dot_nt(qq[h], kk[h]) for h in HR]
        q_c = [jnp.dot(qq[h], ct[h].astype(BF16), preferred_element_type=F32) for h in HR]
        kt = [kk[h].astype(F32).T.astype(BF16) for h in HR]
        dmat = [jnp.where(causal, b_t[h] - b_s[h] + ig_row[h], NEG) for h in HR]
        b_col = [b_t[h][:, 0:1] for h in HR]
        m_inter = [b_col[h] + m_prev[h] for h in HR]
        m_t = [jnp.maximum(m_inter[h], jnp.max(dmat[h], axis=1, keepdims=True)) for h in HR]
        w = [jnp.exp(dmat[h] - m_t[h]) * qk[h] for h in HR]
        inter = [jnp.exp(m_inter[h] - m_t[h]) for h in HR]
        w_v = [jnp.dot(w[h].astype(BF16), vv[h], preferred_element_type=F32) for h in HR]
        b_last = [b_t[h][L - 1:L, 0:1] for h in HR]
        decay = [b_last[h] - b_col[h] + ig_col[h] for h in HR]
        m_new = [jnp.maximum(b_last[h] + m_prev[h], jnp.max(decay[h], axis=0, keepdims=True)) for h in HR]
        ws = [jnp.exp(decay[h] - m_new[h]) for h in HR]
        cscale = [jnp.exp(b_last[h] + m_prev[h] - m_new[h]) for h in HR]
        wv = [(ws[h] * vv[h].astype(F32)).astype(BF16) for h in HR]
        k_wv = [jnp.dot(kt[h], wv[h], preferred_element_type=F32) for h in HR]
        for h in HR:
            num = inter[h] * q_c[h] + w_v[h]
            den = (inter[h] * jnp.sum(qq[h].astype(F32) * nvec[h], axis=1, keepdims=True)
                   + jnp.sum(w[h], axis=1, keepdims=True))
            hval = num / jnp.maximum(jnp.abs(den), jnp.exp(-m_t[h]))
            ct_ref[h] = cscale[h] * ct[h] + k_wv[h]
            n_ref[h] = cscale[h] * nvec[h] + jnp.sum(ws[h] * kk[h].astype(F32), axis=0, keepdims=True)
            m_ref[h] = jnp.broadcast_to(m_new[h], (1, LANES))
            gate = jax.nn.sigmoid(og_ref[0, rows, cols[h]].astype(F32))
            out_ref[0, rows, cols[h]] = (gate * hval).astype(out_ref.dtype)
        return carry

    lax.fori_loop(0, ts // L, chunk, 0)


def mlstm(y3, small3, conv_w, conv_b, gate_b, ts=512):
    b, s, _ = y3.shape
    W, H, L = MIX_WIDTH, ML_HEADS, ML_CHUNK
    gr = small3[:, :, S_ML_I:S_ML_I + 2 * H].reshape(b, s // L, L, 2 * H).transpose(0, 1, 3, 2)
    cq, ck, cv, co = C_ML_Q // W, C_ML_K // W, C_ML_V // W, C_ML_O // W
    return pl.pallas_call(
        functools.partial(_mlstm_kernel, ts=ts),
        out_shape=jax.ShapeDtypeStruct((b, s, W), BF16),
        grid=(b, s // ts),
        in_specs=[pl.BlockSpec((1, ts, W), lambda i, j: (i, j, cq)),
                  pl.BlockSpec((1, ts, W), lambda i, j: (i, j, ck)),
                  pl.BlockSpec((1, ts, W), lambda i, j: (i, j, cv)),
                  pl.BlockSpec((1, ts, W), lambda i, j: (i, j, co)),
                  pl.BlockSpec((1, ts, N_SMALL), lambda i, j: (i, j, 0)),
                  pl.BlockSpec((1, ts // L, 2 * H, L), lambda i, j: (i, j, 0, 0)),
                  pl.BlockSpec((CONV_WIDTH, 2 * W), lambda i, j: (0, 0)),
                  pl.BlockSpec((1, 2 * W), lambda i, j: (0, 0)),
                  pl.BlockSpec(memory_space=pltpu.SMEM)],
        out_specs=pl.BlockSpec((1, ts, W), lambda i, j: (i, j, 0)),
        scratch_shapes=[pltpu.VMEM((H, ML_HEAD_DIM, ML_HEAD_DIM), F32),
                        pltpu.VMEM((H, 1, ML_HEAD_DIM), F32),
                        pltpu.VMEM((H, 1, LANES), F32),
                        pltpu.VMEM((ts + 8, 2 * W), F32),
                        pltpu.VMEM((ts, 2 * W), BF16)],
        compiler_params=_params("parallel", "arbitrary"),
        name="mlstm",
    )(y3, y3, y3, y3, small3, gr, conv_w, conv_b.reshape(1, 2 * W), gate_b)


def _headnorm_kernel(x_ref, g_ref, o_ref):
    x = x_ref[0].astype(F32)
    same_head = (_iota((LANES, LANES), 0) // HEAD_DIM == _iota((LANES, LANES), 1) // HEAD_DIM)
    ss = _split_dot(x * x, same_head.astype(BF16))
    o_ref[0] = (x * lax.rsqrt(ss * (1.0 / HEAD_DIM) + EPS) * g_ref[0]).astype(o_ref.dtype)


def nsa_headnorm(y3, q_norm, k_norm, ts=1024):
    b, s, _ = y3.shape
    qb = C_NSA_Q // LANES
    ksb = C_NSA_KV // LANES + 2
    kwb = C_NSA_KV // LANES + 4
    gains = jnp.stack([jnp.tile(q_norm, 2)] * 4 + [jnp.tile(k_norm[1], 2), jnp.tile(k_norm[2], 2)])

    def col(j):
        return jnp.where(j < 4, qb + j, jnp.where(j == 4, ksb, kwb))

    return pl.pallas_call(
        _headnorm_kernel,
        out_shape=jax.ShapeDtypeStruct((b, s, 6 * LANES), BF16),
        grid=(b, s // ts, 6),
        in_specs=[pl.BlockSpec((1, ts, LANES), lambda i, t, j: (i, t, col(j))),
                  pl.BlockSpec((1, 1, LANES), lambda i, t, j: (j, 0, 0))],
        out_specs=pl.BlockSpec((1, ts, LANES), lambda i, t, j: (i, t, j)),
        compiler_params=_params("parallel", "parallel", "parallel"),
        name="nsa_headnorm",
    )(y3, gains.reshape(6, 1, LANES))


def _gelu_tanh(x):
    return 0.5 * x * (1.0 + jnp.tanh(0.7978845608028654 * (x + 0.044715 * (x * x * x))))


def _compress_kernel(ra_ref, rb_ref, pos_ref, w1_ref, b1_ref, w2_ref, b2_ref, kn_ref, kc_ref, vc_ref):
    half = (CMP_BLOCK // 2) * HEAD_DIM
    for j, o_ref in enumerate((kc_ref, vc_ref)):
        xa = (ra_ref[j, 0, 0].astype(F32) + pos_ref[j, :, 0:half]).astype(BF16)
        xb = (rb_ref[j, 0, 0].astype(F32) + pos_ref[j, :, half:2 * half]).astype(BF16)
        hid = (jnp.dot(xa, w1_ref[j, 0:half, :], preferred_element_type=F32)
               + jnp.dot(xb, w1_ref[j, half:2 * half, :], preferred_element_type=F32) + b1_ref[j])
        out = jnp.dot(_gelu_tanh(hid).astype(BF16), w2_ref[j], preferred_element_type=F32) + b2_ref[j]
        if j == 0:
            out = out * lax.rsqrt(jnp.mean(out * out, axis=-1, keepdims=True) + EPS) * kn_ref[...]
        o_ref[0, 0] = out


def nsa_compress(y3, cmp_pos, cmp_w1, cmp_b1, cmp_w2, cmp_b2, k_norm0):
    b, s, _ = y3.shape
    G, dh = NSA_KV_HEADS, HEAD_DIM
    nr = s // CMP_STRIDE
    wide = CMP_STRIDE * dh
    kv = y3[:, :, C_NSA_KV:C_NSA_KV + 2 * G * dh].reshape(b, s, 2, G, dh)
    ra = kv.transpose(2, 0, 3, 1, 4).reshape(2, b, G, nr, wide)
    rb = jnp.concatenate([ra[:, :, :, 1:], jnp.zeros((2, b, G, 1, wide), ra.dtype)], axis=3)
    hidden = cmp_w1.shape[-1]
    out = jax.ShapeDtypeStruct((b, G, nr, dh), F32)
    blk = pl.BlockSpec((2, 1, 1, nr, wide), lambda i, g: (0, i, g, 0, 0))
    oblk = pl.BlockSpec((1, 1, nr, dh), lambda i, g: (i, g, 0, 0))

    def full(shape):
        return pl.BlockSpec(shape, lambda i, g: (0,) * len(shape))

    return pl.pallas_call(
        _compress_kernel,
        out_shape=(out, out),
        grid=(b, G),
        in_specs=[blk, blk, full((2, 1, 2 * wide)), full((2, 2 * wide, hidden)), full((2, 1, hidden)),
                  full((2, hidden, dh)), full((2, 1, dh)), full((1, dh))],
        out_specs=(oblk, oblk),
        compiler_params=_params("parallel", "parallel"),
        name="nsa_compress",
    )(ra, rb, cmp_pos.reshape(2, 1, 2 * wide), cmp_w1.astype(BF16), cmp_b1.reshape(2, 1, hidden),
      cmp_w2.astype(BF16), cmp_b2.reshape(2, 1, dh), k_norm0.reshape(1, dh))


def _nsa_kernel_rowmajor(q_ref, gl_ref, kc_ref, vc_ref, ks_ref, vs_ref, kw_ref, vw_ref, o_ref, *, n_sel, top):
    QB, R, dh = 128, NSA_GROUP, HEAD_DIM
    rows = R * QB
    g = pl.program_id(1)
    qi = pl.program_id(2)
    q0 = qi * QB
    q = q_ref[0, 0, 0] * jnp.asarray(dh ** -0.5, BF16)
    row = _iota((rows, 1), 0)
    t_f = (q0 + (row & (QB - 1))).astype(F32)
    head = g * R + (row >> 7)
    slope = jnp.exp2(-(head + 1).astype(F32))

    n_cmp = kc_ref.shape[2]
    kc = kc_ref[0, 0]
    kc_hi = kc.astype(BF16)
    kc_lo = (kc - kc_hi.astype(F32)).astype(BF16)
    cmp_end = (_iota((1, n_cmp), 1) * CMP_STRIDE + (CMP_BLOCK - 1)).astype(F32)
    dist = t_f - cmp_end
    valid = dist >= 0.0
    s = jnp.where(valid, _dot_nt(q, kc_hi) + _dot_nt(q, kc_lo) - slope * dist, NEG)
    e = jnp.exp(s - jnp.max(s, axis=1, keepdims=True))
    p = jnp.where(valid, e / jnp.sum(e, axis=1, keepdims=True), 0.0)
    o_cmp = jnp.dot(p.astype(BF16), vc_ref[0, 0].astype(BF16), preferred_element_type=F32)

    p_grp = p[0:QB] + p[QB:2 * QB] + p[2 * QB:3 * QB] + p[3 * QB:4 * QB]
    c0 = _iota((n_cmp, n_sel), 0) * CMP_STRIDE
    s0 = _iota((n_cmp, n_sel), 1) * SEL_BLOCK
    overlap = ((c0 < s0 + SEL_BLOCK) & (c0 + CMP_BLOCK > s0)).astype(BF16)
    imp = _split_dot(p_grp, overlap)
    tq = q0 + _iota((QB, 1), 0)
    j_idx = _iota((1, n_sel), 1)
    cur = tq >> 6
    forced = (j_idx == 0) | (j_idx == cur) | (j_idx == cur - 1)
    imp = jnp.where(j_idx * SEL_BLOCK <= tq, imp + jnp.where(forced, FORCE_BONUS, 0.0), -1.0)
    sel = jnp.zeros((QB, n_sel), F32)
    for _ in range(top):
        mx = jnp.max(imp, axis=1, keepdims=True)
        first = jnp.min(jnp.where(imp == mx, j_idx, n_sel), axis=1, keepdims=True)
        pick = j_idx == first
        sel = jnp.where(pick, 1.0, sel)
        imp = jnp.where(pick, -3e38, imp)
    sel_bf = sel.astype(BF16)

    def flash(k_ref, v_ref, lo, hi, mask_fn):
        def body(kb, st):
            m_run, l_run, acc = st
            k0 = pl.multiple_of(kb * QB, QB)
            kblk = k_ref[0, 0, pl.ds(k0, QB), :]
            vblk = v_ref[0, 0, pl.ds(k0, QB), :]
            dist = t_f - (k0 + _iota((1, QB), 1)).astype(F32)
            ok = mask_fn(kb, dist)
            sc = jnp.where(ok, _dot_nt(q, kblk) - slope * dist, NEG)
            m_new = jnp.maximum(m_run, jnp.max(sc, axis=1, keepdims=True))
            alpha = jnp.exp(m_run - m_new)
            pr = jnp.where(ok, jnp.exp(sc - m_new), 0.0)
            l_run = alpha * l_run + jnp.sum(pr, axis=1, keepdims=True)
            acc = alpha * acc + jnp.dot(pr.astype(BF16), vblk, preferred_element_type=F32)
            return m_new, l_run, acc

        init = (jnp.full((rows, 1), NEG, F32), jnp.zeros((rows, 1), F32), jnp.zeros((rows, dh), F32))
        _, l_run, acc = lax.fori_loop(lo, hi, body, init)
        return acc / l_run

    def sel_mask(kb, dist):
        expand = (_iota((n_sel, QB), 0) == 2 * kb + (_iota((n_sel, QB), 1) >> 6)).astype(BF16)
        m = jnp.dot(sel_bf, expand, preferred_element_type=F32)
        m = jnp.concatenate([m] * R, axis=0)
        return (m > 0.5) & (dist >= 0.0)

    def win_mask(kb, dist):
        return (dist >= 0.0) & (dist < float(WINDOW))

    o_sel = flash(ks_ref, vs_ref, 0, qi + 1, sel_mask)
    o_win = flash(kw_ref, vw_ref, jnp.maximum(qi - WINDOW // QB, 0), qi + 1, win_mask)

    gate = jax.nn.sigmoid(gl_ref[0, 0, 0])
    o_ref[0, 0, 0] = (gate[:, 0:1] * o_cmp + gate[:, 1:2] * o_sel + gate[:, 2:3] * o_win).astype(o_ref.dtype)


def nsa_attention_rowmajor(y3, small3, normed, kc, vc):
    b, s, _ = y3.shape
    G, R, dh, QB = NSA_KV_HEADS, NSA_GROUP, HEAD_DIM, 128
    nq = s // QB
    n_sel = s // SEL_BLOCK
    top = min(SEL_TOPK, n_sel)

    def stack_heads(a, width):
        return (a.reshape(b, nq, QB, G, R, width).transpose(0, 3, 1, 4, 2, 5)
                .reshape(b, G, nq, R * QB, width))

    def kv_heads(a):
        return a.reshape(b, s, G, dh).transpose(0, 2, 1, 3)

    q = stack_heads(normed[:, :, 0:MIX_WIDTH], dh)
    gl = stack_heads(small3[:, :, S_NSA_GATE:S_NSA_GATE + NSA_HEADS * N_BRANCH], N_BRANCH)
    ks = kv_heads(normed[:, :, 4 * LANES:5 * LANES])
    kw = kv_heads(normed[:, :, 5 * LANES:6 * LANES])
    vs = kv_heads(y3[:, :, C_NSA_KV + 3 * LANES:C_NSA_KV + 4 * LANES])
    vw = kv_heads(y3[:, :, C_NSA_KV + 5 * LANES:C_NSA_KV + 6 * LANES])
    n_cmp = kc.shape[2]
    qspec = pl.BlockSpec((1, 1, 1, R * QB, dh), lambda i, g, j: (i, g, j, 0, 0))
    gspec = pl.BlockSpec((1, 1, 1, R * QB, N_BRANCH), lambda i, g, j: (i, g, j, 0, 0))
    cspec = pl.BlockSpec((1, 1, n_cmp, dh), lambda i, g, j: (i, g, 0, 0))
    kvspec = pl.BlockSpec((1, 1, s, dh), lambda i, g, j: (i, g, 0, 0))
    out = pl.pallas_call(
        functools.partial(_nsa_kernel_rowmajor, n_sel=n_sel, top=top),
        out_shape=jax.ShapeDtypeStruct((b, G, nq, R * QB, dh), BF16),
        grid=(b, G, nq),
        in_specs=[qspec, gspec, cspec, cspec, kvspec, kvspec, kvspec, kvspec],
        out_specs=qspec,
        compiler_params=_params("parallel", "parallel", "parallel"),
        name="nsa_attention",
    )(q, gl, kc, vc, ks, vs, kw, vw)
    return (out.reshape(b, G, nq, R, QB, dh).transpose(0, 2, 4, 1, 3, 5).reshape(b, s, MIX_WIDTH))


NSA_QB = 128
NSA_KS = 512
NSA_CK = 256
NSA_VR = 80
A_FEAT, A_PEN, A_BIAS = 0, 64, 128
A_DUMMY = A_BIAS + 4


def _head_rms(x, gain):
    w = x.shape[1]
    same_head = (_iota((w, w), 0) // HEAD_DIM == _iota((w, w), 1) // HEAD_DIM).astype(BF16)
    ss = _split_dot(x * x, same_head)
    return x * lax.rsqrt(ss * (1.0 / HEAD_DIM) + EPS) * gain


def _nsa_kernel(q_ref, gl_ref, kca_ref, vcT_ref, ks_ref, kw_ref, vs_ref, vw_ref, kconst_ref, gq_ref, gk_ref,
                o_ref, qt_ref, ksa_ref, kwa_ref, vsa_ref, vwa_ref, s_ref, *, n_sel, top):
    QB, R, dh, CK, VR, KS = NSA_QB, NSA_GROUP, HEAD_DIM, NSA_CK, NSA_VR, NSA_KS
    HQ = R * QB
    NPAD = WINDOW // QB
    SUB = KS // QB
    g = pl.program_id(1)
    qi = pl.program_id(2)
    q0 = qi * QB
    nkb = vsa_ref.shape[0]

    @pl.when(qi == 0)
    def _():
        ksa_ref[...] = kconst_ref[...]
        kwa_ref[0:WINDOW, :] = jnp.where(_iota((WINDOW, CK), 1) == A_DUMMY, 1.0, 0.0).astype(BF16)
        kwa_ref[WINDOW:, :] = kconst_ref[...]
        kwa_ref[WINDOW:, A_PEN:A_PEN + 64] = jnp.zeros((kwa_ref.shape[0] - WINDOW, 64), BF16)
        for gg in range(NSA_KV_HEADS):
            @pl.when(g == gg)
            def _():
                heads = slice(gg * dh, (gg + 1) * dh)
                for c in range(ks_ref.shape[1] // KS):
                    keys = slice(c * KS, (c + 1) * KS)
                    ks_n = _head_rms(ks_ref[0, keys, :].astype(F32), gk_ref[0])
                    kw_n = _head_rms(kw_ref[0, keys, :].astype(F32), gk_ref[1])
                    ksa_ref[keys, A_FEAT:A_FEAT + dh] = ks_n[:, heads].astype(BF16)
                    kwa_ref[WINDOW + c * KS:WINDOW + (c + 1) * KS, A_FEAT:A_FEAT + dh] = kw_n[:, heads].astype(BF16)
                for c in range(nkb):
                    keys = slice(c * QB, (c + 1) * QB)
                    vsa_ref[c, 0:dh, :] = vs_ref[0, keys, :].astype(F32).T[heads].astype(BF16)
                    vwa_ref[NPAD + c, 0:dh, :] = vw_ref[0, keys, :].astype(F32).T[heads].astype(BF16)
        vwa_ref[0:NPAD, 0:dh, :] = jnp.zeros((NPAD, dh, QB), BF16)
        ones_rows = jnp.where(_iota((nkb + NPAD, VR - dh, QB), 1) == 0, 1.0, 0.0).astype(BF16)
        vsa_ref[:, dh:VR, :] = ones_rows[0:nkb]
        vwa_ref[:, dh:VR, :] = ones_rows
        qt_ref[A_BIAS + 16:CK, :] = jnp.zeros((CK - A_BIAS - 16, HQ), BF16)

    q_n = _head_rms(q_ref[0].astype(F32), gq_ref[...]).astype(BF16)
    q_rows = (q_n.astype(F32) * (dh ** -0.5)).T
    qT = jnp.concatenate([q_rows[r * dh:(r + 1) * dh] for r in range(R)], axis=1).astype(BF16)
    qt_ref[A_FEAT:A_FEAT + dh, :] = qT
    qt_ref[A_PEN:A_PEN + dh, :] = qT
    lane = _iota((16, HQ), 1)
    rowi = _iota((16, HQ), 0)
    t_q = q0 + (lane & (QB - 1))
    slope = jnp.exp2(-(g * R + (lane >> 7) + 1).astype(F32))
    t_hi = ((t_q >> 6) << 6).astype(F32)
    t_lo = (t_q & 63).astype(F32)
    bias_rows = jnp.where(rowi < 2, slope,
                          jnp.where(rowi == 2, -slope * t_hi,
                                    jnp.where(rowi == 3, -slope * t_lo,
                                              jnp.where(rowi == A_DUMMY - A_BIAS, NEG, 0.0))))
    qt_ref[A_BIAS:A_BIAS + 16, :] = bias_rows.astype(BF16)
    k_loc = _iota((QB, HQ), 0)
    q_loc = _iota((QB, HQ), 1) & (QB - 1)

    def pv(v_ref_, kb0, pr):
        out = None
        for i in range(pr.shape[0] // QB):
            term = jnp.dot(v_ref_[kb0 + i], pr[i * QB:(i + 1) * QB], preferred_element_type=F32)
            out = term if out is None else out + term
        return out

    n_cmp = kca_ref.shape[2]
    sc = jnp.dot(kca_ref[0, 0], qt_ref[...], preferred_element_type=F32)
    cmp_end = _iota((n_cmp, HQ), 0) * CMP_STRIDE + (CMP_BLOCK - 1)
    valid = cmp_end <= q0 + (_iota((n_cmp, HQ), 1) & (QB - 1))
    sc = jnp.where(valid, sc, NEG)
    e = jnp.exp(sc - jnp.max(sc, axis=0, keepdims=True))
    p = jnp.where(valid, e * (1.0 / jnp.sum(e, axis=0, keepdims=True)), 0.0)
    o_cmp = jnp.dot(vcT_ref[0, 0], p.astype(BF16), preferred_element_type=F32)

    sw = jnp.dot(kwa_ref[pl.ds(pl.multiple_of(q0, QB), WINDOW + QB), :], qt_ref[...],
                 preferred_element_type=F32)
    sw = jnp.concatenate([jnp.where(k_loc > q_loc, sw[0:QB], NEG), sw[QB:WINDOW],
                          jnp.where(k_loc <= q_loc, sw[WINDOW:WINDOW + QB], NEG)], axis=0)
    pw = jnp.exp(sw - jnp.max(sw, axis=0, keepdims=True)).astype(BF16)
    acc_w = pv(vwa_ref, qi, pw)
    o_win = acc_w[0:dh] / acc_w[dh:dh + 1]

    p_grp = p[:, 0:QB] + p[:, QB:2 * QB] + p[:, 2 * QB:3 * QB] + p[:, 3 * QB:4 * QB]
    c0 = _iota((n_sel, n_cmp), 1) * CMP_STRIDE
    s0 = _iota((n_sel, n_cmp), 0) * SEL_BLOCK
    overlap_t = ((c0 < s0 + SEL_BLOCK) & (c0 + CMP_BLOCK > s0)).astype(BF16)
    imp = _split_dot_left(overlap_t, p_grp)
    j_idx = _iota((n_sel, QB), 0)
    tq = q0 + _iota((n_sel, QB), 1)
    cur = tq >> 6
    forced = (j_idx == 0) | (j_idx == cur) | (j_idx == cur - 1)
    causal_blk = j_idx * SEL_BLOCK <= tq
    imp = jnp.where(causal_blk, imp + jnp.where(forced, FORCE_BONUS, 0.0), -1.0)
    sel = jnp.zeros((n_sel, QB), F32)
    for _ in range(top):
        mx = jnp.max(imp, axis=0, keepdims=True)
        first = jnp.min(jnp.where(imp == mx, j_idx, n_sel), axis=0, keepdims=True)
        pick = j_idx == first
        sel = jnp.where(pick, 1.0, sel)
        imp = jnp.where(pick, -3e38, imp)
    pen = jnp.where((sel > 0.5) & causal_blk, 0.0, NEG)
    if n_sel < 64:
        pen = jnp.concatenate([pen, jnp.zeros((64 - n_sel, QB), F32)], axis=0)
    qt_ref[A_PEN:A_PEN + 64, :] = jnp.concatenate([pen] * R, axis=1).astype(BF16)

    def score(j):
        return jnp.dot(ksa_ref[pl.ds(pl.multiple_of(j * KS, KS), KS), :], qt_ref[...],
                       preferred_element_type=F32)

    def absorb(s, j, st):
        m_run, acc = st
        m_new = jnp.maximum(m_run, jnp.max(s, axis=0, keepdims=True))
        pr = jnp.exp(s - m_new).astype(BF16)
        return m_new, jnp.exp(m_run - m_new) * acc + pv(vsa_ref, j * SUB, pr)

    def body(j, carry):
        s_cur, st = carry
        s_next = score(j + 1)
        return s_next, absorb(s_cur, j, st)

    n_full = qi // SUB
    init = (jnp.full((1, HQ), NEG, F32), jnp.zeros((VR, HQ), F32))
    s_last, st = lax.fori_loop(0, n_full, body, (score(0), init))
    s_ref[...] = s_last
    diag = pl.ds(pl.multiple_of(q0 - n_full * KS, QB), QB)
    s_ref[diag, :] = jnp.where(k_loc <= q_loc, s_ref[diag, :], NEG)
    _, acc_s = absorb(s_ref[...], n_full, st)
    o_sel = acc_s[0:dh] / acc_s[dh:dh + 1]

    gl_t = gl_ref[0].T
    gate = []
    for br in range(N_BRANCH):
        per_kv = [jnp.concatenate([gl_t[(gg * R + r) * N_BRANCH + br:(gg * R + r) * N_BRANCH + br + 1]
                                   for r in range(R)], axis=1) for gg in range(NSA_KV_HEADS)]
        gate.append(jax.nn.sigmoid(jnp.where(g == 0, per_kv[0], per_kv[1])))
    o_t = gate[0] * o_cmp + gate[1] * o_sel + gate[2] * o_win
    o_ref[0] = jnp.concatenate([o_t[:, r * QB:(r + 1) * QB] for r in range(R)], axis=0).T.astype(o_ref.dtype)


def _nsa_pair_kernel(q_ref, gl_ref, kca_ref, vcT_ref, ks_ref, kw_ref, vs_ref, vw_ref, kconst_ref, gq_ref, gk_ref,
                     o_ref, qt_ref, ksa_ref, kwa_ref, vsa_ref, vwa_ref, s_ref, *, n_sel, top):
    QB, R, dh, CK, VR, KS = NSA_QB, NSA_GROUP, HEAD_DIM, NSA_CK, NSA_VR, NSA_KS
    G = NSA_KV_HEADS
    GR = range(G)
    HQ = R * QB
    NPAD = WINDOW // QB
    SUB = KS // QB
    qi = pl.program_id(1)
    q0 = qi * QB
    nkb = vsa_ref.shape[1]

    @pl.when(qi == 0)
    def _():
        pad_keys = jnp.where(_iota((WINDOW, CK), 1) == A_DUMMY, 1.0, 0.0).astype(BF16)
        ones_rows = jnp.where(_iota((nkb + NPAD, VR - dh, QB), 1) == 0, 1.0, 0.0).astype(BF16)
        for g in GR:
            heads = slice(g * dh, (g + 1) * dh)
            ksa_ref[g] = kconst_ref[...]
            kwa_ref[g, 0:WINDOW, :] = pad_keys
            kwa_ref[g, WINDOW:, :] = kconst_ref[...]
            kwa_ref[g, WINDOW:, A_PEN:A_PEN + 64] = jnp.zeros((kwa_ref.shape[1] - WINDOW, 64), BF16)
            vwa_ref[g, 0:NPAD, 0:dh, :] = jnp.zeros((NPAD, dh, QB), BF16)
            vsa_ref[g, :, dh:VR, :] = ones_rows[0:nkb]
            vwa_ref[g, :, dh:VR, :] = ones_rows
        for c in range(ks_ref.shape[1] // KS):
            keys = slice(c * KS, (c + 1) * KS)
            ks_n = _head_rms(ks_ref[0, keys, :].astype(F32), gk_ref[0]).astype(BF16)
            kw_n = _head_rms(kw_ref[0, keys, :].astype(F32), gk_ref[1]).astype(BF16)
            for g in GR:
                heads = slice(g * dh, (g + 1) * dh)
                ksa_ref[g, keys, A_FEAT:A_FEAT + dh] = ks_n[:, heads]
                kwa_ref[g, WINDOW + c * KS:WINDOW + (c + 1) * KS, A_FEAT:A_FEAT + dh] = kw_n[:, heads]
        for c in range(nkb):
            keys = slice(c * QB, (c + 1) * QB)
            vs_t = vs_ref[0, keys, :].astype(F32).T.astype(BF16)
            vw_t = vw_ref[0, keys, :].astype(F32).T.astype(BF16)
            for g in GR:
                heads = slice(g * dh, (g + 1) * dh)
                vsa_ref[g, c, 0:dh, :] = vs_t[heads]
                vwa_ref[g, NPAD + c, 0:dh, :] = vw_t[heads]
        qt_ref[:, A_BIAS + 16:CK, :] = jnp.zeros((G, CK - A_BIAS - 16, HQ), BF16)

    q_n = _head_rms(q_ref[0].astype(F32), gq_ref[...]).astype(BF16)
    q_rows = (q_n.astype(F32) * (dh ** -0.5)).T
    lane = _iota((16, HQ), 1)
    rowi = _iota((16, HQ), 0)
    t_q = q0 + (lane & (QB - 1))
    t_hi = ((t_q >> 6) << 6).astype(F32)
    t_lo = (t_q & 63).astype(F32)
    for g in GR:
        qT = jnp.concatenate([q_rows[(g * R + r) * dh:(g * R + r + 1) * dh] for r in range(R)],
                             axis=1).astype(BF16)
        qt_ref[g, A_FEAT:A_FEAT + dh, :] = qT
        qt_ref[g, A_PEN:A_PEN + dh, :] = qT
        slope = jnp.exp2(-(g * R + (lane >> 7) + 1).astype(F32))
        bias_rows = jnp.where(rowi < 2, slope,
                              jnp.where(rowi == 2, -slope * t_hi,
                                        jnp.where(rowi == 3, -slope * t_lo,
                                                  jnp.where(rowi == A_DUMMY - A_BIAS, NEG, 0.0))))
        qt_ref[g, A_BIAS:A_BIAS + 16, :] = bias_rows.astype(BF16)
    k_loc = _iota((QB, HQ), 0)
    q_loc = _iota((QB, HQ), 1) & (QB - 1)

    def pv(v_ref_, g, kb0, pr):
        out = None
        for i in range(pr.shape[0] // QB):
            term = jnp.dot(v_ref_[g, kb0 + i], pr[i * QB:(i + 1) * QB], preferred_element_type=F32)
            out = term if out is None else out + term
        return out

    n_cmp = kca_ref.shape[2]
    cmp_end = _iota((n_cmp, HQ), 0) * CMP_STRIDE + (CMP_BLOCK - 1)
    valid = cmp_end <= q0 + (_iota((n_cmp, HQ), 1) & (QB - 1))
    sc = [jnp.where(valid, jnp.dot(kca_ref[0, g], qt_ref[g], preferred_element_type=F32), NEG) for g in GR]
    e = [jnp.exp(sc[g] - jnp.max(sc[g], axis=0, keepdims=True)) for g in GR]
    p = [jnp.where(valid, e[g] * (1.0 / jnp.sum(e[g], axis=0, keepdims=True)), 0.0) for g in GR]
    o_cmp = [jnp.dot(vcT_ref[0, g], p[g].astype(BF16), preferred_element_type=F32) for g in GR]

    win_rows = pl.ds(pl.multiple_of(q0, QB), WINDOW + QB)
    sw = [jnp.dot(kwa_ref[g, win_rows, :], qt_ref[g], preferred_element_type=F32) for g in GR]
    sw = [jnp.concatenate([jnp.where(k_loc > q_loc, sw[g][0:QB], NEG), sw[g][QB:WINDOW],
                           jnp.where(k_loc <= q_loc, sw[g][WINDOW:WINDOW + QB], NEG)], axis=0) for g in GR]
    pw = [jnp.exp(sw[g] - jnp.max(sw[g], axis=0, keepdims=True)).astype(BF16) for g in GR]
    acc_w = [pv(vwa_ref, g, qi, pw[g]) for g in GR]
    o_win = [acc_w[g][0:dh] / acc_w[g][dh:dh + 1] for g in GR]

    c0 = _iota((n_sel, n_cmp), 1) * CMP_STRIDE
    s0 = _iota((n_sel, n_cmp), 0) * SEL_BLOCK
    overlap_t = ((c0 < s0 + SEL_BLOCK) & (c0 + CMP_BLOCK > s0)).astype(BF16)
    j_idx = _iota((n_sel, QB), 0)
    tq = q0 + _iota((n_sel, QB), 1)
    cur = tq >> 6
    forced = (j_idx == 0) | (j_idx == cur) | (j_idx == cur - 1)
    causal_blk = j_idx * SEL_BLOCK <= tq
    p_grp = [p[g][:, 0:QB] + p[g][:, QB:2 * QB] + p[g][:, 2 * QB:3 * QB] + p[g][:, 3 * QB:4 * QB] for g in GR]
    imp = [jnp.where(causal_blk, _split_dot_left(overlap_t, p_grp[g]) + jnp.where(forced, FORCE_BONUS, 0.0), -1.0)
           for g in GR]
    sel = [jnp.zeros((n_sel, QB), F32) for g in GR]
    for _ in range(top):
        for g in GR:
            mx = jnp.max(imp[g], axis=0, keepdims=True)
            first = jnp.min(jnp.where(imp[g] == mx, j_idx, n_sel), axis=0, keepdims=True)
            pick = j_idx == first
            sel[g] = jnp.where(pick, 1.0, sel[g])
            imp[g] = jnp.where(pick, -3e38, imp[g])
    for g in GR:
        pen = jnp.where((sel[g] > 0.5) & causal_blk, 0.0, NEG)
        if n_sel < 64:
            pen = jnp.concatenate([pen, jnp.zeros((64 - n_sel, QB), F32)], axis=0)
        qt_ref[g, A_PEN:A_PEN + 64, :] = jnp.concatenate([pen] * R, axis=1).astype(BF16)

    def score(j):
        rows = pl.ds(pl.multiple_of(j * KS, KS), KS)
        return [jnp.dot(ksa_ref[g, rows, :], qt_ref[g], preferred_element_type=F32) for g in GR]

    def absorb(s, j, st):
        m_new = [jnp.maximum(st[g][0], jnp.max(s[g], axis=0, keepdims=True)) for g in GR]
        pr = [jnp.exp(s[g] - m_new[g]).astype(BF16) for g in GR]
        return [(m_new[g], jnp.exp(st[g][0] - m_new[g]) * st[g][1] + pv(vsa_ref, g, j * SUB, pr[g])) for g in GR]

    n_full = qi // SUB
    init = [(jnp.full((1, HQ), NEG, F32), jnp.zeros((VR, HQ), F32)) for g in GR]
    st = lax.fori_loop(0, n_full, lambda j, st_: absorb(score(j), j, st_), init)
    s_last = score(n_full)
    diag = pl.ds(pl.multiple_of(q0 - n_full * KS, QB), QB)
    for g in GR:
        s_ref[g] = s_last[g]
        s_ref[g, diag, :] = jnp.where(k_loc <= q_loc, s_ref[g, diag, :], NEG)
    st = absorb([s_ref[g] for g in GR], n_full, st)
    o_sel = [st[g][1][0:dh] / st[g][1][dh:dh + 1] for g in GR]

    gl_t = gl_ref[0].T
    rows_out = []
    for g in GR:
        gate = [jax.nn.sigmoid(jnp.concatenate(
            [gl_t[(g * R + r) * N_BRANCH + br:(g * R + r) * N_BRANCH + br + 1] for r in range(R)], axis=1))
            for br in range(N_BRANCH)]
        o_t = gate[0] * o_cmp[g] + gate[1] * o_sel[g] + gate[2] * o_win[g]
        rows_out += [o_t[:, r * QB:(r + 1) * QB] for r in range(R)]
    o_ref[0] = jnp.concatenate(rows_out, axis=0).T.astype(o_ref.dtype)


def _nsa_t_kernel(qn_ref, gl_ref, kca_ref, vcT_ref, ksn_ref, kwn_ref, vs_ref, vw_ref, kconst_ref, o_ref,
                  qt_ref, ksa_ref, kwa_ref, vsa_ref, vwa_ref, *, n_sel, top):
    QB, R, dh, CK, VR, KS = NSA_QB, NSA_GROUP, HEAD_DIM, NSA_CK, NSA_VR, NSA_KS
    HQ = R * QB
    g = pl.program_id(1)
    qi = pl.program_id(2)
    q0 = qi * QB
    nkb = vsa_ref.shape[0]

    @pl.when(qi == 0)
    def _():
        ksa_ref[...] = kconst_ref[...]
        kwa_ref[...] = kconst_ref[...]
        kwa_ref[:, A_PEN:A_PEN + 64] = jnp.zeros((kwa_ref.shape[0], 64), BF16)
        for gg in range(NSA_KV_HEADS):
            @pl.when(g == gg)
            def _():
                heads = slice(gg * dh, (gg + 1) * dh)
                ksa_ref[:, A_FEAT:A_FEAT + dh] = ksn_ref[0, :, heads]
                kwa_ref[:, A_FEAT:A_FEAT + dh] = kwn_ref[0, :, heads]
                for c in range(nkb):
                    keys = slice(c * QB, (c + 1) * QB)
                    vsa_ref[c, 0:dh, :] = vs_ref[0, keys, :].astype(F32).T[heads].astype(BF16)
                    vwa_ref[c, 0:dh, :] = vw_ref[0, keys, :].astype(F32).T[heads].astype(BF16)
        ones_rows = jnp.where(_iota((nkb, VR - dh, QB), 1) == 0, 1.0, 0.0).astype(BF16)
        vsa_ref[:, dh:VR, :] = ones_rows
        vwa_ref[:, dh:VR, :] = ones_rows
        qt_ref[A_BIAS + 16:CK, :] = jnp.zeros((CK - A_BIAS - 16, HQ), BF16)

    q_rows = (qn_ref[0].astype(F32) * (dh ** -0.5)).T
    qT = jnp.concatenate([q_rows[r * dh:(r + 1) * dh] for r in range(R)], axis=1).astype(BF16)
    qt_ref[A_FEAT:A_FEAT + dh, :] = qT
    qt_ref[A_PEN:A_PEN + dh, :] = qT
    lane = _iota((16, HQ), 1)
    rowi = _iota((16, HQ), 0)
    t_q = q0 + (lane & (QB - 1))
    slope = jnp.exp2(-(g * R + (lane >> 7) + 1).astype(F32))
    t_hi = ((t_q >> 6) << 6).astype(F32)
    t_lo = (t_q & 63).astype(F32)
    bias_rows = jnp.where(rowi < 2, slope,
                          jnp.where(rowi == 2, -slope * t_hi, jnp.where(rowi == 3, -slope * t_lo, 0.0)))
    qt_ref[A_BIAS:A_BIAS + 16, :] = bias_rows.astype(BF16)

    n_cmp = kca_ref.shape[2]
    sc = jnp.dot(kca_ref[0, 0], qt_ref[...], preferred_element_type=F32)
    cmp_end = _iota((n_cmp, HQ), 0) * CMP_STRIDE + (CMP_BLOCK - 1)
    valid = cmp_end <= q0 + (_iota((n_cmp, HQ), 1) & (QB - 1))
    sc = jnp.where(valid, sc, NEG)
    e = jnp.exp(sc - jnp.max(sc, axis=0, keepdims=True))
    p = jnp.where(valid, e / jnp.sum(e, axis=0, keepdims=True), 0.0)
    o_cmp = jnp.dot(vcT_ref[0, 0], p.astype(BF16), preferred_element_type=F32)

    p_grp = p[:, 0:QB] + p[:, QB:2 * QB] + p[:, 2 * QB:3 * QB] + p[:, 3 * QB:4 * QB]
    c0 = _iota((n_sel, n_cmp), 1) * CMP_STRIDE
    s0 = _iota((n_sel, n_cmp), 0) * SEL_BLOCK
    overlap_t = ((c0 < s0 + SEL_BLOCK) & (c0 + CMP_BLOCK > s0)).astype(BF16)
    imp = _split_dot_left(overlap_t, p_grp)
    j_idx = _iota((n_sel, QB), 0)
    tq = q0 + _iota((n_sel, QB), 1)
    cur = tq >> 6
    forced = (j_idx == 0) | (j_idx == cur) | (j_idx == cur - 1)
    imp = jnp.where(j_idx * SEL_BLOCK <= tq, imp + jnp.where(forced, FORCE_BONUS, 0.0), -1.0)
    sel = jnp.zeros((n_sel, QB), F32)
    for _ in range(top):
        mx = jnp.max(imp, axis=0, keepdims=True)
        first = jnp.min(jnp.where(imp == mx, j_idx, n_sel), axis=0, keepdims=True)
        pick = j_idx == first
        sel = jnp.where(pick, 1.0, sel)
        imp = jnp.where(pick, -3e38, imp)
    pen = jnp.where(sel > 0.5, 0.0, NEG)
    if n_sel < 64:
        pen = jnp.concatenate([pen, jnp.zeros((64 - n_sel, QB), F32)], axis=0)
    qt_ref[A_PEN:A_PEN + 64, :] = jnp.concatenate([pen] * R, axis=1).astype(BF16)

    def attend(kaug_ref, vaug_ref, k0, nk, st, mode):
        k0 = pl.multiple_of(k0, QB)
        s = jnp.dot(kaug_ref[pl.ds(k0, nk), :], qt_ref[...], preferred_element_type=F32)
        if mode != "full":
            dist = (q0 + (_iota((nk, HQ), 1) & (QB - 1))) - (k0 + _iota((nk, HQ), 0))
            ok = dist >= 0
            if mode == "window":
                ok = ok & (dist < WINDOW)
            s = jnp.where(ok, s, NEG)
        m_run, acc = st
        m_new = jnp.maximum(m_run, jnp.max(s, axis=0, keepdims=True))
        pr = jnp.exp(s - m_new).astype(BF16)
        acc = jnp.exp(m_run - m_new) * acc
        kb0 = k0 // QB
        for i in range(nk // QB):
            acc = acc + jnp.dot(vaug_ref[kb0 + i], pr[i * QB:(i + 1) * QB], preferred_element_type=F32)
        return m_new, acc

    def finish(st):
        return st[1][0:dh] / st[1][dh:dh + 1]

    init = (jnp.full((1, HQ), NEG, F32), jnp.zeros((VR, HQ), F32))
    n_full = qi // (KS // QB)
    st = lax.fori_loop(0, n_full, lambda j, s_: attend(ksa_ref, vsa_ref, j * KS, KS, s_, "full"), init)
    o_sel = finish(attend(ksa_ref, vsa_ref, n_full * KS, KS, st, "causal"))
    o_win = finish(attend(kwa_ref, vwa_ref, jnp.maximum(q0 - WINDOW, 0), WINDOW + QB, init, "window"))

    gl_t = gl_ref[0].T
    gate = []
    for br in range(N_BRANCH):
        per_kv = [jnp.concatenate([gl_t[(gg * R + r) * N_BRANCH + br:(gg * R + r) * N_BRANCH + br + 1]
                                   for r in range(R)], axis=1) for gg in range(NSA_KV_HEADS)]
        gate.append(jax.nn.sigmoid(jnp.where(g == 0, per_kv[0], per_kv[1])))
    o_t = gate[0] * o_cmp + gate[1] * o_sel + gate[2] * o_win
    o_ref[0] = jnp.concatenate([o_t[:, r * QB:(r + 1) * QB] for r in range(R)], axis=0).T.astype(o_ref.dtype)


def nsa_attention(y3, small3, q_norm, k_norm, kc, vc):
    b, s, _ = y3.shape
    G, R, dh, QB, CK, VR = NSA_KV_HEADS, NSA_GROUP, HEAD_DIM, NSA_QB, NSA_CK, NSA_VR
    assert G == 2, "the kernel picks a kv head's gate rows with a two-way select"
    HQ = R * QB
    nq = s // QB
    n_sel = s // SEL_BLOCK
    assert n_sel <= 64, "selection one-hot columns hold at most 64 blocks"
    top = min(SEL_TOPK, n_sel)
    n_cmp = kc.shape[2]

    def pos_cols(pos):
        return np.stack([pos // 64 * 64, pos % 64, np.ones_like(pos), np.ones_like(pos)], axis=1)

    vc_t = vc.transpose(0, 1, 3, 2).astype(BF16)

    pos = np.arange(s)
    kconst = np.zeros((s, CK), np.float32)
    kconst[pos, A_PEN + pos // SEL_BLOCK] = 1.0
    kconst[:, A_BIAS:A_BIAS + 4] = pos_cols(pos)
    kconst = jnp.asarray(kconst, BF16)

    kc_hi = kc.astype(BF16)
    kc_lo = (kc - kc_hi.astype(F32)).astype(BF16)
    cend = np.arange(n_cmp) * CMP_STRIDE + (CMP_BLOCK - 1)
    cbias = np.zeros((n_cmp, CK - 2 * dh), np.float32)
    cbias[:, 0:4] = pos_cols(cend)
    kc_aug = jnp.concatenate([kc_hi, kc_lo, jnp.broadcast_to(jnp.asarray(cbias, BF16), (b, G, n_cmp, CK - 2 * dh))],
                             axis=-1)

    qw = G * R * dh
    kvb = C_NSA_KV // LANES
    gq = jnp.tile(q_norm, G * R).reshape(1, qw)
    gk = jnp.stack([jnp.tile(k_norm[1], G), jnp.tile(k_norm[2], G)]).reshape(2, 1, LANES)

    def kv_spec(blk):
        return pl.BlockSpec((1, s, LANES), lambda i, j: (i, 0, kvb + blk))

    return pl.pallas_call(
        functools.partial(_nsa_pair_kernel, n_sel=n_sel, top=top),
        out_shape=jax.ShapeDtypeStruct((b, s, MIX_WIDTH), BF16),
        grid=(b, nq),
        in_specs=[pl.BlockSpec((1, QB, qw), lambda i, j: (i, j, C_NSA_Q // qw)),
                  pl.BlockSpec((1, QB, N_SMALL), lambda i, j: (i, j, 0)),
                  pl.BlockSpec((1, G, n_cmp, CK), lambda i, j: (i, 0, 0, 0)),
                  pl.BlockSpec((1, G, dh, n_cmp), lambda i, j: (i, 0, 0, 0)),
                  kv_spec(2), kv_spec(4), kv_spec(3), kv_spec(5),
                  pl.BlockSpec((s, CK), lambda i, j: (0, 0)),
                  pl.BlockSpec((1, qw), lambda i, j: (0, 0)),
                  pl.BlockSpec((2, 1, LANES), lambda i, j: (0, 0, 0))],
        out_specs=pl.BlockSpec((1, QB, qw), lambda i, j: (i, j, 0)),
        scratch_shapes=[pltpu.VMEM((G, CK, HQ), BF16),
                        pltpu.VMEM((G, s, CK), BF16), pltpu.VMEM((G, s + WINDOW, CK), BF16),
                        pltpu.VMEM((G, nq, VR, QB), BF16), pltpu.VMEM((G, nq + WINDOW // QB, VR, QB), BF16),
                        pltpu.VMEM((G, NSA_KS, HQ), F32)],
        compiler_params=_params("parallel", "arbitrary"),
        name="nsa_attention",
    )(y3, small3, kc_aug, vc_t, y3, y3, y3, y3, kconst, gq, gk)


def _merge_kernel(on_ref, os_ref, om_ref, g0_ref, g1_ref, g2_ref, wb_ref, wo_ref, x_ref, mod_ref, o_ref):
    merged = None
    for i, (o_r, g_r) in enumerate(((on_ref, g0_ref), (os_ref, g1_ref), (om_ref, g2_ref))):
        br = jnp.dot(o_r[0], wb_ref[i], preferred_element_type=F32)
        term = jax.nn.sigmoid(g_r[0].astype(F32)) * br
        merged = term if merged is None else merged + term
    out = jnp.dot(merged.astype(BF16), wo_ref[...], preferred_element_type=F32)
    o_ref[0] = x_ref[0] + mod_ref[0, 2:3, :] * out


def merge_project(o_nsa, o_sb, o_ml, y3, w_branch, w_out, x, mod, tm=512):
    b, s, d = x.shape
    W = MIX_WIDTH
    ospec = pl.BlockSpec((1, tm, W), lambda i, j: (i, j, 0))
    xspec = pl.BlockSpec((1, tm, d), lambda i, j: (i, j, 0))
    gspecs = [pl.BlockSpec((1, tm, d), functools.partial(lambda i, j, c: (i, j, c), c=C_MERGE // d + c))
              for c in range(N_BRANCH)]
    return pl.pallas_call(
        _merge_kernel,
        out_shape=jax.ShapeDtypeStruct((b, s, d), F32),
        grid=(b, s // tm),
        in_specs=[ospec, ospec, ospec] + gspecs + [
            pl.BlockSpec((N_BRANCH, W, d), lambda i, j: (0, 0, 0)),
            pl.BlockSpec((d, d), lambda i, j: (0, 0)),
            xspec,
            pl.BlockSpec((1, 6, d), lambda i, j: (i, 0, 0))],
        out_specs=xspec,
        compiler_params=_params("parallel", "parallel"),
        name="merge_project",
    )(o_nsa, o_sb, o_ml, y3, y3, y3, w_branch.astype(BF16), w_out.astype(BF16), x, mod)


def _ffn_kernel(x_ref, g_ref, mod_ref, wg_ref, wu_ref, wd_ref, o_ref, h_ref, acc_ref):
    f = pl.program_id(2)

    @pl.when(f == 0)
    def _():
        h_ref[...] = _norm_mod(x_ref[0], g_ref[...], mod_ref[0], 3, 4).astype(BF16)
        acc_ref[...] = jnp.zeros_like(acc_ref)

    h = h_ref[...]
    a = jnp.dot(h, wg_ref[...], preferred_element_type=F32)
    u = jnp.dot(h, wu_ref[...], preferred_element_type=F32)
    act = (a * jax.nn.sigmoid(a) * u).astype(BF16)
    acc_ref[...] += jnp.dot(act, wd_ref[...], preferred_element_type=F32)

    @pl.when(f == pl.num_programs(2) - 1)
    def _():
        o_ref[0] = x_ref[0] + mod_ref[0, 5:6, :] * acc_ref[...]


def dense_ffn(x, g, mod, wg, wu, wd, tm=512, n_ftiles=2):
    b, s, d = x.shape
    ff = wg.shape[1]
    tf = -(-ff // (n_ftiles * LANES)) * LANES
    pad = n_ftiles * tf - ff
    wg = jnp.pad(to_bf16(wg), ((0, 0), (0, pad)))
    wu = jnp.pad(to_bf16(wu), ((0, 0), (0, pad)))
    wd = jnp.pad(to_bf16(wd), ((0, pad), (0, 0)))
    xspec = pl.BlockSpec((1, tm, d), lambda i, j, f: (i, j, 0))
    return pl.pallas_call(
        _ffn_kernel,
        out_shape=jax.ShapeDtypeStruct((b, s, d), F32),
        grid=(b, s // tm, n_ftiles),
        in_specs=[xspec,
                  pl.BlockSpec((1, d), lambda i, j, f: (0, 0)),
                  pl.BlockSpec((1, 6, d), lambda i, j, f: (i, 0, 0)),
                  pl.BlockSpec((d, tf), lambda i, j, f: (0, f)),
                  pl.BlockSpec((d, tf), lambda i, j, f: (0, f)),
                  pl.BlockSpec((tf, d), lambda i, j, f: (f, 0))],
        out_specs=xspec,
        scratch_shapes=[pltpu.VMEM((tm, d), BF16), pltpu.VMEM((tm, d), F32)],
        compiler_params=_params("parallel", "parallel", "arbitrary"),
        name="dense_ffn",
    )(x, g.reshape(1, d), mod, wg, wu, wd)


def _router_kernel(x_ref, g_ref, mod_ref, wr_ref, h_ref, e_ref, p_ref):
    h = _norm_mod(x_ref[0], g_ref[...], mod_ref[0], 3, 4)
    h_ref[...] = h
    lane = _iota((1, LANES), 1)
    real = lane < N_EXPERTS
    logits = jnp.where(real, jnp.dot(h, wr_ref[...], precision=HIGHEST, preferred_element_type=F32), NEG)
    e = jnp.exp(logits - jnp.max(logits, axis=1, keepdims=True))
    p = jnp.where(real, e / jnp.sum(e, axis=1, keepdims=True), -1.0)
    p1 = jnp.max(p, axis=1, keepdims=True)
    i1 = jnp.min(jnp.where(p == p1, lane, LANES), axis=1, keepdims=True)
    rest = jnp.where(lane == i1, -1.0, p)
    p2 = jnp.max(rest, axis=1, keepdims=True)
    i2 = jnp.min(jnp.where(rest == p2, lane, LANES), axis=1, keepdims=True)
    tot = p1 + p2
    e_ref[...] = jnp.where(lane == 0, i1, jnp.where(lane == 1, i2, 0))[:, 0:N_EXPERTS]
    p_ref[...] = jnp.where(lane == 0, p1 / tot, jnp.where(lane == 1, p2 / tot, 0.0))[:, 0:N_EXPERTS]


def moe_router(x, g, mod, w_router, tm=512):
    b, s, d = x.shape
    t = b * s
    spb = s // tm
    wr = jnp.pad(w_router, ((0, 0), (0, LANES - N_EXPERTS)))
    return pl.pallas_call(
        _router_kernel,
        out_shape=(jax.ShapeDtypeStruct((t, d), F32),
                   jax.ShapeDtypeStruct((t, N_EXPERTS), I32),
                   jax.ShapeDtypeStruct((t, N_EXPERTS), F32)),
        grid=(b, spb),
        in_specs=[pl.BlockSpec((1, tm, d), lambda i, j: (i, j, 0)),
                  pl.BlockSpec((1, d), lambda i, j: (0, 0)),
                  pl.BlockSpec((1, 6, d), lambda i, j: (i, 0, 0)),
                  pl.BlockSpec((d, LANES), lambda i, j: (0, 0))],
        out_specs=(pl.BlockSpec((tm, d), lambda i, j: (i * spb + j, 0)),
                   pl.BlockSpec((tm, N_EXPERTS), lambda i, j: (i * spb + j, 0)),
                   pl.BlockSpec((tm, N_EXPERTS), lambda i, j: (i * spb + j, 0))),
        compiler_params=_params("parallel", "parallel"),
        name="moe_router",
    )(x, g.reshape(1, d), mod, wr)


def _dispatch_kernel(dest_ref, h_ref, zero_hbm, xpad_hbm, sem, *, td):
    del zero_hbm
    base = pl.program_id(0) * td * TOP_K

    def row_copy(a):
        return pltpu.make_async_copy(h_ref.at[a // TOP_K], xpad_hbm.at[dest_ref[base + a]], sem)

    def issue(a, c):
        row_copy(a).start()
        return c

    def drain(a, c):
        row_copy(a).wait()
        return c

    lax.fori_loop(0, td * TOP_K, issue, 0)
    lax.fori_loop(0, td * TOP_K, drain, 0)


def moe_dispatch(h, dest, n_rows, td=256):
    t, d = h.shape
    return pl.pallas_call(
        functools.partial(_dispatch_kernel, td=td),
        out_shape=jax.ShapeDtypeStruct((n_rows, d), h.dtype),
        grid_spec=pltpu.PrefetchScalarGridSpec(
            num_scalar_prefetch=1,
            grid=(t // td,),
            in_specs=[pl.BlockSpec((td, d), lambda i, dr: (i, 0)), pl.BlockSpec(memory_space=pl.ANY)],
            out_specs=pl.BlockSpec(memory_space=pl.ANY),
            scratch_shapes=[pltpu.SemaphoreType.DMA(())]),
        input_output_aliases={2: 0},
        compiler_params=pltpu.CompilerParams(dimension_semantics=("arbitrary",), has_side_effects=True),
        name="moe_dispatch",
    )(dest, h, jnp.zeros((n_rows, d), h.dtype))


def _expert_kernel(be_ref, nu_ref, x_ref, wg_ref, wu_ref, wd_ref, o_ref, xb_ref, acc_ref):
    i = pl.program_id(0)
    f = pl.program_id(1)
    used = i < nu_ref[0]

    @pl.when(f == 0)
    def _():
        xb_ref[...] = x_ref[...].astype(BF16)
        acc_ref[...] = jnp.zeros_like(acc_ref)

    @pl.when(used)
    def _():
        xb = xb_ref[...]
        a = jnp.dot(xb, wg_ref[0], preferred_element_type=F32)
        u = jnp.dot(xb, wu_ref[0], preferred_element_type=F32)
        act = (a * jax.nn.sigmoid(a) * u).astype(BF16)
        acc_ref[...] += jnp.dot(act, wd_ref[0], preferred_element_type=F32)

    @pl.when(f == pl.num_programs(1) - 1)
    def _():
        o_ref[...] = acc_ref[...]


def moe_experts(x_pad, blk_expert, n_used, wg, wu, wd, tb, tf=896):
    p, d = x_pad.shape
    ff = wg.shape[2]
    return pl.pallas_call(
        _expert_kernel,
        out_shape=jax.ShapeDtypeStruct((p, d), F32),
        grid_spec=pltpu.PrefetchScalarGridSpec(
            num_scalar_prefetch=2,
            grid=(p // tb, ff // tf),
            in_specs=[pl.BlockSpec((tb, d), lambda i, f, be, nu: (i, 0)),
                      pl.BlockSpec((1, d, tf), lambda i, f, be, nu: (be[i], 0, f)),
                      pl.BlockSpec((1, d, tf), lambda i, f, be, nu: (be[i], 0, f)),
                      pl.BlockSpec((1, tf, d), lambda i, f, be, nu: (be[i], f, 0))],
            out_specs=pl.BlockSpec((tb, d), lambda i, f, be, nu: (i, 0)),
            scratch_shapes=[pltpu.VMEM((tb, d), BF16), pltpu.VMEM((tb, d), F32)]),
        compiler_params=_params("parallel", "arbitrary"),
        name="moe_experts",
    )(blk_expert, n_used, x_pad, wg, wu, wd)


def _combine_kernel(dest_ref, y_hbm, x_ref, p_ref, mod_ref, o_ref, buf0, buf1, sem, *, td, spb):
    tok0 = (pl.program_id(0) * spb + pl.program_id(1)) * td

    def row_copies(r):
        a = (tok0 + r) * TOP_K
        return (pltpu.make_async_copy(y_hbm.at[dest_ref[a]], buf0.at[r], sem),
                pltpu.make_async_copy(y_hbm.at[dest_ref[a + 1]], buf1.at[r], sem))

    def issue(r, c):
        c0, c1 = row_copies(r)
        c0.start()
        c1.start()
        return c

    def drain(r, c):
        c0, c1 = row_copies(r)
        c0.wait()
        c1.wait()
        return c

    lax.fori_loop(0, td, issue, 0)
    lax.fori_loop(0, td, drain, 0)
    w = p_ref[...]
    f = w[:, 0:1] * buf0[...] + w[:, 1:2] * buf1[...]
    o_ref[0] = x_ref[0] + mod_ref[0, 5:6, :] * f


def moe_combine(y, dest, x, top_p, mod, td=256):
    b, s, d = x.shape
    spb = s // td
    return pl.pallas_call(
        functools.partial(_combine_kernel, td=td, spb=spb),
        out_shape=jax.ShapeDtypeStruct((b, s, d), F32),
        grid_spec=pltpu.PrefetchScalarGridSpec(
            num_scalar_prefetch=1,
            grid=(b, spb),
            in_specs=[pl.BlockSpec(memory_space=pl.ANY),
                      pl.BlockSpec((1, td, d), lambda i, j, dr: (i, j, 0)),
                      pl.BlockSpec((td, N_EXPERTS), lambda i, j, dr: (i * spb + j, 0)),
                      pl.BlockSpec((1, 6, d), lambda i, j, dr: (i, 0, 0))],
            out_specs=pl.BlockSpec((1, td, d), lambda i, j, dr: (i, j, 0)),
            scratch_shapes=[pltpu.VMEM((td, d), F32), pltpu.VMEM((td, d), F32),
                            pltpu.SemaphoreType.DMA(())]),
        compiler_params=_params("arbitrary", "arbitrary"),
        name="moe_combine",
    )(dest, y, x, top_p, mod)


def moe_ffn(x, g, mod, w_router, wg, wu, wd, tb=512):
    b, s, d = x.shape
    t = b * s
    a = t * TOP_K
    h, top_e, top_p = moe_router(x, g, mod, w_router)
    e_flat = top_e[:, 0:TOP_K].reshape(a)
    onehot = (e_flat[:, None] == jnp.arange(N_EXPERTS, dtype=I32)[None, :]).astype(I32)
    csum = jnp.cumsum(onehot, axis=0)
    rank = jnp.sum(onehot * csum, axis=1) - 1
    counts = csum[-1]
    padded = (counts + tb - 1) // tb * tb
    pad_ends = jnp.cumsum(padded)
    pad_starts = pad_ends - padded
    dest = (jnp.sum(onehot * pad_starts[None, :], axis=1) + rank).astype(I32)
    n_rows = (a // tb + N_EXPERTS + 1) * tb
    n_blk = n_rows // tb
    blk_expert = jnp.minimum(
        jnp.searchsorted(pad_ends, jnp.arange(n_blk, dtype=I32) * tb, side="right"), N_EXPERTS - 1).astype(I32)
    n_used = (pad_ends[-1:] // tb).astype(I32)
    pad_lo = jnp.concatenate([pad_starts + counts, pad_ends[-1:]]).astype(I32)
    pad_hi = jnp.concatenate([pad_ends, jnp.full((1,), n_rows, I32)]).astype(I32)
    slot_assign = moe_invert(dest, pad_lo, pad_hi, n_rows, tb)
    y2 = moe_experts_fused(h, slot_assign, blk_expert, n_used, to_bf16(wg), to_bf16(wu), to_bf16(wd), tb)
    return moe_mix(y2, x, top_p, mod)


def _invert_kernel(dest_ref, lo_ref, hi_ref, sa_ref, *, n_assign, tb, n_chunks):
    phase = pl.program_id(0)
    chunk = pl.program_id(1)

    @pl.when((phase == 0) & (chunk < lo_ref.shape[0]))
    def _():
        def fill(p, c):
            sa_ref[p] = n_assign + (p & (2 * tb - 1))
            return c

        lax.fori_loop(lo_ref[chunk], hi_ref[chunk], fill, 0)

    @pl.when(phase == 1)
    def _():
        per = n_assign // n_chunks

        def put(j, c):
            a = chunk * per + j
            sa_ref[dest_ref[a]] = a
            return c

        lax.fori_loop(0, per, put, 0, unroll=8)


def moe_invert(dest, pad_lo, pad_hi, n_rows, tb, n_chunks=16):
    n_assign = dest.shape[0]
    assert tb & (tb - 1) == 0 and n_assign % n_chunks == 0 and pad_lo.shape[0] <= n_chunks
    smem = pl.BlockSpec(memory_space=pltpu.SMEM)
    return pl.pallas_call(
        functools.partial(_invert_kernel, n_assign=n_assign, tb=tb, n_chunks=n_chunks),
        out_shape=jax.ShapeDtypeStruct((n_rows,), I32),
        grid=(2, n_chunks),
        in_specs=[smem, smem, smem],
        out_specs=smem,
        compiler_params=pltpu.CompilerParams(dimension_semantics=("arbitrary", "arbitrary")),
        name="moe_invert",
    )(dest, pad_lo, pad_hi)


def _expert_fused_kernel(be_ref, nu_ref, sa_ref, h_hbm, wg_ref, wu_ref, wd_ref, y_hbm,
                         xin_ref, yout_ref, xb_ref, acc_ref, sem_in, sem_out, *, tb, n_tok, n_f):
    i = pl.program_id(0)
    f = pl.program_id(1)
    n_used = nu_ref[0]
    n_assign = n_tok * TOP_K
    rows_f = tb // n_f
    active = i <= n_used

    def gather_row(blk, r):
        a = sa_ref[blk * tb + r]
        tok = jnp.where(a < n_assign, a >> 1, 0)
        return pltpu.make_async_copy(h_hbm.at[tok], xin_ref.at[blk % 2, r], sem_in.at[blk % 2])

    def scatter_row(blk, r):
        a = jnp.where(blk >= 0, sa_ref[jnp.maximum(blk, 0) * tb + r], n_assign + tb + r)
        row = jnp.where(a < n_assign, (a & 1) * n_tok + (a >> 1), a)
        return pltpu.make_async_copy(yout_ref.at[(blk + 2) % 2, r], y_hbm.at[row], sem_out.at[(blk + 2) % 2])

    def for_rows(fn):
        def body(r, c):
            fn(r)
            return c
        lax.fori_loop(0, tb, body, 0, unroll=8)

    @pl.when(f == 0)
    def _():
        @pl.when(i == 0)
        def _():
            yout_ref[1] = jnp.zeros((tb, yout_ref.shape[2]), F32)
            for_rows(lambda r: gather_row(0, r).start())

            def clear_row(r):
                return pltpu.make_async_copy(yout_ref.at[1, r], y_hbm.at[n_assign + r], sem_out.at[0])

            for_rows(lambda r: clear_row(r).start())
            for_rows(lambda r: clear_row(r).wait())

        @pl.when((i == 0) | (i - 1 <= n_used))
        def _():
            for_rows(lambda r: gather_row(i, r).wait())

        @pl.when(active)
        def _():
            xb_ref[...] = xin_ref[i % 2].astype(BF16)
            acc_ref[...] = jnp.zeros_like(acc_ref)

    @pl.when(active)
    def _():
        for r in range(rows_f):
            gather_row(i + 1, f * rows_f + r).start()
            scatter_row(i - 1, f * rows_f + r).start()
        xb = xb_ref[...]
        a = jnp.dot(xb, wg_ref[0], preferred_element_type=F32)
        u = jnp.dot(xb, wu_ref[0], preferred_element_type=F32)
        act = (a * jax.nn.sigmoid(a) * u).astype(BF16)
        acc_ref[...] += jnp.dot(act, wd_ref[0], preferred_element_type=F32)

    @pl.when(f == n_f - 1)
    def _():
        @pl.when((i >= 1) & (i - 1 <= n_used))
        def _():
            for_rows(lambda r: scatter_row(i - 2, r).wait())

        @pl.when(active)
        def _():
            yout_ref[i % 2] = acc_ref[...]


def moe_experts_fused(h, slot_assign, blk_expert, n_used, wg, wu, wd, tb, tf=896):
    n_tok, d = h.shape
    p = slot_assign.shape[0]
    ff = wg.shape[2]
    return pl.pallas_call(
        functools.partial(_expert_fused_kernel, tb=tb, n_tok=n_tok, n_f=ff // tf),
        out_shape=jax.ShapeDtypeStruct((n_tok * TOP_K + 2 * tb, d), F32),
        grid_spec=pltpu.PrefetchScalarGridSpec(
            num_scalar_prefetch=3,
            grid=(p // tb, ff // tf),
            in_specs=[pl.BlockSpec(memory_space=pl.ANY),
                      pl.BlockSpec((1, d, tf), lambda i, f, be, nu, sa: (be[i], 0, f)),
                      pl.BlockSpec((1, d, tf), lambda i, f, be, nu, sa: (be[i], 0, f)),
                      pl.BlockSpec((1, tf, d), lambda i, f, be, nu, sa: (be[i], f, 0))],
            out_specs=pl.BlockSpec(memory_space=pl.ANY),
            scratch_shapes=[pltpu.VMEM((2, tb, d), F32), pltpu.VMEM((2, tb, d), F32),
                            pltpu.VMEM((tb, d), BF16), pltpu.VMEM((tb, d), F32),
                            pltpu.SemaphoreType.DMA((2,)), pltpu.SemaphoreType.DMA((2,))]),
        compiler_params=pltpu.CompilerParams(dimension_semantics=("arbitrary", "arbitrary"),
                                             vmem_limit_bytes=VMEM_LIMIT, has_side_effects=True),
        name="moe_experts",
    )(blk_expert, n_used, slot_assign, h, wg, wu, wd)


def _mix_kernel(y0_ref, y1_ref, x_ref, p_ref, mod_ref, o_ref):
    w = p_ref[...]
    f = w[:, 0:1] * y0_ref[...] + w[:, 1:2] * y1_ref[...]
    o_ref[0] = x_ref[0] + mod_ref[0, 5:6, :] * f


def moe_mix(y2, x, top_p, mod, td=512):
    b, s, d = x.shape
    spb = s // td
    nt = b * spb
    return pl.pallas_call(
        _mix_kernel,
        out_shape=jax.ShapeDtypeStruct((b, s, d), F32),
        grid=(b, spb),
        in_specs=[pl.BlockSpec((td, d), lambda i, j: (i * spb + j, 0)),
                  pl.BlockSpec((td, d), lambda i, j: (nt + i * spb + j, 0)),
                  pl.BlockSpec((1, td, d), lambda i, j: (i, j, 0)),
                  pl.BlockSpec((td, N_EXPERTS), lambda i, j: (i * spb + j, 0)),
                  pl.BlockSpec((1, 6, d), lambda i, j: (i, 0, 0))],
        out_specs=pl.BlockSpec((1, td, d), lambda i, j: (i, j, 0)),
        compiler_params=_params("parallel", "parallel"),
        name="moe_mix",
    )(y2, y2, x, top_p, mod)


def _pack_w_in(w_in):
    kv = 2 * NSA_KV_HEADS * HEAD_DIM * 3
    w_in = to_bf16(w_in)
    o = 0
    nsa_q = w_in[:, o:o + MIX_WIDTH]; o += MIX_WIDTH
    nsa_kv = w_in[:, o:o + kv]; o += kv
    nsa_gate = w_in[:, o:o + NSA_HEADS * N_BRANCH]; o += NSA_HEADS * N_BRANCH
    sb = w_in[:, o:o + 3 * MIX_WIDTH]; o += 3 * MIX_WIDTH
    ml_qkv = w_in[:, o:o + 3 * MIX_WIDTH]; o += 3 * MIX_WIDTH
    ml_if = w_in[:, o:o + 2 * ML_HEADS]; o += 2 * ML_HEADS
    ml_o = w_in[:, o:o + MIX_WIDTH]; o += MIX_WIDTH
    merge = w_in[:, o:]
    main = jnp.concatenate([merge, nsa_q, ml_qkv, ml_o, sb, nsa_kv], axis=1)
    small = jnp.concatenate([nsa_gate, ml_if], axis=1)
    small = jnp.pad(small, ((0, 0), (0, N_SMALL - small.shape[1])))
    return main, small


def token_mixer_layer(x, mod, norm_g, w_in, nsa_q_norm, nsa_k_norm, cmp_pos, cmp_w1, cmp_b1, cmp_w2,
                      cmp_b2, ml_conv_w, ml_conv_b, ml_gate_b, w_branch, w_out):
    w_main, w_small = _pack_w_in(w_in)
    y3, small3 = in_projection(x, norm_g, mod, w_main, w_small)
    o_sb = sb_attention(y3)
    o_ml = mlstm(y3, small3, ml_conv_w, ml_conv_b, ml_gate_b)
    kc, vc = nsa_compress(y3, cmp_pos, cmp_w1, cmp_b1, cmp_w2, cmp_b2, nsa_k_norm[0])
    o_nsa = nsa_attention(y3, small3, nsa_q_norm, nsa_k_norm, kc, vc)
    return merge_project(o_nsa, o_sb, o_ml, y3, w_branch, w_out, x, mod)


def kernel(x, c, ada_w, ada_b, norm_mix, norm_ffn, w_in, nsa_q_norm, nsa_k_norm, cmp_pos, cmp_w1, cmp_b1,
           cmp_w2, cmp_b2, ml_conv_w, ml_conv_b, ml_gate_b, w_branch, w_out, ffn_wg, ffn_wu, ffn_wd,
           moe_router, moe_wg, moe_wu, moe_wd):
    depth = ada_w.shape[0]
    b, s, d = x.shape
    mods = adaln(c, ada_w, ada_b).reshape(depth, b, 6, d)
    for layer in range(depth):
        mod = mods[layer]
        x = token_mixer_layer(x, mod, norm_mix[layer], w_in[layer], nsa_q_norm[layer], nsa_k_norm[layer],
                              cmp_pos[layer], cmp_w1[layer], cmp_b1[layer], cmp_w2[layer], cmp_b2[layer],
                              ml_conv_w[layer], ml_conv_b[layer], ml_gate_b[layer], w_branch[layer],
                              w_out[layer])
        j = layer // 2
        if layer % 2 == 0:
            x = dense_ffn(x, norm_ffn[layer], mod, ffn_wg[j], ffn_wu[j], ffn_wd[j])
        else:
            x = moe_ffn(x, norm_ffn[layer], mod, moe_router[j], moe_wg[j], moe_wu[j], moe_wd[j])
    return x
```

```python
import functools

import numpy as np
import jax
import jax.numpy as jnp
from jax import lax
from jax.experimental import pallas as pl
from jax.experimental.pallas import tpu as pltpu

F32 = jnp.float32
BF16 = jnp.bfloat16
I32 = jnp.int32
HIGHEST = lax.Precision.HIGHEST

EPS = 1e-6
NEG = -1e30
HEAD_DIM = 64
MIX_WIDTH = 512
NSA_HEADS = 8
NSA_KV_HEADS = 2
NSA_GROUP = NSA_HEADS // NSA_KV_HEADS
CMP_BLOCK = 32
CMP_STRIDE = 16
SEL_BLOCK = 64
SEL_TOPK = 16
WINDOW = 512
FORCE_BONUS = 1e4
ML_HEADS = 4
ML_HEAD_DIM = 128
ML_CHUNK = 64
ML_UNROLL = 2
CONV_WIDTH = 4
N_BRANCH = 3
N_EXPERTS = 8
TOP_K = 2
LANES = 128

C_MERGE = 0
C_NSA_Q = 3072
C_ML_Q = 3584
C_ML_K = 4096
C_ML_V = 4608
C_ML_O = 5120
C_SB_Q = 5632
C_SB_K = 6144
C_SB_V = 6656
C_NSA_KV = 7168
N_MAIN = 7936
S_NSA_GATE = 0
S_ML_I = 24
S_ML_F = 28
N_SMALL = 128

VMEM_LIMIT = 56 * 1024 * 1024


def _params(*sem):
    return pltpu.CompilerParams(dimension_semantics=sem, vmem_limit_bytes=VMEM_LIMIT)


def _iota(shape, dim):
    return lax.broadcasted_iota(I32, shape, dim)


def _split_dot(a32, b_bf16):
    hi = a32.astype(BF16)
    lo = (a32 - hi.astype(F32)).astype(BF16)
    return (jnp.dot(hi, b_bf16, preferred_element_type=F32)
            + jnp.dot(lo, b_bf16, preferred_element_type=F32))


def _split_dot_left(a_bf16, b32):
    hi = b32.astype(BF16)
    lo = (b32 - hi.astype(F32)).astype(BF16)
    return (jnp.dot(a_bf16, hi, preferred_element_type=F32)
            + jnp.dot(a_bf16, lo, preferred_element_type=F32))


def _dot_nt(a, b):
    return lax.dot_general(a, b, (((1,), (1,)), ((), ())), preferred_element_type=F32)


def _log_sigmoid(z):
    return jnp.minimum(z, 0.0) - jnp.log1p(jnp.exp(-jnp.abs(z)))


def _cast_kernel(x_ref, o_ref):
    o_ref[...] = x_ref[...].astype(o_ref.dtype)


def to_bf16(w, max_rows=512):
    cols = w.shape[-1]
    w2 = w.reshape(-1, cols)
    rows = w2.shape[0]
    tr = max(t for t in range(8, max_rows + 1, 8) if rows % t == 0)
    out = pl.pallas_call(
        _cast_kernel,
        out_shape=jax.ShapeDtypeStruct((rows, cols), BF16),
        grid=(rows // tr,),
        in_specs=[pl.BlockSpec((tr, cols), lambda i: (i, 0))],
        out_specs=pl.BlockSpec((tr, cols), lambda i: (i, 0)),
        compiler_params=_params("parallel"),
        name="to_bf16",
    )(w2)
    return out.reshape(w.shape)


def _adaln_kernel(c_ref, w_ref, b_ref, o_ref):
    c = c_ref[...]
    cond = c * jax.nn.sigmoid(c)
    o_ref[0] = jnp.dot(cond, w_ref[0], precision=HIGHEST, preferred_element_type=F32) + b_ref[0]


def adaln(c, ada_w, ada_b):
    depth, d, n = ada_w.shape
    b = c.shape[0]
    tn = 1536
    return pl.pallas_call(
        _adaln_kernel,
        out_shape=jax.ShapeDtypeStruct((depth, b, n), F32),
        grid=(depth, n // tn),
        in_specs=[pl.BlockSpec((b, d), lambda l, j: (0, 0)),
                  pl.BlockSpec((1, d, tn), lambda l, j: (l, 0, j)),
                  pl.BlockSpec((1, 1, tn), lambda l, j: (l, 0, j))],
        out_specs=pl.BlockSpec((1, b, tn), lambda l, j: (l, 0, j)),
        compiler_params=_params("parallel", "parallel"),
        name="adaln",
    )(c, ada_w, ada_b.reshape(depth, 1, n))


def _norm_mod(x, g, mod, shift_row, scale_row):
    ms = jnp.mean(x * x, axis=-1, keepdims=True)
    y = x * lax.rsqrt(ms + EPS) * g
    return y * (1.0 + mod[scale_row:scale_row + 1, :]) + mod[shift_row:shift_row + 1, :]


def _in_proj_kernel(x_ref, g_ref, mod_ref, wm_ref, ws_ref, y_ref, sm_ref, h_ref):
    @pl.when(pl.program_id(2) == 0)
    def _():
        h_ref[...] = _norm_mod(x_ref[0], g_ref[...], mod_ref[0], 0, 1).astype(BF16)
        sm_ref[0] = jnp.dot(h_ref[...], ws_ref[...], preferred_element_type=F32)

    y_ref[0] = jnp.dot(h_ref[...], wm_ref[...], preferred_element_type=F32).astype(y_ref.dtype)


def in_projection(x, g, mod, w_main, w_small, tm=512, n_tiles=2):
    b, s, d = x.shape
    tn = N_MAIN // n_tiles
    return pl.pallas_call(
        _in_proj_kernel,
        out_shape=(jax.ShapeDtypeStruct((b, s, N_MAIN), BF16), jax.ShapeDtypeStruct((b, s, N_SMALL), F32)),
        grid=(b, s // tm, n_tiles),
        in_specs=[pl.BlockSpec((1, tm, d), lambda i, j, n: (i, j, 0)),
                  pl.BlockSpec((1, d), lambda i, j, n: (0, 0)),
                  pl.BlockSpec((1, 6, d), lambda i, j, n: (i, 0, 0)),
                  pl.BlockSpec((d, tn), lambda i, j, n: (0, n)),
                  pl.BlockSpec((d, N_SMALL), lambda i, j, n: (0, 0))],
        out_specs=(pl.BlockSpec((1, tm, tn), lambda i, j, n: (i, j, n)),
                   pl.BlockSpec((1, tm, N_SMALL), lambda i, j, n: (i, j, 0))),
        scratch_shapes=[pltpu.VMEM((tm, d), BF16)],
        compiler_params=_params("parallel", "parallel", "arbitrary"),
        name="in_projection",
    )(x, g.reshape(1, d), mod, w_main, w_small)


SB_EXP_FLOOR = -104.0
SB_EAGER = 2


def _sb_kernel(q_ref, k_ref, v_ref, o_ref, vt_ref, *, tq, n_str):
    TK, dh = LANES, HEAD_DIM
    n_sub = tq // TK
    W = 2 * tq
    SR = range(n_str)
    qi = pl.program_id(2)

    @pl.when(qi == 0)
    def _():
        for c in range(v_ref.shape[1] // tq):
            v_t = v_ref[0, c * tq:(c + 1) * tq, :].astype(F32).T.astype(BF16)
            for p in SR:
                vt_ref[p, c] = v_t[p * LANES:(p + 1) * LANES]

    q_t = (q_ref[0].astype(F32) * (dh ** -0.5)).T
    chan = _iota((2 * dh, tq), 0)
    q_cat = []
    for p in SR:
        q_p = q_t[p * LANES:(p + 1) * LANES]
        q_cat.append(jnp.concatenate([jnp.where(chan < dh, q_p, 0.0), jnp.where(chan < dh, 0.0, q_p)],
                                     axis=1).astype(BF16))
    later = (_iota((TK, TK), 0) < _iota((TK, TK), 1)).astype(BF16)
    suffix = jnp.concatenate([jnp.concatenate([later, later], axis=1), jnp.ones((8, 2 * TK), BF16)], axis=0)

    def steps(blocks, st):
        work = [(p, b) for p in SR for b in range(len(blocks))]
        lss, his, los = {}, {}, {}
        for p, b in work:
            j, keep = blocks[b]
            k0 = pl.multiple_of(j * tq, tq)
            z = jnp.dot(k_ref[0, pl.ds(k0, tq), p * LANES:(p + 1) * LANES], q_cat[p],
                        preferred_element_type=F32)
            lk = -(jnp.maximum(z, 0.0) + jnp.log(1.0 + jnp.exp(-jnp.abs(z))))
            lss[p, b] = lk + z
            if keep is not None:
                lk = jnp.where(keep, lk, 0.0)
            his[p, b] = lk.astype(BF16)
            los[p, b] = (lk - his[p, b].astype(F32)).astype(BF16)
        carry = [st[p][0] for p in SR]
        afters = {}
        for p, b in work:
            after = [None] * n_sub
            for sub in range(n_sub - 1, -1, -1):
                rows = slice(sub * TK, (sub + 1) * TK)
                res = jnp.dot(suffix, jnp.concatenate([his[p, b][rows], los[p, b][rows]], axis=0),
                              preferred_element_type=F32)
                after[sub] = res[0:TK] + carry[p]
                carry[p] = carry[p] + res[TK:TK + 1]
            afters[p, b] = jnp.concatenate(after, axis=0)
        acc = [st[p][1] for p in SR]
        for p, b in work:
            j, keep = blocks[b]
            a = jnp.exp(lss[p, b] + afters[p, b])
            if keep is not None:
                a = jnp.where(keep, a, 0.0)
            acc[p] = acc[p] + jnp.dot(vt_ref[p, j], a.astype(BF16), preferred_element_type=F32)
        return [(carry[p], acc[p]) for p in SR]

    def cond(c):
        live = jnp.max(c[1][0][0])
        for p in range(1, n_str):
            live = jnp.maximum(live, jnp.max(c[1][p][0]))
        return (c[0] >= 0) & (live > SB_EXP_FLOOR)

    def body(c):
        return c[0] - 1, steps([(c[0], None)], c[1])

    strict = _iota((tq, W), 0) < (_iota((tq, W), 1) & (tq - 1))
    eager = [(jnp.maximum(qi - d, 0), qi >= d) for d in range(1, SB_EAGER + 1)]
    st = steps([(qi, strict)] + eager, [(jnp.zeros((1, W), F32), jnp.zeros((2 * dh, W), F32)) for p in SR])
    _, st = lax.while_loop(cond, body, (qi - 1 - SB_EAGER, st))
    out_rows = []
    for p in SR:
        out_rows += [st[p][1][0:dh, 0:tq], st[p][1][dh:2 * dh, tq:W]]
    o_ref[0] = jnp.concatenate(out_rows, axis=0).T.astype(o_ref.dtype)


def sb_attention(y3, tq=256, n_str=2):
    b, s, _ = y3.shape
    wide = n_str * LANES
    qb, kb, vb = C_SB_Q // wide, C_SB_K // wide, C_SB_V // wide
    return pl.pallas_call(
        functools.partial(_sb_kernel, tq=tq, n_str=n_str),
        out_shape=jax.ShapeDtypeStruct((b, s, MIX_WIDTH), BF16),
        grid=(b, MIX_WIDTH // wide, s // tq),
        in_specs=[pl.BlockSpec((1, tq, wide), lambda i, p, j: (i, j, qb + p)),
                  pl.BlockSpec((1, s, wide), lambda i, p, j: (i, 0, kb + p)),
                  pl.BlockSpec((1, s, wide), lambda i, p, j: (i, 0, vb + p))],
        out_specs=pl.BlockSpec((1, tq, wide), lambda i, p, j: (i, j, p)),
        scratch_shapes=[pltpu.VMEM((n_str, s // tq, LANES, tq), BF16)],
        compiler_params=_params("parallel", "parallel", "arbitrary"),
        name="sb_attention",
    )(y3, y3, y3)


def _mlstm_kernel(q_ref, k_ref, v_ref, og_ref, sm_ref, gr_ref, cw_ref, cb_ref, gb_ref, out_ref,
                  ct_ref, n_ref, m_ref, xbuf_ref, qk_ref, *, ts):
    L, dh, H, W = ML_CHUNK, ML_HEAD_DIM, ML_HEADS, MIX_WIDTH
    halo = 8
    sblk = pl.program_id(1)

    @pl.when(sblk == 0)
    def _():
        ct_ref[...] = jnp.zeros_like(ct_ref)
        n_ref[...] = jnp.zeros_like(n_ref)
        m_ref[...] = jnp.zeros_like(m_ref)
        xbuf_ref[0:halo, :] = jnp.zeros((halo, 2 * W), F32)

    @pl.when(sblk > 0)
    def _():
        xbuf_ref[0:halo, :] = xbuf_ref[ts:ts + halo, :]

    xbuf_ref[halo:halo + ts, 0:W] = q_ref[0].astype(F32)
    xbuf_ref[halo:halo + ts, W:2 * W] = k_ref[0].astype(F32)
    conv = cb_ref[...] + jnp.zeros((ts, 2 * W), F32)
    for j in range(CONV_WIDTH):
        off = halo - (CONV_WIDTH - 1) + j
        conv = conv + cw_ref[j:j + 1, :] * xbuf_ref[off:off + ts, :]
    act = conv * jax.nn.sigmoid(conv)
    qk_ref[:, 0:W] = (act[:, 0:W] * (dh ** -0.5)).astype(BF16)
    qk_ref[:, W:2 * W] = act[:, W:2 * W].astype(BF16)

    it0, it1 = _iota((L, L), 0), _iota((L, L), 1)
    causal = it0 >= it1
    tri_lo = causal.astype(BF16)
    tri_up = (it0 <= it1).astype(BF16)

    def chunks(it, carry):
        HR = range(H)
        UR = range(ML_UNROLL)
        UH = [(u, h) for u in UR for h in HR]
        cols = [slice(h * dh, (h + 1) * dh) for h in HR]
        rows = [pl.ds(pl.multiple_of((it * ML_UNROLL + u) * L, L), L) for u in UR]
        sm = [sm_ref[0, rows[u], :] for u in UR]
        gr = [gr_ref[0, it * ML_UNROLL + u] for u in UR]
        ig_col = {(u, h): sm[u][:, S_ML_I + h:S_ML_I + h + 1] + gb_ref[0, h] for u, h in UH}
        lf_col = {(u, h): _log_sigmoid(sm[u][:, S_ML_F + h:S_ML_F + h + 1] + gb_ref[1, h]) for u, h in UH}
        ig_row = {(u, h): gr[u][h:h + 1, :] + gb_ref[0, h] for u, h in UH}
        lf_row = {(u, h): _log_sigmoid(gr[u][H + h:H + h + 1, :] + gb_ref[1, h]) for u, h in UH}
        b_t = {k: _split_dot_left(tri_lo, jnp.broadcast_to(lf_col[k], (L, L))) for k in UH}
        b_s = {k: _split_dot(jnp.broadcast_to(lf_row[k], (L, L)), tri_up) for k in UH}
        qq = {(u, h): qk_ref[rows[u], cols[h]] for u, h in UH}
        kk = {(u, h): qk_ref[rows[u], W + h * dh:W + (h + 1) * dh] for u, h in UH}
        vv = {(u, h): v_ref[0, rows[u], cols[h]] for u, h in UH}
        qk = {k: _dot_nt(qq[k], kk[k]) for k in UH}
        kt = {k: kk[k].astype(F32).T.astype(BF16) for k in UH}
        dmat = {k: jnp.where(causal, b_t[k] - b_s[k] + ig_row[k], NEG) for k in UH}
        d_max = {k: jnp.max(dmat[k], axis=1, keepdims=True) for k in UH}
        b_col = {k: b_t[k][:, 0:1] for k in UH}
        b_last = {k: b_t[k][L - 1:L, 0:1] for k in UH}
        decay = {k: b_last[k] - b_col[k] + ig_col[k] for k in UH}
        decay_max = {k: jnp.max(decay[k], axis=0, keepdims=True) for k in UH}
        m_prev, m_new = {}, {}
        for u, h in UH:
            m_prev[u, h] = m_ref[h][:, 0:1] if u == 0 else m_new[u - 1, h]
            m_new[u, h] = jnp.maximum(b_last[u, h] + m_prev[u, h], decay_max[u, h])
        m_inter = {k: b_col[k] + m_prev[k] for k in UH}
        m_t = {k: jnp.maximum(m_inter[k], d_max[k]) for k in UH}
        w = {k: jnp.exp(dmat[k] - m_t[k]) * qk[k] for k in UH}
        inter = {k: jnp.exp(m_inter[k] - m_t[k]) for k in UH}
        w_v = {k: jnp.dot(w[k].astype(BF16), vv[k], preferred_element_type=F32) for k in UH}
        ws = {k: jnp.exp(decay[k] - m_new[k]) for k in UH}
        cscale = {k: jnp.exp(b_last[k] + m_prev[k] - m_new[k]) for k in UH}
        wv = {k: (ws[k] * vv[k].astype(F32)).astype(BF16) for k in UH}
        k_wv = {k: jnp.dot(kt[k], wv[k], preferred_element_type=F32) for k in UH}
        k_ws = {k: jnp.sum(ws[k] * kk[k].astype(F32), axis=0, keepdims=True) for k in UH}
        w_sum = {k: jnp.sum(w[k], axis=1, keepdims=True) for k in UH}
        ct = {h: ct_ref[h] for h in HR}
        nvec = {h: n_ref[h] for h in HR}
        for u, h in UH:
            k = (u, h)
            num = inter[k] * jnp.dot(qq[k], ct[h].astype(BF16), preferred_element_type=F32) + w_v[k]
            den = inter[k] * jnp.sum(qq[k].astype(F32) * nvec[h], axis=1, keepdims=True) + w_sum[k]
            hval = num / jnp.maximum(jnp.abs(den), jnp.exp(-m_t[k]))
            ct[h] = cscale[k] * ct[h] + k_wv[k]
            nvec[h] = cscale[k] * nvec[h] + k_ws[k]
            gate = jax.nn.sigmoid(og_ref[0, rows[u], cols[h]].astype(F32))
            out_ref[0, rows[u], cols[h]] = (gate * hval).astype(out_ref.dtype)
        for h in HR:
            ct_ref[h] = ct[h]
            n_ref[h] = nvec[h]
            m_ref[h] = jnp.broadcast_to(m_new[ML_UNROLL - 1, h], (1, LANES))
        return carry

    lax.fori_loop(0, ts // (L * ML_UNROLL), chunks, 0)


def mlstm(y3, small3, conv_w, conv_b, gate_b, ts=512):
    b, s, _ = y3.shape
    W, H, L = MIX_WIDTH, ML_HEADS, ML_CHUNK
    gr = small3[:, :, S_ML_I:S_ML_I + 2 * H].reshape(b, s // L, L, 2 * H).transpose(0, 1, 3, 2)
    cq, ck, cv, co = C_ML_Q // W, C_ML_K // W, C_ML_V // W, C_ML_O // W
    return pl.pallas_call(
        functools.partial(_mlstm_kernel, ts=ts),
        out_shape=jax.ShapeDtypeStruct((b, s, W), BF16),
        grid=(b, s // ts),
        in_specs=[pl.BlockSpec((1, ts, W), lambda i, j: (i, j, cq)),
                  pl.BlockSpec((1, ts, W), lambda i, j: (i, j, ck)),
                  pl.BlockSpec((1, ts, W), lambda i, j: (i, j, cv)),
                  pl.BlockSpec((1, ts, W), lambda i, j: (i, j, co)),
                  pl.BlockSpec((1, ts, N_SMALL), lambda i, j: (i, j, 0)),
                  pl.BlockSpec((1, ts // L, 2 * H, L), lambda i, j: (i, j, 0, 0)),
                  pl.BlockSpec((CONV_WIDTH, 2 * W), lambda i, j: (0, 0)),
                  pl.BlockSpec((1, 2 * W), lambda i, j: (0, 0)),
                  pl.BlockSpec(memory_space=pltpu.SMEM)],
        out_specs=pl.BlockSpec((1, ts, W), lambda i, j: (i, j, 0)),
        scratch_shapes=[pltpu.VMEM((H, ML_HEAD_DIM, ML_HEAD_DIM), F32),
                        pltpu.VMEM((H, 1, ML_HEAD_DIM), F32),
                        pltpu.VMEM((H, 1, LANES), F32),
                        pltpu.VMEM((ts + 8, 2 * W), F32),
                        pltpu.VMEM((ts, 2 * W), BF16)],
        compiler_params=_params("parallel", "arbitrary"),
        name="mlstm",
    )(y3, y3, y3, y3, small3, gr, conv_w, conv_b.reshape(1, 2 * W), gate_b)


def _gelu_tanh(x):
    return 0.5 * x * (1.0 + jnp.tanh(0.7978845608028654 * (x + 0.044715 * (x * x * x))))


def _compress_kernel(ra_ref, rb_ref, pos_ref, w1_ref, b1_ref, w2_ref, b2_ref, kn_ref, kc_ref, vc_ref):
    half = (CMP_BLOCK // 2) * HEAD_DIM
    for j, o_ref in enumerate((kc_ref, vc_ref)):
        xa = (ra_ref[j, 0, 0].astype(F32) + pos_ref[j, :, 0:half]).astype(BF16)
        xb = (rb_ref[j, 0, 0].astype(F32) + pos_ref[j, :, half:2 * half]).astype(BF16)
        hid = (jnp.dot(xa, w1_ref[j, 0:half, :], preferred_element_type=F32)
               + jnp.dot(xb, w1_ref[j, half:2 * half, :], preferred_element_type=F32) + b1_ref[j])
        out = jnp.dot(_gelu_tanh(hid).astype(BF16), w2_ref[j], preferred_element_type=F32) + b2_ref[j]
        if j == 0:
            out = out * lax.rsqrt(jnp.mean(out * out, axis=-1, keepdims=True) + EPS) * kn_ref[...]
        o_ref[0, 0] = out


def nsa_compress(y3, cmp_pos, cmp_w1, cmp_b1, cmp_w2, cmp_b2, k_norm0):
    b, s, _ = y3.shape
    G, dh = NSA_KV_HEADS, HEAD_DIM
    nr = s // CMP_STRIDE
    wide = CMP_STRIDE * dh
    kv = y3[:, :, C_NSA_KV:C_NSA_KV + 2 * G * dh].reshape(b, s, 2, G, dh)
    ra = kv.transpose(2, 0, 3, 1, 4).reshape(2, b, G, nr, wide)
    rb = jnp.concatenate([ra[:, :, :, 1:], jnp.zeros((2, b, G, 1, wide), ra.dtype)], axis=3)
    hidden = cmp_w1.shape[-1]
    out = jax.ShapeDtypeStruct((b, G, nr, dh), F32)
    blk = pl.BlockSpec((2, 1, 1, nr, wide), lambda i, g: (0, i, g, 0, 0))
    oblk = pl.BlockSpec((1, 1, nr, dh), lambda i, g: (i, g, 0, 0))

    def full(shape):
        return pl.BlockSpec(shape, lambda i, g: (0,) * len(shape))

    return pl.pallas_call(
        _compress_kernel,
        out_shape=(out, out),
        grid=(b, G),
        in_specs=[blk, blk, full((2, 1, 2 * wide)), full((2, 2 * wide, hidden)), full((2, 1, hidden)),
                  full((2, hidden, dh)), full((2, 1, dh)), full((1, dh))],
        out_specs=(oblk, oblk),
        compiler_params=_params("parallel", "parallel"),
        name="nsa_compress",
    )(ra, rb, cmp_pos.reshape(2, 1, 2 * wide), cmp_w1.astype(BF16), cmp_b1.reshape(2, 1, hidden),
      cmp_w2.astype(BF16), cmp_b2.reshape(2, 1, dh), k_norm0.reshape(1, dh))


NSA_QB = 128
NSA_KS = 512
NSA_CK = 256
NSA_VR = 80
A_FEAT, A_PEN, A_BIAS = 0, 64, 128
A_DUMMY = A_BIAS + 4


def _head_rms(x, gain):
    w = x.shape[1]
    same_head = (_iota((w, w), 0) // HEAD_DIM == _iota((w, w), 1) // HEAD_DIM).astype(BF16)
    ss = _split_dot(x * x, same_head)
    return x * lax.rsqrt(ss * (1.0 / HEAD_DIM) + EPS) * gain


def _nsa_kernel(q_ref, gl_ref, kca_ref, vcT_ref, ks_ref, kw_ref, vs_ref, vw_ref, kconst_ref, gq_ref, gk_ref,
                     o_ref, qt_ref, ksa_ref, kwa_ref, vsa_ref, vwa_ref, s_ref, *, n_sel, top):
    QB, R, dh, CK, VR, KS = NSA_QB, NSA_GROUP, HEAD_DIM, NSA_CK, NSA_VR, NSA_KS
    G = NSA_KV_HEADS
    GR = range(G)
    HQ = R * QB
    NPAD = WINDOW // QB
    SUB = KS // QB
    qi = pl.program_id(1)
    q0 = qi * QB
    nkb = vsa_ref.shape[1]

    @pl.when(qi == 0)
    def _():
        pad_keys = jnp.where(_iota((WINDOW, CK), 1) == A_DUMMY, 1.0, 0.0).astype(BF16)
        ones_rows = jnp.where(_iota((nkb + NPAD, VR - dh, QB), 1) == 0, 1.0, 0.0).astype(BF16)
        for g in GR:
            heads = slice(g * dh, (g + 1) * dh)
            ksa_ref[g] = kconst_ref[...]
            kwa_ref[g, 0:WINDOW, :] = pad_keys
            kwa_ref[g, WINDOW:, :] = kconst_ref[...]
            kwa_ref[g, WINDOW:, A_PEN:A_PEN + 64] = jnp.zeros((kwa_ref.shape[1] - WINDOW, 64), BF16)
            vwa_ref[g, 0:NPAD, 0:dh, :] = jnp.zeros((NPAD, dh, QB), BF16)
            vsa_ref[g, :, dh:VR, :] = ones_rows[0:nkb]
            vwa_ref[g, :, dh:VR, :] = ones_rows
        for c in range(ks_ref.shape[1] // KS):
            keys = slice(c * KS, (c + 1) * KS)
            ks_n = _head_rms(ks_ref[0, keys, :].astype(F32), gk_ref[0]).astype(BF16)
            kw_n = _head_rms(kw_ref[0, keys, :].astype(F32), gk_ref[1]).astype(BF16)
            for g in GR:
                heads = slice(g * dh, (g + 1) * dh)
                ksa_ref[g, keys, A_FEAT:A_FEAT + dh] = ks_n[:, heads]
                kwa_ref[g, WINDOW + c * KS:WINDOW + (c + 1) * KS, A_FEAT:A_FEAT + dh] = kw_n[:, heads]
        for c in range(nkb):
            keys = slice(c * QB, (c + 1) * QB)
            vs_t = vs_ref[0, keys, :].astype(F32).T.astype(BF16)
            vw_t = vw_ref[0, keys, :].astype(F32).T.astype(BF16)
            for g in GR:
                heads = slice(g * dh, (g + 1) * dh)
                vsa_ref[g, c, 0:dh, :] = vs_t[heads]
                vwa_ref[g, NPAD + c, 0:dh, :] = vw_t[heads]
        qt_ref[:, A_BIAS + 16:CK, :] = jnp.zeros((G, CK - A_BIAS - 16, HQ), BF16)

    q_n = _head_rms(q_ref[0].astype(F32), gq_ref[...]).astype(BF16)
    q_rows = (q_n.astype(F32) * (dh ** -0.5)).T
    lane = _iota((16, HQ), 1)
    rowi = _iota((16, HQ), 0)
    t_q = q0 + (lane & (QB - 1))
    t_hi = ((t_q >> 6) << 6).astype(F32)
    t_lo = (t_q & 63).astype(F32)
    for g in GR:
        qT = jnp.concatenate([q_rows[(g * R + r) * dh:(g * R + r + 1) * dh] for r in range(R)],
                             axis=1).astype(BF16)
        qt_ref[g, A_FEAT:A_FEAT + dh, :] = qT
        qt_ref[g, A_PEN:A_PEN + dh, :] = qT
        slope = jnp.exp2(-(g * R + (lane >> 7) + 1).astype(F32))
        bias_rows = jnp.where(rowi < 2, slope,
                              jnp.where(rowi == 2, -slope * t_hi,
                                        jnp.where(rowi == 3, -slope * t_lo,
                                                  jnp.where(rowi == A_DUMMY - A_BIAS, NEG, 0.0))))
        qt_ref[g, A_BIAS:A_BIAS + 16, :] = bias_rows.astype(BF16)
    k_loc = _iota((QB, HQ), 0)
    q_loc = _iota((QB, HQ), 1) & (QB - 1)

    def pv(v_ref_, g, kb0, pr):
        out = None
        for i in range(pr.shape[0] // QB):
            term = jnp.dot(v_ref_[g, kb0 + i], pr[i * QB:(i + 1) * QB], preferred_element_type=F32)
            out = term if out is None else out + term
        return out

    n_cmp = kca_ref.shape[2]
    cmp_end = _iota((n_cmp, HQ), 0) * CMP_STRIDE + (CMP_BLOCK - 1)
    valid = cmp_end <= q0 + (_iota((n_cmp, HQ), 1) & (QB - 1))
    sc = [jnp.where(valid, jnp.dot(kca_ref[0, g], qt_ref[g], preferred_element_type=F32), NEG) for g in GR]
    e = [jnp.exp(sc[g] - jnp.max(sc[g], axis=0, keepdims=True)) for g in GR]
    p = [jnp.where(valid, e[g] * (1.0 / jnp.sum(e[g], axis=0, keepdims=True)), 0.0) for g in GR]
    o_cmp = [jnp.dot(vcT_ref[0, g], p[g].astype(BF16), preferred_element_type=F32) for g in GR]

    win_rows = pl.ds(pl.multiple_of(q0, QB), WINDOW + QB)
    sw = [jnp.dot(kwa_ref[g, win_rows, :], qt_ref[g], preferred_element_type=F32) for g in GR]
    sw = [jnp.concatenate([jnp.where(k_loc > q_loc, sw[g][0:QB], NEG), sw[g][QB:WINDOW],
                           jnp.where(k_loc <= q_loc, sw[g][WINDOW:WINDOW + QB], NEG)], axis=0) for g in GR]
    pw = [jnp.exp(sw[g] - jnp.max(sw[g], axis=0, keepdims=True)).astype(BF16) for g in GR]
    acc_w = [pv(vwa_ref, g, qi, pw[g]) for g in GR]
    o_win = [acc_w[g][0:dh] / acc_w[g][dh:dh + 1] for g in GR]

    c0 = _iota((n_sel, n_cmp), 1) * CMP_STRIDE
    s0 = _iota((n_sel, n_cmp), 0) * SEL_BLOCK
    overlap_t = ((c0 < s0 + SEL_BLOCK) & (c0 + CMP_BLOCK > s0)).astype(BF16)
    j_idx = _iota((n_sel, QB), 0)
    tq = q0 + _iota((n_sel, QB), 1)
    cur = tq >> 6
    forced = (j_idx == 0) | (j_idx == cur) | (j_idx == cur - 1)
    causal_blk = j_idx * SEL_BLOCK <= tq
    p_grp = [p[g][:, 0:QB] + p[g][:, QB:2 * QB] + p[g][:, 2 * QB:3 * QB] + p[g][:, 3 * QB:4 * QB] for g in GR]
    imp = [jnp.where(causal_blk, _split_dot_left(overlap_t, p_grp[g]) + jnp.where(forced, FORCE_BONUS, 0.0), -1.0)
           for g in GR]
    sel = [jnp.zeros((n_sel, QB), F32) for g in GR]
    for _ in range(top):
        for g in GR:
            mx = jnp.max(imp[g], axis=0, keepdims=True)
            first = jnp.min(jnp.where(imp[g] == mx, j_idx, n_sel), axis=0, keepdims=True)
            pick = j_idx == first
            sel[g] = jnp.where(pick, 1.0, sel[g])
            imp[g] = jnp.where(pick, -3e38, imp[g])
    for g in GR:
        pen = jnp.where((sel[g] > 0.5) & causal_blk, 0.0, NEG)
        if n_sel < 64:
            pen = jnp.concatenate([pen, jnp.zeros((64 - n_sel, QB), F32)], axis=0)
        qt_ref[g, A_PEN:A_PEN + 64, :] = jnp.concatenate([pen] * R, axis=1).astype(BF16)

    def score(j):
        rows = pl.ds(pl.multiple_of(j * KS, KS), KS)
        return [jnp.dot(ksa_ref[g, rows, :], qt_ref[g], preferred_element_type=F32) for g in GR]

    def absorb(s, j, st):
        m_new = [jnp.maximum(st[g][0], jnp.max(s[g], axis=0, keepdims=True)) for g in GR]
        pr = [jnp.exp(s[g] - m_new[g]).astype(BF16) for g in GR]
        return [(m_new[g], jnp.exp(st[g][0] - m_new[g]) * st[g][1] + pv(vsa_ref, g, j * SUB, pr[g])) for g in GR]

    n_full = qi // SUB
    init = [(jnp.full((1, HQ), NEG, F32), jnp.zeros((VR, HQ), F32)) for g in GR]
    st = lax.fori_loop(0, n_full, lambda j, st_: absorb(score(j), j, st_), init)
    s_last = score(n_full)
    diag = pl.ds(pl.multiple_of(q0 - n_full * KS, QB), QB)
    for g in GR:
        s_ref[g] = s_last[g]
        s_ref[g, diag, :] = jnp.where(k_loc <= q_loc, s_ref[g, diag, :], NEG)
    st = absorb([s_ref[g] for g in GR], n_full, st)
    o_sel = [st[g][1][0:dh] / st[g][1][dh:dh + 1] for g in GR]

    gl_t = gl_ref[0].T
    rows_out = []
    for g in GR:
        gate = [jax.nn.sigmoid(jnp.concatenate(
            [gl_t[(g * R + r) * N_BRANCH + br:(g * R + r) * N_BRANCH + br + 1] for r in range(R)], axis=1))
            for br in range(N_BRANCH)]
        o_t = gate[0] * o_cmp[g] + gate[1] * o_sel[g] + gate[2] * o_win[g]
        rows_out += [o_t[:, r * QB:(r + 1) * QB] for r in range(R)]
    o_ref[0] = jnp.concatenate(rows_out, axis=0).T.astype(o_ref.dtype)


def nsa_attention(y3, small3, q_norm, k_norm, kc, vc):
    b, s, _ = y3.shape
    G, R, dh, QB, CK, VR = NSA_KV_HEADS, NSA_GROUP, HEAD_DIM, NSA_QB, NSA_CK, NSA_VR
    assert G == 2, "the kernel picks a kv head's gate rows with a two-way select"
    HQ = R * QB
    nq = s // QB
    n_sel = s // SEL_BLOCK
    assert n_sel <= 64, "selection one-hot columns hold at most 64 blocks"
    top = min(SEL_TOPK, n_sel)
    n_cmp = kc.shape[2]

    def pos_cols(pos):
        return np.stack([pos // 64 * 64, pos % 64, np.ones_like(pos), np.ones_like(pos)], axis=1)

    vc_t = vc.transpose(0, 1, 3, 2).astype(BF16)

    pos = np.arange(s)
    kconst = np.zeros((s, CK), np.float32)
    kconst[pos, A_PEN + pos // SEL_BLOCK] = 1.0
    kconst[:, A_BIAS:A_BIAS + 4] = pos_cols(pos)
    kconst = jnp.asarray(kconst, BF16)

    kc_hi = kc.astype(BF16)
    kc_lo = (kc - kc_hi.astype(F32)).astype(BF16)
    cend = np.arange(n_cmp) * CMP_STRIDE + (CMP_BLOCK - 1)
    cbias = np.zeros((n_cmp, CK - 2 * dh), np.float32)
    cbias[:, 0:4] = pos_cols(cend)
    kc_aug = jnp.concatenate([kc_hi, kc_lo, jnp.broadcast_to(jnp.asarray(cbias, BF16), (b, G, n_cmp, CK - 2 * dh))],
                             axis=-1)

    qw = G * R * dh
    kvb = C_NSA_KV // LANES
    gq = jnp.tile(q_norm, G * R).reshape(1, qw)
    gk = jnp.stack([jnp.tile(k_norm[1], G), jnp.tile(k_norm[2], G)]).reshape(2, 1, LANES)

    def kv_spec(blk):
        return pl.BlockSpec((1, s, LANES), lambda i, j: (i, 0, kvb + blk))

    return pl.pallas_call(
        functools.partial(_nsa_kernel, n_sel=n_sel, top=top),
        out_shape=jax.ShapeDtypeStruct((b, s, MIX_WIDTH), BF16),
        grid=(b, nq),
        in_specs=[pl.BlockSpec((1, QB, qw), lambda i, j: (i, j, C_NSA_Q // qw)),
                  pl.BlockSpec((1, QB, N_SMALL), lambda i, j: (i, j, 0)),
                  pl.BlockSpec((1, G, n_cmp, CK), lambda i, j: (i, 0, 0, 0)),
                  pl.BlockSpec((1, G, dh, n_cmp), lambda i, j: (i, 0, 0, 0)),
                  kv_spec(2), kv_spec(4), kv_spec(3), kv_spec(5),
                  pl.BlockSpec((s, CK), lambda i, j: (0, 0)),
                  pl.BlockSpec((1, qw), lambda i, j: (0, 0)),
                  pl.BlockSpec((2, 1, LANES), lambda i, j: (0, 0, 0))],
        out_specs=pl.BlockSpec((1, QB, qw), lambda i, j: (i, j, 0)),
        scratch_shapes=[pltpu.VMEM((G, CK, HQ), BF16),
                        pltpu.VMEM((G, s, CK), BF16), pltpu.VMEM((G, s + WINDOW, CK), BF16),
                        pltpu.VMEM((G, nq, VR, QB), BF16), pltpu.VMEM((G, nq + WINDOW // QB, VR, QB), BF16),
                        pltpu.VMEM((G, NSA_KS, HQ), F32)],
        compiler_params=_params("parallel", "arbitrary"),
        name="nsa_attention",
    )(y3, small3, kc_aug, vc_t, y3, y3, y3, y3, kconst, gq, gk)


def _merge_kernel(on_ref, os_ref, om_ref, g0_ref, g1_ref, g2_ref, wb_ref, wo_ref, x_ref, mod_ref, o_ref):
    merged = None
    for i, (o_r, g_r) in enumerate(((on_ref, g0_ref), (os_ref, g1_ref), (om_ref, g2_ref))):
        br = jnp.dot(o_r[0], wb_ref[i], preferred_element_type=F32)
        term = jax.nn.sigmoid(g_r[0].astype(F32)) * br
        merged = term if merged is None else merged + term
    out = jnp.dot(merged.astype(BF16), wo_ref[...], preferred_element_type=F32)
    o_ref[0] = x_ref[0] + mod_ref[0, 2:3, :] * out


def merge_project(o_nsa, o_sb, o_ml, y3, w_branch, w_out, x, mod, tm=512):
    b, s, d = x.shape
    W = MIX_WIDTH
    ospec = pl.BlockSpec((1, tm, W), lambda i, j: (i, j, 0))
    xspec = pl.BlockSpec((1, tm, d), lambda i, j: (i, j, 0))
    gspecs = [pl.BlockSpec((1, tm, d), functools.partial(lambda i, j, c: (i, j, c), c=C_MERGE // d + c))
              for c in range(N_BRANCH)]
    return pl.pallas_call(
        _merge_kernel,
        out_shape=jax.ShapeDtypeStruct((b, s, d), F32),
        grid=(b, s // tm),
        in_specs=[ospec, ospec, ospec] + gspecs + [
            pl.BlockSpec((N_BRANCH, W, d), lambda i, j: (0, 0, 0)),
            pl.BlockSpec((d, d), lambda i, j: (0, 0)),
            xspec,
            pl.BlockSpec((1, 6, d), lambda i, j: (i, 0, 0))],
        out_specs=xspec,
        compiler_params=_params("parallel", "parallel"),
        name="merge_project",
    )(o_nsa, o_sb, o_ml, y3, y3, y3, w_branch.astype(BF16), w_out.astype(BF16), x, mod)


def _ffn_kernel(x_ref, g_ref, mod_ref, wg_ref, wu_ref, wd_ref, o_ref, h_ref, acc_ref):
    f = pl.program_id(2)

    @pl.when(f == 0)
    def _():
        h_ref[...] = _norm_mod(x_ref[0], g_ref[...], mod_ref[0], 3, 4).astype(BF16)
        acc_ref[...] = jnp.zeros_like(acc_ref)

    h = h_ref[...]
    a = jnp.dot(h, wg_ref[...], preferred_element_type=F32)
    u = jnp.dot(h, wu_ref[...], preferred_element_type=F32)
    act = (a * jax.nn.sigmoid(a) * u).astype(BF16)
    acc_ref[...] += jnp.dot(act, wd_ref[...], preferred_element_type=F32)

    @pl.when(f == pl.num_programs(2) - 1)
    def _():
        o_ref[0] = x_ref[0] + mod_ref[0, 5:6, :] * acc_ref[...]


def dense_ffn(x, g, mod, wg, wu, wd, tm=512, n_ftiles=2):
    b, s, d = x.shape
    ff = wg.shape[1]
    tf = -(-ff // (n_ftiles * LANES)) * LANES
    pad = n_ftiles * tf - ff
    wg = jnp.pad(to_bf16(wg), ((0, 0), (0, pad)))
    wu = jnp.pad(to_bf16(wu), ((0, 0), (0, pad)))
    wd = jnp.pad(to_bf16(wd), ((0, pad), (0, 0)))
    xspec = pl.BlockSpec((1, tm, d), lambda i, j, f: (i, j, 0))
    return pl.pallas_call(
        _ffn_kernel,
        out_shape=jax.ShapeDtypeStruct((b, s, d), F32),
        grid=(b, s // tm, n_ftiles),
        in_specs=[xspec,
                  pl.BlockSpec((1, d), lambda i, j, f: (0, 0)),
                  pl.BlockSpec((1, 6, d), lambda i, j, f: (i, 0, 0)),
                  pl.BlockSpec((d, tf), lambda i, j, f: (0, f)),
                  pl.BlockSpec((d, tf), lambda i, j, f: (0, f)),
                  pl.BlockSpec((tf, d), lambda i, j, f: (f, 0))],
        out_specs=xspec,
        scratch_shapes=[pltpu.VMEM((tm, d), BF16), pltpu.VMEM((tm, d), F32)],
        compiler_params=_params("parallel", "parallel", "arbitrary"),
        name="dense_ffn",
    )(x, g.reshape(1, d), mod, wg, wu, wd)


def _router_kernel(x_ref, g_ref, mod_ref, wr_ref, h_ref, e_ref, p_ref):
    h = _norm_mod(x_ref[0], g_ref[...], mod_ref[0], 3, 4)
    h_ref[...] = h
    lane = _iota((1, LANES), 1)
    real = lane < N_EXPERTS
    logits = jnp.where(real, jnp.dot(h, wr_ref[...], precision=HIGHEST, preferred_element_type=F32), NEG)
    e = jnp.exp(logits - jnp.max(logits, axis=1, keepdims=True))
    p = jnp.where(real, e / jnp.sum(e, axis=1, keepdims=True), -1.0)
    p1 = jnp.max(p, axis=1, keepdims=True)
    i1 = jnp.min(jnp.where(p == p1, lane, LANES), axis=1, keepdims=True)
    rest = jnp.where(lane == i1, -1.0, p)
    p2 = jnp.max(rest, axis=1, keepdims=True)
    i2 = jnp.min(jnp.where(rest == p2, lane, LANES), axis=1, keepdims=True)
    tot = p1 + p2
    e_ref[...] = jnp.where(lane == 0, i1, jnp.where(lane == 1, i2, 0))[:, 0:N_EXPERTS]
    p_ref[...] = jnp.where(lane == 0, p1 / tot, jnp.where(lane == 1, p2 / tot, 0.0))[:, 0:N_EXPERTS]


def moe_router(x, g, mod, w_router, tm=512):
    b, s, d = x.shape
    t = b * s
    spb = s // tm
    wr = jnp.pad(w_router, ((0, 0), (0, LANES - N_EXPERTS)))
    return pl.pallas_call(
        _router_kernel,
        out_shape=(jax.ShapeDtypeStruct((t, d), F32),
                   jax.ShapeDtypeStruct((t, N_EXPERTS), I32),
                   jax.ShapeDtypeStruct((t, N_EXPERTS), F32)),
        grid=(b, spb),
        in_specs=[pl.BlockSpec((1, tm, d), lambda i, j: (i, j, 0)),
                  pl.BlockSpec((1, d), lambda i, j: (0, 0)),
                  pl.BlockSpec((1, 6, d), lambda i, j: (i, 0, 0)),
                  pl.BlockSpec((d, LANES), lambda i, j: (0, 0))],
        out_specs=(pl.BlockSpec((tm, d), lambda i, j: (i * spb + j, 0)),
                   pl.BlockSpec((tm, N_EXPERTS), lambda i, j: (i * spb + j, 0)),
                   pl.BlockSpec((tm, N_EXPERTS), lambda i, j: (i * spb + j, 0))),
        compiler_params=_params("parallel", "parallel"),
        name="moe_router",
    )(x, g.reshape(1, d), mod, wr)


def moe_ffn(x, g, mod, w_router, wg, wu, wd, tb=512):
    b, s, d = x.shape
    t = b * s
    a = t * TOP_K
    h, top_e, top_p = moe_router(x, g, mod, w_router)
    e_flat = top_e[:, 0:TOP_K].reshape(a)
    onehot = (e_flat[:, None] == jnp.arange(N_EXPERTS, dtype=I32)[None, :]).astype(I32)
    csum = jnp.cumsum(onehot, axis=0)
    rank = jnp.sum(onehot * csum, axis=1) - 1
    counts = csum[-1]
    padded = (counts + tb - 1) // tb * tb
    pad_ends = jnp.cumsum(padded)
    pad_starts = pad_ends - padded
    dest = (jnp.sum(onehot * pad_starts[None, :], axis=1) + rank).astype(I32)
    n_rows = (a // tb + N_EXPERTS + 1) * tb
    n_blk = n_rows // tb
    blk_expert = jnp.minimum(
        jnp.searchsorted(pad_ends, jnp.arange(n_blk, dtype=I32) * tb, side="right"), N_EXPERTS - 1).astype(I32)
    n_used = (pad_ends[-1:] // tb).astype(I32)
    pad_lo = jnp.concatenate([pad_starts + counts, pad_ends[-1:]]).astype(I32)
    pad_hi = jnp.concatenate([pad_ends, jnp.full((1,), n_rows, I32)]).astype(I32)
    slot_assign = moe_invert(dest, pad_lo, pad_hi, n_rows, tb)
    y2 = moe_experts(h, slot_assign, blk_expert, n_used, to_bf16(wg), to_bf16(wu), to_bf16(wd), tb)
    return moe_mix(y2, x, top_p, mod)


def _invert_kernel(dest_ref, lo_ref, hi_ref, sa_ref, *, n_assign, tb, n_chunks):
    phase = pl.program_id(0)
    chunk = pl.program_id(1)

    @pl.when((phase == 0) & (chunk < lo_ref.shape[0]))
    def _():
        def fill(p, c):
            sa_ref[p] = n_assign + (p & (2 * tb - 1))
            return c

        lax.fori_loop(lo_ref[chunk], hi_ref[chunk], fill, 0)

    @pl.when(phase == 1)
    def _():
        per = n_assign // n_chunks

        def put(j, c):
            a = chunk * per + j
            sa_ref[dest_ref[a]] = a
            return c

        lax.fori_loop(0, per, put, 0, unroll=8)


def moe_invert(dest, pad_lo, pad_hi, n_rows, tb, n_chunks=16):
    n_assign = dest.shape[0]
    assert tb & (tb - 1) == 0 and n_assign % n_chunks == 0 and pad_lo.shape[0] <= n_chunks
    smem = pl.BlockSpec(memory_space=pltpu.SMEM)
    return pl.pallas_call(
        functools.partial(_invert_kernel, n_assign=n_assign, tb=tb, n_chunks=n_chunks),
        out_shape=jax.ShapeDtypeStruct((n_rows,), I32),
        grid=(2, n_chunks),
        in_specs=[smem, smem, smem],
        out_specs=smem,
        compiler_params=pltpu.CompilerParams(dimension_semantics=("arbitrary", "arbitrary")),
        name="moe_invert",
    )(dest, pad_lo, pad_hi)


def _expert_kernel(be_ref, nu_ref, sa_ref, h_hbm, wg_ref, wu_ref, wd_ref, y_hbm,
                         xin_ref, yout_ref, xb_ref, acc_ref, sem_in, sem_out, *, tb, n_tok, n_f):
    i = pl.program_id(0)
    f = pl.program_id(1)
    n_used = nu_ref[0]
    n_assign = n_tok * TOP_K
    rows_f = tb // n_f
    active = i <= n_used

    def gather_row(blk, r):
        a = sa_ref[blk * tb + r]
        tok = jnp.where(a < n_assign, a >> 1, 0)
        return pltpu.make_async_copy(h_hbm.at[tok], xin_ref.at[blk % 2, r], sem_in.at[blk % 2])

    def scatter_row(blk, r):
        a = jnp.where(blk >= 0, sa_ref[jnp.maximum(blk, 0) * tb + r], n_assign + tb + r)
        row = jnp.where(a < n_assign, (a & 1) * n_tok + (a >> 1), a)
        return pltpu.make_async_copy(yout_ref.at[(blk + 2) % 2, r], y_hbm.at[row], sem_out.at[(blk + 2) % 2])

    def for_rows(fn):
        def body(r, c):
            fn(r)
            return c
        lax.fori_loop(0, tb, body, 0, unroll=8)

    @pl.when(f == 0)
    def _():
        @pl.when(i == 0)
        def _():
            yout_ref[1] = jnp.zeros((tb, yout_ref.shape[2]), F32)
            for_rows(lambda r: gather_row(0, r).start())

            def clear_row(r):
                return pltpu.make_async_copy(yout_ref.at[1, r], y_hbm.at[n_assign + r], sem_out.at[0])

            for_rows(lambda r: clear_row(r).start())
            for_rows(lambda r: clear_row(r).wait())

        @pl.when((i == 0) | (i - 1 <= n_used))
        def _():
            for_rows(lambda r: gather_row(i, r).wait())

        @pl.when(active)
        def _():
            xb_ref[...] = xin_ref[i % 2].astype(BF16)
            acc_ref[...] = jnp.zeros_like(acc_ref)

    @pl.when(active)
    def _():
        for r in range(rows_f):
            gather_row(i + 1, f * rows_f + r).start()
            scatter_row(i - 1, f * rows_f + r).start()
        xb = xb_ref[...]
        a = jnp.dot(xb, wg_ref[0], preferred_element_type=F32)
        u = jnp.dot(xb, wu_ref[0], preferred_element_type=F32)
        act = (a * jax.nn.sigmoid(a) * u).astype(BF16)
        acc_ref[...] += jnp.dot(act, wd_ref[0], preferred_element_type=F32)

    @pl.when(f == n_f - 1)
    def _():
        @pl.when((i >= 1) & (i - 1 <= n_used))
        def _():
            for_rows(lambda r: scatter_row(i - 2, r).wait())

        @pl.when(active)
        def _():
            yout_ref[i % 2] = acc_ref[...]


def moe_experts(h, slot_assign, blk_expert, n_used, wg, wu, wd, tb, tf=896):
    n_tok, d = h.shape
    p = slot_assign.shape[0]
    ff = wg.shape[2]
    return pl.pallas_call(
        functools.partial(_expert_kernel, tb=tb, n_tok=n_tok, n_f=ff // tf),
        out_shape=jax.ShapeDtypeStruct((n_tok * TOP_K + 2 * tb, d), F32),
        grid_spec=pltpu.PrefetchScalarGridSpec(
            num_scalar_prefetch=3,
            grid=(p // tb, ff // tf),
            in_specs=[pl.BlockSpec(memory_space=pl.ANY),
                      pl.BlockSpec((1, d, tf), lambda i, f, be, nu, sa: (be[i], 0, f)),
                      pl.BlockSpec((1, d, tf), lambda i, f, be, nu, sa: (be[i], 0, f)),
                      pl.BlockSpec((1, tf, d), lambda i, f, be, nu, sa: (be[i], f, 0))],
            out_specs=pl.BlockSpec(memory_space=pl.ANY),
            scratch_shapes=[pltpu.VMEM((2, tb, d), F32), pltpu.VMEM((2, tb, d), F32),
                            pltpu.VMEM((tb, d), BF16), pltpu.VMEM((tb, d), F32),
                            pltpu.SemaphoreType.DMA((2,)), pltpu.SemaphoreType.DMA((2,))]),
        compiler_params=pltpu.CompilerParams(dimension_semantics=("arbitrary", "arbitrary"),
                                             vmem_limit_bytes=VMEM_LIMIT, has_side_effects=True),
        name="moe_experts",
    )(blk_expert, n_used, slot_assign, h, wg, wu, wd)


def _mix_kernel(y0_ref, y1_ref, x_ref, p_ref, mod_ref, o_ref):
    w = p_ref[...]
    f = w[:, 0:1] * y0_ref[...] + w[:, 1:2] * y1_ref[...]
    o_ref[0] = x_ref[0] + mod_ref[0, 5:6, :] * f


def moe_mix(y2, x, top_p, mod, td=512):
    b, s, d = x.shape
    spb = s // td
    nt = b * spb
    return pl.pallas_call(
        _mix_kernel,
        out_shape=jax.ShapeDtypeStruct((b, s, d), F32),
        grid=(b, spb),
        in_specs=[pl.BlockSpec((td, d), lambda i, j: (i * spb + j, 0)),
                  pl.BlockSpec((td, d), lambda i, j: (nt + i * spb + j, 0)),
                  pl.BlockSpec((1, td, d), lambda i, j: (i, j, 0)),
                  pl.BlockSpec((td, N_EXPERTS), lambda i, j: (i * spb + j, 0)),
                  pl.BlockSpec((1, 6, d), lambda i, j: (i, 0, 0))],
        out_specs=pl.BlockSpec((1, td, d), lambda i, j: (i, j, 0)),
        compiler_params=_params("parallel", "parallel"),
        name="moe_mix",
    )(y2, y2, x, top_p, mod)


def _pack_w_in(w_in):
    kv = 2 * NSA_KV_HEADS * HEAD_DIM * 3
    w_in = to_bf16(w_in)
    o = 0
    nsa_q = w_in[:, o:o + MIX_WIDTH]; o += MIX_WIDTH
    nsa_kv = w_in[:, o:o + kv]; o += kv
    nsa_gate = w_in[:, o:o + NSA_HEADS * N_BRANCH]; o += NSA_HEADS * N_BRANCH
    sb = w_in[:, o:o + 3 * MIX_WIDTH]; o += 3 * MIX_WIDTH
    ml_qkv = w_in[:, o:o + 3 * MIX_WIDTH]; o += 3 * MIX_WIDTH
    ml_if = w_in[:, o:o + 2 * ML_HEADS]; o += 2 * ML_HEADS
    ml_o = w_in[:, o:o + MIX_WIDTH]; o += MIX_WIDTH
    merge = w_in[:, o:]
    main = jnp.concatenate([merge, nsa_q, ml_qkv, ml_o, sb, nsa_kv], axis=1)
    small = jnp.concatenate([nsa_gate, ml_if], axis=1)
    small = jnp.pad(small, ((0, 0), (0, N_SMALL - small.shape[1])))
    return main, small


def token_mixer_layer(x, mod, norm_g, w_in, nsa_q_norm, nsa_k_norm, cmp_pos, cmp_w1, cmp_b1, cmp_w2,
                      cmp_b2, ml_conv_w, ml_conv_b, ml_gate_b, w_branch, w_out):
    w_main, w_small = _pack_w_in(w_in)
    y3, small3 = in_projection(x, norm_g, mod, w_main, w_small)
    o_sb = sb_attention(y3)
    o_ml = mlstm(y3, small3, ml_conv_w, ml_conv_b, ml_gate_b)
    kc, vc = nsa_compress(y3, cmp_pos, cmp_w1, cmp_b1, cmp_w2, cmp_b2, nsa_k_norm[0])
    o_nsa = nsa_attention(y3, small3, nsa_q_norm, nsa_k_norm, kc, vc)
    return merge_project(o_nsa, o_sb, o_ml, y3, w_branch, w_out, x, mod)


def kernel(x, c, ada_w, ada_b, norm_mix, norm_ffn, w_in, nsa_q_norm, nsa_k_norm, cmp_pos, cmp_w1, cmp_b1,
           cmp_w2, cmp_b2, ml_conv_w, ml_conv_b, ml_gate_b, w_branch, w_out, ffn_wg, ffn_wu, ffn_wd,
           moe_router, moe_wg, moe_wu, moe_wd):
    depth = ada_w.shape[0]
    b, s, d = x.shape
    mods = adaln(c, ada_w, ada_b).reshape(depth, b, 6, d)
    for layer in range(depth):
        mod = mods[layer]
        x = token_mixer_layer(x, mod, norm_mix[layer], w_in[layer], nsa_q_norm[layer], nsa_k_norm[layer],
                              cmp_pos[layer], cmp_w1[layer], cmp_b1[layer], cmp_w2[layer], cmp_b2[layer],
                              ml_conv_w[layer], ml_conv_b[layer], ml_gate_b[layer], w_branch[layer],
                              w_out[layer])
        j = layer // 2
        if layer % 2 == 0:
            x = dense_ffn(x, norm_ffn[layer], mod, ffn_wg[j], ffn_wu[j], ffn_wd[j])
        else:
            x = moe_ffn(x, norm_ffn[layer], mod, moe_router[j], moe_wg[j], moe_wu[j], moe_wd[j])
    return x
```

```python
import functools

import numpy as np
import jax
import jax.numpy as jnp
from jax import lax
from jax.experimental import pallas as pl
from jax.experimental.pallas import tpu as pltpu

F32 = jnp.float32
BF16 = jnp.bfloat16
I32 = jnp.int32
HIGHEST = lax.Precision.HIGHEST

EPS = 1e-6
NEG = -1e30
HEAD_DIM = 64
MIX_WIDTH = 512
NSA_HEADS = 8
NSA_KV_HEADS = 2
NSA_GROUP = NSA_HEADS // NSA_KV_HEADS
CMP_BLOCK = 32
CMP_STRIDE = 16
SEL_BLOCK = 64
SEL_TOPK = 16
WINDOW = 512
FORCE_BONUS = 1e4
ML_HEADS = 4
ML_HEAD_DIM = 128
ML_CHUNK = 64
ML_UNROLL = 2
CONV_WIDTH = 4
N_BRANCH = 3
N_EXPERTS = 8
TOP_K = 2
LANES = 128

C_MERGE = 0
C_NSA_Q = 3072
C_ML_Q = 3584
C_ML_K = 4096
C_ML_V = 4608
C_ML_O = 5120
C_SB_Q = 5632
C_SB_K = 6144
C_SB_V = 6656
C_NSA_KV = 7168
N_MAIN = 7936
S_NSA_GATE = 0
S_ML_I = 24
S_ML_F = 28
N_SMALL = 128

VMEM_LIMIT = 56 * 1024 * 1024


def _params(*sem):
    return pltpu.CompilerParams(dimension_semantics=sem, vmem_limit_bytes=VMEM_LIMIT)


def _iota(shape, dim):
    return lax.broadcasted_iota(I32, shape, dim)


def _split_dot(a32, b_bf16):
    hi = a32.astype(BF16)
    lo = (a32 - hi.astype(F32)).astype(BF16)
    return (jnp.dot(hi, b_bf16, preferred_element_type=F32)
            + jnp.dot(lo, b_bf16, preferred_element_type=F32))


def _split_dot_left(a_bf16, b32):
    hi = b32.astype(BF16)
    lo = (b32 - hi.astype(F32)).astype(BF16)
    return (jnp.dot(a_bf16, hi, preferred_element_type=F32)
            + jnp.dot(a_bf16, lo, preferred_element_type=F32))


def _dot_nt(a, b):
    return lax.dot_general(a, b, (((1,), (1,)), ((), ())), preferred_element_type=F32)


def _log_sigmoid(z):
    return jnp.minimum(z, 0.0) - jnp.log1p(jnp.exp(-jnp.abs(z)))


def _cast_kernel(x_ref, o_ref):
    o_ref[...] = x_ref[...].astype(o_ref.dtype)


def to_bf16(w, max_rows=512):
    cols = w.shape[-1]
    w2 = w.reshape(-1, cols)
    rows = w2.shape[0]
    tr = max(t for t in range(8, max_rows + 1, 8) if rows % t == 0)
    out = pl.pallas_call(
        _cast_kernel,
        out_shape=jax.ShapeDtypeStruct((rows, cols), BF16),
        grid=(rows // tr,),
        in_specs=[pl.BlockSpec((tr, cols), lambda i: (i, 0))],
        out_specs=pl.BlockSpec((tr, cols), lambda i: (i, 0)),
        compiler_params=_params("parallel"),
        name="to_bf16",
    )(w2)
    return out.reshape(w.shape)


def _adaln_kernel(c_ref, w_ref, b_ref, o_ref):
    c = c_ref[...]
    cond = c * jax.nn.sigmoid(c)
    o_ref[0] = jnp.dot(cond, w_ref[0], precision=HIGHEST, preferred_element_type=F32) + b_ref[0]


def adaln(c, ada_w, ada_b):
    depth, d, n = ada_w.shape
    b = c.shape[0]
    tn = 1536
    return pl.pallas_call(
        _adaln_kernel,
        out_shape=jax.ShapeDtypeStruct((depth, b, n), F32),
        grid=(depth, n // tn),
        in_specs=[pl.BlockSpec((b, d), lambda l, j: (0, 0)),
                  pl.BlockSpec((1, d, tn), lambda l, j: (l, 0, j)),
                  pl.BlockSpec((1, 1, tn), lambda l, j: (l, 0, j))],
        out_specs=pl.BlockSpec((1, b, tn), lambda l, j: (l, 0, j)),
        compiler_params=_params("parallel", "parallel"),
        name="adaln",
    )(c, ada_w, ada_b.reshape(depth, 1, n))


def _norm_mod(x, g, mod, shift_row, scale_row):
    ms = jnp.mean(x * x, axis=-1, keepdims=True)
    y = x * lax.rsqrt(ms + EPS) * g
    return y * (1.0 + mod[scale_row:scale_row + 1, :]) + mod[shift_row:shift_row + 1, :]


def _in_proj_kernel(x_ref, g_ref, mod_ref, wm_ref, ws_ref, y_ref, sm_ref, h_ref):
    @pl.when(pl.program_id(2) == 0)
    def _():
        h_ref[...] = _norm_mod(x_ref[0], g_ref[...], mod_ref[0], 0, 1).astype(BF16)
        sm_ref[0] = jnp.dot(h_ref[...], ws_ref[...], preferred_element_type=F32)

    y_ref[0] = jnp.dot(h_ref[...], wm_ref[...], preferred_element_type=F32).astype(y_ref.dtype)


def in_projection(x, g, mod, w_main, w_small, tm=512, n_tiles=2):
    b, s, d = x.shape
    tn = N_MAIN // n_tiles
    return pl.pallas_call(
        _in_proj_kernel,
        out_shape=(jax.ShapeDtypeStruct((b, s, N_MAIN), BF16), jax.ShapeDtypeStruct((b, s, N_SMALL), F32)),
        grid=(b, s // tm, n_tiles),
        in_specs=[pl.BlockSpec((1, tm, d), lambda i, j, n: (i, j, 0)),
                  pl.BlockSpec((1, d), lambda i, j, n: (0, 0)),
                  pl.BlockSpec((1, 6, d), lambda i, j, n: (i, 0, 0)),
                  pl.BlockSpec((d, tn), lambda i, j, n: (0, n)),
                  pl.BlockSpec((d, N_SMALL), lambda i, j, n: (0, 0))],
        out_specs=(pl.BlockSpec((1, tm, tn), lambda i, j, n: (i, j, n)),
                   pl.BlockSpec((1, tm, N_SMALL), lambda i, j, n: (i, j, 0))),
        scratch_shapes=[pltpu.VMEM((tm, d), BF16)],
        compiler_params=_params("parallel", "parallel", "arbitrary"),
        name="in_projection",
    )(x, g.reshape(1, d), mod, w_main, w_small)


SB_EXP_FLOOR = -104.0
SB_EAGER = 2


def _sb_kernel(q_ref, k_ref, v_ref, o_ref, vt_ref, *, tq, n_str):
    TK, dh = LANES, HEAD_DIM
    n_sub = tq // TK
    W = 2 * tq
    SR = range(n_str)
    qi = pl.program_id(2)

    @pl.when(qi == 0)
    def _():
        for c in range(v_ref.shape[1] // tq):
            v_t = v_ref[0, c * tq:(c + 1) * tq, :].astype(F32).T.astype(BF16)
            for p in SR:
                vt_ref[p, c] = v_t[p * LANES:(p + 1) * LANES]

    q_t = (q_ref[0].astype(F32) * (dh ** -0.5)).T
    chan = _iota((2 * dh, tq), 0)
    q_cat = []
    for p in SR:
        q_p = q_t[p * LANES:(p + 1) * LANES]
        q_cat.append(jnp.concatenate([jnp.where(chan < dh, q_p, 0.0), jnp.where(chan < dh, 0.0, q_p)],
                                     axis=1).astype(BF16))
    later = (_iota((TK, TK), 0) < _iota((TK, TK), 1)).astype(BF16)
    suffix = jnp.concatenate([jnp.concatenate([later, later], axis=1), jnp.ones((8, 2 * TK), BF16)], axis=0)

    def steps(blocks, st):
        work = [(p, b) for p in SR for b in range(len(blocks))]
        lss, his, los = {}, {}, {}
        for p, b in work:
            j, keep = blocks[b]
            k0 = pl.multiple_of(j * tq, tq)
            z = jnp.dot(k_ref[0, pl.ds(k0, tq), p * LANES:(p + 1) * LANES], q_cat[p],
                        preferred_element_type=F32)
            lk = -(jnp.maximum(z, 0.0) + jnp.log(1.0 + jnp.exp(-jnp.abs(z))))
            lss[p, b] = lk + z
            if keep is not None:
                lk = jnp.where(keep, lk, 0.0)
            his[p, b] = lk.astype(BF16)
            los[p, b] = (lk - his[p, b].astype(F32)).astype(BF16)
        carry = [st[p][0] for p in SR]
        afters = {}
        for p, b in work:
            after = [None] * n_sub
            for sub in range(n_sub - 1, -1, -1):
                rows = slice(sub * TK, (sub + 1) * TK)
                res = jnp.dot(suffix, jnp.concatenate([his[p, b][rows], los[p, b][rows]], axis=0),
                              preferred_element_type=F32)
                after[sub] = res[0:TK] + carry[p]
                carry[p] = carry[p] + res[TK:TK + 1]
            afters[p, b] = jnp.concatenate(after, axis=0)
        acc = [st[p][1] for p in SR]
        for p, b in work:
            j, keep = blocks[b]
            a = jnp.exp(lss[p, b] + afters[p, b])
            if keep is not None:
                a = jnp.where(keep, a, 0.0)
            acc[p] = acc[p] + jnp.dot(vt_ref[p, j], a.astype(BF16), preferred_element_type=F32)
        return [(carry[p], acc[p]) for p in SR]

    def cond(c):
        live = jnp.max(c[1][0][0])
        for p in range(1, n_str):
            live = jnp.maximum(live, jnp.max(c[1][p][0]))
        return (c[0] >= 0) & (live > SB_EXP_FLOOR)

    def body(c):
        return c[0] - 1, steps([(c[0], None)], c[1])

    strict = _iota((tq, W), 0) < (_iota((tq, W), 1) & (tq - 1))
    eager = [(jnp.maximum(qi - d, 0), qi >= d) for d in range(1, SB_EAGER + 1)]
    st = steps([(qi, strict)] + eager, [(jnp.zeros((1, W), F32), jnp.zeros((2 * dh, W), F32)) for p in SR])
    _, st = lax.while_loop(cond, body, (qi - 1 - SB_EAGER, st))
    out_rows = []
    for p in SR:
        out_rows += [st[p][1][0:dh, 0:tq], st[p][1][dh:2 * dh, tq:W]]
    o_ref[0] = jnp.concatenate(out_rows, axis=0).T.astype(o_ref.dtype)


def sb_attention(y3, tq=256, n_str=2):
    b, s, _ = y3.shape
    wide = n_str * LANES
    qb, kb, vb = C_SB_Q // wide, C_SB_K // wide, C_SB_V // wide
    return pl.pallas_call(
        functools.partial(_sb_kernel, tq=tq, n_str=n_str),
        out_shape=jax.ShapeDtypeStruct((b, s, MIX_WIDTH), BF16),
        grid=(b, MIX_WIDTH // wide, s // tq),
        in_specs=[pl.BlockSpec((1, tq, wide), lambda i, p, j: (i, j, qb + p)),
                  pl.BlockSpec((1, s, wide), lambda i, p, j: (i, 0, kb + p)),
                  pl.BlockSpec((1, s, wide), lambda i, p, j: (i, 0, vb + p))],
        out_specs=pl.BlockSpec((1, tq, wide), lambda i, p, j: (i, j, p)),
        scratch_shapes=[pltpu.VMEM((n_str, s // tq, LANES, tq), BF16)],
        compiler_params=_params("parallel", "parallel", "arbitrary"),
        name="sb_attention",
    )(y3, y3, y3)


def _mlstm_kernel(q_ref, k_ref, v_ref, og_ref, sm_ref, gr_ref, cw_ref, cb_ref, gb_ref, out_ref,
                  ct_ref, n_ref, m_ref, xbuf_ref, qk_ref, *, ts):
    L, dh, H, W = ML_CHUNK, ML_HEAD_DIM, ML_HEADS, MIX_WIDTH
    halo = 8
    sblk = pl.program_id(1)

    @pl.when(sblk == 0)
    def _():
        ct_ref[...] = jnp.zeros_like(ct_ref)
        n_ref[...] = jnp.zeros_like(n_ref)
        m_ref[...] = jnp.zeros_like(m_ref)
        xbuf_ref[0:halo, :] = jnp.zeros((halo, 2 * W), F32)

    @pl.when(sblk > 0)
    def _():
        xbuf_ref[0:halo, :] = xbuf_ref[ts:ts + halo, :]

    xbuf_ref[halo:halo + ts, 0:W] = q_ref[0].astype(F32)
    xbuf_ref[halo:halo + ts, W:2 * W] = k_ref[0].astype(F32)
    conv = cb_ref[...] + jnp.zeros((ts, 2 * W), F32)
    for j in range(CONV_WIDTH):
        off = halo - (CONV_WIDTH - 1) + j
        conv = conv + cw_ref[j:j + 1, :] * xbuf_ref[off:off + ts, :]
    act = conv * jax.nn.sigmoid(conv)
    qk_ref[:, 0:W] = (act[:, 0:W] * (dh ** -0.5)).astype(BF16)
    qk_ref[:, W:2 * W] = act[:, W:2 * W].astype(BF16)

    it0, it1 = _iota((L, L), 0), _iota((L, L), 1)
    causal = it0 >= it1
    tri_lo = causal.astype(BF16)
    tri_up = (it0 <= it1).astype(BF16)

    def chunks(it, carry):
        HR = range(H)
        UR = range(ML_UNROLL)
        UH = [(u, h) for u in UR for h in HR]
        cols = [slice(h * dh, (h + 1) * dh) for h in HR]
        rows = [pl.ds(pl.multiple_of((it * ML_UNROLL + u) * L, L), L) for u in UR]
        sm = [sm_ref[0, rows[u], :] for u in UR]
        gr = [gr_ref[0, it * ML_UNROLL + u] for u in UR]
        ig_col = {(u, h): sm[u][:, S_ML_I + h:S_ML_I + h + 1] + gb_ref[0, h] for u, h in UH}
        lf_col = {(u, h): _log_sigmoid(sm[u][:, S_ML_F + h:S_ML_F + h + 1] + gb_ref[1, h]) for u, h in UH}
        ig_row = {(u, h): gr[u][h:h + 1, :] + gb_ref[0, h] for u, h in UH}
        lf_row = {(u, h): _log_sigmoid(gr[u][H + h:H + h + 1, :] + gb_ref[1, h]) for u, h in UH}
        b_t = {k: _split_dot_left(tri_lo, jnp.broadcast_to(lf_col[k], (L, L))) for k in UH}
        b_s = {k: _split_dot(jnp.broadcast_to(lf_row[k], (L, L)), tri_up) for k in UH}
        qq = {(u, h): qk_ref[rows[u], cols[h]] for u, h in UH}
        kk = {(u, h): qk_ref[rows[u], W + h * dh:W + (h + 1) * dh] for u, h in UH}
        vv = {(u, h): v_ref[0, rows[u], cols[h]] for u, h in UH}
        qk = {k: _dot_nt(qq[k], kk[k]) for k in UH}
        kt = {k: kk[k].astype(F32).T.astype(BF16) for k in UH}
        dmat = {k: jnp.where(causal, b_t[k] - b_s[k] + ig_row[k], NEG) for k in UH}
        d_max = {k: jnp.max(dmat[k], axis=1, keepdims=True) for k in UH}
        b_col = {k: b_t[k][:, 0:1] for k in UH}
        b_last = {k: b_t[k][L - 1:L, 0:1] for k in UH}
        decay = {k: b_last[k] - b_col[k] + ig_col[k] for k in UH}
        decay_max = {k: jnp.max(decay[k], axis=0, keepdims=True) for k in UH}
        m_prev, m_new = {}, {}
        for u, h in UH:
            m_prev[u, h] = m_ref[h][:, 0:1] if u == 0 else m_new[u - 1, h]
            m_new[u, h] = jnp.maximum(b_last[u, h] + m_prev[u, h], decay_max[u, h])
        m_inter = {k: b_col[k] + m_prev[k] for k in UH}
        m_t = {k: jnp.maximum(m_inter[k], d_max[k]) for k in UH}
        w = {k: jnp.exp(dmat[k] - m_t[k]) * qk[k] for k in UH}
        inter = {k: jnp.exp(m_inter[k] - m_t[k]) for k in UH}
        w_v = {k: jnp.dot(w[k].astype(BF16), vv[k], preferred_element_type=F32) for k in UH}
        ws = {k: jnp.exp(decay[k] - m_new[k]) for k in UH}
        cscale = {k: jnp.exp(b_last[k] + m_prev[k] - m_new[k]) for k in UH}
        wv = {k: (ws[k] * vv[k].astype(F32)).astype(BF16) for k in UH}
        k_wv = {k: jnp.dot(kt[k], wv[k], preferred_element_type=F32) for k in UH}
        k_ws = {k: jnp.sum(ws[k] * kk[k].astype(F32), axis=0, keepdims=True) for k in UH}
        w_sum = {k: jnp.sum(w[k], axis=1, keepdims=True) for k in UH}
        ct = {h: ct_ref[h] for h in HR}
        nvec = {h: n_ref[h] for h in HR}
        for u, h in UH:
            k = (u, h)
            num = inter[k] * jnp.dot(qq[k], ct[h].astype(BF16), preferred_element_type=F32) + w_v[k]
            den = inter[k] * jnp.sum(qq[k].astype(F32) * nvec[h], axis=1, keepdims=True) + w_sum[k]
            hval = num / jnp.maximum(jnp.abs(den), jnp.exp(-m_t[k]))
            ct[h] = cscale[k] * ct[h] + k_wv[k]
            nvec[h] = cscale[k] * nvec[h] + k_ws[k]
            gate = jax.nn.sigmoid(og_ref[0, rows[u], cols[h]].astype(F32))
            out_ref[0, rows[u], cols[h]] = (gate * hval).astype(out_ref.dtype)
        for h in HR:
            ct_ref[h] = ct[h]
            n_ref[h] = nvec[h]
            m_ref[h] = jnp.broadcast_to(m_new[ML_UNROLL - 1, h], (1, LANES))
        return carry

    lax.fori_loop(0, ts // (L * ML_UNROLL), chunks, 0)


def mlstm(y3, small3, conv_w, conv_b, gate_b, ts=512):
    b, s, _ = y3.shape
    W, H, L = MIX_WIDTH, ML_HEADS, ML_CHUNK
    gr = small3[:, :, S_ML_I:S_ML_I + 2 * H].reshape(b, s // L, L, 2 * H).transpose(0, 1, 3, 2)
    cq, ck, cv, co = C_ML_Q // W, C_ML_K // W, C_ML_V // W, C_ML_O // W
    return pl.pallas_call(
        functools.partial(_mlstm_kernel, ts=ts),
        out_shape=jax.ShapeDtypeStruct((b, s, W), BF16),
        grid=(b, s // ts),
        in_specs=[pl.BlockSpec((1, ts, W), lambda i, j: (i, j, cq)),
                  pl.BlockSpec((1, ts, W), lambda i, j: (i, j, ck)),
                  pl.BlockSpec((1, ts, W), lambda i, j: (i, j, cv)),
                  pl.BlockSpec((1, ts, W), lambda i, j: (i, j, co)),
                  pl.BlockSpec((1, ts, N_SMALL), lambda i, j: (i, j, 0)),
                  pl.BlockSpec((1, ts // L, 2 * H, L), lambda i, j: (i, j, 0, 0)),
                  pl.BlockSpec((CONV_WIDTH, 2 * W), lambda i, j: (0, 0)),
                  pl.BlockSpec((1, 2 * W), lambda i, j: (0, 0)),
                  pl.BlockSpec(memory_space=pltpu.SMEM)],
        out_specs=pl.BlockSpec((1, ts, W), lambda i, j: (i, j, 0)),
        scratch_shapes=[pltpu.VMEM((H, ML_HEAD_DIM, ML_HEAD_DIM), F32),
                        pltpu.VMEM((H, 1, ML_HEAD_DIM), F32),
                        pltpu.VMEM((H, 1, LANES), F32),
                        pltpu.VMEM((ts + 8, 2 * W), F32),
                        pltpu.VMEM((ts, 2 * W), BF16)],
        compiler_params=_params("parallel", "arbitrary"),
        name="mlstm",
    )(y3, y3, y3, y3, small3, gr, conv_w, conv_b.reshape(1, 2 * W), gate_b)


def _gelu_tanh(x):
    return 0.5 * x * (1.0 + jnp.tanh(0.7978845608028654 * (x + 0.044715 * (x * x * x))))


def _compress_kernel(ra_ref, rb_ref, pos_ref, w1_ref, b1_ref, w2_ref, b2_ref, kn_ref, kc_ref, vc_ref):
    half = (CMP_BLOCK // 2) * HEAD_DIM
    for j, o_ref in enumerate((kc_ref, vc_ref)):
        xa = (ra_ref[j, 0, 0].astype(F32) + pos_ref[j, :, 0:half]).astype(BF16)
        xb = (rb_ref[j, 0, 0].astype(F32) + pos_ref[j, :, half:2 * half]).astype(BF16)
        hid = (jnp.dot(xa, w1_ref[j, 0:half, :], preferred_element_type=F32)
               + jnp.dot(xb, w1_ref[j, half:2 * half, :], preferred_element_type=F32) + b1_ref[j])
        out = jnp.dot(_gelu_tanh(hid).astype(BF16), w2_ref[j], preferred_element_type=F32) + b2_ref[j]
        if j == 0:
            out = out * lax.rsqrt(jnp.mean(out * out, axis=-1, keepdims=True) + EPS) * kn_ref[...]
        o_ref[0, 0] = out


def nsa_compress(y3, cmp_pos, cmp_w1, cmp_b1, cmp_w2, cmp_b2, k_norm0):
    b, s, _ = y3.shape
    G, dh = NSA_KV_HEADS, HEAD_DIM
    nr = s // CMP_STRIDE
    wide = CMP_STRIDE * dh
    kv = y3[:, :, C_NSA_KV:C_NSA_KV + 2 * G * dh].reshape(b, s, 2, G, dh)
    ra = kv.transpose(2, 0, 3, 1, 4).reshape(2, b, G, nr, wide)
    rb = jnp.concatenate([ra[:, :, :, 1:], jnp.zeros((2, b, G, 1, wide), ra.dtype)], axis=3)
    hidden = cmp_w1.shape[-1]
    out = jax.ShapeDtypeStruct((b, G, nr, dh), F32)
    blk = pl.BlockSpec((2, 1, 1, nr, wide), lambda i, g: (0, i, g, 0, 0))
    oblk = pl.BlockSpec((1, 1, nr, dh), lambda i, g: (i, g, 0, 0))

    def full(shape):
        return pl.BlockSpec(shape, lambda i, g: (0,) * len(shape))

    return pl.pallas_call(
        _compress_kernel,
        out_shape=(out, out),
        grid=(b, G),
        in_specs=[blk, blk, full((2, 1, 2 * wide)), full((2, 2 * wide, hidden)), full((2, 1, hidden)),
                  full((2, hidden, dh)), full((2, 1, dh)), full((1, dh))],
        out_specs=(oblk, oblk),
        compiler_params=_params("parallel", "parallel"),
        name="nsa_compress",
    )(ra, rb, cmp_pos.reshape(2, 1, 2 * wide), cmp_w1.astype(BF16), cmp_b1.reshape(2, 1, hidden),
      cmp_w2.astype(BF16), cmp_b2.reshape(2, 1, dh), k_norm0.reshape(1, dh))


NSA_QB = 128
NSA_KS = 512
NSA_CK = 256
NSA_VR = 80
A_FEAT, A_PEN, A_BIAS = 0, 64, 128
A_DUMMY = A_BIAS + 4


def _head_rms(x, gain):
    w = x.shape[1]
    same_head = (_iota((w, w), 0) // HEAD_DIM == _iota((w, w), 1) // HEAD_DIM).astype(BF16)
    ss = _split_dot(x * x, same_head)
    return x * lax.rsqrt(ss * (1.0 / HEAD_DIM) + EPS) * gain


def _nsa_kernel(q_ref, gl_ref, kca_ref, vcT_ref, ks_ref, kw_ref, vs_ref, vw_ref, kconst_ref, gq_ref, gk_ref,
                     o_ref, qt_ref, ksa_ref, kwa_ref, vsa_ref, vwa_ref, s_ref, *, n_sel, top):
    QB, R, dh, CK, VR, KS = NSA_QB, NSA_GROUP, HEAD_DIM, NSA_CK, NSA_VR, NSA_KS
    G = NSA_KV_HEADS
    GR = range(G)
    HQ = R * QB
    NPAD = WINDOW // QB
    SUB = KS // QB
    qi = pl.program_id(1)
    q0 = qi * QB
    nkb = vsa_ref.shape[1]

    @pl.when(qi == 0)
    def _():
        pad_keys = jnp.where(_iota((WINDOW, CK), 1) == A_DUMMY, 1.0, 0.0).astype(BF16)
        ones_rows = jnp.where(_iota((nkb + NPAD, VR - dh, QB), 1) == 0, 1.0, 0.0).astype(BF16)
        for g in GR:
            heads = slice(g * dh, (g + 1) * dh)
            ksa_ref[g] = kconst_ref[...]
            kwa_ref[g, 0:WINDOW, :] = pad_keys
            kwa_ref[g, WINDOW:, :] = kconst_ref[...]
            kwa_ref[g, WINDOW:, A_PEN:A_PEN + 64] = jnp.zeros((kwa_ref.shape[1] - WINDOW, 64), BF16)
            vwa_ref[g, 0:NPAD, 0:dh, :] = jnp.zeros((NPAD, dh, QB), BF16)
            vsa_ref[g, :, dh:VR, :] = ones_rows[0:nkb]
            vwa_ref[g, :, dh:VR, :] = ones_rows
        for c in range(ks_ref.shape[1] // KS):
            keys = slice(c * KS, (c + 1) * KS)
            ks_n = _head_rms(ks_ref[0, keys, :].astype(F32), gk_ref[0]).astype(BF16)
            kw_n = _head_rms(kw_ref[0, keys, :].astype(F32), gk_ref[1]).astype(BF16)
            for g in GR:
                heads = slice(g * dh, (g + 1) * dh)
                ksa_ref[g, keys, A_FEAT:A_FEAT + dh] = ks_n[:, heads]
                kwa_ref[g, WINDOW + c * KS:WINDOW + (c + 1) * KS, A_FEAT:A_FEAT + dh] = kw_n[:, heads]
        for c in range(nkb):
            keys = slice(c * QB, (c + 1) * QB)
            vs_t = vs_ref[0, keys, :].astype(F32).T.astype(BF16)
            vw_t = vw_ref[0, keys, :].astype(F32).T.astype(BF16)
            for g in GR:
                heads = slice(g * dh, (g + 1) * dh)
                vsa_ref[g, c, 0:dh, :] = vs_t[heads]
                vwa_ref[g, NPAD + c, 0:dh, :] = vw_t[heads]
        qt_ref[:, A_BIAS + 16:CK, :] = jnp.zeros((G, CK - A_BIAS - 16, HQ), BF16)

    q_n = _head_rms(q_ref[0].astype(F32), gq_ref[...]).astype(BF16)
    q_rows = (q_n.astype(F32) * (dh ** -0.5)).T
    lane = _iota((16, HQ), 1)
    rowi = _iota((16, HQ), 0)
    t_q = q0 + (lane & (QB - 1))
    t_hi = ((t_q >> 6) << 6).astype(F32)
    t_lo = (t_q & 63).astype(F32)
    for g in GR:
        qT = jnp.concatenate([q_rows[(g * R + r) * dh:(g * R + r + 1) * dh] for r in range(R)],
                             axis=1).astype(BF16)
        qt_ref[g, A_FEAT:A_FEAT + dh, :] = qT
        qt_ref[g, A_PEN:A_PEN + dh, :] = qT
        slope = jnp.exp2(-(g * R + (lane >> 7) + 1).astype(F32))
        bias_rows = jnp.where(rowi < 2, slope,
                              jnp.where(rowi == 2, -slope * t_hi,
                                        jnp.where(rowi == 3, -slope * t_lo,
                                                  jnp.where(rowi == A_DUMMY - A_BIAS, NEG, 0.0))))
        qt_ref[g, A_BIAS:A_BIAS + 16, :] = bias_rows.astype(BF16)
    k_loc = _iota((QB, HQ), 0)
    q_loc = _iota((QB, HQ), 1) & (QB - 1)

    def pv(v_ref_, g, kb0, pr):
        out = None
        for i in range(pr.shape[0] // QB):
            term = jnp.dot(v_ref_[g, kb0 + i], pr[i * QB:(i + 1) * QB], preferred_element_type=F32)
            out = term if out is None else out + term
        return out

    n_cmp = kca_ref.shape[2]
    cmp_end = _iota((n_cmp, HQ), 0) * CMP_STRIDE + (CMP_BLOCK - 1)
    valid = cmp_end <= q0 + (_iota((n_cmp, HQ), 1) & (QB - 1))
    sc = [jnp.where(valid, jnp.dot(kca_ref[0, g], qt_ref[g], preferred_element_type=F32), NEG) for g in GR]
    e = [jnp.exp(sc[g] - jnp.max(sc[g], axis=0, keepdims=True)) for g in GR]
    p = [jnp.where(valid, e[g] * (1.0 / jnp.sum(e[g], axis=0, keepdims=True)), 0.0) for g in GR]
    o_cmp = [jnp.dot(vcT_ref[0, g], p[g].astype(BF16), preferred_element_type=F32) for g in GR]

    win_rows = pl.ds(pl.multiple_of(q0, QB), WINDOW + QB)
    sw = [jnp.dot(kwa_ref[g, win_rows, :], qt_ref[g], preferred_element_type=F32) for g in GR]
    sw = [jnp.concatenate([jnp.where(k_loc > q_loc, sw[g][0:QB], NEG), sw[g][QB:WINDOW],
                           jnp.where(k_loc <= q_loc, sw[g][WINDOW:WINDOW + QB], NEG)], axis=0) for g in GR]
    pw = [jnp.exp(sw[g] - jnp.max(sw[g], axis=0, keepdims=True)).astype(BF16) for g in GR]
    acc_w = [pv(vwa_ref, g, qi, pw[g]) for g in GR]
    o_win = [acc_w[g][0:dh] / acc_w[g][dh:dh + 1] for g in GR]

    c0 = _iota((n_sel, n_cmp), 1) * CMP_STRIDE
    s0 = _iota((n_sel, n_cmp), 0) * SEL_BLOCK
    overlap_t = ((c0 < s0 + SEL_BLOCK) & (c0 + CMP_BLOCK > s0)).astype(BF16)
    j_idx = _iota((n_sel, QB), 0)
    tq = q0 + _iota((n_sel, QB), 1)
    cur = tq >> 6
    forced = (j_idx == 0) | (j_idx == cur) | (j_idx == cur - 1)
    causal_blk = j_idx * SEL_BLOCK <= tq
    p_grp = [p[g][:, 0:QB] + p[g][:, QB:2 * QB] + p[g][:, 2 * QB:3 * QB] + p[g][:, 3 * QB:4 * QB] for g in GR]
    imp = [jnp.where(causal_blk, _split_dot_left(overlap_t, p_grp[g]) + jnp.where(forced, FORCE_BONUS, 0.0), -1.0)
           for g in GR]
    sel = [jnp.zeros((n_sel, QB), F32) for g in GR]
    for _ in range(top):
        for g in GR:
            mx = jnp.max(imp[g], axis=0, keepdims=True)
            first = jnp.min(jnp.where(imp[g] == mx, j_idx, n_sel), axis=0, keepdims=True)
            pick = j_idx == first
            sel[g] = jnp.where(pick, 1.0, sel[g])
            imp[g] = jnp.where(pick, -3e38, imp[g])
    for g in GR:
        pen = jnp.where((sel[g] > 0.5) & causal_blk, 0.0, NEG)
        if n_sel < 64:
            pen = jnp.concatenate([pen, jnp.zeros((64 - n_sel, QB), F32)], axis=0)
        qt_ref[g, A_PEN:A_PEN + 64, :] = jnp.concatenate([pen] * R, axis=1).astype(BF16)

    def score(j):
        rows = pl.ds(pl.multiple_of(j * KS, KS), KS)
        return [jnp.dot(ksa_ref[g, rows, :], qt_ref[g], preferred_element_type=F32) for g in GR]

    def absorb(s, j, st):
        m_new = [jnp.maximum(st[g][0], jnp.max(s[g], axis=0, keepdims=True)) for g in GR]
        pr = [jnp.exp(s[g] - m_new[g]).astype(BF16) for g in GR]
        return [(m_new[g], jnp.exp(st[g][0] - m_new[g]) * st[g][1] + pv(vsa_ref, g, j * SUB, pr[g])) for g in GR]

    n_full = qi // SUB
    init = [(jnp.full((1, HQ), NEG, F32), jnp.zeros((VR, HQ), F32)) for g in GR]
    st = lax.fori_loop(0, n_full, lambda j, st_: absorb(score(j), j, st_), init)
    s_last = score(n_full)
    diag = pl.ds(pl.multiple_of(q0 - n_full * KS, QB), QB)
    for g in GR:
        s_ref[g] = s_last[g]
        s_ref[g, diag, :] = jnp.where(k_loc <= q_loc, s_ref[g, diag, :], NEG)
    st = absorb([s_ref[g] for g in GR], n_full, st)
    o_sel = [st[g][1][0:dh] / st[g][1][dh:dh + 1] for g in GR]

    gl_t = gl_ref[0].T
    rows_out = []
    for g in GR:
        gate = [jax.nn.sigmoid(jnp.concatenate(
            [gl_t[(g * R + r) * N_BRANCH + br:(g * R + r) * N_BRANCH + br + 1] for r in range(R)], axis=1))
            for br in range(N_BRANCH)]
        o_t = gate[0] * o_cmp[g] + gate[1] * o_sel[g] + gate[2] * o_win[g]
        rows_out += [o_t[:, r * QB:(r + 1) * QB] for r in range(R)]
    o_ref[0] = jnp.concatenate(rows_out, axis=0).T.astype(o_ref.dtype)


def nsa_attention(y3, small3, q_norm, k_norm, kc, vc):
    b, s, _ = y3.shape
    G, R, dh, QB, CK, VR = NSA_KV_HEADS, NSA_GROUP, HEAD_DIM, NSA_QB, NSA_CK, NSA_VR
    assert G == 2, "the kernel picks a kv head's gate rows with a two-way select"
    HQ = R * QB
    nq = s // QB
    n_sel = s // SEL_BLOCK
    assert n_sel <= 64, "selection one-hot columns hold at most 64 blocks"
    top = min(SEL_TOPK, n_sel)
    n_cmp = kc.shape[2]

    def pos_cols(pos):
        return np.stack([pos // 64 * 64, pos % 64, np.ones_like(pos), np.ones_like(pos)], axis=1)

    vc_t = vc.transpose(0, 1, 3, 2).astype(BF16)

    pos = np.arange(s)
    kconst = np.zeros((s, CK), np.float32)
    kconst[pos, A_PEN + pos // SEL_BLOCK] = 1.0
    kconst[:, A_BIAS:A_BIAS + 4] = pos_cols(pos)
    kconst = jnp.asarray(kconst, BF16)

    kc_hi = kc.astype(BF16)
    kc_lo = (kc - kc_hi.astype(F32)).astype(BF16)
    cend = np.arange(n_cmp) * CMP_STRIDE + (CMP_BLOCK - 1)
    cbias = np.zeros((n_cmp, CK - 2 * dh), np.float32)
    cbias[:, 0:4] = pos_cols(cend)
    kc_aug = jnp.concatenate([kc_hi, kc_lo, jnp.broadcast_to(jnp.asarray(cbias, BF16), (b, G, n_cmp, CK - 2 * dh))],
                             axis=-1)

    qw = G * R * dh
    kvb = C_NSA_KV // LANES
    gq = jnp.tile(q_norm, G * R).reshape(1, qw)
    gk = jnp.stack([jnp.tile(k_norm[1], G), jnp.tile(k_norm[2], G)]).reshape(2, 1, LANES)

    def kv_spec(blk):
        return pl.BlockSpec((1, s, LANES), lambda i, j: (i, 0, kvb + blk))

    return pl.pallas_call(
        functools.partial(_nsa_kernel, n_sel=n_sel, top=top),
        out_shape=jax.ShapeDtypeStruct((b, s, MIX_WIDTH), BF16),
        grid=(b, nq),
        in_specs=[pl.BlockSpec((1, QB, qw), lambda i, j: (i, j, C_NSA_Q // qw)),
                  pl.BlockSpec((1, QB, N_SMALL), lambda i, j: (i, j, 0)),
                  pl.BlockSpec((1, G, n_cmp, CK), lambda i, j: (i, 0, 0, 0)),
                  pl.BlockSpec((1, G, dh, n_cmp), lambda i, j: (i, 0, 0, 0)),
                  kv_spec(2), kv_spec(4), kv_spec(3), kv_spec(5),
                  pl.BlockSpec((s, CK), lambda i, j: (0, 0)),
                  pl.BlockSpec((1, qw), lambda i, j: (0, 0)),
                  pl.BlockSpec((2, 1, LANES), lambda i, j: (0, 0, 0))],
        out_specs=pl.BlockSpec((1, QB, qw), lambda i, j: (i, j, 0)),
        scratch_shapes=[pltpu.VMEM((G, CK, HQ), BF16),
                        pltpu.VMEM((G, s, CK), BF16), pltpu.VMEM((G, s + WINDOW, CK), BF16),
                        pltpu.VMEM((G, nq, VR, QB), BF16), pltpu.VMEM((G, nq + WINDOW // QB, VR, QB), BF16),
                        pltpu.VMEM((G, NSA_KS, HQ), F32)],
        compiler_params=_params("parallel", "arbitrary"),
        name="nsa_attention",
    )(y3, small3, kc_aug, vc_t, y3, y3, y3, y3, kconst, gq, gk)


def _merge_kernel(on_ref, os_ref, om_ref, g0_ref, g1_ref, g2_ref, wb_ref, wo_ref, x_ref, mod_ref, o_ref):
    merged = None
    for i, (o_r, g_r) in enumerate(((on_ref, g0_ref), (os_ref, g1_ref), (om_ref, g2_ref))):
        br = jnp.dot(o_r[0], wb_ref[i], preferred_element_type=F32)
        term = jax.nn.sigmoid(g_r[0].astype(F32)) * br
        merged = term if merged is None else merged + term
    out = jnp.dot(merged.astype(BF16), wo_ref[...], preferred_element_type=F32)
    o_ref[0] = x_ref[0] + mod_ref[0, 2:3, :] * out


def merge_project(o_nsa, o_sb, o_ml, y3, w_branch, w_out, x, mod, tm=512):
    b, s, d = x.shape
    W = MIX_WIDTH
    ospec = pl.BlockSpec((1, tm, W), lambda i, j: (i, j, 0))
    xspec = pl.BlockSpec((1, tm, d), lambda i, j: (i, j, 0))
    gspecs = [pl.BlockSpec((1, tm, d), functools.partial(lambda i, j, c: (i, j, c), c=C_MERGE // d + c))
              for c in range(N_BRANCH)]
    return pl.pallas_call(
        _merge_kernel,
        out_shape=jax.ShapeDtypeStruct((b, s, d), F32),
        grid=(b, s // tm),
        in_specs=[ospec, ospec, ospec] + gspecs + [
            pl.BlockSpec((N_BRANCH, W, d), lambda i, j: (0, 0, 0)),
            pl.BlockSpec((d, d), lambda i, j: (0, 0)),
            xspec,
            pl.BlockSpec((1, 6, d), lambda i, j: (i, 0, 0))],
        out_specs=xspec,
        compiler_params=_params("parallel", "parallel"),
        name="merge_project",
    )(o_nsa, o_sb, o_ml, y3, y3, y3, w_branch.astype(BF16), w_out.astype(BF16), x, mod)


def _ffn_kernel(x_ref, g_ref, mod_ref, wg_ref, wu_ref, wd_ref, o_ref, h_ref, acc_ref):
    f = pl.program_id(2)

    @pl.when(f == 0)
    def _():
        h_ref[...] = _norm_mod(x_ref[0], g_ref[...], mod_ref[0], 3, 4).astype(BF16)
        acc_ref[...] = jnp.zeros_like(acc_ref)

    h = h_ref[...]
    a = jnp.dot(h, wg_ref[...], preferred_element_type=F32)
    u = jnp.dot(h, wu_ref[...], preferred_element_type=F32)
    act = (a * jax.nn.sigmoid(a) * u).astype(BF16)
    acc_ref[...] += jnp.dot(act, wd_ref[...], preferred_element_type=F32)

    @pl.when(f == pl.num_programs(2) - 1)
    def _():
        o_ref[0] = x_ref[0] + mod_ref[0, 5:6, :] * acc_ref[...]


def dense_ffn(x, g, mod, wg, wu, wd, tm=512, n_ftiles=2):
    b, s, d = x.shape
    ff = wg.shape[1]
    tf = -(-ff // (n_ftiles * LANES)) * LANES
    pad = n_ftiles * tf - ff
    wg = jnp.pad(to_bf16(wg), ((0, 0), (0, pad)))
    wu = jnp.pad(to_bf16(wu), ((0, 0), (0, pad)))
    wd = jnp.pad(to_bf16(wd), ((0, pad), (0, 0)))
    xspec = pl.BlockSpec((1, tm, d), lambda i, j, f: (i, j, 0))
    return pl.pallas_call(
        _ffn_kernel,
        out_shape=jax.ShapeDtypeStruct((b, s, d), F32),
        grid=(b, s // tm, n_ftiles),
        in_specs=[xspec,
                  pl.BlockSpec((1, d), lambda i, j, f: (0, 0)),
                  pl.BlockSpec((1, 6, d), lambda i, j, f: (i, 0, 0)),
                  pl.BlockSpec((d, tf), lambda i, j, f: (0, f)),
                  pl.BlockSpec((d, tf), lambda i, j, f: (0, f)),
                  pl.BlockSpec((tf, d), lambda i, j, f: (f, 0))],
        out_specs=xspec,
        scratch_shapes=[pltpu.VMEM((tm, d), BF16), pltpu.VMEM((tm, d), F32)],
        compiler_params=_params("parallel", "parallel", "arbitrary"),
        name="dense_ffn",
    )(x, g.reshape(1, d), mod, wg, wu, wd)


def _router_kernel(x_ref, g_ref, mod_ref, wr_ref, h_ref, e_ref, p_ref):
    h = _norm_mod(x_ref[0], g_ref[...], mod_ref[0], 3, 4)
    h_ref[...] = h
    lane = _iota((1, LANES), 1)
    real = lane < N_EXPERTS
    logits = jnp.where(real, jnp.dot(h, wr_ref[...], precision=HIGHEST, preferred_element_type=F32), NEG)
    e = jnp.exp(logits - jnp.max(logits, axis=1, keepdims=True))
    p = jnp.where(real, e / jnp.sum(e, axis=1, keepdims=True), -1.0)
    p1 = jnp.max(p, axis=1, keepdims=True)
    i1 = jnp.min(jnp.where(p == p1, lane, LANES), axis=1, keepdims=True)
    rest = jnp.where(lane == i1, -1.0, p)
    p2 = jnp.max(rest, axis=1, keepdims=True)
    i2 = jnp.min(jnp.where(rest == p2, lane, LANES), axis=1, keepdims=True)
    tot = p1 + p2
    e_ref[...] = jnp.where(lane == 0, i1, jnp.where(lane == 1, i2, 0))[:, 0:N_EXPERTS]
    p_ref[...] = jnp.where(lane == 0, p1 / tot, jnp.where(lane == 1, p2 / tot, 0.0))[:, 0:N_EXPERTS]


def moe_router(x, g, mod, w_router, tm=512):
    b, s, d = x.shape
    t = b * s
    spb = s // tm
    wr = jnp.pad(w_router, ((0, 0), (0, LANES - N_EXPERTS)))
    return pl.pallas_call(
        _router_kernel,
        out_shape=(jax.ShapeDtypeStruct((t, d), F32),
                   jax.ShapeDtypeStruct((t, N_EXPERTS), I32),
                   jax.ShapeDtypeStruct((t, N_EXPERTS), F32)),
        grid=(b, spb),
        in_specs=[pl.BlockSpec((1, tm, d), lambda i, j: (i, j, 0)),
                  pl.BlockSpec((1, d), lambda i, j: (0, 0)),
                  pl.BlockSpec((1, 6, d), lambda i, j: (i, 0, 0)),
                  pl.BlockSpec((d, LANES), lambda i, j: (0, 0))],
        out_specs=(pl.BlockSpec((tm, d), lambda i, j: (i * spb + j, 0)),
                   pl.BlockSpec((tm, N_EXPERTS), lambda i, j: (i * spb + j, 0)),
                   pl.BlockSpec((tm, N_EXPERTS), lambda i, j: (i * spb + j, 0))),
        compiler_params=_params("parallel", "parallel"),
        name="moe_router",
    )(x, g.reshape(1, d), mod, wr)


def moe_ffn(x, g, mod, w_router, wg, wu, wd, tb=512):
    b, s, d = x.shape
    t = b * s
    a = t * TOP_K
    h, top_e, top_p = moe_router(x, g, mod, w_router)
    e_flat = top_e[:, 0:TOP_K].reshape(a)
    onehot = (e_flat[:, None] == jnp.arange(N_EXPERTS, dtype=I32)[None, :]).astype(I32)
    csum = jnp.cumsum(onehot, axis=0)
    rank = jnp.sum(onehot * csum, axis=1) - 1
    counts = csum[-1]
    padded = (counts + tb - 1) // tb * tb
    pad_ends = jnp.cumsum(padded)
    pad_starts = pad_ends - padded
    dest = (jnp.sum(onehot * pad_starts[None, :], axis=1) + rank).astype(I32)
    n_rows = (a // tb + N_EXPERTS + 1) * tb
    n_blk = n_rows // tb
    blk_expert = jnp.minimum(
        jnp.searchsorted(pad_ends, jnp.arange(n_blk, dtype=I32) * tb, side="right"), N_EXPERTS - 1).astype(I32)
    n_used = (pad_ends[-1:] // tb).astype(I32)
    pad_lo = jnp.concatenate([pad_starts + counts, pad_ends[-1:]]).astype(I32)
    pad_hi = jnp.concatenate([pad_ends, jnp.full((1,), n_rows, I32)]).astype(I32)
    slot_assign = moe_invert(dest, pad_lo, pad_hi, n_rows, tb)
    y2 = moe_experts(h, slot_assign, blk_expert, n_used, to_bf16(wg), to_bf16(wu), to_bf16(wd), tb)
    return moe_mix(y2, x, top_p, mod)


def _invert_kernel(dest_ref, lo_ref, hi_ref, sa_ref, *, n_assign, tb, n_chunks):
    phase = pl.program_id(0)
    chunk = pl.program_id(1)

    @pl.when((phase == 0) & (chunk < lo_ref.shape[0]))
    def _():
        def fill(p, c):
            sa_ref[p] = n_assign + (p & (2 * tb - 1))
            return c

        lax.fori_loop(lo_ref[chunk], hi_ref[chunk], fill, 0)

    @pl.when(phase == 1)
    def _():
        per = n_assign // n_chunks

        def put(j, c):
            a = chunk * per + j
            sa_ref[dest_ref[a]] = a
            return c

        lax.fori_loop(0, per, put, 0, unroll=8)


def moe_invert(dest, pad_lo, pad_hi, n_rows, tb, n_chunks=16):
    n_assign = dest.shape[0]
    assert tb & (tb - 1) == 0 and n_assign % n_chunks == 0 and pad_lo.shape[0] <= n_chunks
    smem = pl.BlockSpec(memory_space=pltpu.SMEM)
    return pl.pallas_call(
        functools.partial(_invert_kernel, n_assign=n_assign, tb=tb, n_chunks=n_chunks),
        out_shape=jax.ShapeDtypeStruct((n_rows,), I32),
        grid=(2, n_chunks),
        in_specs=[smem, smem, smem],
        out_specs=smem,
        compiler_params=pltpu.CompilerParams(dimension_semantics=("arbitrary", "arbitrary")),
        name="moe_invert",
    )(dest, pad_lo, pad_hi)


def _expert_kernel(be_ref, nu_ref, sa_ref, h_hbm, wg_ref, wu_ref, wd_ref, y_hbm,
                         xin_ref, yout_ref, xb_ref, acc_ref, sem_in, sem_out, *, tb, n_tok, n_f):
    i = pl.program_id(0)
    f = pl.program_id(1)
    n_used = nu_ref[0]
    n_assign = n_tok * TOP_K
    rows_f = tb // n_f
    active = i <= n_used

    def gather_row(blk, r):
        a = sa_ref[blk * tb + r]
        tok = jnp.where(a < n_assign, a >> 1, 0)
        return pltpu.make_async_copy(h_hbm.at[tok], xin_ref.at[blk % 2, r], sem_in.at[blk % 2])

    def scatter_row(blk, r):
        a = jnp.where(blk >= 0, sa_ref[jnp.maximum(blk, 0) * tb + r], n_assign + tb + r)
        row = jnp.where(a < n_assign, (a & 1) * n_tok + (a >> 1), a)
        return pltpu.make_async_copy(yout_ref.at[(blk + 2) % 2, r], y_hbm.at[row], sem_out.at[(blk + 2) % 2])

    def for_rows(fn):
        def body(r, c):
            fn(r)
            return c
        lax.fori_loop(0, tb, body, 0, unroll=8)

    @pl.when(f == 0)
    def _():
        @pl.when(i == 0)
        def _():
            yout_ref[1] = jnp.zeros((tb, yout_ref.shape[2]), F32)
            for_rows(lambda r: gather_row(0, r).start())

            def clear_row(r):
                return pltpu.make_async_copy(yout_ref.at[1, r], y_hbm.at[n_assign + r], sem_out.at[0])

            for_rows(lambda r: clear_row(r).start())
            for_rows(lambda r: clear_row(r).wait())

        @pl.when((i == 0) | (i - 1 <= n_used))
        def _():
            for_rows(lambda r: gather_row(i, r).wait())

        @pl.when(active)
        def _():
            xb_ref[...] = xin_ref[i % 2].astype(BF16)
            acc_ref[...] = jnp.zeros_like(acc_ref)

    @pl.when(active)
    def _():
        for r in range(rows_f):
            gather_row(i + 1, f * rows_f + r).start()
            scatter_row(i - 1, f * rows_f + r).start()
        xb = xb_ref[...]
        a = jnp.dot(xb, wg_ref[0], preferred_element_type=F32)
        u = jnp.dot(xb, wu_ref[0], preferred_element_type=F32)
        act = (a * jax.nn.sigmoid(a) * u).astype(BF16)
        acc_ref[...] += jnp.dot(act, wd_ref[0], preferred_element_type=F32)

    @pl.when(f == n_f - 1)
    def _():
        @pl.when((i >= 1) & (i - 1 <= n_used))
        def _():
            for_rows(lambda r: scatter_row(i - 2, r).wait())

        @pl.when(active)
        def _():
            yout_ref[i % 2] = acc_ref[...]


def moe_experts(h, slot_assign, blk_expert, n_used, wg, wu, wd, tb, tf=1792):
    n_tok, d = h.shape
    p = slot_assign.shape[0]
    ff = wg.shape[2]
    return pl.pallas_call(
        functools.partial(_expert_kernel, tb=tb, n_tok=n_tok, n_f=ff // tf),
        out_shape=jax.ShapeDtypeStruct((n_tok * TOP_K + 2 * tb, d), F32),
        grid_spec=pltpu.PrefetchScalarGridSpec(
            num_scalar_prefetch=3,
            grid=(p // tb, ff // tf),
            in_specs=[pl.BlockSpec(memory_space=pl.ANY),
                      pl.BlockSpec((1, d, tf), lambda i, f, be, nu, sa: (be[i], 0, f)),
                      pl.BlockSpec((1, d, tf), lambda i, f, be, nu, sa: (be[i], 0, f)),
                      pl.BlockSpec((1, tf, d), lambda i, f, be, nu, sa: (be[i], f, 0))],
            out_specs=pl.BlockSpec(memory_space=pl.ANY),
            scratch_shapes=[pltpu.VMEM((2, tb, d), F32), pltpu.VMEM((2, tb, d), F32),
                            pltpu.VMEM((tb, d), BF16), pltpu.VMEM((tb, d), F32),
                            pltpu.SemaphoreType.DMA((2,)), pltpu.SemaphoreType.DMA((2,))]),
        compiler_params=pltpu.CompilerParams(dimension_semantics=("arbitrary", "arbitrary"),
                                             vmem_limit_bytes=VMEM_LIMIT, has_side_effects=True),
        name="moe_experts",
    )(blk_expert, n_used, slot_assign, h, wg, wu, wd)


def _mix_kernel(y0_ref, y1_ref, x_ref, p_ref, mod_ref, o_ref):
    w = p_ref[...]
    f = w[:, 0:1] * y0_ref[...] + w[:, 1:2] * y1_ref[...]
    o_ref[0] = x_ref[0] + mod_ref[0, 5:6, :] * f


def moe_mix(y2, x, top_p, mod, td=512):
    b, s, d = x.shape
    spb = s // td
    nt = b * spb
    return pl.pallas_call(
        _mix_kernel,
        out_shape=jax.ShapeDtypeStruct((b, s, d), F32),
        grid=(b, spb),
        in_specs=[pl.BlockSpec((td, d), lambda i, j: (i * spb + j, 0)),
                  pl.BlockSpec((td, d), lambda i, j: (nt + i * spb + j, 0)),
                  pl.BlockSpec((1, td, d), lambda i, j: (i, j, 0)),
                  pl.BlockSpec((td, N_EXPERTS), lambda i, j: (i * spb + j, 0)),
                  pl.BlockSpec((1, 6, d), lambda i, j: (i, 0, 0))],
        out_specs=pl.BlockSpec((1, td, d), lambda i, j: (i, j, 0)),
        compiler_params=_params("parallel", "parallel"),
        name="moe_mix",
    )(y2, y2, x, top_p, mod)


def _pack_w_in(w_in):
    kv = 2 * NSA_KV_HEADS * HEAD_DIM * 3
    w_in = to_bf16(w_in)
    o = 0
    nsa_q = w_in[:, o:o + MIX_WIDTH]; o += MIX_WIDTH
    nsa_kv = w_in[:, o:o + kv]; o += kv
    nsa_gate = w_in[:, o:o + NSA_HEADS * N_BRANCH]; o += NSA_HEADS * N_BRANCH
    sb = w_in[:, o:o + 3 * MIX_WIDTH]; o += 3 * MIX_WIDTH
    ml_qkv = w_in[:, o:o + 3 * MIX_WIDTH]; o += 3 * MIX_WIDTH
    ml_if = w_in[:, o:o + 2 * ML_HEADS]; o += 2 * ML_HEADS
    ml_o = w_in[:, o:o + MIX_WIDTH]; o += MIX_WIDTH
    merge = w_in[:, o:]
    main = jnp.concatenate([merge, nsa_q, ml_qkv, ml_o, sb, nsa_kv], axis=1)
    small = jnp.concatenate([nsa_gate, ml_if], axis=1)
    small = jnp.pad(small, ((0, 0), (0, N_SMALL - small.shape[1])))
    return main, small


def token_mixer_layer(x, mod, norm_g, w_in, nsa_q_norm, nsa_k_norm, cmp_pos, cmp_w1, cmp_b1, cmp_w2,
                      cmp_b2, ml_conv_w, ml_conv_b, ml_gate_b, w_branch, w_out):
    w_main, w_small = _pack_w_in(w_in)
    y3, small3 = in_projection(x, norm_g, mod, w_main, w_small)
    o_sb = sb_attention(y3)
    o_ml = mlstm(y3, small3, ml_conv_w, ml_conv_b, ml_gate_b)
    kc, vc = nsa_compress(y3, cmp_pos, cmp_w1, cmp_b1, cmp_w2, cmp_b2, nsa_k_norm[0])
    o_nsa = nsa_attention(y3, small3, nsa_q_norm, nsa_k_norm, kc, vc)
    return merge_project(o_nsa, o_sb, o_ml, y3, w_branch, w_out, x, mod)


def kernel(x, c, ada_w, ada_b, norm_mix, norm_ffn, w_in, nsa_q_norm, nsa_k_norm, cmp_pos, cmp_w1, cmp_b1,
           cmp_w2, cmp_b2, ml_conv_w, ml_conv_b, ml_gate_b, w_branch, w_out, ffn_wg, ffn_wu, ffn_wd,
           moe_router, moe_wg, moe_wu, moe_wd):
    depth = ada_w.shape[0]
    b, s, d = x.shape
    mods = adaln(c, ada_w, ada_b).reshape(depth, b, 6, d)
    for layer in range(depth):
        mod = mods[layer]
        x = token_mixer_layer(x, mod, norm_mix[layer], w_in[layer], nsa_q_norm[layer], nsa_k_norm[layer],
                              cmp_pos[layer], cmp_w1[layer], cmp_b1[layer], cmp_w2[layer], cmp_b2[layer],
                              ml_conv_w[layer], ml_conv_b[layer], ml_gate_b[layer], w_branch[layer],
                              w_out[layer])
        j = layer // 2
        if layer % 2 == 0:
            x = dense_ffn(x, norm_ffn[layer], mod, ffn_wg[j], ffn_wu[j], ffn_wd[j])
        else:
            x = moe_ffn(x, norm_ffn[layer], mod, moe_router[j], moe_wg[j], moe_wu[j], moe_wd[j])
    return x
```

```python
import functools

import numpy as np
import jax
import jax.numpy as jnp
from jax import lax
from jax.experimental import pallas as pl
from jax.experimental.pallas import tpu as pltpu

F32 = jnp.float32
BF16 = jnp.bfloat16
I32 = jnp.int32
HIGHEST = lax.Precision.HIGHEST

EPS = 1e-6
NEG = -1e30
HEAD_DIM = 64
MIX_WIDTH = 512
NSA_HEADS = 8
NSA_KV_HEADS = 2
NSA_GROUP = NSA_HEADS // NSA_KV_HEADS
CMP_BLOCK = 32
CMP_STRIDE = 16
SEL_BLOCK = 64
SEL_TOPK = 16
WINDOW = 512
FORCE_BONUS = 1e4
ML_HEADS = 4
ML_HEAD_DIM = 128
ML_CHUNK = 64
ML_UNROLL = 2
CONV_WIDTH = 4
N_BRANCH = 3
N_EXPERTS = 8
TOP_K = 2
LANES = 128

C_MERGE = 0
C_NSA_Q = 3072
C_ML_Q = 3584
C_ML_K = 4096
C_ML_V = 4608
C_ML_O = 5120
C_SB_Q = 5632
C_SB_K = 6144
C_SB_V = 6656
C_NSA_KV = 7168
N_MAIN = 7936
S_NSA_GATE = 0
S_ML_I = 24
S_ML_F = 28
N_SMALL = 128

VMEM_LIMIT = 56 * 1024 * 1024


def _params(*sem):
    return pltpu.CompilerParams(dimension_semantics=sem, vmem_limit_bytes=VMEM_LIMIT)


def _iota(shape, dim):
    return lax.broadcasted_iota(I32, shape, dim)


def _split_dot(a32, b_bf16):
    hi = a32.astype(BF16)
    lo = (a32 - hi.astype(F32)).astype(BF16)
    return (jnp.dot(hi, b_bf16, preferred_element_type=F32)
            + jnp.dot(lo, b_bf16, preferred_element_type=F32))


def _split_dot_left(a_bf16, b32):
    hi = b32.astype(BF16)
    lo = (b32 - hi.astype(F32)).astype(BF16)
    return (jnp.dot(a_bf16, hi, preferred_element_type=F32)
            + jnp.dot(a_bf16, lo, preferred_element_type=F32))


def _dot_nt(a, b):
    return lax.dot_general(a, b, (((1,), (1,)), ((), ())), preferred_element_type=F32)


def _log_sigmoid(z):
    return jnp.minimum(z, 0.0) - jnp.log1p(jnp.exp(-jnp.abs(z)))


def _cast_kernel(x_ref, o_ref):
    o_ref[...] = x_ref[...].astype(o_ref.dtype)


def to_bf16(w, max_rows=512):
    cols = w.shape[-1]
    w2 = w.reshape(-1, cols)
    rows = w2.shape[0]
    tr = max(t for t in range(8, max_rows + 1, 8) if rows % t == 0)
    out = pl.pallas_call(
        _cast_kernel,
        out_shape=jax.ShapeDtypeStruct((rows, cols), BF16),
        grid=(rows // tr,),
        in_specs=[pl.BlockSpec((tr, cols), lambda i: (i, 0))],
        out_specs=pl.BlockSpec((tr, cols), lambda i: (i, 0)),
        compiler_params=_params("parallel"),
        name="to_bf16",
    )(w2)
    return out.reshape(w.shape)


def _adaln_kernel(c_ref, w_ref, b_ref, o_ref):
    c = c_ref[...]
    cond = c * jax.nn.sigmoid(c)
    o_ref[0] = jnp.dot(cond, w_ref[0], precision=HIGHEST, preferred_element_type=F32) + b_ref[0]


def adaln(c, ada_w, ada_b):
    depth, d, n = ada_w.shape
    b = c.shape[0]
    tn = 1536
    return pl.pallas_call(
        _adaln_kernel,
        out_shape=jax.ShapeDtypeStruct((depth, b, n), F32),
        grid=(depth, n // tn),
        in_specs=[pl.BlockSpec((b, d), lambda l, j: (0, 0)),
                  pl.BlockSpec((1, d, tn), lambda l, j: (l, 0, j)),
                  pl.BlockSpec((1, 1, tn), lambda l, j: (l, 0, j))],
        out_specs=pl.BlockSpec((1, b, tn), lambda l, j: (l, 0, j)),
        compiler_params=_params("parallel", "parallel"),
        name="adaln",
    )(c, ada_w, ada_b.reshape(depth, 1, n))


def _norm_mod(x, g, mod, shift_row, scale_row):
    ms = jnp.mean(x * x, axis=-1, keepdims=True)
    y = x * lax.rsqrt(ms + EPS) * g
    return y * (1.0 + mod[scale_row:scale_row + 1, :]) + mod[shift_row:shift_row + 1, :]


def _in_proj_kernel(x_ref, g_ref, mod_ref, wm_ref, ws_ref, y_ref, sm_ref, h_ref):
    @pl.when(pl.program_id(2) == 0)
    def _():
        h_ref[...] = _norm_mod(x_ref[0], g_ref[...], mod_ref[0], 0, 1).astype(BF16)
        sm_ref[0] = jnp.dot(h_ref[...], ws_ref[...], preferred_element_type=F32)

    y_ref[0] = jnp.dot(h_ref[...], wm_ref[...], preferred_element_type=F32).astype(y_ref.dtype)


def in_projection(x, g, mod, w_main, w_small, tm=512, n_tiles=2):
    b, s, d = x.shape
    tn = N_MAIN // n_tiles
    return pl.pallas_call(
        _in_proj_kernel,
        out_shape=(jax.ShapeDtypeStruct((b, s, N_MAIN), BF16), jax.ShapeDtypeStruct((b, s, N_SMALL), F32)),
        grid=(b, s // tm, n_tiles),
        in_specs=[pl.BlockSpec((1, tm, d), lambda i, j, n: (i, j, 0)),
                  pl.BlockSpec((1, d), lambda i, j, n: (0, 0)),
                  pl.BlockSpec((1, 6, d), lambda i, j, n: (i, 0, 0)),
                  pl.BlockSpec((d, tn), lambda i, j, n: (0, n)),
                  pl.BlockSpec((d, N_SMALL), lambda i, j, n: (0, 0))],
        out_specs=(pl.BlockSpec((1, tm, tn), lambda i, j, n: (i, j, n)),
                   pl.BlockSpec((1, tm, N_SMALL), lambda i, j, n: (i, j, 0))),
        scratch_shapes=[pltpu.VMEM((tm, d), BF16)],
        compiler_params=_params("parallel", "parallel", "arbitrary"),
        name="in_projection",
    )(x, g.reshape(1, d), mod, w_main, w_small)


SB_EXP_FLOOR = -104.0
SB_EAGER = 2


def _sb_kernel(q_ref, k_ref, v_ref, o_ref, vt_ref, *, tq, n_str):
    TK, dh = LANES, HEAD_DIM
    n_sub = tq // TK
    W = 2 * tq
    SR = range(n_str)
    qi = pl.program_id(2)

    @pl.when(qi == 0)
    def _():
        for c in range(v_ref.shape[1] // tq):
            v_t = v_ref[0, c * tq:(c + 1) * tq, :].astype(F32).T.astype(BF16)
            for p in SR:
                vt_ref[p, c] = v_t[p * LANES:(p + 1) * LANES]

    q_t = (q_ref[0].astype(F32) * (dh ** -0.5)).T
    chan = _iota((2 * dh, tq), 0)
    q_cat = []
    for p in SR:
        q_p = q_t[p * LANES:(p + 1) * LANES]
        q_cat.append(jnp.concatenate([jnp.where(chan < dh, q_p, 0.0), jnp.where(chan < dh, 0.0, q_p)],
                                     axis=1).astype(BF16))
    later = (_iota((TK, TK), 0) < _iota((TK, TK), 1)).astype(BF16)
    suffix = jnp.concatenate([jnp.concatenate([later, later], axis=1), jnp.ones((8, 2 * TK), BF16)], axis=0)

    def steps(blocks, st):
        work = [(p, b) for p in SR for b in range(len(blocks))]
        lss, his, los = {}, {}, {}
        for p, b in work:
            j, keep = blocks[b]
            k0 = pl.multiple_of(j * tq, tq)
            z = jnp.dot(k_ref[0, pl.ds(k0, tq), p * LANES:(p + 1) * LANES], q_cat[p],
                        preferred_element_type=F32)
            lk = -(jnp.maximum(z, 0.0) + jnp.log(1.0 + jnp.exp(-jnp.abs(z))))
            lss[p, b] = lk + z
            if keep is not None:
                lk = jnp.where(keep, lk, 0.0)
            his[p, b] = lk.astype(BF16)
            los[p, b] = (lk - his[p, b].astype(F32)).astype(BF16)
        carry = [st[p][0] for p in SR]
        afters = {}
        for p, b in work:
            after = [None] * n_sub
            for sub in range(n_sub - 1, -1, -1):
                rows = slice(sub * TK, (sub + 1) * TK)
                res = jnp.dot(suffix, jnp.concatenate([his[p, b][rows], los[p, b][rows]], axis=0),
                              preferred_element_type=F32)
                after[sub] = res[0:TK] + carry[p]
                carry[p] = carry[p] + res[TK:TK + 1]
            afters[p, b] = jnp.concatenate(after, axis=0)
        acc = [st[p][1] for p in SR]
        for p, b in work:
            j, keep = blocks[b]
            a = jnp.exp(lss[p, b] + afters[p, b])
            if keep is not None:
                a = jnp.where(keep, a, 0.0)
            acc[p] = acc[p] + jnp.dot(vt_ref[p, j], a.astype(BF16), preferred_element_type=F32)
        return [(carry[p], acc[p]) for p in SR]

    def cond(c):
        live = jnp.max(c[1][0][0])
        for p in range(1, n_str):
            live = jnp.maximum(live, jnp.max(c[1][p][0]))
        return (c[0] >= 0) & (live > SB_EXP_FLOOR)

    def body(c):
        return c[0] - 1, steps([(c[0], None)], c[1])

    strict = _iota((tq, W), 0) < (_iota((tq, W), 1) & (tq - 1))
    eager = [(jnp.maximum(qi - d, 0), qi >= d) for d in range(1, SB_EAGER + 1)]
    st = steps([(qi, strict)] + eager, [(jnp.zeros((1, W), F32), jnp.zeros((2 * dh, W), F32)) for p in SR])
    _, st = lax.while_loop(cond, body, (qi - 1 - SB_EAGER, st))
    out_rows = []
    for p in SR:
        out_rows += [st[p][1][0:dh, 0:tq], st[p][1][dh:2 * dh, tq:W]]
    o_ref[0] = jnp.concatenate(out_rows, axis=0).T.astype(o_ref.dtype)


def sb_attention(y3, tq=256, n_str=2):
    b, s, _ = y3.shape
    wide = n_str * LANES
    qb, kb, vb = C_SB_Q // wide, C_SB_K // wide, C_SB_V // wide
    return pl.pallas_call(
        functools.partial(_sb_kernel, tq=tq, n_str=n_str),
        out_shape=jax.ShapeDtypeStruct((b, s, MIX_WIDTH), BF16),
        grid=(b, MIX_WIDTH // wide, s // tq),
        in_specs=[pl.BlockSpec((1, tq, wide), lambda i, p, j: (i, j, qb + p)),
                  pl.BlockSpec((1, s, wide), lambda i, p, j: (i, 0, kb + p)),
                  pl.BlockSpec((1, s, wide), lambda i, p, j: (i, 0, vb + p))],
        out_specs=pl.BlockSpec((1, tq, wide), lambda i, p, j: (i, j, p)),
        scratch_shapes=[pltpu.VMEM((n_str, s // tq, LANES, tq), BF16)],
        compiler_params=_params("parallel", "parallel", "arbitrary"),
        name="sb_attention",
    )(y3, y3, y3)


def _mlstm_kernel(q_ref, k_ref, v_ref, og_ref, sm_ref, gr_ref, cw_ref, cb_ref, gb_ref, out_ref,
                  ct_ref, n_ref, m_ref, xbuf_ref, qk_ref, *, ts):
    L, dh, H, W = ML_CHUNK, ML_HEAD_DIM, ML_HEADS, MIX_WIDTH
    halo = 8
    sblk = pl.program_id(1)

    @pl.when(sblk == 0)
    def _():
        ct_ref[...] = jnp.zeros_like(ct_ref)
        n_ref[...] = jnp.zeros_like(n_ref)
        m_ref[...] = jnp.zeros_like(m_ref)
        xbuf_ref[0:halo, :] = jnp.zeros((halo, 2 * W), F32)

    @pl.when(sblk > 0)
    def _():
        xbuf_ref[0:halo, :] = xbuf_ref[ts:ts + halo, :]

    xbuf_ref[halo:halo + ts, 0:W] = q_ref[0].astype(F32)
    xbuf_ref[halo:halo + ts, W:2 * W] = k_ref[0].astype(F32)
    conv = cb_ref[...] + jnp.zeros((ts, 2 * W), F32)
    for j in range(CONV_WIDTH):
        off = halo - (CONV_WIDTH - 1) + j
        conv = conv + cw_ref[j:j + 1, :] * xbuf_ref[off:off + ts, :]
    act = conv * jax.nn.sigmoid(conv)
    qk_ref[:, 0:W] = (act[:, 0:W] * (dh ** -0.5)).astype(BF16)
    qk_ref[:, W:2 * W] = act[:, W:2 * W].astype(BF16)

    it0, it1 = _iota((L, L), 0), _iota((L, L), 1)
    causal = it0 >= it1
    tri_lo = causal.astype(BF16)
    tri_up = (it0 <= it1).astype(BF16)

    def chunks(it, carry):
        HR = range(H)
        UR = range(ML_UNROLL)
        UH = [(u, h) for u in UR for h in HR]
        cols = [slice(h * dh, (h + 1) * dh) for h in HR]
        rows = [pl.ds(pl.multiple_of((it * ML_UNROLL + u) * L, L), L) for u in UR]
        sm = [sm_ref[0, rows[u], :] for u in UR]
        gr = [gr_ref[0, it * ML_UNROLL + u] for u in UR]
        ig_col = {(u, h): sm[u][:, S_ML_I + h:S_ML_I + h + 1] + gb_ref[0, h] for u, h in UH}
        lf_col = {(u, h): _log_sigmoid(sm[u][:, S_ML_F + h:S_ML_F + h + 1] + gb_ref[1, h]) for u, h in UH}
        ig_row = {(u, h): gr[u][h:h + 1, :] + gb_ref[0, h] for u, h in UH}
        lf_row = {(u, h): _log_sigmoid(gr[u][H + h:H + h + 1, :] + gb_ref[1, h]) for u, h in UH}
        b_t = {k: _split_dot_left(tri_lo, jnp.broadcast_to(lf_col[k], (L, L))) for k in UH}
        b_s = {k: _split_dot(jnp.broadcast_to(lf_row[k], (L, L)), tri_up) for k in UH}
        qq = {(u, h): qk_ref[rows[u], cols[h]] for u, h in UH}
        kk = {(u, h): qk_ref[rows[u], W + h * dh:W + (h + 1) * dh] for u, h in UH}
        vv = {(u, h): v_ref[0, rows[u], cols[h]] for u, h in UH}
        qk = {k: _dot_nt(qq[k], kk[k]) for k in UH}
        kt = {k: kk[k].astype(F32).T.astype(BF16) for k in UH}
        dmat = {k: jnp.where(causal, b_t[k] - b_s[k] + ig_row[k], NEG) for k in UH}
        d_max = {k: jnp.max(dmat[k], axis=1, keepdims=True) for k in UH}
        b_col = {k: b_t[k][:, 0:1] for k in UH}
        b_last = {k: b_t[k][L - 1:L, 0:1] for k in UH}
        decay = {k: b_last[k] - b_col[k] + ig_col[k] for k in UH}
        decay_max = {k: jnp.max(decay[k], axis=0, keepdims=True) for k in UH}
        m_prev, m_new = {}, {}
        for u, h in UH:
            m_prev[u, h] = m_ref[h][:, 0:1] if u == 0 else m_new[u - 1, h]
            m_new[u, h] = jnp.maximum(b_last[u, h] + m_prev[u, h], decay_max[u, h])
        m_inter = {k: b_col[k] + m_prev[k] for k in UH}
        m_t = {k: jnp.maximum(m_inter[k], d_max[k]) for k in UH}
        w = {k: jnp.exp(dmat[k] - m_t[k]) * qk[k] for k in UH}
        inter = {k: jnp.exp(m_inter[k] - m_t[k]) for k in UH}
        w_v = {k: jnp.dot(w[k].astype(BF16), vv[k], preferred_element_type=F32) for k in UH}
        ws = {k: jnp.exp(decay[k] - m_new[k]) for k in UH}
        cscale = {k: jnp.exp(b_last[k] + m_prev[k] - m_new[k]) for k in UH}
        wv = {k: (ws[k] * vv[k].astype(F32)).astype(BF16) for k in UH}
        k_wv = {k: jnp.dot(kt[k], wv[k], preferred_element_type=F32) for k in UH}
        k_ws = {k: jnp.sum(ws[k] * kk[k].astype(F32), axis=0, keepdims=True) for k in UH}
        w_sum = {k: jnp.sum(w[k], axis=1, keepdims=True) for k in UH}
        ct = {h: ct_ref[h] for h in HR}
        nvec = {h: n_ref[h] for h in HR}
        for u, h in UH:
            k = (u, h)
            num = inter[k] * jnp.dot(qq[k], ct[h].astype(BF16), preferred_element_type=F32) + w_v[k]
            den = inter[k] * jnp.sum(qq[k].astype(F32) * nvec[h], axis=1, keepdims=True) + w_sum[k]
            hval = num / jnp.maximum(jnp.abs(den), jnp.exp(-m_t[k]))
            ct[h] = cscale[k] * ct[h] + k_wv[k]
            nvec[h] = cscale[k] * nvec[h] + k_ws[k]
            gate = jax.nn.sigmoid(og_ref[0, rows[u], cols[h]].astype(F32))
            out_ref[0, rows[u], cols[h]] = (gate * hval).astype(out_ref.dtype)
        for h in HR:
            ct_ref[h] = ct[h]
            n_ref[h] = nvec[h]
            m_ref[h] = jnp.broadcast_to(m_new[ML_UNROLL - 1, h], (1, LANES))
        return carry

    lax.fori_loop(0, ts // (L * ML_UNROLL), chunks, 0)


def mlstm(y3, small3, conv_w, conv_b, gate_b, ts=512):
    b, s, _ = y3.shape
    W, H, L = MIX_WIDTH, ML_HEADS, ML_CHUNK
    gr = small3[:, :, S_ML_I:S_ML_I + 2 * H].reshape(b, s // L, L, 2 * H).transpose(0, 1, 3, 2)
    cq, ck, cv, co = C_ML_Q // W, C_ML_K // W, C_ML_V // W, C_ML_O // W
    return pl.pallas_call(
        functools.partial(_mlstm_kernel, ts=ts),
        out_shape=jax.ShapeDtypeStruct((b, s, W), BF16),
        grid=(b, s // ts),
        in_specs=[pl.BlockSpec((1, ts, W), lambda i, j: (i, j, cq)),
                  pl.BlockSpec((1, ts, W), lambda i, j: (i, j, ck)),
                  pl.BlockSpec((1, ts, W), lambda i, j: (i, j, cv)),
                  pl.BlockSpec((1, ts, W), lambda i, j: (i, j, co)),
                  pl.BlockSpec((1, ts, N_SMALL), lambda i, j: (i, j, 0)),
                  pl.BlockSpec((1, ts // L, 2 * H, L), lambda i, j: (i, j, 0, 0)),
                  pl.BlockSpec((CONV_WIDTH, 2 * W), lambda i, j: (0, 0)),
                  pl.BlockSpec((1, 2 * W), lambda i, j: (0, 0)),
                  pl.BlockSpec(memory_space=pltpu.SMEM)],
        out_specs=pl.BlockSpec((1, ts, W), lambda i, j: (i, j, 0)),
        scratch_shapes=[pltpu.VMEM((H, ML_HEAD_DIM, ML_HEAD_DIM), F32),
                        pltpu.VMEM((H, 1, ML_HEAD_DIM), F32),
                        pltpu.VMEM((H, 1, LANES), F32),
                        pltpu.VMEM((ts + 8, 2 * W), F32),
                        pltpu.VMEM((ts, 2 * W), BF16)],
        compiler_params=_params("parallel", "arbitrary"),
        name="mlstm",
    )(y3, y3, y3, y3, small3, gr, conv_w, conv_b.reshape(1, 2 * W), gate_b)


def _gelu_tanh(x):
    return 0.5 * x * (1.0 + jnp.tanh(0.7978845608028654 * (x + 0.044715 * (x * x * x))))


def _compress_kernel(ra_ref, rb_ref, pos_ref, w1_ref, b1_ref, w2_ref, b2_ref, kn_ref, kc_ref, vc_ref):
    half = (CMP_BLOCK // 2) * HEAD_DIM
    for j, o_ref in enumerate((kc_ref, vc_ref)):
        xa = (ra_ref[j, 0, 0].astype(F32) + pos_ref[j, :, 0:half]).astype(BF16)
        xb = (rb_ref[j, 0, 0].astype(F32) + pos_ref[j, :, half:2 * half]).astype(BF16)
        hid = (jnp.dot(xa, w1_ref[j, 0:half, :], preferred_element_type=F32)
               + jnp.dot(xb, w1_ref[j, half:2 * half, :], preferred_element_type=F32) + b1_ref[j])
        out = jnp.dot(_gelu_tanh(hid).astype(BF16), w2_ref[j], preferred_element_type=F32) + b2_ref[j]
        if j == 0:
            out = out * lax.rsqrt(jnp.mean(out * out, axis=-1, keepdims=True) + EPS) * kn_ref[...]
        o_ref[0, 0] = out


def nsa_compress(y3, cmp_pos, cmp_w1, cmp_b1, cmp_w2, cmp_b2, k_norm0):
    b, s, _ = y3.shape
    G, dh = NSA_KV_HEADS, HEAD_DIM
    nr = s // CMP_STRIDE
    wide = CMP_STRIDE * dh
    kv = y3[:, :, C_NSA_KV:C_NSA_KV + 2 * G * dh].reshape(b, s, 2, G, dh)
    ra = kv.transpose(2, 0, 3, 1, 4).reshape(2, b, G, nr, wide)
    rb = jnp.concatenate([ra[:, :, :, 1:], jnp.zeros((2, b, G, 1, wide), ra.dtype)], axis=3)
    hidden = cmp_w1.shape[-1]
    out = jax.ShapeDtypeStruct((b, G, nr, dh), F32)
    blk = pl.BlockSpec((2, 1, 1, nr, wide), lambda i, g: (0, i, g, 0, 0))
    oblk = pl.BlockSpec((1, 1, nr, dh), lambda i, g: (i, g, 0, 0))

    def full(shape):
        return pl.BlockSpec(shape, lambda i, g: (0,) * len(shape))

    return pl.pallas_call(
        _compress_kernel,
        out_shape=(out, out),
        grid=(b, G),
        in_specs=[blk, blk, full((2, 1, 2 * wide)), full((2, 2 * wide, hidden)), full((2, 1, hidden)),
                  full((2, hidden, dh)), full((2, 1, dh)), full((1, dh))],
        out_specs=(oblk, oblk),
        compiler_params=_params("parallel", "parallel"),
        name="nsa_compress",
    )(ra, rb, cmp_pos.reshape(2, 1, 2 * wide), cmp_w1.astype(BF16), cmp_b1.reshape(2, 1, hidden),
      cmp_w2.astype(BF16), cmp_b2.reshape(2, 1, dh), k_norm0.reshape(1, dh))


NSA_QB = 128
NSA_KS = 512
NSA_CK = 256
NSA_VR = 80
A_FEAT, A_PEN, A_BIAS = 0, 64, 128
A_DUMMY = A_BIAS + 4


def _head_rms(x, gain):
    w = x.shape[1]
    same_head = (_iota((w, w), 0) // HEAD_DIM == _iota((w, w), 1) // HEAD_DIM).astype(BF16)
    ss = _split_dot(x * x, same_head)
    return x * lax.rsqrt(ss * (1.0 / HEAD_DIM) + EPS) * gain


def _nsa_kernel(q_ref, gl_ref, kca_ref, vcT_ref, ks_ref, kw_ref, vs_ref, vw_ref, kconst_ref, gq_ref, gk_ref,
                     o_ref, qt_ref, ksa_ref, kwa_ref, vsa_ref, vwa_ref, s_ref, *, n_sel, top):
    QB, R, dh, CK, VR, KS = NSA_QB, NSA_GROUP, HEAD_DIM, NSA_CK, NSA_VR, NSA_KS
    G = NSA_KV_HEADS
    GR = range(G)
    HQ = R * QB
    NPAD = WINDOW // QB
    SUB = KS // QB
    qi = pl.program_id(1)
    q0 = qi * QB
    nkb = vsa_ref.shape[1]

    @pl.when(qi == 0)
    def _():
        pad_keys = jnp.where(_iota((WINDOW, CK), 1) == A_DUMMY, 1.0, 0.0).astype(BF16)
        ones_rows = jnp.where(_iota((nkb + NPAD, VR - dh, QB), 1) == 0, 1.0, 0.0).astype(BF16)
        for g in GR:
            heads = slice(g * dh, (g + 1) * dh)
            ksa_ref[g] = kconst_ref[...]
            kwa_ref[g, 0:WINDOW, :] = pad_keys
            kwa_ref[g, WINDOW:, :] = kconst_ref[...]
            kwa_ref[g, WINDOW:, A_PEN:A_PEN + 64] = jnp.zeros((kwa_ref.shape[1] - WINDOW, 64), BF16)
            vwa_ref[g, 0:NPAD, 0:dh, :] = jnp.zeros((NPAD, dh, QB), BF16)
            vsa_ref[g, :, dh:VR, :] = ones_rows[0:nkb]
            vwa_ref[g, :, dh:VR, :] = ones_rows
        for c in range(ks_ref.shape[1] // KS):
            keys = slice(c * KS, (c + 1) * KS)
            ks_n = _head_rms(ks_ref[0, keys, :].astype(F32), gk_ref[0]).astype(BF16)
            kw_n = _head_rms(kw_ref[0, keys, :].astype(F32), gk_ref[1]).astype(BF16)
            for g in GR:
                heads = slice(g * dh, (g + 1) * dh)
                ksa_ref[g, keys, A_FEAT:A_FEAT + dh] = ks_n[:, heads]
                kwa_ref[g, WINDOW + c * KS:WINDOW + (c + 1) * KS, A_FEAT:A_FEAT + dh] = kw_n[:, heads]
        for c in range(nkb):
            keys = slice(c * QB, (c + 1) * QB)
            vs_t = vs_ref[0, keys, :].astype(F32).T.astype(BF16)
            vw_t = vw_ref[0, keys, :].astype(F32).T.astype(BF16)
            for g in GR:
                heads = slice(g * dh, (g + 1) * dh)
                vsa_ref[g, c, 0:dh, :] = vs_t[heads]
                vwa_ref[g, NPAD + c, 0:dh, :] = vw_t[heads]
        qt_ref[:, A_BIAS + 16:CK, :] = jnp.zeros((G, CK - A_BIAS - 16, HQ), BF16)

    q_n = _head_rms(q_ref[0].astype(F32), gq_ref[...]).astype(BF16)
    q_rows = (q_n.astype(F32) * (dh ** -0.5)).T
    lane = _iota((16, HQ), 1)
    rowi = _iota((16, HQ), 0)
    t_q = q0 + (lane & (QB - 1))
    t_hi = ((t_q >> 6) << 6).astype(F32)
    t_lo = (t_q & 63).astype(F32)
    for g in GR:
        qT = jnp.concatenate([q_rows[(g * R + r) * dh:(g * R + r + 1) * dh] for r in range(R)],
                             axis=1).astype(BF16)
        qt_ref[g, A_FEAT:A_FEAT + dh, :] = qT
        qt_ref[g, A_PEN:A_PEN + dh, :] = qT
        slope = jnp.exp2(-(g * R + (lane >> 7) + 1).astype(F32))
        bias_rows = jnp.where(rowi < 2, slope,
                              jnp.where(rowi == 2, -slope * t_hi,
                                        jnp.where(rowi == 3, -slope * t_lo,
                                                  jnp.where(rowi == A_DUMMY - A_BIAS, NEG, 0.0))))
        qt_ref[g, A_BIAS:A_BIAS + 16, :] = bias_rows.astype(BF16)
    k_loc = _iota((QB, HQ), 0)
    q_loc = _iota((QB, HQ), 1) & (QB - 1)

    def pv(v_ref_, g, kb0, pr):
        out = None
        for i in range(pr.shape[0] // QB):
            term = jnp.dot(v_ref_[g, kb0 + i], pr[i * QB:(i + 1) * QB], preferred_element_type=F32)
            out = term if out is None else out + term
        return out

    n_cmp = kca_ref.shape[2]
    cmp_end = _iota((n_cmp, HQ), 0) * CMP_STRIDE + (CMP_BLOCK - 1)
    valid = cmp_end <= q0 + (_iota((n_cmp, HQ), 1) & (QB - 1))
    sc = [jnp.where(valid, jnp.dot(kca_ref[0, g], qt_ref[g], preferred_element_type=F32), NEG) for g in GR]
    mx = [jnp.max(sc[g], axis=0, keepdims=True) for g in GR]
    e = [jnp.exp(sc[g] - mx[g]) for g in GR]
    inv = [jnp.where(mx[g] > 0.5 * NEG, 1.0 / jnp.sum(e[g], axis=0, keepdims=True), 0.0) for g in GR]
    p = [e[g] * inv[g] for g in GR]
    o_cmp = [jnp.dot(vcT_ref[0, g], p[g].astype(BF16), preferred_element_type=F32) for g in GR]

    win_rows = pl.ds(pl.multiple_of(q0, QB), WINDOW + QB)
    sw = [jnp.dot(kwa_ref[g, win_rows, :], qt_ref[g], preferred_element_type=F32) for g in GR]
    sw = [jnp.concatenate([jnp.where(k_loc > q_loc, sw[g][0:QB], NEG), sw[g][QB:WINDOW],
                           jnp.where(k_loc <= q_loc, sw[g][WINDOW:WINDOW + QB], NEG)], axis=0) for g in GR]
    pw = [jnp.exp(sw[g] - jnp.max(sw[g], axis=0, keepdims=True)).astype(BF16) for g in GR]
    acc_w = [pv(vwa_ref, g, qi, pw[g]) for g in GR]
    o_win = [acc_w[g][0:dh] / acc_w[g][dh:dh + 1] for g in GR]

    c0 = _iota((n_sel, n_cmp), 1) * CMP_STRIDE
    s0 = _iota((n_sel, n_cmp), 0) * SEL_BLOCK
    overlap_t = ((c0 < s0 + SEL_BLOCK) & (c0 + CMP_BLOCK > s0)).astype(BF16)
    j_idx = _iota((n_sel, QB), 0)
    tq = q0 + _iota((n_sel, QB), 1)
    cur = tq >> 6
    forced = (j_idx == 0) | (j_idx == cur) | (j_idx == cur - 1)
    causal_blk = j_idx * SEL_BLOCK <= tq
    p_grp = [p[g][:, 0:QB] + p[g][:, QB:2 * QB] + p[g][:, 2 * QB:3 * QB] + p[g][:, 3 * QB:4 * QB] for g in GR]
    imp = [jnp.where(causal_blk, _split_dot_left(overlap_t, p_grp[g]) + jnp.where(forced, FORCE_BONUS, 0.0), -1.0)
           for g in GR]
    sel = [jnp.zeros((n_sel, QB), F32) for g in GR]
    for _ in range(top):
        for g in GR:
            mx = jnp.max(imp[g], axis=0, keepdims=True)
            first = jnp.min(jnp.where(imp[g] == mx, j_idx, n_sel), axis=0, keepdims=True)
            pick = j_idx == first
            sel[g] = jnp.where(pick, 1.0, sel[g])
            imp[g] = jnp.where(pick, -3e38, imp[g])
    for g in GR:
        pen = jnp.where((sel[g] > 0.5) & causal_blk, 0.0, NEG)
        if n_sel < 64:
            pen = jnp.concatenate([pen, jnp.zeros((64 - n_sel, QB), F32)], axis=0)
        qt_ref[g, A_PEN:A_PEN + 64, :] = jnp.concatenate([pen] * R, axis=1).astype(BF16)

    def score(j):
        rows = pl.ds(pl.multiple_of(j * KS, KS), KS)
        return [jnp.dot(ksa_ref[g, rows, :], qt_ref[g], preferred_element_type=F32) for g in GR]

    def absorb(s, j, st):
        m_new = [jnp.maximum(st[g][0], jnp.max(s[g], axis=0, keepdims=True)) for g in GR]
        pr = [jnp.exp(s[g] - m_new[g]).astype(BF16) for g in GR]
        return [(m_new[g], jnp.exp(st[g][0] - m_new[g]) * st[g][1] + pv(vsa_ref, g, j * SUB, pr[g])) for g in GR]

    n_full = qi // SUB
    init = [(jnp.full((1, HQ), NEG, F32), jnp.zeros((VR, HQ), F32)) for g in GR]
    st = lax.fori_loop(0, n_full, lambda j, st_: absorb(score(j), j, st_), init)
    s_last = score(n_full)
    diag = pl.ds(pl.multiple_of(q0 - n_full * KS, QB), QB)
    for g in GR:
        s_ref[g] = s_last[g]
        s_ref[g, diag, :] = jnp.where(k_loc <= q_loc, s_ref[g, diag, :], NEG)
    st = absorb([s_ref[g] for g in GR], n_full, st)
    o_sel = [st[g][1][0:dh] / st[g][1][dh:dh + 1] for g in GR]

    gl_t = gl_ref[0].T
    rows_out = []
    for g in GR:
        gate = [jax.nn.sigmoid(jnp.concatenate(
            [gl_t[(g * R + r) * N_BRANCH + br:(g * R + r) * N_BRANCH + br + 1] for r in range(R)], axis=1))
            for br in range(N_BRANCH)]
        o_t = gate[0] * o_cmp[g] + gate[1] * o_sel[g] + gate[2] * o_win[g]
        rows_out += [o_t[:, r * QB:(r + 1) * QB] for r in range(R)]
    o_ref[0] = jnp.concatenate(rows_out, axis=0).T.astype(o_ref.dtype)


def nsa_attention(y3, small3, q_norm, k_norm, kc, vc):
    b, s, _ = y3.shape
    G, R, dh, QB, CK, VR = NSA_KV_HEADS, NSA_GROUP, HEAD_DIM, NSA_QB, NSA_CK, NSA_VR
    assert G == 2, "the kernel picks a kv head's gate rows with a two-way select"
    HQ = R * QB
    nq = s // QB
    n_sel = s // SEL_BLOCK
    assert n_sel <= 64, "selection one-hot columns hold at most 64 blocks"
    top = min(SEL_TOPK, n_sel)
    n_cmp = kc.shape[2]

    def pos_cols(pos):
        return np.stack([pos // 64 * 64, pos % 64, np.ones_like(pos), np.ones_like(pos)], axis=1)

    vc_t = vc.transpose(0, 1, 3, 2).astype(BF16)

    pos = np.arange(s)
    kconst = np.zeros((s, CK), np.float32)
    kconst[pos, A_PEN + pos // SEL_BLOCK] = 1.0
    kconst[:, A_BIAS:A_BIAS + 4] = pos_cols(pos)
    kconst = jnp.asarray(kconst, BF16)

    kc_hi = kc.astype(BF16)
    kc_lo = (kc - kc_hi.astype(F32)).astype(BF16)
    cend = np.arange(n_cmp) * CMP_STRIDE + (CMP_BLOCK - 1)
    cbias = np.zeros((n_cmp, CK - 2 * dh), np.float32)
    cbias[:, 0:4] = pos_cols(cend)
    kc_aug = jnp.concatenate([kc_hi, kc_lo, jnp.broadcast_to(jnp.asarray(cbias, BF16), (b, G, n_cmp, CK - 2 * dh))],
                             axis=-1)

    qw = G * R * dh
    kvb = C_NSA_KV // LANES
    gq = jnp.tile(q_norm, G * R).reshape(1, qw)
    gk = jnp.stack([jnp.tile(k_norm[1], G), jnp.tile(k_norm[2], G)]).reshape(2, 1, LANES)

    def kv_spec(blk):
        return pl.BlockSpec((1, s, LANES), lambda i, j: (i, 0, kvb + blk))

    return pl.pallas_call(
        functools.partial(_nsa_kernel, n_sel=n_sel, top=top),
        out_shape=jax.ShapeDtypeStruct((b, s, MIX_WIDTH), BF16),
        grid=(b, nq),
        in_specs=[pl.BlockSpec((1, QB, qw), lambda i, j: (i, j, C_NSA_Q // qw)),
                  pl.BlockSpec((1, QB, N_SMALL), lambda i, j: (i, j, 0)),
                  pl.BlockSpec((1, G, n_cmp, CK), lambda i, j: (i, 0, 0, 0)),
                  pl.BlockSpec((1, G, dh, n_cmp), lambda i, j: (i, 0, 0, 0)),
                  kv_spec(2), kv_spec(4), kv_spec(3), kv_spec(5),
                  pl.BlockSpec((s, CK), lambda i, j: (0, 0)),
                  pl.BlockSpec((1, qw), lambda i, j: (0, 0)),
                  pl.BlockSpec((2, 1, LANES), lambda i, j: (0, 0, 0))],
        out_specs=pl.BlockSpec((1, QB, qw), lambda i, j: (i, j, 0)),
        scratch_shapes=[pltpu.VMEM((G, CK, HQ), BF16),
                        pltpu.VMEM((G, s, CK), BF16), pltpu.VMEM((G, s + WINDOW, CK), BF16),
                        pltpu.VMEM((G, nq, VR, QB), BF16), pltpu.VMEM((G, nq + WINDOW // QB, VR, QB), BF16),
                        pltpu.VMEM((G, NSA_KS, HQ), F32)],
        compiler_params=_params("parallel", "arbitrary"),
        name="nsa_attention",
    )(y3, small3, kc_aug, vc_t, y3, y3, y3, y3, kconst, gq, gk)


def _merge_kernel(on_ref, os_ref, om_ref, g0_ref, g1_ref, g2_ref, wb_ref, wo_ref, x_ref, mod_ref, o_ref):
    merged = None
    for i, (o_r, g_r) in enumerate(((on_ref, g0_ref), (os_ref, g1_ref), (om_ref, g2_ref))):
        br = jnp.dot(o_r[0], wb_ref[i], preferred_element_type=F32)
        term = jax.nn.sigmoid(g_r[0].astype(F32)) * br
        merged = term if merged is None else merged + term
    out = jnp.dot(merged.astype(BF16), wo_ref[...], preferred_element_type=F32)
    o_ref[0] = x_ref[0] + mod_ref[0, 2:3, :] * out


def merge_project(o_nsa, o_sb, o_ml, y3, w_branch, w_out, x, mod, tm=512):
    b, s, d = x.shape
    W = MIX_WIDTH
    ospec = pl.BlockSpec((1, tm, W), lambda i, j: (i, j, 0))
    xspec = pl.BlockSpec((1, tm, d), lambda i, j: (i, j, 0))
    gspecs = [pl.BlockSpec((1, tm, d), functools.partial(lambda i, j, c: (i, j, c), c=C_MERGE // d + c))
              for c in range(N_BRANCH)]
    return pl.pallas_call(
        _merge_kernel,
        out_shape=jax.ShapeDtypeStruct((b, s, d), F32),
        grid=(b, s // tm),
        in_specs=[ospec, ospec, ospec] + gspecs + [
            pl.BlockSpec((N_BRANCH, W, d), lambda i, j: (0, 0, 0)),
            pl.BlockSpec((d, d), lambda i, j: (0, 0)),
            xspec,
            pl.BlockSpec((1, 6, d), lambda i, j: (i, 0, 0))],
        out_specs=xspec,
        compiler_params=_params("parallel", "parallel"),
        name="merge_project",
    )(o_nsa, o_sb, o_ml, y3, y3, y3, w_branch.astype(BF16), w_out.astype(BF16), x, mod)


def _ffn_kernel(x_ref, g_ref, mod_ref, wg_ref, wu_ref, wd_ref, o_ref, h_ref, acc_ref):
    f = pl.program_id(2)

    @pl.when(f == 0)
    def _():
        h_ref[...] = _norm_mod(x_ref[0], g_ref[...], mod_ref[0], 3, 4).astype(BF16)
        acc_ref[...] = jnp.zeros_like(acc_ref)

    h = h_ref[...]
    a = jnp.dot(h, wg_ref[...], preferred_element_type=F32)
    u = jnp.dot(h, wu_ref[...], preferred_element_type=F32)
    act = (a * jax.nn.sigmoid(a) * u).astype(BF16)
    acc_ref[...] += jnp.dot(act, wd_ref[...], preferred_element_type=F32)

    @pl.when(f == pl.num_programs(2) - 1)
    def _():
        o_ref[0] = x_ref[0] + mod_ref[0, 5:6, :] * acc_ref[...]


def dense_ffn(x, g, mod, wg, wu, wd, tm=512, n_ftiles=2):
    b, s, d = x.shape
    ff = wg.shape[1]
    tf = -(-ff // (n_ftiles * LANES)) * LANES
    pad = n_ftiles * tf - ff
    wg = jnp.pad(to_bf16(wg), ((0, 0), (0, pad)))
    wu = jnp.pad(to_bf16(wu), ((0, 0), (0, pad)))
    wd = jnp.pad(to_bf16(wd), ((0, pad), (0, 0)))
    xspec = pl.BlockSpec((1, tm, d), lambda i, j, f: (i, j, 0))
    return pl.pallas_call(
        _ffn_kernel,
        out_shape=jax.ShapeDtypeStruct((b, s, d), F32),
        grid=(b, s // tm, n_ftiles),
        in_specs=[xspec,
                  pl.BlockSpec((1, d), lambda i, j, f: (0, 0)),
                  pl.BlockSpec((1, 6, d), lambda i, j, f: (i, 0, 0)),
                  pl.BlockSpec((d, tf), lambda i, j, f: (0, f)),
                  pl.BlockSpec((d, tf), lambda i, j, f: (0, f)),
                  pl.BlockSpec((tf, d), lambda i, j, f: (f, 0))],
        out_specs=xspec,
        scratch_shapes=[pltpu.VMEM((tm, d), BF16), pltpu.VMEM((tm, d), F32)],
        compiler_params=_params("parallel", "parallel", "arbitrary"),
        name="dense_ffn",
    )(x, g.reshape(1, d), mod, wg, wu, wd)


def _router_kernel(x_ref, g_ref, mod_ref, wr_ref, h_ref, e_ref, p_ref):
    h = _norm_mod(x_ref[0], g_ref[...], mod_ref[0], 3, 4)
    h_ref[...] = h
    lane = _iota((1, LANES), 1)
    real = lane < N_EXPERTS
    logits = jnp.where(real, jnp.dot(h, wr_ref[...], precision=HIGHEST, preferred_element_type=F32), NEG)
    e = jnp.exp(logits - jnp.max(logits, axis=1, keepdims=True))
    p = jnp.where(real, e / jnp.sum(e, axis=1, keepdims=True), -1.0)
    p1 = jnp.max(p, axis=1, keepdims=True)
    i1 = jnp.min(jnp.where(p == p1, lane, LANES), axis=1, keepdims=True)
    rest = jnp.where(lane == i1, -1.0, p)
    p2 = jnp.max(rest, axis=1, keepdims=True)
    i2 = jnp.min(jnp.where(rest == p2, lane, LANES), axis=1, keepdims=True)
    tot = p1 + p2
    e_ref[...] = jnp.where(lane == 0, i1, jnp.where(lane == 1, i2, 0))[:, 0:N_EXPERTS]
    p_ref[...] = jnp.where(lane == 0, p1 / tot, jnp.where(lane == 1, p2 / tot, 0.0))[:, 0:N_EXPERTS]


def moe_router(x, g, mod, w_router, tm=512):
    b, s, d = x.shape
    t = b * s
    spb = s // tm
    wr = jnp.pad(w_router, ((0, 0), (0, LANES - N_EXPERTS)))
    return pl.pallas_call(
        _router_kernel,
        out_shape=(jax.ShapeDtypeStruct((t, d), F32),
                   jax.ShapeDtypeStruct((t, N_EXPERTS), I32),
                   jax.ShapeDtypeStruct((t, N_EXPERTS), F32)),
        grid=(b, spb),
        in_specs=[pl.BlockSpec((1, tm, d), lambda i, j: (i, j, 0)),
                  pl.BlockSpec((1, d), lambda i, j: (0, 0)),
                  pl.BlockSpec((1, 6, d), lambda i, j: (i, 0, 0)),
                  pl.BlockSpec((d, LANES), lambda i, j: (0, 0))],
        out_specs=(pl.BlockSpec((tm, d), lambda i, j: (i * spb + j, 0)),
                   pl.BlockSpec((tm, N_EXPERTS), lambda i, j: (i * spb + j, 0)),
                   pl.BlockSpec((tm, N_EXPERTS), lambda i, j: (i * spb + j, 0))),
        compiler_params=_params("parallel", "parallel"),
        name="moe_router",
    )(x, g.reshape(1, d), mod, wr)


def moe_ffn(x, g, mod, w_router, wg, wu, wd, tb=512):
    b, s, d = x.shape
    t = b * s
    a = t * TOP_K
    h, top_e, top_p = moe_router(x, g, mod, w_router)
    e_flat = top_e[:, 0:TOP_K].reshape(a)
    onehot = (e_flat[:, None] == jnp.arange(N_EXPERTS, dtype=I32)[None, :]).astype(I32)
    csum = jnp.cumsum(onehot, axis=0)
    rank = jnp.sum(onehot * csum, axis=1) - 1
    counts = csum[-1]
    padded = (counts + tb - 1) // tb * tb
    pad_ends = jnp.cumsum(padded)
    pad_starts = pad_ends - padded
    dest = (jnp.sum(onehot * pad_starts[None, :], axis=1) + rank).astype(I32)
    n_rows = (a // tb + N_EXPERTS + 1) * tb
    n_blk = n_rows // tb
    blk_expert = jnp.minimum(
        jnp.searchsorted(pad_ends, jnp.arange(n_blk, dtype=I32) * tb, side="right"), N_EXPERTS - 1).astype(I32)
    n_used = (pad_ends[-1:] // tb).astype(I32)
    pad_lo = jnp.concatenate([pad_starts + counts, pad_ends[-1:]]).astype(I32)
    pad_hi = jnp.concatenate([pad_ends, jnp.full((1,), n_rows, I32)]).astype(I32)
    slot_assign = moe_invert(dest, pad_lo, pad_hi, n_rows, tb)
    y2 = moe_experts(h, slot_assign, blk_expert, n_used, to_bf16(wg), to_bf16(wu), to_bf16(wd), tb)
    return moe_mix(y2, x, top_p, mod)


def _invert_kernel(dest_ref, lo_ref, hi_ref, sa_ref, *, n_assign, tb, n_chunks):
    phase = pl.program_id(0)
    chunk = pl.program_id(1)

    @pl.when((phase == 0) & (chunk < lo_ref.shape[0]))
    def _():
        def fill(p, c):
            sa_ref[p] = n_assign + (p & (2 * tb - 1))
            return c

        lax.fori_loop(lo_ref[chunk], hi_ref[chunk], fill, 0)

    @pl.when(phase == 1)
    def _():
        per = n_assign // n_chunks

        def put(j, c):
            a = chunk * per + j
            sa_ref[dest_ref[a]] = a
            return c

        lax.fori_loop(0, per, put, 0, unroll=8)


def moe_invert(dest, pad_lo, pad_hi, n_rows, tb, n_chunks=16):
    n_assign = dest.shape[0]
    assert tb & (tb - 1) == 0 and n_assign % n_chunks == 0 and pad_lo.shape[0] <= n_chunks
    smem = pl.BlockSpec(memory_space=pltpu.SMEM)
    return pl.pallas_call(
        functools.partial(_invert_kernel, n_assign=n_assign, tb=tb, n_chunks=n_chunks),
        out_shape=jax.ShapeDtypeStruct((n_rows,), I32),
        grid=(2, n_chunks),
        in_specs=[smem, smem, smem],
        out_specs=smem,
        compiler_params=pltpu.CompilerParams(dimension_semantics=("arbitrary", "arbitrary")),
        name="moe_invert",
    )(dest, pad_lo, pad_hi)


def _expert_kernel(be_ref, nu_ref, sa_ref, h_hbm, wg_ref, wu_ref, wd_ref, y_hbm,
                   xin_ref, yout_ref, xb_ref, acc_ref, sem_in, sem_out, *, tb, n_tok, n_f):
    i = pl.program_id(0)
    f = pl.program_id(1)
    n_used = nu_ref[0]
    n_assign = n_tok * TOP_K
    rows_f = tb // n_f
    active = i <= n_used

    def gather_row(blk, r):
        a = sa_ref[blk * tb + r]
        tok = jnp.where(a < n_assign, a >> 1, 0)
        return pltpu.make_async_copy(h_hbm.at[tok], xin_ref.at[blk % 2, r], sem_in.at[blk % 2])

    def scatter_row(blk, r):
        a = jnp.where(blk >= 0, sa_ref[jnp.maximum(blk, 0) * tb + r], n_assign + tb + r)
        row = jnp.where(a < n_assign, (a & 1) * n_tok + (a >> 1), a)
        return pltpu.make_async_copy(yout_ref.at[(blk + 2) % 2, r], y_hbm.at[row], sem_out.at[(blk + 2) % 2])

    def for_rows(fn):
        def body(r, c):
            fn(r)
            return c
        lax.fori_loop(0, tb, body, 0, unroll=8)

    @pl.when(f == 0)
    def _():
        @pl.when(i == 0)
        def _():
            yout_ref[1] = jnp.zeros((tb, yout_ref.shape[2]), F32)
            for_rows(lambda r: gather_row(0, r).start())

            def clear_row(r):
                return pltpu.make_async_copy(yout_ref.at[1, r], y_hbm.at[n_assign + r], sem_out.at[0])

            for_rows(lambda r: clear_row(r).start())
            for_rows(lambda r: clear_row(r).wait())

        @pl.when((i == 0) | (i - 1 <= n_used))
        def _():
            for_rows(lambda r: gather_row(i, r).wait())

        @pl.when(active)
        def _():
            xb_ref[...] = xin_ref[i % 2].astype(BF16)
            acc_ref[...] = jnp.zeros_like(acc_ref)

    @pl.when(active)
    def _():
        for r in range(rows_f):
            gather_row(i + 1, f * rows_f + r).start()
            scatter_row(i - 1, f * rows_f + r).start()
        xb = xb_ref[...]
        a = jnp.dot(xb, wg_ref[0], preferred_element_type=F32)
        u = jnp.dot(xb, wu_ref[0], preferred_element_type=F32)
        act = (a * jax.nn.sigmoid(a) * u).astype(BF16)
        acc_ref[...] += jnp.dot(act, wd_ref[0], preferred_element_type=F32)

    @pl.when(f == n_f - 1)
    def _():
        @pl.when((i >= 1) & (i - 1 <= n_used))
        def _():
            for_rows(lambda r: scatter_row(i - 2, r).wait())

        @pl.when(active)
        def _():
            yout_ref[i % 2] = acc_ref[...]


def moe_experts(h, slot_assign, blk_expert, n_used, wg, wu, wd, tb, tf=1792):
    n_tok, d = h.shape
    p = slot_assign.shape[0]
    ff = wg.shape[2]
    return pl.pallas_call(
        functools.partial(_expert_kernel, tb=tb, n_tok=n_tok, n_f=ff // tf),
        out_shape=jax.ShapeDtypeStruct((n_tok * TOP_K + 2 * tb, d), F32),
        grid_spec=pltpu.PrefetchScalarGridSpec(
            num_scalar_prefetch=3,
            grid=(p // tb, ff // tf),
            in_specs=[pl.BlockSpec(memory_space=pl.ANY),
                      pl.BlockSpec((1, d, tf), lambda i, f, be, nu, sa: (be[i], 0, f)),
                      pl.BlockSpec((1, d, tf), lambda i, f, be, nu, sa: (be[i], 0, f)),
                      pl.BlockSpec((1, tf, d), lambda i, f, be, nu, sa: (be[i], f, 0))],
            out_specs=pl.BlockSpec(memory_space=pl.ANY),
            scratch_shapes=[pltpu.VMEM((2, tb, d), F32), pltpu.VMEM((2, tb, d), F32),
                            pltpu.VMEM((tb, d), BF16), pltpu.VMEM((tb, d), F32),
                            pltpu.SemaphoreType.DMA((2,)), pltpu.SemaphoreType.DMA((2,))]),
        compiler_params=pltpu.CompilerParams(dimension_semantics=("arbitrary", "arbitrary"),
                                             vmem_limit_bytes=VMEM_LIMIT, has_side_effects=True),
        name="moe_experts",
    )(blk_expert, n_used, slot_assign, h, wg, wu, wd)


def _mix_kernel(y0_ref, y1_ref, x_ref, p_ref, mod_ref, o_ref):
    w = p_ref[...]
    f = w[:, 0:1] * y0_ref[...] + w[:, 1:2] * y1_ref[...]
    o_ref[0] = x_ref[0] + mod_ref[0, 5:6, :] * f


def moe_mix(y2, x, top_p, mod, td=512):
    b, s, d = x.shape
    spb = s // td
    nt = b * spb
    return pl.pallas_call(
        _mix_kernel,
        out_shape=jax.ShapeDtypeStruct((b, s, d), F32),
        grid=(b, spb),
        in_specs=[pl.BlockSpec((td, d), lambda i, j: (i * spb + j, 0)),
                  pl.BlockSpec((td, d), lambda i, j: (nt + i * spb + j, 0)),
                  pl.BlockSpec((1, td, d), lambda i, j: (i, j, 0)),
                  pl.BlockSpec((td, N_EXPERTS), lambda i, j: (i * spb + j, 0)),
                  pl.BlockSpec((1, 6, d), lambda i, j: (i, 0, 0))],
        out_specs=pl.BlockSpec((1, td, d), lambda i, j: (i, j, 0)),
        compiler_params=_params("parallel", "parallel"),
        name="moe_mix",
    )(y2, y2, x, top_p, mod)


def _pack_w_in(w_in):
    kv = 2 * NSA_KV_HEADS * HEAD_DIM * 3
    w_in = to_bf16(w_in)
    o = 0
    nsa_q = w_in[:, o:o + MIX_WIDTH]; o += MIX_WIDTH
    nsa_kv = w_in[:, o:o + kv]; o += kv
    nsa_gate = w_in[:, o:o + NSA_HEADS * N_BRANCH]; o += NSA_HEADS * N_BRANCH
    sb = w_in[:, o:o + 3 * MIX_WIDTH]; o += 3 * MIX_WIDTH
    ml_qkv = w_in[:, o:o + 3 * MIX_WIDTH]; o += 3 * MIX_WIDTH
    ml_if = w_in[:, o:o + 2 * ML_HEADS]; o += 2 * ML_HEADS
    ml_o = w_in[:, o:o + MIX_WIDTH]; o += MIX_WIDTH
    merge = w_in[:, o:]
    main = jnp.concatenate([merge, nsa_q, ml_qkv, ml_o, sb, nsa_kv], axis=1)
    small = jnp.concatenate([nsa_gate, ml_if], axis=1)
    small = jnp.pad(small, ((0, 0), (0, N_SMALL - small.shape[1])))
    return main, small


def token_mixer_layer(x, mod, norm_g, w_in, nsa_q_norm, nsa_k_norm, cmp_pos, cmp_w1, cmp_b1, cmp_w2,
                      cmp_b2, ml_conv_w, ml_conv_b, ml_gate_b, w_branch, w_out):
    w_main, w_small = _pack_w_in(w_in)
    y3, small3 = in_projection(x, norm_g, mod, w_main, w_small)
    o_sb = sb_attention(y3)
    o_ml = mlstm(y3, small3, ml_conv_w, ml_conv_b, ml_gate_b)
    kc, vc = nsa_compress(y3, cmp_pos, cmp_w1, cmp_b1, cmp_w2, cmp_b2, nsa_k_norm[0])
    o_nsa = nsa_attention(y3, small3, nsa_q_norm, nsa_k_norm, kc, vc)
    return merge_project(o_nsa, o_sb, o_ml, y3, w_branch, w_out, x, mod)


def kernel(x, c, ada_w, ada_b, norm_mix, norm_ffn, w_in, nsa_q_norm, nsa_k_norm, cmp_pos, cmp_w1, cmp_b1,
           cmp_w2, cmp_b2, ml_conv_w, ml_conv_b, ml_gate_b, w_branch, w_out, ffn_wg, ffn_wu, ffn_wd,
           moe_router, moe_wg, moe_wu, moe_wd):
    depth = ada_w.shape[0]
    b, s, d = x.shape
    mods = adaln(c, ada_w, ada_b).reshape(depth, b, 6, d)
    for layer in range(depth):
        mod = mods[layer]
        x = token_mixer_layer(x, mod, norm_mix[layer], w_in[layer], nsa_q_norm[layer], nsa_k_norm[layer],
                              cmp_pos[layer], cmp_w1[layer], cmp_b1[layer], cmp_w2[layer], cmp_b2[layer],
                              ml_conv_w[layer], ml_conv_b[layer], ml_gate_b[layer], w_branch[layer],
                              w_out[layer])
        j = layer // 2
        if layer % 2 == 0:
            x = dense_ffn(x, norm_ffn[layer], mod, ffn_wg[j], ffn_wu[j], ffn_wd[j])
        else:
            x = moe_ffn(x, norm_ffn[layer], mod, moe_router[j], moe_wg[j], moe_wu[j], moe_wd[j])
    return x
```

```python
import functools

import numpy as np
import jax
import jax.numpy as jnp
from jax import lax
from jax.experimental import pallas as pl
from jax.experimental.pallas import tpu as pltpu

F32 = jnp.float32
BF16 = jnp.bfloat16
I32 = jnp.int32
HIGHEST = lax.Precision.HIGHEST

EPS = 1e-6
NEG = -1e30
HEAD_DIM = 64
MIX_WIDTH = 512
NSA_HEADS = 8
NSA_KV_HEADS = 2
NSA_GROUP = NSA_HEADS // NSA_KV_HEADS
CMP_BLOCK = 32
CMP_STRIDE = 16
SEL_BLOCK = 64
SEL_TOPK = 16
WINDOW = 512
FORCE_BONUS = 1e4
ML_HEADS = 4
ML_HEAD_DIM = 128
ML_CHUNK = 64
ML_UNROLL = 2
CONV_WIDTH = 4
N_BRANCH = 3
N_EXPERTS = 8
TOP_K = 2
LANES = 128

C_MERGE = 0
C_NSA_Q = 3072
C_ML_Q = 3584
C_ML_K = 4096
C_ML_V = 4608
C_ML_O = 5120
C_SB_Q = 5632
C_SB_K = 6144
C_SB_V = 6656
C_NSA_KV = 7168
N_MAIN = 7936
S_NSA_GATE = 0
S_ML_I = 24
S_ML_F = 28
N_SMALL = 128

VMEM_LIMIT = 56 * 1024 * 1024


def _params(*sem):
    return pltpu.CompilerParams(dimension_semantics=sem, vmem_limit_bytes=VMEM_LIMIT)


def _iota(shape, dim):
    return lax.broadcasted_iota(I32, shape, dim)


def _split_dot(a32, b_bf16):
    hi = a32.astype(BF16)
    lo = (a32 - hi.astype(F32)).astype(BF16)
    return (jnp.dot(hi, b_bf16, preferred_element_type=F32)
            + jnp.dot(lo, b_bf16, preferred_element_type=F32))


def _split_dot_left(a_bf16, b32):
    hi = b32.astype(BF16)
    lo = (b32 - hi.astype(F32)).astype(BF16)
    return (jnp.dot(a_bf16, hi, preferred_element_type=F32)
            + jnp.dot(a_bf16, lo, preferred_element_type=F32))


def _dot_nt(a, b):
    return lax.dot_general(a, b, (((1,), (1,)), ((), ())), preferred_element_type=F32)


def _log_sigmoid(z):
    return jnp.minimum(z, 0.0) - jnp.log1p(jnp.exp(-jnp.abs(z)))


def _cast_kernel(x_ref, o_ref):
    o_ref[...] = x_ref[...].astype(o_ref.dtype)


def to_bf16(w, max_rows=512):
    cols = w.shape[-1]
    w2 = w.reshape(-1, cols)
    rows = w2.shape[0]
    tr = max(t for t in range(8, max_rows + 1, 8) if rows % t == 0)
    out = pl.pallas_call(
        _cast_kernel,
        out_shape=jax.ShapeDtypeStruct((rows, cols), BF16),
        grid=(rows // tr,),
        in_specs=[pl.BlockSpec((tr, cols), lambda i: (i, 0))],
        out_specs=pl.BlockSpec((tr, cols), lambda i: (i, 0)),
        compiler_params=_params("parallel"),
        name="to_bf16",
    )(w2)
    return out.reshape(w.shape)


def _adaln_kernel(c_ref, w_ref, b_ref, o_ref):
    c = c_ref[...]
    cond = c * jax.nn.sigmoid(c)
    o_ref[0] = jnp.dot(cond, w_ref[0], precision=HIGHEST, preferred_element_type=F32) + b_ref[0]


def adaln(c, ada_w, ada_b):
    depth, d, n = ada_w.shape
    b = c.shape[0]
    tn = 1536
    return pl.pallas_call(
        _adaln_kernel,
        out_shape=jax.ShapeDtypeStruct((depth, b, n), F32),
        grid=(depth, n // tn),
        in_specs=[pl.BlockSpec((b, d), lambda l, j: (0, 0)),
                  pl.BlockSpec((1, d, tn), lambda l, j: (l, 0, j)),
                  pl.BlockSpec((1, 1, tn), lambda l, j: (l, 0, j))],
        out_specs=pl.BlockSpec((1, b, tn), lambda l, j: (l, 0, j)),
        compiler_params=_params("parallel", "parallel"),
        name="adaln",
    )(c, ada_w, ada_b.reshape(depth, 1, n))


def _norm_mod(x, g, mod, shift_row, scale_row):
    ms = jnp.mean(x * x, axis=-1, keepdims=True)
    y = x * lax.rsqrt(ms + EPS) * g
    return y * (1.0 + mod[scale_row:scale_row + 1, :]) + mod[shift_row:shift_row + 1, :]


def _in_proj_kernel(x_ref, g_ref, mod_ref, wm_ref, ws_ref, y_ref, sm_ref, h_ref):
    @pl.when(pl.program_id(2) == 0)
    def _():
        h_ref[...] = _norm_mod(x_ref[0], g_ref[...], mod_ref[0], 0, 1).astype(BF16)
        sm_ref[0] = jnp.dot(h_ref[...], ws_ref[...], preferred_element_type=F32)

    y_ref[0] = jnp.dot(h_ref[...], wm_ref[...], preferred_element_type=F32).astype(y_ref.dtype)


def in_projection(x, g, mod, w_main, w_small, tm=512, n_tiles=2):
    b, s, d = x.shape
    tn = N_MAIN // n_tiles
    return pl.pallas_call(
        _in_proj_kernel,
        out_shape=(jax.ShapeDtypeStruct((b, s, N_MAIN), BF16), jax.ShapeDtypeStruct((b, s, N_SMALL), F32)),
        grid=(b, s // tm, n_tiles),
        in_specs=[pl.BlockSpec((1, tm, d), lambda i, j, n: (i, j, 0)),
                  pl.BlockSpec((1, d), lambda i, j, n: (0, 0)),
                  pl.BlockSpec((1, 6, d), lambda i, j, n: (i, 0, 0)),
                  pl.BlockSpec((d, tn), lambda i, j, n: (0, n)),
                  pl.BlockSpec((d, N_SMALL), lambda i, j, n: (0, 0))],
        out_specs=(pl.BlockSpec((1, tm, tn), lambda i, j, n: (i, j, n)),
                   pl.BlockSpec((1, tm, N_SMALL), lambda i, j, n: (i, j, 0))),
        scratch_shapes=[pltpu.VMEM((tm, d), BF16)],
        compiler_params=_params("parallel", "parallel", "arbitrary"),
        name="in_projection",
    )(x, g.reshape(1, d), mod, w_main, w_small)


SB_EXP_FLOOR = -104.0
SB_EAGER = 2


def _sb_kernel(q_ref, k_ref, v_ref, o_ref, vt_ref, *, tq, n_str):
    TK, dh = LANES, HEAD_DIM
    n_sub = tq // TK
    W = 2 * tq
    SR = range(n_str)
    qi = pl.program_id(2)

    @pl.when(qi == 0)
    def _():
        for c in range(v_ref.shape[1] // tq):
            v_t = v_ref[0, c * tq:(c + 1) * tq, :].astype(F32).T.astype(BF16)
            for p in SR:
                vt_ref[p, c] = v_t[p * LANES:(p + 1) * LANES]

    q_t = (q_ref[0].astype(F32) * (dh ** -0.5)).T
    chan = _iota((2 * dh, tq), 0)
    q_cat = []
    for p in SR:
        q_p = q_t[p * LANES:(p + 1) * LANES]
        q_cat.append(jnp.concatenate([jnp.where(chan < dh, q_p, 0.0), jnp.where(chan < dh, 0.0, q_p)],
                                     axis=1).astype(BF16))
    later = (_iota((TK, TK), 0) < _iota((TK, TK), 1)).astype(BF16)
    suffix = jnp.concatenate([jnp.concatenate([later, later], axis=1), jnp.ones((8, 2 * TK), BF16)], axis=0)

    def steps(blocks, st):
        work = [(p, b) for p in SR for b in range(len(blocks))]
        lss, his, los = {}, {}, {}
        for p, b in work:
            j, keep = blocks[b]
            k0 = pl.multiple_of(j * tq, tq)
            z = jnp.dot(k_ref[0, pl.ds(k0, tq), p * LANES:(p + 1) * LANES], q_cat[p],
                        preferred_element_type=F32)
            lk = -(jnp.maximum(z, 0.0) + jnp.log(1.0 + jnp.exp(-jnp.abs(z))))
            lss[p, b] = lk + z
            if keep is not None:
                lk = jnp.where(keep, lk, 0.0)
            his[p, b] = lk.astype(BF16)
            los[p, b] = (lk - his[p, b].astype(F32)).astype(BF16)
        carry = [st[p][0] for p in SR]
        afters = {}
        for p, b in work:
            after = [None] * n_sub
            for sub in range(n_sub - 1, -1, -1):
                rows = slice(sub * TK, (sub + 1) * TK)
                res = jnp.dot(suffix, jnp.concatenate([his[p, b][rows], los[p, b][rows]], axis=0),
                              preferred_element_type=F32)
                after[sub] = res[0:TK] + carry[p]
                carry[p] = carry[p] + res[TK:TK + 1]
            afters[p, b] = jnp.concatenate(after, axis=0)
        acc = [st[p][1] for p in SR]
        for p, b in work:
            j, keep = blocks[b]
            a = jnp.exp(lss[p, b] + afters[p, b])
            if keep is not None:
                a = jnp.where(keep, a, 0.0)
            acc[p] = acc[p] + jnp.dot(vt_ref[p, j], a.astype(BF16), preferred_element_type=F32)
        return [(carry[p], acc[p]) for p in SR]

    def cond(c):
        live = jnp.max(c[1][0][0])
        for p in range(1, n_str):
            live = jnp.maximum(live, jnp.max(c[1][p][0]))
        return (c[0] >= 0) & (live > SB_EXP_FLOOR)

    def body(c):
        return c[0] - 1, steps([(c[0], None)], c[1])

    strict = _iota((tq, W), 0) < (_iota((tq, W), 1) & (tq - 1))
    eager = [(jnp.maximum(qi - d, 0), qi >= d) for d in range(1, SB_EAGER + 1)]
    st = steps([(qi, strict)] + eager, [(jnp.zeros((1, W), F32), jnp.zeros((2 * dh, W), F32)) for p in SR])
    _, st = lax.while_loop(cond, body, (qi - 1 - SB_EAGER, st))
    out_rows = []
    for p in SR:
        out_rows += [st[p][1][0:dh, 0:tq], st[p][1][dh:2 * dh, tq:W]]
    o_ref[0] = jnp.concatenate(out_rows, axis=0).T.astype(o_ref.dtype)


def sb_attention(y3, tq=256, n_str=2):
    b, s, _ = y3.shape
    wide = n_str * LANES
    qb, kb, vb = C_SB_Q // wide, C_SB_K // wide, C_SB_V // wide
    return pl.pallas_call(
        functools.partial(_sb_kernel, tq=tq, n_str=n_str),
        out_shape=jax.ShapeDtypeStruct((b, s, MIX_WIDTH), BF16),
        grid=(b, MIX_WIDTH // wide, s // tq),
        in_specs=[pl.BlockSpec((1, tq, wide), lambda i, p, j: (i, j, qb + p)),
                  pl.BlockSpec((1, s, wide), lambda i, p, j: (i, 0, kb + p)),
                  pl.BlockSpec((1, s, wide), lambda i, p, j: (i, 0, vb + p))],
        out_specs=pl.BlockSpec((1, tq, wide), lambda i, p, j: (i, j, p)),
        scratch_shapes=[pltpu.VMEM((n_str, s // tq, LANES, tq), BF16)],
        compiler_params=_params("parallel", "parallel", "arbitrary"),
        name="sb_attention",
    )(y3, y3, y3)


def _mlstm_kernel(q_ref, k_ref, v_ref, og_ref, sm_ref, gr_ref, cw_ref, cb_ref, gb_ref, out_ref,
                  ct_ref, n_ref, m_ref, xbuf_ref, qk_ref, *, ts):
    L, dh, H, W = ML_CHUNK, ML_HEAD_DIM, ML_HEADS, MIX_WIDTH
    halo = 8
    sblk = pl.program_id(1)

    @pl.when(sblk == 0)
    def _():
        ct_ref[...] = jnp.zeros_like(ct_ref)
        n_ref[...] = jnp.zeros_like(n_ref)
        m_ref[...] = jnp.zeros_like(m_ref)
        xbuf_ref[0:halo, :] = jnp.zeros((halo, 2 * W), F32)

    @pl.when(sblk > 0)
    def _():
        xbuf_ref[0:halo, :] = xbuf_ref[ts:ts + halo, :]

    xbuf_ref[halo:halo + ts, 0:W] = q_ref[0].astype(F32)
    xbuf_ref[halo:halo + ts, W:2 * W] = k_ref[0].astype(F32)
    conv = cb_ref[...] + jnp.zeros((ts, 2 * W), F32)
    for j in range(CONV_WIDTH):
        off = halo - (CONV_WIDTH - 1) + j
        conv = conv + cw_ref[j:j + 1, :] * xbuf_ref[off:off + ts, :]
    act = conv * jax.nn.sigmoid(conv)
    qk_ref[:, 0:W] = (act[:, 0:W] * (dh ** -0.5)).astype(BF16)
    qk_ref[:, W:2 * W] = act[:, W:2 * W].astype(BF16)

    it0, it1 = _iota((L, L), 0), _iota((L, L), 1)
    causal = it0 >= it1
    tri_lo = causal.astype(BF16)
    tri_up = (it0 <= it1).astype(BF16)

    def chunks(it, carry):
        HR = range(H)
        UR = range(ML_UNROLL)
        UH = [(u, h) for u in UR for h in HR]
        cols = [slice(h * dh, (h + 1) * dh) for h in HR]
        rows = [pl.ds(pl.multiple_of((it * ML_UNROLL + u) * L, L), L) for u in UR]
        sm = [sm_ref[0, rows[u], :] for u in UR]
        gr = [gr_ref[0, it * ML_UNROLL + u] for u in UR]
        ig_col = {(u, h): sm[u][:, S_ML_I + h:S_ML_I + h + 1] + gb_ref[0, h] for u, h in UH}
        lf_col = {(u, h): _log_sigmoid(sm[u][:, S_ML_F + h:S_ML_F + h + 1] + gb_ref[1, h]) for u, h in UH}
        ig_row = {(u, h): gr[u][h:h + 1, :] + gb_ref[0, h] for u, h in UH}
        lf_row = {(u, h): _log_sigmoid(gr[u][H + h:H + h + 1, :] + gb_ref[1, h]) for u, h in UH}
        b_t = {k: _split_dot_left(tri_lo, jnp.broadcast_to(lf_col[k], (L, L))) for k in UH}
        b_s = {k: _split_dot(jnp.broadcast_to(lf_row[k], (L, L)), tri_up) for k in UH}
        qq = {(u, h): qk_ref[rows[u], cols[h]] for u, h in UH}
        kk = {(u, h): qk_ref[rows[u], W + h * dh:W + (h + 1) * dh] for u, h in UH}
        vv = {(u, h): v_ref[0, rows[u], cols[h]] for u, h in UH}
        qk = {k: _dot_nt(qq[k], kk[k]) for k in UH}
        kt = {k: kk[k].astype(F32).T.astype(BF16) for k in UH}
        dmat = {k: jnp.where(causal, b_t[k] - b_s[k] + ig_row[k], NEG) for k in UH}
        d_max = {k: jnp.max(dmat[k], axis=1, keepdims=True) for k in UH}
        b_col = {k: b_t[k][:, 0:1] for k in UH}
        b_last = {k: b_t[k][L - 1:L, 0:1] for k in UH}
        decay = {k: b_last[k] - b_col[k] + ig_col[k] for k in UH}
        decay_max = {k: jnp.max(decay[k], axis=0, keepdims=True) for k in UH}
        m_prev, m_new = {}, {}
        for u, h in UH:
            m_prev[u, h] = m_ref[h][:, 0:1] if u == 0 else m_new[u - 1, h]
            m_new[u, h] = jnp.maximum(b_last[u, h] + m_prev[u, h], decay_max[u, h])
        m_inter = {k: b_col[k] + m_prev[k] for k in UH}
        m_t = {k: jnp.maximum(m_inter[k], d_max[k]) for k in UH}
        w = {k: jnp.exp(dmat[k] - m_t[k]) * qk[k] for k in UH}
        inter = {k: jnp.exp(m_inter[k] - m_t[k]) for k in UH}
        w_v = {k: jnp.dot(w[k].astype(BF16), vv[k], preferred_element_type=F32) for k in UH}
        ws = {k: jnp.exp(decay[k] - m_new[k]) for k in UH}
        cscale = {k: jnp.exp(b_last[k] + m_prev[k] - m_new[k]) for k in UH}
        wv = {k: (ws[k] * vv[k].astype(F32)).astype(BF16) for k in UH}
        k_wv = {k: jnp.dot(kt[k], wv[k], preferred_element_type=F32) for k in UH}
        k_ws = {k: jnp.sum(ws[k] * kk[k].astype(F32), axis=0, keepdims=True) for k in UH}
        w_sum = {k: jnp.sum(w[k], axis=1, keepdims=True) for k in UH}
        ct = {h: ct_ref[h] for h in HR}
        nvec = {h: n_ref[h] for h in HR}
        for u, h in UH:
            k = (u, h)
            num = inter[k] * jnp.dot(qq[k], ct[h].astype(BF16), preferred_element_type=F32) + w_v[k]
            den = inter[k] * jnp.sum(qq[k].astype(F32) * nvec[h], axis=1, keepdims=True) + w_sum[k]
            hval = num / jnp.maximum(jnp.abs(den), jnp.exp(-m_t[k]))
            ct[h] = cscale[k] * ct[h] + k_wv[k]
            nvec[h] = cscale[k] * nvec[h] + k_ws[k]
            gate = jax.nn.sigmoid(og_ref[0, rows[u], cols[h]].astype(F32))
            out_ref[0, rows[u], cols[h]] = (gate * hval).astype(out_ref.dtype)
        for h in HR:
            ct_ref[h] = ct[h]
            n_ref[h] = nvec[h]
            m_ref[h] = jnp.broadcast_to(m_new[ML_UNROLL - 1, h], (1, LANES))
        return carry

    lax.fori_loop(0, ts // (L * ML_UNROLL), chunks, 0)


def mlstm(y3, small3, conv_w, conv_b, gate_b, ts=512):
    b, s, _ = y3.shape
    W, H, L = MIX_WIDTH, ML_HEADS, ML_CHUNK
    gr = small3[:, :, S_ML_I:S_ML_I + 2 * H].reshape(b, s // L, L, 2 * H).transpose(0, 1, 3, 2)
    cq, ck, cv, co = C_ML_Q // W, C_ML_K // W, C_ML_V // W, C_ML_O // W
    return pl.pallas_call(
        functools.partial(_mlstm_kernel, ts=ts),
        out_shape=jax.ShapeDtypeStruct((b, s, W), BF16),
        grid=(b, s // ts),
        in_specs=[pl.BlockSpec((1, ts, W), lambda i, j: (i, j, cq)),
                  pl.BlockSpec((1, ts, W), lambda i, j: (i, j, ck)),
                  pl.BlockSpec((1, ts, W), lambda i, j: (i, j, cv)),
                  pl.BlockSpec((1, ts, W), lambda i, j: (i, j, co)),
                  pl.BlockSpec((1, ts, N_SMALL), lambda i, j: (i, j, 0)),
                  pl.BlockSpec((1, ts // L, 2 * H, L), lambda i, j: (i, j, 0, 0)),
                  pl.BlockSpec((CONV_WIDTH, 2 * W), lambda i, j: (0, 0)),
                  pl.BlockSpec((1, 2 * W), lambda i, j: (0, 0)),
                  pl.BlockSpec(memory_space=pltpu.SMEM)],
        out_specs=pl.BlockSpec((1, ts, W), lambda i, j: (i, j, 0)),
        scratch_shapes=[pltpu.VMEM((H, ML_HEAD_DIM, ML_HEAD_DIM), F32),
                        pltpu.VMEM((H, 1, ML_HEAD_DIM), F32),
                        pltpu.VMEM((H, 1, LANES), F32),
                        pltpu.VMEM((ts + 8, 2 * W), F32),
                        pltpu.VMEM((ts, 2 * W), BF16)],
        compiler_params=_params("parallel", "arbitrary"),
        name="mlstm",
    )(y3, y3, y3, y3, small3, gr, conv_w, conv_b.reshape(1, 2 * W), gate_b)


def _gelu_tanh(x):
    return 0.5 * x * (1.0 + jnp.tanh(0.7978845608028654 * (x + 0.044715 * (x * x * x))))


def _compress_kernel(ra_ref, rb_ref, pos_ref, w1_ref, b1_ref, w2_ref, b2_ref, kn_ref, kc_ref, vc_ref):
    half = (CMP_BLOCK // 2) * HEAD_DIM
    for j, o_ref in enumerate((kc_ref, vc_ref)):
        xa = (ra_ref[j, 0, 0].astype(F32) + pos_ref[j, :, 0:half]).astype(BF16)
        xb = (rb_ref[j, 0, 0].astype(F32) + pos_ref[j, :, half:2 * half]).astype(BF16)
        hid = (jnp.dot(xa, w1_ref[j, 0:half, :], preferred_element_type=F32)
               + jnp.dot(xb, w1_ref[j, half:2 * half, :], preferred_element_type=F32) + b1_ref[j])
        out = jnp.dot(_gelu_tanh(hid).astype(BF16), w2_ref[j], preferred_element_type=F32) + b2_ref[j]
        if j == 0:
            out = out * lax.rsqrt(jnp.mean(out * out, axis=-1, keepdims=True) + EPS) * kn_ref[...]
        o_ref[0, 0] = out


def nsa_compress(y3, cmp_pos, cmp_w1, cmp_b1, cmp_w2, cmp_b2, k_norm0):
    b, s, _ = y3.shape
    G, dh = NSA_KV_HEADS, HEAD_DIM
    nr = s // CMP_STRIDE
    wide = CMP_STRIDE * dh
    kv = y3[:, :, C_NSA_KV:C_NSA_KV + 2 * G * dh].reshape(b, s, 2, G, dh)
    ra = kv.transpose(2, 0, 3, 1, 4).reshape(2, b, G, nr, wide)
    rb = jnp.concatenate([ra[:, :, :, 1:], jnp.zeros((2, b, G, 1, wide), ra.dtype)], axis=3)
    hidden = cmp_w1.shape[-1]
    out = jax.ShapeDtypeStruct((b, G, nr, dh), F32)
    blk = pl.BlockSpec((2, 1, 1, nr, wide), lambda i, g: (0, i, g, 0, 0))
    oblk = pl.BlockSpec((1, 1, nr, dh), lambda i, g: (i, g, 0, 0))

    def full(shape):
        return pl.BlockSpec(shape, lambda i, g: (0,) * len(shape))

    return pl.pallas_call(
        _compress_kernel,
        out_shape=(out, out),
        grid=(b, G),
        in_specs=[blk, blk, full((2, 1, 2 * wide)), full((2, 2 * wide, hidden)), full((2, 1, hidden)),
                  full((2, hidden, dh)), full((2, 1, dh)), full((1, dh))],
        out_specs=(oblk, oblk),
        compiler_params=_params("parallel", "parallel"),
        name="nsa_compress",
    )(ra, rb, cmp_pos.reshape(2, 1, 2 * wide), cmp_w1.astype(BF16), cmp_b1.reshape(2, 1, hidden),
      cmp_w2.astype(BF16), cmp_b2.reshape(2, 1, dh), k_norm0.reshape(1, dh))


NSA_QB = 128
NSA_QBLK = 2
NSA_KS = 512
NSA_CK = 256
NSA_VR = 80
A_FEAT, A_PEN, A_BIAS = 0, 64, 128
A_DUMMY = A_BIAS + 4


def _head_rms(x, gain):
    w = x.shape[1]
    same_head = (_iota((w, w), 0) // HEAD_DIM == _iota((w, w), 1) // HEAD_DIM).astype(BF16)
    ss = _split_dot(x * x, same_head)
    return x * lax.rsqrt(ss * (1.0 / HEAD_DIM) + EPS) * gain


def _nsa_kernel(q_ref, gl_ref, kca_ref, vcT_ref, ks_ref, kw_ref, vs_ref, vw_ref, kconst_ref, gq_ref, gk_ref,
                     o_ref, qt_ref, ksa_ref, kwa_ref, vsa_ref, vwa_ref, s_ref, *, n_sel, top):
    QB, R, dh, CK, VR, KS = NSA_QB, NSA_GROUP, HEAD_DIM, NSA_CK, NSA_VR, NSA_KS
    G = NSA_KV_HEADS
    GR = range(G)
    UR = range(NSA_QBLK)
    UG = [(u, g) for u in UR for g in GR]
    HQ = R * QB
    NPAD = WINDOW // QB
    SUB = KS // QB
    step = pl.program_id(1)
    qi = [step * NSA_QBLK + u for u in UR]
    q0 = [qi[u] * QB for u in UR]
    nkb = vsa_ref.shape[1]

    @pl.when(step == 0)
    def _():
        pad_keys = jnp.where(_iota((WINDOW, CK), 1) == A_DUMMY, 1.0, 0.0).astype(BF16)
        ones_rows = jnp.where(_iota((nkb + NPAD, VR - dh, QB), 1) == 0, 1.0, 0.0).astype(BF16)
        for g in GR:
            heads = slice(g * dh, (g + 1) * dh)
            ksa_ref[g] = kconst_ref[...]
            kwa_ref[g, 0:WINDOW, :] = pad_keys
            kwa_ref[g, WINDOW:, :] = kconst_ref[...]
            kwa_ref[g, WINDOW:, A_PEN:A_PEN + 64] = jnp.zeros((kwa_ref.shape[1] - WINDOW, 64), BF16)
            vwa_ref[g, 0:NPAD, 0:dh, :] = jnp.zeros((NPAD, dh, QB), BF16)
            vsa_ref[g, :, dh:VR, :] = ones_rows[0:nkb]
            vwa_ref[g, :, dh:VR, :] = ones_rows
        for c in range(ks_ref.shape[1] // KS):
            keys = slice(c * KS, (c + 1) * KS)
            ks_n = _head_rms(ks_ref[0, keys, :].astype(F32), gk_ref[0]).astype(BF16)
            kw_n = _head_rms(kw_ref[0, keys, :].astype(F32), gk_ref[1]).astype(BF16)
            for g in GR:
                heads = slice(g * dh, (g + 1) * dh)
                ksa_ref[g, keys, A_FEAT:A_FEAT + dh] = ks_n[:, heads]
                kwa_ref[g, WINDOW + c * KS:WINDOW + (c + 1) * KS, A_FEAT:A_FEAT + dh] = kw_n[:, heads]
        for c in range(nkb):
            keys = slice(c * QB, (c + 1) * QB)
            vs_t = vs_ref[0, keys, :].astype(F32).T.astype(BF16)
            vw_t = vw_ref[0, keys, :].astype(F32).T.astype(BF16)
            for g in GR:
                heads = slice(g * dh, (g + 1) * dh)
                vsa_ref[g, c, 0:dh, :] = vs_t[heads]
                vwa_ref[g, NPAD + c, 0:dh, :] = vw_t[heads]
        qt_ref[:, :, A_BIAS + 16:CK, :] = jnp.zeros((NSA_QBLK, G, CK - A_BIAS - 16, HQ), BF16)

    q_n = _head_rms(q_ref[0].astype(F32), gq_ref[...]).astype(BF16)
    q_rows = (q_n.astype(F32) * (dh ** -0.5)).T
    lane = _iota((16, HQ), 1)
    rowi = _iota((16, HQ), 0)
    for u, g in UG:
        t_q = q0[u] + (lane & (QB - 1))
        t_hi = ((t_q >> 6) << 6).astype(F32)
        t_lo = (t_q & 63).astype(F32)
        qT = jnp.concatenate([q_rows[(g * R + r) * dh:(g * R + r + 1) * dh, u * QB:(u + 1) * QB]
                              for r in range(R)], axis=1).astype(BF16)
        qt_ref[u, g, A_FEAT:A_FEAT + dh, :] = qT
        qt_ref[u, g, A_PEN:A_PEN + dh, :] = qT
        slope = jnp.exp2(-(g * R + (lane >> 7) + 1).astype(F32))
        bias_rows = jnp.where(rowi < 2, slope,
                              jnp.where(rowi == 2, -slope * t_hi,
                                        jnp.where(rowi == 3, -slope * t_lo,
                                                  jnp.where(rowi == A_DUMMY - A_BIAS, NEG, 0.0))))
        qt_ref[u, g, A_BIAS:A_BIAS + 16, :] = bias_rows.astype(BF16)
    k_loc = _iota((QB, HQ), 0)
    q_loc = _iota((QB, HQ), 1) & (QB - 1)

    def pv(v_ref_, g, kb0, pr):
        out = None
        for i in range(pr.shape[0] // QB):
            term = jnp.dot(v_ref_[g, kb0 + i], pr[i * QB:(i + 1) * QB], preferred_element_type=F32)
            out = term if out is None else out + term
        return out

    n_cmp = kca_ref.shape[2]
    cmp_end = _iota((n_cmp, HQ), 0) * CMP_STRIDE + (CMP_BLOCK - 1)
    valid = [cmp_end <= q0[u] + (_iota((n_cmp, HQ), 1) & (QB - 1)) for u in UR]
    sc = {(u, g): jnp.where(valid[u], jnp.dot(kca_ref[0, g], qt_ref[u, g], preferred_element_type=F32), NEG)
          for u, g in UG}
    mx = {k: jnp.max(sc[k], axis=0, keepdims=True) for k in UG}
    e = {k: jnp.exp(sc[k] - mx[k]) for k in UG}
    inv = {k: jnp.where(mx[k] > 0.5 * NEG, 1.0 / jnp.sum(e[k], axis=0, keepdims=True), 0.0) for k in UG}
    p = {k: e[k] * inv[k] for k in UG}
    o_cmp = {(u, g): jnp.dot(vcT_ref[0, g], p[u, g].astype(BF16), preferred_element_type=F32)
             for u, g in UG}

    sw = {(u, g): jnp.dot(kwa_ref[g, pl.ds(pl.multiple_of(q0[u], QB), WINDOW + QB), :], qt_ref[u, g],
                          preferred_element_type=F32) for u, g in UG}
    sw = {k: jnp.concatenate([jnp.where(k_loc > q_loc, sw[k][0:QB], NEG), sw[k][QB:WINDOW],
                              jnp.where(k_loc <= q_loc, sw[k][WINDOW:WINDOW + QB], NEG)], axis=0) for k in UG}
    pw = {k: jnp.exp(sw[k] - jnp.max(sw[k], axis=0, keepdims=True)).astype(BF16) for k in UG}
    acc_w = {(u, g): pv(vwa_ref, g, qi[u], pw[u, g]) for u, g in UG}
    o_win = {k: acc_w[k][0:dh] / acc_w[k][dh:dh + 1] for k in UG}

    c0 = _iota((n_sel, n_cmp), 1) * CMP_STRIDE
    s0 = _iota((n_sel, n_cmp), 0) * SEL_BLOCK
    overlap_t = ((c0 < s0 + SEL_BLOCK) & (c0 + CMP_BLOCK > s0)).astype(BF16)
    j_idx = _iota((n_sel, QB), 0)
    tq = [q0[u] + _iota((n_sel, QB), 1) for u in UR]
    forced = [(j_idx == 0) | (j_idx == (tq[u] >> 6)) | (j_idx == (tq[u] >> 6) - 1) for u in UR]
    causal_blk = [j_idx * SEL_BLOCK <= tq[u] for u in UR]
    p_grp = {k: p[k][:, 0:QB] + p[k][:, QB:2 * QB] + p[k][:, 2 * QB:3 * QB] + p[k][:, 3 * QB:4 * QB] for k in UG}
    imp = {(u, g): jnp.where(causal_blk[u], _split_dot_left(overlap_t, p_grp[u, g])
                             + jnp.where(forced[u], FORCE_BONUS, 0.0), -1.0) for u, g in UG}
    sel = {k: jnp.zeros((n_sel, QB), F32) for k in UG}
    for _ in range(top):
        for k in UG:
            top_v = jnp.max(imp[k], axis=0, keepdims=True)
            first = jnp.min(jnp.where(imp[k] == top_v, j_idx, n_sel), axis=0, keepdims=True)
            pick = j_idx == first
            sel[k] = jnp.where(pick, 1.0, sel[k])
            imp[k] = jnp.where(pick, -3e38, imp[k])
    for u, g in UG:
        pen = jnp.where((sel[u, g] > 0.5) & causal_blk[u], 0.0, NEG)
        if n_sel < 64:
            pen = jnp.concatenate([pen, jnp.zeros((64 - n_sel, QB), F32)], axis=0)
        qt_ref[u, g, A_PEN:A_PEN + 64, :] = jnp.concatenate([pen] * R, axis=1).astype(BF16)

    def score(j):
        rows = pl.ds(pl.multiple_of(j * KS, KS), KS)
        return [jnp.dot(ksa_ref[g, rows, :], qt_ref[u, g], preferred_element_type=F32) for u, g in UG]

    def absorb(s, j, st):
        m_new = [jnp.maximum(st[n][0], jnp.max(s[n], axis=0, keepdims=True)) for n in range(len(UG))]
        pr = [jnp.exp(s[n] - m_new[n]).astype(BF16) for n in range(len(UG))]
        return [(m_new[n], jnp.exp(st[n][0] - m_new[n]) * st[n][1] + pv(vsa_ref, UG[n][1], j * SUB, pr[n]))
                for n in range(len(UG))]

    n_full = qi[0] // SUB
    init = [(jnp.full((1, HQ), NEG, F32), jnp.zeros((VR, HQ), F32)) for _ in UG]
    st = lax.fori_loop(0, n_full, lambda j, st_: absorb(score(j), j, st_), init)
    s_last = score(n_full)
    for n, (u, g) in enumerate(UG):
        diag = pl.ds(pl.multiple_of(q0[u] - n_full * KS, QB), QB)
        s_ref[u, g] = s_last[n]
        s_ref[u, g, diag, :] = jnp.where(k_loc <= q_loc, s_ref[u, g, diag, :], NEG)
    st = absorb([s_ref[u, g] for u, g in UG], n_full, st)
    o_sel = {UG[n]: st[n][1][0:dh] / st[n][1][dh:dh + 1] for n in range(len(UG))}

    gl_t = gl_ref[0].T
    for u in UR:
        rows_out = []
        for g in GR:
            gate = [jax.nn.sigmoid(jnp.concatenate(
                [gl_t[(g * R + r) * N_BRANCH + br:(g * R + r) * N_BRANCH + br + 1, u * QB:(u + 1) * QB]
                 for r in range(R)], axis=1)) for br in range(N_BRANCH)]
            o_t = gate[0] * o_cmp[u, g] + gate[1] * o_sel[u, g] + gate[2] * o_win[u, g]
            rows_out += [o_t[:, r * QB:(r + 1) * QB] for r in range(R)]
        o_ref[0, u * QB:(u + 1) * QB, :] = jnp.concatenate(rows_out, axis=0).T.astype(o_ref.dtype)


def nsa_attention(y3, small3, q_norm, k_norm, kc, vc):
    b, s, _ = y3.shape
    G, R, dh, QB, CK, VR = NSA_KV_HEADS, NSA_GROUP, HEAD_DIM, NSA_QB, NSA_CK, NSA_VR
    assert (NSA_KS // QB) % NSA_QBLK == 0, "a step's query blocks must share their span of NSA_KS keys"
    HQ = R * QB
    nq = s // QB
    n_sel = s // SEL_BLOCK
    assert n_sel <= 64, "selection one-hot columns hold at most 64 blocks"
    top = min(SEL_TOPK, n_sel)
    n_cmp = kc.shape[2]

    def pos_cols(pos):
        return np.stack([pos // 64 * 64, pos % 64, np.ones_like(pos), np.ones_like(pos)], axis=1)

    vc_t = vc.transpose(0, 1, 3, 2).astype(BF16)

    pos = np.arange(s)
    kconst = np.zeros((s, CK), np.float32)
    kconst[pos, A_PEN + pos // SEL_BLOCK] = 1.0
    kconst[:, A_BIAS:A_BIAS + 4] = pos_cols(pos)
    kconst = jnp.asarray(kconst, BF16)

    kc_hi = kc.astype(BF16)
    kc_lo = (kc - kc_hi.astype(F32)).astype(BF16)
    cend = np.arange(n_cmp) * CMP_STRIDE + (CMP_BLOCK - 1)
    cbias = np.zeros((n_cmp, CK - 2 * dh), np.float32)
    cbias[:, 0:4] = pos_cols(cend)
    kc_aug = jnp.concatenate([kc_hi, kc_lo, jnp.broadcast_to(jnp.asarray(cbias, BF16), (b, G, n_cmp, CK - 2 * dh))],
                             axis=-1)

    qw = G * R * dh
    kvb = C_NSA_KV // LANES
    gq = jnp.tile(q_norm, G * R).reshape(1, qw)
    gk = jnp.stack([jnp.tile(k_norm[1], G), jnp.tile(k_norm[2], G)]).reshape(2, 1, LANES)

    def kv_spec(blk):
        return pl.BlockSpec((1, s, LANES), lambda i, j: (i, 0, kvb + blk))

    return pl.pallas_call(
        functools.partial(_nsa_kernel, n_sel=n_sel, top=top),
        out_shape=jax.ShapeDtypeStruct((b, s, MIX_WIDTH), BF16),
        grid=(b, nq // NSA_QBLK),
        in_specs=[pl.BlockSpec((1, NSA_QBLK * QB, qw), lambda i, j: (i, j, C_NSA_Q // qw)),
                  pl.BlockSpec((1, NSA_QBLK * QB, N_SMALL), lambda i, j: (i, j, 0)),
                  pl.BlockSpec((1, G, n_cmp, CK), lambda i, j: (i, 0, 0, 0)),
                  pl.BlockSpec((1, G, dh, n_cmp), lambda i, j: (i, 0, 0, 0)),
                  kv_spec(2), kv_spec(4), kv_spec(3), kv_spec(5),
                  pl.BlockSpec((s, CK), lambda i, j: (0, 0)),
                  pl.BlockSpec((1, qw), lambda i, j: (0, 0)),
                  pl.BlockSpec((2, 1, LANES), lambda i, j: (0, 0, 0))],
        out_specs=pl.BlockSpec((1, NSA_QBLK * QB, qw), lambda i, j: (i, j, 0)),
        scratch_shapes=[pltpu.VMEM((NSA_QBLK, G, CK, HQ), BF16),
                        pltpu.VMEM((G, s, CK), BF16), pltpu.VMEM((G, s + WINDOW, CK), BF16),
                        pltpu.VMEM((G, nq, VR, QB), BF16), pltpu.VMEM((G, nq + WINDOW // QB, VR, QB), BF16),
                        pltpu.VMEM((NSA_QBLK, G, NSA_KS, HQ), F32)],
        compiler_params=_params("parallel", "arbitrary"),
        name="nsa_attention",
    )(y3, small3, kc_aug, vc_t, y3, y3, y3, y3, kconst, gq, gk)


def _merge_kernel(on_ref, os_ref, om_ref, g0_ref, g1_ref, g2_ref, wb_ref, wo_ref, x_ref, mod_ref, o_ref):
    merged = None
    for i, (o_r, g_r) in enumerate(((on_ref, g0_ref), (os_ref, g1_ref), (om_ref, g2_ref))):
        br = jnp.dot(o_r[0], wb_ref[i], preferred_element_type=F32)
        term = jax.nn.sigmoid(g_r[0].astype(F32)) * br
        merged = term if merged is None else merged + term
    out = jnp.dot(merged.astype(BF16), wo_ref[...], preferred_element_type=F32)
    o_ref[0] = x_ref[0] + mod_ref[0, 2:3, :] * out


def merge_project(o_nsa, o_sb, o_ml, y3, w_branch, w_out, x, mod, tm=512):
    b, s, d = x.shape
    W = MIX_WIDTH
    ospec = pl.BlockSpec((1, tm, W), lambda i, j: (i, j, 0))
    xspec = pl.BlockSpec((1, tm, d), lambda i, j: (i, j, 0))
    gspecs = [pl.BlockSpec((1, tm, d), functools.partial(lambda i, j, c: (i, j, c), c=C_MERGE // d + c))
              for c in range(N_BRANCH)]
    return pl.pallas_call(
        _merge_kernel,
        out_shape=jax.ShapeDtypeStruct((b, s, d), F32),
        grid=(b, s // tm),
        in_specs=[ospec, ospec, ospec] + gspecs + [
            pl.BlockSpec((N_BRANCH, W, d), lambda i, j: (0, 0, 0)),
            pl.BlockSpec((d, d), lambda i, j: (0, 0)),
            xspec,
            pl.BlockSpec((1, 6, d), lambda i, j: (i, 0, 0))],
        out_specs=xspec,
        compiler_params=_params("parallel", "parallel"),
        name="merge_project",
    )(o_nsa, o_sb, o_ml, y3, y3, y3, w_branch.astype(BF16), w_out.astype(BF16), x, mod)


def _ffn_kernel(x_ref, g_ref, mod_ref, wg_ref, wu_ref, wd_ref, o_ref, h_ref, acc_ref):
    f = pl.program_id(2)

    @pl.when(f == 0)
    def _():
        h_ref[...] = _norm_mod(x_ref[0], g_ref[...], mod_ref[0], 3, 4).astype(BF16)
        acc_ref[...] = jnp.zeros_like(acc_ref)

    h = h_ref[...]
    a = jnp.dot(h, wg_ref[...], preferred_element_type=F32)
    u = jnp.dot(h, wu_ref[...], preferred_element_type=F32)
    act = (a * jax.nn.sigmoid(a) * u).astype(BF16)
    acc_ref[...] += jnp.dot(act, wd_ref[...], preferred_element_type=F32)

    @pl.when(f == pl.num_programs(2) - 1)
    def _():
        o_ref[0] = x_ref[0] + mod_ref[0, 5:6, :] * acc_ref[...]


def dense_ffn(x, g, mod, wg, wu, wd, tm=512, n_ftiles=2):
    b, s, d = x.shape
    ff = wg.shape[1]
    tf = -(-ff // (n_ftiles * LANES)) * LANES
    pad = n_ftiles * tf - ff
    wg = jnp.pad(to_bf16(wg), ((0, 0), (0, pad)))
    wu = jnp.pad(to_bf16(wu), ((0, 0), (0, pad)))
    wd = jnp.pad(to_bf16(wd), ((0, pad), (0, 0)))
    xspec = pl.BlockSpec((1, tm, d), lambda i, j, f: (i, j, 0))
    return pl.pallas_call(
        _ffn_kernel,
        out_shape=jax.ShapeDtypeStruct((b, s, d), F32),
        grid=(b, s // tm, n_ftiles),
        in_specs=[xspec,
                  pl.BlockSpec((1, d), lambda i, j, f: (0, 0)),
                  pl.BlockSpec((1, 6, d), lambda i, j, f: (i, 0, 0)),
                  pl.BlockSpec((d, tf), lambda i, j, f: (0, f)),
                  pl.BlockSpec((d, tf), lambda i, j, f: (0, f)),
                  pl.BlockSpec((tf, d), lambda i, j, f: (f, 0))],
        out_specs=xspec,
        scratch_shapes=[pltpu.VMEM((tm, d), BF16), pltpu.VMEM((tm, d), F32)],
        compiler_params=_params("parallel", "parallel", "arbitrary"),
        name="dense_ffn",
    )(x, g.reshape(1, d), mod, wg, wu, wd)


def _router_kernel(x_ref, g_ref, mod_ref, wr_ref, h_ref, e_ref, p_ref):
    h = _norm_mod(x_ref[0], g_ref[...], mod_ref[0], 3, 4)
    h_ref[...] = h
    lane = _iota((1, LANES), 1)
    real = lane < N_EXPERTS
    logits = jnp.where(real, jnp.dot(h, wr_ref[...], precision=HIGHEST, preferred_element_type=F32), NEG)
    e = jnp.exp(logits - jnp.max(logits, axis=1, keepdims=True))
    p = jnp.where(real, e / jnp.sum(e, axis=1, keepdims=True), -1.0)
    p1 = jnp.max(p, axis=1, keepdims=True)
    i1 = jnp.min(jnp.where(p == p1, lane, LANES), axis=1, keepdims=True)
    rest = jnp.where(lane == i1, -1.0, p)
    p2 = jnp.max(rest, axis=1, keepdims=True)
    i2 = jnp.min(jnp.where(rest == p2, lane, LANES), axis=1, keepdims=True)
    tot = p1 + p2
    e_ref[...] = jnp.where(lane == 0, i1, jnp.where(lane == 1, i2, 0))[:, 0:N_EXPERTS]
    p_ref[...] = jnp.where(lane == 0, p1 / tot, jnp.where(lane == 1, p2 / tot, 0.0))[:, 0:N_EXPERTS]


def moe_router(x, g, mod, w_router, tm=512):
    b, s, d = x.shape
    t = b * s
    spb = s // tm
    wr = jnp.pad(w_router, ((0, 0), (0, LANES - N_EXPERTS)))
    return pl.pallas_call(
        _router_kernel,
        out_shape=(jax.ShapeDtypeStruct((t, d), F32),
                   jax.ShapeDtypeStruct((t, N_EXPERTS), I32),
                   jax.ShapeDtypeStruct((t, N_EXPERTS), F32)),
        grid=(b, spb),
        in_specs=[pl.BlockSpec((1, tm, d), lambda i, j: (i, j, 0)),
                  pl.BlockSpec((1, d), lambda i, j: (0, 0)),
                  pl.BlockSpec((1, 6, d), lambda i, j: (i, 0, 0)),
                  pl.BlockSpec((d, LANES), lambda i, j: (0, 0))],
        out_specs=(pl.BlockSpec((tm, d), lambda i, j: (i * spb + j, 0)),
                   pl.BlockSpec((tm, N_EXPERTS), lambda i, j: (i * spb + j, 0)),
                   pl.BlockSpec((tm, N_EXPERTS), lambda i, j: (i * spb + j, 0))),
        compiler_params=_params("parallel", "parallel"),
        name="moe_router",
    )(x, g.reshape(1, d), mod, wr)


def moe_ffn(x, g, mod, w_router, wg, wu, wd, tb=512):
    b, s, d = x.shape
    t = b * s
    a = t * TOP_K
    h, top_e, top_p = moe_router(x, g, mod, w_router)
    e_flat = top_e[:, 0:TOP_K].reshape(a)
    onehot = (e_flat[:, None] == jnp.arange(N_EXPERTS, dtype=I32)[None, :]).astype(I32)
    csum = jnp.cumsum(onehot, axis=0)
    rank = jnp.sum(onehot * csum, axis=1) - 1
    counts = csum[-1]
    padded = (counts + tb - 1) // tb * tb
    pad_ends = jnp.cumsum(padded)
    pad_starts = pad_ends - padded
    dest = (jnp.sum(onehot * pad_starts[None, :], axis=1) + rank).astype(I32)
    n_rows = (a // tb + N_EXPERTS + 1) * tb
    n_blk = n_rows // tb
    blk_expert = jnp.minimum(
        jnp.searchsorted(pad_ends, jnp.arange(n_blk, dtype=I32) * tb, side="right"), N_EXPERTS - 1).astype(I32)
    n_used = (pad_ends[-1:] // tb).astype(I32)
    pad_lo = jnp.concatenate([pad_starts + counts, pad_ends[-1:]]).astype(I32)
    pad_hi = jnp.concatenate([pad_ends, jnp.full((1,), n_rows, I32)]).astype(I32)
    slot_assign = moe_invert(dest, pad_lo, pad_hi, n_rows, tb)
    y2 = moe_experts(h, slot_assign, blk_expert, n_used, to_bf16(wg), to_bf16(wu), to_bf16(wd), tb)
    return moe_mix(y2, x, top_p, mod)


def _invert_kernel(dest_ref, lo_ref, hi_ref, sa_ref, *, n_assign, tb, n_chunks):
    phase = pl.program_id(0)
    chunk = pl.program_id(1)

    @pl.when((phase == 0) & (chunk < lo_ref.shape[0]))
    def _():
        def fill(p, c):
            sa_ref[p] = n_assign + (p & (2 * tb - 1))
            return c

        lax.fori_loop(lo_ref[chunk], hi_ref[chunk], fill, 0)

    @pl.when(phase == 1)
    def _():
        per = n_assign // n_chunks

        def put(j, c):
            a = chunk * per + j
            sa_ref[dest_ref[a]] = a
            return c

        lax.fori_loop(0, per, put, 0, unroll=8)


def moe_invert(dest, pad_lo, pad_hi, n_rows, tb, n_chunks=16):
    n_assign = dest.shape[0]
    assert tb & (tb - 1) == 0 and n_assign % n_chunks == 0 and pad_lo.shape[0] <= n_chunks
    smem = pl.BlockSpec(memory_space=pltpu.SMEM)
    return pl.pallas_call(
        functools.partial(_invert_kernel, n_assign=n_assign, tb=tb, n_chunks=n_chunks),
        out_shape=jax.ShapeDtypeStruct((n_rows,), I32),
        grid=(2, n_chunks),
        in_specs=[smem, smem, smem],
        out_specs=smem,
        compiler_params=pltpu.CompilerParams(dimension_semantics=("arbitrary", "arbitrary")),
        name="moe_invert",
    )(dest, pad_lo, pad_hi)


def _expert_kernel(be_ref, nu_ref, sa_ref, h_hbm, wg_ref, wu_ref, wd_ref, y_hbm,
                   xin_ref, yout_ref, xb_ref, acc_ref, sem_in, sem_out, *, tb, n_tok, n_f):
    i = pl.program_id(0)
    f = pl.program_id(1)
    n_used = nu_ref[0]
    n_assign = n_tok * TOP_K
    rows_f = tb // n_f
    active = i <= n_used

    def gather_row(blk, r):
        a = sa_ref[blk * tb + r]
        tok = jnp.where(a < n_assign, a >> 1, 0)
        return pltpu.make_async_copy(h_hbm.at[tok], xin_ref.at[blk % 2, r], sem_in.at[blk % 2])

    def scatter_row(blk, r):
        a = jnp.where(blk >= 0, sa_ref[jnp.maximum(blk, 0) * tb + r], n_assign + tb + r)
        row = jnp.where(a < n_assign, (a & 1) * n_tok + (a >> 1), a)
        return pltpu.make_async_copy(yout_ref.at[(blk + 2) % 2, r], y_hbm.at[row], sem_out.at[(blk + 2) % 2])

    def for_rows(fn):
        def body(r, c):
            fn(r)
            return c
        lax.fori_loop(0, tb, body, 0, unroll=8)

    @pl.when(f == 0)
    def _():
        @pl.when(i == 0)
        def _():
            yout_ref[1] = jnp.zeros((tb, yout_ref.shape[2]), F32)
            for_rows(lambda r: gather_row(0, r).start())

            def clear_row(r):
                return pltpu.make_async_copy(yout_ref.at[1, r], y_hbm.at[n_assign + r], sem_out.at[0])

            for_rows(lambda r: clear_row(r).start())
            for_rows(lambda r: clear_row(r).wait())

        @pl.when((i == 0) | (i - 1 <= n_used))
        def _():
            for_rows(lambda r: gather_row(i, r).wait())

        @pl.when(active)
        def _():
            xb_ref[...] = xin_ref[i % 2].astype(BF16)
            acc_ref[...] = jnp.zeros_like(acc_ref)

    @pl.when(active)
    def _():
        for r in range(rows_f):
            gather_row(i + 1, f * rows_f + r).start()
            scatter_row(i - 1, f * rows_f + r).start()
        xb = xb_ref[...]
        a = jnp.dot(xb, wg_ref[0], preferred_element_type=F32)
        u = jnp.dot(xb, wu_ref[0], preferred_element_type=F32)
        act = (a * jax.nn.sigmoid(a) * u).astype(BF16)
        acc_ref[...] += jnp.dot(act, wd_ref[0], preferred_element_type=F32)

    @pl.when(f == n_f - 1)
    def _():
        @pl.when((i >= 1) & (i - 1 <= n_used))
        def _():
            for_rows(lambda r: scatter_row(i - 2, r).wait())

        @pl.when(active)
        def _():
            yout_ref[i % 2] = acc_ref[...]


def moe_experts(h, slot_assign, blk_expert, n_used, wg, wu, wd, tb, tf=1792):
    n_tok, d = h.shape
    p = slot_assign.shape[0]
    ff = wg.shape[2]
    return pl.pallas_call(
        functools.partial(_expert_kernel, tb=tb, n_tok=n_tok, n_f=ff // tf),
        out_shape=jax.ShapeDtypeStruct((n_tok * TOP_K + 2 * tb, d), F32),
        grid_spec=pltpu.PrefetchScalarGridSpec(
            num_scalar_prefetch=3,
            grid=(p // tb, ff // tf),
            in_specs=[pl.BlockSpec(memory_space=pl.ANY),
                      pl.BlockSpec((1, d, tf), lambda i, f, be, nu, sa: (be[i], 0, f)),
                      pl.BlockSpec((1, d, tf), lambda i, f, be, nu, sa: (be[i], 0, f)),
                      pl.BlockSpec((1, tf, d), lambda i, f, be, nu, sa: (be[i], f, 0))],
            out_specs=pl.BlockSpec(memory_space=pl.ANY),
            scratch_shapes=[pltpu.VMEM((2, tb, d), F32), pltpu.VMEM((2, tb, d), F32),
                            pltpu.VMEM((tb, d), BF16), pltpu.VMEM((tb, d), F32),
                            pltpu.SemaphoreType.DMA((2,)), pltpu.SemaphoreType.DMA((2,))]),
        compiler_params=pltpu.CompilerParams(dimension_semantics=("arbitrary", "arbitrary"),
                                             vmem_limit_bytes=VMEM_LIMIT, has_side_effects=True),
        name="moe_experts",
    )(blk_expert, n_used, slot_assign, h, wg, wu, wd)


def _mix_kernel(y0_ref, y1_ref, x_ref, p_ref, mod_ref, o_ref):
    w = p_ref[...]
    f = w[:, 0:1] * y0_ref[...] + w[:, 1:2] * y1_ref[...]
    o_ref[0] = x_ref[0] + mod_ref[0, 5:6, :] * f


def moe_mix(y2, x, top_p, mod, td=512):
    b, s, d = x.shape
    spb = s // td
    nt = b * spb
    return pl.pallas_call(
        _mix_kernel,
        out_shape=jax.ShapeDtypeStruct((b, s, d), F32),
        grid=(b, spb),
        in_specs=[pl.BlockSpec((td, d), lambda i, j: (i * spb + j, 0)),
                  pl.BlockSpec((td, d), lambda i, j: (nt + i * spb + j, 0)),
                  pl.BlockSpec((1, td, d), lambda i, j: (i, j, 0)),
                  pl.BlockSpec((td, N_EXPERTS), lambda i, j: (i * spb + j, 0)),
                  pl.BlockSpec((1, 6, d), lambda i, j: (i, 0, 0))],
        out_specs=pl.BlockSpec((1, td, d), lambda i, j: (i, j, 0)),
        compiler_params=_params("parallel", "parallel"),
        name="moe_mix",
    )(y2, y2, x, top_p, mod)


def _pack_w_in(w_in):
    kv = 2 * NSA_KV_HEADS * HEAD_DIM * 3
    w_in = to_bf16(w_in)
    o = 0
    nsa_q = w_in[:, o:o + MIX_WIDTH]; o += MIX_WIDTH
    nsa_kv = w_in[:, o:o + kv]; o += kv
    nsa_gate = w_in[:, o:o + NSA_HEADS * N_BRANCH]; o += NSA_HEADS * N_BRANCH
    sb = w_in[:, o:o + 3 * MIX_WIDTH]; o += 3 * MIX_WIDTH
    ml_qkv = w_in[:, o:o + 3 * MIX_WIDTH]; o += 3 * MIX_WIDTH
    ml_if = w_in[:, o:o + 2 * ML_HEADS]; o += 2 * ML_HEADS
    ml_o = w_in[:, o:o + MIX_WIDTH]; o += MIX_WIDTH
    merge = w_in[:, o:]
    main = jnp.concatenate([merge, nsa_q, ml_qkv, ml_o, sb, nsa_kv], axis=1)
    small = jnp.concatenate([nsa_gate, ml_if], axis=1)
    small = jnp.pad(small, ((0, 0), (0, N_SMALL - small.shape[1])))
    return main, small


def token_mixer_layer(x, mod, norm_g, w_in, nsa_q_norm, nsa_k_norm, cmp_pos, cmp_w1, cmp_b1, cmp_w2,
                      cmp_b2, ml_conv_w, ml_conv_b, ml_gate_b, w_branch, w_out):
    w_main, w_small = _pack_w_in(w_in)
    y3, small3 = in_projection(x, norm_g, mod, w_main, w_small)
    o_sb = sb_attention(y3)
    o_ml = mlstm(y3, small3, ml_conv_w, ml_conv_b, ml_gate_b)
    kc, vc = nsa_compress(y3, cmp_pos, cmp_w1, cmp_b1, cmp_w2, cmp_b2, nsa_k_norm[0])
    o_nsa = nsa_attention(y3, small3, nsa_q_norm, nsa_k_norm, kc, vc)
    return merge_project(o_nsa, o_sb, o_ml, y3, w_branch, w_out, x, mod)


def kernel(x, c, ada_w, ada_b, norm_mix, norm_ffn, w_in, nsa_q_norm, nsa_k_norm, cmp_pos, cmp_w1, cmp_b1,
           cmp_w2, cmp_b2, ml_conv_w, ml_conv_b, ml_gate_b, w_branch, w_out, ffn_wg, ffn_wu, ffn_wd,
           moe_router, moe_wg, moe_wu, moe_wd):
    depth = ada_w.shape[0]
    b, s, d = x.shape
    mods = adaln(c, ada_w, ada_b).reshape(depth, b, 6, d)
    for layer in range(depth):
        mod = mods[layer]
        x = token_mixer_layer(x, mod, norm_mix[layer], w_in[layer], nsa_q_norm[layer], nsa_k_norm[layer],
                              cmp_pos[layer], cmp_w1[layer], cmp_b1[layer], cmp_w2[layer], cmp_b2[layer],
                              ml_conv_w[layer], ml_conv_b[layer], ml_gate_b[layer], w_branch[layer],
                              w_out[layer])
        j = layer // 2
        if layer % 2 == 0:
            x = dense_ffn(x, norm_ffn[layer], mod, ffn_wg[j], ffn_wu[j], ffn_wd[j])
        else:
            x = moe_ffn(x, norm_ffn[layer], mod, moe_router[j], moe_wg[j], moe_wu[j], moe_wd[j])
    return x
```

```python
import functools

import numpy as np
import jax
import jax.numpy as jnp
from jax import lax
from jax.experimental import pallas as pl
from jax.experimental.pallas import tpu as pltpu

F32 = jnp.float32
BF16 = jnp.bfloat16
I32 = jnp.int32
HIGHEST = lax.Precision.HIGHEST

EPS = 1e-6
NEG = -1e30
HEAD_DIM = 64
MIX_WIDTH = 512
NSA_HEADS = 8
NSA_KV_HEADS = 2
NSA_GROUP = NSA_HEADS // NSA_KV_HEADS
CMP_BLOCK = 32
CMP_STRIDE = 16
SEL_BLOCK = 64
SEL_TOPK = 16
WINDOW = 512
FORCE_BONUS = 1e4
ML_HEADS = 4
ML_HEAD_DIM = 128
ML_CHUNK = 64
ML_UNROLL = 2
CONV_WIDTH = 4
N_BRANCH = 3
N_EXPERTS = 8
TOP_K = 2
LANES = 128

C_MERGE = 0
C_NSA_Q = 3072
C_ML_Q = 3584
C_ML_K = 4096
C_ML_V = 4608
C_ML_O = 5120
C_SB_Q = 5632
C_SB_K = 6144
C_SB_V = 6656
C_NSA_KV = 7168
N_MAIN = 7936
S_NSA_GATE = 0
S_ML_I = 24
S_ML_F = 28
N_SMALL = 128

VMEM_LIMIT = 56 * 1024 * 1024


def _params(*sem):
    return pltpu.CompilerParams(dimension_semantics=sem, vmem_limit_bytes=VMEM_LIMIT)


def _iota(shape, dim):
    return lax.broadcasted_iota(I32, shape, dim)


def _split_dot(a32, b_bf16):
    hi = a32.astype(BF16)
    lo = (a32 - hi.astype(F32)).astype(BF16)
    return (jnp.dot(hi, b_bf16, preferred_element_type=F32)
            + jnp.dot(lo, b_bf16, preferred_element_type=F32))


def _split_dot_left(a_bf16, b32):
    hi = b32.astype(BF16)
    lo = (b32 - hi.astype(F32)).astype(BF16)
    return (jnp.dot(a_bf16, hi, preferred_element_type=F32)
            + jnp.dot(a_bf16, lo, preferred_element_type=F32))


def _dot_nt(a, b):
    return lax.dot_general(a, b, (((1,), (1,)), ((), ())), preferred_element_type=F32)


def _log_sigmoid(z):
    return jnp.minimum(z, 0.0) - jnp.log1p(jnp.exp(-jnp.abs(z)))


def _cast_kernel(x_ref, o_ref):
    o_ref[...] = x_ref[...].astype(o_ref.dtype)


def to_bf16(w, max_rows=512):
    cols = w.shape[-1]
    w2 = w.reshape(-1, cols)
    rows = w2.shape[0]
    tr = max(t for t in range(8, max_rows + 1, 8) if rows % t == 0)
    out = pl.pallas_call(
        _cast_kernel,
        out_shape=jax.ShapeDtypeStruct((rows, cols), BF16),
        grid=(rows // tr,),
        in_specs=[pl.BlockSpec((tr, cols), lambda i: (i, 0))],
        out_specs=pl.BlockSpec((tr, cols), lambda i: (i, 0)),
        compiler_params=_params("parallel"),
        name="to_bf16",
    )(w2)
    return out.reshape(w.shape)


def _adaln_kernel(c_ref, w_ref, b_ref, o_ref):
    c = c_ref[...]
    cond = c * jax.nn.sigmoid(c)
    o_ref[0] = jnp.dot(cond, w_ref[0], precision=HIGHEST, preferred_element_type=F32) + b_ref[0]


def adaln(c, ada_w, ada_b):
    depth, d, n = ada_w.shape
    b = c.shape[0]
    tn = 1536
    return pl.pallas_call(
        _adaln_kernel,
        out_shape=jax.ShapeDtypeStruct((depth, b, n), F32),
        grid=(depth, n // tn),
        in_specs=[pl.BlockSpec((b, d), lambda l, j: (0, 0)),
                  pl.BlockSpec((1, d, tn), lambda l, j: (l, 0, j)),
                  pl.BlockSpec((1, 1, tn), lambda l, j: (l, 0, j))],
        out_specs=pl.BlockSpec((1, b, tn), lambda l, j: (l, 0, j)),
        compiler_params=_params("parallel", "parallel"),
        name="adaln",
    )(c, ada_w, ada_b.reshape(depth, 1, n))


def _norm_mod(x, g, mod, shift_row, scale_row):
    ms = jnp.mean(x * x, axis=-1, keepdims=True)
    y = x * lax.rsqrt(ms + EPS) * g
    return y * (1.0 + mod[scale_row:scale_row + 1, :]) + mod[shift_row:shift_row + 1, :]


def _in_proj_kernel(x_ref, g_ref, mod_ref, wm_ref, ws_ref, y_ref, sm_ref, h_ref):
    @pl.when(pl.program_id(2) == 0)
    def _():
        h_ref[...] = _norm_mod(x_ref[0], g_ref[...], mod_ref[0], 0, 1).astype(BF16)
        sm_ref[0] = jnp.dot(h_ref[...], ws_ref[...], preferred_element_type=F32)

    y_ref[0] = jnp.dot(h_ref[...], wm_ref[...], preferred_element_type=F32).astype(y_ref.dtype)


def in_projection(x, g, mod, w_main, w_small, tm=512, n_tiles=2):
    b, s, d = x.shape
    tn = N_MAIN // n_tiles
    return pl.pallas_call(
        _in_proj_kernel,
        out_shape=(jax.ShapeDtypeStruct((b, s, N_MAIN), BF16), jax.ShapeDtypeStruct((b, s, N_SMALL), F32)),
        grid=(b, s // tm, n_tiles),
        in_specs=[pl.BlockSpec((1, tm, d), lambda i, j, n: (i, j, 0)),
                  pl.BlockSpec((1, d), lambda i, j, n: (0, 0)),
                  pl.BlockSpec((1, 6, d), lambda i, j, n: (i, 0, 0)),
                  pl.BlockSpec((d, tn), lambda i, j, n: (0, n)),
                  pl.BlockSpec((d, N_SMALL), lambda i, j, n: (0, 0))],
        out_specs=(pl.BlockSpec((1, tm, tn), lambda i, j, n: (i, j, n)),
                   pl.BlockSpec((1, tm, N_SMALL), lambda i, j, n: (i, j, 0))),
        scratch_shapes=[pltpu.VMEM((tm, d), BF16)],
        compiler_params=_params("parallel", "parallel", "arbitrary"),
        name="in_projection",
    )(x, g.reshape(1, d), mod, w_main, w_small)


SB_EXP_FLOOR = -104.0
SB_EAGER = 2


def _sb_kernel(q_ref, k_ref, v_ref, o_ref, vt_ref, *, tq, n_str):
    TK, dh = LANES, HEAD_DIM
    n_sub = tq // TK
    W = 2 * tq
    SR = range(n_str)
    qi = pl.program_id(2)

    @pl.when(qi == 0)
    def _():
        for c in range(v_ref.shape[1] // tq):
            v_t = v_ref[0, c * tq:(c + 1) * tq, :].astype(F32).T.astype(BF16)
            for p in SR:
                vt_ref[p, c] = v_t[p * LANES:(p + 1) * LANES]

    q_t = (q_ref[0].astype(F32) * (dh ** -0.5)).T
    chan = _iota((2 * dh, tq), 0)
    q_cat = []
    for p in SR:
        q_p = q_t[p * LANES:(p + 1) * LANES]
        q_cat.append(jnp.concatenate([jnp.where(chan < dh, q_p, 0.0), jnp.where(chan < dh, 0.0, q_p)],
                                     axis=1).astype(BF16))
    later = (_iota((TK, TK), 0) < _iota((TK, TK), 1)).astype(BF16)
    suffix = jnp.concatenate([jnp.concatenate([later, later], axis=1), jnp.ones((8, 2 * TK), BF16)], axis=0)

    def steps(blocks, st):
        work = [(p, b) for p in SR for b in range(len(blocks))]
        lss, his, los = {}, {}, {}
        for p, b in work:
            j, keep = blocks[b]
            k0 = pl.multiple_of(j * tq, tq)
            z = jnp.dot(k_ref[0, pl.ds(k0, tq), p * LANES:(p + 1) * LANES], q_cat[p],
                        preferred_element_type=F32)
            lk = -(jnp.maximum(z, 0.0) + jnp.log(1.0 + jnp.exp(-jnp.abs(z))))
            lss[p, b] = lk + z
            if keep is not None:
                lk = jnp.where(keep, lk, 0.0)
            his[p, b] = lk.astype(BF16)
            los[p, b] = (lk - his[p, b].astype(F32)).astype(BF16)
        carry = [st[p][0] for p in SR]
        afters = {}
        for p, b in work:
            after = [None] * n_sub
            for sub in range(n_sub - 1, -1, -1):
                rows = slice(sub * TK, (sub + 1) * TK)
                res = jnp.dot(suffix, jnp.concatenate([his[p, b][rows], los[p, b][rows]], axis=0),
                              preferred_element_type=F32)
                after[sub] = res[0:TK] + carry[p]
                carry[p] = carry[p] + res[TK:TK + 1]
            afters[p, b] = jnp.concatenate(after, axis=0)
        acc = [st[p][1] for p in SR]
        for p, b in work:
            j, keep = blocks[b]
            a = jnp.exp(lss[p, b] + afters[p, b])
            if keep is not None:
                a = jnp.where(keep, a, 0.0)
            acc[p] = acc[p] + jnp.dot(vt_ref[p, j], a.astype(BF16), preferred_element_type=F32)
        return [(carry[p], acc[p]) for p in SR]

    def cond(c):
        live = jnp.max(c[1][0][0])
        for p in range(1, n_str):
            live = jnp.maximum(live, jnp.max(c[1][p][0]))
        return (c[0] >= 0) & (live > SB_EXP_FLOOR)

    def body(c):
        return c[0] - 1, steps([(c[0], None)], c[1])

    strict = _iota((tq, W), 0) < (_iota((tq, W), 1) & (tq - 1))
    eager = [(jnp.maximum(qi - d, 0), qi >= d) for d in range(1, SB_EAGER + 1)]
    st = steps([(qi, strict)] + eager, [(jnp.zeros((1, W), F32), jnp.zeros((2 * dh, W), F32)) for p in SR])
    _, st = lax.while_loop(cond, body, (qi - 1 - SB_EAGER, st))
    out_rows = []
    for p in SR:
        out_rows += [st[p][1][0:dh, 0:tq], st[p][1][dh:2 * dh, tq:W]]
    o_ref[0] = jnp.concatenate(out_rows, axis=0).T.astype(o_ref.dtype)


def sb_attention(y3, tq=256, n_str=2):
    b, s, _ = y3.shape
    wide = n_str * LANES
    qb, kb, vb = C_SB_Q // wide, C_SB_K // wide, C_SB_V // wide
    return pl.pallas_call(
        functools.partial(_sb_kernel, tq=tq, n_str=n_str),
        out_shape=jax.ShapeDtypeStruct((b, s, MIX_WIDTH), BF16),
        grid=(b, MIX_WIDTH // wide, s // tq),
        in_specs=[pl.BlockSpec((1, tq, wide), lambda i, p, j: (i, j, qb + p)),
                  pl.BlockSpec((1, s, wide), lambda i, p, j: (i, 0, kb + p)),
                  pl.BlockSpec((1, s, wide), lambda i, p, j: (i, 0, vb + p))],
        out_specs=pl.BlockSpec((1, tq, wide), lambda i, p, j: (i, j, p)),
        scratch_shapes=[pltpu.VMEM((n_str, s // tq, LANES, tq), BF16)],
        compiler_params=_params("parallel", "parallel", "arbitrary"),
        name="sb_attention",
    )(y3, y3, y3)


def _mlstm_kernel(q_ref, k_ref, v_ref, og_ref, sm_ref, gr_ref, cw_ref, cb_ref, gb_ref, out_ref,
                  ct_ref, n_ref, m_ref, xbuf_ref, qk_ref, *, ts):
    L, dh, H, W = ML_CHUNK, ML_HEAD_DIM, ML_HEADS, MIX_WIDTH
    halo = 8
    sblk = pl.program_id(1)

    @pl.when(sblk == 0)
    def _():
        ct_ref[...] = jnp.zeros_like(ct_ref)
        n_ref[...] = jnp.zeros_like(n_ref)
        m_ref[...] = jnp.zeros_like(m_ref)
        xbuf_ref[0:halo, :] = jnp.zeros((halo, 2 * W), F32)

    @pl.when(sblk > 0)
    def _():
        xbuf_ref[0:halo, :] = xbuf_ref[ts:ts + halo, :]

    xbuf_ref[halo:halo + ts, 0:W] = q_ref[0].astype(F32)
    xbuf_ref[halo:halo + ts, W:2 * W] = k_ref[0].astype(F32)
    conv = cb_ref[...] + jnp.zeros((ts, 2 * W), F32)
    for j in range(CONV_WIDTH):
        off = halo - (CONV_WIDTH - 1) + j
        conv = conv + cw_ref[j:j + 1, :] * xbuf_ref[off:off + ts, :]
    act = conv * jax.nn.sigmoid(conv)
    qk_ref[:, 0:W] = (act[:, 0:W] * (dh ** -0.5)).astype(BF16)
    qk_ref[:, W:2 * W] = act[:, W:2 * W].astype(BF16)

    it0, it1 = _iota((L, L), 0), _iota((L, L), 1)
    causal = it0 >= it1
    tri_lo = causal.astype(BF16)
    tri_up = (it0 <= it1).astype(BF16)

    def chunks(it, carry):
        HR = range(H)
        UR = range(ML_UNROLL)
        UH = [(u, h) for u in UR for h in HR]
        cols = [slice(h * dh, (h + 1) * dh) for h in HR]
        rows = [pl.ds(pl.multiple_of((it * ML_UNROLL + u) * L, L), L) for u in UR]
        sm = [sm_ref[0, rows[u], :] for u in UR]
        gr = [gr_ref[0, it * ML_UNROLL + u] for u in UR]
        ig_col = {(u, h): sm[u][:, S_ML_I + h:S_ML_I + h + 1] + gb_ref[0, h] for u, h in UH}
        lf_col = {(u, h): _log_sigmoid(sm[u][:, S_ML_F + h:S_ML_F + h + 1] + gb_ref[1, h]) for u, h in UH}
        ig_row = {(u, h): gr[u][h:h + 1, :] + gb_ref[0, h] for u, h in UH}
        lf_row = {(u, h): _log_sigmoid(gr[u][H + h:H + h + 1, :] + gb_ref[1, h]) for u, h in UH}
        b_t = {k: _split_dot_left(tri_lo, jnp.broadcast_to(lf_col[k], (L, L))) for k in UH}
        b_s = {k: _split_dot(jnp.broadcast_to(lf_row[k], (L, L)), tri_up) for k in UH}
        qq = {(u, h): qk_ref[rows[u], cols[h]] for u, h in UH}
        kk = {(u, h): qk_ref[rows[u], W + h * dh:W + (h + 1) * dh] for u, h in UH}
        vv = {(u, h): v_ref[0, rows[u], cols[h]] for u, h in UH}
        qk = {k: _dot_nt(qq[k], kk[k]) for k in UH}
        kt = {k: kk[k].astype(F32).T.astype(BF16) for k in UH}
        dmat = {k: jnp.where(causal, b_t[k] - b_s[k] + ig_row[k], NEG) for k in UH}
        d_max = {k: jnp.max(dmat[k], axis=1, keepdims=True) for k in UH}
        b_col = {k: b_t[k][:, 0:1] for k in UH}
        b_last = {k: b_t[k][L - 1:L, 0:1] for k in UH}
        decay = {k: b_last[k] - b_col[k] + ig_col[k] for k in UH}
        decay_max = {k: jnp.max(decay[k], axis=0, keepdims=True) for k in UH}
        m_prev, m_new = {}, {}
        for u, h in UH:
            m_prev[u, h] = m_ref[h][:, 0:1] if u == 0 else m_new[u - 1, h]
            m_new[u, h] = jnp.maximum(b_last[u, h] + m_prev[u, h], decay_max[u, h])
        m_inter = {k: b_col[k] + m_prev[k] for k in UH}
        m_t = {k: jnp.maximum(m_inter[k], d_max[k]) for k in UH}
        w = {k: jnp.exp(dmat[k] - m_t[k]) * qk[k] for k in UH}
        inter = {k: jnp.exp(m_inter[k] - m_t[k]) for k in UH}
        w_v = {k: jnp.dot(w[k].astype(BF16), vv[k], preferred_element_type=F32) for k in UH}
        ws = {k: jnp.exp(decay[k] - m_new[k]) for k in UH}
        cscale = {k: jnp.exp(b_last[k] + m_prev[k] - m_new[k]) for k in UH}
        wv = {k: (ws[k] * vv[k].astype(F32)).astype(BF16) for k in UH}
        k_wv = {k: jnp.dot(kt[k], wv[k], preferred_element_type=F32) for k in UH}
        k_ws = {k: jnp.sum(ws[k] * kk[k].astype(F32), axis=0, keepdims=True) for k in UH}
        w_sum = {k: jnp.sum(w[k], axis=1, keepdims=True) for k in UH}
        ct = {h: ct_ref[h] for h in HR}
        nvec = {h: n_ref[h] for h in HR}
        for u, h in UH:
            k = (u, h)
            num = inter[k] * jnp.dot(qq[k], ct[h].astype(BF16), preferred_element_type=F32) + w_v[k]
            den = inter[k] * jnp.sum(qq[k].astype(F32) * nvec[h], axis=1, keepdims=True) + w_sum[k]
            hval = num / jnp.maximum(jnp.abs(den), jnp.exp(-m_t[k]))
            ct[h] = cscale[k] * ct[h] + k_wv[k]
            nvec[h] = cscale[k] * nvec[h] + k_ws[k]
            gate = jax.nn.sigmoid(og_ref[0, rows[u], cols[h]].astype(F32))
            out_ref[0, rows[u], cols[h]] = (gate * hval).astype(out_ref.dtype)
        for h in HR:
            ct_ref[h] = ct[h]
            n_ref[h] = nvec[h]
            m_ref[h] = jnp.broadcast_to(m_new[ML_UNROLL - 1, h], (1, LANES))
        return carry

    lax.fori_loop(0, ts // (L * ML_UNROLL), chunks, 0)


def mlstm(y3, small3, conv_w, conv_b, gate_b, ts=512):
    b, s, _ = y3.shape
    W, H, L = MIX_WIDTH, ML_HEADS, ML_CHUNK
    gr = small3[:, :, S_ML_I:S_ML_I + 2 * H].reshape(b, s // L, L, 2 * H).transpose(0, 1, 3, 2)
    cq, ck, cv, co = C_ML_Q // W, C_ML_K // W, C_ML_V // W, C_ML_O // W
    return pl.pallas_call(
        functools.partial(_mlstm_kernel, ts=ts),
        out_shape=jax.ShapeDtypeStruct((b, s, W), BF16),
        grid=(b, s // ts),
        in_specs=[pl.BlockSpec((1, ts, W), lambda i, j: (i, j, cq)),
                  pl.BlockSpec((1, ts, W), lambda i, j: (i, j, ck)),
                  pl.BlockSpec((1, ts, W), lambda i, j: (i, j, cv)),
                  pl.BlockSpec((1, ts, W), lambda i, j: (i, j, co)),
                  pl.BlockSpec((1, ts, N_SMALL), lambda i, j: (i, j, 0)),
                  pl.BlockSpec((1, ts // L, 2 * H, L), lambda i, j: (i, j, 0, 0)),
                  pl.BlockSpec((CONV_WIDTH, 2 * W), lambda i, j: (0, 0)),
                  pl.BlockSpec((1, 2 * W), lambda i, j: (0, 0)),
                  pl.BlockSpec(memory_space=pltpu.SMEM)],
        out_specs=pl.BlockSpec((1, ts, W), lambda i, j: (i, j, 0)),
        scratch_shapes=[pltpu.VMEM((H, ML_HEAD_DIM, ML_HEAD_DIM), F32),
                        pltpu.VMEM((H, 1, ML_HEAD_DIM), F32),
                        pltpu.VMEM((H, 1, LANES), F32),
                        pltpu.VMEM((ts + 8, 2 * W), F32),
                        pltpu.VMEM((ts, 2 * W), BF16)],
        compiler_params=_params("parallel", "arbitrary"),
        name="mlstm",
    )(y3, y3, y3, y3, small3, gr, conv_w, conv_b.reshape(1, 2 * W), gate_b)


def _gelu_tanh(x):
    return 0.5 * x * (1.0 + jnp.tanh(0.7978845608028654 * (x + 0.044715 * (x * x * x))))


def _compress_kernel(ra_ref, rb_ref, pos_ref, w1_ref, b1_ref, w2_ref, b2_ref, kn_ref, kc_ref, vc_ref):
    half = (CMP_BLOCK // 2) * HEAD_DIM
    for j, o_ref in enumerate((kc_ref, vc_ref)):
        xa = (ra_ref[j, 0, 0].astype(F32) + pos_ref[j, :, 0:half]).astype(BF16)
        xb = (rb_ref[j, 0, 0].astype(F32) + pos_ref[j, :, half:2 * half]).astype(BF16)
        hid = (jnp.dot(xa, w1_ref[j, 0:half, :], preferred_element_type=F32)
               + jnp.dot(xb, w1_ref[j, half:2 * half, :], preferred_element_type=F32) + b1_ref[j])
        out = jnp.dot(_gelu_tanh(hid).astype(BF16), w2_ref[j], preferred_element_type=F32) + b2_ref[j]
        if j == 0:
            out = out * lax.rsqrt(jnp.mean(out * out, axis=-1, keepdims=True) + EPS) * kn_ref[...]
        o_ref[0, 0] = out


def nsa_compress(y3, cmp_pos, cmp_w1, cmp_b1, cmp_w2, cmp_b2, k_norm0):
    b, s, _ = y3.shape
    G, dh = NSA_KV_HEADS, HEAD_DIM
    nr = s // CMP_STRIDE
    wide = CMP_STRIDE * dh
    kv = y3[:, :, C_NSA_KV:C_NSA_KV + 2 * G * dh].reshape(b, s, 2, G, dh)
    ra = kv.transpose(2, 0, 3, 1, 4).reshape(2, b, G, nr, wide)
    rb = jnp.concatenate([ra[:, :, :, 1:], jnp.zeros((2, b, G, 1, wide), ra.dtype)], axis=3)
    hidden = cmp_w1.shape[-1]
    out = jax.ShapeDtypeStruct((b, G, nr, dh), F32)
    blk = pl.BlockSpec((2, 1, 1, nr, wide), lambda i, g: (0, i, g, 0, 0))
    oblk = pl.BlockSpec((1, 1, nr, dh), lambda i, g: (i, g, 0, 0))

    def full(shape):
        return pl.BlockSpec(shape, lambda i, g: (0,) * len(shape))

    return pl.pallas_call(
        _compress_kernel,
        out_shape=(out, out),
        grid=(b, G),
        in_specs=[blk, blk, full((2, 1, 2 * wide)), full((2, 2 * wide, hidden)), full((2, 1, hidden)),
                  full((2, hidden, dh)), full((2, 1, dh)), full((1, dh))],
        out_specs=(oblk, oblk),
        compiler_params=_params("parallel", "parallel"),
        name="nsa_compress",
    )(ra, rb, cmp_pos.reshape(2, 1, 2 * wide), cmp_w1.astype(BF16), cmp_b1.reshape(2, 1, hidden),
      cmp_w2.astype(BF16), cmp_b2.reshape(2, 1, dh), k_norm0.reshape(1, dh))


NSA_QB = 128
NSA_QBLK = 2
NSA_KS = 512
NSA_CK = 256
NSA_VR = 80
A_FEAT, A_PEN, A_BIAS = 0, 64, 128
A_DUMMY = A_BIAS + 4


def _head_rms(x, gain):
    w = x.shape[1]
    same_head = (_iota((w, w), 0) // HEAD_DIM == _iota((w, w), 1) // HEAD_DIM).astype(BF16)
    ss = _split_dot(x * x, same_head)
    return x * lax.rsqrt(ss * (1.0 / HEAD_DIM) + EPS) * gain


def _nsa_kernel(q_ref, gl_ref, kca_ref, vcT_ref, ks_ref, kw_ref, vs_ref, vw_ref, kconst_ref, gq_ref, gk_ref,
                     o_ref, qt_ref, ksa_ref, kwa_ref, vsa_ref, vwa_ref, s_ref, *, n_sel, top):
    QB, R, dh, CK, VR, KS = NSA_QB, NSA_GROUP, HEAD_DIM, NSA_CK, NSA_VR, NSA_KS
    G = NSA_KV_HEADS
    GR = range(G)
    UR = range(NSA_QBLK)
    UG = [(u, g) for u in UR for g in GR]
    HQ = R * QB
    NPAD = WINDOW // QB
    SUB = KS // QB
    step = pl.program_id(1)
    qi = [step * NSA_QBLK + u for u in UR]
    q0 = [qi[u] * QB for u in UR]
    nkb = vsa_ref.shape[1]

    @pl.when(step == 0)
    def _():
        pad_keys = jnp.where(_iota((WINDOW, CK), 1) == A_DUMMY, 1.0, 0.0).astype(BF16)
        ones_rows = jnp.where(_iota((nkb + NPAD, VR - dh, QB), 1) == 0, 1.0, 0.0).astype(BF16)
        for g in GR:
            heads = slice(g * dh, (g + 1) * dh)
            ksa_ref[g] = kconst_ref[...]
            kwa_ref[g, 0:WINDOW, :] = pad_keys
            kwa_ref[g, WINDOW:, :] = kconst_ref[...]
            kwa_ref[g, WINDOW:, A_PEN:A_PEN + 64] = jnp.zeros((kwa_ref.shape[1] - WINDOW, 64), BF16)
            vwa_ref[g, 0:NPAD, 0:dh, :] = jnp.zeros((NPAD, dh, QB), BF16)
            vsa_ref[g, :, dh:VR, :] = ones_rows[0:nkb]
            vwa_ref[g, :, dh:VR, :] = ones_rows
        for c in range(ks_ref.shape[1] // KS):
            keys = slice(c * KS, (c + 1) * KS)
            ks_n = _head_rms(ks_ref[0, keys, :].astype(F32), gk_ref[0]).astype(BF16)
            kw_n = _head_rms(kw_ref[0, keys, :].astype(F32), gk_ref[1]).astype(BF16)
            for g in GR:
                heads = slice(g * dh, (g + 1) * dh)
                ksa_ref[g, keys, A_FEAT:A_FEAT + dh] = ks_n[:, heads]
                kwa_ref[g, WINDOW + c * KS:WINDOW + (c + 1) * KS, A_FEAT:A_FEAT + dh] = kw_n[:, heads]
        for c in range(nkb):
            keys = slice(c * QB, (c + 1) * QB)
            vs_t = vs_ref[0, keys, :].astype(F32).T.astype(BF16)
            vw_t = vw_ref[0, keys, :].astype(F32).T.astype(BF16)
            for g in GR:
                heads = slice(g * dh, (g + 1) * dh)
                vsa_ref[g, c, 0:dh, :] = vs_t[heads]
                vwa_ref[g, NPAD + c, 0:dh, :] = vw_t[heads]
        qt_ref[:, :, A_BIAS + 16:CK, :] = jnp.zeros((NSA_QBLK, G, CK - A_BIAS - 16, HQ), BF16)

    q_n = _head_rms(q_ref[0].astype(F32), gq_ref[...]).astype(BF16)
    q_rows = (q_n.astype(F32) * (dh ** -0.5)).T
    lane = _iota((16, HQ), 1)
    rowi = _iota((16, HQ), 0)
    for u, g in UG:
        t_q = q0[u] + (lane & (QB - 1))
        t_hi = ((t_q >> 6) << 6).astype(F32)
        t_lo = (t_q & 63).astype(F32)
        qT = jnp.concatenate([q_rows[(g * R + r) * dh:(g * R + r + 1) * dh, u * QB:(u + 1) * QB]
                              for r in range(R)], axis=1).astype(BF16)
        qt_ref[u, g, A_FEAT:A_FEAT + dh, :] = qT
        qt_ref[u, g, A_PEN:A_PEN + dh, :] = qT
        slope = jnp.exp2(-(g * R + (lane >> 7) + 1).astype(F32))
        bias_rows = jnp.where(rowi < 2, slope,
                              jnp.where(rowi == 2, -slope * t_hi,
                                        jnp.where(rowi == 3, -slope * t_lo,
                                                  jnp.where(rowi == A_DUMMY - A_BIAS, NEG, 0.0))))
        qt_ref[u, g, A_BIAS:A_BIAS + 16, :] = bias_rows.astype(BF16)
    k_loc = _iota((QB, HQ), 0)
    q_loc = _iota((QB, HQ), 1) & (QB - 1)

    def pv(v_ref_, g, kb0, pr):
        out = None
        for i in range(pr.shape[0] // QB):
            term = jnp.dot(v_ref_[g, kb0 + i], pr[i * QB:(i + 1) * QB], preferred_element_type=F32)
            out = term if out is None else out + term
        return out

    n_cmp = kca_ref.shape[2]
    cmp_end = _iota((n_cmp, HQ), 0) * CMP_STRIDE + (CMP_BLOCK - 1)
    valid = [cmp_end <= q0[u] + (_iota((n_cmp, HQ), 1) & (QB - 1)) for u in UR]
    sc = {(u, g): jnp.where(valid[u], jnp.dot(kca_ref[0, g], qt_ref[u, g], preferred_element_type=F32), NEG)
          for u, g in UG}
    mx = {k: jnp.max(sc[k], axis=0, keepdims=True) for k in UG}
    e = {k: jnp.exp(sc[k] - mx[k]) for k in UG}
    inv = {k: jnp.where(mx[k] > 0.5 * NEG, 1.0 / jnp.sum(e[k], axis=0, keepdims=True), 0.0) for k in UG}
    p = {k: e[k] * inv[k] for k in UG}
    o_cmp = {(u, g): jnp.dot(vcT_ref[0, g], p[u, g].astype(BF16), preferred_element_type=F32)
             for u, g in UG}

    sw = {(u, g): jnp.dot(kwa_ref[g, pl.ds(pl.multiple_of(q0[u], QB), WINDOW + QB), :], qt_ref[u, g],
                          preferred_element_type=F32) for u, g in UG}
    sw = {k: jnp.concatenate([jnp.where(k_loc > q_loc, sw[k][0:QB], NEG), sw[k][QB:WINDOW],
                              jnp.where(k_loc <= q_loc, sw[k][WINDOW:WINDOW + QB], NEG)], axis=0) for k in UG}
    pw = {k: jnp.exp(sw[k] - jnp.max(sw[k], axis=0, keepdims=True)).astype(BF16) for k in UG}
    acc_w = {(u, g): pv(vwa_ref, g, qi[u], pw[u, g]) for u, g in UG}
    o_win = {k: acc_w[k][0:dh] / acc_w[k][dh:dh + 1] for k in UG}

    c0 = _iota((n_sel, n_cmp), 1) * CMP_STRIDE
    s0 = _iota((n_sel, n_cmp), 0) * SEL_BLOCK
    overlap_t = ((c0 < s0 + SEL_BLOCK) & (c0 + CMP_BLOCK > s0)).astype(BF16)
    j_idx = _iota((n_sel, QB), 0)
    tq = [q0[u] + _iota((n_sel, QB), 1) for u in UR]
    forced = [(j_idx == 0) | (j_idx == (tq[u] >> 6)) | (j_idx == (tq[u] >> 6) - 1) for u in UR]
    causal_blk = [j_idx * SEL_BLOCK <= tq[u] for u in UR]
    p_grp = {k: p[k][:, 0:QB] + p[k][:, QB:2 * QB] + p[k][:, 2 * QB:3 * QB] + p[k][:, 3 * QB:4 * QB] for k in UG}
    imp = {(u, g): jnp.where(causal_blk[u], _split_dot_left(overlap_t, p_grp[u, g])
                             + jnp.where(forced[u], FORCE_BONUS, 0.0), -1.0) for u, g in UG}
    sel = {k: jnp.zeros((n_sel, QB), F32) for k in UG}
    for _ in range(top):
        for k in UG:
            top_v = jnp.max(imp[k], axis=0, keepdims=True)
            first = jnp.min(jnp.where(imp[k] == top_v, j_idx, n_sel), axis=0, keepdims=True)
            pick = j_idx == first
            sel[k] = jnp.where(pick, 1.0, sel[k])
            imp[k] = jnp.where(pick, -3e38, imp[k])
    for u, g in UG:
        pen = jnp.where((sel[u, g] > 0.5) & causal_blk[u], 0.0, NEG)
        if n_sel < 64:
            pen = jnp.concatenate([pen, jnp.zeros((64 - n_sel, QB), F32)], axis=0)
        qt_ref[u, g, A_PEN:A_PEN + 64, :] = jnp.concatenate([pen] * R, axis=1).astype(BF16)

    def score(j):
        rows = pl.ds(pl.multiple_of(j * KS, KS), KS)
        return [jnp.dot(ksa_ref[g, rows, :], qt_ref[u, g], preferred_element_type=F32) for u, g in UG]

    def absorb(s, j, st):
        m_new = [jnp.maximum(st[n][0], jnp.max(s[n], axis=0, keepdims=True)) for n in range(len(UG))]
        pr = [jnp.exp(s[n] - m_new[n]).astype(BF16) for n in range(len(UG))]
        return [(m_new[n], jnp.exp(st[n][0] - m_new[n]) * st[n][1] + pv(vsa_ref, UG[n][1], j * SUB, pr[n]))
                for n in range(len(UG))]

    n_full = qi[0] // SUB
    init = [(jnp.full((1, HQ), NEG, F32), jnp.zeros((VR, HQ), F32)) for _ in UG]
    st = lax.fori_loop(0, n_full, lambda j, st_: absorb(score(j), j, st_), init)
    s_last = score(n_full)
    for n, (u, g) in enumerate(UG):
        diag = pl.ds(pl.multiple_of(q0[u] - n_full * KS, QB), QB)
        s_ref[u, g] = s_last[n]
        s_ref[u, g, diag, :] = jnp.where(k_loc <= q_loc, s_ref[u, g, diag, :], NEG)
    st = absorb([s_ref[u, g] for u, g in UG], n_full, st)
    o_sel = {UG[n]: st[n][1][0:dh] / st[n][1][dh:dh + 1] for n in range(len(UG))}

    gl_t = gl_ref[0].T
    for u in UR:
        rows_out = []
        for g in GR:
            gate = [jax.nn.sigmoid(jnp.concatenate(
                [gl_t[(g * R + r) * N_BRANCH + br:(g * R + r) * N_BRANCH + br + 1, u * QB:(u + 1) * QB]
                 for r in range(R)], axis=1)) for br in range(N_BRANCH)]
            o_t = gate[0] * o_cmp[u, g] + gate[1] * o_sel[u, g] + gate[2] * o_win[u, g]
            rows_out += [o_t[:, r * QB:(r + 1) * QB] for r in range(R)]
        o_ref[0, u * QB:(u + 1) * QB, :] = jnp.concatenate(rows_out, axis=0).T.astype(o_ref.dtype)


def nsa_attention(y3, small3, q_norm, k_norm, kc, vc):
    b, s, _ = y3.shape
    G, R, dh, QB, CK, VR = NSA_KV_HEADS, NSA_GROUP, HEAD_DIM, NSA_QB, NSA_CK, NSA_VR
    assert (NSA_KS // QB) % NSA_QBLK == 0, "a step's query blocks must share their span of NSA_KS keys"
    HQ = R * QB
    nq = s // QB
    n_sel = s // SEL_BLOCK
    assert n_sel <= 64, "selection one-hot columns hold at most 64 blocks"
    top = min(SEL_TOPK, n_sel)
    n_cmp = kc.shape[2]

    def pos_cols(pos):
        return np.stack([pos // 64 * 64, pos % 64, np.ones_like(pos), np.ones_like(pos)], axis=1)

    vc_t = vc.transpose(0, 1, 3, 2).astype(BF16)

    pos = np.arange(s)
    kconst = np.zeros((s, CK), np.float32)
    kconst[pos, A_PEN + pos // SEL_BLOCK] = 1.0
    kconst[:, A_BIAS:A_BIAS + 4] = pos_cols(pos)
    kconst = jnp.asarray(kconst, BF16)

    kc_hi = kc.astype(BF16)
    kc_lo = (kc - kc_hi.astype(F32)).astype(BF16)
    cend = np.arange(n_cmp) * CMP_STRIDE + (CMP_BLOCK - 1)
    cbias = np.zeros((n_cmp, CK - 2 * dh), np.float32)
    cbias[:, 0:4] = pos_cols(cend)
    kc_aug = jnp.concatenate([kc_hi, kc_lo, jnp.broadcast_to(jnp.asarray(cbias, BF16), (b, G, n_cmp, CK - 2 * dh))],
                             axis=-1)

    qw = G * R * dh
    kvb = C_NSA_KV // LANES
    gq = jnp.tile(q_norm, G * R).reshape(1, qw)
    gk = jnp.stack([jnp.tile(k_norm[1], G), jnp.tile(k_norm[2], G)]).reshape(2, 1, LANES)

    def kv_spec(blk):
        return pl.BlockSpec((1, s, LANES), lambda i, j: (i, 0, kvb + blk))

    return pl.pallas_call(
        functools.partial(_nsa_kernel, n_sel=n_sel, top=top),
        out_shape=jax.ShapeDtypeStruct((b, s, MIX_WIDTH), BF16),
        grid=(b, nq // NSA_QBLK),
        in_specs=[pl.BlockSpec((1, NSA_QBLK * QB, qw), lambda i, j: (i, j, C_NSA_Q // qw)),
                  pl.BlockSpec((1, NSA_QBLK * QB, N_SMALL), lambda i, j: (i, j, 0)),
                  pl.BlockSpec((1, G, n_cmp, CK), lambda i, j: (i, 0, 0, 0)),
                  pl.BlockSpec((1, G, dh, n_cmp), lambda i, j: (i, 0, 0, 0)),
                  kv_spec(2), kv_spec(4), kv_spec(3), kv_spec(5),
                  pl.BlockSpec((s, CK), lambda i, j: (0, 0)),
                  pl.BlockSpec((1, qw), lambda i, j: (0, 0)),
                  pl.BlockSpec((2, 1, LANES), lambda i, j: (0, 0, 0))],
        out_specs=pl.BlockSpec((1, NSA_QBLK * QB, qw), lambda i, j: (i, j, 0)),
        scratch_shapes=[pltpu.VMEM((NSA_QBLK, G, CK, HQ), BF16),
                        pltpu.VMEM((G, s, CK), BF16), pltpu.VMEM((G, s + WINDOW, CK), BF16),
                        pltpu.VMEM((G, nq, VR, QB), BF16), pltpu.VMEM((G, nq + WINDOW // QB, VR, QB), BF16),
                        pltpu.VMEM((NSA_QBLK, G, NSA_KS, HQ), F32)],
        compiler_params=_params("parallel", "arbitrary"),
        name="nsa_attention",
    )(y3, small3, kc_aug, vc_t, y3, y3, y3, y3, kconst, gq, gk)


def _merge_kernel(on_ref, os_ref, om_ref, g0_ref, g1_ref, g2_ref, wb_ref, wo_ref, x_ref, mod_ref, o_ref):
    merged = None
    for i, (o_r, g_r) in enumerate(((on_ref, g0_ref), (os_ref, g1_ref), (om_ref, g2_ref))):
        br = jnp.dot(o_r[0], wb_ref[i], preferred_element_type=F32)
        term = jax.nn.sigmoid(g_r[0].astype(F32)) * br
        merged = term if merged is None else merged + term
    out = jnp.dot(merged.astype(BF16), wo_ref[...], preferred_element_type=F32)
    o_ref[0] = x_ref[0] + mod_ref[0, 2:3, :] * out


def merge_project(o_nsa, o_sb, o_ml, y3, w_branch, w_out, x, mod, tm=512):
    b, s, d = x.shape
    W = MIX_WIDTH
    ospec = pl.BlockSpec((1, tm, W), lambda i, j: (i, j, 0))
    xspec = pl.BlockSpec((1, tm, d), lambda i, j: (i, j, 0))
    gspecs = [pl.BlockSpec((1, tm, d), functools.partial(lambda i, j, c: (i, j, c), c=C_MERGE // d + c))
              for c in range(N_BRANCH)]
    return pl.pallas_call(
        _merge_kernel,
        out_shape=jax.ShapeDtypeStruct((b, s, d), F32),
        grid=(b, s // tm),
        in_specs=[ospec, ospec, ospec] + gspecs + [
            pl.BlockSpec((N_BRANCH, W, d), lambda i, j: (0, 0, 0)),
            pl.BlockSpec((d, d), lambda i, j: (0, 0)),
            xspec,
            pl.BlockSpec((1, 6, d), lambda i, j: (i, 0, 0))],
        out_specs=xspec,
        compiler_params=_params("parallel", "parallel"),
        name="merge_project",
    )(o_nsa, o_sb, o_ml, y3, y3, y3, w_branch.astype(BF16), w_out.astype(BF16), x, mod)


def _ffn_kernel(x_ref, g_ref, mod_ref, wg_ref, wu_ref, wd_ref, o_ref, h_ref, acc_ref):
    f = pl.program_id(2)

    @pl.when(f == 0)
    def _():
        h_ref[...] = _norm_mod(x_ref[0], g_ref[...], mod_ref[0], 3, 4).astype(BF16)
        acc_ref[...] = jnp.zeros_like(acc_ref)

    h = h_ref[...]
    a = jnp.dot(h, wg_ref[...], preferred_element_type=F32)
    u = jnp.dot(h, wu_ref[...], preferred_element_type=F32)
    act = (a * jax.nn.sigmoid(a) * u).astype(BF16)
    acc_ref[...] += jnp.dot(act, wd_ref[...], preferred_element_type=F32)

    @pl.when(f == pl.num_programs(2) - 1)
    def _():
        o_ref[0] = x_ref[0] + mod_ref[0, 5:6, :] * acc_ref[...]


def dense_ffn(x, g, mod, wg, wu, wd, tm=512, n_ftiles=2):
    b, s, d = x.shape
    ff = wg.shape[1]
    tf = -(-ff // (n_ftiles * LANES)) * LANES
    pad = n_ftiles * tf - ff
    wg = jnp.pad(to_bf16(wg), ((0, 0), (0, pad)))
    wu = jnp.pad(to_bf16(wu), ((0, 0), (0, pad)))
    wd = jnp.pad(to_bf16(wd), ((0, pad), (0, 0)))
    xspec = pl.BlockSpec((1, tm, d), lambda i, j, f: (i, j, 0))
    return pl.pallas_call(
        _ffn_kernel,
        out_shape=jax.ShapeDtypeStruct((b, s, d), F32),
        grid=(b, s // tm, n_ftiles),
        in_specs=[xspec,
                  pl.BlockSpec((1, d), lambda i, j, f: (0, 0)),
                  pl.BlockSpec((1, 6, d), lambda i, j, f: (i, 0, 0)),
                  pl.BlockSpec((d, tf), lambda i, j, f: (0, f)),
                  pl.BlockSpec((d, tf), lambda i, j, f: (0, f)),
                  pl.BlockSpec((tf, d), lambda i, j, f: (f, 0))],
        out_specs=xspec,
        scratch_shapes=[pltpu.VMEM((tm, d), BF16), pltpu.VMEM((tm, d), F32)],
        compiler_params=_params("parallel", "parallel", "arbitrary"),
        name="dense_ffn",
    )(x, g.reshape(1, d), mod, wg, wu, wd)


def _router_kernel(x_ref, g_ref, mod_ref, wr_ref, h_ref, e_ref, p_ref):
    h = _norm_mod(x_ref[0], g_ref[...], mod_ref[0], 3, 4)
    h_ref[...] = h
    lane = _iota((1, LANES), 1)
    real = lane < N_EXPERTS
    logits = jnp.where(real, jnp.dot(h, wr_ref[...], precision=HIGHEST, preferred_element_type=F32), NEG)
    e = jnp.exp(logits - jnp.max(logits, axis=1, keepdims=True))
    p = jnp.where(real, e / jnp.sum(e, axis=1, keepdims=True), -1.0)
    p1 = jnp.max(p, axis=1, keepdims=True)
    i1 = jnp.min(jnp.where(p == p1, lane, LANES), axis=1, keepdims=True)
    rest = jnp.where(lane == i1, -1.0, p)
    p2 = jnp.max(rest, axis=1, keepdims=True)
    i2 = jnp.min(jnp.where(rest == p2, lane, LANES), axis=1, keepdims=True)
    tot = p1 + p2
    e_ref[...] = jnp.where(lane == 0, i1, jnp.where(lane == 1, i2, 0))[:, 0:N_EXPERTS]
    p_ref[...] = jnp.where(lane == 0, p1 / tot, jnp.where(lane == 1, p2 / tot, 0.0))[:, 0:N_EXPERTS]


def moe_router(x, g, mod, w_router, tm=512):
    b, s, d = x.shape
    t = b * s
    spb = s // tm
    wr = jnp.pad(w_router, ((0, 0), (0, LANES - N_EXPERTS)))
    return pl.pallas_call(
        _router_kernel,
        out_shape=(jax.ShapeDtypeStruct((t, d), F32),
                   jax.ShapeDtypeStruct((t, N_EXPERTS), I32),
                   jax.ShapeDtypeStruct((t, N_EXPERTS), F32)),
        grid=(b, spb),
        in_specs=[pl.BlockSpec((1, tm, d), lambda i, j: (i, j, 0)),
                  pl.BlockSpec((1, d), lambda i, j: (0, 0)),
                  pl.BlockSpec((1, 6, d), lambda i, j: (i, 0, 0)),
                  pl.BlockSpec((d, LANES), lambda i, j: (0, 0))],
        out_specs=(pl.BlockSpec((tm, d), lambda i, j: (i * spb + j, 0)),
                   pl.BlockSpec((tm, N_EXPERTS), lambda i, j: (i * spb + j, 0)),
                   pl.BlockSpec((tm, N_EXPERTS), lambda i, j: (i * spb + j, 0))),
        compiler_params=_params("parallel", "parallel"),
        name="moe_router",
    )(x, g.reshape(1, d), mod, wr)


def moe_ffn(x, g, mod, w_router, wg, wu, wd, tb=512):
    b, s, d = x.shape
    t = b * s
    a = t * TOP_K
    h, top_e, top_p = moe_router(x, g, mod, w_router)
    e_flat = top_e[:, 0:TOP_K].reshape(a)
    onehot = (e_flat[:, None] == jnp.arange(N_EXPERTS, dtype=I32)[None, :]).astype(I32)
    csum = jnp.cumsum(onehot, axis=0)
    rank = jnp.sum(onehot * csum, axis=1) - 1
    counts = csum[-1]
    padded = (counts + tb - 1) // tb * tb
    pad_ends = jnp.cumsum(padded)
    pad_starts = pad_ends - padded
    dest = (jnp.sum(onehot * pad_starts[None, :], axis=1) + rank).astype(I32)
    n_rows = (a // tb + N_EXPERTS + 1) * tb
    n_blk = n_rows // tb
    blk_expert = jnp.minimum(
        jnp.searchsorted(pad_ends, jnp.arange(n_blk, dtype=I32) * tb, side="right"), N_EXPERTS - 1).astype(I32)
    n_used = (pad_ends[-1:] // tb).astype(I32)
    pad_lo = jnp.concatenate([pad_starts + counts, pad_ends[-1:]]).astype(I32)
    pad_hi = jnp.concatenate([pad_ends, jnp.full((1,), n_rows, I32)]).astype(I32)
    slot_assign = moe_invert(dest, pad_lo, pad_hi, n_rows, tb)
    y2 = moe_experts(h, slot_assign, blk_expert, n_used, to_bf16(wg), to_bf16(wu), to_bf16(wd), tb)
    return moe_mix(y2, x, top_p, mod)


def _invert_kernel(dest_ref, lo_ref, hi_ref, sa_ref, *, n_assign, tb, n_chunks):
    phase = pl.program_id(0)
    chunk = pl.program_id(1)

    @pl.when((phase == 0) & (chunk < lo_ref.shape[0]))
    def _():
        def fill(p, c):
            sa_ref[p] = n_assign + (p & (2 * tb - 1))
            return c

        lax.fori_loop(lo_ref[chunk], hi_ref[chunk], fill, 0)

    @pl.when(phase == 1)
    def _():
        per = n_assign // n_chunks

        def put(j, c):
            a = chunk * per + j
            sa_ref[dest_ref[a]] = a
            return c

        lax.fori_loop(0, per, put, 0, unroll=8)


def moe_invert(dest, pad_lo, pad_hi, n_rows, tb, n_chunks=16):
    n_assign = dest.shape[0]
    assert tb & (tb - 1) == 0 and n_assign % n_chunks == 0 and pad_lo.shape[0] <= n_chunks
    smem = pl.BlockSpec(memory_space=pltpu.SMEM)
    return pl.pallas_call(
        functools.partial(_invert_kernel, n_assign=n_assign, tb=tb, n_chunks=n_chunks),
        out_shape=jax.ShapeDtypeStruct((n_rows,), I32),
        grid=(2, n_chunks),
        in_specs=[smem, smem, smem],
        out_specs=smem,
        compiler_params=pltpu.CompilerParams(dimension_semantics=("arbitrary", "arbitrary")),
        name="moe_invert",
    )(dest, pad_lo, pad_hi)


def _expert_kernel(be_ref, nu_ref, sa_ref, h_hbm, wg_ref, wu_ref, wd_ref, y_hbm,
                   xin_ref, yout_ref, xb_ref, acc_ref, sem_in, sem_out, *, tb, n_tok, n_f):
    i = pl.program_id(0)
    f = pl.program_id(1)
    n_used = nu_ref[0]
    n_assign = n_tok * TOP_K
    rows_f = tb // n_f
    active = i <= n_used

    def gather_row(blk, r):
        a = sa_ref[blk * tb + r]
        tok = jnp.where(a < n_assign, a >> 1, 0)
        return pltpu.make_async_copy(h_hbm.at[tok], xin_ref.at[blk % 2, r], sem_in.at[blk % 2])

    def scatter_row(blk, r):
        a = jnp.where(blk >= 0, sa_ref[jnp.maximum(blk, 0) * tb + r], n_assign + tb + r)
        row = jnp.where(a < n_assign, (a & 1) * n_tok + (a >> 1), a)
        return pltpu.make_async_copy(yout_ref.at[(blk + 2) % 2, r], y_hbm.at[row], sem_out.at[(blk + 2) % 2])

    def for_rows(fn):
        def body(r, c):
            fn(r)
            return c
        lax.fori_loop(0, tb, body, 0, unroll=8)

    def wait_rows(row_copy):
        for_rows(lambda r: row_copy.wait())

    @pl.when(f == 0)
    def _():
        @pl.when(i == 0)
        def _():
            yout_ref[1] = jnp.zeros((tb, yout_ref.shape[2]), F32)
            for_rows(lambda r: gather_row(0, r).start())

            def clear_row(r):
                return pltpu.make_async_copy(yout_ref.at[1, r], y_hbm.at[n_assign + r], sem_out.at[0])

            for_rows(lambda r: clear_row(r).start())
            wait_rows(clear_row(0))

        @pl.when((i == 0) | (i - 1 <= n_used))
        def _():
            wait_rows(gather_row(i, 0))

        @pl.when(active)
        def _():
            xb_ref[...] = xin_ref[i % 2].astype(BF16)
            acc_ref[...] = jnp.zeros_like(acc_ref)

    @pl.when(active)
    def _():
        for r in range(rows_f):
            gather_row(i + 1, f * rows_f + r).start()
            scatter_row(i - 1, f * rows_f + r).start()
        xb = xb_ref[...]
        a = jnp.dot(xb, wg_ref[0], preferred_element_type=F32)
        u = jnp.dot(xb, wu_ref[0], preferred_element_type=F32)
        act = (a * jax.nn.sigmoid(a) * u).astype(BF16)
        acc_ref[...] += jnp.dot(act, wd_ref[0], preferred_element_type=F32)

    @pl.when(f == n_f - 1)
    def _():
        @pl.when((i >= 1) & (i - 1 <= n_used))
        def _():
            wait_rows(scatter_row(i - 2, 0))

        @pl.when(active)
        def _():
            yout_ref[i % 2] = acc_ref[...]


def moe_experts(h, slot_assign, blk_expert, n_used, wg, wu, wd, tb, tf=1792):
    n_tok, d = h.shape
    p = slot_assign.shape[0]
    ff = wg.shape[2]
    return pl.pallas_call(
        functools.partial(_expert_kernel, tb=tb, n_tok=n_tok, n_f=ff // tf),
        out_shape=jax.ShapeDtypeStruct((n_tok * TOP_K + 2 * tb, d), F32),
        grid_spec=pltpu.PrefetchScalarGridSpec(
            num_scalar_prefetch=3,
            grid=(p // tb, ff // tf),
            in_specs=[pl.BlockSpec(memory_space=pl.ANY),
                      pl.BlockSpec((1, d, tf), lambda i, f, be, nu, sa: (be[i], 0, f)),
                      pl.BlockSpec((1, d, tf), lambda i, f, be, nu, sa: (be[i], 0, f)),
                      pl.BlockSpec((1, tf, d), lambda i, f, be, nu, sa: (be[i], f, 0))],
            out_specs=pl.BlockSpec(memory_space=pl.ANY),
            scratch_shapes=[pltpu.VMEM((2, tb, d), F32), pltpu.VMEM((2, tb, d), F32),
                            pltpu.VMEM((tb, d), BF16), pltpu.VMEM((tb, d), F32),
                            pltpu.SemaphoreType.DMA((2,)), pltpu.SemaphoreType.DMA((2,))]),
        compiler_params=pltpu.CompilerParams(dimension_semantics=("arbitrary", "arbitrary"),
                                             vmem_limit_bytes=VMEM_LIMIT, has_side_effects=True),
        name="moe_experts",
    )(blk_expert, n_used, slot_assign, h, wg, wu, wd)


def _mix_kernel(y0_ref, y1_ref, x_ref, p_ref, mod_ref, o_ref):
    w = p_ref[...]
    f = w[:, 0:1] * y0_ref[...] + w[:, 1:2] * y1_ref[...]
    o_ref[0] = x_ref[0] + mod_ref[0, 5:6, :] * f


def moe_mix(y2, x, top_p, mod, td=512):
    b, s, d = x.shape
    spb = s // td
    nt = b * spb
    return pl.pallas_call(
        _mix_kernel,
        out_shape=jax.ShapeDtypeStruct((b, s, d), F32),
        grid=(b, spb),
        in_specs=[pl.BlockSpec((td, d), lambda i, j: (i * spb + j, 0)),
                  pl.BlockSpec((td, d), lambda i, j: (nt + i * spb + j, 0)),
                  pl.BlockSpec((1, td, d), lambda i, j: (i, j, 0)),
                  pl.BlockSpec((td, N_EXPERTS), lambda i, j: (i * spb + j, 0)),
                  pl.BlockSpec((1, 6, d), lambda i, j: (i, 0, 0))],
        out_specs=pl.BlockSpec((1, td, d), lambda i, j: (i, j, 0)),
        compiler_params=_params("parallel", "parallel"),
        name="moe_mix",
    )(y2, y2, x, top_p, mod)


def _pack_w_in(w_in):
    kv = 2 * NSA_KV_HEADS * HEAD_DIM * 3
    w_in = to_bf16(w_in)
    o = 0
    nsa_q = w_in[:, o:o + MIX_WIDTH]; o += MIX_WIDTH
    nsa_kv = w_in[:, o:o + kv]; o += kv
    nsa_gate = w_in[:, o:o + NSA_HEADS * N_BRANCH]; o += NSA_HEADS * N_BRANCH
    sb = w_in[:, o:o + 3 * MIX_WIDTH]; o += 3 * MIX_WIDTH
    ml_qkv = w_in[:, o:o + 3 * MIX_WIDTH]; o += 3 * MIX_WIDTH
    ml_if = w_in[:, o:o + 2 * ML_HEADS]; o += 2 * ML_HEADS
    ml_o = w_in[:, o:o + MIX_WIDTH]; o += MIX_WIDTH
    merge = w_in[:, o:]
    main = jnp.concatenate([merge, nsa_q, ml_qkv, ml_o, sb, nsa_kv], axis=1)
    small = jnp.concatenate([nsa_gate, ml_if], axis=1)
    small = jnp.pad(small, ((0, 0), (0, N_SMALL - small.shape[1])))
    return main, small


def token_mixer_layer(x, mod, norm_g, w_in, nsa_q_norm, nsa_k_norm, cmp_pos, cmp_w1, cmp_b1, cmp_w2,
                      cmp_b2, ml_conv_w, ml_conv_b, ml_gate_b, w_branch, w_out):
    w_main, w_small = _pack_w_in(w_in)
    y3, small3 = in_projection(x, norm_g, mod, w_main, w_small)
    o_sb = sb_attention(y3)
    o_ml = mlstm(y3, small3, ml_conv_w, ml_conv_b, ml_gate_b)
    kc, vc = nsa_compress(y3, cmp_pos, cmp_w1, cmp_b1, cmp_w2, cmp_b2, nsa_k_norm[0])
    o_nsa = nsa_attention(y3, small3, nsa_q_norm, nsa_k_norm, kc, vc)
    return merge_project(o_nsa, o_sb, o_ml, y3, w_branch, w_out, x, mod)


def kernel(x, c, ada_w, ada_b, norm_mix, norm_ffn, w_in, nsa_q_norm, nsa_k_norm, cmp_pos, cmp_w1, cmp_b1,
           cmp_w2, cmp_b2, ml_conv_w, ml_conv_b, ml_gate_b, w_branch, w_out, ffn_wg, ffn_wu, ffn_wd,
           moe_router, moe_wg, moe_wu, moe_wd):
    depth = ada_w.shape[0]
    b, s, d = x.shape
    mods = adaln(c, ada_w, ada_b).reshape(depth, b, 6, d)
    for layer in range(depth):
        mod = mods[layer]
        x = token_mixer_layer(x, mod, norm_mix[layer], w_in[layer], nsa_q_norm[layer], nsa_k_norm[layer],
                              cmp_pos[layer], cmp_w1[layer], cmp_b1[layer], cmp_w2[layer], cmp_b2[layer],
                              ml_conv_w[layer], ml_conv_b[layer], ml_gate_b[layer], w_branch[layer],
                              w_out[layer])
        j = layer // 2
        if layer % 2 == 0:
            x = dense_ffn(x, norm_ffn[layer], mod, ffn_wg[j], ffn_wu[j], ffn_wd[j])
        else:
            x = moe_ffn(x, norm_ffn[layer], mod, moe_router[j], moe_wg[j], moe_wu[j], moe_wd[j])
    return x
```

```python
import functools

import numpy as np
import jax
import jax.numpy as jnp
from jax import lax
from jax.experimental import pallas as pl
from jax.experimental.pallas import tpu as pltpu

F32 = jnp.float32
BF16 = jnp.bfloat16
I32 = jnp.int32
HIGHEST = lax.Precision.HIGHEST

EPS = 1e-6
NEG = -1e30
HEAD_DIM = 64
MIX_WIDTH = 512
NSA_HEADS = 8
NSA_KV_HEADS = 2
NSA_GROUP = NSA_HEADS // NSA_KV_HEADS
CMP_BLOCK = 32
CMP_STRIDE = 16
SEL_BLOCK = 64
SEL_TOPK = 16
WINDOW = 512
FORCE_BONUS = 1e4
ML_HEADS = 4
ML_HEAD_DIM = 128
ML_CHUNK = 64
ML_UNROLL = 2
CONV_WIDTH = 4
N_BRANCH = 3
N_EXPERTS = 8
TOP_K = 2
LANES = 128

C_MERGE = 0
C_NSA_Q = 3072
C_ML_Q = 3584
C_ML_K = 4096
C_ML_V = 4608
C_ML_O = 5120
C_SB_Q = 5632
C_SB_K = 6144
C_SB_V = 6656
C_NSA_KV = 7168
N_MAIN = 7936
S_NSA_GATE = 0
S_ML_I = 24
S_ML_F = 28
N_SMALL = 128

VMEM_LIMIT = 56 * 1024 * 1024


def _params(*sem):
    return pltpu.CompilerParams(dimension_semantics=sem, vmem_limit_bytes=VMEM_LIMIT)


def _iota(shape, dim):
    return lax.broadcasted_iota(I32, shape, dim)


def _split_dot(a32, b_bf16):
    hi = a32.astype(BF16)
    lo = (a32 - hi.astype(F32)).astype(BF16)
    return (jnp.dot(hi, b_bf16, preferred_element_type=F32)
            + jnp.dot(lo, b_bf16, preferred_element_type=F32))


def _split_dot_left(a_bf16, b32):
    hi = b32.astype(BF16)
    lo = (b32 - hi.astype(F32)).astype(BF16)
    return (jnp.dot(a_bf16, hi, preferred_element_type=F32)
            + jnp.dot(a_bf16, lo, preferred_element_type=F32))


def _dot_nt(a, b):
    return lax.dot_general(a, b, (((1,), (1,)), ((), ())), preferred_element_type=F32)


def _log_sigmoid(z):
    return jnp.minimum(z, 0.0) - jnp.log1p(jnp.exp(-jnp.abs(z)))


def _cast_kernel(x_ref, o_ref):
    o_ref[...] = x_ref[...].astype(o_ref.dtype)


def to_bf16(w, max_rows=512):
    cols = w.shape[-1]
    w2 = w.reshape(-1, cols)
    rows = w2.shape[0]
    tr = max(t for t in range(8, max_rows + 1, 8) if rows % t == 0)
    out = pl.pallas_call(
        _cast_kernel,
        out_shape=jax.ShapeDtypeStruct((rows, cols), BF16),
        grid=(rows // tr,),
        in_specs=[pl.BlockSpec((tr, cols), lambda i: (i, 0))],
        out_specs=pl.BlockSpec((tr, cols), lambda i: (i, 0)),
        compiler_params=_params("parallel"),
        name="to_bf16",
    )(w2)
    return out.reshape(w.shape)


def _adaln_kernel(c_ref, w_ref, b_ref, o_ref):
    c = c_ref[...]
    cond = c * jax.nn.sigmoid(c)
    o_ref[0] = jnp.dot(cond, w_ref[0], precision=HIGHEST, preferred_element_type=F32) + b_ref[0]


def adaln(c, ada_w, ada_b):
    depth, d, n = ada_w.shape
    b = c.shape[0]
    tn = 1536
    return pl.pallas_call(
        _adaln_kernel,
        out_shape=jax.ShapeDtypeStruct((depth, b, n), F32),
        grid=(depth, n // tn),
        in_specs=[pl.BlockSpec((b, d), lambda l, j: (0, 0)),
                  pl.BlockSpec((1, d, tn), lambda l, j: (l, 0, j)),
                  pl.BlockSpec((1, 1, tn), lambda l, j: (l, 0, j))],
        out_specs=pl.BlockSpec((1, b, tn), lambda l, j: (l, 0, j)),
        compiler_params=_params("parallel", "parallel"),
        name="adaln",
    )(c, ada_w, ada_b.reshape(depth, 1, n))


def _norm_mod(x, g, mod, shift_row, scale_row):
    ms = jnp.mean(x * x, axis=-1, keepdims=True)
    y = x * lax.rsqrt(ms + EPS) * g
    return y * (1.0 + mod[scale_row:scale_row + 1, :]) + mod[shift_row:shift_row + 1, :]


def _in_proj_kernel(x_ref, g_ref, mod_ref, wm_ref, ws_ref, y_ref, sm_ref, h_ref):
    @pl.when(pl.program_id(2) == 0)
    def _():
        h_ref[...] = _norm_mod(x_ref[0], g_ref[...], mod_ref[0], 0, 1).astype(BF16)
        sm_ref[0] = jnp.dot(h_ref[...], ws_ref[...], preferred_element_type=F32)

    y_ref[0] = jnp.dot(h_ref[...], wm_ref[...], preferred_element_type=F32).astype(y_ref.dtype)


def in_projection(x, g, mod, w_main, w_small, tm=512, n_tiles=2):
    b, s, d = x.shape
    tn = N_MAIN // n_tiles
    return pl.pallas_call(
        _in_proj_kernel,
        out_shape=(jax.ShapeDtypeStruct((b, s, N_MAIN), BF16), jax.ShapeDtypeStruct((b, s, N_SMALL), F32)),
        grid=(b, s // tm, n_tiles),
        in_specs=[pl.BlockSpec((1, tm, d), lambda i, j, n: (i, j, 0)),
                  pl.BlockSpec((1, d), lambda i, j, n: (0, 0)),
                  pl.BlockSpec((1, 6, d), lambda i, j, n: (i, 0, 0)),
                  pl.BlockSpec((d, tn), lambda i, j, n: (0, n)),
                  pl.BlockSpec((d, N_SMALL), lambda i, j, n: (0, 0))],
        out_specs=(pl.BlockSpec((1, tm, tn), lambda i, j, n: (i, j, n)),
                   pl.BlockSpec((1, tm, N_SMALL), lambda i, j, n: (i, j, 0))),
        scratch_shapes=[pltpu.VMEM((tm, d), BF16)],
        compiler_params=_params("parallel", "parallel", "arbitrary"),
        name="in_projection",
    )(x, g.reshape(1, d), mod, w_main, w_small)


SB_EXP_DROP = 104.0
SB_EAGER = 2


def _sb_kernel(q_ref, k_ref, v_ref, o_ref, vt_ref, *, tq, n_str):
    TK, dh = LANES, HEAD_DIM
    n_sub = tq // TK
    W = 2 * tq
    SR = range(n_str)
    qi = pl.program_id(2)

    @pl.when(qi == 0)
    def _():
        for c in range(v_ref.shape[1] // tq):
            v_t = v_ref[0, c * tq:(c + 1) * tq, :].astype(F32).T.astype(BF16)
            for p in SR:
                vt_ref[p, c] = v_t[p * LANES:(p + 1) * LANES]

    q_t = (q_ref[0].astype(F32) * (dh ** -0.5)).T
    chan = _iota((2 * dh, tq), 0)
    q_cat = []
    for p in SR:
        q_p = q_t[p * LANES:(p + 1) * LANES]
        q_cat.append(jnp.concatenate([jnp.where(chan < dh, q_p, 0.0), jnp.where(chan < dh, 0.0, q_p)],
                                     axis=1).astype(BF16))
    later = (_iota((TK, TK), 0) < _iota((TK, TK), 1)).astype(BF16)
    suffix = jnp.concatenate([jnp.concatenate([later, later], axis=1), jnp.ones((8, 2 * TK), BF16)], axis=0)

    def steps(blocks, st):
        work = [(p, b) for p in SR for b in range(len(blocks))]
        lss, his, los = {}, {}, {}
        for p, b in work:
            j, keep = blocks[b]
            k0 = pl.multiple_of(j * tq, tq)
            z = jnp.dot(k_ref[0, pl.ds(k0, tq), p * LANES:(p + 1) * LANES], q_cat[p],
                        preferred_element_type=F32)
            drop = jnp.maximum(z, 0.0) + jnp.log(1.0 + jnp.exp(-jnp.abs(z)))
            lss[p, b] = z - drop
            if keep is not None:
                drop = jnp.where(keep, drop, 0.0)
            his[p, b] = drop.astype(BF16)
            los[p, b] = (drop - his[p, b].astype(F32)).astype(BF16)
        carry = [st[p][0] for p in SR]
        afters = {}
        for p, b in work:
            after = [None] * n_sub
            for sub in range(n_sub - 1, -1, -1):
                rows = slice(sub * TK, (sub + 1) * TK)
                res = jnp.dot(suffix, jnp.concatenate([his[p, b][rows], los[p, b][rows]], axis=0),
                              preferred_element_type=F32)
                after[sub] = res[0:TK] + carry[p]
                carry[p] = carry[p] + res[TK:TK + 1]
            afters[p, b] = jnp.concatenate(after, axis=0)
        acc = [st[p][1] for p in SR]
        for p, b in work:
            j, keep = blocks[b]
            a = jnp.exp(lss[p, b] - afters[p, b])
            if keep is not None:
                a = jnp.where(keep, a, 0.0)
            acc[p] = acc[p] + jnp.dot(vt_ref[p, j], a.astype(BF16), preferred_element_type=F32)
        return [(carry[p], acc[p]) for p in SR]

    def cond(c):
        least = jnp.min(c[1][0][0])
        for p in range(1, n_str):
            least = jnp.minimum(least, jnp.min(c[1][p][0]))
        return (c[0] >= 0) & (least < SB_EXP_DROP)

    def body(c):
        return c[0] - 1, steps([(c[0], None)], c[1])

    strict = _iota((tq, W), 0) < (_iota((tq, W), 1) & (tq - 1))
    eager = [(jnp.maximum(qi - d, 0), qi >= d) for d in range(1, SB_EAGER + 1)]
    st = steps([(qi, strict)] + eager, [(jnp.zeros((1, W), F32), jnp.zeros((2 * dh, W), F32)) for p in SR])
    _, st = lax.while_loop(cond, body, (qi - 1 - SB_EAGER, st))
    out_rows = []
    for p in SR:
        out_rows += [st[p][1][0:dh, 0:tq], st[p][1][dh:2 * dh, tq:W]]
    o_ref[0] = jnp.concatenate(out_rows, axis=0).T.astype(o_ref.dtype)


def sb_attention(y3, tq=256, n_str=2):
    b, s, _ = y3.shape
    wide = n_str * LANES
    qb, kb, vb = C_SB_Q // wide, C_SB_K // wide, C_SB_V // wide
    return pl.pallas_call(
        functools.partial(_sb_kernel, tq=tq, n_str=n_str),
        out_shape=jax.ShapeDtypeStruct((b, s, MIX_WIDTH), BF16),
        grid=(b, MIX_WIDTH // wide, s // tq),
        in_specs=[pl.BlockSpec((1, tq, wide), lambda i, p, j: (i, j, qb + p)),
                  pl.BlockSpec((1, s, wide), lambda i, p, j: (i, 0, kb + p)),
                  pl.BlockSpec((1, s, wide), lambda i, p, j: (i, 0, vb + p))],
        out_specs=pl.BlockSpec((1, tq, wide), lambda i, p, j: (i, j, p)),
        scratch_shapes=[pltpu.VMEM((n_str, s // tq, LANES, tq), BF16)],
        compiler_params=_params("parallel", "parallel", "arbitrary"),
        name="sb_attention",
    )(y3, y3, y3)


def _mlstm_kernel(q_ref, k_ref, v_ref, og_ref, sm_ref, gr_ref, cw_ref, cb_ref, gb_ref, out_ref,
                  ct_ref, n_ref, m_ref, xbuf_ref, qk_ref, *, ts):
    L, dh, H, W = ML_CHUNK, ML_HEAD_DIM, ML_HEADS, MIX_WIDTH
    halo = 8
    sblk = pl.program_id(1)

    @pl.when(sblk == 0)
    def _():
        ct_ref[...] = jnp.zeros_like(ct_ref)
        n_ref[...] = jnp.zeros_like(n_ref)
        m_ref[...] = jnp.zeros_like(m_ref)
        xbuf_ref[0:halo, :] = jnp.zeros((halo, 2 * W), F32)

    @pl.when(sblk > 0)
    def _():
        xbuf_ref[0:halo, :] = xbuf_ref[ts:ts + halo, :]

    xbuf_ref[halo:halo + ts, 0:W] = q_ref[0].astype(F32)
    xbuf_ref[halo:halo + ts, W:2 * W] = k_ref[0].astype(F32)
    conv = cb_ref[...]
    for j in range(CONV_WIDTH):
        off = halo - (CONV_WIDTH - 1) + j
        conv = conv + cw_ref[j:j + 1, :] * xbuf_ref[off:off + ts, :]
    act = conv * jax.nn.sigmoid(conv)
    qk_ref[:, 0:W] = (act[:, 0:W] * (dh ** -0.5)).astype(BF16)
    qk_ref[:, W:2 * W] = act[:, W:2 * W].astype(BF16)

    it0, it1 = _iota((L, L), 0), _iota((L, L), 1)
    causal = it0 >= it1
    tri_lo = causal.astype(BF16)
    tri_up = (it0 <= it1).astype(BF16)

    def chunks(it, carry):
        HR = range(H)
        UR = range(ML_UNROLL)
        UH = [(u, h) for u in UR for h in HR]
        cols = [slice(h * dh, (h + 1) * dh) for h in HR]
        rows = [pl.ds(pl.multiple_of((it * ML_UNROLL + u) * L, L), L) for u in UR]
        sm = [sm_ref[0, rows[u], :] for u in UR]
        gr = [gr_ref[0, it * ML_UNROLL + u] for u in UR]
        ig_col = {(u, h): sm[u][:, S_ML_I + h:S_ML_I + h + 1] + gb_ref[0, h] for u, h in UH}
        lf_col = {(u, h): _log_sigmoid(sm[u][:, S_ML_F + h:S_ML_F + h + 1] + gb_ref[1, h]) for u, h in UH}
        ig_row = {(u, h): gr[u][h:h + 1, :] + gb_ref[0, h] for u, h in UH}
        lf_row = {(u, h): _log_sigmoid(gr[u][H + h:H + h + 1, :] + gb_ref[1, h]) for u, h in UH}
        b_t = {k: _split_dot_left(tri_lo, jnp.broadcast_to(lf_col[k], (L, L))) for k in UH}
        b_s = {k: _split_dot(jnp.broadcast_to(lf_row[k], (L, L)), tri_up) for k in UH}
        qq = {(u, h): qk_ref[rows[u], cols[h]] for u, h in UH}
        kk = {(u, h): qk_ref[rows[u], W + h * dh:W + (h + 1) * dh] for u, h in UH}
        vv = {(u, h): v_ref[0, rows[u], cols[h]] for u, h in UH}
        qk = {k: _dot_nt(qq[k], kk[k]) for k in UH}
        kt = {k: kk[k].astype(F32).T.astype(BF16) for k in UH}
        dmat = {k: jnp.where(causal, b_t[k] - b_s[k] + ig_row[k], NEG) for k in UH}
        d_max = {k: jnp.max(dmat[k], axis=1, keepdims=True) for k in UH}
        b_col = {k: b_t[k][:, 0:1] for k in UH}
        b_last = {k: b_t[k][L - 1:L, 0:1] for k in UH}
        decay = {k: b_last[k] - b_col[k] + ig_col[k] for k in UH}
        decay_max = {k: jnp.max(decay[k], axis=0, keepdims=True) for k in UH}
        m_prev, m_new = {}, {}
        for u, h in UH:
            m_prev[u, h] = m_ref[h][:, 0:1] if u == 0 else m_new[u - 1, h]
            m_new[u, h] = jnp.maximum(b_last[u, h] + m_prev[u, h], decay_max[u, h])
        m_inter = {k: b_col[k] + m_prev[k] for k in UH}
        m_t = {k: jnp.maximum(m_inter[k], d_max[k]) for k in UH}
        w = {k: jnp.exp(dmat[k] - m_t[k]) * qk[k] for k in UH}
        inter = {k: jnp.exp(m_inter[k] - m_t[k]) for k in UH}
        w_v = {k: jnp.dot(w[k].astype(BF16), vv[k], preferred_element_type=F32) for k in UH}
        ws = {k: jnp.exp(decay[k] - m_new[k]) for k in UH}
        cscale = {k: jnp.exp(b_last[k] + m_prev[k] - m_new[k]) for k in UH}
        wv = {k: (ws[k] * vv[k].astype(F32)).astype(BF16) for k in UH}
        k_wv = {k: jnp.dot(kt[k], wv[k], preferred_element_type=F32) for k in UH}
        k_ws = {k: jnp.sum(ws[k] * kk[k].astype(F32), axis=0, keepdims=True) for k in UH}
        w_sum = {k: jnp.sum(w[k], axis=1, keepdims=True) for k in UH}
        ct = {h: ct_ref[h] for h in HR}
        nvec = {h: n_ref[h] for h in HR}
        for u, h in UH:
            k = (u, h)
            num = inter[k] * jnp.dot(qq[k], ct[h].astype(BF16), preferred_element_type=F32) + w_v[k]
            den = inter[k] * jnp.sum(qq[k].astype(F32) * nvec[h], axis=1, keepdims=True) + w_sum[k]
            hval = num / jnp.maximum(jnp.abs(den), jnp.exp(-m_t[k]))
            ct[h] = cscale[k] * ct[h] + k_wv[k]
            nvec[h] = cscale[k] * nvec[h] + k_ws[k]
            gate = jax.nn.sigmoid(og_ref[0, rows[u], cols[h]].astype(F32))
            out_ref[0, rows[u], cols[h]] = (gate * hval).astype(out_ref.dtype)
        for h in HR:
            ct_ref[h] = ct[h]
            n_ref[h] = nvec[h]
            m_ref[h] = jnp.broadcast_to(m_new[ML_UNROLL - 1, h], (1, LANES))
        return carry

    lax.fori_loop(0, ts // (L * ML_UNROLL), chunks, 0)


def mlstm(y3, small3, conv_w, conv_b, gate_b, ts=512):
    b, s, _ = y3.shape
    W, H, L = MIX_WIDTH, ML_HEADS, ML_CHUNK
    gr = small3[:, :, S_ML_I:S_ML_I + 2 * H].reshape(b, s // L, L, 2 * H).transpose(0, 1, 3, 2)
    cq, ck, cv, co = C_ML_Q // W, C_ML_K // W, C_ML_V // W, C_ML_O // W
    return pl.pallas_call(
        functools.partial(_mlstm_kernel, ts=ts),
        out_shape=jax.ShapeDtypeStruct((b, s, W), BF16),
        grid=(b, s // ts),
        in_specs=[pl.BlockSpec((1, ts, W), lambda i, j: (i, j, cq)),
                  pl.BlockSpec((1, ts, W), lambda i, j: (i, j, ck)),
                  pl.BlockSpec((1, ts, W), lambda i, j: (i, j, cv)),
                  pl.BlockSpec((1, ts, W), lambda i, j: (i, j, co)),
                  pl.BlockSpec((1, ts, N_SMALL), lambda i, j: (i, j, 0)),
                  pl.BlockSpec((1, ts // L, 2 * H, L), lambda i, j: (i, j, 0, 0)),
                  pl.BlockSpec((CONV_WIDTH, 2 * W), lambda i, j: (0, 0)),
                  pl.BlockSpec((1, 2 * W), lambda i, j: (0, 0)),
                  pl.BlockSpec(memory_space=pltpu.SMEM)],
        out_specs=pl.BlockSpec((1, ts, W), lambda i, j: (i, j, 0)),
        scratch_shapes=[pltpu.VMEM((H, ML_HEAD_DIM, ML_HEAD_DIM), F32),
                        pltpu.VMEM((H, 1, ML_HEAD_DIM), F32),
                        pltpu.VMEM((H, 1, LANES), F32),
                        pltpu.VMEM((ts + 8, 2 * W), F32),
                        pltpu.VMEM((ts, 2 * W), BF16)],
        compiler_params=_params("parallel", "arbitrary"),
        name="mlstm",
    )(y3, y3, y3, y3, small3, gr, conv_w, conv_b.reshape(1, 2 * W), gate_b)


def _gelu_tanh(x):
    return 0.5 * x * (1.0 + jnp.tanh(0.7978845608028654 * (x + 0.044715 * (x * x * x))))


def _compress_kernel(ra_ref, rb_ref, pos_ref, w1_ref, b1_ref, w2_ref, b2_ref, kn_ref, kc_ref, vc_ref):
    half = (CMP_BLOCK // 2) * HEAD_DIM
    for j, o_ref in enumerate((kc_ref, vc_ref)):
        xa = (ra_ref[j, 0, 0].astype(F32) + pos_ref[j, :, 0:half]).astype(BF16)
        xb = (rb_ref[j, 0, 0].astype(F32) + pos_ref[j, :, half:2 * half]).astype(BF16)
        hid = (jnp.dot(xa, w1_ref[j, 0:half, :], preferred_element_type=F32)
               + jnp.dot(xb, w1_ref[j, half:2 * half, :], preferred_element_type=F32) + b1_ref[j])
        out = jnp.dot(_gelu_tanh(hid).astype(BF16), w2_ref[j], preferred_element_type=F32) + b2_ref[j]
        if j == 0:
            out = out * lax.rsqrt(jnp.mean(out * out, axis=-1, keepdims=True) + EPS) * kn_ref[...]
        o_ref[0, 0] = out


def nsa_compress(y3, cmp_pos, cmp_w1, cmp_b1, cmp_w2, cmp_b2, k_norm0):
    b, s, _ = y3.shape
    G, dh = NSA_KV_HEADS, HEAD_DIM
    nr = s // CMP_STRIDE
    wide = CMP_STRIDE * dh
    kv = y3[:, :, C_NSA_KV:C_NSA_KV + 2 * G * dh].reshape(b, s, 2, G, dh)
    ra = kv.transpose(2, 0, 3, 1, 4).reshape(2, b, G, nr, wide)
    rb = jnp.concatenate([ra[:, :, :, 1:], jnp.zeros((2, b, G, 1, wide), ra.dtype)], axis=3)
    hidden = cmp_w1.shape[-1]
    out = jax.ShapeDtypeStruct((b, G, nr, dh), F32)
    blk = pl.BlockSpec((2, 1, 1, nr, wide), lambda i, g: (0, i, g, 0, 0))
    oblk = pl.BlockSpec((1, 1, nr, dh), lambda i, g: (i, g, 0, 0))

    def full(shape):
        return pl.BlockSpec(shape, lambda i, g: (0,) * len(shape))

    return pl.pallas_call(
        _compress_kernel,
        out_shape=(out, out),
        grid=(b, G),
        in_specs=[blk, blk, full((2, 1, 2 * wide)), full((2, 2 * wide, hidden)), full((2, 1, hidden)),
                  full((2, hidden, dh)), full((2, 1, dh)), full((1, dh))],
        out_specs=(oblk, oblk),
        compiler_params=_params("parallel", "parallel"),
        name="nsa_compress",
    )(ra, rb, cmp_pos.reshape(2, 1, 2 * wide), cmp_w1.astype(BF16), cmp_b1.reshape(2, 1, hidden),
      cmp_w2.astype(BF16), cmp_b2.reshape(2, 1, dh), k_norm0.reshape(1, dh))


NSA_QB = 128
NSA_QBLK = 2
NSA_KS = 512
NSA_CK = 256
NSA_VR = 80
A_FEAT, A_PEN, A_BIAS = 0, 64, 128
A_DUMMY = A_BIAS + 4


def _head_rms(x, gain):
    w = x.shape[1]
    same_head = (_iota((w, w), 0) // HEAD_DIM == _iota((w, w), 1) // HEAD_DIM).astype(BF16)
    ss = _split_dot(x * x, same_head)
    return x * lax.rsqrt(ss * (1.0 / HEAD_DIM) + EPS) * gain


def _nsa_kernel(q_ref, gl_ref, kca_ref, vcT_ref, ks_ref, kw_ref, vs_ref, vw_ref, kconst_ref, gq_ref, gk_ref,
                     o_ref, qt_ref, ksa_ref, kwa_ref, vsa_ref, vwa_ref, s_ref, *, n_sel, top):
    QB, R, dh, CK, VR, KS = NSA_QB, NSA_GROUP, HEAD_DIM, NSA_CK, NSA_VR, NSA_KS
    G = NSA_KV_HEADS
    GR = range(G)
    UR = range(NSA_QBLK)
    UG = [(u, g) for u in UR for g in GR]
    HQ = R * QB
    NPAD = WINDOW // QB
    SUB = KS // QB
    step = pl.program_id(1)
    qi = [step * NSA_QBLK + u for u in UR]
    q0 = [qi[u] * QB for u in UR]
    nkb = vsa_ref.shape[1]

    @pl.when(step == 0)
    def _():
        pad_keys = jnp.where(_iota((WINDOW, CK), 1) == A_DUMMY, 1.0, 0.0).astype(BF16)
        ones_rows = jnp.where(_iota((nkb + NPAD, VR - dh, QB), 1) == 0, 1.0, 0.0).astype(BF16)
        for g in GR:
            heads = slice(g * dh, (g + 1) * dh)
            ksa_ref[g] = kconst_ref[...]
            kwa_ref[g, 0:WINDOW, :] = pad_keys
            kwa_ref[g, WINDOW:, :] = kconst_ref[...]
            kwa_ref[g, WINDOW:, A_PEN:A_PEN + 64] = jnp.zeros((kwa_ref.shape[1] - WINDOW, 64), BF16)
            vwa_ref[g, 0:NPAD, 0:dh, :] = jnp.zeros((NPAD, dh, QB), BF16)
            vsa_ref[g, :, dh:VR, :] = ones_rows[0:nkb]
            vwa_ref[g, :, dh:VR, :] = ones_rows
        for c in range(ks_ref.shape[1] // KS):
            keys = slice(c * KS, (c + 1) * KS)
            ks_n = _head_rms(ks_ref[0, keys, :].astype(F32), gk_ref[0]).astype(BF16)
            kw_n = _head_rms(kw_ref[0, keys, :].astype(F32), gk_ref[1]).astype(BF16)
            for g in GR:
                heads = slice(g * dh, (g + 1) * dh)
                ksa_ref[g, keys, A_FEAT:A_FEAT + dh] = ks_n[:, heads]
                kwa_ref[g, WINDOW + c * KS:WINDOW + (c + 1) * KS, A_FEAT:A_FEAT + dh] = kw_n[:, heads]
        for c in range(nkb):
            keys = slice(c * QB, (c + 1) * QB)
            vs_t = vs_ref[0, keys, :].astype(F32).T.astype(BF16)
            vw_t = vw_ref[0, keys, :].astype(F32).T.astype(BF16)
            for g in GR:
                heads = slice(g * dh, (g + 1) * dh)
                vsa_ref[g, c, 0:dh, :] = vs_t[heads]
                vwa_ref[g, NPAD + c, 0:dh, :] = vw_t[heads]
        qt_ref[:, :, A_BIAS + 16:CK, :] = jnp.zeros((NSA_QBLK, G, CK - A_BIAS - 16, HQ), BF16)

    q_n = _head_rms(q_ref[0].astype(F32), gq_ref[...]).astype(BF16)
    q_rows = (q_n.astype(F32) * (dh ** -0.5)).T
    lane = _iota((16, HQ), 1)
    rowi = _iota((16, HQ), 0)
    for u, g in UG:
        t_q = q0[u] + (lane & (QB - 1))
        t_hi = ((t_q >> 6) << 6).astype(F32)
        t_lo = (t_q & 63).astype(F32)
        qT = jnp.concatenate([q_rows[(g * R + r) * dh:(g * R + r + 1) * dh, u * QB:(u + 1) * QB]
                              for r in range(R)], axis=1).astype(BF16)
        qt_ref[u, g, A_FEAT:A_FEAT + dh, :] = qT
        qt_ref[u, g, A_PEN:A_PEN + dh, :] = qT
        slope = jnp.exp2(-(g * R + (lane >> 7) + 1).astype(F32))
        bias_rows = jnp.where(rowi < 2, slope,
                              jnp.where(rowi == 2, -slope * t_hi,
                                        jnp.where(rowi == 3, -slope * t_lo,
                                                  jnp.where(rowi == A_DUMMY - A_BIAS, NEG, 0.0))))
        qt_ref[u, g, A_BIAS:A_BIAS + 16, :] = bias_rows.astype(BF16)
    k_loc = _iota((QB, HQ), 0)
    q_loc = _iota((QB, HQ), 1) & (QB - 1)

    def pv(v_ref_, g, kb0, pr):
        out = None
        for i in range(pr.shape[0] // QB):
            term = jnp.dot(v_ref_[g, kb0 + i], pr[i * QB:(i + 1) * QB], preferred_element_type=F32)
            out = term if out is None else out + term
        return out

    n_cmp = kca_ref.shape[2]
    cmp_end = _iota((n_cmp, HQ), 0) * CMP_STRIDE + (CMP_BLOCK - 1)
    valid = [cmp_end <= q0[u] + (_iota((n_cmp, HQ), 1) & (QB - 1)) for u in UR]
    sc = {(u, g): jnp.where(valid[u], jnp.dot(kca_ref[0, g], qt_ref[u, g], preferred_element_type=F32), NEG)
          for u, g in UG}
    mx = {k: jnp.max(sc[k], axis=0, keepdims=True) for k in UG}
    e = {k: jnp.exp(sc[k] - mx[k]) for k in UG}
    inv = {k: jnp.where(mx[k] > 0.5 * NEG, 1.0 / jnp.sum(e[k], axis=0, keepdims=True), 0.0) for k in UG}
    p = {k: e[k] * inv[k] for k in UG}
    o_cmp = {(u, g): jnp.dot(vcT_ref[0, g], p[u, g].astype(BF16), preferred_element_type=F32)
             for u, g in UG}

    sw = {(u, g): jnp.dot(kwa_ref[g, pl.ds(pl.multiple_of(q0[u], QB), WINDOW + QB), :], qt_ref[u, g],
                          preferred_element_type=F32) for u, g in UG}
    sw = {k: jnp.concatenate([jnp.where(k_loc > q_loc, sw[k][0:QB], NEG), sw[k][QB:WINDOW],
                              jnp.where(k_loc <= q_loc, sw[k][WINDOW:WINDOW + QB], NEG)], axis=0) for k in UG}
    pw = {k: jnp.exp(sw[k] - jnp.max(sw[k], axis=0, keepdims=True)).astype(BF16) for k in UG}
    acc_w = {(u, g): pv(vwa_ref, g, qi[u], pw[u, g]) for u, g in UG}
    o_win = {k: acc_w[k][0:dh] / acc_w[k][dh:dh + 1] for k in UG}

    c0 = _iota((n_sel, n_cmp), 1) * CMP_STRIDE
    s0 = _iota((n_sel, n_cmp), 0) * SEL_BLOCK
    overlap_t = ((c0 < s0 + SEL_BLOCK) & (c0 + CMP_BLOCK > s0)).astype(BF16)
    j_idx = _iota((n_sel, QB), 0)
    tq = [q0[u] + _iota((n_sel, QB), 1) for u in UR]
    forced = [(j_idx == 0) | (j_idx == (tq[u] >> 6)) | (j_idx == (tq[u] >> 6) - 1) for u in UR]
    causal_blk = [j_idx * SEL_BLOCK <= tq[u] for u in UR]
    p_grp = {k: p[k][:, 0:QB] + p[k][:, QB:2 * QB] + p[k][:, 2 * QB:3 * QB] + p[k][:, 3 * QB:4 * QB] for k in UG}
    imp = {(u, g): jnp.where(causal_blk[u], _split_dot_left(overlap_t, p_grp[u, g])
                             + jnp.where(forced[u], FORCE_BONUS, 0.0), -1.0) for u, g in UG}
    sel = {k: jnp.zeros((n_sel, QB), F32) for k in UG}
    for _ in range(top):
        for k in UG:
            top_v = jnp.max(imp[k], axis=0, keepdims=True)
            first = jnp.min(jnp.where(imp[k] == top_v, j_idx, n_sel), axis=0, keepdims=True)
            pick = j_idx == first
            sel[k] = jnp.where(pick, 1.0, sel[k])
            imp[k] = jnp.where(pick, -3e38, imp[k])
    for u, g in UG:
        pen = jnp.where((sel[u, g] > 0.5) & causal_blk[u], 0.0, NEG)
        if n_sel < 64:
            pen = jnp.concatenate([pen, jnp.zeros((64 - n_sel, QB), F32)], axis=0)
        qt_ref[u, g, A_PEN:A_PEN + 64, :] = jnp.concatenate([pen] * R, axis=1).astype(BF16)

    def score(j):
        rows = pl.ds(pl.multiple_of(j * KS, KS), KS)
        return [jnp.dot(ksa_ref[g, rows, :], qt_ref[u, g], preferred_element_type=F32) for u, g in UG]

    def absorb(s, j, st):
        m_new = [jnp.maximum(st[n][0], jnp.max(s[n], axis=0, keepdims=True)) for n in range(len(UG))]
        pr = [jnp.exp(s[n] - m_new[n]).astype(BF16) for n in range(len(UG))]
        return [(m_new[n], jnp.exp(st[n][0] - m_new[n]) * st[n][1] + pv(vsa_ref, UG[n][1], j * SUB, pr[n]))
                for n in range(len(UG))]

    n_full = qi[0] // SUB
    init = [(jnp.full((1, HQ), NEG, F32), jnp.zeros((VR, HQ), F32)) for _ in UG]
    st = lax.fori_loop(0, n_full, lambda j, st_: absorb(score(j), j, st_), init)
    s_last = score(n_full)
    for n, (u, g) in enumerate(UG):
        diag = pl.ds(pl.multiple_of(q0[u] - n_full * KS, QB), QB)
        s_ref[u, g] = s_last[n]
        s_ref[u, g, diag, :] = jnp.where(k_loc <= q_loc, s_ref[u, g, diag, :], NEG)
    st = absorb([s_ref[u, g] for u, g in UG], n_full, st)
    o_sel = {UG[n]: st[n][1][0:dh] / st[n][1][dh:dh + 1] for n in range(len(UG))}

    gl_t = gl_ref[0].T
    for u in UR:
        rows_out = []
        for g in GR:
            gate = [jax.nn.sigmoid(jnp.concatenate(
                [gl_t[(g * R + r) * N_BRANCH + br:(g * R + r) * N_BRANCH + br + 1, u * QB:(u + 1) * QB]
                 for r in range(R)], axis=1)) for br in range(N_BRANCH)]
            o_t = gate[0] * o_cmp[u, g] + gate[1] * o_sel[u, g] + gate[2] * o_win[u, g]
            rows_out += [o_t[:, r * QB:(r + 1) * QB] for r in range(R)]
        o_ref[0, u * QB:(u + 1) * QB, :] = jnp.concatenate(rows_out, axis=0).T.astype(o_ref.dtype)


def nsa_attention(y3, small3, q_norm, k_norm, kc, vc):
    b, s, _ = y3.shape
    G, R, dh, QB, CK, VR = NSA_KV_HEADS, NSA_GROUP, HEAD_DIM, NSA_QB, NSA_CK, NSA_VR
    assert (NSA_KS // QB) % NSA_QBLK == 0, "a step's query blocks must share their span of NSA_KS keys"
    HQ = R * QB
    nq = s // QB
    n_sel = s // SEL_BLOCK
    assert n_sel <= 64, "selection one-hot columns hold at most 64 blocks"
    top = min(SEL_TOPK, n_sel)
    n_cmp = kc.shape[2]

    def pos_cols(pos):
        return np.stack([pos // 64 * 64, pos % 64, np.ones_like(pos), np.ones_like(pos)], axis=1)

    vc_t = vc.transpose(0, 1, 3, 2).astype(BF16)

    pos = np.arange(s)
    kconst = np.zeros((s, CK), np.float32)
    kconst[pos, A_PEN + pos // SEL_BLOCK] = 1.0
    kconst[:, A_BIAS:A_BIAS + 4] = pos_cols(pos)
    kconst = jnp.asarray(kconst, BF16)

    kc_hi = kc.astype(BF16)
    kc_lo = (kc - kc_hi.astype(F32)).astype(BF16)
    cend = np.arange(n_cmp) * CMP_STRIDE + (CMP_BLOCK - 1)
    cbias = np.zeros((n_cmp, CK - 2 * dh), np.float32)
    cbias[:, 0:4] = pos_cols(cend)
    kc_aug = jnp.concatenate([kc_hi, kc_lo, jnp.broadcast_to(jnp.asarray(cbias, BF16), (b, G, n_cmp, CK - 2 * dh))],
                             axis=-1)

    qw = G * R * dh
    kvb = C_NSA_KV // LANES
    gq = jnp.tile(q_norm, G * R).reshape(1, qw)
    gk = jnp.stack([jnp.tile(k_norm[1], G), jnp.tile(k_norm[2], G)]).reshape(2, 1, LANES)

    def kv_spec(blk):
        return pl.BlockSpec((1, s, LANES), lambda i, j: (i, 0, kvb + blk))

    return pl.pallas_call(
        functools.partial(_nsa_kernel, n_sel=n_sel, top=top),
        out_shape=jax.ShapeDtypeStruct((b, s, MIX_WIDTH), BF16),
        grid=(b, nq // NSA_QBLK),
        in_specs=[pl.BlockSpec((1, NSA_QBLK * QB, qw), lambda i, j: (i, j, C_NSA_Q // qw)),
                  pl.BlockSpec((1, NSA_QBLK * QB, N_SMALL), lambda i, j: (i, j, 0)),
                  pl.BlockSpec((1, G, n_cmp, CK), lambda i, j: (i, 0, 0, 0)),
                  pl.BlockSpec((1, G, dh, n_cmp), lambda i, j: (i, 0, 0, 0)),
                  kv_spec(2), kv_spec(4), kv_spec(3), kv_spec(5),
                  pl.BlockSpec((s, CK), lambda i, j: (0, 0)),
                  pl.BlockSpec((1, qw), lambda i, j: (0, 0)),
                  pl.BlockSpec((2, 1, LANES), lambda i, j: (0, 0, 0))],
        out_specs=pl.BlockSpec((1, NSA_QBLK * QB, qw), lambda i, j: (i, j, 0)),
        scratch_shapes=[pltpu.VMEM((NSA_QBLK, G, CK, HQ), BF16),
                        pltpu.VMEM((G, s, CK), BF16), pltpu.VMEM((G, s + WINDOW, CK), BF16),
                        pltpu.VMEM((G, nq, VR, QB), BF16), pltpu.VMEM((G, nq + WINDOW // QB, VR, QB), BF16),
                        pltpu.VMEM((NSA_QBLK, G, NSA_KS, HQ), F32)],
        compiler_params=_params("parallel", "arbitrary"),
        name="nsa_attention",
    )(y3, small3, kc_aug, vc_t, y3, y3, y3, y3, kconst, gq, gk)


def _merge_kernel(on_ref, os_ref, om_ref, g0_ref, g1_ref, g2_ref, wb_ref, wo_ref, x_ref, mod_ref, o_ref):
    merged = None
    for i, (o_r, g_r) in enumerate(((on_ref, g0_ref), (os_ref, g1_ref), (om_ref, g2_ref))):
        br = jnp.dot(o_r[0], wb_ref[i], preferred_element_type=F32)
        term = jax.nn.sigmoid(g_r[0].astype(F32)) * br
        merged = term if merged is None else merged + term
    out = jnp.dot(merged.astype(BF16), wo_ref[...], preferred_element_type=F32)
    o_ref[0] = x_ref[0] + mod_ref[0, 2:3, :] * out


def merge_project(o_nsa, o_sb, o_ml, y3, w_branch, w_out, x, mod, tm=512):
    b, s, d = x.shape
    W = MIX_WIDTH
    ospec = pl.BlockSpec((1, tm, W), lambda i, j: (i, j, 0))
    xspec = pl.BlockSpec((1, tm, d), lambda i, j: (i, j, 0))
    gspecs = [pl.BlockSpec((1, tm, d), functools.partial(lambda i, j, c: (i, j, c), c=C_MERGE // d + c))
              for c in range(N_BRANCH)]
    return pl.pallas_call(
        _merge_kernel,
        out_shape=jax.ShapeDtypeStruct((b, s, d), F32),
        grid=(b, s // tm),
        in_specs=[ospec, ospec, ospec] + gspecs + [
            pl.BlockSpec((N_BRANCH, W, d), lambda i, j: (0, 0, 0)),
            pl.BlockSpec((d, d), lambda i, j: (0, 0)),
            xspec,
            pl.BlockSpec((1, 6, d), lambda i, j: (i, 0, 0))],
        out_specs=xspec,
        compiler_params=_params("parallel", "parallel"),
        name="merge_project",
    )(o_nsa, o_sb, o_ml, y3, y3, y3, w_branch.astype(BF16), w_out.astype(BF16), x, mod)


def _ffn_kernel(x_ref, g_ref, mod_ref, wg_ref, wu_ref, wd_ref, o_ref, h_ref, acc_ref):
    f = pl.program_id(2)

    @pl.when(f == 0)
    def _():
        h_ref[...] = _norm_mod(x_ref[0], g_ref[...], mod_ref[0], 3, 4).astype(BF16)
        acc_ref[...] = jnp.zeros_like(acc_ref)

    h = h_ref[...]
    a = jnp.dot(h, wg_ref[...], preferred_element_type=F32)
    u = jnp.dot(h, wu_ref[...], preferred_element_type=F32)
    act = (a * jax.nn.sigmoid(a) * u).astype(BF16)
    acc_ref[...] += jnp.dot(act, wd_ref[...], preferred_element_type=F32)

    @pl.when(f == pl.num_programs(2) - 1)
    def _():
        o_ref[0] = x_ref[0] + mod_ref[0, 5:6, :] * acc_ref[...]


def dense_ffn(x, g, mod, wg, wu, wd, tm=512, n_ftiles=2):
    b, s, d = x.shape
    ff = wg.shape[1]
    tf = -(-ff // (n_ftiles * LANES)) * LANES
    pad = n_ftiles * tf - ff
    wg = jnp.pad(to_bf16(wg), ((0, 0), (0, pad)))
    wu = jnp.pad(to_bf16(wu), ((0, 0), (0, pad)))
    wd = jnp.pad(to_bf16(wd), ((0, pad), (0, 0)))
    xspec = pl.BlockSpec((1, tm, d), lambda i, j, f: (i, j, 0))
    return pl.pallas_call(
        _ffn_kernel,
        out_shape=jax.ShapeDtypeStruct((b, s, d), F32),
        grid=(b, s // tm, n_ftiles),
        in_specs=[xspec,
                  pl.BlockSpec((1, d), lambda i, j, f: (0, 0)),
                  pl.BlockSpec((1, 6, d), lambda i, j, f: (i, 0, 0)),
                  pl.BlockSpec((d, tf), lambda i, j, f: (0, f)),
                  pl.BlockSpec((d, tf), lambda i, j, f: (0, f)),
                  pl.BlockSpec((tf, d), lambda i, j, f: (f, 0))],
        out_specs=xspec,
        scratch_shapes=[pltpu.VMEM((tm, d), BF16), pltpu.VMEM((tm, d), F32)],
        compiler_params=_params("parallel", "parallel", "arbitrary"),
        name="dense_ffn",
    )(x, g.reshape(1, d), mod, wg, wu, wd)


def _router_kernel(x_ref, g_ref, mod_ref, wr_ref, h_ref, e_ref, p_ref):
    h = _norm_mod(x_ref[0], g_ref[...], mod_ref[0], 3, 4)
    h_ref[...] = h
    lane = _iota((1, LANES), 1)
    real = lane < N_EXPERTS
    logits = jnp.where(real, jnp.dot(h, wr_ref[...], precision=HIGHEST, preferred_element_type=F32), NEG)
    e = jnp.exp(logits - jnp.max(logits, axis=1, keepdims=True))
    p = jnp.where(real, e / jnp.sum(e, axis=1, keepdims=True), -1.0)
    p1 = jnp.max(p, axis=1, keepdims=True)
    i1 = jnp.min(jnp.where(p == p1, lane, LANES), axis=1, keepdims=True)
    rest = jnp.where(lane == i1, -1.0, p)
    p2 = jnp.max(rest, axis=1, keepdims=True)
    i2 = jnp.min(jnp.where(rest == p2, lane, LANES), axis=1, keepdims=True)
    tot = p1 + p2
    e_ref[...] = jnp.where(lane == 0, i1, jnp.where(lane == 1, i2, 0))[:, 0:N_EXPERTS]
    p_ref[...] = jnp.where(lane == 0, p1 / tot, jnp.where(lane == 1, p2 / tot, 0.0))[:, 0:N_EXPERTS]


def moe_router(x, g, mod, w_router, tm=512):
    b, s, d = x.shape
    t = b * s
    spb = s // tm
    wr = jnp.pad(w_router, ((0, 0), (0, LANES - N_EXPERTS)))
    return pl.pallas_call(
        _router_kernel,
        out_shape=(jax.ShapeDtypeStruct((t, d), F32),
                   jax.ShapeDtypeStruct((t, N_EXPERTS), I32),
                   jax.ShapeDtypeStruct((t, N_EXPERTS), F32)),
        grid=(b, spb),
        in_specs=[pl.BlockSpec((1, tm, d), lambda i, j: (i, j, 0)),
                  pl.BlockSpec((1, d), lambda i, j: (0, 0)),
                  pl.BlockSpec((1, 6, d), lambda i, j: (i, 0, 0)),
                  pl.BlockSpec((d, LANES), lambda i, j: (0, 0))],
        out_specs=(pl.BlockSpec((tm, d), lambda i, j: (i * spb + j, 0)),
                   pl.BlockSpec((tm, N_EXPERTS), lambda i, j: (i * spb + j, 0)),
                   pl.BlockSpec((tm, N_EXPERTS), lambda i, j: (i * spb + j, 0))),
        compiler_params=_params("parallel", "parallel"),
        name="moe_router",
    )(x, g.reshape(1, d), mod, wr)


def moe_ffn(x, g, mod, w_router, wg, wu, wd, tb=512):
    b, s, d = x.shape
    t = b * s
    a = t * TOP_K
    h, top_e, top_p = moe_router(x, g, mod, w_router)
    e_flat = top_e[:, 0:TOP_K].reshape(a)
    onehot = (e_flat[:, None] == jnp.arange(N_EXPERTS, dtype=I32)[None, :]).astype(I32)
    csum = jnp.cumsum(onehot, axis=0)
    rank = jnp.sum(onehot * csum, axis=1) - 1
    counts = csum[-1]
    padded = (counts + tb - 1) // tb * tb
    pad_ends = jnp.cumsum(padded)
    pad_starts = pad_ends - padded
    dest = (jnp.sum(onehot * pad_starts[None, :], axis=1) + rank).astype(I32)
    n_rows = (a // tb + N_EXPERTS + 1) * tb
    n_blk = n_rows // tb
    blk_expert = jnp.minimum(
        jnp.searchsorted(pad_ends, jnp.arange(n_blk, dtype=I32) * tb, side="right"), N_EXPERTS - 1).astype(I32)
    n_used = (pad_ends[-1:] // tb).astype(I32)
    pad_lo = jnp.concatenate([pad_starts + counts, pad_ends[-1:]]).astype(I32)
    pad_hi = jnp.concatenate([pad_ends, jnp.full((1,), n_rows, I32)]).astype(I32)
    slot_assign = moe_invert(dest, pad_lo, pad_hi, n_rows, tb)
    y2 = moe_experts(h, slot_assign, blk_expert, n_used, to_bf16(wg), to_bf16(wu), to_bf16(wd), tb)
    return moe_mix(y2, x, top_p, mod)


def _invert_kernel(dest_ref, lo_ref, hi_ref, sa_ref, *, n_assign, tb, n_chunks):
    phase = pl.program_id(0)
    chunk = pl.program_id(1)

    @pl.when((phase == 0) & (chunk < lo_ref.shape[0]))
    def _():
        def fill(p, c):
            sa_ref[p] = n_assign + (p & (2 * tb - 1))
            return c

        lax.fori_loop(lo_ref[chunk], hi_ref[chunk], fill, 0)

    @pl.when(phase == 1)
    def _():
        per = n_assign // n_chunks

        def put(j, c):
            a = chunk * per + j
            sa_ref[dest_ref[a]] = a
            return c

        lax.fori_loop(0, per, put, 0, unroll=8)


def moe_invert(dest, pad_lo, pad_hi, n_rows, tb, n_chunks=16):
    n_assign = dest.shape[0]
    assert tb & (tb - 1) == 0 and n_assign % n_chunks == 0 and pad_lo.shape[0] <= n_chunks
    smem = pl.BlockSpec(memory_space=pltpu.SMEM)
    return pl.pallas_call(
        functools.partial(_invert_kernel, n_assign=n_assign, tb=tb, n_chunks=n_chunks),
        out_shape=jax.ShapeDtypeStruct((n_rows,), I32),
        grid=(2, n_chunks),
        in_specs=[smem, smem, smem],
        out_specs=smem,
        compiler_params=pltpu.CompilerParams(dimension_semantics=("arbitrary", "arbitrary")),
        name="moe_invert",
    )(dest, pad_lo, pad_hi)


def _expert_kernel(be_ref, nu_ref, sa_ref, h_hbm, wg_ref, wu_ref, wd_ref, y_hbm,
                   xin_ref, yout_ref, xb_ref, acc_ref, sem_in, sem_out, *, tb, n_tok, n_f):
    i = pl.program_id(0)
    f = pl.program_id(1)
    n_used = nu_ref[0]
    n_assign = n_tok * TOP_K
    rows_f = tb // n_f
    active = i <= n_used

    def gather_row(blk, r):
        a = sa_ref[blk * tb + r]
        tok = jnp.where(a < n_assign, a >> 1, 0)
        return pltpu.make_async_copy(h_hbm.at[tok], xin_ref.at[blk % 2, r], sem_in.at[blk % 2])

    def scatter_row(blk, r):
        a = jnp.where(blk >= 0, sa_ref[jnp.maximum(blk, 0) * tb + r], n_assign + tb + r)
        row = jnp.where(a < n_assign, (a & 1) * n_tok + (a >> 1), a)
        return pltpu.make_async_copy(yout_ref.at[(blk + 2) % 2, r], y_hbm.at[row], sem_out.at[(blk + 2) % 2])

    def for_rows(fn):
        def body(r, c):
            fn(r)
            return c
        lax.fori_loop(0, tb, body, 0, unroll=8)

    def wait_rows(row_copy):
        for_rows(lambda r: row_copy.wait())

    @pl.when(f == 0)
    def _():
        @pl.when(i == 0)
        def _():
            yout_ref[1] = jnp.zeros((tb, yout_ref.shape[2]), F32)
            for_rows(lambda r: gather_row(0, r).start())

            def clear_row(r):
                return pltpu.make_async_copy(yout_ref.at[1, r], y_hbm.at[n_assign + r], sem_out.at[0])

            for_rows(lambda r: clear_row(r).start())
            wait_rows(clear_row(0))

        @pl.when((i == 0) | (i - 1 <= n_used))
        def _():
            wait_rows(gather_row(i, 0))

        @pl.when(active)
        def _():
            xb_ref[...] = xin_ref[i % 2].astype(BF16)
            acc_ref[...] = jnp.zeros_like(acc_ref)

    @pl.when(active)
    def _():
        for r in range(rows_f):
            gather_row(i + 1, f * rows_f + r).start()
            scatter_row(i - 1, f * rows_f + r).start()
        xb = xb_ref[...]
        a = jnp.dot(xb, wg_ref[0], preferred_element_type=F32)
        u = jnp.dot(xb, wu_ref[0], preferred_element_type=F32)
        act = (a * jax.nn.sigmoid(a) * u).astype(BF16)
        acc_ref[...] += jnp.dot(act, wd_ref[0], preferred_element_type=F32)

    @pl.when(f == n_f - 1)
    def _():
        @pl.when((i >= 1) & (i - 1 <= n_used))
        def _():
            wait_rows(scatter_row(i - 2, 0))

        @pl.when(active)
        def _():
            yout_ref[i % 2] = acc_ref[...]


def moe_experts(h, slot_assign, blk_expert, n_used, wg, wu, wd, tb, tf=1792):
    n_tok, d = h.shape
    p = slot_assign.shape[0]
    ff = wg.shape[2]
    return pl.pallas_call(
        functools.partial(_expert_kernel, tb=tb, n_tok=n_tok, n_f=ff // tf),
        out_shape=jax.ShapeDtypeStruct((n_tok * TOP_K + 2 * tb, d), F32),
        grid_spec=pltpu.PrefetchScalarGridSpec(
            num_scalar_prefetch=3,
            grid=(p // tb, ff // tf),
            in_specs=[pl.BlockSpec(memory_space=pl.ANY),
                      pl.BlockSpec((1, d, tf), lambda i, f, be, nu, sa: (be[i], 0, f)),
                      pl.BlockSpec((1, d, tf), lambda i, f, be, nu, sa: (be[i], 0, f)),
                      pl.BlockSpec((1, tf, d), lambda i, f, be, nu, sa: (be[i], f, 0))],
            out_specs=pl.BlockSpec(memory_space=pl.ANY),
            scratch_shapes=[pltpu.VMEM((2, tb, d), F32), pltpu.VMEM((2, tb, d), F32),
                            pltpu.VMEM((tb, d), BF16), pltpu.VMEM((tb, d), F32),
                            pltpu.SemaphoreType.DMA((2,)), pltpu.SemaphoreType.DMA((2,))]),
        compiler_params=pltpu.CompilerParams(dimension_semantics=("arbitrary", "arbitrary"),
                                             vmem_limit_bytes=VMEM_LIMIT, has_side_effects=True),
        name="moe_experts",
    )(blk_expert, n_used, slot_assign, h, wg, wu, wd)


def _mix_kernel(y0_ref, y1_ref, x_ref, p_ref, mod_ref, o_ref):
    w = p_ref[...]
    f = w[:, 0:1] * y0_ref[...] + w[:, 1:2] * y1_ref[...]
    o_ref[0] = x_ref[0] + mod_ref[0, 5:6, :] * f


def moe_mix(y2, x, top_p, mod, td=512):
    b, s, d = x.shape
    spb = s // td
    nt = b * spb
    return pl.pallas_call(
        _mix_kernel,
        out_shape=jax.ShapeDtypeStruct((b, s, d), F32),
        grid=(b, spb),
        in_specs=[pl.BlockSpec((td, d), lambda i, j: (i * spb + j, 0)),
                  pl.BlockSpec((td, d), lambda i, j: (nt + i * spb + j, 0)),
                  pl.BlockSpec((1, td, d), lambda i, j: (i, j, 0)),
                  pl.BlockSpec((td, N_EXPERTS), lambda i, j: (i * spb + j, 0)),
                  pl.BlockSpec((1, 6, d), lambda i, j: (i, 0, 0))],
        out_specs=pl.BlockSpec((1, td, d), lambda i, j: (i, j, 0)),
        compiler_params=_params("parallel", "parallel"),
        name="moe_mix",
    )(y2, y2, x, top_p, mod)


def _pack_w_in(w_in):
    kv = 2 * NSA_KV_HEADS * HEAD_DIM * 3
    w_in = to_bf16(w_in)
    o = 0
    nsa_q = w_in[:, o:o + MIX_WIDTH]; o += MIX_WIDTH
    nsa_kv = w_in[:, o:o + kv]; o += kv
    nsa_gate = w_in[:, o:o + NSA_HEADS * N_BRANCH]; o += NSA_HEADS * N_BRANCH
    sb = w_in[:, o:o + 3 * MIX_WIDTH]; o += 3 * MIX_WIDTH
    ml_qkv = w_in[:, o:o + 3 * MIX_WIDTH]; o += 3 * MIX_WIDTH
    ml_if = w_in[:, o:o + 2 * ML_HEADS]; o += 2 * ML_HEADS
    ml_o = w_in[:, o:o + MIX_WIDTH]; o += MIX_WIDTH
    merge = w_in[:, o:]
    main = jnp.concatenate([merge, nsa_q, ml_qkv, ml_o, sb, nsa_kv], axis=1)
    small = jnp.concatenate([nsa_gate, ml_if], axis=1)
    small = jnp.pad(small, ((0, 0), (0, N_SMALL - small.shape[1])))
    return main, small


def token_mixer_layer(x, mod, norm_g, w_in, nsa_q_norm, nsa_k_norm, cmp_pos, cmp_w1, cmp_b1, cmp_w2,
                      cmp_b2, ml_conv_w, ml_conv_b, ml_gate_b, w_branch, w_out):
    w_main, w_small = _pack_w_in(w_in)
    y3, small3 = in_projection(x, norm_g, mod, w_main, w_small)
    o_sb = sb_attention(y3)
    o_ml = mlstm(y3, small3, ml_conv_w, ml_conv_b, ml_gate_b)
    kc, vc = nsa_compress(y3, cmp_pos, cmp_w1, cmp_b1, cmp_w2, cmp_b2, nsa_k_norm[0])
    o_nsa = nsa_attention(y3, small3, nsa_q_norm, nsa_k_norm, kc, vc)
    return merge_project(o_nsa, o_sb, o_ml, y3, w_branch, w_out, x, mod)


def kernel(x, c, ada_w, ada_b, norm_mix, norm_ffn, w_in, nsa_q_norm, nsa_k_norm, cmp_pos, cmp_w1, cmp_b1,
           cmp_w2, cmp_b2, ml_conv_w, ml_conv_b, ml_gate_b, w_branch, w_out, ffn_wg, ffn_wu, ffn_wd,
           moe_router, moe_wg, moe_wu, moe_wd):
    depth = ada_w.shape[0]
    b, s, d = x.shape
    mods = adaln(c, ada_w, ada_b).reshape(depth, b, 6, d)
    for layer in range(depth):
        mod = mods[layer]
        x = token_mixer_layer(x, mod, norm_mix[layer], w_in[layer], nsa_q_norm[layer], nsa_k_norm[layer],
                              cmp_pos[layer], cmp_w1[layer], cmp_b1[layer], cmp_w2[layer], cmp_b2[layer],
                              ml_conv_w[layer], ml_conv_b[layer], ml_gate_b[layer], w_branch[layer],
                              w_out[layer])
        j = layer // 2
        if layer % 2 == 0:
            x = dense_ffn(x, norm_ffn[layer], mod, ffn_wg[j], ffn_wu[j], ffn_wd[j])
        else:
            x = moe_ffn(x, norm_ffn[layer], mod, moe_router[j], moe_wg[j], moe_wu[j], moe_wd[j])
    return x
```

```python
import functools

import numpy as np
import jax
import jax.numpy as jnp
from jax import lax
from jax.experimental import pallas as pl
from jax.experimental.pallas import tpu as pltpu

F32 = jnp.float32
BF16 = jnp.bfloat16
I32 = jnp.int32
HIGHEST = lax.Precision.HIGHEST

EPS = 1e-6
NEG = -1e30
HEAD_DIM = 64
MIX_WIDTH = 512
NSA_HEADS = 8
NSA_KV_HEADS = 2
NSA_GROUP = NSA_HEADS // NSA_KV_HEADS
CMP_BLOCK = 32
CMP_STRIDE = 16
SEL_BLOCK = 64
SEL_TOPK = 16
WINDOW = 512
FORCE_BONUS = 1e4
ML_HEADS = 4
ML_HEAD_DIM = 128
ML_CHUNK = 64
ML_UNROLL = 2
CONV_WIDTH = 4
N_BRANCH = 3
N_EXPERTS = 8
TOP_K = 2
LANES = 128

C_MERGE = 0
C_NSA_Q = 3072
C_ML_Q = 3584
C_ML_K = 4096
C_ML_V = 4608
C_ML_O = 5120
C_SB_Q = 5632
C_SB_K = 6144
C_SB_V = 6656
C_NSA_KV = 7168
N_MAIN = 7936
S_NSA_GATE = 0
S_ML_I = 24
S_ML_F = 28
N_SMALL = 128

VMEM_LIMIT = 56 * 1024 * 1024


def _params(*sem):
    return pltpu.CompilerParams(dimension_semantics=sem, vmem_limit_bytes=VMEM_LIMIT)


def _iota(shape, dim):
    return lax.broadcasted_iota(I32, shape, dim)


def _split_dot(a32, b_bf16):
    hi = a32.astype(BF16)
    lo = (a32 - hi.astype(F32)).astype(BF16)
    return (jnp.dot(hi, b_bf16, preferred_element_type=F32)
            + jnp.dot(lo, b_bf16, preferred_element_type=F32))


def _split_dot_left(a_bf16, b32):
    hi = b32.astype(BF16)
    lo = (b32 - hi.astype(F32)).astype(BF16)
    return (jnp.dot(a_bf16, hi, preferred_element_type=F32)
            + jnp.dot(a_bf16, lo, preferred_element_type=F32))


def _dot_nt(a, b):
    return lax.dot_general(a, b, (((1,), (1,)), ((), ())), preferred_element_type=F32)


def _log_sigmoid(z):
    return jnp.minimum(z, 0.0) - jnp.log1p(jnp.exp(-jnp.abs(z)))


def _cast_kernel(x_ref, o_ref):
    o_ref[...] = x_ref[...].astype(o_ref.dtype)


def to_bf16(w, max_rows=512):
    cols = w.shape[-1]
    w2 = w.reshape(-1, cols)
    rows = w2.shape[0]
    tr = max(t for t in range(8, max_rows + 1, 8) if rows % t == 0)
    out = pl.pallas_call(
        _cast_kernel,
        out_shape=jax.ShapeDtypeStruct((rows, cols), BF16),
        grid=(rows // tr,),
        in_specs=[pl.BlockSpec((tr, cols), lambda i: (i, 0))],
        out_specs=pl.BlockSpec((tr, cols), lambda i: (i, 0)),
        compiler_params=_params("parallel"),
        name="to_bf16",
    )(w2)
    return out.reshape(w.shape)


def _adaln_kernel(c_ref, w_ref, b_ref, o_ref):
    c = c_ref[...]
    cond = c * jax.nn.sigmoid(c)
    o_ref[0] = jnp.dot(cond, w_ref[0], precision=HIGHEST, preferred_element_type=F32) + b_ref[0]


def adaln(c, ada_w, ada_b):
    depth, d, n = ada_w.shape
    b = c.shape[0]
    tn = 1536
    return pl.pallas_call(
        _adaln_kernel,
        out_shape=jax.ShapeDtypeStruct((depth, b, n), F32),
        grid=(depth, n // tn),
        in_specs=[pl.BlockSpec((b, d), lambda l, j: (0, 0)),
                  pl.BlockSpec((1, d, tn), lambda l, j: (l, 0, j)),
                  pl.BlockSpec((1, 1, tn), lambda l, j: (l, 0, j))],
        out_specs=pl.BlockSpec((1, b, tn), lambda l, j: (l, 0, j)),
        compiler_params=_params("parallel", "parallel"),
        name="adaln",
    )(c, ada_w, ada_b.reshape(depth, 1, n))


def _norm_mod(x, g, mod, shift_row, scale_row):
    ms = jnp.mean(x * x, axis=-1, keepdims=True)
    y = x * lax.rsqrt(ms + EPS) * g
    return y * (1.0 + mod[scale_row:scale_row + 1, :]) + mod[shift_row:shift_row + 1, :]


def _in_proj_kernel(x_ref, g_ref, mod_ref, wm_ref, ws_ref, y_ref, sm_ref, h_ref):
    @pl.when(pl.program_id(2) == 0)
    def _():
        h_ref[...] = _norm_mod(x_ref[0], g_ref[...], mod_ref[0], 0, 1).astype(BF16)
        sm_ref[0] = jnp.dot(h_ref[...], ws_ref[...], preferred_element_type=F32)

    y_ref[0] = jnp.dot(h_ref[...], wm_ref[...], preferred_element_type=F32).astype(y_ref.dtype)


def in_projection(x, g, mod, w_main, w_small, tm=512, n_tiles=2):
    b, s, d = x.shape
    tn = N_MAIN // n_tiles
    return pl.pallas_call(
        _in_proj_kernel,
        out_shape=(jax.ShapeDtypeStruct((b, s, N_MAIN), BF16), jax.ShapeDtypeStruct((b, s, N_SMALL), F32)),
        grid=(b, s // tm, n_tiles),
        in_specs=[pl.BlockSpec((1, tm, d), lambda i, j, n: (i, j, 0)),
                  pl.BlockSpec((1, d), lambda i, j, n: (0, 0)),
                  pl.BlockSpec((1, 6, d), lambda i, j, n: (i, 0, 0)),
                  pl.BlockSpec((d, tn), lambda i, j, n: (0, n)),
                  pl.BlockSpec((d, N_SMALL), lambda i, j, n: (0, 0))],
        out_specs=(pl.BlockSpec((1, tm, tn), lambda i, j, n: (i, j, n)),
                   pl.BlockSpec((1, tm, N_SMALL), lambda i, j, n: (i, j, 0))),
        scratch_shapes=[pltpu.VMEM((tm, d), BF16)],
        compiler_params=_params("parallel", "parallel", "arbitrary"),
        name="in_projection",
    )(x, g.reshape(1, d), mod, w_main, w_small)


SB_EXP_DROP = 104.0
SB_EAGER = 2


def _sb_kernel(q_ref, k_ref, v_ref, o_ref, vt_ref, *, tq, n_str):
    TK, dh = LANES, HEAD_DIM
    n_sub = tq // TK
    W = 2 * tq
    SR = range(n_str)
    qi = pl.program_id(2)

    @pl.when(qi == 0)
    def _():
        for c in range(v_ref.shape[1] // tq):
            v_t = v_ref[0, c * tq:(c + 1) * tq, :].astype(F32).T.astype(BF16)
            for p in SR:
                vt_ref[p, c] = v_t[p * LANES:(p + 1) * LANES]

    q_t = (q_ref[0].astype(F32) * (dh ** -0.5)).T
    chan = _iota((2 * dh, tq), 0)
    q_cat = []
    for p in SR:
        q_p = q_t[p * LANES:(p + 1) * LANES]
        q_cat.append(jnp.concatenate([jnp.where(chan < dh, q_p, 0.0), jnp.where(chan < dh, 0.0, q_p)],
                                     axis=1).astype(BF16))
    later = (_iota((TK, TK), 0) < _iota((TK, TK), 1)).astype(BF16)
    suffix = jnp.concatenate([jnp.concatenate([later, later], axis=1), jnp.ones((8, 2 * TK), BF16)], axis=0)

    def steps(blocks, st):
        work = [(p, b) for p in SR for b in range(len(blocks))]
        lss, his, los = {}, {}, {}
        for p, b in work:
            j, keep = blocks[b]
            k0 = pl.multiple_of(j * tq, tq)
            z = jnp.dot(k_ref[0, pl.ds(k0, tq), p * LANES:(p + 1) * LANES], q_cat[p],
                        preferred_element_type=F32)
            drop = jnp.maximum(z, 0.0) + jnp.log(1.0 + jnp.exp(-jnp.abs(z)))
            lss[p, b] = z - drop
            if keep is not None:
                drop = jnp.where(keep, drop, 0.0)
            his[p, b] = drop.astype(BF16)
            los[p, b] = (drop - his[p, b].astype(F32)).astype(BF16)
        carry = [st[p][0] for p in SR]
        afters = {}
        for p, b in work:
            after = [None] * n_sub
            for sub in range(n_sub - 1, -1, -1):
                rows = slice(sub * TK, (sub + 1) * TK)
                res = jnp.dot(suffix, jnp.concatenate([his[p, b][rows], los[p, b][rows]], axis=0),
                              preferred_element_type=F32)
                after[sub] = res[0:TK] + carry[p]
                carry[p] = carry[p] + res[TK:TK + 1]
            afters[p, b] = jnp.concatenate(after, axis=0)
        acc = [st[p][1] for p in SR]
        for p, b in work:
            j, keep = blocks[b]
            a = jnp.exp(lss[p, b] - afters[p, b])
            if keep is not None:
                a = jnp.where(keep, a, 0.0)
            acc[p] = acc[p] + jnp.dot(vt_ref[p, j], a.astype(BF16), preferred_element_type=F32)
        return [(carry[p], acc[p]) for p in SR]

    def cond(c):
        least = jnp.min(c[1][0][0])
        for p in range(1, n_str):
            least = jnp.minimum(least, jnp.min(c[1][p][0]))
        return (c[0] >= 0) & (least < SB_EXP_DROP)

    def body(c):
        return c[0] - 1, steps([(c[0], None)], c[1])

    strict = _iota((tq, W), 0) < (_iota((tq, W), 1) & (tq - 1))
    eager = [(jnp.maximum(qi - d, 0), qi >= d) for d in range(1, SB_EAGER + 1)]
    st = steps([(qi, strict)] + eager, [(jnp.zeros((1, W), F32), jnp.zeros((2 * dh, W), F32)) for p in SR])
    _, st = lax.while_loop(cond, body, (qi - 1 - SB_EAGER, st))
    out_rows = []
    for p in SR:
        out_rows += [st[p][1][0:dh, 0:tq], st[p][1][dh:2 * dh, tq:W]]
    o_ref[0] = jnp.concatenate(out_rows, axis=0).T.astype(o_ref.dtype)


def sb_attention(y3, tq=256, n_str=2):
    b, s, _ = y3.shape
    wide = n_str * LANES
    qb, kb, vb = C_SB_Q // wide, C_SB_K // wide, C_SB_V // wide
    return pl.pallas_call(
        functools.partial(_sb_kernel, tq=tq, n_str=n_str),
        out_shape=jax.ShapeDtypeStruct((b, s, MIX_WIDTH), BF16),
        grid=(b, MIX_WIDTH // wide, s // tq),
        in_specs=[pl.BlockSpec((1, tq, wide), lambda i, p, j: (i, j, qb + p)),
                  pl.BlockSpec((1, s, wide), lambda i, p, j: (i, 0, kb + p)),
                  pl.BlockSpec((1, s, wide), lambda i, p, j: (i, 0, vb + p))],
        out_specs=pl.BlockSpec((1, tq, wide), lambda i, p, j: (i, j, p)),
        scratch_shapes=[pltpu.VMEM((n_str, s // tq, LANES, tq), BF16)],
        compiler_params=_params("parallel", "parallel", "arbitrary"),
        name="sb_attention",
    )(y3, y3, y3)


def _mlstm_kernel(q_ref, k_ref, v_ref, og_ref, sm_ref, gr_ref, cw_ref, cb_ref, gb_ref, out_ref,
                  ct_ref, n_ref, m_ref, xbuf_ref, qk_ref, *, ts):
    L, dh, H, W = ML_CHUNK, ML_HEAD_DIM, ML_HEADS, MIX_WIDTH
    halo = 8
    sblk = pl.program_id(1)

    @pl.when(sblk == 0)
    def _():
        ct_ref[...] = jnp.zeros_like(ct_ref)
        n_ref[...] = jnp.zeros_like(n_ref)
        m_ref[...] = jnp.zeros_like(m_ref)
        xbuf_ref[0:halo, :] = jnp.zeros((halo, 2 * W), F32)

    @pl.when(sblk > 0)
    def _():
        xbuf_ref[0:halo, :] = xbuf_ref[ts:ts + halo, :]

    xbuf_ref[halo:halo + ts, 0:W] = q_ref[0].astype(F32)
    xbuf_ref[halo:halo + ts, W:2 * W] = k_ref[0].astype(F32)
    conv = cb_ref[...]
    for j in range(CONV_WIDTH):
        off = halo - (CONV_WIDTH - 1) + j
        conv = conv + cw_ref[j:j + 1, :] * xbuf_ref[off:off + ts, :]
    act = conv * jax.nn.sigmoid(conv)
    qk_ref[:, 0:W] = (act[:, 0:W] * (dh ** -0.5)).astype(BF16)
    qk_ref[:, W:2 * W] = act[:, W:2 * W].astype(BF16)

    it0, it1 = _iota((L, L), 0), _iota((L, L), 1)
    causal = it0 >= it1
    tri_lo = causal.astype(BF16)
    tri_up = (it0 <= it1).astype(BF16)

    def chunks(it, carry):
        HR = range(H)
        UR = range(ML_UNROLL)
        UH = [(u, h) for u in UR for h in HR]
        cols = [slice(h * dh, (h + 1) * dh) for h in HR]
        rows = [pl.ds(pl.multiple_of((it * ML_UNROLL + u) * L, L), L) for u in UR]
        sm = [sm_ref[0, rows[u], :] for u in UR]
        gr = [gr_ref[0, it * ML_UNROLL + u] for u in UR]
        ig_col = {(u, h): sm[u][:, S_ML_I + h:S_ML_I + h + 1] + gb_ref[0, h] for u, h in UH}
        lf_col = {(u, h): _log_sigmoid(sm[u][:, S_ML_F + h:S_ML_F + h + 1] + gb_ref[1, h]) for u, h in UH}
        ig_row = {(u, h): gr[u][h:h + 1, :] + gb_ref[0, h] for u, h in UH}
        lf_row = {(u, h): _log_sigmoid(gr[u][H + h:H + h + 1, :] + gb_ref[1, h]) for u, h in UH}
        b_t = {k: _split_dot_left(tri_lo, jnp.broadcast_to(lf_col[k], (L, L))) for k in UH}
        b_s = {k: _split_dot(jnp.broadcast_to(lf_row[k], (L, L)), tri_up) for k in UH}
        qq = {(u, h): qk_ref[rows[u], cols[h]] for u, h in UH}
        kk = {(u, h): qk_ref[rows[u], W + h * dh:W + (h + 1) * dh] for u, h in UH}
        vv = {(u, h): v_ref[0, rows[u], cols[h]] for u, h in UH}
        qk = {k: _dot_nt(qq[k], kk[k]) for k in UH}
        kt = {k: kk[k].astype(F32).T.astype(BF16) for k in UH}
        dmat = {k: jnp.where(causal, b_t[k] - b_s[k] + ig_row[k], NEG) for k in UH}
        d_max = {k: jnp.max(dmat[k], axis=1, keepdims=True) for k in UH}
        b_col = {k: b_t[k][:, 0:1] for k in UH}
        b_last = {k: b_t[k][L - 1:L, 0:1] for k in UH}
        decay = {k: b_last[k] - b_col[k] + ig_col[k] for k in UH}
        decay_max = {k: jnp.max(decay[k], axis=0, keepdims=True) for k in UH}
        m_prev, m_new = {}, {}
        for u, h in UH:
            m_prev[u, h] = m_ref[h][:, 0:1] if u == 0 else m_new[u - 1, h]
            m_new[u, h] = jnp.maximum(b_last[u, h] + m_prev[u, h], decay_max[u, h])
        m_inter = {k: b_col[k] + m_prev[k] for k in UH}
        m_t = {k: jnp.maximum(m_inter[k], d_max[k]) for k in UH}
        w = {k: jnp.exp(dmat[k] - m_t[k]) * qk[k] for k in UH}
        inter = {k: jnp.exp(m_inter[k] - m_t[k]) for k in UH}
        w_v = {k: jnp.dot(w[k].astype(BF16), vv[k], preferred_element_type=F32) for k in UH}
        ws = {k: jnp.exp(decay[k] - m_new[k]) for k in UH}
        cscale = {k: jnp.exp(b_last[k] + m_prev[k] - m_new[k]) for k in UH}
        wv = {k: (ws[k] * vv[k].astype(F32)).astype(BF16) for k in UH}
        k_wv = {k: jnp.dot(kt[k], wv[k], preferred_element_type=F32) for k in UH}
        k_ws = {k: jnp.sum(ws[k] * kk[k].astype(F32), axis=0, keepdims=True) for k in UH}
        w_sum = {k: jnp.sum(w[k], axis=1, keepdims=True) for k in UH}
        ct = {h: ct_ref[h] for h in HR}
        nvec = {h: n_ref[h] for h in HR}
        for u, h in UH:
            k = (u, h)
            num = inter[k] * jnp.dot(qq[k], ct[h].astype(BF16), preferred_element_type=F32) + w_v[k]
            den = inter[k] * jnp.sum(qq[k].astype(F32) * nvec[h], axis=1, keepdims=True) + w_sum[k]
            hval = num / jnp.maximum(jnp.abs(den), jnp.exp(-m_t[k]))
            ct[h] = cscale[k] * ct[h] + k_wv[k]
            nvec[h] = cscale[k] * nvec[h] + k_ws[k]
            gate = jax.nn.sigmoid(og_ref[0, rows[u], cols[h]].astype(F32))
            out_ref[0, rows[u], cols[h]] = (gate * hval).astype(out_ref.dtype)
        for h in HR:
            ct_ref[h] = ct[h]
            n_ref[h] = nvec[h]
            m_ref[h] = jnp.broadcast_to(m_new[ML_UNROLL - 1, h], (1, LANES))
        return carry

    lax.fori_loop(0, ts // (L * ML_UNROLL), chunks, 0)


def mlstm(y3, small3, conv_w, conv_b, gate_b, ts=512):
    b, s, _ = y3.shape
    W, H, L = MIX_WIDTH, ML_HEADS, ML_CHUNK
    gr = small3[:, :, S_ML_I:S_ML_I + 2 * H].reshape(b, s // L, L, 2 * H).transpose(0, 1, 3, 2)
    cq, ck, cv, co = C_ML_Q // W, C_ML_K // W, C_ML_V // W, C_ML_O // W
    return pl.pallas_call(
        functools.partial(_mlstm_kernel, ts=ts),
        out_shape=jax.ShapeDtypeStruct((b, s, W), BF16),
        grid=(b, s // ts),
        in_specs=[pl.BlockSpec((1, ts, W), lambda i, j: (i, j, cq)),
                  pl.BlockSpec((1, ts, W), lambda i, j: (i, j, ck)),
                  pl.BlockSpec((1, ts, W), lambda i, j: (i, j, cv)),
                  pl.BlockSpec((1, ts, W), lambda i, j: (i, j, co)),
                  pl.BlockSpec((1, ts, N_SMALL), lambda i, j: (i, j, 0)),
                  pl.BlockSpec((1, ts // L, 2 * H, L), lambda i, j: (i, j, 0, 0)),
                  pl.BlockSpec((CONV_WIDTH, 2 * W), lambda i, j: (0, 0)),
                  pl.BlockSpec((1, 2 * W), lambda i, j: (0, 0)),
                  pl.BlockSpec(memory_space=pltpu.SMEM)],
        out_specs=pl.BlockSpec((1, ts, W), lambda i, j: (i, j, 0)),
        scratch_shapes=[pltpu.VMEM((H, ML_HEAD_DIM, ML_HEAD_DIM), F32),
                        pltpu.VMEM((H, 1, ML_HEAD_DIM), F32),
                        pltpu.VMEM((H, 1, LANES), F32),
                        pltpu.VMEM((ts + 8, 2 * W), F32),
                        pltpu.VMEM((ts, 2 * W), BF16)],
        compiler_params=_params("parallel", "arbitrary"),
        name="mlstm",
    )(y3, y3, y3, y3, small3, gr, conv_w, conv_b.reshape(1, 2 * W), gate_b)


def _gelu_tanh(x):
    return 0.5 * x * (1.0 + jnp.tanh(0.7978845608028654 * (x + 0.044715 * (x * x * x))))


def _compress_kernel(ra_ref, rb_ref, pos_ref, w1_ref, b1_ref, w2_ref, b2_ref, kn_ref, kc_ref, vc_ref):
    half = (CMP_BLOCK // 2) * HEAD_DIM
    for j, o_ref in enumerate((kc_ref, vc_ref)):
        xa = (ra_ref[j, 0, 0].astype(F32) + pos_ref[j, :, 0:half]).astype(BF16)
        xb = (rb_ref[j, 0, 0].astype(F32) + pos_ref[j, :, half:2 * half]).astype(BF16)
        hid = (jnp.dot(xa, w1_ref[j, 0:half, :], preferred_element_type=F32)
               + jnp.dot(xb, w1_ref[j, half:2 * half, :], preferred_element_type=F32) + b1_ref[j])
        out = jnp.dot(_gelu_tanh(hid).astype(BF16), w2_ref[j], preferred_element_type=F32) + b2_ref[j]
        if j == 0:
            out = out * lax.rsqrt(jnp.mean(out * out, axis=-1, keepdims=True) + EPS) * kn_ref[...]
        o_ref[0, 0] = out


def nsa_compress(y3, cmp_pos, cmp_w1, cmp_b1, cmp_w2, cmp_b2, k_norm0):
    b, s, _ = y3.shape
    G, dh = NSA_KV_HEADS, HEAD_DIM
    nr = s // CMP_STRIDE
    wide = CMP_STRIDE * dh
    kv = y3[:, :, C_NSA_KV:C_NSA_KV + 2 * G * dh].reshape(b, s, 2, G, dh)
    ra = kv.transpose(2, 0, 3, 1, 4).reshape(2, b, G, nr, wide)
    rb = jnp.concatenate([ra[:, :, :, 1:], jnp.zeros((2, b, G, 1, wide), ra.dtype)], axis=3)
    hidden = cmp_w1.shape[-1]
    out = jax.ShapeDtypeStruct((b, G, nr, dh), F32)
    blk = pl.BlockSpec((2, 1, 1, nr, wide), lambda i, g: (0, i, g, 0, 0))
    oblk = pl.BlockSpec((1, 1, nr, dh), lambda i, g: (i, g, 0, 0))

    def full(shape):
        return pl.BlockSpec(shape, lambda i, g: (0,) * len(shape))

    return pl.pallas_call(
        _compress_kernel,
        out_shape=(out, out),
        grid=(b, G),
        in_specs=[blk, blk, full((2, 1, 2 * wide)), full((2, 2 * wide, hidden)), full((2, 1, hidden)),
                  full((2, hidden, dh)), full((2, 1, dh)), full((1, dh))],
        out_specs=(oblk, oblk),
        compiler_params=_params("parallel", "parallel"),
        name="nsa_compress",
    )(ra, rb, cmp_pos.reshape(2, 1, 2 * wide), cmp_w1.astype(BF16), cmp_b1.reshape(2, 1, hidden),
      cmp_w2.astype(BF16), cmp_b2.reshape(2, 1, dh), k_norm0.reshape(1, dh))


NSA_QB = 128
NSA_QBLK = 2
NSA_KS = 512
NSA_CK = 256
NSA_VR = 80
A_FEAT, A_PEN, A_BIAS = 0, 64, 128
A_DUMMY = A_BIAS + 4


def _head_rms(x, gain):
    w = x.shape[1]
    same_head = (_iota((w, w), 0) // HEAD_DIM == _iota((w, w), 1) // HEAD_DIM).astype(BF16)
    ss = _split_dot(x * x, same_head)
    return x * lax.rsqrt(ss * (1.0 / HEAD_DIM) + EPS) * gain


def _nsa_kernel(q_ref, gl_ref, kca_ref, vcT_ref, ks_ref, kw_ref, vs_ref, vw_ref, kconst_ref, gq_ref, gk_ref,
                     o_ref, qt_ref, ksa_ref, kwa_ref, vsa_ref, vwa_ref, s_ref, *, n_sel, top):
    QB, R, dh, CK, VR, KS = NSA_QB, NSA_GROUP, HEAD_DIM, NSA_CK, NSA_VR, NSA_KS
    G = NSA_KV_HEADS
    GR = range(G)
    UR = range(NSA_QBLK)
    UG = [(u, g) for u in UR for g in GR]
    HQ = R * QB
    NPAD = WINDOW // QB
    SUB = KS // QB
    step = pl.program_id(1)
    qi = [step * NSA_QBLK + u for u in UR]
    q0 = [qi[u] * QB for u in UR]
    nkb = vsa_ref.shape[1]

    @pl.when(step == 0)
    def _():
        pad_keys = jnp.where(_iota((WINDOW, CK), 1) == A_DUMMY, 1.0, 0.0).astype(BF16)
        ones_rows = jnp.where(_iota((nkb + NPAD, VR - dh, QB), 1) == 0, 1.0, 0.0).astype(BF16)
        for g in GR:
            heads = slice(g * dh, (g + 1) * dh)
            ksa_ref[g] = kconst_ref[...]
            kwa_ref[g, 0:WINDOW, :] = pad_keys
            kwa_ref[g, WINDOW:, :] = kconst_ref[...]
            kwa_ref[g, WINDOW:, A_PEN:A_PEN + 64] = jnp.zeros((kwa_ref.shape[1] - WINDOW, 64), BF16)
            vwa_ref[g, 0:NPAD, 0:dh, :] = jnp.zeros((NPAD, dh, QB), BF16)
            vsa_ref[g, :, dh:VR, :] = ones_rows[0:nkb]
            vwa_ref[g, :, dh:VR, :] = ones_rows
        for c in range(ks_ref.shape[1] // KS):
            keys = slice(c * KS, (c + 1) * KS)
            ks_n = _head_rms(ks_ref[0, keys, :].astype(F32), gk_ref[0]).astype(BF16)
            kw_n = _head_rms(kw_ref[0, keys, :].astype(F32), gk_ref[1]).astype(BF16)
            for g in GR:
                heads = slice(g * dh, (g + 1) * dh)
                ksa_ref[g, keys, A_FEAT:A_FEAT + dh] = ks_n[:, heads]
                kwa_ref[g, WINDOW + c * KS:WINDOW + (c + 1) * KS, A_FEAT:A_FEAT + dh] = kw_n[:, heads]
        for c in range(nkb):
            keys = slice(c * QB, (c + 1) * QB)
            vs_t = vs_ref[0, keys, :].astype(F32).T.astype(BF16)
            vw_t = vw_ref[0, keys, :].astype(F32).T.astype(BF16)
            for g in GR:
                heads = slice(g * dh, (g + 1) * dh)
                vsa_ref[g, c, 0:dh, :] = vs_t[heads]
                vwa_ref[g, NPAD + c, 0:dh, :] = vw_t[heads]
        qt_ref[:, :, A_BIAS + 16:CK, :] = jnp.zeros((NSA_QBLK, G, CK - A_BIAS - 16, HQ), BF16)

    q_n = _head_rms(q_ref[0].astype(F32), gq_ref[...]).astype(BF16)
    q_rows = (q_n.astype(F32) * (dh ** -0.5)).T
    lane = _iota((16, HQ), 1)
    rowi = _iota((16, HQ), 0)
    for u, g in UG:
        t_q = q0[u] + (lane & (QB - 1))
        t_hi = ((t_q >> 6) << 6).astype(F32)
        t_lo = (t_q & 63).astype(F32)
        qT = jnp.concatenate([q_rows[(g * R + r) * dh:(g * R + r + 1) * dh, u * QB:(u + 1) * QB]
                              for r in range(R)], axis=1).astype(BF16)
        qt_ref[u, g, A_FEAT:A_FEAT + dh, :] = qT
        qt_ref[u, g, A_PEN:A_PEN + dh, :] = qT
        slope = jnp.exp2(-(g * R + (lane >> 7) + 1).astype(F32))
        bias_rows = jnp.where(rowi < 2, slope,
                              jnp.where(rowi == 2, -slope * t_hi,
                                        jnp.where(rowi == 3, -slope * t_lo,
                                                  jnp.where(rowi == A_DUMMY - A_BIAS, NEG, 0.0))))
        qt_ref[u, g, A_BIAS:A_BIAS + 16, :] = bias_rows.astype(BF16)
    k_loc = _iota((QB, HQ), 0)
    q_loc = _iota((QB, HQ), 1) & (QB - 1)

    def pv(v_ref_, g, kb0, pr):
        out = None
        for i in range(pr.shape[0] // QB):
            term = jnp.dot(v_ref_[g, kb0 + i], pr[i * QB:(i + 1) * QB], preferred_element_type=F32)
            out = term if out is None else out + term
        return out

    n_cmp = kca_ref.shape[2]
    cmp_end = _iota((n_cmp, HQ), 0) * CMP_STRIDE + (CMP_BLOCK - 1)
    valid = [cmp_end <= q0[u] + (_iota((n_cmp, HQ), 1) & (QB - 1)) for u in UR]
    sc = {(u, g): jnp.where(valid[u], jnp.dot(kca_ref[0, g], qt_ref[u, g], preferred_element_type=F32), NEG)
          for u, g in UG}
    mx = {k: jnp.max(sc[k], axis=0, keepdims=True) for k in UG}
    e = {k: jnp.exp(sc[k] - mx[k]) for k in UG}
    inv = {k: jnp.where(mx[k] > 0.5 * NEG, 1.0 / jnp.sum(e[k], axis=0, keepdims=True), 0.0) for k in UG}
    p = {k: e[k] * inv[k] for k in UG}
    o_cmp = {(u, g): jnp.dot(vcT_ref[0, g], p[u, g].astype(BF16), preferred_element_type=F32)
             for u, g in UG}

    sw = {(u, g): jnp.dot(kwa_ref[g, pl.ds(pl.multiple_of(q0[u], QB), WINDOW + QB), :], qt_ref[u, g],
                          preferred_element_type=F32) for u, g in UG}
    sw = {k: jnp.concatenate([jnp.where(k_loc > q_loc, sw[k][0:QB], NEG), sw[k][QB:WINDOW],
                              jnp.where(k_loc <= q_loc, sw[k][WINDOW:WINDOW + QB], NEG)], axis=0) for k in UG}
    pw = {k: jnp.exp(sw[k] - jnp.max(sw[k], axis=0, keepdims=True)).astype(BF16) for k in UG}
    acc_w = {(u, g): pv(vwa_ref, g, qi[u], pw[u, g]) for u, g in UG}
    o_win = {k: acc_w[k][0:dh] / acc_w[k][dh:dh + 1] for k in UG}

    c0 = _iota((n_sel, n_cmp), 1) * CMP_STRIDE
    s0 = _iota((n_sel, n_cmp), 0) * SEL_BLOCK
    overlap_t = ((c0 < s0 + SEL_BLOCK) & (c0 + CMP_BLOCK > s0)).astype(BF16)
    j_idx = _iota((n_sel, QB), 0)
    tq = [q0[u] + _iota((n_sel, QB), 1) for u in UR]
    forced = [(j_idx == 0) | (j_idx == (tq[u] >> 6)) | (j_idx == (tq[u] >> 6) - 1) for u in UR]
    causal_blk = [j_idx * SEL_BLOCK <= tq[u] for u in UR]
    p_grp = {k: p[k][:, 0:QB] + p[k][:, QB:2 * QB] + p[k][:, 2 * QB:3 * QB] + p[k][:, 3 * QB:4 * QB] for k in UG}
    imp = {(u, g): jnp.where(causal_blk[u], _split_dot_left(overlap_t, p_grp[u, g])
                             + jnp.where(forced[u], FORCE_BONUS, 0.0), -1.0) for u, g in UG}
    sel = {k: jnp.zeros((n_sel, QB), F32) for k in UG}
    for _ in range(top):
        for k in UG:
            top_v = jnp.max(imp[k], axis=0, keepdims=True)
            first = jnp.min(jnp.where(imp[k] == top_v, j_idx, n_sel), axis=0, keepdims=True)
            pick = j_idx == first
            sel[k] = jnp.where(pick, 1.0, sel[k])
            imp[k] = jnp.where(pick, -3e38, imp[k])
    for u, g in UG:
        pen = jnp.where((sel[u, g] > 0.5) & causal_blk[u], 0.0, NEG)
        if n_sel < 64:
            pen = jnp.concatenate([pen, jnp.zeros((64 - n_sel, QB), F32)], axis=0)
        qt_ref[u, g, A_PEN:A_PEN + 64, :] = jnp.concatenate([pen] * R, axis=1).astype(BF16)

    def score(j):
        rows = pl.ds(pl.multiple_of(j * KS, KS), KS)
        return [jnp.dot(ksa_ref[g, rows, :], qt_ref[u, g], preferred_element_type=F32) for u, g in UG]

    def absorb(s, j, st):
        m_new = [jnp.maximum(st[n][0], jnp.max(s[n], axis=0, keepdims=True)) for n in range(len(UG))]
        pr = [jnp.exp(s[n] - m_new[n]).astype(BF16) for n in range(len(UG))]
        return [(m_new[n], jnp.exp(st[n][0] - m_new[n]) * st[n][1] + pv(vsa_ref, UG[n][1], j * SUB, pr[n]))
                for n in range(len(UG))]

    n_full = qi[0] // SUB
    init = [(jnp.full((1, HQ), NEG, F32), jnp.zeros((VR, HQ), F32)) for _ in UG]
    st = lax.fori_loop(0, n_full, lambda j, st_: absorb(score(j), j, st_), init)
    s_last = score(n_full)
    for n, (u, g) in enumerate(UG):
        diag = pl.ds(pl.multiple_of(q0[u] - n_full * KS, QB), QB)
        s_ref[u, g] = s_last[n]
        s_ref[u, g, diag, :] = jnp.where(k_loc <= q_loc, s_ref[u, g, diag, :], NEG)
    st = absorb([s_ref[u, g] for u, g in UG], n_full, st)
    o_sel = {UG[n]: st[n][1][0:dh] / st[n][1][dh:dh + 1] for n in range(len(UG))}

    gl_t = gl_ref[0].T
    for u in UR:
        rows_out = []
        for g in GR:
            gate = [jax.nn.sigmoid(jnp.concatenate(
                [gl_t[(g * R + r) * N_BRANCH + br:(g * R + r) * N_BRANCH + br + 1, u * QB:(u + 1) * QB]
                 for r in range(R)], axis=1)) for br in range(N_BRANCH)]
            o_t = gate[0] * o_cmp[u, g] + gate[1] * o_sel[u, g] + gate[2] * o_win[u, g]
            rows_out += [o_t[:, r * QB:(r + 1) * QB] for r in range(R)]
        o_ref[0, u * QB:(u + 1) * QB, :] = jnp.concatenate(rows_out, axis=0).T.astype(o_ref.dtype)


def nsa_attention(y3, small3, q_norm, k_norm, kc, vc):
    b, s, _ = y3.shape
    G, R, dh, QB, CK, VR = NSA_KV_HEADS, NSA_GROUP, HEAD_DIM, NSA_QB, NSA_CK, NSA_VR
    assert (NSA_KS // QB) % NSA_QBLK == 0, "a step's query blocks must share their span of NSA_KS keys"
    HQ = R * QB
    nq = s // QB
    n_sel = s // SEL_BLOCK
    assert n_sel <= 64, "selection one-hot columns hold at most 64 blocks"
    top = min(SEL_TOPK, n_sel)
    n_cmp = kc.shape[2]

    def pos_cols(pos):
        return np.stack([pos // 64 * 64, pos % 64, np.ones_like(pos), np.ones_like(pos)], axis=1)

    vc_t = vc.transpose(0, 1, 3, 2).astype(BF16)

    pos = np.arange(s)
    kconst = np.zeros((s, CK), np.float32)
    kconst[pos, A_PEN + pos // SEL_BLOCK] = 1.0
    kconst[:, A_BIAS:A_BIAS + 4] = pos_cols(pos)
    kconst = jnp.asarray(kconst, BF16)

    kc_hi = kc.astype(BF16)
    kc_lo = (kc - kc_hi.astype(F32)).astype(BF16)
    cend = np.arange(n_cmp) * CMP_STRIDE + (CMP_BLOCK - 1)
    cbias = np.zeros((n_cmp, CK - 2 * dh), np.float32)
    cbias[:, 0:4] = pos_cols(cend)
    kc_aug = jnp.concatenate([kc_hi, kc_lo, jnp.broadcast_to(jnp.asarray(cbias, BF16), (b, G, n_cmp, CK - 2 * dh))],
                             axis=-1)

    qw = G * R * dh
    kvb = C_NSA_KV // LANES
    gq = jnp.tile(q_norm, G * R).reshape(1, qw)
    gk = jnp.stack([jnp.tile(k_norm[1], G), jnp.tile(k_norm[2], G)]).reshape(2, 1, LANES)

    def kv_spec(blk):
        return pl.BlockSpec((1, s, LANES), lambda i, j: (i, 0, kvb + blk))

    return pl.pallas_call(
        functools.partial(_nsa_kernel, n_sel=n_sel, top=top),
        out_shape=jax.ShapeDtypeStruct((b, s, MIX_WIDTH), BF16),
        grid=(b, nq // NSA_QBLK),
        in_specs=[pl.BlockSpec((1, NSA_QBLK * QB, qw), lambda i, j: (i, j, C_NSA_Q // qw)),
                  pl.BlockSpec((1, NSA_QBLK * QB, N_SMALL), lambda i, j: (i, j, 0)),
                  pl.BlockSpec((1, G, n_cmp, CK), lambda i, j: (i, 0, 0, 0)),
                  pl.BlockSpec((1, G, dh, n_cmp), lambda i, j: (i, 0, 0, 0)),
                  kv_spec(2), kv_spec(4), kv_spec(3), kv_spec(5),
                  pl.BlockSpec((s, CK), lambda i, j: (0, 0)),
                  pl.BlockSpec((1, qw), lambda i, j: (0, 0)),
                  pl.BlockSpec((2, 1, LANES), lambda i, j: (0, 0, 0))],
        out_specs=pl.BlockSpec((1, NSA_QBLK * QB, qw), lambda i, j: (i, j, 0)),
        scratch_shapes=[pltpu.VMEM((NSA_QBLK, G, CK, HQ), BF16),
                        pltpu.VMEM((G, s, CK), BF16), pltpu.VMEM((G, s + WINDOW, CK), BF16),
                        pltpu.VMEM((G, nq, VR, QB), BF16), pltpu.VMEM((G, nq + WINDOW // QB, VR, QB), BF16),
                        pltpu.VMEM((NSA_QBLK, G, NSA_KS, HQ), F32)],
        compiler_params=_params("parallel", "arbitrary"),
        name="nsa_attention",
    )(y3, small3, kc_aug, vc_t, y3, y3, y3, y3, kconst, gq, gk)


def _merge_kernel(on_ref, os_ref, om_ref, g0_ref, g1_ref, g2_ref, wb_ref, wo_ref, x_ref, mod_ref, o_ref):
    merged = None
    for i, (o_r, g_r) in enumerate(((on_ref, g0_ref), (os_ref, g1_ref), (om_ref, g2_ref))):
        br = jnp.dot(o_r[0], wb_ref[i], preferred_element_type=F32)
        term = jax.nn.sigmoid(g_r[0].astype(F32)) * br
        merged = term if merged is None else merged + term
    out = jnp.dot(merged.astype(BF16), wo_ref[...], preferred_element_type=F32)
    o_ref[0] = x_ref[0] + mod_ref[0, 2:3, :] * out


def merge_project(o_nsa, o_sb, o_ml, y3, w_branch, w_out, x, mod, tm=512):
    b, s, d = x.shape
    W = MIX_WIDTH
    ospec = pl.BlockSpec((1, tm, W), lambda i, j: (i, j, 0))
    xspec = pl.BlockSpec((1, tm, d), lambda i, j: (i, j, 0))
    gspecs = [pl.BlockSpec((1, tm, d), functools.partial(lambda i, j, c: (i, j, c), c=C_MERGE // d + c))
              for c in range(N_BRANCH)]
    return pl.pallas_call(
        _merge_kernel,
        out_shape=jax.ShapeDtypeStruct((b, s, d), F32),
        grid=(b, s // tm),
        in_specs=[ospec, ospec, ospec] + gspecs + [
            pl.BlockSpec((N_BRANCH, W, d), lambda i, j: (0, 0, 0)),
            pl.BlockSpec((d, d), lambda i, j: (0, 0)),
            xspec,
            pl.BlockSpec((1, 6, d), lambda i, j: (i, 0, 0))],
        out_specs=xspec,
        compiler_params=_params("parallel", "parallel"),
        name="merge_project",
    )(o_nsa, o_sb, o_ml, y3, y3, y3, w_branch.astype(BF16), w_out.astype(BF16), x, mod)


def _ffn_kernel(x_ref, g_ref, mod_ref, wg_ref, wu_ref, wd_ref, o_ref, h_ref, acc_ref):
    f = pl.program_id(2)

    @pl.when(f == 0)
    def _():
        h_ref[...] = _norm_mod(x_ref[0], g_ref[...], mod_ref[0], 3, 4).astype(BF16)
        acc_ref[...] = jnp.zeros_like(acc_ref)

    h = h_ref[...]
    a = jnp.dot(h, wg_ref[...], preferred_element_type=F32)
    u = jnp.dot(h, wu_ref[...], preferred_element_type=F32)
    act = (a * jax.nn.sigmoid(a) * u).astype(BF16)
    acc_ref[...] += jnp.dot(act, wd_ref[...], preferred_element_type=F32)

    @pl.when(f == pl.num_programs(2) - 1)
    def _():
        o_ref[0] = x_ref[0] + mod_ref[0, 5:6, :] * acc_ref[...]


def dense_ffn(x, g, mod, wg, wu, wd, tm=512, n_ftiles=2):
    b, s, d = x.shape
    ff = wg.shape[1]
    tf = -(-ff // (n_ftiles * LANES)) * LANES
    pad = n_ftiles * tf - ff
    wg = jnp.pad(to_bf16(wg), ((0, 0), (0, pad)))
    wu = jnp.pad(to_bf16(wu), ((0, 0), (0, pad)))
    wd = jnp.pad(to_bf16(wd), ((0, pad), (0, 0)))
    xspec = pl.BlockSpec((1, tm, d), lambda i, j, f: (i, j, 0))
    return pl.pallas_call(
        _ffn_kernel,
        out_shape=jax.ShapeDtypeStruct((b, s, d), F32),
        grid=(b, s // tm, n_ftiles),
        in_specs=[xspec,
                  pl.BlockSpec((1, d), lambda i, j, f: (0, 0)),
                  pl.BlockSpec((1, 6, d), lambda i, j, f: (i, 0, 0)),
                  pl.BlockSpec((d, tf), lambda i, j, f: (0, f)),
                  pl.BlockSpec((d, tf), lambda i, j, f: (0, f)),
                  pl.BlockSpec((tf, d), lambda i, j, f: (f, 0))],
        out_specs=xspec,
        scratch_shapes=[pltpu.VMEM((tm, d), BF16), pltpu.VMEM((tm, d), F32)],
        compiler_params=_params("parallel", "parallel", "arbitrary"),
        name="dense_ffn",
    )(x, g.reshape(1, d), mod, wg, wu, wd)


def _router_kernel(x_ref, g_ref, mod_ref, wr_ref, h_ref, e_ref, p_ref):
    h = _norm_mod(x_ref[0], g_ref[...], mod_ref[0], 3, 4)
    h_ref[...] = h
    lane = _iota((1, LANES), 1)
    real = lane < N_EXPERTS
    logits = jnp.where(real, jnp.dot(h, wr_ref[...], precision=HIGHEST, preferred_element_type=F32), NEG)
    e = jnp.exp(logits - jnp.max(logits, axis=1, keepdims=True))
    p = jnp.where(real, e / jnp.sum(e, axis=1, keepdims=True), -1.0)
    p1 = jnp.max(p, axis=1, keepdims=True)
    i1 = jnp.min(jnp.where(p == p1, lane, LANES), axis=1, keepdims=True)
    rest = jnp.where(lane == i1, -1.0, p)
    p2 = jnp.max(rest, axis=1, keepdims=True)
    i2 = jnp.min(jnp.where(rest == p2, lane, LANES), axis=1, keepdims=True)
    tot = p1 + p2
    e_ref[...] = jnp.where(lane == 0, i1, jnp.where(lane == 1, i2, 0))[:, 0:N_EXPERTS]
    p_ref[...] = jnp.where(lane == 0, p1 / tot, jnp.where(lane == 1, p2 / tot, 0.0))[:, 0:N_EXPERTS]


def moe_router(x, g, mod, w_router, tm=512):
    b, s, d = x.shape
    t = b * s
    spb = s // tm
    wr = jnp.pad(w_router, ((0, 0), (0, LANES - N_EXPERTS)))
    return pl.pallas_call(
        _router_kernel,
        out_shape=(jax.ShapeDtypeStruct((t, d), F32),
                   jax.ShapeDtypeStruct((t, N_EXPERTS), I32),
                   jax.ShapeDtypeStruct((t, N_EXPERTS), F32)),
        grid=(b, spb),
        in_specs=[pl.BlockSpec((1, tm, d), lambda i, j: (i, j, 0)),
                  pl.BlockSpec((1, d), lambda i, j: (0, 0)),
                  pl.BlockSpec((1, 6, d), lambda i, j: (i, 0, 0)),
                  pl.BlockSpec((d, LANES), lambda i, j: (0, 0))],
        out_specs=(pl.BlockSpec((tm, d), lambda i, j: (i * spb + j, 0)),
                   pl.BlockSpec((tm, N_EXPERTS), lambda i, j: (i * spb + j, 0)),
                   pl.BlockSpec((tm, N_EXPERTS), lambda i, j: (i * spb + j, 0))),
        compiler_params=_params("parallel", "parallel"),
        name="moe_router",
    )(x, g.reshape(1, d), mod, wr)


def moe_ffn(x, g, mod, w_router, wg, wu, wd, tb=512):
    b, s, d = x.shape
    t = b * s
    a = t * TOP_K
    h, top_e, top_p = moe_router(x, g, mod, w_router)
    e_flat = top_e[:, 0:TOP_K].reshape(a)
    onehot = (e_flat[:, None] == jnp.arange(N_EXPERTS, dtype=I32)[None, :]).astype(I32)
    csum = jnp.cumsum(onehot, axis=0)
    rank = jnp.sum(onehot * csum, axis=1) - 1
    counts = csum[-1]
    padded = (counts + tb - 1) // tb * tb
    pad_ends = jnp.cumsum(padded)
    pad_starts = pad_ends - padded
    dest = (jnp.sum(onehot * pad_starts[None, :], axis=1) + rank).astype(I32)
    n_rows = (a // tb + N_EXPERTS + 1) * tb
    n_blk = n_rows // tb
    blk_expert = jnp.minimum(
        jnp.searchsorted(pad_ends, jnp.arange(n_blk, dtype=I32) * tb, side="right"), N_EXPERTS - 1).astype(I32)
    n_used = (pad_ends[-1:] // tb).astype(I32)
    pad_lo = jnp.concatenate([pad_starts + counts, pad_ends[-1:]]).astype(I32)
    pad_hi = jnp.concatenate([pad_ends, jnp.full((1,), n_rows, I32)]).astype(I32)
    slot_assign = moe_invert(dest, pad_lo, pad_hi, n_rows, tb)
    y2 = moe_experts(h, slot_assign, blk_expert, n_used, to_bf16(wg), to_bf16(wu), to_bf16(wd), tb)
    return moe_mix(y2, x, top_p, mod)


def _invert_kernel(dest_ref, lo_ref, hi_ref, sa_ref, *, n_assign, tb, n_chunks):
    phase = pl.program_id(0)
    chunk = pl.program_id(1)

    @pl.when((phase == 0) & (chunk < lo_ref.shape[0]))
    def _():
        def fill(p, c):
            sa_ref[p] = n_assign + (p & (2 * tb - 1))
            return c

        lax.fori_loop(lo_ref[chunk], hi_ref[chunk], fill, 0)

    @pl.when(phase == 1)
    def _():
        per = n_assign // n_chunks

        def put(j, c):
            a = chunk * per + j
            sa_ref[dest_ref[a]] = a
            return c

        lax.fori_loop(0, per, put, 0, unroll=8)


def moe_invert(dest, pad_lo, pad_hi, n_rows, tb, n_chunks=16):
    n_assign = dest.shape[0]
    assert tb & (tb - 1) == 0 and n_assign % n_chunks == 0 and pad_lo.shape[0] <= n_chunks
    smem = pl.BlockSpec(memory_space=pltpu.SMEM)
    return pl.pallas_call(
        functools.partial(_invert_kernel, n_assign=n_assign, tb=tb, n_chunks=n_chunks),
        out_shape=jax.ShapeDtypeStruct((n_rows,), I32),
        grid=(2, n_chunks),
        in_specs=[smem, smem, smem],
        out_specs=smem,
        compiler_params=pltpu.CompilerParams(dimension_semantics=("arbitrary", "arbitrary")),
        name="moe_invert",
    )(dest, pad_lo, pad_hi)


def _expert_kernel(be_ref, nu_ref, sa_ref, h_hbm, wg_ref, wu_ref, wd_ref, y_hbm,
                   xin_ref, yout_ref, xb_ref, acc_ref, sem_in, sem_out, *, tb, n_tok, n_f):
    i = pl.program_id(0)
    f = pl.program_id(1)
    n_used = nu_ref[0]
    n_assign = n_tok * TOP_K
    rows_f = tb // n_f
    active = i <= n_used

    def gather_row(blk, r):
        a = sa_ref[blk * tb + r]
        tok = jnp.where(a < n_assign, a >> 1, 0)
        return pltpu.make_async_copy(h_hbm.at[tok], xin_ref.at[blk % 2, r], sem_in.at[blk % 2])

    def scatter_row(blk, r):
        a = jnp.where(blk >= 0, sa_ref[jnp.maximum(blk, 0) * tb + r], n_assign + tb + r)
        row = jnp.where(a < n_assign, (a & 1) * n_tok + (a >> 1), a)
        return pltpu.make_async_copy(yout_ref.at[(blk + 2) % 2, r], y_hbm.at[row], sem_out.at[(blk + 2) % 2])

    def for_rows(fn):
        def body(r, c):
            fn(r)
            return c
        lax.fori_loop(0, tb, body, 0, unroll=8)

    def wait_rows(row_copy):
        for_rows(lambda r: row_copy.wait())

    @pl.when(f == 0)
    def _():
        @pl.when(i == 0)
        def _():
            yout_ref[1] = jnp.zeros((tb, yout_ref.shape[2]), F32)
            for_rows(lambda r: gather_row(0, r).start())

            def clear_row(r):
                return pltpu.make_async_copy(yout_ref.at[1, r], y_hbm.at[n_assign + r], sem_out.at[0])

            for_rows(lambda r: clear_row(r).start())
            wait_rows(clear_row(0))

        @pl.when((i == 0) | (i - 1 <= n_used))
        def _():
            wait_rows(gather_row(i, 0))

        @pl.when(active)
        def _():
            xb_ref[...] = xin_ref[i % 2].astype(BF16)
            acc_ref[...] = jnp.zeros_like(acc_ref)

    @pl.when(active)
    def _():
        for r in range(rows_f):
            gather_row(i + 1, f * rows_f + r).start()
            scatter_row(i - 1, f * rows_f + r).start(priority=r % 2)
        xb = xb_ref[...]
        a = jnp.dot(xb, wg_ref[0], preferred_element_type=F32)
        u = jnp.dot(xb, wu_ref[0], preferred_element_type=F32)
        act = (a * jax.nn.sigmoid(a) * u).astype(BF16)
        acc_ref[...] += jnp.dot(act, wd_ref[0], preferred_element_type=F32)

    @pl.when(f == n_f - 1)
    def _():
        @pl.when((i >= 1) & (i - 1 <= n_used))
        def _():
            wait_rows(scatter_row(i - 2, 0))

        @pl.when(active)
        def _():
            yout_ref[i % 2] = acc_ref[...]


def moe_experts(h, slot_assign, blk_expert, n_used, wg, wu, wd, tb, tf=1792):
    n_tok, d = h.shape
    p = slot_assign.shape[0]
    ff = wg.shape[2]
    return pl.pallas_call(
        functools.partial(_expert_kernel, tb=tb, n_tok=n_tok, n_f=ff // tf),
        out_shape=jax.ShapeDtypeStruct((n_tok * TOP_K + 2 * tb, d), F32),
        grid_spec=pltpu.PrefetchScalarGridSpec(
            num_scalar_prefetch=3,
            grid=(p // tb, ff // tf),
            in_specs=[pl.BlockSpec(memory_space=pl.ANY),
                      pl.BlockSpec((1, d, tf), lambda i, f, be, nu, sa: (be[i], 0, f)),
                      pl.BlockSpec((1, d, tf), lambda i, f, be, nu, sa: (be[i], 0, f)),
                      pl.BlockSpec((1, tf, d), lambda i, f, be, nu, sa: (be[i], f, 0))],
            out_specs=pl.BlockSpec(memory_space=pl.ANY),
            scratch_shapes=[pltpu.VMEM((2, tb, d), F32), pltpu.VMEM((2, tb, d), F32),
                            pltpu.VMEM((tb, d), BF16), pltpu.VMEM((tb, d), F32),
                            pltpu.SemaphoreType.DMA((2,)), pltpu.SemaphoreType.DMA((2,))]),
        compiler_params=pltpu.CompilerParams(dimension_semantics=("arbitrary", "arbitrary"),
                                             vmem_limit_bytes=VMEM_LIMIT, has_side_effects=True),
        name="moe_experts",
    )(blk_expert, n_used, slot_assign, h, wg, wu, wd)


def _mix_kernel(y0_ref, y1_ref, x_ref, p_ref, mod_ref, o_ref):
    w = p_ref[...]
    f = w[:, 0:1] * y0_ref[...] + w[:, 1:2] * y1_ref[...]
    o_ref[0] = x_ref[0] + mod_ref[0, 5:6, :] * f


def moe_mix(y2, x, top_p, mod, td=512):
    b, s, d = x.shape
    spb = s // td
    nt = b * spb
    return pl.pallas_call(
        _mix_kernel,
        out_shape=jax.ShapeDtypeStruct((b, s, d), F32),
        grid=(b, spb),
        in_specs=[pl.BlockSpec((td, d), lambda i, j: (i * spb + j, 0)),
                  pl.BlockSpec((td, d), lambda i, j: (nt + i * spb + j, 0)),
                  pl.BlockSpec((1, td, d), lambda i, j: (i, j, 0)),
                  pl.BlockSpec((td, N_EXPERTS), lambda i, j: (i * spb + j, 0)),
                  pl.BlockSpec((1, 6, d), lambda i, j: (i, 0, 0))],
        out_specs=pl.BlockSpec((1, td, d), lambda i, j: (i, j, 0)),
        compiler_params=_params("parallel", "parallel"),
        name="moe_mix",
    )(y2, y2, x, top_p, mod)


def _pack_w_in(w_in):
    kv = 2 * NSA_KV_HEADS * HEAD_DIM * 3
    w_in = to_bf16(w_in)
    o = 0
    nsa_q = w_in[:, o:o + MIX_WIDTH]; o += MIX_WIDTH
    nsa_kv = w_in[:, o:o + kv]; o += kv
    nsa_gate = w_in[:, o:o + NSA_HEADS * N_BRANCH]; o += NSA_HEADS * N_BRANCH
    sb = w_in[:, o:o + 3 * MIX_WIDTH]; o += 3 * MIX_WIDTH
    ml_qkv = w_in[:, o:o + 3 * MIX_WIDTH]; o += 3 * MIX_WIDTH
    ml_if = w_in[:, o:o + 2 * ML_HEADS]; o += 2 * ML_HEADS
    ml_o = w_in[:, o:o + MIX_WIDTH]; o += MIX_WIDTH
    merge = w_in[:, o:]
    main = jnp.concatenate([merge, nsa_q, ml_qkv, ml_o, sb, nsa_kv], axis=1)
    small = jnp.concatenate([nsa_gate, ml_if], axis=1)
    small = jnp.pad(small, ((0, 0), (0, N_SMALL - small.shape[1])))
    return main, small


def token_mixer_layer(x, mod, norm_g, w_in, nsa_q_norm, nsa_k_norm, cmp_pos, cmp_w1, cmp_b1, cmp_w2,
                      cmp_b2, ml_conv_w, ml_conv_b, ml_gate_b, w_branch, w_out):
    w_main, w_small = _pack_w_in(w_in)
    y3, small3 = in_projection(x, norm_g, mod, w_main, w_small)
    o_sb = sb_attention(y3)
    o_ml = mlstm(y3, small3, ml_conv_w, ml_conv_b, ml_gate_b)
    kc, vc = nsa_compress(y3, cmp_pos, cmp_w1, cmp_b1, cmp_w2, cmp_b2, nsa_k_norm[0])
    o_nsa = nsa_attention(y3, small3, nsa_q_norm, nsa_k_norm, kc, vc)
    return merge_project(o_nsa, o_sb, o_ml, y3, w_branch, w_out, x, mod)


def kernel(x, c, ada_w, ada_b, norm_mix, norm_ffn, w_in, nsa_q_norm, nsa_k_norm, cmp_pos, cmp_w1, cmp_b1,
           cmp_w2, cmp_b2, ml_conv_w, ml_conv_b, ml_gate_b, w_branch, w_out, ffn_wg, ffn_wu, ffn_wd,
           moe_router, moe_wg, moe_wu, moe_wd):
    depth = ada_w.shape[0]
    b, s, d = x.shape
    mods = adaln(c, ada_w, ada_b).reshape(depth, b, 6, d)
    for layer in range(depth):
        mod = mods[layer]
        x = token_mixer_layer(x, mod, norm_mix[layer], w_in[layer], nsa_q_norm[layer], nsa_k_norm[layer],
                              cmp_pos[layer], cmp_w1[layer], cmp_b1[layer], cmp_w2[layer], cmp_b2[layer],
                              ml_conv_w[layer], ml_conv_b[layer], ml_gate_b[layer], w_branch[layer],
                              w_out[layer])
        j = layer // 2
        if layer % 2 == 0:
            x = dense_ffn(x, norm_ffn[layer], mod, ffn_wg[j], ffn_wu[j], ffn_wd[j])
        else:
            x = moe_ffn(x, norm_ffn[layer], mod, moe_router[j], moe_wg[j], moe_wu[j], moe_wd[j])
    return x
```

```python
import functools

import numpy as np
import jax
import jax.numpy as jnp
from jax import lax
from jax.experimental import pallas as pl
from jax.experimental.pallas import tpu as pltpu

F32 = jnp.float32
BF16 = jnp.bfloat16
I32 = jnp.int32
HIGHEST = lax.Precision.HIGHEST

EPS = 1e-6
NEG = -1e30
HEAD_DIM = 64
MIX_WIDTH = 512
NSA_HEADS = 8
NSA_KV_HEADS = 2
NSA_GROUP = NSA_HEADS // NSA_KV_HEADS
CMP_BLOCK = 32
CMP_STRIDE = 16
SEL_BLOCK = 64
SEL_TOPK = 16
WINDOW = 512
FORCE_BONUS = 1e4
ML_HEADS = 4
ML_HEAD_DIM = 128
ML_CHUNK = 64
ML_UNROLL = 2
CONV_WIDTH = 4
N_BRANCH = 3
N_EXPERTS = 8
TOP_K = 2
LANES = 128

C_MERGE = 0
C_NSA_Q = 3072
C_ML_Q = 3584
C_ML_K = 4096
C_ML_V = 4608
C_ML_O = 5120
C_SB_Q = 5632
C_SB_K = 6144
C_SB_V = 6656
C_NSA_KV = 7168
N_MAIN = 7936
S_NSA_GATE = 0
S_ML_I = 24
S_ML_F = 28
N_SMALL = 128

VMEM_LIMIT = 56 * 1024 * 1024


def _params(*sem):
    return pltpu.CompilerParams(dimension_semantics=sem, vmem_limit_bytes=VMEM_LIMIT)


def _iota(shape, dim):
    return lax.broadcasted_iota(I32, shape, dim)


def _split_dot(a32, b_bf16):
    hi = a32.astype(BF16)
    lo = (a32 - hi.astype(F32)).astype(BF16)
    return (jnp.dot(hi, b_bf16, preferred_element_type=F32)
            + jnp.dot(lo, b_bf16, preferred_element_type=F32))


def _split_dot_left(a_bf16, b32):
    hi = b32.astype(BF16)
    lo = (b32 - hi.astype(F32)).astype(BF16)
    return (jnp.dot(a_bf16, hi, preferred_element_type=F32)
            + jnp.dot(a_bf16, lo, preferred_element_type=F32))


def _dot_nt(a, b):
    return lax.dot_general(a, b, (((1,), (1,)), ((), ())), preferred_element_type=F32)


def _log_sigmoid(z):
    return jnp.minimum(z, 0.0) - jnp.log1p(jnp.exp(-jnp.abs(z)))


def _cast_kernel(x_ref, o_ref):
    o_ref[...] = x_ref[...].astype(o_ref.dtype)


def to_bf16(w, max_rows=512):
    cols = w.shape[-1]
    w2 = w.reshape(-1, cols)
    rows = w2.shape[0]
    tr = max(t for t in range(8, max_rows + 1, 8) if rows % t == 0)
    out = pl.pallas_call(
        _cast_kernel,
        out_shape=jax.ShapeDtypeStruct((rows, cols), BF16),
        grid=(rows // tr,),
        in_specs=[pl.BlockSpec((tr, cols), lambda i: (i, 0))],
        out_specs=pl.BlockSpec((tr, cols), lambda i: (i, 0)),
        compiler_params=_params("parallel"),
        name="to_bf16",
    )(w2)
    return out.reshape(w.shape)


def _adaln_kernel(c_ref, w_ref, b_ref, o_ref):
    c = c_ref[...]
    cond = c * jax.nn.sigmoid(c)
    o_ref[0] = jnp.dot(cond, w_ref[0], precision=HIGHEST, preferred_element_type=F32) + b_ref[0]


def adaln(c, ada_w, ada_b):
    depth, d, n = ada_w.shape
    b = c.shape[0]
    tn = 1536
    return pl.pallas_call(
        _adaln_kernel,
        out_shape=jax.ShapeDtypeStruct((depth, b, n), F32),
        grid=(depth, n // tn),
        in_specs=[pl.BlockSpec((b, d), lambda l, j: (0, 0)),
                  pl.BlockSpec((1, d, tn), lambda l, j: (l, 0, j)),
                  pl.BlockSpec((1, 1, tn), lambda l, j: (l, 0, j))],
        out_specs=pl.BlockSpec((1, b, tn), lambda l, j: (l, 0, j)),
        compiler_params=_params("parallel", "parallel"),
        name="adaln",
    )(c, ada_w, ada_b.reshape(depth, 1, n))


def _norm_mod(x, g, mod, shift_row, scale_row):
    ms = jnp.mean(x * x, axis=-1, keepdims=True)
    y = x * lax.rsqrt(ms + EPS) * g
    return y * (1.0 + mod[scale_row:scale_row + 1, :]) + mod[shift_row:shift_row + 1, :]


def _in_proj_kernel(x_ref, g_ref, mod_ref, wm_ref, ws_ref, y_ref, sm_ref, h_ref):
    @pl.when(pl.program_id(2) == 0)
    def _():
        h_ref[...] = _norm_mod(x_ref[0], g_ref[...], mod_ref[0], 0, 1).astype(BF16)
        sm_ref[0] = jnp.dot(h_ref[...], ws_ref[...], preferred_element_type=F32)

    y_ref[0] = jnp.dot(h_ref[...], wm_ref[...], preferred_element_type=F32).astype(y_ref.dtype)


def in_projection(x, g, mod, w_main, w_small, tm=512, n_tiles=2):
    b, s, d = x.shape
    tn = N_MAIN // n_tiles
    return pl.pallas_call(
        _in_proj_kernel,
        out_shape=(jax.ShapeDtypeStruct((b, s, N_MAIN), BF16), jax.ShapeDtypeStruct((b, s, N_SMALL), F32)),
        grid=(b, s // tm, n_tiles),
        in_specs=[pl.BlockSpec((1, tm, d), lambda i, j, n: (i, j, 0)),
                  pl.BlockSpec((1, d), lambda i, j, n: (0, 0)),
                  pl.BlockSpec((1, 6, d), lambda i, j, n: (i, 0, 0)),
                  pl.BlockSpec((d, tn), lambda i, j, n: (0, n)),
                  pl.BlockSpec((d, N_SMALL), lambda i, j, n: (0, 0))],
        out_specs=(pl.BlockSpec((1, tm, tn), lambda i, j, n: (i, j, n)),
                   pl.BlockSpec((1, tm, N_SMALL), lambda i, j, n: (i, j, 0))),
        scratch_shapes=[pltpu.VMEM((tm, d), BF16)],
        compiler_params=pltpu.CompilerParams(dimension_semantics=("parallel", "parallel", "arbitrary"),
                                             vmem_limit_bytes=VMEM_LIMIT,
                                             allow_input_fusion=[False, False, False, True, True]),
        name="in_projection",
    )(x, g.reshape(1, d), mod, w_main, w_small)


SB_EXP_DROP = 104.0
SB_EAGER = 2


def _sb_kernel(q_ref, k_ref, v_ref, o_ref, vt_ref, *, tq, n_str):
    TK, dh = LANES, HEAD_DIM
    n_sub = tq // TK
    W = 2 * tq
    SR = range(n_str)
    qi = pl.program_id(2)

    @pl.when(qi == 0)
    def _():
        for c in range(v_ref.shape[1] // tq):
            v_t = v_ref[0, c * tq:(c + 1) * tq, :].astype(F32).T.astype(BF16)
            for p in SR:
                vt_ref[p, c] = v_t[p * LANES:(p + 1) * LANES]

    q_t = (q_ref[0].astype(F32) * (dh ** -0.5)).T
    chan = _iota((2 * dh, tq), 0)
    q_cat = []
    for p in SR:
        q_p = q_t[p * LANES:(p + 1) * LANES]
        q_cat.append(jnp.concatenate([jnp.where(chan < dh, q_p, 0.0), jnp.where(chan < dh, 0.0, q_p)],
                                     axis=1).astype(BF16))
    later = (_iota((TK, TK), 0) < _iota((TK, TK), 1)).astype(BF16)
    suffix = jnp.concatenate([jnp.concatenate([later, later], axis=1), jnp.ones((8, 2 * TK), BF16)], axis=0)

    def steps(blocks, st):
        work = [(p, b) for p in SR for b in range(len(blocks))]
        lss, his, los = {}, {}, {}
        for p, b in work:
            j, keep = blocks[b]
            k0 = pl.multiple_of(j * tq, tq)
            z = jnp.dot(k_ref[0, pl.ds(k0, tq), p * LANES:(p + 1) * LANES], q_cat[p],
                        preferred_element_type=F32)
            drop = jnp.maximum(z, 0.0) + jnp.log(1.0 + jnp.exp(-jnp.abs(z)))
            lss[p, b] = z - drop
            if keep is not None:
                drop = jnp.where(keep, drop, 0.0)
            his[p, b] = drop.astype(BF16)
            los[p, b] = (drop - his[p, b].astype(F32)).astype(BF16)
        carry = [st[p][0] for p in SR]
        afters = {}
        for p, b in work:
            after = [None] * n_sub
            for sub in range(n_sub - 1, -1, -1):
                rows = slice(sub * TK, (sub + 1) * TK)
                res = jnp.dot(suffix, jnp.concatenate([his[p, b][rows], los[p, b][rows]], axis=0),
                              preferred_element_type=F32)
                after[sub] = res[0:TK] + carry[p]
                carry[p] = carry[p] + res[TK:TK + 1]
            afters[p, b] = jnp.concatenate(after, axis=0)
        acc = [st[p][1] for p in SR]
        for p, b in work:
            j, keep = blocks[b]
            a = jnp.exp(lss[p, b] - afters[p, b])
            if keep is not None:
                a = jnp.where(keep, a, 0.0)
            acc[p] = acc[p] + jnp.dot(vt_ref[p, j], a.astype(BF16), preferred_element_type=F32)
        return [(carry[p], acc[p]) for p in SR]

    def cond(c):
        least = jnp.min(c[1][0][0])
        for p in range(1, n_str):
            least = jnp.minimum(least, jnp.min(c[1][p][0]))
        return (c[0] >= 0) & (least < SB_EXP_DROP)

    def body(c):
        return c[0] - 1, steps([(c[0], None)], c[1])

    strict = _iota((tq, W), 0) < (_iota((tq, W), 1) & (tq - 1))
    eager = [(jnp.maximum(qi - d, 0), qi >= d) for d in range(1, SB_EAGER + 1)]
    st = steps([(qi, strict)] + eager, [(jnp.zeros((1, W), F32), jnp.zeros((2 * dh, W), F32)) for p in SR])
    _, st = lax.while_loop(cond, body, (qi - 1 - SB_EAGER, st))
    out_rows = []
    for p in SR:
        out_rows += [st[p][1][0:dh, 0:tq], st[p][1][dh:2 * dh, tq:W]]
    o_ref[0] = jnp.concatenate(out_rows, axis=0).T.astype(o_ref.dtype)


def sb_attention(y3, tq=256, n_str=2):
    b, s, _ = y3.shape
    wide = n_str * LANES
    qb, kb, vb = C_SB_Q // wide, C_SB_K // wide, C_SB_V // wide
    return pl.pallas_call(
        functools.partial(_sb_kernel, tq=tq, n_str=n_str),
        out_shape=jax.ShapeDtypeStruct((b, s, MIX_WIDTH), BF16),
        grid=(b, MIX_WIDTH // wide, s // tq),
        in_specs=[pl.BlockSpec((1, tq, wide), lambda i, p, j: (i, j, qb + p)),
                  pl.BlockSpec((1, s, wide), lambda i, p, j: (i, 0, kb + p)),
                  pl.BlockSpec((1, s, wide), lambda i, p, j: (i, 0, vb + p))],
        out_specs=pl.BlockSpec((1, tq, wide), lambda i, p, j: (i, j, p)),
        scratch_shapes=[pltpu.VMEM((n_str, s // tq, LANES, tq), BF16)],
        compiler_params=_params("parallel", "parallel", "arbitrary"),
        name="sb_attention",
    )(y3, y3, y3)


def _mlstm_kernel(q_ref, k_ref, v_ref, og_ref, sm_ref, gr_ref, cw_ref, cb_ref, gb_ref, out_ref,
                  ct_ref, n_ref, m_ref, xbuf_ref, qk_ref, *, ts):
    L, dh, H, W = ML_CHUNK, ML_HEAD_DIM, ML_HEADS, MIX_WIDTH
    halo = 8
    sblk = pl.program_id(1)

    @pl.when(sblk == 0)
    def _():
        ct_ref[...] = jnp.zeros_like(ct_ref)
        n_ref[...] = jnp.zeros_like(n_ref)
        m_ref[...] = jnp.zeros_like(m_ref)
        xbuf_ref[0:halo, :] = jnp.zeros((halo, 2 * W), F32)

    @pl.when(sblk > 0)
    def _():
        xbuf_ref[0:halo, :] = xbuf_ref[ts:ts + halo, :]

    xbuf_ref[halo:halo + ts, 0:W] = q_ref[0].astype(F32)
    xbuf_ref[halo:halo + ts, W:2 * W] = k_ref[0].astype(F32)
    conv = cb_ref[...]
    for j in range(CONV_WIDTH):
        off = halo - (CONV_WIDTH - 1) + j
        conv = conv + cw_ref[j:j + 1, :] * xbuf_ref[off:off + ts, :]
    act = conv * jax.nn.sigmoid(conv)
    qk_ref[:, 0:W] = (act[:, 0:W] * (dh ** -0.5)).astype(BF16)
    qk_ref[:, W:2 * W] = act[:, W:2 * W].astype(BF16)

    it0, it1 = _iota((L, L), 0), _iota((L, L), 1)
    causal = it0 >= it1
    tri_lo = causal.astype(BF16)
    tri_up = (it0 <= it1).astype(BF16)

    def chunks(it, carry):
        HR = range(H)
        UR = range(ML_UNROLL)
        UH = [(u, h) for u in UR for h in HR]
        cols = [slice(h * dh, (h + 1) * dh) for h in HR]
        rows = [pl.ds(pl.multiple_of((it * ML_UNROLL + u) * L, L), L) for u in UR]
        sm = [sm_ref[0, rows[u], :] for u in UR]
        gr = [gr_ref[0, it * ML_UNROLL + u] for u in UR]
        ig_col = {(u, h): sm[u][:, S_ML_I + h:S_ML_I + h + 1] + gb_ref[0, h] for u, h in UH}
        lf_col = {(u, h): _log_sigmoid(sm[u][:, S_ML_F + h:S_ML_F + h + 1] + gb_ref[1, h]) for u, h in UH}
        ig_row = {(u, h): gr[u][h:h + 1, :] + gb_ref[0, h] for u, h in UH}
        lf_row = {(u, h): _log_sigmoid(gr[u][H + h:H + h + 1, :] + gb_ref[1, h]) for u, h in UH}
        b_t = {k: _split_dot_left(tri_lo, jnp.broadcast_to(lf_col[k], (L, L))) for k in UH}
        b_s = {k: _split_dot(jnp.broadcast_to(lf_row[k], (L, L)), tri_up) for k in UH}
        qq = {(u, h): qk_ref[rows[u], cols[h]] for u, h in UH}
        kk = {(u, h): qk_ref[rows[u], W + h * dh:W + (h + 1) * dh] for u, h in UH}
        vv = {(u, h): v_ref[0, rows[u], cols[h]] for u, h in UH}
        qk = {k: _dot_nt(qq[k], kk[k]) for k in UH}
        kt = {k: kk[k].astype(F32).T.astype(BF16) for k in UH}
        dmat = {k: jnp.where(causal, b_t[k] - b_s[k] + ig_row[k], NEG) for k in UH}
        d_max = {k: jnp.max(dmat[k], axis=1, keepdims=True) for k in UH}
        b_col = {k: b_t[k][:, 0:1] for k in UH}
        b_last = {k: b_t[k][L - 1:L, 0:1] for k in UH}
        decay = {k: b_last[k] - b_col[k] + ig_col[k] for k in UH}
        decay_max = {k: jnp.max(decay[k], axis=0, keepdims=True) for k in UH}
        m_prev, m_new = {}, {}
        for u, h in UH:
            m_prev[u, h] = m_ref[h][:, 0:1] if u == 0 else m_new[u - 1, h]
            m_new[u, h] = jnp.maximum(b_last[u, h] + m_prev[u, h], decay_max[u, h])
        m_inter = {k: b_col[k] + m_prev[k] for k in UH}
        m_t = {k: jnp.maximum(m_inter[k], d_max[k]) for k in UH}
        w = {k: jnp.exp(dmat[k] - m_t[k]) * qk[k] for k in UH}
        inter = {k: jnp.exp(m_inter[k] - m_t[k]) for k in UH}
        w_v = {k: jnp.dot(w[k].astype(BF16), vv[k], preferred_element_type=F32) for k in UH}
        ws = {k: jnp.exp(decay[k] - m_new[k]) for k in UH}
        cscale = {k: jnp.exp(b_last[k] + m_prev[k] - m_new[k]) for k in UH}
        wv = {k: (ws[k] * vv[k].astype(F32)).astype(BF16) for k in UH}
        k_wv = {k: jnp.dot(kt[k], wv[k], preferred_element_type=F32) for k in UH}
        k_ws = {k: jnp.sum(ws[k] * kk[k].astype(F32), axis=0, keepdims=True) for k in UH}
        w_sum = {k: jnp.sum(w[k], axis=1, keepdims=True) for k in UH}
        ct = {h: ct_ref[h] for h in HR}
        nvec = {h: n_ref[h] for h in HR}
        for u, h in UH:
            k = (u, h)
            num = inter[k] * jnp.dot(qq[k], ct[h].astype(BF16), preferred_element_type=F32) + w_v[k]
            den = inter[k] * jnp.sum(qq[k].astype(F32) * nvec[h], axis=1, keepdims=True) + w_sum[k]
            hval = num / jnp.maximum(jnp.abs(den), jnp.exp(-m_t[k]))
            ct[h] = cscale[k] * ct[h] + k_wv[k]
            nvec[h] = cscale[k] * nvec[h] + k_ws[k]
            gate = jax.nn.sigmoid(og_ref[0, rows[u], cols[h]].astype(F32))
            out_ref[0, rows[u], cols[h]] = (gate * hval).astype(out_ref.dtype)
        for h in HR:
            ct_ref[h] = ct[h]
            n_ref[h] = nvec[h]
            m_ref[h] = jnp.broadcast_to(m_new[ML_UNROLL - 1, h], (1, LANES))
        return carry

    lax.fori_loop(0, ts // (L * ML_UNROLL), chunks, 0)


def mlstm(y3, small3, conv_w, conv_b, gate_b, ts=512):
    b, s, _ = y3.shape
    W, H, L = MIX_WIDTH, ML_HEADS, ML_CHUNK
    gr = small3[:, :, S_ML_I:S_ML_I + 2 * H].reshape(b, s // L, L, 2 * H).transpose(0, 1, 3, 2)
    cq, ck, cv, co = C_ML_Q // W, C_ML_K // W, C_ML_V // W, C_ML_O // W
    return pl.pallas_call(
        functools.partial(_mlstm_kernel, ts=ts),
        out_shape=jax.ShapeDtypeStruct((b, s, W), BF16),
        grid=(b, s // ts),
        in_specs=[pl.BlockSpec((1, ts, W), lambda i, j: (i, j, cq)),
                  pl.BlockSpec((1, ts, W), lambda i, j: (i, j, ck)),
                  pl.BlockSpec((1, ts, W), lambda i, j: (i, j, cv)),
                  pl.BlockSpec((1, ts, W), lambda i, j: (i, j, co)),
                  pl.BlockSpec((1, ts, N_SMALL), lambda i, j: (i, j, 0)),
                  pl.BlockSpec((1, ts // L, 2 * H, L), lambda i, j: (i, j, 0, 0)),
                  pl.BlockSpec((CONV_WIDTH, 2 * W), lambda i, j: (0, 0)),
                  pl.BlockSpec((1, 2 * W), lambda i, j: (0, 0)),
                  pl.BlockSpec(memory_space=pltpu.SMEM)],
        out_specs=pl.BlockSpec((1, ts, W), lambda i, j: (i, j, 0)),
        scratch_shapes=[pltpu.VMEM((H, ML_HEAD_DIM, ML_HEAD_DIM), F32),
                        pltpu.VMEM((H, 1, ML_HEAD_DIM), F32),
                        pltpu.VMEM((H, 1, LANES), F32),
                        pltpu.VMEM((ts + 8, 2 * W), F32),
                        pltpu.VMEM((ts, 2 * W), BF16)],
        compiler_params=_params("parallel", "arbitrary"),
        name="mlstm",
    )(y3, y3, y3, y3, small3, gr, conv_w, conv_b.reshape(1, 2 * W), gate_b)


def _gelu_tanh(x):
    return 0.5 * x * (1.0 + jnp.tanh(0.7978845608028654 * (x + 0.044715 * (x * x * x))))


def _compress_kernel(ra_ref, rb_ref, pos_ref, w1_ref, b1_ref, w2_ref, b2_ref, kn_ref, kc_ref, vc_ref):
    half = (CMP_BLOCK // 2) * HEAD_DIM
    for j, o_ref in enumerate((kc_ref, vc_ref)):
        xa = (ra_ref[j, 0, 0].astype(F32) + pos_ref[j, :, 0:half]).astype(BF16)
        xb = (rb_ref[j, 0, 0].astype(F32) + pos_ref[j, :, half:2 * half]).astype(BF16)
        hid = (jnp.dot(xa, w1_ref[j, 0:half, :], preferred_element_type=F32)
               + jnp.dot(xb, w1_ref[j, half:2 * half, :], preferred_element_type=F32) + b1_ref[j])
        out = jnp.dot(_gelu_tanh(hid).astype(BF16), w2_ref[j], preferred_element_type=F32) + b2_ref[j]
        if j == 0:
            out = out * lax.rsqrt(jnp.mean(out * out, axis=-1, keepdims=True) + EPS) * kn_ref[...]
        o_ref[0, 0] = out


def nsa_compress(y3, cmp_pos, cmp_w1, cmp_b1, cmp_w2, cmp_b2, k_norm0):
    b, s, _ = y3.shape
    G, dh = NSA_KV_HEADS, HEAD_DIM
    nr = s // CMP_STRIDE
    wide = CMP_STRIDE * dh
    kv = y3[:, :, C_NSA_KV:C_NSA_KV + 2 * G * dh].reshape(b, s, 2, G, dh)
    ra = kv.transpose(2, 0, 3, 1, 4).reshape(2, b, G, nr, wide)
    rb = jnp.concatenate([ra[:, :, :, 1:], jnp.zeros((2, b, G, 1, wide), ra.dtype)], axis=3)
    hidden = cmp_w1.shape[-1]
    out = jax.ShapeDtypeStruct((b, G, nr, dh), F32)
    blk = pl.BlockSpec((2, 1, 1, nr, wide), lambda i, g: (0, i, g, 0, 0))
    oblk = pl.BlockSpec((1, 1, nr, dh), lambda i, g: (i, g, 0, 0))

    def full(shape):
        return pl.BlockSpec(shape, lambda i, g: (0,) * len(shape))

    return pl.pallas_call(
        _compress_kernel,
        out_shape=(out, out),
        grid=(b, G),
        in_specs=[blk, blk, full((2, 1, 2 * wide)), full((2, 2 * wide, hidden)), full((2, 1, hidden)),
                  full((2, hidden, dh)), full((2, 1, dh)), full((1, dh))],
        out_specs=(oblk, oblk),
        compiler_params=_params("parallel", "parallel"),
        name="nsa_compress",
    )(ra, rb, cmp_pos.reshape(2, 1, 2 * wide), cmp_w1.astype(BF16), cmp_b1.reshape(2, 1, hidden),
      cmp_w2.astype(BF16), cmp_b2.reshape(2, 1, dh), k_norm0.reshape(1, dh))


NSA_QB = 128
NSA_QBLK = 2
NSA_KS = 512
NSA_CK = 256
NSA_VR = 80
A_FEAT, A_PEN, A_BIAS = 0, 64, 128
A_DUMMY = A_BIAS + 4


def _head_rms(x, gain):
    w = x.shape[1]
    same_head = (_iota((w, w), 0) // HEAD_DIM == _iota((w, w), 1) // HEAD_DIM).astype(BF16)
    ss = _split_dot(x * x, same_head)
    return x * lax.rsqrt(ss * (1.0 / HEAD_DIM) + EPS) * gain


def _nsa_kernel(q_ref, gl_ref, kca_ref, vcT_ref, ks_ref, kw_ref, vs_ref, vw_ref, kconst_ref, gq_ref, gk_ref,
                     o_ref, qt_ref, ksa_ref, kwa_ref, vsa_ref, vwa_ref, s_ref, *, n_sel, top):
    QB, R, dh, CK, VR, KS = NSA_QB, NSA_GROUP, HEAD_DIM, NSA_CK, NSA_VR, NSA_KS
    G = NSA_KV_HEADS
    GR = range(G)
    UR = range(NSA_QBLK)
    UG = [(u, g) for u in UR for g in GR]
    HQ = R * QB
    NPAD = WINDOW // QB
    SUB = KS // QB
    step = pl.program_id(1)
    qi = [step * NSA_QBLK + u for u in UR]
    q0 = [qi[u] * QB for u in UR]
    nkb = vsa_ref.shape[1]

    @pl.when(step == 0)
    def _():
        pad_keys = jnp.where(_iota((WINDOW, CK), 1) == A_DUMMY, 1.0, 0.0).astype(BF16)
        ones_rows = jnp.where(_iota((nkb + NPAD, VR - dh, QB), 1) == 0, 1.0, 0.0).astype(BF16)
        for g in GR:
            heads = slice(g * dh, (g + 1) * dh)
            ksa_ref[g] = kconst_ref[...]
            kwa_ref[g, 0:WINDOW, :] = pad_keys
            kwa_ref[g, WINDOW:, :] = kconst_ref[...]
            kwa_ref[g, WINDOW:, A_PEN:A_PEN + 64] = jnp.zeros((kwa_ref.shape[1] - WINDOW, 64), BF16)
            vwa_ref[g, 0:NPAD, 0:dh, :] = jnp.zeros((NPAD, dh, QB), BF16)
            vsa_ref[g, :, dh:VR, :] = ones_rows[0:nkb]
            vwa_ref[g, :, dh:VR, :] = ones_rows
        for c in range(ks_ref.shape[1] // KS):
            keys = slice(c * KS, (c + 1) * KS)
            ks_n = _head_rms(ks_ref[0, keys, :].astype(F32), gk_ref[0]).astype(BF16)
            kw_n = _head_rms(kw_ref[0, keys, :].astype(F32), gk_ref[1]).astype(BF16)
            for g in GR:
                heads = slice(g * dh, (g + 1) * dh)
                ksa_ref[g, keys, A_FEAT:A_FEAT + dh] = ks_n[:, heads]
                kwa_ref[g, WINDOW + c * KS:WINDOW + (c + 1) * KS, A_FEAT:A_FEAT + dh] = kw_n[:, heads]
        for c in range(nkb):
            keys = slice(c * QB, (c + 1) * QB)
            vs_t = vs_ref[0, keys, :].astype(F32).T.astype(BF16)
            vw_t = vw_ref[0, keys, :].astype(F32).T.astype(BF16)
            for g in GR:
                heads = slice(g * dh, (g + 1) * dh)
                vsa_ref[g, c, 0:dh, :] = vs_t[heads]
                vwa_ref[g, NPAD + c, 0:dh, :] = vw_t[heads]
        qt_ref[:, :, A_BIAS + 16:CK, :] = jnp.zeros((NSA_QBLK, G, CK - A_BIAS - 16, HQ), BF16)

    q_n = _head_rms(q_ref[0].astype(F32), gq_ref[...]).astype(BF16)
    q_rows = (q_n.astype(F32) * (dh ** -0.5)).T
    lane = _iota((16, HQ), 1)
    rowi = _iota((16, HQ), 0)
    for u, g in UG:
        t_q = q0[u] + (lane & (QB - 1))
        t_hi = ((t_q >> 6) << 6).astype(F32)
        t_lo = (t_q & 63).astype(F32)
        qT = jnp.concatenate([q_rows[(g * R + r) * dh:(g * R + r + 1) * dh, u * QB:(u + 1) * QB]
                              for r in range(R)], axis=1).astype(BF16)
        qt_ref[u, g, A_FEAT:A_FEAT + dh, :] = qT
        qt_ref[u, g, A_PEN:A_PEN + dh, :] = qT
        slope = jnp.exp2(-(g * R + (lane >> 7) + 1).astype(F32))
        bias_rows = jnp.where(rowi < 2, slope,
                              jnp.where(rowi == 2, -slope * t_hi,
                                        jnp.where(rowi == 3, -slope * t_lo,
                                                  jnp.where(rowi == A_DUMMY - A_BIAS, NEG, 0.0))))
        qt_ref[u, g, A_BIAS:A_BIAS + 16, :] = bias_rows.astype(BF16)
    k_loc = _iota((QB, HQ), 0)
    q_loc = _iota((QB, HQ), 1) & (QB - 1)

    def pv(v_ref_, g, kb0, pr):
        out = None
        for i in range(pr.shape[0] // QB):
            term = jnp.dot(v_ref_[g, kb0 + i], pr[i * QB:(i + 1) * QB], preferred_element_type=F32)
            out = term if out is None else out + term
        return out

    n_cmp = kca_ref.shape[2]
    cmp_end = _iota((n_cmp, HQ), 0) * CMP_STRIDE + (CMP_BLOCK - 1)
    valid = [cmp_end <= q0[u] + (_iota((n_cmp, HQ), 1) & (QB - 1)) for u in UR]
    sc = {(u, g): jnp.where(valid[u], jnp.dot(kca_ref[0, g], qt_ref[u, g], preferred_element_type=F32), NEG)
          for u, g in UG}
    mx = {k: jnp.max(sc[k], axis=0, keepdims=True) for k in UG}
    e = {k: jnp.exp(sc[k] - mx[k]) for k in UG}
    inv = {k: jnp.where(mx[k] > 0.5 * NEG, 1.0 / jnp.sum(e[k], axis=0, keepdims=True), 0.0) for k in UG}
    p = {k: e[k] * inv[k] for k in UG}
    o_cmp = {(u, g): jnp.dot(vcT_ref[0, g], p[u, g].astype(BF16), preferred_element_type=F32)
             for u, g in UG}

    sw = {(u, g): jnp.dot(kwa_ref[g, pl.ds(pl.multiple_of(q0[u], QB), WINDOW + QB), :], qt_ref[u, g],
                          preferred_element_type=F32) for u, g in UG}
    sw = {k: jnp.concatenate([jnp.where(k_loc > q_loc, sw[k][0:QB], NEG), sw[k][QB:WINDOW],
                              jnp.where(k_loc <= q_loc, sw[k][WINDOW:WINDOW + QB], NEG)], axis=0) for k in UG}
    pw = {k: jnp.exp(sw[k] - jnp.max(sw[k], axis=0, keepdims=True)).astype(BF16) for k in UG}
    acc_w = {(u, g): pv(vwa_ref, g, qi[u], pw[u, g]) for u, g in UG}
    o_win = {k: acc_w[k][0:dh] / acc_w[k][dh:dh + 1] for k in UG}

    c0 = _iota((n_sel, n_cmp), 1) * CMP_STRIDE
    s0 = _iota((n_sel, n_cmp), 0) * SEL_BLOCK
    overlap_t = ((c0 < s0 + SEL_BLOCK) & (c0 + CMP_BLOCK > s0)).astype(BF16)
    j_idx = _iota((n_sel, QB), 0)
    tq = [q0[u] + _iota((n_sel, QB), 1) for u in UR]
    forced = [(j_idx == 0) | (j_idx == (tq[u] >> 6)) | (j_idx == (tq[u] >> 6) - 1) for u in UR]
    causal_blk = [j_idx * SEL_BLOCK <= tq[u] for u in UR]
    p_grp = {k: p[k][:, 0:QB] + p[k][:, QB:2 * QB] + p[k][:, 2 * QB:3 * QB] + p[k][:, 3 * QB:4 * QB] for k in UG}
    imp = {(u, g): jnp.where(causal_blk[u], _split_dot_left(overlap_t, p_grp[u, g])
                             + jnp.where(forced[u], FORCE_BONUS, 0.0), -1.0) for u, g in UG}
    sel = {k: jnp.zeros((n_sel, QB), F32) for k in UG}
    for _ in range(top):
        for k in UG:
            top_v = jnp.max(imp[k], axis=0, keepdims=True)
            first = jnp.min(jnp.where(imp[k] == top_v, j_idx, n_sel), axis=0, keepdims=True)
            pick = j_idx == first
            sel[k] = jnp.where(pick, 1.0, sel[k])
            imp[k] = jnp.where(pick, -3e38, imp[k])
    for u, g in UG:
        pen = jnp.where((sel[u, g] > 0.5) & causal_blk[u], 0.0, NEG)
        if n_sel < 64:
            pen = jnp.concatenate([pen, jnp.zeros((64 - n_sel, QB), F32)], axis=0)
        qt_ref[u, g, A_PEN:A_PEN + 64, :] = jnp.concatenate([pen] * R, axis=1).astype(BF16)

    def score(j):
        rows = pl.ds(pl.multiple_of(j * KS, KS), KS)
        return [jnp.dot(ksa_ref[g, rows, :], qt_ref[u, g], preferred_element_type=F32) for u, g in UG]

    def absorb(s, j, st):
        m_new = [jnp.maximum(st[n][0], jnp.max(s[n], axis=0, keepdims=True)) for n in range(len(UG))]
        pr = [jnp.exp(s[n] - m_new[n]).astype(BF16) for n in range(len(UG))]
        return [(m_new[n], jnp.exp(st[n][0] - m_new[n]) * st[n][1] + pv(vsa_ref, UG[n][1], j * SUB, pr[n]))
                for n in range(len(UG))]

    n_full = qi[0] // SUB
    init = [(jnp.full((1, HQ), NEG, F32), jnp.zeros((VR, HQ), F32)) for _ in UG]
    st = lax.fori_loop(0, n_full, lambda j, st_: absorb(score(j), j, st_), init)
    s_last = score(n_full)
    for n, (u, g) in enumerate(UG):
        diag = pl.ds(pl.multiple_of(q0[u] - n_full * KS, QB), QB)
        s_ref[u, g] = s_last[n]
        s_ref[u, g, diag, :] = jnp.where(k_loc <= q_loc, s_ref[u, g, diag, :], NEG)
    st = absorb([s_ref[u, g] for u, g in UG], n_full, st)
    o_sel = {UG[n]: st[n][1][0:dh] / st[n][1][dh:dh + 1] for n in range(len(UG))}

    gl_t = gl_ref[0].T
    for u in UR:
        rows_out = []
        for g in GR:
            gate = [jax.nn.sigmoid(jnp.concatenate(
                [gl_t[(g * R + r) * N_BRANCH + br:(g * R + r) * N_BRANCH + br + 1, u * QB:(u + 1) * QB]
                 for r in range(R)], axis=1)) for br in range(N_BRANCH)]
            o_t = gate[0] * o_cmp[u, g] + gate[1] * o_sel[u, g] + gate[2] * o_win[u, g]
            rows_out += [o_t[:, r * QB:(r + 1) * QB] for r in range(R)]
        o_ref[0, u * QB:(u + 1) * QB, :] = jnp.concatenate(rows_out, axis=0).T.astype(o_ref.dtype)


def nsa_attention(y3, small3, q_norm, k_norm, kc, vc):
    b, s, _ = y3.shape
    G, R, dh, QB, CK, VR = NSA_KV_HEADS, NSA_GROUP, HEAD_DIM, NSA_QB, NSA_CK, NSA_VR
    assert (NSA_KS // QB) % NSA_QBLK == 0, "a step's query blocks must share their span of NSA_KS keys"
    HQ = R * QB
    nq = s // QB
    n_sel = s // SEL_BLOCK
    assert n_sel <= 64, "selection one-hot columns hold at most 64 blocks"
    top = min(SEL_TOPK, n_sel)
    n_cmp = kc.shape[2]

    def pos_cols(pos):
        return np.stack([pos // 64 * 64, pos % 64, np.ones_like(pos), np.ones_like(pos)], axis=1)

    vc_t = vc.transpose(0, 1, 3, 2).astype(BF16)

    pos = np.arange(s)
    kconst = np.zeros((s, CK), np.float32)
    kconst[pos, A_PEN + pos // SEL_BLOCK] = 1.0
    kconst[:, A_BIAS:A_BIAS + 4] = pos_cols(pos)
    kconst = jnp.asarray(kconst, BF16)

    kc_hi = kc.astype(BF16)
    kc_lo = (kc - kc_hi.astype(F32)).astype(BF16)
    cend = np.arange(n_cmp) * CMP_STRIDE + (CMP_BLOCK - 1)
    cbias = np.zeros((n_cmp, CK - 2 * dh), np.float32)
    cbias[:, 0:4] = pos_cols(cend)
    kc_aug = jnp.concatenate([kc_hi, kc_lo, jnp.broadcast_to(jnp.asarray(cbias, BF16), (b, G, n_cmp, CK - 2 * dh))],
                             axis=-1)

    qw = G * R * dh
    kvb = C_NSA_KV // LANES
    gq = jnp.tile(q_norm, G * R).reshape(1, qw)
    gk = jnp.stack([jnp.tile(k_norm[1], G), jnp.tile(k_norm[2], G)]).reshape(2, 1, LANES)

    def kv_spec(blk):
        return pl.BlockSpec((1, s, LANES), lambda i, j: (i, 0, kvb + blk))

    return pl.pallas_call(
        functools.partial(_nsa_kernel, n_sel=n_sel, top=top),
        out_shape=jax.ShapeDtypeStruct((b, s, MIX_WIDTH), BF16),
        grid=(b, nq // NSA_QBLK),
        in_specs=[pl.BlockSpec((1, NSA_QBLK * QB, qw), lambda i, j: (i, j, C_NSA_Q // qw)),
                  pl.BlockSpec((1, NSA_QBLK * QB, N_SMALL), lambda i, j: (i, j, 0)),
                  pl.BlockSpec((1, G, n_cmp, CK), lambda i, j: (i, 0, 0, 0)),
                  pl.BlockSpec((1, G, dh, n_cmp), lambda i, j: (i, 0, 0, 0)),
                  kv_spec(2), kv_spec(4), kv_spec(3), kv_spec(5),
                  pl.BlockSpec((s, CK), lambda i, j: (0, 0)),
                  pl.BlockSpec((1, qw), lambda i, j: (0, 0)),
                  pl.BlockSpec((2, 1, LANES), lambda i, j: (0, 0, 0))],
        out_specs=pl.BlockSpec((1, NSA_QBLK * QB, qw), lambda i, j: (i, j, 0)),
        scratch_shapes=[pltpu.VMEM((NSA_QBLK, G, CK, HQ), BF16),
                        pltpu.VMEM((G, s, CK), BF16), pltpu.VMEM((G, s + WINDOW, CK), BF16),
                        pltpu.VMEM((G, nq, VR, QB), BF16), pltpu.VMEM((G, nq + WINDOW // QB, VR, QB), BF16),
                        pltpu.VMEM((NSA_QBLK, G, NSA_KS, HQ), F32)],
        compiler_params=_params("parallel", "arbitrary"),
        name="nsa_attention",
    )(y3, small3, kc_aug, vc_t, y3, y3, y3, y3, kconst, gq, gk)


def _merge_kernel(on_ref, os_ref, om_ref, g0_ref, g1_ref, g2_ref, wb_ref, wo_ref, x_ref, mod_ref, o_ref):
    merged = None
    for i, (o_r, g_r) in enumerate(((on_ref, g0_ref), (os_ref, g1_ref), (om_ref, g2_ref))):
        br = jnp.dot(o_r[0], wb_ref[i], preferred_element_type=F32)
        term = jax.nn.sigmoid(g_r[0].astype(F32)) * br
        merged = term if merged is None else merged + term
    out = jnp.dot(merged.astype(BF16), wo_ref[...], preferred_element_type=F32)
    o_ref[0] = x_ref[0] + mod_ref[0, 2:3, :] * out


def merge_project(o_nsa, o_sb, o_ml, y3, w_branch, w_out, x, mod, tm=512):
    b, s, d = x.shape
    W = MIX_WIDTH
    ospec = pl.BlockSpec((1, tm, W), lambda i, j: (i, j, 0))
    xspec = pl.BlockSpec((1, tm, d), lambda i, j: (i, j, 0))
    gspecs = [pl.BlockSpec((1, tm, d), functools.partial(lambda i, j, c: (i, j, c), c=C_MERGE // d + c))
              for c in range(N_BRANCH)]
    return pl.pallas_call(
        _merge_kernel,
        out_shape=jax.ShapeDtypeStruct((b, s, d), F32),
        grid=(b, s // tm),
        in_specs=[ospec, ospec, ospec] + gspecs + [
            pl.BlockSpec((N_BRANCH, W, d), lambda i, j: (0, 0, 0)),
            pl.BlockSpec((d, d), lambda i, j: (0, 0)),
            xspec,
            pl.BlockSpec((1, 6, d), lambda i, j: (i, 0, 0))],
        out_specs=xspec,
        compiler_params=_params("parallel", "parallel"),
        name="merge_project",
    )(o_nsa, o_sb, o_ml, y3, y3, y3, w_branch.astype(BF16), w_out.astype(BF16), x, mod)


def _ffn_kernel(x_ref, g_ref, mod_ref, wg_ref, wu_ref, wd_ref, o_ref, h_ref, acc_ref):
    f = pl.program_id(2)

    @pl.when(f == 0)
    def _():
        h_ref[...] = _norm_mod(x_ref[0], g_ref[...], mod_ref[0], 3, 4).astype(BF16)
        acc_ref[...] = jnp.zeros_like(acc_ref)

    h = h_ref[...]
    a = jnp.dot(h, wg_ref[...], preferred_element_type=F32)
    u = jnp.dot(h, wu_ref[...], preferred_element_type=F32)
    act = (a * jax.nn.sigmoid(a) * u).astype(BF16)
    acc_ref[...] += jnp.dot(act, wd_ref[...], preferred_element_type=F32)

    @pl.when(f == pl.num_programs(2) - 1)
    def _():
        o_ref[0] = x_ref[0] + mod_ref[0, 5:6, :] * acc_ref[...]


def dense_ffn(x, g, mod, wg, wu, wd, tm=512, n_ftiles=2):
    b, s, d = x.shape
    ff = wg.shape[1]
    tf = -(-ff // (n_ftiles * LANES)) * LANES
    pad = n_ftiles * tf - ff
    wg = jnp.pad(to_bf16(wg), ((0, 0), (0, pad)))
    wu = jnp.pad(to_bf16(wu), ((0, 0), (0, pad)))
    wd = jnp.pad(to_bf16(wd), ((0, pad), (0, 0)))
    xspec = pl.BlockSpec((1, tm, d), lambda i, j, f: (i, j, 0))
    return pl.pallas_call(
        _ffn_kernel,
        out_shape=jax.ShapeDtypeStruct((b, s, d), F32),
        grid=(b, s // tm, n_ftiles),
        in_specs=[xspec,
                  pl.BlockSpec((1, d), lambda i, j, f: (0, 0)),
                  pl.BlockSpec((1, 6, d), lambda i, j, f: (i, 0, 0)),
                  pl.BlockSpec((d, tf), lambda i, j, f: (0, f)),
                  pl.BlockSpec((d, tf), lambda i, j, f: (0, f)),
                  pl.BlockSpec((tf, d), lambda i, j, f: (f, 0))],
        out_specs=xspec,
        scratch_shapes=[pltpu.VMEM((tm, d), BF16), pltpu.VMEM((tm, d), F32)],
        compiler_params=_params("parallel", "parallel", "arbitrary"),
        name="dense_ffn",
    )(x, g.reshape(1, d), mod, wg, wu, wd)


def _router_kernel(x_ref, g_ref, mod_ref, wr_ref, h_ref, e_ref, p_ref):
    h = _norm_mod(x_ref[0], g_ref[...], mod_ref[0], 3, 4)
    h_ref[...] = h
    lane = _iota((1, LANES), 1)
    real = lane < N_EXPERTS
    logits = jnp.where(real, jnp.dot(h, wr_ref[...], precision=HIGHEST, preferred_element_type=F32), NEG)
    e = jnp.exp(logits - jnp.max(logits, axis=1, keepdims=True))
    p = jnp.where(real, e / jnp.sum(e, axis=1, keepdims=True), -1.0)
    p1 = jnp.max(p, axis=1, keepdims=True)
    i1 = jnp.min(jnp.where(p == p1, lane, LANES), axis=1, keepdims=True)
    rest = jnp.where(lane == i1, -1.0, p)
    p2 = jnp.max(rest, axis=1, keepdims=True)
    i2 = jnp.min(jnp.where(rest == p2, lane, LANES), axis=1, keepdims=True)
    tot = p1 + p2
    e_ref[...] = jnp.where(lane == 0, i1, jnp.where(lane == 1, i2, 0))[:, 0:N_EXPERTS]
    p_ref[...] = jnp.where(lane == 0, p1 / tot, jnp.where(lane == 1, p2 / tot, 0.0))[:, 0:N_EXPERTS]


def moe_router(x, g, mod, w_router, tm=512):
    b, s, d = x.shape
    t = b * s
    spb = s // tm
    wr = jnp.pad(w_router, ((0, 0), (0, LANES - N_EXPERTS)))
    return pl.pallas_call(
        _router_kernel,
        out_shape=(jax.ShapeDtypeStruct((t, d), F32),
                   jax.ShapeDtypeStruct((t, N_EXPERTS), I32),
                   jax.ShapeDtypeStruct((t, N_EXPERTS), F32)),
        grid=(b, spb),
        in_specs=[pl.BlockSpec((1, tm, d), lambda i, j: (i, j, 0)),
                  pl.BlockSpec((1, d), lambda i, j: (0, 0)),
                  pl.BlockSpec((1, 6, d), lambda i, j: (i, 0, 0)),
                  pl.BlockSpec((d, LANES), lambda i, j: (0, 0))],
        out_specs=(pl.BlockSpec((tm, d), lambda i, j: (i * spb + j, 0)),
                   pl.BlockSpec((tm, N_EXPERTS), lambda i, j: (i * spb + j, 0)),
                   pl.BlockSpec((tm, N_EXPERTS), lambda i, j: (i * spb + j, 0))),
        compiler_params=_params("parallel", "parallel"),
        name="moe_router",
    )(x, g.reshape(1, d), mod, wr)


def moe_ffn(x, g, mod, w_router, wg, wu, wd, tb=512):
    b, s, d = x.shape
    t = b * s
    a = t * TOP_K
    h, top_e, top_p = moe_router(x, g, mod, w_router)
    e_flat = top_e[:, 0:TOP_K].reshape(a)
    onehot = (e_flat[:, None] == jnp.arange(N_EXPERTS, dtype=I32)[None, :]).astype(I32)
    csum = jnp.cumsum(onehot, axis=0)
    rank = jnp.sum(onehot * csum, axis=1) - 1
    counts = csum[-1]
    padded = (counts + tb - 1) // tb * tb
    pad_ends = jnp.cumsum(padded)
    pad_starts = pad_ends - padded
    dest = (jnp.sum(onehot * pad_starts[None, :], axis=1) + rank).astype(I32)
    n_rows = (a // tb + N_EXPERTS + 1) * tb
    n_blk = n_rows // tb
    blk_expert = jnp.minimum(
        jnp.searchsorted(pad_ends, jnp.arange(n_blk, dtype=I32) * tb, side="right"), N_EXPERTS - 1).astype(I32)
    n_used = (pad_ends[-1:] // tb).astype(I32)
    pad_lo = jnp.concatenate([pad_starts + counts, pad_ends[-1:]]).astype(I32)
    pad_hi = jnp.concatenate([pad_ends, jnp.full((1,), n_rows, I32)]).astype(I32)
    slot_assign = moe_invert(dest, pad_lo, pad_hi, n_rows, tb)
    y2 = moe_experts(h, slot_assign, blk_expert, n_used, to_bf16(wg), to_bf16(wu), to_bf16(wd), tb)
    return moe_mix(y2, x, top_p, mod)


def _invert_kernel(dest_ref, lo_ref, hi_ref, sa_ref, *, n_assign, tb, n_chunks):
    phase = pl.program_id(0)
    chunk = pl.program_id(1)

    @pl.when((phase == 0) & (chunk < lo_ref.shape[0]))
    def _():
        def fill(p, c):
            sa_ref[p] = n_assign + (p & (2 * tb - 1))
            return c

        lax.fori_loop(lo_ref[chunk], hi_ref[chunk], fill, 0)

    @pl.when(phase == 1)
    def _():
        per = n_assign // n_chunks

        def put(j, c):
            a = chunk * per + j
            sa_ref[dest_ref[a]] = a
            return c

        lax.fori_loop(0, per, put, 0, unroll=8)


def moe_invert(dest, pad_lo, pad_hi, n_rows, tb, n_chunks=16):
    n_assign = dest.shape[0]
    assert tb & (tb - 1) == 0 and n_assign % n_chunks == 0 and pad_lo.shape[0] <= n_chunks
    smem = pl.BlockSpec(memory_space=pltpu.SMEM)
    return pl.pallas_call(
        functools.partial(_invert_kernel, n_assign=n_assign, tb=tb, n_chunks=n_chunks),
        out_shape=jax.ShapeDtypeStruct((n_rows,), I32),
        grid=(2, n_chunks),
        in_specs=[smem, smem, smem],
        out_specs=smem,
        compiler_params=pltpu.CompilerParams(dimension_semantics=("arbitrary", "arbitrary")),
        name="moe_invert",
    )(dest, pad_lo, pad_hi)


def _expert_kernel(be_ref, nu_ref, sa_ref, h_hbm, wg_ref, wu_ref, wd_ref, y_hbm,
                   xin_ref, yout_ref, xb_ref, acc_ref, sem_in, sem_out, *, tb, n_tok, n_f):
    i = pl.program_id(0)
    f = pl.program_id(1)
    n_used = nu_ref[0]
    n_assign = n_tok * TOP_K
    rows_f = tb // n_f
    active = i <= n_used

    def gather_row(blk, r):
        a = sa_ref[blk * tb + r]
        tok = jnp.where(a < n_assign, a >> 1, 0)
        return pltpu.make_async_copy(h_hbm.at[tok], xin_ref.at[blk % 2, r], sem_in.at[blk % 2])

    def scatter_row(blk, r):
        a = jnp.where(blk >= 0, sa_ref[jnp.maximum(blk, 0) * tb + r], n_assign + tb + r)
        row = jnp.where(a < n_assign, (a & 1) * n_tok + (a >> 1), a)
        return pltpu.make_async_copy(yout_ref.at[(blk + 2) % 2, r], y_hbm.at[row], sem_out.at[(blk + 2) % 2])

    def for_rows(fn):
        def body(r, c):
            fn(r)
            return c
        lax.fori_loop(0, tb, body, 0, unroll=8)

    def wait_rows(row_copy):
        for_rows(lambda r: row_copy.wait())

    @pl.when(f == 0)
    def _():
        @pl.when(i == 0)
        def _():
            yout_ref[1] = jnp.zeros((tb, yout_ref.shape[2]), F32)
            for_rows(lambda r: gather_row(0, r).start())

            def clear_row(r):
                return pltpu.make_async_copy(yout_ref.at[1, r], y_hbm.at[n_assign + r], sem_out.at[0])

            for_rows(lambda r: clear_row(r).start())
            wait_rows(clear_row(0))

        @pl.when((i == 0) | (i - 1 <= n_used))
        def _():
            wait_rows(gather_row(i, 0))

        @pl.when(active)
        def _():
            xb_ref[...] = xin_ref[i % 2].astype(BF16)
            acc_ref[...] = jnp.zeros_like(acc_ref)

    @pl.when(active)
    def _():
        for r in range(rows_f):
            gather_row(i + 1, f * rows_f + r).start()
            scatter_row(i - 1, f * rows_f + r).start()
        xb = xb_ref[...]
        a = jnp.dot(xb, wg_ref[0], preferred_element_type=F32)
        u = jnp.dot(xb, wu_ref[0], preferred_element_type=F32)
        act = (a * jax.nn.sigmoid(a) * u).astype(BF16)
        acc_ref[...] += jnp.dot(act, wd_ref[0], preferred_element_type=F32)

    @pl.when(f == n_f - 1)
    def _():
        @pl.when((i >= 1) & (i - 1 <= n_used))
        def _():
            wait_rows(scatter_row(i - 2, 0))

        @pl.when(active)
        def _():
            yout_ref[i % 2] = acc_ref[...]


def moe_experts(h, slot_assign, blk_expert, n_used, wg, wu, wd, tb, tf=1792):
    n_tok, d = h.shape
    p = slot_assign.shape[0]
    ff = wg.shape[2]
    return pl.pallas_call(
        functools.partial(_expert_kernel, tb=tb, n_tok=n_tok, n_f=ff // tf),
        out_shape=jax.ShapeDtypeStruct((n_tok * TOP_K + 2 * tb, d), F32),
        grid_spec=pltpu.PrefetchScalarGridSpec(
            num_scalar_prefetch=3,
            grid=(p // tb, ff // tf),
            in_specs=[pl.BlockSpec(memory_space=pl.ANY),
                      pl.BlockSpec((1, d, tf), lambda i, f, be, nu, sa: (be[i], 0, f)),
                      pl.BlockSpec((1, d, tf), lambda i, f, be, nu, sa: (be[i], 0, f)),
                      pl.BlockSpec((1, tf, d), lambda i, f, be, nu, sa: (be[i], f, 0))],
            out_specs=pl.BlockSpec(memory_space=pl.ANY),
            scratch_shapes=[pltpu.VMEM((2, tb, d), F32), pltpu.VMEM((2, tb, d), F32),
                            pltpu.VMEM((tb, d), BF16), pltpu.VMEM((tb, d), F32),
                            pltpu.SemaphoreType.DMA((2,)), pltpu.SemaphoreType.DMA((2,))]),
        compiler_params=pltpu.CompilerParams(dimension_semantics=("arbitrary", "arbitrary"),
                                             vmem_limit_bytes=VMEM_LIMIT, has_side_effects=True),
        name="moe_experts",
    )(blk_expert, n_used, slot_assign, h, wg, wu, wd)


def _mix_kernel(y0_ref, y1_ref, x_ref, p_ref, mod_ref, o_ref):
    w = p_ref[...]
    f = w[:, 0:1] * y0_ref[...] + w[:, 1:2] * y1_ref[...]
    o_ref[0] = x_ref[0] + mod_ref[0, 5:6, :] * f


def moe_mix(y2, x, top_p, mod, td=512):
    b, s, d = x.shape
    spb = s // td
    nt = b * spb
    return pl.pallas_call(
        _mix_kernel,
        out_shape=jax.ShapeDtypeStruct((b, s, d), F32),
        grid=(b, spb),
        in_specs=[pl.BlockSpec((td, d), lambda i, j: (i * spb + j, 0)),
                  pl.BlockSpec((td, d), lambda i, j: (nt + i * spb + j, 0)),
                  pl.BlockSpec((1, td, d), lambda i, j: (i, j, 0)),
                  pl.BlockSpec((td, N_EXPERTS), lambda i, j: (i * spb + j, 0)),
                  pl.BlockSpec((1, 6, d), lambda i, j: (i, 0, 0))],
        out_specs=pl.BlockSpec((1, td, d), lambda i, j: (i, j, 0)),
        compiler_params=_params("parallel", "parallel"),
        name="moe_mix",
    )(y2, y2, x, top_p, mod)


def _pack_w_in(w_in):
    kv = 2 * NSA_KV_HEADS * HEAD_DIM * 3
    w_in = to_bf16(w_in)
    o = 0
    nsa_q = w_in[:, o:o + MIX_WIDTH]; o += MIX_WIDTH
    nsa_kv = w_in[:, o:o + kv]; o += kv
    nsa_gate = w_in[:, o:o + NSA_HEADS * N_BRANCH]; o += NSA_HEADS * N_BRANCH
    sb = w_in[:, o:o + 3 * MIX_WIDTH]; o += 3 * MIX_WIDTH
    ml_qkv = w_in[:, o:o + 3 * MIX_WIDTH]; o += 3 * MIX_WIDTH
    ml_if = w_in[:, o:o + 2 * ML_HEADS]; o += 2 * ML_HEADS
    ml_o = w_in[:, o:o + MIX_WIDTH]; o += MIX_WIDTH
    merge = w_in[:, o:]
    main = jnp.concatenate([merge, nsa_q, ml_qkv, ml_o, sb, nsa_kv], axis=1)
    small = jnp.concatenate([nsa_gate, ml_if], axis=1)
    small = jnp.pad(small, ((0, 0), (0, N_SMALL - small.shape[1])))
    return main, small


def token_mixer_layer(x, mod, norm_g, w_in, nsa_q_norm, nsa_k_norm, cmp_pos, cmp_w1, cmp_b1, cmp_w2,
                      cmp_b2, ml_conv_w, ml_conv_b, ml_gate_b, w_branch, w_out):
    w_main, w_small = _pack_w_in(w_in)
    y3, small3 = in_projection(x, norm_g, mod, w_main, w_small)
    o_sb = sb_attention(y3)
    o_ml = mlstm(y3, small3, ml_conv_w, ml_conv_b, ml_gate_b)
    kc, vc = nsa_compress(y3, cmp_pos, cmp_w1, cmp_b1, cmp_w2, cmp_b2, nsa_k_norm[0])
    o_nsa = nsa_attention(y3, small3, nsa_q_norm, nsa_k_norm, kc, vc)
    return merge_project(o_nsa, o_sb, o_ml, y3, w_branch, w_out, x, mod)


def kernel(x, c, ada_w, ada_b, norm_mix, norm_ffn, w_in, nsa_q_norm, nsa_k_norm, cmp_pos, cmp_w1, cmp_b1,
           cmp_w2, cmp_b2, ml_conv_w, ml_conv_b, ml_gate_b, w_branch, w_out, ffn_wg, ffn_wu, ffn_wd,
           moe_router, moe_wg, moe_wu, moe_wd):
    depth = ada_w.shape[0]
    b, s, d = x.shape
    mods = adaln(c, ada_w, ada_b).reshape(depth, b, 6, d)
    for layer in range(depth):
        mod = mods[layer]
        x = token_mixer_layer(x, mod, norm_mix[layer], w_in[layer], nsa_q_norm[layer], nsa_k_norm[layer],
                              cmp_pos[layer], cmp_w1[layer], cmp_b1[layer], cmp_w2[layer], cmp_b2[layer],
                              ml_conv_w[layer], ml_conv_b[layer], ml_gate_b[layer], w_branch[layer],
                              w_out[layer])
        j = layer // 2
        if layer % 2 == 0:
            x = dense_ffn(x, norm_ffn[layer], mod, ffn_wg[j], ffn_wu[j], ffn_wd[j])
        else:
            x = moe_ffn(x, norm_ffn[layer], mod, moe_router[j], moe_wg[j], moe_wu[j], moe_wd[j])
    return x
```
